```python
import math
import jax, jax.numpy as jnp
from jax import lax
import numpy as np

D_MODEL = 2048
BATCH = 8
SEQ = 8192
DEPTH = 2

CHUNK = 64
Q_BLOCK = 2 * CHUNK

N_A = (DEPTH + 1) // 2
N_B = DEPTH - N_A

D_RNN = 5 * D_MODEL // 4
LRU_BLOCKS = 10
LRU_BW = D_RNN // LRU_BLOCKS
CONV_W = 4
LRU_C = 8.0

N_HEADS = 16
HEAD_DIM = 128
D_ATTN = N_HEADS * HEAD_DIM

EPS = 1e-6

kernel_name = "hawk_stickbreak_yoco_trunk"


def _rmsnorm(x, g):
    xf = x.astype(jnp.float32)
    y = xf * lax.rsqrt(jnp.mean(xf * xf, axis=-1, keepdims=True) + EPS)
    return (y * g.astype(jnp.float32)).astype(x.dtype)


def _causal_depthwise_conv(xb, w, b):
    kernel = w[:, None, :].astype(xb.dtype)
    y = lax.conv_general_dilated(
        xb, kernel, window_strides=(1,), padding=[(CONV_W - 1, 0)],
        dimension_numbers=("NWC", "WIO", "NWC"), feature_group_count=xb.shape[-1])
    return y + b.astype(xb.dtype)


def _block_diag(xb, w, b):
    bsz, s, _ = xb.shape
    xr = xb.reshape(bsz, s, LRU_BLOCKS, LRU_BW)
    y = jnp.einsum("bsnc,ncd->bsnd", xr, w.astype(xb.dtype)).reshape(bsz, s, D_RNN)
    return y + b.astype(xb.dtype)


def _rg_lru(xb, w_r, b_r, w_i, b_i, lam):
    r = jax.nn.sigmoid(_block_diag(xb, w_r, b_r).astype(jnp.float32))
    i = jax.nn.sigmoid(_block_diag(xb, w_i, b_i).astype(jnp.float32))
    log_a = -LRU_C * r * jax.nn.softplus(-lam.astype(jnp.float32))
    a = jnp.exp(log_a)
    mult = jnp.sqrt(jnp.maximum(-jnp.expm1(2.0 * log_a), 0.0))
    u = mult * (i * xb.astype(jnp.float32))

    def combine(left, right):
        a1, b1 = left
        a2, b2 = right
        return a1 * a2, a2 * b1 + b2

    _, h = lax.associative_scan(combine, (a, u), axis=1)
    return h.astype(xb.dtype)


def _stick_breaking_attention(q, k, v):
    bsz, s, h, dh = q.shape
    n_blocks = s // Q_BLOCK
    scale = 1.0 / math.sqrt(dh)
    qb = q.reshape(bsz, n_blocks, Q_BLOCK, h, dh).transpose(1, 0, 2, 3, 4)
    starts = jnp.arange(n_blocks, dtype=jnp.int32) * Q_BLOCK
    key_pos = jnp.arange(s, dtype=jnp.int32)

    def one_block(args):
        start, qblk = args
        z = jnp.einsum("bqhd,bkhd->bhqk", qblk, k).astype(jnp.float32) * scale
        q_pos = start + jnp.arange(Q_BLOCK, dtype=jnp.int32)
        causal = key_pos[None, :] < q_pos[:, None]
        log_keep = jnp.where(causal, jax.nn.log_sigmoid(-z), 0.0)
        later = lax.cumsum(log_keep, axis=3, reverse=True) - log_keep
        weights = jnp.where(causal, jnp.exp(jax.nn.log_sigmoid(z) + later), 0.0)
        return jnp.einsum("bhqk,bkhd->bqhd", weights.astype(v.dtype), v)

    out = lax.map(one_block, (starts, qb))
    return out.transpose(1, 0, 2, 3, 4).reshape(bsz, s, h, dh)


def _fwd_setup_inputs(seed: int = 0) -> dict:
    key = jax.random.key(seed)
    ks = jax.random.split(key, 20)

    def nrm(k, shape, scale):
        return jax.random.normal(k, shape, jnp.float32) * scale

    x = nrm(ks[0], (BATCH, SEQ, D_MODEL), 1.0)
    a_norm = 1.0 + nrm(ks[1], (N_A, D_MODEL), 0.01)
    a_w_in = nrm(ks[2], (N_A, D_MODEL, 2 * D_RNN), D_MODEL ** -0.5)
    a_conv_w = nrm(ks[3], (N_A, CONV_W, D_RNN), CONV_W ** -0.5)
    a_conv_b = nrm(ks[4], (N_A, D_RNN), 0.01)
    a_w_r = nrm(ks[5], (N_A, LRU_BLOCKS, LRU_BW, LRU_BW), LRU_BW ** -0.5)
    a_b_r = nrm(ks[6], (N_A, D_RNN), 0.01)
    a_w_i = nrm(ks[7], (N_A, LRU_BLOCKS, LRU_BW, LRU_BW), LRU_BW ** -0.5)
    a_b_i = nrm(ks[8], (N_A, D_RNN), 0.01)
    u = jax.random.uniform(ks[9], (N_A, D_RNN), jnp.float32, minval=0.9, maxval=0.999)
    a_root = u ** (1.0 / LRU_C)
    a_lambda = jnp.log(a_root) - jnp.log1p(-a_root)
    a_w_out = nrm(ks[10], (N_A, D_RNN, D_MODEL), D_RNN ** -0.5)
    kv_norm = 1.0 + nrm(ks[11], (D_MODEL,), 0.01)
    w_kv = nrm(ks[12], (D_MODEL, 2 * D_ATTN), D_MODEL ** -0.5)
    b_norm = 1.0 + nrm(ks[13], (N_B, D_MODEL), 0.01)
    b_w_in = nrm(ks[14], (N_B, D_MODEL, 2 * D_ATTN), D_MODEL ** -0.5)
    b_w_out = nrm(ks[15], (N_B, D_ATTN, D_MODEL), D_ATTN ** -0.5)
    final_norm = 1.0 + nrm(ks[16], (D_MODEL,), 0.01)
    return {
        "x": x, "a_norm": a_norm, "a_w_in": a_w_in, "a_conv_w": a_conv_w,
        "a_conv_b": a_conv_b, "a_w_r": a_w_r, "a_b_r": a_b_r, "a_w_i": a_w_i,
        "a_b_i": a_b_i, "a_lambda": a_lambda, "a_w_out": a_w_out,
        "kv_norm": kv_norm, "w_kv": w_kv, "b_norm": b_norm, "b_w_in": b_w_in,
        "b_w_out": b_w_out, "final_norm": final_norm,
    }


def _fwd_reference(x, a_norm, a_w_in, a_conv_w, a_conv_b, a_w_r, a_b_r, a_w_i, a_b_i,
              a_lambda, a_w_out, kv_norm, w_kv, b_norm, b_w_in, b_w_out, final_norm):
    bsz, s, _ = x.shape
    k = v = None
    for layer in range(DEPTH):
        if layer < N_A:
            h = _rmsnorm(x, a_norm[layer])
            proj = h @ a_w_in[layer].astype(x.dtype)
            xb, gate = proj[..., :D_RNN], proj[..., D_RNN:]
            xb = _causal_depthwise_conv(xb, a_conv_w[layer], a_conv_b[layer])
            y = _rg_lru(xb, a_w_r[layer], a_b_r[layer], a_w_i[layer], a_b_i[layer],
                        a_lambda[layer])
            x = x + (y * jax.nn.silu(gate)) @ a_w_out[layer].astype(x.dtype)
            if layer == N_A - 1:
                kv = _rmsnorm(x, kv_norm) @ w_kv.astype(x.dtype)
                k = kv[..., :D_ATTN].reshape(bsz, s, N_HEADS, HEAD_DIM)
                v = kv[..., D_ATTN:].reshape(bsz, s, N_HEADS, HEAD_DIM)
        else:
            lb = layer - N_A
            h = _rmsnorm(x, b_norm[lb])
            proj = h @ b_w_in[lb].astype(x.dtype)
            q = proj[..., :D_ATTN].reshape(bsz, s, N_HEADS, HEAD_DIM)
            gate = proj[..., D_ATTN:]
            o = _stick_breaking_attention(q, k, v).reshape(bsz, s, D_ATTN)
            x = x + (o * jax.nn.silu(gate)) @ b_w_out[lb].astype(x.dtype)
    return _rmsnorm(x, final_norm)


import jax as _jax
import jax.numpy as _jnp

TWIN_FORMAT = 'train_step'
FWD_PARAMS = ['x', 'a_norm', 'a_w_in', 'a_conv_w', 'a_conv_b', 'a_w_r', 'a_b_r', 'a_w_i', 'a_b_i', 'a_lambda', 'a_w_out', 'kv_norm', 'w_kv', 'b_norm', 'b_w_in', 'b_w_out', 'final_norm']
TWIN_WEIGHTS = ['a_norm', 'a_w_in', 'a_conv_w', 'a_conv_b', 'a_w_r', 'a_b_r', 'a_w_i', 'a_b_i', 'a_lambda', 'a_w_out', 'kv_norm', 'w_kv', 'b_norm', 'b_w_in', 'b_w_out', 'final_norm']
TWIN_DIFF_INPUT = 'x'
TWIN_INPUTS = ['x', 'a_norm', 'a_w_in', 'a_conv_w', 'a_conv_b', 'a_w_r', 'a_b_r', 'a_w_i', 'a_b_i', 'a_lambda', 'a_w_out', 'kv_norm', 'w_kv', 'b_norm', 'b_w_in', 'b_w_out', 'final_norm', 'loss_target', 'm_a_norm', 'm_a_w_in', 'm_a_conv_w', 'm_a_conv_b', 'm_a_w_r', 'm_a_b_r', 'm_a_w_i', 'm_a_b_i', 'm_a_lambda', 'm_a_w_out', 'm_kv_norm', 'm_w_kv', 'm_b_norm', 'm_b_w_in', 'm_b_w_out', 'm_final_norm', 'v_a_norm', 'v_a_w_in', 'v_a_conv_w', 'v_a_conv_b', 'v_a_w_r', 'v_a_b_r', 'v_a_w_i', 'v_a_b_i', 'v_a_lambda', 'v_a_w_out', 'v_kv_norm', 'v_w_kv', 'v_b_norm', 'v_b_w_in', 'v_b_w_out', 'v_final_norm']
TWIN_OUTPUTS = ['loss', 'grad_x', 'grad_a_norm', 'grad_a_w_in', 'grad_a_conv_w', 'grad_a_conv_b', 'grad_a_w_r', 'grad_a_b_r', 'grad_a_w_i', 'grad_a_b_i', 'grad_a_lambda', 'grad_a_w_out', 'grad_kv_norm', 'grad_w_kv', 'grad_b_norm', 'grad_b_w_in', 'grad_b_w_out', 'grad_final_norm', 'delta_a_norm', 'delta_a_w_in', 'delta_a_conv_w', 'delta_a_conv_b', 'delta_a_w_r', 'delta_a_b_r', 'delta_a_w_i', 'delta_a_b_i', 'delta_a_lambda', 'delta_a_w_out', 'delta_kv_norm', 'delta_w_kv', 'delta_b_norm', 'delta_b_w_in', 'delta_b_w_out', 'delta_final_norm', 'new_m_a_norm', 'new_m_a_w_in', 'new_m_a_conv_w', 'new_m_a_conv_b', 'new_m_a_w_r', 'new_m_a_b_r', 'new_m_a_w_i', 'new_m_a_b_i', 'new_m_a_lambda', 'new_m_a_w_out', 'new_m_kv_norm', 'new_m_w_kv', 'new_m_b_norm', 'new_m_b_w_in', 'new_m_b_w_out', 'new_m_final_norm', 'new_v_a_norm', 'new_v_a_w_in', 'new_v_a_conv_w', 'new_v_a_conv_b', 'new_v_a_w_r', 'new_v_a_b_r', 'new_v_a_w_i', 'new_v_a_b_i', 'new_v_a_lambda', 'new_v_a_w_out', 'new_v_kv_norm', 'new_v_w_kv', 'new_v_b_norm', 'new_v_b_w_in', 'new_v_b_w_out', 'new_v_final_norm']
TWIN_LEAF_KINDS = {'loss': 'loss', 'grad_x': 'grad_x', 'grad_a_norm': 'grad_w', 'grad_a_w_in': 'grad_w', 'grad_a_conv_w': 'grad_w', 'grad_a_conv_b': 'grad_w', 'grad_a_w_r': 'grad_w', 'grad_a_b_r': 'grad_w', 'grad_a_w_i': 'grad_w', 'grad_a_b_i': 'grad_w', 'grad_a_lambda': 'grad_w', 'grad_a_w_out': 'grad_w', 'grad_kv_norm': 'grad_w', 'grad_w_kv': 'grad_w', 'grad_b_norm': 'grad_w', 'grad_b_w_in': 'grad_w', 'grad_b_w_out': 'grad_w', 'grad_final_norm': 'grad_w', 'delta_a_norm': 'delta_w', 'delta_a_w_in': 'delta_w', 'delta_a_conv_w': 'delta_w', 'delta_a_conv_b': 'delta_w', 'delta_a_w_r': 'delta_w', 'delta_a_b_r': 'delta_w', 'delta_a_w_i': 'delta_w', 'delta_a_b_i': 'delta_w', 'delta_a_lambda': 'delta_w', 'delta_a_w_out': 'delta_w', 'delta_kv_norm': 'delta_w', 'delta_w_kv': 'delta_w', 'delta_b_norm': 'delta_w', 'delta_b_w_in': 'delta_w', 'delta_b_w_out': 'delta_w', 'delta_final_norm': 'delta_w', 'new_m_a_norm': 'new_m', 'new_m_a_w_in': 'new_m', 'new_m_a_conv_w': 'new_m', 'new_m_a_conv_b': 'new_m', 'new_m_a_w_r': 'new_m', 'new_m_a_b_r': 'new_m', 'new_m_a_w_i': 'new_m', 'new_m_a_b_i': 'new_m', 'new_m_a_lambda': 'new_m', 'new_m_a_w_out': 'new_m', 'new_m_kv_norm': 'new_m', 'new_m_w_kv': 'new_m', 'new_m_b_norm': 'new_m', 'new_m_b_w_in': 'new_m', 'new_m_b_w_out': 'new_m', 'new_m_final_norm': 'new_m', 'new_v_a_norm': 'new_v', 'new_v_a_w_in': 'new_v', 'new_v_a_conv_w': 'new_v', 'new_v_a_conv_b': 'new_v', 'new_v_a_w_r': 'new_v', 'new_v_a_b_r': 'new_v', 'new_v_a_w_i': 'new_v', 'new_v_a_b_i': 'new_v', 'new_v_a_lambda': 'new_v', 'new_v_a_w_out': 'new_v', 'new_v_kv_norm': 'new_v', 'new_v_w_kv': 'new_v', 'new_v_b_norm': 'new_v', 'new_v_b_w_in': 'new_v', 'new_v_b_w_out': 'new_v', 'new_v_final_norm': 'new_v'}


def _forward(args):
    return _fwd_reference(*[args[k] for k in FWD_PARAMS])


def _output_shape():
    def fwd():
        inp = _fwd_setup_inputs(0)
        return _fwd_reference(*[inp[k] for k in FWD_PARAMS])
    out = _jax.eval_shape(fwd)
    return out.shape, out.dtype

N_MICROBATCH = 1
ADAM_LR = 0.001
ADAM_B1 = 0.9
ADAM_B2 = 0.999
ADAM_EPS = 1e-08
ADAM_WD = 0.01
ADAM_STEP = 10
PER_EXAMPLE_BATCH_AXIS = {'x': 0, 'loss_target': 0}
SHARED_INPUTS = []
_WEIGHT_DTYPES = {'a_norm': _jnp.float32, 'a_w_in': _jnp.float32, 'a_conv_w': _jnp.float32, 'a_conv_b': _jnp.float32, 'a_w_r': _jnp.float32, 'a_b_r': _jnp.float32, 'a_w_i': _jnp.float32, 'a_b_i': _jnp.float32, 'a_lambda': _jnp.float32, 'a_w_out': _jnp.float32, 'kv_norm': _jnp.float32, 'w_kv': _jnp.float32, 'b_norm': _jnp.float32, 'b_w_in': _jnp.float32, 'b_w_out': _jnp.float32, 'final_norm': _jnp.float32}
MOMENT_SCALE = {'a_norm': 7.002370e-02, 'a_w_in': 4.425868e-02, 'a_conv_w': 4.553270e-02, 'a_conv_b': 5.876591e-01, 'a_w_r': 1.266902e-02, 'a_b_r': 1.129968e-02, 'a_w_i': 2.233224e-02, 'a_b_i': 1.605023e-02, 'a_lambda': 2.312346e-02, 'a_w_out': 4.918008e-02, 'kv_norm': 4.808862e-02, 'w_kv': 3.349724e-02, 'b_norm': 4.957209e-02, 'b_w_in': 3.446468e-02, 'b_w_out': 4.315539e-02, 'final_norm': 3.197366e+01}


def _to_microbatches(a, axis):
    t = _jnp.moveaxis(a, axis, 0)
    t = t.reshape((N_MICROBATCH, t.shape[0] // N_MICROBATCH) + t.shape[1:])
    return _jnp.moveaxis(t, 1, axis + 1)


def setup_inputs(seed: int = 0) -> dict:
    inp = _fwd_setup_inputs(seed)
    key = _jax.random.fold_in(_jax.random.key(seed), 7919)
    shape, _ = _output_shape()
    out = dict(inp)
    out["loss_target"] = _jax.random.normal(_jax.random.fold_in(key, 0), shape, _jnp.float32)
    for i, name in enumerate(TWIN_WEIGHTS):
        w = inp[name].astype(_jnp.float32)
        if MOMENT_SCALE is None:
            s = _jnp.sqrt(_jnp.mean(_jnp.square(w)) + 1e-30)
        else:
            s = MOMENT_SCALE[name]
        km, kv = _jax.random.split(_jax.random.fold_in(key, i + 1))
        out[name] = w
        out["m_" + name] = s * _jax.random.normal(km, w.shape, _jnp.float32)
        out["v_" + name] = (s * s) * _jax.random.uniform(kv, w.shape, _jnp.float32, 0.5, 1.5)
    if N_MICROBATCH > 1:
        for name, axis in PER_EXAMPLE_BATCH_AXIS.items():
            out[name] = _to_microbatches(out[name], axis)
    return {'x': out['x'], 'a_norm': out['a_norm'], 'a_w_in': out['a_w_in'], 'a_conv_w': out['a_conv_w'], 'a_conv_b': out['a_conv_b'], 'a_w_r': out['a_w_r'], 'a_b_r': out['a_b_r'], 'a_w_i': out['a_w_i'], 'a_b_i': out['a_b_i'], 'a_lambda': out['a_lambda'], 'a_w_out': out['a_w_out'], 'kv_norm': out['kv_norm'], 'w_kv': out['w_kv'], 'b_norm': out['b_norm'], 'b_w_in': out['b_w_in'], 'b_w_out': out['b_w_out'], 'final_norm': out['final_norm'], 'loss_target': out['loss_target'], 'm_a_norm': out['m_a_norm'], 'm_a_w_in': out['m_a_w_in'], 'm_a_conv_w': out['m_a_conv_w'], 'm_a_conv_b': out['m_a_conv_b'], 'm_a_w_r': out['m_a_w_r'], 'm_a_b_r': out['m_a_b_r'], 'm_a_w_i': out['m_a_w_i'], 'm_a_b_i': out['m_a_b_i'], 'm_a_lambda': out['m_a_lambda'], 'm_a_w_out': out['m_a_w_out'], 'm_kv_norm': out['m_kv_norm'], 'm_w_kv': out['m_w_kv'], 'm_b_norm': out['m_b_norm'], 'm_b_w_in': out['m_b_w_in'], 'm_b_w_out': out['m_b_w_out'], 'm_final_norm': out['m_final_norm'], 'v_a_norm': out['v_a_norm'], 'v_a_w_in': out['v_a_w_in'], 'v_a_conv_w': out['v_a_conv_w'], 'v_a_conv_b': out['v_a_conv_b'], 'v_a_w_r': out['v_a_w_r'], 'v_a_b_r': out['v_a_b_r'], 'v_a_w_i': out['v_a_w_i'], 'v_a_b_i': out['v_a_b_i'], 'v_a_lambda': out['v_a_lambda'], 'v_a_w_out': out['v_a_w_out'], 'v_kv_norm': out['v_kv_norm'], 'v_w_kv': out['v_w_kv'], 'v_b_norm': out['v_b_norm'], 'v_b_w_in': out['v_b_w_in'], 'v_b_w_out': out['v_b_w_out'], 'v_final_norm': out['v_final_norm']}


def _loss(weights, diff, rest, loss_target):
    with _jax.named_scope("forward"):
        args = {**rest, TWIN_DIFF_INPUT: diff, **{k: w.astype(_WEIGHT_DTYPES[k]) for k, w in weights.items()}}
        y = _forward(args)
    with _jax.named_scope("loss_head"):
        err = _jnp.square(y.astype(_jnp.float32) - loss_target)
        return 0.5 * _jnp.sum(_jnp.mean(err, axis=-1)) if err.ndim else 0.5 * err


def _adamw(w, g, m, v):
    m = ADAM_B1 * m + (1.0 - ADAM_B1) * g
    v = ADAM_B2 * v + (1.0 - ADAM_B2) * _jnp.square(g)
    m_hat = m / (1.0 - ADAM_B1 ** ADAM_STEP)
    v_hat = v / (1.0 - ADAM_B2 ** ADAM_STEP)
    delta = -ADAM_LR * (m_hat / (_jnp.sqrt(v_hat) + ADAM_EPS) + ADAM_WD * w)
    return delta, m, v


def reference(x, a_norm, a_w_in, a_conv_w, a_conv_b, a_w_r, a_b_r, a_w_i, a_b_i, a_lambda, a_w_out, kv_norm, w_kv, b_norm, b_w_in, b_w_out, final_norm, loss_target, m_a_norm, m_a_w_in, m_a_conv_w, m_a_conv_b, m_a_w_r, m_a_b_r, m_a_w_i, m_a_b_i, m_a_lambda, m_a_w_out, m_kv_norm, m_w_kv, m_b_norm, m_b_w_in, m_b_w_out, m_final_norm, v_a_norm, v_a_w_in, v_a_conv_w, v_a_conv_b, v_a_w_r, v_a_b_r, v_a_w_i, v_a_b_i, v_a_lambda, v_a_w_out, v_kv_norm, v_w_kv, v_b_norm, v_b_w_in, v_b_w_out, v_final_norm):
    given = dict(x=x, a_norm=a_norm, a_w_in=a_w_in, a_conv_w=a_conv_w, a_conv_b=a_conv_b, a_w_r=a_w_r, a_b_r=a_b_r, a_w_i=a_w_i, a_b_i=a_b_i, a_lambda=a_lambda, a_w_out=a_w_out, kv_norm=kv_norm, w_kv=w_kv, b_norm=b_norm, b_w_in=b_w_in, b_w_out=b_w_out, final_norm=final_norm, loss_target=loss_target, m_a_norm=m_a_norm, m_a_w_in=m_a_w_in, m_a_conv_w=m_a_conv_w, m_a_conv_b=m_a_conv_b, m_a_w_r=m_a_w_r, m_a_b_r=m_a_b_r, m_a_w_i=m_a_w_i, m_a_b_i=m_a_b_i, m_a_lambda=m_a_lambda, m_a_w_out=m_a_w_out, m_kv_norm=m_kv_norm, m_w_kv=m_w_kv, m_b_norm=m_b_norm, m_b_w_in=m_b_w_in, m_b_w_out=m_b_w_out, m_final_norm=m_final_norm, v_a_norm=v_a_norm, v_a_w_in=v_a_w_in, v_a_conv_w=v_a_conv_w, v_a_conv_b=v_a_conv_b, v_a_w_r=v_a_w_r, v_a_b_r=v_a_b_r, v_a_w_i=v_a_w_i, v_a_b_i=v_a_b_i, v_a_lambda=v_a_lambda, v_a_w_out=v_a_w_out, v_kv_norm=v_kv_norm, v_w_kv=v_w_kv, v_b_norm=v_b_norm, v_b_w_in=v_b_w_in, v_b_w_out=v_b_w_out, v_final_norm=v_final_norm)
    weights = {n: given[n] for n in TWIN_WEIGHTS}
    shared = {n: given[n] for n in SHARED_INPUTS}
    per_example = {n: given[n] for n in ['x']}
    grad_fn = _jax.value_and_grad(_loss, argnums=(0, 1))

    def one_microbatch(ex, loss_target):
        ex = dict(ex)
        diff = ex.pop(TWIN_DIFF_INPUT)
        return grad_fn(weights, diff, {**shared, **ex}, loss_target)

    if N_MICROBATCH == 1:
        loss, (grad_w, grad_x) = one_microbatch(per_example, given["loss_target"])
    else:
        def body(carry, xs):
            loss_sum, grad_sum = carry
            l_k, (gw_k, gx_k) = one_microbatch(xs[0], xs[1])
            with _jax.named_scope("update"):
                return (loss_sum + l_k, _jax.tree.map(_jnp.add, grad_sum, gw_k)), gx_k

        init = (_jnp.zeros((), _jnp.float32), _jax.tree.map(_jnp.zeros_like, weights))
        (loss, grad_w), grad_x = _jax.lax.scan(body, init, (per_example, given["loss_target"]))
    with _jax.named_scope("update"):
        delta_w, new_m, new_v = {}, {}, {}
        for n in TWIN_WEIGHTS:
            delta_w[n], new_m[n], new_v[n] = _adamw(weights[n], grad_w[n], given["m_" + n], given["v_" + n])
    return (loss, grad_x, *[grad_w[n] for n in TWIN_WEIGHTS], *[delta_w[n] for n in TWIN_WEIGHTS],
            *[new_m[n] for n in TWIN_WEIGHTS], *[new_v[n] for n in TWIN_WEIGHTS])
```

```python
import math

import jax
import jax.numpy as jnp
from jax import lax
from jax.experimental import pallas as pl
from jax.experimental.pallas import tpu as pltpu

F32 = jnp.float32
BF16 = jnp.bfloat16
MESH = pl.DeviceIdType.MESH

EPS = 1e-6
LRU_C = 8.0
CONV_W = 4
HEAD_DIM = 128
ADAM_LR = 0.001
ADAM_B1 = 0.9
ADAM_B2 = 0.999
ADAM_EPS = 1e-08
ADAM_WD = 0.01
ADAM_STEP = 10

V7X_VMEM_LIMIT = 56 * 1024 * 1024
LANES = 128
SUBLANES = 8
ATT_BLOCK = 256


def _pick(dim, cands):
    for c in cands:
        if dim % c == 0:
            return c
    return dim


def _params(sem, vmem=V7X_VMEM_LIMIT):
    return pltpu.CompilerParams(dimension_semantics=sem, vmem_limit_bytes=vmem)


def _sigmoid(x):
    return 1.0 / (1.0 + jnp.exp(-x))


def matmul(a, b, *, ta=False, tb=False, out_dtype=F32, residual=None, name):
    m = a.shape[1] if ta else a.shape[0]
    kdim = a.shape[0] if ta else a.shape[1]
    n = b.shape[0] if tb else b.shape[1]
    assert (b.shape[1] if tb else b.shape[0]) == kdim
    tm = _pick(m, (1024, 640, 512, 256, 128))
    tn = _pick(n, (1024, 640, 512, 256, 128))
    tk = _pick(kdim, (512, 256, 128))
    nk = kdim // tk
    dn = (((0 if ta else 1,), (1 if tb else 0,)), ((), ()))

    def body(*refs):
        if residual is None:
            a_ref, b_ref, o_ref, acc = refs
            r_ref = None
        else:
            a_ref, b_ref, r_ref, o_ref, acc = refs
        k = pl.program_id(2)

        @pl.when(k == 0)
        def _():
            acc[...] = jnp.zeros_like(acc)

        acc[...] += lax.dot_general(a_ref[...].astype(BF16), b_ref[...].astype(BF16), dn,
                                    preferred_element_type=F32)

        @pl.when(k == nk - 1)
        def _():
            r = acc[...]
            if r_ref is not None:
                r = r + r_ref[...]
            o_ref[...] = r.astype(out_dtype)

    a_spec = (pl.BlockSpec((tk, tm), lambda i, j, k: (k, i)) if ta
              else pl.BlockSpec((tm, tk), lambda i, j, k: (i, k)))
    b_spec = (pl.BlockSpec((tn, tk), lambda i, j, k: (j, k)) if tb
              else pl.BlockSpec((tk, tn), lambda i, j, k: (k, j)))
    o_spec = pl.BlockSpec((tm, tn), lambda i, j, k: (i, j))
    in_specs = [a_spec, b_spec]
    args = [a, b]
    if residual is not None:
        in_specs.append(o_spec)
        args.append(residual)
    return pl.pallas_call(
        body, name=name, grid=(m // tm, n // tn, nk),
        in_specs=in_specs, out_specs=o_spec,
        out_shape=jax.ShapeDtypeStruct((m, n), out_dtype),
        scratch_shapes=[pltpu.VMEM((tm, tn), F32)],
        compiler_params=_params(("parallel", "parallel", "arbitrary")),
    )(*args)


def rms_fwd(x, gains, *, name):
    s, d = x.shape
    tr = _pick(s, (512, 256, 128, 8))
    ng = len(gains)

    def body(*refs):
        x_ref = refs[0]
        g_refs = refs[1:1 + ng]
        o_refs = refs[1 + ng:]
        xv = x_ref[...]
        y = xv * lax.rsqrt(jnp.mean(xv * xv, axis=-1, keepdims=True) + EPS)
        for g_ref, o_ref in zip(g_refs, o_refs):
            o_ref[...] = (y * g_ref[...]).astype(BF16)

    row = pl.BlockSpec((tr, d), lambda i: (i, 0))
    vec = pl.BlockSpec((1, d), lambda i: (0, 0))
    return pl.pallas_call(
        body, name=name, grid=(s // tr,),
        in_specs=[row] + [vec] * ng, out_specs=[row] * ng,
        out_shape=[jax.ShapeDtypeStruct((s, d), BF16)] * ng,
        compiler_params=_params(("parallel",)),
    )(x, *gains)


def rms_bwd(x, dres, norms, *, name):
    s, d = x.shape
    tr = _pick(s, (256, 128, 8))
    ng = len(norms)

    def body(*refs):
        x_ref, dres_ref = refs[0], refs[1]
        g_refs = refs[2:2 + ng]
        dh_refs = refs[2 + ng:2 + 2 * ng]
        dx_ref, dxb_ref = refs[2 + 2 * ng], refs[3 + 2 * ng]
        dg_refs = refs[4 + 2 * ng:]
        i = pl.program_id(0)
        xv = x_ref[...]
        r = lax.rsqrt(jnp.mean(xv * xv, axis=-1, keepdims=True) + EPS)
        xhat = xv * r
        dx = dres_ref[...]
        for g_ref, dh_ref, dg_ref in zip(g_refs, dh_refs, dg_refs):
            dh = dh_ref[...]
            part = jnp.sum(dh * xhat, axis=0, keepdims=True)

            @pl.when(i == 0)
            def _():
                dg_ref[...] = part

            @pl.when(i > 0)
            def _():
                dg_ref[...] += part

            dxhat = dh * g_ref[...]
            dx = dx + r * (dxhat - xhat * jnp.mean(dxhat * xhat, axis=-1, keepdims=True))
        dx_ref[...] = dx
        dxb_ref[...] = dx.astype(BF16)

    row = pl.BlockSpec((tr, d), lambda i: (i, 0))
    vec = pl.BlockSpec((1, d), lambda i: (0, 0))
    outs = pl.pallas_call(
        body, name=name, grid=(s // tr,),
        in_specs=[row, row] + [vec] * ng + [row] * ng,
        out_specs=[row, row] + [vec] * ng,
        out_shape=[jax.ShapeDtypeStruct((s, d), F32), jax.ShapeDtypeStruct((s, d), BF16)]
        + [jax.ShapeDtypeStruct((1, d), F32)] * ng,
        compiler_params=_params(("arbitrary",)),
    )(x, dres, *[g for g, _ in norms], *[dh for _, dh in norms])
    return outs[0], outs[1], list(outs[2:])


def loss_bwd(x2, target, gain, *, name):
    s, d = x2.shape
    tr = _pick(s, (256, 128, 8))
    nsteps = s // tr

    def body(x_ref, t_ref, g_ref, loss_ref, dg_ref, dx_ref, dxb_ref, sq_acc):
        i = pl.program_id(0)
        xv = x_ref[...]
        r = lax.rsqrt(jnp.mean(xv * xv, axis=-1, keepdims=True) + EPS)
        xhat = xv * r
        g = g_ref[...]
        err = xhat * g - t_ref[...]
        dy = err * (1.0 / d)
        sq = jnp.sum(err * err, axis=0, keepdims=True)
        dgp = jnp.sum(dy * xhat, axis=0, keepdims=True)

        @pl.when(i == 0)
        def _():
            sq_acc[...] = sq
            dg_ref[...] = dgp

        @pl.when(i > 0)
        def _():
            sq_acc[...] += sq
            dg_ref[...] += dgp

        dxhat = dy * g
        dx = r * (dxhat - xhat * jnp.mean(dxhat * xhat, axis=-1, keepdims=True))
        dx_ref[...] = dx
        dxb_ref[...] = dx.astype(BF16)

        @pl.when(i == nsteps - 1)
        def _():
            tot = jnp.sum(sq_acc[...], axis=-1, keepdims=True) * (0.5 / d)
            loss_ref[...] = jnp.broadcast_to(tot, (1, LANES))

    row = pl.BlockSpec((tr, d), lambda i: (i, 0))
    vec = pl.BlockSpec((1, d), lambda i: (0, 0))
    return pl.pallas_call(
        body, name=name, grid=(nsteps,),
        in_specs=[row, row, vec],
        out_specs=[pl.BlockSpec((1, LANES), lambda i: (0, 0)), vec, row, row],
        out_shape=[jax.ShapeDtypeStruct((1, LANES), F32), jax.ShapeDtypeStruct((1, d), F32),
                   jax.ShapeDtypeStruct((s, d), F32), jax.ShapeDtypeStruct((s, d), BF16)],
        scratch_shapes=[pltpu.VMEM((1, d), F32)],
        compiler_params=_params(("arbitrary",)),
    )(x2, target, gain)


def _lru_gates(xb, wr, wi, br, bi, sp):
    xbb = xb.astype(BF16)
    r = _sigmoid(jnp.dot(xbb, wr, preferred_element_type=F32) + br)
    ig = _sigmoid(jnp.dot(xbb, wi, preferred_element_type=F32) + bi)
    log_a = (-LRU_C) * r * sp
    a = jnp.exp(log_a)
    mult = jnp.sqrt(jnp.maximum(-jnp.tanh(log_a) * (a * a + 1.0), 0.0))
    return r, ig, a, mult


def _softplus_neg(lam):
    e = jnp.exp(-jnp.abs(lam))
    sp = jnp.maximum(-lam, 0.0) + jnp.log(1.0 + e)
    sg = jnp.where(lam >= 0, e, 1.0) / (1.0 + e)
    return sp, sg


def _conv(pad_ref, w, b, t):
    acc = b + w[CONV_W - 1:CONV_W, :] * pad_ref[pl.ds(SUBLANES, t), :]
    for dlt in range(1, CONV_W):
        acc = acc + w[CONV_W - 1 - dlt:CONV_W - dlt, :] * pad_ref[pl.ds(SUBLANES - dlt, t), :]
    return acc


def _lru_specs(t, bw, nb, time_of):
    blk = lambda c0: pl.BlockSpec((t, bw), lambda n, i, c0=c0: (time_of(i), c0 + n))
    vec = pl.BlockSpec((1, bw), lambda n, i: (0, n))
    wspec = pl.BlockSpec((None, bw, bw), lambda n, i: (n, 0, 0))
    cwspec = pl.BlockSpec((CONV_W, bw), lambda n, i: (0, n))
    return blk, vec, wspec, cwspec


def lru_fwd(proj, conv_w, conv_b, w_r, b_r, w_i, b_i, lam, *, name):
    s, r2 = proj.shape
    rr = r2 // 2
    nb, bw, _ = w_r.shape
    t = _pick(s, (512, 256, 128, 64, 8))
    ngroups = t // SUBLANES

    def body(xp_ref, gate_ref, cw_ref, cb_ref, wr_ref, br_ref, wi_ref, bi_ref, lam_ref,
             m_ref, h_ref, pad, hcarry, a_scr, u_scr):
        i = pl.program_id(1)

        @pl.when(i == 0)
        def _():
            pad[0:SUBLANES, :] = jnp.zeros((SUBLANES, bw), F32)
            hcarry[...] = jnp.zeros_like(hcarry)

        xpre = xp_ref[...]
        pad[pl.ds(SUBLANES, t), :] = xpre
        xb = _conv(pad, cw_ref[...], cb_ref[...], t)
        pad[0:SUBLANES, :] = xpre[t - SUBLANES:, :]
        sp, _ = _softplus_neg(lam_ref[...])
        _, ig, a, mult = _lru_gates(xb, wr_ref[...], wi_ref[...], br_ref[...], bi_ref[...], sp)
        a_scr[...] = a
        u_scr[...] = mult * (ig * xb)
        row = lax.broadcasted_iota(jnp.int32, (SUBLANES, bw), 0)

        def group(gi, hprev):
            off = pl.multiple_of(gi * SUBLANES, SUBLANES)
            av = a_scr[pl.ds(off, SUBLANES), :]
            uv = u_scr[pl.ds(off, SUBLANES), :]
            for dlt in (1, 2, 4):
                keep = row >= dlt
                uv = jnp.where(keep, av * pltpu.roll(uv, dlt, 0) + uv, uv)
                av = jnp.where(keep, av * pltpu.roll(av, dlt, 0), av)
            hv = av * hprev + uv
            h_ref[pl.ds(off, SUBLANES), :] = hv
            return hv[SUBLANES - 1:SUBLANES, :]

        hcarry[...] = lax.fori_loop(0, ngroups, group, hcarry[...])
        gate = gate_ref[...]
        m_ref[...] = (h_ref[...] * (gate * _sigmoid(gate))).astype(BF16)

    blk, vec, wspec, cwspec = _lru_specs(t, bw, nb, lambda i: i)
    return pl.pallas_call(
        body, name=name, grid=(nb, s // t),
        in_specs=[blk(0), blk(nb), cwspec, vec, wspec, vec, wspec, vec, vec],
        out_specs=[blk(0), blk(0)],
        out_shape=[jax.ShapeDtypeStruct((s, rr), BF16), jax.ShapeDtypeStruct((s, rr), F32)],
        scratch_shapes=[pltpu.VMEM((t + SUBLANES, bw), F32), pltpu.VMEM((1, bw), F32),
                        pltpu.VMEM((t, bw), F32), pltpu.VMEM((t, bw), F32)],
        compiler_params=_params(("parallel", "arbitrary")),
    )(proj, proj, conv_w, conv_b, w_r, b_r, w_i, b_i, lam)


def lru_bwd(proj, hst, dm, conv_w, conv_b, w_r, b_r, w_i, b_i, lam, *, name):
    s, r2 = proj.shape
    rr = r2 // 2
    nb, bw, _ = w_r.shape
    t = _pick(s, (512, 256, 128, 64, 8))
    nt = s // t
    ngroups = t // SUBLANES
    nt_dims = (((1,), (1,)), ((), ()))
    tn_dims = (((0,), (0,)), ((), ()))

    def body(xp_ref, xhalo_ref, gate_ref, h_ref, hhalo_ref, dm_ref, cw_ref, cb_ref, wr_ref, br_ref, wi_ref,
             bi_ref, lam_ref,
             dxp_ref, dgate_ref, dcw_ref, dcb_ref, dwr_ref, dbr_ref, dwi_ref, dbi_ref, dlam_ref,
             pad, hpad, dpad, ecarry, a_scr, b_scr, d_scr):
        step = pl.program_id(1)

        @pl.when(step == 0)
        def _():
            dpad[pl.ds(t, SUBLANES), :] = jnp.zeros((SUBLANES, bw), F32)
            ecarry[...] = jnp.zeros_like(ecarry)
            dcw_ref[...] = jnp.zeros_like(dcw_ref)
            dcb_ref[...] = jnp.zeros_like(dcb_ref)
            dwr_ref[...] = jnp.zeros_like(dwr_ref)
            dbr_ref[...] = jnp.zeros_like(dbr_ref)
            dwi_ref[...] = jnp.zeros_like(dwi_ref)
            dbi_ref[...] = jnp.zeros_like(dbi_ref)
            dlam_ref[...] = jnp.zeros_like(dlam_ref)

        past = jnp.where(step == nt - 1, 0.0, 1.0)
        pad[0:SUBLANES, :] = xhalo_ref[...] * past
        pad[pl.ds(SUBLANES, t), :] = xp_ref[...]
        hpad[0:SUBLANES, :] = hhalo_ref[...] * past
        hpad[pl.ds(SUBLANES, t), :] = h_ref[...]
        cw = cw_ref[...]
        xb = _conv(pad, cw, cb_ref[...], t)
        sp, sg = _softplus_neg(lam_ref[...])
        wr = wr_ref[...]
        wi = wi_ref[...]
        r, ig, a, mult = _lru_gates(xb, wr, wi, br_ref[...], bi_ref[...], sp)
        gate = gate_ref[...]
        sgate = _sigmoid(gate)
        dmv = dm_ref[...]
        dgate_ref[...] = (dmv * h_ref[...] * (sgate * (1.0 + gate * (1.0 - sgate)))).astype(BF16)
        dy = dmv * (gate * sgate)
        a_scr[...] = a
        b_scr[...] = a * dy
        row = lax.broadcasted_iota(jnp.int32, (SUBLANES, bw), 0)

        def group(gi, enext):
            off = pl.multiple_of((ngroups - 1 - gi) * SUBLANES, SUBLANES)
            av = a_scr[pl.ds(off, SUBLANES), :]
            bv = b_scr[pl.ds(off, SUBLANES), :]
            for dlt in (1, 2, 4):
                keep = row < SUBLANES - dlt
                bv = jnp.where(keep, av * pltpu.roll(bv, SUBLANES - dlt, 0) + bv, bv)
                av = jnp.where(keep, av * pltpu.roll(av, SUBLANES - dlt, 0), av)
            ev = av * enext + bv
            d_scr[pl.ds(off, SUBLANES), :] = jnp.where(row == SUBLANES - 1, enext,
                                                       pltpu.roll(ev, SUBLANES - 1, 0))
            return ev[0:1, :]

        ecarry[...] = lax.fori_loop(0, ngroups, group, ecarry[...])
        dtot = dy + d_scr[...]
        da = dtot * hpad[pl.ds(SUBLANES - 1, t), :]
        dmult = dtot * (ig * xb)
        dlog_a = da * a - dmult * (a * a) / mult
        dr_pre = dlog_a * ((-LRU_C) * sp) * (r * (1.0 - r))
        di_pre = (dtot * mult * xb) * (ig * (1.0 - ig))
        dlam_ref[...] += jnp.sum(dlog_a * r, axis=0, keepdims=True) * (LRU_C * sg)
        dbr_ref[...] += jnp.sum(dr_pre, axis=0, keepdims=True)
        dbi_ref[...] += jnp.sum(di_pre, axis=0, keepdims=True)
        drb = dr_pre.astype(BF16)
        dib = di_pre.astype(BF16)
        xbb = xb.astype(BF16)
        dxb = (dtot * mult * ig
               + lax.dot_general(drb, wr, nt_dims, preferred_element_type=F32)
               + lax.dot_general(dib, wi, nt_dims, preferred_element_type=F32))
        dwr_ref[...] += lax.dot_general(xbb, drb, tn_dims, preferred_element_type=F32)
        dwi_ref[...] += lax.dot_general(xbb, dib, tn_dims, preferred_element_type=F32)
        dcb_ref[...] += jnp.sum(dxb, axis=0, keepdims=True)
        dpad[pl.ds(0, t), :] = dxb
        dxpre = cw[CONV_W - 1:CONV_W, :] * dxb
        dcw_ref[CONV_W - 1:CONV_W, :] += jnp.sum(dxb * pad[pl.ds(SUBLANES, t), :], axis=0, keepdims=True)
        for dlt in range(1, CONV_W):
            dxpre = dxpre + cw[CONV_W - 1 - dlt:CONV_W - dlt, :] * dpad[pl.ds(dlt, t), :]
            dcw_ref[CONV_W - 1 - dlt:CONV_W - dlt, :] += jnp.sum(
                dxb * pad[pl.ds(SUBLANES - dlt, t), :], axis=0, keepdims=True)
        dpad[pl.ds(t, SUBLANES), :] = dxb[0:SUBLANES, :]
        dxp_ref[...] = dxpre.astype(BF16)

    rev = lambda i: nt - 1 - i
    blk, vec, wspec, cwspec = _lru_specs(t, bw, nb, rev)
    halo = pl.BlockSpec((SUBLANES, bw), lambda n, i: (jnp.maximum(rev(i) * ngroups - 1, 0), n))
    return pl.pallas_call(
        body, name=name, grid=(nb, nt),
        in_specs=[blk(0), halo, blk(nb), blk(0), halo, blk(0), cwspec, vec, wspec, vec, wspec, vec, vec],
        out_specs=[blk(0), blk(0), cwspec, vec, wspec, vec, wspec, vec, vec],
        out_shape=[jax.ShapeDtypeStruct((s, rr), BF16), jax.ShapeDtypeStruct((s, rr), BF16),
                   jax.ShapeDtypeStruct((CONV_W, rr), F32), jax.ShapeDtypeStruct((1, rr), F32),
                   jax.ShapeDtypeStruct((nb, bw, bw), F32), jax.ShapeDtypeStruct((1, rr), F32),
                   jax.ShapeDtypeStruct((nb, bw, bw), F32), jax.ShapeDtypeStruct((1, rr), F32),
                   jax.ShapeDtypeStruct((1, rr), F32)],
        scratch_shapes=[pltpu.VMEM((t + SUBLANES, bw), F32), pltpu.VMEM((t + SUBLANES, bw), F32),
                        pltpu.VMEM((t + SUBLANES, bw), F32), pltpu.VMEM((1, bw), F32),
                        pltpu.VMEM((t, bw), F32), pltpu.VMEM((t, bw), F32), pltpu.VMEM((t, bw), F32)],
        compiler_params=_params(("parallel", "arbitrary")),
    )(proj, proj, proj, hst, hst, dm, conv_w, conv_b, w_r, b_r, w_i, b_i, lam)


def _split_bf16(x):
    hi = x.astype(BF16)
    lo = (x - hi.astype(F32)).astype(BF16)
    return hi, lo


def _log_terms(z):
    l = jnp.log(1.0 + jnp.exp(-jnp.abs(z)))
    return -jnp.maximum(z, 0.0) - l, jnp.minimum(z, 0.0) - l


def attn_fwd(projb, kv, *, name):
    s, a2 = projb.shape
    a = a2 // 2
    nh = a // HEAD_DIM
    blk = ATT_BLOCK if s % ATT_BLOCK == 0 else s
    nq = s // blk
    assert nq <= LANES
    scale = 1.0 / math.sqrt(HEAD_DIM)
    nt_dims = (((1,), (1,)), ((), ()))

    def body(q_ref, g_ref, k_ref, v_ref, m_ref, o_ref, off_ref, acc):
        i = pl.program_id(1)
        qb = q_ref[...].astype(BF16)
        rowi = lax.broadcasted_iota(jnp.int32, (blk, blk), 0)
        coli = lax.broadcasted_iota(jnp.int32, (blk, blk), 1)
        later_mat = (rowi > coli).astype(BF16)
        causal = coli < rowi
        lane = lax.broadcasted_iota(jnp.int32, (blk, LANES), 1)

        def tile(j, carry, masked):
            off = pl.multiple_of(j * blk, blk)
            kj = k_ref[pl.ds(off, blk), :]
            vj = v_ref[pl.ds(off, blk), :]
            z = lax.dot_general(qb, kj, nt_dims, preferred_element_type=F32) * scale
            lk, ls = _log_terms(z)
            if masked:
                lk = jnp.where(causal, lk, 0.0)
            hi, lo = _split_bf16(lk)
            later = (jnp.dot(hi, later_mat, preferred_element_type=F32)
                     + jnp.dot(lo, later_mat, preferred_element_type=F32))
            w = jnp.exp(ls + later + carry)
            if masked:
                w = jnp.where(causal, w, 0.0)
            acc[...] += jnp.dot(w.astype(BF16), vj, preferred_element_type=F32)
            off_ref[...] = jnp.where(lane == j, carry, off_ref[...])
            return carry + jnp.sum(lk, axis=1, keepdims=True)

        acc[...] = jnp.zeros_like(acc)
        off_ref[...] = jnp.zeros_like(off_ref)
        carry = tile(i, jnp.zeros((blk, 1), F32), True)
        lax.fori_loop(0, i, lambda jj, c: tile(i - 1 - jj, c, False), carry)
        o = acc[...]
        o_ref[...] = o
        gate = g_ref[...]
        m_ref[...] = (o * (gate * _sigmoid(gate))).astype(BF16)

    qspec = lambda c0: pl.BlockSpec((blk, HEAD_DIM), lambda h, i, c0=c0: (i, c0 + h))
    kspec = lambda c0: pl.BlockSpec((s, HEAD_DIM), lambda h, i, c0=c0: (0, c0 + h))
    return pl.pallas_call(
        body, name=name, grid=(nh, nq),
        in_specs=[qspec(0), qspec(nh), kspec(0), kspec(nh)],
        out_specs=[qspec(0), qspec(0), pl.BlockSpec((None, None, blk, LANES), lambda h, i: (h, i, 0, 0))],
        out_shape=[jax.ShapeDtypeStruct((s, a), BF16), jax.ShapeDtypeStruct((s, a), F32),
                   jax.ShapeDtypeStruct((nh, nq, blk, LANES), F32)],
        scratch_shapes=[pltpu.VMEM((blk, HEAD_DIM), F32)],
        compiler_params=_params(("parallel", "arbitrary")),
    )(projb, projb, kv, kv)


def attn_bwd(projb, do, kv, offs, *, name):
    s, a2 = projb.shape
    a = a2 // 2
    nh = a // HEAD_DIM
    blk = ATT_BLOCK if s % ATT_BLOCK == 0 else s
    nq = s // blk
    scale = 1.0 / math.sqrt(HEAD_DIM)
    nt_dims = (((1,), (1,)), ((), ()))

    def body(q_ref, do_ref, k_ref, v_ref, off_ref, dq_ref, dk_ref, dv_ref, kt, dk_acc, dv_acc, dqt_acc):
        i = pl.program_id(1)

        @pl.when(i == 0)
        def _():
            dk_acc[...] = jnp.zeros_like(dk_acc)
            dv_acc[...] = jnp.zeros_like(dv_acc)
            for j in range(nq):
                kt[j] = k_ref[j * blk:(j + 1) * blk, :].astype(F32).T.astype(BF16)

        qb = q_ref[...].astype(BF16)
        dob = do_ref[...]
        rowi = lax.broadcasted_iota(jnp.int32, (blk, blk), 0)
        coli = lax.broadcasted_iota(jnp.int32, (blk, blk), 1)
        later_mat = (coli > rowi).astype(BF16)
        before_mat = (coli < rowi).astype(BF16)
        causal = rowi < coli
        offs_t = off_ref[...].T
        sub = lax.broadcasted_iota(jnp.int32, (LANES, blk), 0)
        dqt_acc[...] = jnp.zeros_like(dqt_acc)

        def tile(j, gcarry, masked):
            off = pl.multiple_of(j * blk, blk)
            kj = k_ref[pl.ds(off, blk), :]
            vj = v_ref[pl.ds(off, blk), :]
            z = lax.dot_general(kj, qb, nt_dims, preferred_element_type=F32) * scale
            lk, ls = _log_terms(z)
            if masked:
                lk = jnp.where(causal, lk, 0.0)
            hi, lo = _split_bf16(lk)
            later = (jnp.dot(later_mat, hi, preferred_element_type=F32)
                     + jnp.dot(later_mat, lo, preferred_element_type=F32))
            offj = jnp.sum(jnp.where(sub == j, offs_t, 0.0), axis=0, keepdims=True)
            w = jnp.exp(ls + later + offj)
            if masked:
                w = jnp.where(causal, w, 0.0)
            dw = lax.dot_general(vj, dob, nt_dims, preferred_element_type=F32)
            g = w * dw
            dv_acc[pl.ds(off, blk), :] += jnp.dot(w.astype(BF16), dob, preferred_element_type=F32)
            ghi, glo = _split_bf16(g)
            gbefore = (jnp.dot(before_mat, ghi, preferred_element_type=F32)
                       + jnp.dot(before_mat, glo, preferred_element_type=F32))
            sig = jnp.exp(ls)
            dz = g * (1.0 - sig) - (gbefore + gcarry) * sig
            if masked:
                dz = jnp.where(causal, dz, 0.0)
            dzb = (dz * scale).astype(BF16)
            dk_acc[pl.ds(off, blk), :] += jnp.dot(dzb, qb, preferred_element_type=F32)
            dqt_acc[...] += jnp.dot(kt[j], dzb, preferred_element_type=F32)
            return gcarry + jnp.sum(g, axis=0, keepdims=True)

        gcarry = lax.fori_loop(0, i, lambda j, c: tile(j, c, False), jnp.zeros((1, blk), F32))
        tile(i, gcarry, True)
        dq_ref[...] = dqt_acc[...].T.astype(BF16)

        @pl.when(i == nq - 1)
        def _():
            dk_ref[...] = dk_acc[...].astype(BF16)
            dv_ref[...] = dv_acc[...].astype(BF16)

    qspec = lambda c0: pl.BlockSpec((blk, HEAD_DIM), lambda h, i, c0=c0: (i, c0 + h))
    kspec = lambda c0: pl.BlockSpec((s, HEAD_DIM), lambda h, i, c0=c0: (0, c0 + h))
    return pl.pallas_call(
        body, name=name, grid=(nh, nq),
        in_specs=[qspec(0), qspec(0), kspec(0), kspec(nh),
                  pl.BlockSpec((None, None, blk, LANES), lambda h, i: (h, i, 0, 0))],
        out_specs=[qspec(0), kspec(0), kspec(0)],
        out_shape=[jax.ShapeDtypeStruct((s, a), BF16)] * 3,
        scratch_shapes=[pltpu.VMEM((nq, HEAD_DIM, blk), BF16), pltpu.VMEM((s, HEAD_DIM), F32),
                        pltpu.VMEM((s, HEAD_DIM), F32), pltpu.VMEM((HEAD_DIM, blk), F32)],
        compiler_params=_params(("parallel", "arbitrary")),
    )(projb, do, kv, kv, offs)


def gate_bwd(dm, o, projb, *, name):
    s, a = dm.shape
    tr = _pick(s, (512, 256, 128, 8))

    def body(dm_ref, o_ref, g_ref, do_ref, dg_ref):
        gate = g_ref[...]
        sg = _sigmoid(gate)
        dmv = dm_ref[...]
        do_ref[...] = (dmv * (gate * sg)).astype(BF16)
        dg_ref[...] = (dmv * o_ref[...] * (sg * (1.0 + gate * (1.0 - sg)))).astype(BF16)

    row = pl.BlockSpec((tr, a), lambda i: (i, 0))
    return pl.pallas_call(
        body, name=name, grid=(s // tr,),
        in_specs=[row, row, pl.BlockSpec((tr, a), lambda i: (i, 1))],
        out_specs=[row, row],
        out_shape=[jax.ShapeDtypeStruct((s, a), BF16)] * 2,
        compiler_params=_params(("parallel",)),
    )(dm, o, projb)


def _as2d(x):
    n = x.size
    cols = x.shape[-1]
    if cols % LANES != 0:
        cols = LANES
    return x.reshape(n // cols, cols)


def sum_parts(parts, *, name):
    p, rows, cols = parts.shape
    tr = _pick(rows, (512, 256, 128, 64, 32, 16))

    def body(p_ref, o_ref):
        acc = p_ref[0].astype(F32)
        for k in range(1, p):
            acc = acc + p_ref[k].astype(F32)
        o_ref[...] = acc

    return pl.pallas_call(
        body, name=name, grid=(rows // tr,),
        in_specs=[pl.BlockSpec((p, tr, cols), lambda i: (0, i, 0))],
        out_specs=pl.BlockSpec((tr, cols), lambda i: (i, 0)),
        out_shape=jax.ShapeDtypeStruct((rows, cols), F32),
        compiler_params=_params(("parallel",)),
    )(parts)


def adamw(w, g_parts, m, v, *, name):
    rows, cols = w.shape
    tr = _pick(rows, (128, 64, 32, 16, 8))
    np_ = len(g_parts)
    c1 = 1.0 / (1.0 - ADAM_B1 ** ADAM_STEP)
    c2 = 1.0 / (1.0 - ADAM_B2 ** ADAM_STEP)

    def body(*refs):
        w_ref, m_ref, v_ref = refs[0], refs[1], refs[2]
        g_refs = refs[3:3 + np_]
        go_ref, d_ref, mo_ref, vo_ref = refs[3 + np_:]
        g = g_refs[0][...]
        for gr in g_refs[1:]:
            g = g + gr[...]
        mn = ADAM_B1 * m_ref[...] + (1.0 - ADAM_B1) * g
        vn = ADAM_B2 * v_ref[...] + (1.0 - ADAM_B2) * (g * g)
        go_ref[...] = g
        mo_ref[...] = mn
        vo_ref[...] = vn
        d_ref[...] = (-ADAM_LR) * ((mn * c1) / (jnp.sqrt(vn * c2) + ADAM_EPS) + ADAM_WD * w_ref[...])

    spec = pl.BlockSpec((tr, cols), lambda i: (i, 0))
    return pl.pallas_call(
        body, name=name, grid=(rows // tr,),
        in_specs=[spec] * (3 + np_), out_specs=[spec] * 4,
        out_shape=[jax.ShapeDtypeStruct((rows, cols), F32)] * 4,
        compiler_params=_params(("parallel",)),
    )(w, m, v, *g_parts)


def _place():
    return lax.axis_index("x"), lax.axis_index("y"), lax.axis_index("c")


def _chip_peers(x, y, c):
    return [(1 - x, y, c), (x, 1 - y, c), (1 - x, 1 - y, c)]


def _shard_of(ref, axis, idx, n):
    start = pl.multiple_of(idx * n, n)
    sl = [slice(None)] * len(ref.shape)
    sl[axis] = pl.ds(start, n)
    return ref.at[tuple(sl)]


def gather_weights(shards, axes, *, name):
    na = len(shards)
    hbm = pl.BlockSpec(memory_space=pl.ANY)
    full_shapes = []
    for sh, ax in zip(shards, axes):
        shp = list(sh.shape)
        shp[ax] *= 4
        full_shapes.append(jax.ShapeDtypeStruct(tuple(shp), sh.dtype))

    def body(*refs):
        s_refs = refs[:na]
        f_refs = refs[na:2 * na]
        send_sems, recv_sems, local_sems = refs[2 * na:]
        x, y, c = _place()
        me = 2 * x + y
        peers = _chip_peers(x, y, c)
        copies = []
        for ai in range(na):
            n = s_refs[ai].shape[axes[ai]]
            mine = _shard_of(f_refs[ai], axes[ai], me, n)
            loc = pltpu.make_async_copy(s_refs[ai], mine, local_sems.at[ai])
            loc.start()
            copies.append((loc,))
            for k, peer in enumerate(peers):
                theirs = _shard_of(f_refs[ai], axes[ai], 2 * peer[0] + peer[1], n)
                snd = pltpu.make_async_remote_copy(
                    src_ref=s_refs[ai], dst_ref=mine, send_sem=send_sems.at[ai * 3 + k],
                    recv_sem=recv_sems.at[ai * 3 + k], device_id=peer, device_id_type=MESH)
                snd.start()
                rcv = pltpu.make_async_remote_copy(
                    src_ref=s_refs[ai], dst_ref=theirs, send_sem=send_sems.at[ai * 3 + k],
                    recv_sem=recv_sems.at[ai * 3 + k], device_id=peer, device_id_type=MESH)
                copies.append((snd, rcv))
        for cp in copies:
            if len(cp) == 1:
                cp[0].wait()
            else:
                cp[0].wait_send()
                cp[1].wait_recv()

    return pl.pallas_call(
        body, name=name,
        in_specs=[hbm] * na, out_specs=[hbm] * na, out_shape=full_shapes,
        scratch_shapes=[pltpu.SemaphoreType.DMA((3 * na,)), pltpu.SemaphoreType.DMA((3 * na,)),
                        pltpu.SemaphoreType.DMA((na,))],
    )(*shards)


def scatter_grads(grads, axes, *, name):
    na = len(grads)
    hbm = pl.BlockSpec(memory_space=pl.ANY)
    out_shapes = []
    for g, ax in zip(grads, axes):
        shp = list(g.shape)
        shp[ax] //= 4
        out_shapes.append(jax.ShapeDtypeStruct((4, *shp), g.dtype))

    def body(*refs):
        g_refs = refs[:na]
        p_refs = refs[na:2 * na]
        send_sems, recv_sems, local_sems = refs[2 * na:]
        x, y, c = _place()
        me = 2 * x + y
        peers = _chip_peers(x, y, c)
        copies = []
        for ai in range(na):
            n = p_refs[ai].shape[1 + axes[ai]]
            loc = pltpu.make_async_copy(_shard_of(g_refs[ai], axes[ai], me, n), p_refs[ai].at[0], local_sems.at[ai])
            loc.start()
            copies.append(loc)
            for k, peer in enumerate(peers):
                cp = pltpu.make_async_remote_copy(
                    src_ref=_shard_of(g_refs[ai], axes[ai], 2 * peer[0] + peer[1], n), dst_ref=p_refs[ai].at[1 + k],
                    send_sem=send_sems.at[ai * 3 + k], recv_sem=recv_sems.at[ai * 3 + k],
                    device_id=peer, device_id_type=MESH)
                cp.start()
                copies.append(cp)
        for cp in copies:
            cp.wait()

    return pl.pallas_call(
        body, name=name,
        in_specs=[hbm] * na, out_specs=[hbm] * na, out_shape=out_shapes,
        scratch_shapes=[pltpu.SemaphoreType.DMA((3 * na,)), pltpu.SemaphoreType.DMA((3 * na,)),
                        pltpu.SemaphoreType.DMA((na,))],
    )(*grads)


def swap_cores(arrs, *, name):
    na = len(arrs)
    hbm = pl.BlockSpec(memory_space=pl.ANY)

    def body(*refs):
        a_refs = refs[:na]
        o_refs = refs[na:2 * na]
        send_sems, recv_sems = refs[2 * na:]
        x, y, c = _place()
        copies = []
        for ai in range(na):
            cp = pltpu.make_async_remote_copy(
                src_ref=a_refs[ai], dst_ref=o_refs[ai], send_sem=send_sems.at[ai], recv_sem=recv_sems.at[ai],
                device_id=(x, y, 1 - c), device_id_type=MESH)
            cp.start()
            copies.append(cp)
        for cp in copies:
            cp.wait()

    return pl.pallas_call(
        body, name=name,
        in_specs=[hbm] * na, out_specs=[hbm] * na,
        out_shape=[jax.ShapeDtypeStruct(a.shape, a.dtype) for a in arrs],
        scratch_shapes=[pltpu.SemaphoreType.DMA((na,)), pltpu.SemaphoreType.DMA((na,))],
    )(*arrs)


def allreduce_small(buf, *, name):
    rows, cols = buf.shape

    def body(b_ref, o_ref, slots, send_sems, recv_sems):
        x, y, c = _place()
        me = 4 * x + 2 * y + c
        slots[0] = b_ref[...]
        copies = []
        for rel in range(1, 8):
            peer = (x ^ (rel >> 2), y ^ ((rel >> 1) & 1), c ^ (rel & 1))
            cp = pltpu.make_async_remote_copy(
                src_ref=b_ref, dst_ref=slots.at[rel], send_sem=send_sems.at[rel - 1],
                recv_sem=recv_sems.at[rel - 1], device_id=peer, device_id_type=MESH)
            cp.start()
            copies.append(cp)
        for cp in copies:
            cp.wait()
        acc = slots[me]
        for dev in range(1, 8):
            acc = acc + slots[dev ^ me]
        o_ref[...] = acc

    vm = pl.BlockSpec(memory_space=pltpu.VMEM)
    return pl.pallas_call(
        body, name=name, in_specs=[vm], out_specs=vm,
        out_shape=jax.ShapeDtypeStruct((rows, cols), F32),
        scratch_shapes=[pltpu.VMEM((8, rows, cols), F32), pltpu.SemaphoreType.DMA((7,)),
                        pltpu.SemaphoreType.DMA((7,))],
    )(buf)


def _pack_rows(arrs):
    parts = []
    for a in arrs:
        p = a.reshape(-1, LANES)
        parts.append(jnp.pad(p, ((0, (-p.shape[0]) % SUBLANES), (0, 0))))
    return jnp.concatenate(parts, axis=0)


def _unpack_rows(buf, shapes):
    out, r0 = [], 0
    for shp in shapes:
        n = math.prod(shp) // LANES
        out.append(buf[r0:r0 + n].reshape(shp))
        r0 += n + (-n) % SUBLANES
    return out


def kernel(x, a_norm, a_w_in, a_conv_w, a_conv_b, a_w_r, a_b_r, a_w_i, a_b_i, a_lambda, a_w_out, kv_norm, w_kv, b_norm, b_w_in, b_w_out, final_norm, loss_target, m_a_norm, m_a_w_in, m_a_conv_w, m_a_conv_b, m_a_w_r, m_a_b_r, m_a_w_i, m_a_b_i, m_a_lambda, m_a_w_out, m_kv_norm, m_w_kv, m_b_norm, m_b_w_in, m_b_w_out, m_final_norm, v_a_norm, v_a_w_in, v_a_conv_w, v_a_conv_b, v_a_w_r, v_a_b_r, v_a_w_i, v_a_b_i, v_a_lambda, v_a_w_out, v_kv_norm, v_w_kv, v_b_norm, v_b_w_in, v_b_w_out, v_final_norm):
    weights = dict(a_norm=a_norm, a_w_in=a_w_in, a_conv_w=a_conv_w, a_conv_b=a_conv_b, a_w_r=a_w_r, a_b_r=a_b_r,
                   a_w_i=a_w_i, a_b_i=a_b_i, a_lambda=a_lambda, a_w_out=a_w_out, kv_norm=kv_norm, w_kv=w_kv,
                   b_norm=b_norm, b_w_in=b_w_in, b_w_out=b_w_out, final_norm=final_norm)
    mom1 = dict(a_norm=m_a_norm, a_w_in=m_a_w_in, a_conv_w=m_a_conv_w, a_conv_b=m_a_conv_b, a_w_r=m_a_w_r,
                a_b_r=m_a_b_r, a_w_i=m_a_w_i, a_b_i=m_a_b_i, a_lambda=m_a_lambda, a_w_out=m_a_w_out,
                kv_norm=m_kv_norm, w_kv=m_w_kv, b_norm=m_b_norm, b_w_in=m_b_w_in, b_w_out=m_b_w_out,
                final_norm=m_final_norm)
    mom2 = dict(a_norm=v_a_norm, a_w_in=v_a_w_in, a_conv_w=v_a_conv_w, a_conv_b=v_a_conv_b, a_w_r=v_a_w_r,
                a_b_r=v_a_b_r, a_w_i=v_a_w_i, a_b_i=v_a_b_i, a_lambda=v_a_lambda, a_w_out=v_a_w_out,
                kv_norm=v_kv_norm, w_kv=v_w_kv, b_norm=v_b_norm, b_w_in=v_b_w_in, b_w_out=v_b_w_out,
                final_norm=v_final_norm)
    order = list(weights)
    x0 = x[0]
    target = loss_target[0]
    d = x0.shape[1]
    chip = 2 * lax.axis_index("x") + lax.axis_index("y")

    big = ["a_w_in", "a_w_r", "a_w_i", "a_w_out", "w_kv", "b_w_in", "b_w_out"]
    big_axis = dict(a_w_in=1, a_w_r=1, a_w_i=1, a_w_out=0, w_kv=1, b_w_in=1, b_w_out=0)
    local = dict(a_w_in=a_w_in[0], a_w_r=a_w_r[0], a_w_i=a_w_i[0], a_w_out=a_w_out[0], w_kv=w_kv,
                 b_w_in=b_w_in[0], b_w_out=b_w_out[0])
    shards = [local[n].astype(BF16) for n in big] + [a_conv_w[0], b_norm]
    full = gather_weights(shards, [big_axis[n] for n in big] + [1, 1], name="gather_weights")
    wf = dict(zip(big + ["a_conv_w", "b_norm"], full))
    wf.update(a_norm=a_norm, a_conv_b=a_conv_b, a_b_r=a_b_r, a_b_i=a_b_i, a_lambda=a_lambda,
              kv_norm=kv_norm.reshape(1, d), final_norm=final_norm.reshape(1, d))
    loss_part, grad_x, gbig, gsmall = _local_grads(x0, target, wf)

    parts = scatter_grads([gbig[n] for n in big], [big_axis[n] for n in big], name="scatter_grads")
    sums = [sum_parts(p.reshape(4, *_as2d(p[0]).shape), name="sum_" + n) for n, p in zip(big, parts)]
    others = swap_cores(sums, name="swap_cores")

    small = ["a_norm", "a_conv_b", "a_b_r", "a_b_i", "a_lambda", "kv_norm", "final_norm", "a_conv_w", "b_norm"]
    buf = _pack_rows([gsmall[n] for n in small] + [loss_part])
    red = allreduce_small(buf, name="allreduce_small")
    red_list = _unpack_rows(red, [gsmall[n].shape for n in small] + [(1, LANES)])
    gs = dict(zip(small, red_list[:-1]))
    loss = red_list[-1][0, 0]
    n_conv = a_conv_w.shape[2]
    gs["a_conv_w"] = lax.dynamic_slice_in_dim(gs["a_conv_w"], chip * n_conv, n_conv, axis=1)
    n_bn = b_norm.shape[1]
    gs["b_norm"] = lax.dynamic_slice_in_dim(gs["b_norm"], chip * n_bn, n_bn, axis=1)

    grads, deltas, new_m, new_v = {}, {}, {}, {}
    for n, s_mine, s_other in zip(big, sums, others):
        shp = weights[n].shape
        g, dlt, mn, vn = adamw(_as2d(weights[n]), [s_mine, s_other], _as2d(mom1[n]), _as2d(mom2[n]),
                               name="adamw_" + n)
        grads[n], deltas[n], new_m[n], new_v[n] = (t.reshape(shp) for t in (g, dlt, mn, vn))
    shapes = [weights[n].shape for n in small]
    wpk, gpk, mpk, vpk = (_pack_rows([src[n] for n in small]) for src in (weights, gs, mom1, mom2))
    outs = adamw(wpk, [gpk], mpk, vpk, name="adamw_small")
    for dst, packed in zip((grads, deltas, new_m, new_v), outs):
        for n, val in zip(small, _unpack_rows(packed, shapes)):
            dst[n] = val

    return (loss, grad_x[None], *[grads[n] for n in order], *[deltas[n] for n in order],
            *[new_m[n] for n in order], *[new_v[n] for n in order])


def _local_grads(x0, target, wf):
    a_norm, a_conv_b, a_b_r, a_b_i, a_lambda = (wf[n] for n in ("a_norm", "a_conv_b", "a_b_r", "a_b_i", "a_lambda"))
    kv_norm, final_norm = wf["kv_norm"], wf["final_norm"]

    (h_a,) = rms_fwd(x0, [a_norm], name="norm_a")
    proj_a = matmul(h_a, wf["a_w_in"], name="a_in")
    m_a, hst = lru_fwd(proj_a, wf["a_conv_w"], a_conv_b, wf["a_w_r"], a_b_r, wf["a_w_i"], a_b_i, a_lambda,
                       name="lru_fwd")
    x1 = matmul(m_a, wf["a_w_out"], residual=x0, name="a_out")
    kvn, hb = rms_fwd(x1, [kv_norm, wf["b_norm"]], name="norm_kv_b")
    kv = matmul(kvn, wf["w_kv"], out_dtype=BF16, name="kv_proj")
    proj_b = matmul(hb, wf["b_w_in"], name="b_in")
    m_b, o, offs = attn_fwd(proj_b, kv, name="attn_fwd")
    x2 = matmul(m_b, wf["b_w_out"], residual=x1, name="b_out")
    loss_part, g_final, dx2, dx2b = loss_bwd(x2, target, final_norm, name="loss_bwd")

    dm_b = matmul(dx2b, wf["b_w_out"], tb=True, name="b_out_dx")
    g_b_w_out = matmul(m_b, dx2b, ta=True, out_dtype=BF16, name="b_out_dw")
    do, dgate_b = gate_bwd(dm_b, o, proj_b, name="gate_bwd")
    dq, dk, dv = attn_bwd(proj_b, do, kv, offs, name="attn_bwd")
    dproj_b = jnp.concatenate([dq, dgate_b], axis=1)
    dkv = jnp.concatenate([dk, dv], axis=1)
    dhb = matmul(dproj_b, wf["b_w_in"], tb=True, name="b_in_dx")
    g_b_w_in = matmul(hb, dproj_b, ta=True, out_dtype=BF16, name="b_in_dw")
    dkvn = matmul(dkv, wf["w_kv"], tb=True, name="kv_dx")
    g_w_kv = matmul(kvn, dkv, ta=True, out_dtype=BF16, name="kv_dw")
    dx1, dx1b, (g_kv_norm, g_b_norm) = rms_bwd(
        x1, dx2, [(kv_norm, dkvn), (wf["b_norm"], dhb)], name="norm_kv_b_bwd")

    dm_a = matmul(dx1b, wf["a_w_out"], tb=True, name="a_out_dx")
    g_a_w_out = matmul(m_a, dx1b, ta=True, out_dtype=BF16, name="a_out_dw")
    dxpre, dgate_a, g_conv_w, g_conv_b, g_w_r, g_b_r, g_w_i, g_b_i, g_lambda = lru_bwd(
        proj_a, hst, dm_a, wf["a_conv_w"], a_conv_b, wf["a_w_r"], a_b_r, wf["a_w_i"], a_b_i, a_lambda,
        name="lru_bwd")
    dproj_a = jnp.concatenate([dxpre, dgate_a], axis=1)
    dh_a = matmul(dproj_a, wf["a_w_in"], tb=True, name="a_in_dx")
    g_a_w_in = matmul(h_a, dproj_a, ta=True, out_dtype=BF16, name="a_in_dw")
    grad_x, _, (g_a_norm,) = rms_bwd(x0, dx1, [(a_norm, dh_a)], name="norm_a_bwd")

    gbig = dict(a_w_in=g_a_w_in, a_w_r=g_w_r.astype(BF16), a_w_i=g_w_i.astype(BF16), a_w_out=g_a_w_out,
                w_kv=g_w_kv, b_w_in=g_b_w_in, b_w_out=g_b_w_out)
    gsmall = dict(a_norm=g_a_norm, a_conv_b=g_conv_b, a_b_r=g_b_r, a_b_i=g_b_i, a_lambda=g_lambda,
                  kv_norm=g_kv_norm, final_norm=g_final, a_conv_w=g_conv_w, b_norm=g_b_norm)
    return loss_part, grad_x, gbig, gsmall
```

```python
import math

import jax
import jax.numpy as jnp
from jax import lax
from jax.experimental import pallas as pl
from jax.experimental.pallas import tpu as pltpu

F32 = jnp.float32
BF16 = jnp.bfloat16
MESH = pl.DeviceIdType.MESH

EPS = 1e-6
LRU_C = 8.0
CONV_W = 4
HEAD_DIM = 128
ADAM_LR = 0.001
ADAM_B1 = 0.9
ADAM_B2 = 0.999
ADAM_EPS = 1e-08
ADAM_WD = 0.01
ADAM_STEP = 10

V7X_VMEM_LIMIT = 56 * 1024 * 1024
LANES = 128
SUBLANES = 8
ATT_BLOCK = 256
ATT_HEADS_FWD = 4
ATT_HEADS_BWD = 2
LOG2E = 1.4426950408889634


def _pick(dim, cands):
    for c in cands:
        if dim % c == 0:
            return c
    return dim


def _params(sem, vmem=V7X_VMEM_LIMIT):
    return pltpu.CompilerParams(dimension_semantics=sem, vmem_limit_bytes=vmem)


def _sigmoid(x):
    return 1.0 / (1.0 + jnp.exp(-x))


def matmul(a, b, *, ta=False, tb=False, out_dtype=F32, residual=None, name):
    m = a.shape[1] if ta else a.shape[0]
    kdim = a.shape[0] if ta else a.shape[1]
    n = b.shape[0] if tb else b.shape[1]
    assert (b.shape[1] if tb else b.shape[0]) == kdim
    tm = _pick(m, (1024, 640, 512, 256, 128))
    tn = _pick(n, (1024, 640, 512, 256, 128))
    tk = _pick(kdim, (512, 256, 128))
    nk = kdim // tk
    dn = (((0 if ta else 1,), (1 if tb else 0,)), ((), ()))

    def body(*refs):
        if residual is None:
            a_ref, b_ref, o_ref, acc = refs
            r_ref = None
        else:
            a_ref, b_ref, r_ref, o_ref, acc = refs
        k = pl.program_id(2)

        @pl.when(k == 0)
        def _():
            acc[...] = jnp.zeros_like(acc)

        acc[...] += lax.dot_general(a_ref[...].astype(BF16), b_ref[...].astype(BF16), dn,
                                    preferred_element_type=F32)

        @pl.when(k == nk - 1)
        def _():
            r = acc[...]
            if r_ref is not None:
                r = r + r_ref[...]
            o_ref[...] = r.astype(out_dtype)

    a_spec = (pl.BlockSpec((tk, tm), lambda i, j, k: (k, i)) if ta
              else pl.BlockSpec((tm, tk), lambda i, j, k: (i, k)))
    b_spec = (pl.BlockSpec((tn, tk), lambda i, j, k: (j, k)) if tb
              else pl.BlockSpec((tk, tn), lambda i, j, k: (k, j)))
    o_spec = pl.BlockSpec((tm, tn), lambda i, j, k: (i, j))
    in_specs = [a_spec, b_spec]
    args = [a, b]
    if residual is not None:
        in_specs.append(o_spec)
        args.append(residual)
    return pl.pallas_call(
        body, name=name, grid=(m // tm, n // tn, nk),
        in_specs=in_specs, out_specs=o_spec,
        out_shape=jax.ShapeDtypeStruct((m, n), out_dtype),
        scratch_shapes=[pltpu.VMEM((tm, tn), F32)],
        compiler_params=_params(("parallel", "parallel", "arbitrary")),
    )(*args)


def rms_fwd(x, gains, *, name):
    s, d = x.shape
    tr = _pick(s, (512, 256, 128, 8))
    ng = len(gains)

    def body(*refs):
        x_ref = refs[0]
        g_refs = refs[1:1 + ng]
        o_refs = refs[1 + ng:]
        xv = x_ref[...]
        y = xv * lax.rsqrt(jnp.mean(xv * xv, axis=-1, keepdims=True) + EPS)
        for g_ref, o_ref in zip(g_refs, o_refs):
            o_ref[...] = (y * g_ref[...]).astype(BF16)

    row = pl.BlockSpec((tr, d), lambda i: (i, 0))
    vec = pl.BlockSpec((1, d), lambda i: (0, 0))
    return pl.pallas_call(
        body, name=name, grid=(s // tr,),
        in_specs=[row] + [vec] * ng, out_specs=[row] * ng,
        out_shape=[jax.ShapeDtypeStruct((s, d), BF16)] * ng,
        compiler_params=_params(("parallel",)),
    )(x, *gains)


def rms_bwd(x, dres, norms, *, name):
    s, d = x.shape
    tr = _pick(s, (256, 128, 8))
    ng = len(norms)

    def body(*refs):
        x_ref, dres_ref = refs[0], refs[1]
        g_refs = refs[2:2 + ng]
        dh_refs = refs[2 + ng:2 + 2 * ng]
        dx_ref, dxb_ref = refs[2 + 2 * ng], refs[3 + 2 * ng]
        dg_refs = refs[4 + 2 * ng:]
        i = pl.program_id(0)
        xv = x_ref[...]
        r = lax.rsqrt(jnp.mean(xv * xv, axis=-1, keepdims=True) + EPS)
        xhat = xv * r
        dx = dres_ref[...]
        for g_ref, dh_ref, dg_ref in zip(g_refs, dh_refs, dg_refs):
            dh = dh_ref[...]
            part = jnp.sum(dh * xhat, axis=0, keepdims=True)

            @pl.when(i == 0)
            def _():
                dg_ref[...] = part

            @pl.when(i > 0)
            def _():
                dg_ref[...] += part

            dxhat = dh * g_ref[...]
            dx = dx + r * (dxhat - xhat * jnp.mean(dxhat * xhat, axis=-1, keepdims=True))
        dx_ref[...] = dx
        dxb_ref[...] = dx.astype(BF16)

    row = pl.BlockSpec((tr, d), lambda i: (i, 0))
    vec = pl.BlockSpec((1, d), lambda i: (0, 0))
    outs = pl.pallas_call(
        body, name=name, grid=(s // tr,),
        in_specs=[row, row] + [vec] * ng + [row] * ng,
        out_specs=[row, row] + [vec] * ng,
        out_shape=[jax.ShapeDtypeStruct((s, d), F32), jax.ShapeDtypeStruct((s, d), BF16)]
        + [jax.ShapeDtypeStruct((1, d), F32)] * ng,
        compiler_params=_params(("arbitrary",)),
    )(x, dres, *[g for g, _ in norms], *[dh for _, dh in norms])
    return outs[0], outs[1], list(outs[2:])


def loss_bwd(x2, target, gain, *, name):
    s, d = x2.shape
    tr = _pick(s, (256, 128, 8))
    nsteps = s // tr

    def body(x_ref, t_ref, g_ref, loss_ref, dg_ref, dx_ref, dxb_ref, sq_acc):
        i = pl.program_id(0)
        xv = x_ref[...]
        r = lax.rsqrt(jnp.mean(xv * xv, axis=-1, keepdims=True) + EPS)
        xhat = xv * r
        g = g_ref[...]
        err = xhat * g - t_ref[...]
        dy = err * (1.0 / d)
        sq = jnp.sum(err * err, axis=0, keepdims=True)
        dgp = jnp.sum(dy * xhat, axis=0, keepdims=True)

        @pl.when(i == 0)
        def _():
            sq_acc[...] = sq
            dg_ref[...] = dgp

        @pl.when(i > 0)
        def _():
            sq_acc[...] += sq
            dg_ref[...] += dgp

        dxhat = dy * g
        dx = r * (dxhat - xhat * jnp.mean(dxhat * xhat, axis=-1, keepdims=True))
        dx_ref[...] = dx
        dxb_ref[...] = dx.astype(BF16)

        @pl.when(i == nsteps - 1)
        def _():
            tot = jnp.sum(sq_acc[...], axis=-1, keepdims=True) * (0.5 / d)
            loss_ref[...] = jnp.broadcast_to(tot, (1, LANES))

    row = pl.BlockSpec((tr, d), lambda i: (i, 0))
    vec = pl.BlockSpec((1, d), lambda i: (0, 0))
    return pl.pallas_call(
        body, name=name, grid=(nsteps,),
        in_specs=[row, row, vec],
        out_specs=[pl.BlockSpec((1, LANES), lambda i: (0, 0)), vec, row, row],
        out_shape=[jax.ShapeDtypeStruct((1, LANES), F32), jax.ShapeDtypeStruct((1, d), F32),
                   jax.ShapeDtypeStruct((s, d), F32), jax.ShapeDtypeStruct((s, d), BF16)],
        scratch_shapes=[pltpu.VMEM((1, d), F32)],
        compiler_params=_params(("arbitrary",)),
    )(x2, target, gain)


def _lru_gates(xb, wr, wi, br, bi, sp):
    xbb = xb.astype(BF16)
    r = _sigmoid(jnp.dot(xbb, wr, preferred_element_type=F32) + br)
    ig = _sigmoid(jnp.dot(xbb, wi, preferred_element_type=F32) + bi)
    log_a = (-LRU_C) * r * sp
    a = jnp.exp(log_a)
    mult = jnp.sqrt(jnp.maximum(-jnp.tanh(log_a) * (a * a + 1.0), 0.0))
    return r, ig, a, mult


def _softplus_neg(lam):
    e = jnp.exp(-jnp.abs(lam))
    sp = jnp.maximum(-lam, 0.0) + jnp.log(1.0 + e)
    sg = jnp.where(lam >= 0, e, 1.0) / (1.0 + e)
    return sp, sg


def _conv(pad_ref, w, b, t):
    acc = b + w[CONV_W - 1:CONV_W, :] * pad_ref[pl.ds(SUBLANES, t), :]
    for dlt in range(1, CONV_W):
        acc = acc + w[CONV_W - 1 - dlt:CONV_W - dlt, :] * pad_ref[pl.ds(SUBLANES - dlt, t), :]
    return acc


def _lru_specs(t, bw, nb, time_of):
    blk = lambda c0: pl.BlockSpec((t, bw), lambda n, i, c0=c0: (time_of(i), c0 + n))
    vec = pl.BlockSpec((1, bw), lambda n, i: (0, n))
    wspec = pl.BlockSpec((None, bw, bw), lambda n, i: (n, 0, 0))
    cwspec = pl.BlockSpec((CONV_W, bw), lambda n, i: (0, n))
    return blk, vec, wspec, cwspec


def lru_fwd(proj, conv_w, conv_b, w_r, b_r, w_i, b_i, lam, *, name):
    s, r2 = proj.shape
    rr = r2 // 2
    nb, bw, _ = w_r.shape
    t = _pick(s, (512, 256, 128, 64, 8))
    ngroups = t // SUBLANES

    def body(xp_ref, gate_ref, cw_ref, cb_ref, wr_ref, br_ref, wi_ref, bi_ref, lam_ref,
             m_ref, h_ref, pad, hcarry, a_scr, u_scr):
        i = pl.program_id(1)

        @pl.when(i == 0)
        def _():
            pad[0:SUBLANES, :] = jnp.zeros((SUBLANES, bw), F32)
            hcarry[...] = jnp.zeros_like(hcarry)

        xpre = xp_ref[...]
        pad[pl.ds(SUBLANES, t), :] = xpre
        xb = _conv(pad, cw_ref[...], cb_ref[...], t)
        pad[0:SUBLANES, :] = xpre[t - SUBLANES:, :]
        sp, _ = _softplus_neg(lam_ref[...])
        _, ig, a, mult = _lru_gates(xb, wr_ref[...], wi_ref[...], br_ref[...], bi_ref[...], sp)
        a_scr[...] = a
        u_scr[...] = mult * (ig * xb)
        row = lax.broadcasted_iota(jnp.int32, (SUBLANES, bw), 0)

        def group(gi, hprev):
            off = pl.multiple_of(gi * SUBLANES, SUBLANES)
            av = a_scr[pl.ds(off, SUBLANES), :]
            uv = u_scr[pl.ds(off, SUBLANES), :]
            for dlt in (1, 2, 4):
                keep = row >= dlt
                uv = jnp.where(keep, av * pltpu.roll(uv, dlt, 0) + uv, uv)
                av = jnp.where(keep, av * pltpu.roll(av, dlt, 0), av)
            hv = av * hprev + uv
            h_ref[pl.ds(off, SUBLANES), :] = hv
            return hv[SUBLANES - 1:SUBLANES, :]

        hcarry[...] = lax.fori_loop(0, ngroups, group, hcarry[...])
        gate = gate_ref[...]
        m_ref[...] = (h_ref[...] * (gate * _sigmoid(gate))).astype(BF16)

    blk, vec, wspec, cwspec = _lru_specs(t, bw, nb, lambda i: i)
    return pl.pallas_call(
        body, name=name, grid=(nb, s // t),
        in_specs=[blk(0), blk(nb), cwspec, vec, wspec, vec, wspec, vec, vec],
        out_specs=[blk(0), blk(0)],
        out_shape=[jax.ShapeDtypeStruct((s, rr), BF16), jax.ShapeDtypeStruct((s, rr), F32)],
        scratch_shapes=[pltpu.VMEM((t + SUBLANES, bw), F32), pltpu.VMEM((1, bw), F32),
                        pltpu.VMEM((t, bw), F32), pltpu.VMEM((t, bw), F32)],
        compiler_params=_params(("parallel", "arbitrary")),
    )(proj, proj, conv_w, conv_b, w_r, b_r, w_i, b_i, lam)


def lru_bwd(proj, hst, dm, conv_w, conv_b, w_r, b_r, w_i, b_i, lam, *, name):
    s, r2 = proj.shape
    rr = r2 // 2
    nb, bw, _ = w_r.shape
    t = _pick(s, (512, 256, 128, 64, 8))
    nt = s // t
    ngroups = t // SUBLANES
    nt_dims = (((1,), (1,)), ((), ()))
    tn_dims = (((0,), (0,)), ((), ()))

    def body(xp_ref, xhalo_ref, gate_ref, h_ref, hhalo_ref, dm_ref, cw_ref, cb_ref, wr_ref, br_ref, wi_ref,
             bi_ref, lam_ref,
             dxp_ref, dgate_ref, dcw_ref, dcb_ref, dwr_ref, dbr_ref, dwi_ref, dbi_ref, dlam_ref,
             pad, hpad, dpad, ecarry, a_scr, b_scr, d_scr):
        step = pl.program_id(1)

        @pl.when(step == 0)
        def _():
            dpad[pl.ds(t, SUBLANES), :] = jnp.zeros((SUBLANES, bw), F32)
            ecarry[...] = jnp.zeros_like(ecarry)
            dcw_ref[...] = jnp.zeros_like(dcw_ref)
            dcb_ref[...] = jnp.zeros_like(dcb_ref)
            dwr_ref[...] = jnp.zeros_like(dwr_ref)
            dbr_ref[...] = jnp.zeros_like(dbr_ref)
            dwi_ref[...] = jnp.zeros_like(dwi_ref)
            dbi_ref[...] = jnp.zeros_like(dbi_ref)
            dlam_ref[...] = jnp.zeros_like(dlam_ref)

        past = jnp.where(step == nt - 1, 0.0, 1.0)
        pad[0:SUBLANES, :] = xhalo_ref[...] * past
        pad[pl.ds(SUBLANES, t), :] = xp_ref[...]
        hpad[0:SUBLANES, :] = hhalo_ref[...] * past
        hpad[pl.ds(SUBLANES, t), :] = h_ref[...]
        cw = cw_ref[...]
        xb = _conv(pad, cw, cb_ref[...], t)
        sp, sg = _softplus_neg(lam_ref[...])
        wr = wr_ref[...]
        wi = wi_ref[...]
        r, ig, a, mult = _lru_gates(xb, wr, wi, br_ref[...], bi_ref[...], sp)
        gate = gate_ref[...]
        sgate = _sigmoid(gate)
        dmv = dm_ref[...]
        dgate_ref[...] = (dmv * h_ref[...] * (sgate * (1.0 + gate * (1.0 - sgate)))).astype(BF16)
        dy = dmv * (gate * sgate)
        a_scr[...] = a
        b_scr[...] = a * dy
        row = lax.broadcasted_iota(jnp.int32, (SUBLANES, bw), 0)

        def group(gi, enext):
            off = pl.multiple_of((ngroups - 1 - gi) * SUBLANES, SUBLANES)
            av = a_scr[pl.ds(off, SUBLANES), :]
            bv = b_scr[pl.ds(off, SUBLANES), :]
            for dlt in (1, 2, 4):
                keep = row < SUBLANES - dlt
                bv = jnp.where(keep, av * pltpu.roll(bv, SUBLANES - dlt, 0) + bv, bv)
                av = jnp.where(keep, av * pltpu.roll(av, SUBLANES - dlt, 0), av)
            ev = av * enext + bv
            d_scr[pl.ds(off, SUBLANES), :] = jnp.where(row == SUBLANES - 1, enext,
                                                       pltpu.roll(ev, SUBLANES - 1, 0))
            return ev[0:1, :]

        ecarry[...] = lax.fori_loop(0, ngroups, group, ecarry[...])
        dtot = dy + d_scr[...]
        da = dtot * hpad[pl.ds(SUBLANES - 1, t), :]
        dmult = dtot * (ig * xb)
        dlog_a = da * a - dmult * (a * a) / mult
        dr_pre = dlog_a * ((-LRU_C) * sp) * (r * (1.0 - r))
        di_pre = (dtot * mult * xb) * (ig * (1.0 - ig))
        dlam_ref[...] += jnp.sum(dlog_a * r, axis=0, keepdims=True) * (LRU_C * sg)
        dbr_ref[...] += jnp.sum(dr_pre, axis=0, keepdims=True)
        dbi_ref[...] += jnp.sum(di_pre, axis=0, keepdims=True)
        drb = dr_pre.astype(BF16)
        dib = di_pre.astype(BF16)
        xbb = xb.astype(BF16)
        dxb = (dtot * mult * ig
               + lax.dot_general(drb, wr, nt_dims, preferred_element_type=F32)
               + lax.dot_general(dib, wi, nt_dims, preferred_element_type=F32))
        dwr_ref[...] += lax.dot_general(xbb, drb, tn_dims, preferred_element_type=F32)
        dwi_ref[...] += lax.dot_general(xbb, dib, tn_dims, preferred_element_type=F32)
        dcb_ref[...] += jnp.sum(dxb, axis=0, keepdims=True)
        dpad[pl.ds(0, t), :] = dxb
        dxpre = cw[CONV_W - 1:CONV_W, :] * dxb
        dcw_ref[CONV_W - 1:CONV_W, :] += jnp.sum(dxb * pad[pl.ds(SUBLANES, t), :], axis=0, keepdims=True)
        for dlt in range(1, CONV_W):
            dxpre = dxpre + cw[CONV_W - 1 - dlt:CONV_W - dlt, :] * dpad[pl.ds(dlt, t), :]
            dcw_ref[CONV_W - 1 - dlt:CONV_W - dlt, :] += jnp.sum(
                dxb * pad[pl.ds(SUBLANES - dlt, t), :], axis=0, keepdims=True)
        dpad[pl.ds(t, SUBLANES), :] = dxb[0:SUBLANES, :]
        dxp_ref[...] = dxpre.astype(BF16)

    rev = lambda i: nt - 1 - i
    blk, vec, wspec, cwspec = _lru_specs(t, bw, nb, rev)
    halo = pl.BlockSpec((SUBLANES, bw), lambda n, i: (jnp.maximum(rev(i) * ngroups - 1, 0), n))
    return pl.pallas_call(
        body, name=name, grid=(nb, nt),
        in_specs=[blk(0), halo, blk(nb), blk(0), halo, blk(0), cwspec, vec, wspec, vec, wspec, vec, vec],
        out_specs=[blk(0), blk(0), cwspec, vec, wspec, vec, wspec, vec, vec],
        out_shape=[jax.ShapeDtypeStruct((s, rr), BF16), jax.ShapeDtypeStruct((s, rr), BF16),
                   jax.ShapeDtypeStruct((CONV_W, rr), F32), jax.ShapeDtypeStruct((1, rr), F32),
                   jax.ShapeDtypeStruct((nb, bw, bw), F32), jax.ShapeDtypeStruct((1, rr), F32),
                   jax.ShapeDtypeStruct((nb, bw, bw), F32), jax.ShapeDtypeStruct((1, rr), F32),
                   jax.ShapeDtypeStruct((1, rr), F32)],
        scratch_shapes=[pltpu.VMEM((t + SUBLANES, bw), F32), pltpu.VMEM((t + SUBLANES, bw), F32),
                        pltpu.VMEM((t + SUBLANES, bw), F32), pltpu.VMEM((1, bw), F32),
                        pltpu.VMEM((t, bw), F32), pltpu.VMEM((t, bw), F32), pltpu.VMEM((t, bw), F32)],
        compiler_params=_params(("parallel", "arbitrary")),
    )(proj, proj, proj, hst, hst, dm, conv_w, conv_b, w_r, b_r, w_i, b_i, lam)


def _split_bf16(x, axis):
    hi = x.astype(BF16)
    lo = (x - hi.astype(F32)).astype(BF16)
    return jnp.concatenate([hi, lo], axis=axis)


def _softplus(z):
    return jnp.maximum(z, 0.0) + jnp.log(1.0 + jnp.exp2(jnp.abs(z) * (-LOG2E)))


def attn_fwd(projb, kv, *, name):
    s, a2 = projb.shape
    a = a2 // 2
    nh = a // HEAD_DIM
    blk = ATT_BLOCK if s % ATT_BLOCK == 0 else s
    nq = s // blk
    assert nq <= LANES
    scale = 1.0 / math.sqrt(HEAD_DIM)
    nt_dims = (((1,), (1,)), ((), ()))

    hp = ATT_HEADS_FWD if nh % ATT_HEADS_FWD == 0 else 1
    wd = hp * HEAD_DIM

    def body(q_ref, g_ref, k_ref, v_ref, m_ref, o_ref, off_ref, acc):
        i = pl.program_id(1)
        qb = q_ref[...].astype(BF16)
        rowi = lax.broadcasted_iota(jnp.int32, (blk, blk), 0)
        coli = lax.broadcasted_iota(jnp.int32, (blk, blk), 1)
        from_mat = (rowi >= coli).astype(BF16)
        from_mat2 = jnp.concatenate([from_mat, from_mat], axis=0)
        causal = coli < rowi
        lane = lax.broadcasted_iota(jnp.int32, (blk, LANES), 1)

        def tile(j, carries, masked):
            off = pl.multiple_of(j * blk, blk)
            cols = [slice(hh * HEAD_DIM, (hh + 1) * HEAD_DIM) for hh in range(hp)]
            zraw = [lax.dot_general(qb[:, c], k_ref[pl.ds(off, blk), c], nt_dims, preferred_element_type=F32)
                    for c in cols]
            zs, sps, sums = [], [], []
            for zr in zraw:
                z = zr * scale
                sp = _softplus(z)
                if masked:
                    sp = jnp.where(causal, sp, 0.0)
                sums.append(jnp.dot(_split_bf16(sp, 1), from_mat2, preferred_element_type=F32))
                zs.append(z)
                sps.append(sp)
            out = []
            for hh in range(hp):
                w = jnp.exp(zs[hh] - sums[hh] - carries[hh])
                if masked:
                    w = jnp.where(causal, w, 0.0)
                acc[:, cols[hh]] += jnp.dot(w.astype(BF16), v_ref[pl.ds(off, blk), cols[hh]],
                                            preferred_element_type=F32)
                off_ref[hh] = jnp.where(lane == j, carries[hh], off_ref[hh])
                out.append(carries[hh] + jnp.sum(sps[hh], axis=1, keepdims=True))
            return tuple(out)

        acc[...] = jnp.zeros_like(acc)
        off_ref[...] = jnp.zeros_like(off_ref)
        carries = tile(i, tuple(jnp.zeros((blk, 1), F32) for _ in range(hp)), True)
        lax.fori_loop(0, i, lambda jj, c: tile(i - 1 - jj, c, False), carries)
        o = acc[...]
        o_ref[...] = o
        gate = g_ref[...]
        m_ref[...] = (o * (gate * _sigmoid(gate))).astype(BF16)

    ng = nh // hp
    qspec = lambda c0: pl.BlockSpec((blk, wd), lambda h, i, c0=c0: (i, c0 + h))
    kspec = lambda c0: pl.BlockSpec((s, wd), lambda h, i, c0=c0: (0, c0 + h), pipeline_mode=pl.Buffered(1))
    return pl.pallas_call(
        body, name=name, grid=(ng, nq),
        in_specs=[qspec(0), qspec(ng), kspec(0), kspec(ng)],
        out_specs=[qspec(0), qspec(0), pl.BlockSpec((hp, None, blk, LANES), lambda h, i: (h, i, 0, 0))],
        out_shape=[jax.ShapeDtypeStruct((s, a), BF16), jax.ShapeDtypeStruct((s, a), F32),
                   jax.ShapeDtypeStruct((nh, nq, blk, LANES), F32)],
        scratch_shapes=[pltpu.VMEM((blk, wd), F32)],
        compiler_params=_params(("parallel", "arbitrary")),
    )(projb, projb, kv, kv)


def attn_bwd(projb, do, kv, offs, *, name):
    s, a2 = projb.shape
    a = a2 // 2
    nh = a // HEAD_DIM
    blk = ATT_BLOCK if s % ATT_BLOCK == 0 else s
    nq = s // blk
    scale = 1.0 / math.sqrt(HEAD_DIM)
    nt_dims = (((1,), (1,)), ((), ()))

    hp = ATT_HEADS_BWD if nh % ATT_HEADS_BWD == 0 else 1
    wd = hp * HEAD_DIM
    tn_dims = (((0,), (0,)), ((), ()))

    def body(q_ref, do_ref, k_ref, v_ref, off_ref, dq_ref, dk_ref, dv_ref, dk_acc, dv_acc, dq_acc):
        i = pl.program_id(1)

        @pl.when(i == 0)
        def _():
            dk_acc[...] = jnp.zeros_like(dk_acc)
            dv_acc[...] = jnp.zeros_like(dv_acc)

        qb = q_ref[...].astype(BF16)
        dob = do_ref[...]
        rowi = lax.broadcasted_iota(jnp.int32, (blk, blk), 0)
        coli = lax.broadcasted_iota(jnp.int32, (blk, blk), 1)
        from_mat = (coli >= rowi).astype(BF16)
        upto_mat = (coli <= rowi).astype(BF16)
        from_mat2 = jnp.concatenate([from_mat, from_mat], axis=1)
        upto_mat2 = jnp.concatenate([upto_mat, upto_mat], axis=1)
        causal = rowi < coli
        sub = lax.broadcasted_iota(jnp.int32, (LANES, blk), 0)
        offs_t = [off_ref[hh].T for hh in range(hp)]
        cols = [slice(hh * HEAD_DIM, (hh + 1) * HEAD_DIM) for hh in range(hp)]
        dq_acc[...] = jnp.zeros_like(dq_acc)

        def tile(j, gcarries, masked):
            off = pl.multiple_of(j * blk, blk)
            rows = pl.ds(off, blk)
            zraw = [lax.dot_general(k_ref[rows, c], qb[:, c], nt_dims, preferred_element_type=F32) for c in cols]
            dws = [lax.dot_general(v_ref[rows, c], dob[:, c], nt_dims, preferred_element_type=F32) for c in cols]
            zs, sigs, sums = [], [], []
            for zr in zraw:
                z = zr * scale
                sp = _softplus(z)
                sigs.append(jnp.exp(z - sp))
                if masked:
                    sp = jnp.where(causal, sp, 0.0)
                sums.append(jnp.dot(from_mat2, _split_bf16(sp, 0), preferred_element_type=F32))
                zs.append(z)
            gs, totals = [], []
            for hh in range(hp):
                offj = jnp.sum(jnp.where(sub == j, offs_t[hh], 0.0), axis=0, keepdims=True)
                w = jnp.exp(zs[hh] - sums[hh] - offj)
                if masked:
                    w = jnp.where(causal, w, 0.0)
                g = w * dws[hh]
                dv_acc[rows, cols[hh]] += jnp.dot(w.astype(BF16), dob[:, cols[hh]], preferred_element_type=F32)
                totals.append(jnp.dot(upto_mat2, _split_bf16(g, 0), preferred_element_type=F32))
                gs.append(g)
            out = []
            for hh in range(hp):
                dz = gs[hh] - (totals[hh] + gcarries[hh]) * sigs[hh]
                if masked:
                    dz = jnp.where(causal, dz, 0.0)
                dzb = (dz * scale).astype(BF16)
                dk_acc[rows, cols[hh]] += jnp.dot(dzb, qb[:, cols[hh]], preferred_element_type=F32)
                dq_acc[:, cols[hh]] += lax.dot_general(dzb, k_ref[rows, cols[hh]], tn_dims,
                                                       preferred_element_type=F32)
                out.append(gcarries[hh] + jnp.sum(gs[hh], axis=0, keepdims=True))
            return tuple(out)

        init = tuple(jnp.zeros((1, blk), F32) for _ in range(hp))
        tile(i, lax.fori_loop(0, i, lambda j, c: tile(j, c, False), init), True)
        dq_ref[...] = dq_acc[...].astype(BF16)

        @pl.when(i == nq - 1)
        def _():
            dk_ref[...] = dk_acc[...].astype(BF16)
            dv_ref[...] = dv_acc[...].astype(BF16)

    ng = nh // hp
    once = pl.Buffered(1)
    qspec = lambda c0: pl.BlockSpec((blk, wd), lambda h, i, c0=c0: (i, c0 + h))
    kspec = lambda c0: pl.BlockSpec((s, wd), lambda h, i, c0=c0: (0, c0 + h), pipeline_mode=once)
    return pl.pallas_call(
        body, name=name, grid=(ng, nq),
        in_specs=[qspec(0), qspec(0), kspec(0), kspec(ng),
                  pl.BlockSpec((hp, None, blk, LANES), lambda h, i: (h, i, 0, 0))],
        out_specs=[qspec(0), kspec(0), kspec(0)],
        out_shape=[jax.ShapeDtypeStruct((s, a), BF16)] * 3,
        scratch_shapes=[pltpu.VMEM((s, wd), F32), pltpu.VMEM((s, wd), F32), pltpu.VMEM((blk, wd), F32)],
        compiler_params=_params(("parallel", "arbitrary")),
    )(projb, do, kv, kv, offs)


def gate_bwd(dm, o, projb, *, name):
    s, a = dm.shape
    tr = _pick(s, (512, 256, 128, 8))

    def body(dm_ref, o_ref, g_ref, do_ref, dg_ref):
        gate = g_ref[...]
        sg = _sigmoid(gate)
        dmv = dm_ref[...]
        do_ref[...] = (dmv * (gate * sg)).astype(BF16)
        dg_ref[...] = (dmv * o_ref[...] * (sg * (1.0 + gate * (1.0 - sg)))).astype(BF16)

    row = pl.BlockSpec((tr, a), lambda i: (i, 0))
    return pl.pallas_call(
        body, name=name, grid=(s // tr,),
        in_specs=[row, row, pl.BlockSpec((tr, a), lambda i: (i, 1))],
        out_specs=[row, row],
        out_shape=[jax.ShapeDtypeStruct((s, a), BF16)] * 2,
        compiler_params=_params(("parallel",)),
    )(dm, o, projb)


def _as2d(x):
    n = x.size
    cols = x.shape[-1]
    if cols % LANES != 0:
        cols = LANES
    return x.reshape(n // cols, cols)


def sum_parts(parts, *, name):
    p, rows, cols = parts.shape
    tr = _pick(rows, (512, 256, 128, 64, 32, 16))

    def body(p_ref, o_ref):
        acc = p_ref[0].astype(F32)
        for k in range(1, p):
            acc = acc + p_ref[k].astype(F32)
        o_ref[...] = acc

    return pl.pallas_call(
        body, name=name, grid=(rows // tr,),
        in_specs=[pl.BlockSpec((p, tr, cols), lambda i: (0, i, 0))],
        out_specs=pl.BlockSpec((tr, cols), lambda i: (i, 0)),
        out_shape=jax.ShapeDtypeStruct((rows, cols), F32),
        compiler_params=_params(("parallel",)),
    )(parts)


def adamw(w, g_parts, m, v, *, name):
    rows, cols = w.shape
    tr = _pick(rows, (128, 64, 32, 16, 8))
    np_ = len(g_parts)
    c1 = 1.0 / (1.0 - ADAM_B1 ** ADAM_STEP)
    c2 = 1.0 / (1.0 - ADAM_B2 ** ADAM_STEP)

    def body(*refs):
        w_ref, m_ref, v_ref = refs[0], refs[1], refs[2]
        g_refs = refs[3:3 + np_]
        go_ref, d_ref, mo_ref, vo_ref = refs[3 + np_:]
        g = g_refs[0][...]
        for gr in g_refs[1:]:
            g = g + gr[...]
        mn = ADAM_B1 * m_ref[...] + (1.0 - ADAM_B1) * g
        vn = ADAM_B2 * v_ref[...] + (1.0 - ADAM_B2) * (g * g)
        go_ref[...] = g
        mo_ref[...] = mn
        vo_ref[...] = vn
        d_ref[...] = (-ADAM_LR) * ((mn * c1) / (jnp.sqrt(vn * c2) + ADAM_EPS) + ADAM_WD * w_ref[...])

    spec = pl.BlockSpec((tr, cols), lambda i: (i, 0))
    return pl.pallas_call(
        body, name=name, grid=(rows // tr,),
        in_specs=[spec] * (3 + np_), out_specs=[spec] * 4,
        out_shape=[jax.ShapeDtypeStruct((rows, cols), F32)] * 4,
        compiler_params=_params(("parallel",)),
    )(w, m, v, *g_parts)


def _place():
    return lax.axis_index("x"), lax.axis_index("y"), lax.axis_index("c")


def _chip_peers(x, y, c):
    return [(1 - x, y, c), (x, 1 - y, c), (1 - x, 1 - y, c)]


def _shard_of(ref, axis, idx, n):
    start = pl.multiple_of(idx * n, n)
    sl = [slice(None)] * len(ref.shape)
    sl[axis] = pl.ds(start, n)
    return ref.at[tuple(sl)]


def gather_weights(shards, axes, *, name):
    na = len(shards)
    hbm = pl.BlockSpec(memory_space=pl.ANY)
    full_shapes = []
    for sh, ax in zip(shards, axes):
        shp = list(sh.shape)
        shp[ax] *= 4
        full_shapes.append(jax.ShapeDtypeStruct(tuple(shp), sh.dtype))

    def body(*refs):
        s_refs = refs[:na]
        f_refs = refs[na:2 * na]
        send_sems, recv_sems, local_sems = refs[2 * na:]
        x, y, c = _place()
        me = 2 * x + y
        peers = _chip_peers(x, y, c)
        copies = []
        for ai in range(na):
            n = s_refs[ai].shape[axes[ai]]
            mine = _shard_of(f_refs[ai], axes[ai], me, n)
            loc = pltpu.make_async_copy(s_refs[ai], mine, local_sems.at[ai])
            loc.start()
            copies.append((loc,))
            for k, peer in enumerate(peers):
                theirs = _shard_of(f_refs[ai], axes[ai], 2 * peer[0] + peer[1], n)
                snd = pltpu.make_async_remote_copy(
                    src_ref=s_refs[ai], dst_ref=mine, send_sem=send_sems.at[ai * 3 + k],
                    recv_sem=recv_sems.at[ai * 3 + k], device_id=peer, device_id_type=MESH)
                snd.start()
                rcv = pltpu.make_async_remote_copy(
                    src_ref=s_refs[ai], dst_ref=theirs, send_sem=send_sems.at[ai * 3 + k],
                    recv_sem=recv_sems.at[ai * 3 + k], device_id=peer, device_id_type=MESH)
                copies.append((snd, rcv))
        for cp in copies:
            if len(cp) == 1:
                cp[0].wait()
            else:
                cp[0].wait_send()
                cp[1].wait_recv()

    return pl.pallas_call(
        body, name=name,
        in_specs=[hbm] * na, out_specs=[hbm] * na, out_shape=full_shapes,
        scratch_shapes=[pltpu.SemaphoreType.DMA((3 * na,)), pltpu.SemaphoreType.DMA((3 * na,)),
                        pltpu.SemaphoreType.DMA((na,))],
    )(*shards)


def scatter_grads(grads, axes, *, name):
    na = len(grads)
    hbm = pl.BlockSpec(memory_space=pl.ANY)
    out_shapes = []
    for g, ax in zip(grads, axes):
        shp = list(g.shape)
        shp[ax] //= 4
        out_shapes.append(jax.ShapeDtypeStruct((4, *shp), g.dtype))

    def body(*refs):
        g_refs = refs[:na]
        p_refs = refs[na:2 * na]
        send_sems, recv_sems, local_sems = refs[2 * na:]
        x, y, c = _place()
        me = 2 * x + y
        peers = _chip_peers(x, y, c)
        copies = []
        for ai in range(na):
            n = p_refs[ai].shape[1 + axes[ai]]
            loc = pltpu.make_async_copy(_shard_of(g_refs[ai], axes[ai], me, n), p_refs[ai].at[0], local_sems.at[ai])
            loc.start()
            copies.append(loc)
            for k, peer in enumerate(peers):
                cp = pltpu.make_async_remote_copy(
                    src_ref=_shard_of(g_refs[ai], axes[ai], 2 * peer[0] + peer[1], n), dst_ref=p_refs[ai].at[1 + k],
                    send_sem=send_sems.at[ai * 3 + k], recv_sem=recv_sems.at[ai * 3 + k],
                    device_id=peer, device_id_type=MESH)
                cp.start()
                copies.append(cp)
        for cp in copies:
            cp.wait()

    return pl.pallas_call(
        body, name=name,
        in_specs=[hbm] * na, out_specs=[hbm] * na, out_shape=out_shapes,
        scratch_shapes=[pltpu.SemaphoreType.DMA((3 * na,)), pltpu.SemaphoreType.DMA((3 * na,)),
                        pltpu.SemaphoreType.DMA((na,))],
    )(*grads)


def swap_cores(arrs, *, name):
    na = len(arrs)
    hbm = pl.BlockSpec(memory_space=pl.ANY)

    def body(*refs):
        a_refs = refs[:na]
        o_refs = refs[na:2 * na]
        send_sems, recv_sems = refs[2 * na:]
        x, y, c = _place()
        copies = []
        for ai in range(na):
            cp = pltpu.make_async_remote_copy(
                src_ref=a_refs[ai], dst_ref=o_refs[ai], send_sem=send_sems.at[ai], recv_sem=recv_sems.at[ai],
                device_id=(x, y, 1 - c), device_id_type=MESH)
            cp.start()
            copies.append(cp)
        for cp in copies:
            cp.wait()

    return pl.pallas_call(
        body, name=name,
        in_specs=[hbm] * na, out_specs=[hbm] * na,
        out_shape=[jax.ShapeDtypeStruct(a.shape, a.dtype) for a in arrs],
        scratch_shapes=[pltpu.SemaphoreType.DMA((na,)), pltpu.SemaphoreType.DMA((na,))],
    )(*arrs)


def allreduce_small(buf, *, name):
    rows, cols = buf.shape

    def body(b_ref, o_ref, slots, send_sems, recv_sems):
        x, y, c = _place()
        me = 4 * x + 2 * y + c
        slots[0] = b_ref[...]
        copies = []
        for rel in range(1, 8):
            peer = (x ^ (rel >> 2), y ^ ((rel >> 1) & 1), c ^ (rel & 1))
            cp = pltpu.make_async_remote_copy(
                src_ref=b_ref, dst_ref=slots.at[rel], send_sem=send_sems.at[rel - 1],
                recv_sem=recv_sems.at[rel - 1], device_id=peer, device_id_type=MESH)
            cp.start()
            copies.append(cp)
        for cp in copies:
            cp.wait()
        acc = slots[me]
        for dev in range(1, 8):
            acc = acc + slots[dev ^ me]
        o_ref[...] = acc

    vm = pl.BlockSpec(memory_space=pltpu.VMEM)
    return pl.pallas_call(
        body, name=name, in_specs=[vm], out_specs=vm,
        out_shape=jax.ShapeDtypeStruct((rows, cols), F32),
        scratch_shapes=[pltpu.VMEM((8, rows, cols), F32), pltpu.SemaphoreType.DMA((7,)),
                        pltpu.SemaphoreType.DMA((7,))],
    )(buf)


def _pack_rows(arrs):
    parts = []
    for a in arrs:
        p = a.reshape(-1, LANES)
        parts.append(jnp.pad(p, ((0, (-p.shape[0]) % SUBLANES), (0, 0))))
    return jnp.concatenate(parts, axis=0)


def _unpack_rows(buf, shapes):
    out, r0 = [], 0
    for shp in shapes:
        n = math.prod(shp) // LANES
        out.append(buf[r0:r0 + n].reshape(shp))
        r0 += n + (-n) % SUBLANES
    return out


def kernel(x, a_norm, a_w_in, a_conv_w, a_conv_b, a_w_r, a_b_r, a_w_i, a_b_i, a_lambda, a_w_out, kv_norm, w_kv, b_norm, b_w_in, b_w_out, final_norm, loss_target, m_a_norm, m_a_w_in, m_a_conv_w, m_a_conv_b, m_a_w_r, m_a_b_r, m_a_w_i, m_a_b_i, m_a_lambda, m_a_w_out, m_kv_norm, m_w_kv, m_b_norm, m_b_w_in, m_b_w_out, m_final_norm, v_a_norm, v_a_w_in, v_a_conv_w, v_a_conv_b, v_a_w_r, v_a_b_r, v_a_w_i, v_a_b_i, v_a_lambda, v_a_w_out, v_kv_norm, v_w_kv, v_b_norm, v_b_w_in, v_b_w_out, v_final_norm):
    weights = dict(a_norm=a_norm, a_w_in=a_w_in, a_conv_w=a_conv_w, a_conv_b=a_conv_b, a_w_r=a_w_r, a_b_r=a_b_r,
                   a_w_i=a_w_i, a_b_i=a_b_i, a_lambda=a_lambda, a_w_out=a_w_out, kv_norm=kv_norm, w_kv=w_kv,
                   b_norm=b_norm, b_w_in=b_w_in, b_w_out=b_w_out, final_norm=final_norm)
    mom1 = dict(a_norm=m_a_norm, a_w_in=m_a_w_in, a_conv_w=m_a_conv_w, a_conv_b=m_a_conv_b, a_w_r=m_a_w_r,
                a_b_r=m_a_b_r, a_w_i=m_a_w_i, a_b_i=m_a_b_i, a_lambda=m_a_lambda, a_w_out=m_a_w_out,
                kv_norm=m_kv_norm, w_kv=m_w_kv, b_norm=m_b_norm, b_w_in=m_b_w_in, b_w_out=m_b_w_out,
                final_norm=m_final_norm)
    mom2 = dict(a_norm=v_a_norm, a_w_in=v_a_w_in, a_conv_w=v_a_conv_w, a_conv_b=v_a_conv_b, a_w_r=v_a_w_r,
                a_b_r=v_a_b_r, a_w_i=v_a_w_i, a_b_i=v_a_b_i, a_lambda=v_a_lambda, a_w_out=v_a_w_out,
                kv_norm=v_kv_norm, w_kv=v_w_kv, b_norm=v_b_norm, b_w_in=v_b_w_in, b_w_out=v_b_w_out,
                final_norm=v_final_norm)
    order = list(weights)
    x0 = x[0]
    target = loss_target[0]
    d = x0.shape[1]
    chip = 2 * lax.axis_index("x") + lax.axis_index("y")

    big = ["a_w_in", "a_w_r", "a_w_i", "a_w_out", "w_kv", "b_w_in", "b_w_out"]
    big_axis = dict(a_w_in=1, a_w_r=1, a_w_i=1, a_w_out=0, w_kv=1, b_w_in=1, b_w_out=0)
    local = dict(a_w_in=a_w_in[0], a_w_r=a_w_r[0], a_w_i=a_w_i[0], a_w_out=a_w_out[0], w_kv=w_kv,
                 b_w_in=b_w_in[0], b_w_out=b_w_out[0])
    shards = [local[n].astype(BF16) for n in big] + [a_conv_w[0], b_norm]
    full = gather_weights(shards, [big_axis[n] for n in big] + [1, 1], name="gather_weights")
    wf = dict(zip(big + ["a_conv_w", "b_norm"], full))
    wf.update(a_norm=a_norm, a_conv_b=a_conv_b, a_b_r=a_b_r, a_b_i=a_b_i, a_lambda=a_lambda,
              kv_norm=kv_norm.reshape(1, d), final_norm=final_norm.reshape(1, d))
    loss_part, grad_x, gbig, gsmall = _local_grads(x0, target, wf)

    parts = scatter_grads([gbig[n] for n in big], [big_axis[n] for n in big], name="scatter_grads")
    sums = [sum_parts(p.reshape(4, *_as2d(p[0]).shape), name="sum_" + n) for n, p in zip(big, parts)]
    others = swap_cores(sums, name="swap_cores")

    small = ["a_norm", "a_conv_b", "a_b_r", "a_b_i", "a_lambda", "kv_norm", "final_norm", "a_conv_w", "b_norm"]
    buf = _pack_rows([gsmall[n] for n in small] + [loss_part])
    red = allreduce_small(buf, name="allreduce_small")
    red_list = _unpack_rows(red, [gsmall[n].shape for n in small] + [(1, LANES)])
    gs = dict(zip(small, red_list[:-1]))
    loss = red_list[-1][0, 0]
    n_conv = a_conv_w.shape[2]
    gs["a_conv_w"] = lax.dynamic_slice_in_dim(gs["a_conv_w"], chip * n_conv, n_conv, axis=1)
    n_bn = b_norm.shape[1]
    gs["b_norm"] = lax.dynamic_slice_in_dim(gs["b_norm"], chip * n_bn, n_bn, axis=1)

    grads, deltas, new_m, new_v = {}, {}, {}, {}
    for n, s_mine, s_other in zip(big, sums, others):
        shp = weights[n].shape
        g, dlt, mn, vn = adamw(_as2d(weights[n]), [s_mine, s_other], _as2d(mom1[n]), _as2d(mom2[n]),
                               name="adamw_" + n)
        grads[n], deltas[n], new_m[n], new_v[n] = (t.reshape(shp) for t in (g, dlt, mn, vn))
    shapes = [weights[n].shape for n in small]
    wpk, gpk, mpk, vpk = (_pack_rows([src[n] for n in small]) for src in (weights, gs, mom1, mom2))
    outs = adamw(wpk, [gpk], mpk, vpk, name="adamw_small")
    for dst, packed in zip((grads, deltas, new_m, new_v), outs):
        for n, val in zip(small, _unpack_rows(packed, shapes)):
            dst[n] = val

    return (loss, grad_x[None], *[grads[n] for n in order], *[deltas[n] for n in order],
            *[new_m[n] for n in order], *[new_v[n] for n in order])


def _local_grads(x0, target, wf):
    a_norm, a_conv_b, a_b_r, a_b_i, a_lambda = (wf[n] for n in ("a_norm", "a_conv_b", "a_b_r", "a_b_i", "a_lambda"))
    kv_norm, final_norm = wf["kv_norm"], wf["final_norm"]

    (h_a,) = rms_fwd(x0, [a_norm], name="norm_a")
    proj_a = matmul(h_a, wf["a_w_in"], name="a_in")
    m_a, hst = lru_fwd(proj_a, wf["a_conv_w"], a_conv_b, wf["a_w_r"], a_b_r, wf["a_w_i"], a_b_i, a_lambda,
                       name="lru_fwd")
    x1 = matmul(m_a, wf["a_w_out"], residual=x0, name="a_out")
    kvn, hb = rms_fwd(x1, [kv_norm, wf["b_norm"]], name="norm_kv_b")
    kv = matmul(kvn, wf["w_kv"], out_dtype=BF16, name="kv_proj")
    proj_b = matmul(hb, wf["b_w_in"], name="b_in")
    m_b, o, offs = attn_fwd(proj_b, kv, name="attn_fwd")
    x2 = matmul(m_b, wf["b_w_out"], residual=x1, name="b_out")
    loss_part, g_final, dx2, dx2b = loss_bwd(x2, target, final_norm, name="loss_bwd")

    dm_b = matmul(dx2b, wf["b_w_out"], tb=True, name="b_out_dx")
    g_b_w_out = matmul(m_b, dx2b, ta=True, out_dtype=BF16, name="b_out_dw")
    do, dgate_b = gate_bwd(dm_b, o, proj_b, name="gate_bwd")
    dq, dk, dv = attn_bwd(proj_b, do, kv, offs, name="attn_bwd")
    dproj_b = jnp.concatenate([dq, dgate_b], axis=1)
    dkv = jnp.concatenate([dk, dv], axis=1)
    dhb = matmul(dproj_b, wf["b_w_in"], tb=True, name="b_in_dx")
    g_b_w_in = matmul(hb, dproj_b, ta=True, out_dtype=BF16, name="b_in_dw")
    dkvn = matmul(dkv, wf["w_kv"], tb=True, name="kv_dx")
    g_w_kv = matmul(kvn, dkv, ta=True, out_dtype=BF16, name="kv_dw")
    dx1, dx1b, (g_kv_norm, g_b_norm) = rms_bwd(
        x1, dx2, [(kv_norm, dkvn), (wf["b_norm"], dhb)], name="norm_kv_b_bwd")

    dm_a = matmul(dx1b, wf["a_w_out"], tb=True, name="a_out_dx")
    g_a_w_out = matmul(m_a, dx1b, ta=True, out_dtype=BF16, name="a_out_dw")
    dxpre, dgate_a, g_conv_w, g_conv_b, g_w_r, g_b_r, g_w_i, g_b_i, g_lambda = lru_bwd(
        proj_a, hst, dm_a, wf["a_conv_w"], a_conv_b, wf["a_w_r"], a_b_r, wf["a_w_i"], a_b_i, a_lambda,
        name="lru_bwd")
    dproj_a = jnp.concatenate([dxpre, dgate_a], axis=1)
    dh_a = matmul(dproj_a, wf["a_w_in"], tb=True, name="a_in_dx")
    g_a_w_in = matmul(h_a, dproj_a, ta=True, out_dtype=BF16, name="a_in_dw")
    grad_x, _, (g_a_norm,) = rms_bwd(x0, dx1, [(a_norm, dh_a)], name="norm_a_bwd")

    gbig = dict(a_w_in=g_a_w_in, a_w_r=g_w_r.astype(BF16), a_w_i=g_w_i.astype(BF16), a_w_out=g_a_w_out,
                w_kv=g_w_kv, b_w_in=g_b_w_in, b_w_out=g_b_w_out)
    gsmall = dict(a_norm=g_a_norm, a_conv_b=g_conv_b, a_b_r=g_b_r, a_b_i=g_b_i, a_lambda=g_lambda,
                  kv_norm=g_kv_norm, final_norm=g_final, a_conv_w=g_conv_w, b_norm=g_b_norm)
    return loss_part, grad_x, gbig, gsmall
```

```python
import math

import jax
import jax.numpy as jnp
from jax import lax
from jax.experimental import pallas as pl
from jax.experimental.pallas import tpu as pltpu

F32 = jnp.float32
BF16 = jnp.bfloat16
MESH = pl.DeviceIdType.MESH

EPS = 1e-6
LRU_C = 8.0
CONV_W = 4
HEAD_DIM = 128
ADAM_LR = 0.001
ADAM_B1 = 0.9
ADAM_B2 = 0.999
ADAM_EPS = 1e-08
ADAM_WD = 0.01
ADAM_STEP = 10

V7X_VMEM_LIMIT = 56 * 1024 * 1024
LANES = 128
SUBLANES = 8
ATT_BLOCK = 256
ATT_QTILES = 2
ATT_HEADS_FWD = 2
ATT_HEADS_BWD = 2
LOG2E = 1.4426950408889634


def _pick(dim, cands):
    for c in cands:
        if dim % c == 0:
            return c
    return dim


def _params(sem, vmem=V7X_VMEM_LIMIT):
    return pltpu.CompilerParams(dimension_semantics=sem, vmem_limit_bytes=vmem)


def _sigmoid(x):
    return 1.0 / (1.0 + jnp.exp(-x))


def matmul(a, b, *, ta=False, tb=False, out_dtype=F32, residual=None, name):
    m = a.shape[1] if ta else a.shape[0]
    kdim = a.shape[0] if ta else a.shape[1]
    n = b.shape[0] if tb else b.shape[1]
    assert (b.shape[1] if tb else b.shape[0]) == kdim
    tm = _pick(m, (1024, 640, 512, 256, 128))
    tn = _pick(n, (1024, 640, 512, 256, 128))
    tk = _pick(kdim, (2560, 2048, 1024, 512, 256, 128))
    nk = kdim // tk
    dn = (((0 if ta else 1,), (1 if tb else 0,)), ((), ()))

    def body(*refs):
        if residual is None:
            a_ref, b_ref, o_ref, acc = refs
            r_ref = None
        else:
            a_ref, b_ref, r_ref, o_ref, acc = refs
        k = pl.program_id(2)

        @pl.when(k == 0)
        def _():
            acc[...] = jnp.zeros_like(acc)

        acc[...] += lax.dot_general(a_ref[...].astype(BF16), b_ref[...].astype(BF16), dn,
                                    preferred_element_type=F32)

        @pl.when(k == nk - 1)
        def _():
            r = acc[...]
            if r_ref is not None:
                r = r + r_ref[...]
            o_ref[...] = r.astype(out_dtype)

    a_spec = (pl.BlockSpec((tk, tm), lambda i, j, k: (k, i)) if ta
              else pl.BlockSpec((tm, tk), lambda i, j, k: (i, k)))
    b_spec = (pl.BlockSpec((tn, tk), lambda i, j, k: (j, k)) if tb
              else pl.BlockSpec((tk, tn), lambda i, j, k: (k, j)))
    o_spec = pl.BlockSpec((tm, tn), lambda i, j, k: (i, j))
    in_specs = [a_spec, b_spec]
    args = [a, b]
    if residual is not None:
        in_specs.append(o_spec)
        args.append(residual)
    return pl.pallas_call(
        body, name=name, grid=(m // tm, n // tn, nk),
        in_specs=in_specs, out_specs=o_spec,
        out_shape=jax.ShapeDtypeStruct((m, n), out_dtype),
        scratch_shapes=[pltpu.VMEM((tm, tn), F32)],
        compiler_params=_params(("parallel", "parallel", "arbitrary")),
    )(*args)


def rms_fwd(x, gains, *, name):
    s, d = x.shape
    tr = _pick(s, (512, 256, 128, 8))
    ng = len(gains)

    def body(*refs):
        x_ref = refs[0]
        g_refs = refs[1:1 + ng]
        o_refs = refs[1 + ng:]
        xv = x_ref[...]
        y = xv * lax.rsqrt(jnp.mean(xv * xv, axis=-1, keepdims=True) + EPS)
        for g_ref, o_ref in zip(g_refs, o_refs):
            o_ref[...] = (y * g_ref[...]).astype(BF16)

    row = pl.BlockSpec((tr, d), lambda i: (i, 0))
    vec = pl.BlockSpec((1, d), lambda i: (0, 0))
    return pl.pallas_call(
        body, name=name, grid=(s // tr,),
        in_specs=[row] + [vec] * ng, out_specs=[row] * ng,
        out_shape=[jax.ShapeDtypeStruct((s, d), BF16)] * ng,
        compiler_params=_params(("parallel",)),
    )(x, *gains)


def rms_bwd(x, dres, norms, *, name):
    s, d = x.shape
    tr = _pick(s, (256, 128, 8))
    ng = len(norms)

    def body(*refs):
        x_ref, dres_ref = refs[0], refs[1]
        g_refs = refs[2:2 + ng]
        dh_refs = refs[2 + ng:2 + 2 * ng]
        dx_ref, dxb_ref = refs[2 + 2 * ng], refs[3 + 2 * ng]
        dg_refs = refs[4 + 2 * ng:]
        i = pl.program_id(0)
        xv = x_ref[...]
        r = lax.rsqrt(jnp.mean(xv * xv, axis=-1, keepdims=True) + EPS)
        xhat = xv * r
        dx = dres_ref[...]
        for g_ref, dh_ref, dg_ref in zip(g_refs, dh_refs, dg_refs):
            dh = dh_ref[...]
            part = jnp.sum(dh * xhat, axis=0, keepdims=True)

            @pl.when(i == 0)
            def _():
                dg_ref[...] = part

            @pl.when(i > 0)
            def _():
                dg_ref[...] += part

            dxhat = dh * g_ref[...]
            dx = dx + r * (dxhat - xhat * jnp.mean(dxhat * xhat, axis=-1, keepdims=True))
        dx_ref[...] = dx
        dxb_ref[...] = dx.astype(BF16)

    row = pl.BlockSpec((tr, d), lambda i: (i, 0))
    vec = pl.BlockSpec((1, d), lambda i: (0, 0))
    outs = pl.pallas_call(
        body, name=name, grid=(s // tr,),
        in_specs=[row, row] + [vec] * ng + [row] * ng,
        out_specs=[row, row] + [vec] * ng,
        out_shape=[jax.ShapeDtypeStruct((s, d), F32), jax.ShapeDtypeStruct((s, d), BF16)]
        + [jax.ShapeDtypeStruct((1, d), F32)] * ng,
        compiler_params=_params(("arbitrary",)),
    )(x, dres, *[g for g, _ in norms], *[dh for _, dh in norms])
    return outs[0], outs[1], list(outs[2:])


def loss_bwd(x2, target, gain, *, name):
    s, d = x2.shape
    tr = _pick(s, (256, 128, 8))
    nsteps = s // tr

    def body(x_ref, t_ref, g_ref, loss_ref, dg_ref, dx_ref, dxb_ref, sq_acc):
        i = pl.program_id(0)
        xv = x_ref[...]
        r = lax.rsqrt(jnp.mean(xv * xv, axis=-1, keepdims=True) + EPS)
        xhat = xv * r
        g = g_ref[...]
        err = xhat * g - t_ref[...]
        dy = err * (1.0 / d)
        sq = jnp.sum(err * err, axis=0, keepdims=True)
        dgp = jnp.sum(dy * xhat, axis=0, keepdims=True)

        @pl.when(i == 0)
        def _():
            sq_acc[...] = sq
            dg_ref[...] = dgp

        @pl.when(i > 0)
        def _():
            sq_acc[...] += sq
            dg_ref[...] += dgp

        dxhat = dy * g
        dx = r * (dxhat - xhat * jnp.mean(dxhat * xhat, axis=-1, keepdims=True))
        dx_ref[...] = dx
        dxb_ref[...] = dx.astype(BF16)

        @pl.when(i == nsteps - 1)
        def _():
            tot = jnp.sum(sq_acc[...], axis=-1, keepdims=True) * (0.5 / d)
            loss_ref[...] = jnp.broadcast_to(tot, (1, LANES))

    row = pl.BlockSpec((tr, d), lambda i: (i, 0))
    vec = pl.BlockSpec((1, d), lambda i: (0, 0))
    return pl.pallas_call(
        body, name=name, grid=(nsteps,),
        in_specs=[row, row, vec],
        out_specs=[pl.BlockSpec((1, LANES), lambda i: (0, 0)), vec, row, row],
        out_shape=[jax.ShapeDtypeStruct((1, LANES), F32), jax.ShapeDtypeStruct((1, d), F32),
                   jax.ShapeDtypeStruct((s, d), F32), jax.ShapeDtypeStruct((s, d), BF16)],
        scratch_shapes=[pltpu.VMEM((1, d), F32)],
        compiler_params=_params(("arbitrary",)),
    )(x2, target, gain)


def _lru_gates(xb, wr, wi, br, bi, sp):
    xbb = xb.astype(BF16)
    r = _sigmoid(jnp.dot(xbb, wr, preferred_element_type=F32) + br)
    ig = _sigmoid(jnp.dot(xbb, wi, preferred_element_type=F32) + bi)
    log_a = (-LRU_C) * r * sp
    a = jnp.exp(log_a)
    mult = jnp.sqrt(jnp.maximum(-jnp.tanh(log_a) * (a * a + 1.0), 0.0))
    return r, ig, a, mult


def _softplus_neg(lam):
    e = jnp.exp(-jnp.abs(lam))
    sp = jnp.maximum(-lam, 0.0) + jnp.log(1.0 + e)
    sg = jnp.where(lam >= 0, e, 1.0) / (1.0 + e)
    return sp, sg


def _conv(pad_ref, w, b, t):
    acc = b + w[CONV_W - 1:CONV_W, :] * pad_ref[pl.ds(SUBLANES, t), :]
    for dlt in range(1, CONV_W):
        acc = acc + w[CONV_W - 1 - dlt:CONV_W - dlt, :] * pad_ref[pl.ds(SUBLANES - dlt, t), :]
    return acc


def _lru_specs(t, bw, nb, time_of):
    blk = lambda c0: pl.BlockSpec((t, bw), lambda n, i, c0=c0: (time_of(i), c0 + n))
    vec = pl.BlockSpec((1, bw), lambda n, i: (0, n))
    wspec = pl.BlockSpec((None, bw, bw), lambda n, i: (n, 0, 0))
    cwspec = pl.BlockSpec((CONV_W, bw), lambda n, i: (0, n))
    return blk, vec, wspec, cwspec


def lru_fwd(proj, conv_w, conv_b, w_r, b_r, w_i, b_i, lam, *, name):
    s, r2 = proj.shape
    rr = r2 // 2
    nb, bw, _ = w_r.shape
    t = _pick(s, (512, 256, 128, 64, 8))
    ngroups = t // SUBLANES

    def body(xp_ref, gate_ref, cw_ref, cb_ref, wr_ref, br_ref, wi_ref, bi_ref, lam_ref,
             m_ref, h_ref, pad, hcarry, a_scr, u_scr):
        i = pl.program_id(1)

        @pl.when(i == 0)
        def _():
            pad[0:SUBLANES, :] = jnp.zeros((SUBLANES, bw), F32)
            hcarry[...] = jnp.zeros_like(hcarry)

        xpre = xp_ref[...]
        pad[pl.ds(SUBLANES, t), :] = xpre
        xb = _conv(pad, cw_ref[...], cb_ref[...], t)
        pad[0:SUBLANES, :] = xpre[t - SUBLANES:, :]
        sp, _ = _softplus_neg(lam_ref[...])
        _, ig, a, mult = _lru_gates(xb, wr_ref[...], wi_ref[...], br_ref[...], bi_ref[...], sp)
        a_scr[...] = a
        u_scr[...] = mult * (ig * xb)
        row = lax.broadcasted_iota(jnp.int32, (SUBLANES, bw), 0)

        def group(gi, hprev):
            off = pl.multiple_of(gi * SUBLANES, SUBLANES)
            av = a_scr[pl.ds(off, SUBLANES), :]
            uv = u_scr[pl.ds(off, SUBLANES), :]
            for dlt in (1, 2, 4):
                keep = row >= dlt
                uv = jnp.where(keep, av * pltpu.roll(uv, dlt, 0) + uv, uv)
                av = jnp.where(keep, av * pltpu.roll(av, dlt, 0), av)
            hv = av * hprev + uv
            h_ref[pl.ds(off, SUBLANES), :] = hv
            return hv[SUBLANES - 1:SUBLANES, :]

        hcarry[...] = lax.fori_loop(0, ngroups, group, hcarry[...])
        gate = gate_ref[...]
        m_ref[...] = (h_ref[...] * (gate * _sigmoid(gate))).astype(BF16)

    blk, vec, wspec, cwspec = _lru_specs(t, bw, nb, lambda i: i)
    return pl.pallas_call(
        body, name=name, grid=(nb, s // t),
        in_specs=[blk(0), blk(nb), cwspec, vec, wspec, vec, wspec, vec, vec],
        out_specs=[blk(0), blk(0)],
        out_shape=[jax.ShapeDtypeStruct((s, rr), BF16), jax.ShapeDtypeStruct((s, rr), F32)],
        scratch_shapes=[pltpu.VMEM((t + SUBLANES, bw), F32), pltpu.VMEM((1, bw), F32),
                        pltpu.VMEM((t, bw), F32), pltpu.VMEM((t, bw), F32)],
        compiler_params=_params(("parallel", "arbitrary")),
    )(proj, proj, conv_w, conv_b, w_r, b_r, w_i, b_i, lam)


def lru_bwd(proj, hst, dm, conv_w, conv_b, w_r, b_r, w_i, b_i, lam, *, name):
    s, r2 = proj.shape
    rr = r2 // 2
    nb, bw, _ = w_r.shape
    t = _pick(s, (512, 256, 128, 64, 8))
    nt = s // t
    ngroups = t // SUBLANES
    nt_dims = (((1,), (1,)), ((), ()))
    tn_dims = (((0,), (0,)), ((), ()))

    def body(xp_ref, xhalo_ref, gate_ref, h_ref, hhalo_ref, dm_ref, cw_ref, cb_ref, wr_ref, br_ref, wi_ref,
             bi_ref, lam_ref,
             dxp_ref, dgate_ref, dcw_ref, dcb_ref, dwr_ref, dbr_ref, dwi_ref, dbi_ref, dlam_ref,
             pad, hpad, dpad, ecarry, a_scr, b_scr, d_scr):
        step = pl.program_id(1)

        @pl.when(step == 0)
        def _():
            dpad[pl.ds(t, SUBLANES), :] = jnp.zeros((SUBLANES, bw), F32)
            ecarry[...] = jnp.zeros_like(ecarry)
            dcw_ref[...] = jnp.zeros_like(dcw_ref)
            dcb_ref[...] = jnp.zeros_like(dcb_ref)
            dwr_ref[...] = jnp.zeros_like(dwr_ref)
            dbr_ref[...] = jnp.zeros_like(dbr_ref)
            dwi_ref[...] = jnp.zeros_like(dwi_ref)
            dbi_ref[...] = jnp.zeros_like(dbi_ref)
            dlam_ref[...] = jnp.zeros_like(dlam_ref)

        past = jnp.where(step == nt - 1, 0.0, 1.0)
        pad[0:SUBLANES, :] = xhalo_ref[...] * past
        pad[pl.ds(SUBLANES, t), :] = xp_ref[...]
        hpad[0:SUBLANES, :] = hhalo_ref[...] * past
        hpad[pl.ds(SUBLANES, t), :] = h_ref[...]
        cw = cw_ref[...]
        xb = _conv(pad, cw, cb_ref[...], t)
        sp, sg = _softplus_neg(lam_ref[...])
        wr = wr_ref[...]
        wi = wi_ref[...]
        r, ig, a, mult = _lru_gates(xb, wr, wi, br_ref[...], bi_ref[...], sp)
        gate = gate_ref[...]
        sgate = _sigmoid(gate)
        dmv = dm_ref[...]
        dgate_ref[...] = (dmv * h_ref[...] * (sgate * (1.0 + gate * (1.0 - sgate)))).astype(BF16)
        dy = dmv * (gate * sgate)
        a_scr[...] = a
        b_scr[...] = a * dy
        row = lax.broadcasted_iota(jnp.int32, (SUBLANES, bw), 0)

        def group(gi, enext):
            off = pl.multiple_of((ngroups - 1 - gi) * SUBLANES, SUBLANES)
            av = a_scr[pl.ds(off, SUBLANES), :]
            bv = b_scr[pl.ds(off, SUBLANES), :]
            for dlt in (1, 2, 4):
                keep = row < SUBLANES - dlt
                bv = jnp.where(keep, av * pltpu.roll(bv, SUBLANES - dlt, 0) + bv, bv)
                av = jnp.where(keep, av * pltpu.roll(av, SUBLANES - dlt, 0), av)
            ev = av * enext + bv
            d_scr[pl.ds(off, SUBLANES), :] = jnp.where(row == SUBLANES - 1, enext,
                                                       pltpu.roll(ev, SUBLANES - 1, 0))
            return ev[0:1, :]

        ecarry[...] = lax.fori_loop(0, ngroups, group, ecarry[...])
        dtot = dy + d_scr[...]
        da = dtot * hpad[pl.ds(SUBLANES - 1, t), :]
        dmult = dtot * (ig * xb)
        dlog_a = da * a - dmult * (a * a) / mult
        dr_pre = dlog_a * ((-LRU_C) * sp) * (r * (1.0 - r))
        di_pre = (dtot * mult * xb) * (ig * (1.0 - ig))
        dlam_ref[...] += jnp.sum(dlog_a * r, axis=0, keepdims=True) * (LRU_C * sg)
        dbr_ref[...] += jnp.sum(dr_pre, axis=0, keepdims=True)
        dbi_ref[...] += jnp.sum(di_pre, axis=0, keepdims=True)
        drb = dr_pre.astype(BF16)
        dib = di_pre.astype(BF16)
        xbb = xb.astype(BF16)
        dxb = (dtot * mult * ig
               + lax.dot_general(drb, wr, nt_dims, preferred_element_type=F32)
               + lax.dot_general(dib, wi, nt_dims, preferred_element_type=F32))
        dwr_ref[...] += lax.dot_general(xbb, drb, tn_dims, preferred_element_type=F32)
        dwi_ref[...] += lax.dot_general(xbb, dib, tn_dims, preferred_element_type=F32)
        dcb_ref[...] += jnp.sum(dxb, axis=0, keepdims=True)
        dpad[pl.ds(0, t), :] = dxb
        dxpre = cw[CONV_W - 1:CONV_W, :] * dxb
        dcw_ref[CONV_W - 1:CONV_W, :] += jnp.sum(dxb * pad[pl.ds(SUBLANES, t), :], axis=0, keepdims=True)
        for dlt in range(1, CONV_W):
            dxpre = dxpre + cw[CONV_W - 1 - dlt:CONV_W - dlt, :] * dpad[pl.ds(dlt, t), :]
            dcw_ref[CONV_W - 1 - dlt:CONV_W - dlt, :] += jnp.sum(
                dxb * pad[pl.ds(SUBLANES - dlt, t), :], axis=0, keepdims=True)
        dpad[pl.ds(t, SUBLANES), :] = dxb[0:SUBLANES, :]
        dxp_ref[...] = dxpre.astype(BF16)

    rev = lambda i: nt - 1 - i
    blk, vec, wspec, cwspec = _lru_specs(t, bw, nb, rev)
    halo = pl.BlockSpec((SUBLANES, bw), lambda n, i: (jnp.maximum(rev(i) * ngroups - 1, 0), n))
    return pl.pallas_call(
        body, name=name, grid=(nb, nt),
        in_specs=[blk(0), halo, blk(nb), blk(0), halo, blk(0), cwspec, vec, wspec, vec, wspec, vec, vec],
        out_specs=[blk(0), blk(0), cwspec, vec, wspec, vec, wspec, vec, vec],
        out_shape=[jax.ShapeDtypeStruct((s, rr), BF16), jax.ShapeDtypeStruct((s, rr), BF16),
                   jax.ShapeDtypeStruct((CONV_W, rr), F32), jax.ShapeDtypeStruct((1, rr), F32),
                   jax.ShapeDtypeStruct((nb, bw, bw), F32), jax.ShapeDtypeStruct((1, rr), F32),
                   jax.ShapeDtypeStruct((nb, bw, bw), F32), jax.ShapeDtypeStruct((1, rr), F32),
                   jax.ShapeDtypeStruct((1, rr), F32)],
        scratch_shapes=[pltpu.VMEM((t + SUBLANES, bw), F32), pltpu.VMEM((t + SUBLANES, bw), F32),
                        pltpu.VMEM((t + SUBLANES, bw), F32), pltpu.VMEM((1, bw), F32),
                        pltpu.VMEM((t, bw), F32), pltpu.VMEM((t, bw), F32), pltpu.VMEM((t, bw), F32)],
        compiler_params=_params(("parallel", "arbitrary")),
    )(proj, proj, proj, hst, hst, dm, conv_w, conv_b, w_r, b_r, w_i, b_i, lam)


def _split_bf16(x, axis):
    hi = x.astype(BF16)
    lo = (x - hi.astype(F32)).astype(BF16)
    return jnp.concatenate([hi, lo], axis=axis)


def _softplus(z):
    return jnp.maximum(z, 0.0) + jnp.log(1.0 + jnp.exp2(jnp.abs(z) * (-LOG2E)))


def _att_blocks(s):
    bk = ATT_BLOCK if s % ATT_BLOCK == 0 else s
    bq = ATT_QTILES * bk if s % (ATT_QTILES * bk) == 0 else bk
    return bk, bq


def attn_fwd(projb, kv, *, name):
    s, a2 = projb.shape
    a = a2 // 2
    nh = a // HEAD_DIM
    bk, bq = _att_blocks(s)
    r = bq // bk
    nq = s // bq
    assert s // bk <= LANES
    scale = 1.0 / math.sqrt(HEAD_DIM)
    nt_dims = (((1,), (1,)), ((), ()))
    hp = ATT_HEADS_FWD if nh % ATT_HEADS_FWD == 0 else 1
    wd = hp * HEAD_DIM

    def body(q_ref, g_ref, k_ref, v_ref, m_ref, o_ref, off_ref, acc):
        i = pl.program_id(1)
        qb = q_ref[...].astype(BF16)
        from_mat = (lax.broadcasted_iota(jnp.int32, (bk, bk), 0)
                    >= lax.broadcasted_iota(jnp.int32, (bk, bk), 1)).astype(BF16)
        from_mat2 = jnp.concatenate([from_mat, from_mat], axis=0)
        rowi = lax.broadcasted_iota(jnp.int32, (bq, bk), 0)
        coli = lax.broadcasted_iota(jnp.int32, (bq, bk), 1)
        lane = lax.broadcasted_iota(jnp.int32, (bq, LANES), 1)
        cols = [slice(hh * HEAD_DIM, (hh + 1) * HEAD_DIM) for hh in range(hp)]

        def tile(j, carries, diag):
            causal = None if diag is None else (coli + diag * bk) < rowi
            rows = pl.ds(pl.multiple_of(j * bk, bk), bk)
            zraw = [lax.dot_general(qb[:, c], k_ref[rows, c], nt_dims, preferred_element_type=F32) for c in cols]
            zs, sps, sums = [], [], []
            for zr in zraw:
                z = zr * scale
                sp = _softplus(z)
                if causal is not None:
                    sp = jnp.where(causal, sp, 0.0)
                sums.append(jnp.dot(_split_bf16(sp, 1), from_mat2, preferred_element_type=F32))
                zs.append(z)
                sps.append(sp)
            out = []
            for hh in range(hp):
                w = jnp.exp(zs[hh] - sums[hh] - carries[hh])
                if causal is not None:
                    w = jnp.where(causal, w, 0.0)
                acc[:, cols[hh]] += jnp.dot(w.astype(BF16), v_ref[rows, cols[hh]], preferred_element_type=F32)
                off_ref[hh] = jnp.where(lane == j, carries[hh], off_ref[hh])
                out.append(carries[hh] + jnp.sum(sps[hh], axis=1, keepdims=True))
            return tuple(out)

        acc[...] = jnp.zeros_like(acc)
        off_ref[...] = jnp.zeros_like(off_ref)
        carries = tuple(jnp.zeros((bq, 1), F32) for _ in range(hp))
        for dg in reversed(range(r)):
            carries = tile(r * i + dg, carries, dg)
        lax.fori_loop(0, r * i, lambda jj, c: tile(r * i - 1 - jj, c, None), carries)
        o = acc[...]
        o_ref[...] = o
        gate = g_ref[...]
        m_ref[...] = (o * (gate * _sigmoid(gate))).astype(BF16)

    ng = nh // hp
    qspec = lambda c0: pl.BlockSpec((bq, wd), lambda h, i, c0=c0: (i, c0 + h))
    kspec = lambda c0: pl.BlockSpec((s, wd), lambda h, i, c0=c0: (0, c0 + h), pipeline_mode=pl.Buffered(1))
    return pl.pallas_call(
        body, name=name, grid=(ng, nq),
        in_specs=[qspec(0), qspec(ng), kspec(0), kspec(ng)],
        out_specs=[qspec(0), qspec(0), pl.BlockSpec((hp, None, bq, LANES), lambda h, i: (h, i, 0, 0))],
        out_shape=[jax.ShapeDtypeStruct((s, a), BF16), jax.ShapeDtypeStruct((s, a), F32),
                   jax.ShapeDtypeStruct((nh, nq, bq, LANES), F32)],
        scratch_shapes=[pltpu.VMEM((bq, wd), F32)],
        compiler_params=_params(("parallel", "arbitrary")),
    )(projb, projb, kv, kv)


def attn_bwd(projb, do, kv, offs, *, name):
    s, a2 = projb.shape
    a = a2 // 2
    nh = a // HEAD_DIM
    bk, bq = _att_blocks(s)
    r = bq // bk
    nq = s // bq
    scale = 1.0 / math.sqrt(HEAD_DIM)
    nt_dims = (((1,), (1,)), ((), ()))
    tn_dims = (((0,), (0,)), ((), ()))
    hp = ATT_HEADS_BWD if nh % ATT_HEADS_BWD == 0 else 1
    wd = hp * HEAD_DIM

    def body(q_ref, do_ref, k_ref, v_ref, off_ref, dq_ref, dk_ref, dv_ref, dk_acc, dv_acc, dq_acc):
        i = pl.program_id(1)

        @pl.when(i == 0)
        def _():
            dk_acc[...] = jnp.zeros_like(dk_acc)
            dv_acc[...] = jnp.zeros_like(dv_acc)

        qb = q_ref[...].astype(BF16)
        dob = do_ref[...]
        ki = lax.broadcasted_iota(jnp.int32, (bk, bk), 0)
        kj = lax.broadcasted_iota(jnp.int32, (bk, bk), 1)
        from_mat = (kj >= ki).astype(BF16)
        upto_mat = (kj <= ki).astype(BF16)
        from_mat2 = jnp.concatenate([from_mat, from_mat], axis=1)
        rowi = lax.broadcasted_iota(jnp.int32, (bk, bq), 0)
        coli = lax.broadcasted_iota(jnp.int32, (bk, bq), 1)
        sub = lax.broadcasted_iota(jnp.int32, (LANES, bq), 0)
        offs_t = [off_ref[hh].T for hh in range(hp)]
        cols = [slice(hh * HEAD_DIM, (hh + 1) * HEAD_DIM) for hh in range(hp)]
        dq_acc[...] = jnp.zeros_like(dq_acc)

        def tile(j, gcarries, diag):
            causal = None if diag is None else (rowi + diag * bk) < coli
            rows = pl.ds(pl.multiple_of(j * bk, bk), bk)
            zraw = [lax.dot_general(k_ref[rows, c], qb[:, c], nt_dims, preferred_element_type=F32) for c in cols]
            dws = [lax.dot_general(v_ref[rows, c], dob[:, c], nt_dims, preferred_element_type=F32) for c in cols]
            zs, sigs, sums = [], [], []
            for zr in zraw:
                z = zr * scale
                sp = _softplus(z)
                sigs.append(jnp.exp(z - sp))
                if causal is not None:
                    sp = jnp.where(causal, sp, 0.0)
                sums.append(jnp.dot(from_mat2, _split_bf16(sp, 0), preferred_element_type=F32))
                zs.append(z)
            gs, totals = [], []
            for hh in range(hp):
                offj = jnp.sum(jnp.where(sub == j, offs_t[hh], 0.0), axis=0, keepdims=True)
                w = jnp.exp(zs[hh] - sums[hh] - offj)
                if causal is not None:
                    w = jnp.where(causal, w, 0.0)
                g = w * dws[hh]
                dv_acc[rows, cols[hh]] += jnp.dot(w.astype(BF16), dob[:, cols[hh]], preferred_element_type=F32)
                totals.append(jnp.dot(upto_mat, g.astype(BF16), preferred_element_type=F32))
                gs.append(g)
            out = []
            for hh in range(hp):
                dz = gs[hh] - (totals[hh] + gcarries[hh]) * sigs[hh]
                if causal is not None:
                    dz = jnp.where(causal, dz, 0.0)
                dzb = (dz * scale).astype(BF16)
                dk_acc[rows, cols[hh]] += jnp.dot(dzb, qb[:, cols[hh]], preferred_element_type=F32)
                dq_acc[:, cols[hh]] += lax.dot_general(dzb, k_ref[rows, cols[hh]], tn_dims,
                                                       preferred_element_type=F32)
                out.append(gcarries[hh] + jnp.sum(gs[hh], axis=0, keepdims=True))
            return tuple(out)

        init = tuple(jnp.zeros((1, bq), F32) for _ in range(hp))
        gcarries = lax.fori_loop(0, r * i, lambda j, c: tile(j, c, None), init)
        for dg in range(r):
            gcarries = tile(r * i + dg, gcarries, dg)
        dq_ref[...] = dq_acc[...].astype(BF16)

        @pl.when(i == nq - 1)
        def _():
            dk_ref[...] = dk_acc[...].astype(BF16)
            dv_ref[...] = dv_acc[...].astype(BF16)

    ng = nh // hp
    once = pl.Buffered(1)
    qspec = lambda c0: pl.BlockSpec((bq, wd), lambda h, i, c0=c0: (i, c0 + h))
    kspec = lambda c0: pl.BlockSpec((s, wd), lambda h, i, c0=c0: (0, c0 + h), pipeline_mode=once)
    return pl.pallas_call(
        body, name=name, grid=(ng, nq),
        in_specs=[qspec(0), qspec(0), kspec(0), kspec(ng),
                  pl.BlockSpec((hp, None, bq, LANES), lambda h, i: (h, i, 0, 0))],
        out_specs=[qspec(0), kspec(0), kspec(0)],
        out_shape=[jax.ShapeDtypeStruct((s, a), BF16)] * 3,
        scratch_shapes=[pltpu.VMEM((s, wd), F32), pltpu.VMEM((s, wd), F32), pltpu.VMEM((bq, wd), F32)],
        compiler_params=_params(("parallel", "arbitrary")),
    )(projb, do, kv, kv, offs)


def gate_bwd(dm, o, projb, *, name):
    s, a = dm.shape
    tr = _pick(s, (512, 256, 128, 8))

    def body(dm_ref, o_ref, g_ref, do_ref, dg_ref):
        gate = g_ref[...]
        sg = _sigmoid(gate)
        dmv = dm_ref[...]
        do_ref[...] = (dmv * (gate * sg)).astype(BF16)
        dg_ref[...] = (dmv * o_ref[...] * (sg * (1.0 + gate * (1.0 - sg)))).astype(BF16)

    row = pl.BlockSpec((tr, a), lambda i: (i, 0))
    return pl.pallas_call(
        body, name=name, grid=(s // tr,),
        in_specs=[row, row, pl.BlockSpec((tr, a), lambda i: (i, 1))],
        out_specs=[row, row],
        out_shape=[jax.ShapeDtypeStruct((s, a), BF16)] * 2,
        compiler_params=_params(("parallel",)),
    )(dm, o, projb)


def _as2d(x):
    n = x.size
    cols = x.shape[-1]
    if cols % LANES != 0:
        cols = LANES
    return x.reshape(n // cols, cols)


def sum_parts(parts, *, name):
    p, rows, cols = parts.shape
    tr = _pick(rows, (512, 256, 128, 64, 32, 16))

    def body(p_ref, o_ref):
        acc = p_ref[0].astype(F32)
        for k in range(1, p):
            acc = acc + p_ref[k].astype(F32)
        o_ref[...] = acc

    return pl.pallas_call(
        body, name=name, grid=(rows // tr,),
        in_specs=[pl.BlockSpec((p, tr, cols), lambda i: (0, i, 0))],
        out_specs=pl.BlockSpec((tr, cols), lambda i: (i, 0)),
        out_shape=jax.ShapeDtypeStruct((rows, cols), F32),
        compiler_params=_params(("parallel",)),
    )(parts)


def adamw(w, g_parts, m, v, *, name):
    rows, cols = w.shape
    tr = _pick(rows, (128, 64, 32, 16, 8))
    np_ = len(g_parts)
    c1 = 1.0 / (1.0 - ADAM_B1 ** ADAM_STEP)
    c2 = 1.0 / (1.0 - ADAM_B2 ** ADAM_STEP)

    def body(*refs):
        w_ref, m_ref, v_ref = refs[0], refs[1], refs[2]
        g_refs = refs[3:3 + np_]
        go_ref, d_ref, mo_ref, vo_ref = refs[3 + np_:]
        g = g_refs[0][...]
        for gr in g_refs[1:]:
            g = g + gr[...]
        mn = ADAM_B1 * m_ref[...] + (1.0 - ADAM_B1) * g
        vn = ADAM_B2 * v_ref[...] + (1.0 - ADAM_B2) * (g * g)
        go_ref[...] = g
        mo_ref[...] = mn
        vo_ref[...] = vn
        d_ref[...] = (-ADAM_LR) * ((mn * c1) / (jnp.sqrt(vn * c2) + ADAM_EPS) + ADAM_WD * w_ref[...])

    spec = pl.BlockSpec((tr, cols), lambda i: (i, 0))
    return pl.pallas_call(
        body, name=name, grid=(rows // tr,),
        in_specs=[spec] * (3 + np_), out_specs=[spec] * 4,
        out_shape=[jax.ShapeDtypeStruct((rows, cols), F32)] * 4,
        compiler_params=_params(("parallel",)),
    )(w, m, v, *g_parts)


def _place():
    return lax.axis_index("x"), lax.axis_index("y"), lax.axis_index("c")


def _chip_peers(x, y, c):
    return [(1 - x, y, c), (x, 1 - y, c), (1 - x, 1 - y, c)]


def _shard_of(ref, axis, idx, n):
    start = pl.multiple_of(idx * n, n)
    sl = [slice(None)] * len(ref.shape)
    sl[axis] = pl.ds(start, n)
    return ref.at[tuple(sl)]


def gather_weights(shards, axes, *, name):
    na = len(shards)
    hbm = pl.BlockSpec(memory_space=pl.ANY)
    full_shapes = []
    for sh, ax in zip(shards, axes):
        shp = list(sh.shape)
        shp[ax] *= 4
        full_shapes.append(jax.ShapeDtypeStruct(tuple(shp), sh.dtype))

    def body(*refs):
        s_refs = refs[:na]
        f_refs = refs[na:2 * na]
        send_sems, recv_sems, local_sems = refs[2 * na:]
        x, y, c = _place()
        me = 2 * x + y
        peers = _chip_peers(x, y, c)
        copies = []
        for ai in range(na):
            n = s_refs[ai].shape[axes[ai]]
            mine = _shard_of(f_refs[ai], axes[ai], me, n)
            loc = pltpu.make_async_copy(s_refs[ai], mine, local_sems.at[ai])
            loc.start()
            copies.append((loc,))
            for k, peer in enumerate(peers):
                theirs = _shard_of(f_refs[ai], axes[ai], 2 * peer[0] + peer[1], n)
                snd = pltpu.make_async_remote_copy(
                    src_ref=s_refs[ai], dst_ref=mine, send_sem=send_sems.at[ai * 3 + k],
                    recv_sem=recv_sems.at[ai * 3 + k], device_id=peer, device_id_type=MESH)
                snd.start()
                rcv = pltpu.make_async_remote_copy(
                    src_ref=s_refs[ai], dst_ref=theirs, send_sem=send_sems.at[ai * 3 + k],
                    recv_sem=recv_sems.at[ai * 3 + k], device_id=peer, device_id_type=MESH)
                copies.append((snd, rcv))
        for cp in copies:
            if len(cp) == 1:
                cp[0].wait()
            else:
                cp[0].wait_send()
                cp[1].wait_recv()

    return pl.pallas_call(
        body, name=name,
        in_specs=[hbm] * na, out_specs=[hbm] * na, out_shape=full_shapes,
        scratch_shapes=[pltpu.SemaphoreType.DMA((3 * na,)), pltpu.SemaphoreType.DMA((3 * na,)),
                        pltpu.SemaphoreType.DMA((na,))],
    )(*shards)


def scatter_grads(grads, axes, *, name):
    na = len(grads)
    hbm = pl.BlockSpec(memory_space=pl.ANY)
    out_shapes = []
    for g, ax in zip(grads, axes):
        shp = list(g.shape)
        shp[ax] //= 4
        out_shapes.append(jax.ShapeDtypeStruct((4, *shp), g.dtype))

    def body(*refs):
        g_refs = refs[:na]
        p_refs = refs[na:2 * na]
        send_sems, recv_sems, local_sems = refs[2 * na:]
        x, y, c = _place()
        me = 2 * x + y
        peers = _chip_peers(x, y, c)
        copies = []
        for ai in range(na):
            n = p_refs[ai].shape[1 + axes[ai]]
            loc = pltpu.make_async_copy(_shard_of(g_refs[ai], axes[ai], me, n), p_refs[ai].at[0], local_sems.at[ai])
            loc.start()
            copies.append(loc)
            for k, peer in enumerate(peers):
                cp = pltpu.make_async_remote_copy(
                    src_ref=_shard_of(g_refs[ai], axes[ai], 2 * peer[0] + peer[1], n), dst_ref=p_refs[ai].at[1 + k],
                    send_sem=send_sems.at[ai * 3 + k], recv_sem=recv_sems.at[ai * 3 + k],
                    device_id=peer, device_id_type=MESH)
                cp.start()
                copies.append(cp)
        for cp in copies:
            cp.wait()

    return pl.pallas_call(
        body, name=name,
        in_specs=[hbm] * na, out_specs=[hbm] * na, out_shape=out_shapes,
        scratch_shapes=[pltpu.SemaphoreType.DMA((3 * na,)), pltpu.SemaphoreType.DMA((3 * na,)),
                        pltpu.SemaphoreType.DMA((na,))],
    )(*grads)


def swap_cores(arrs, *, name):
    na = len(arrs)
    hbm = pl.BlockSpec(memory_space=pl.ANY)

    def body(*refs):
        a_refs = refs[:na]
        o_refs = refs[na:2 * na]
        send_sems, recv_sems = refs[2 * na:]
        x, y, c = _place()
        copies = []
        for ai in range(na):
            cp = pltpu.make_async_remote_copy(
                src_ref=a_refs[ai], dst_ref=o_refs[ai], send_sem=send_sems.at[ai], recv_sem=recv_sems.at[ai],
                device_id=(x, y, 1 - c), device_id_type=MESH)
            cp.start()
            copies.append(cp)
        for cp in copies:
            cp.wait()

    return pl.pallas_call(
        body, name=name,
        in_specs=[hbm] * na, out_specs=[hbm] * na,
        out_shape=[jax.ShapeDtypeStruct(a.shape, a.dtype) for a in arrs],
        scratch_shapes=[pltpu.SemaphoreType.DMA((na,)), pltpu.SemaphoreType.DMA((na,))],
    )(*arrs)


def allreduce_small(buf, *, name):
    rows, cols = buf.shape

    def body(b_ref, o_ref, slots, send_sems, recv_sems):
        x, y, c = _place()
        me = 4 * x + 2 * y + c
        slots[0] = b_ref[...]
        copies = []
        for rel in range(1, 8):
            peer = (x ^ (rel >> 2), y ^ ((rel >> 1) & 1), c ^ (rel & 1))
            cp = pltpu.make_async_remote_copy(
                src_ref=b_ref, dst_ref=slots.at[rel], send_sem=send_sems.at[rel - 1],
                recv_sem=recv_sems.at[rel - 1], device_id=peer, device_id_type=MESH)
            cp.start()
            copies.append(cp)
        for cp in copies:
            cp.wait()
        acc = slots[me]
        for dev in range(1, 8):
            acc = acc + slots[dev ^ me]
        o_ref[...] = acc

    vm = pl.BlockSpec(memory_space=pltpu.VMEM)
    return pl.pallas_call(
        body, name=name, in_specs=[vm], out_specs=vm,
        out_shape=jax.ShapeDtypeStruct((rows, cols), F32),
        scratch_shapes=[pltpu.VMEM((8, rows, cols), F32), pltpu.SemaphoreType.DMA((7,)),
                        pltpu.SemaphoreType.DMA((7,))],
    )(buf)


def _pack_rows(arrs):
    parts = []
    for a in arrs:
        p = a.reshape(-1, LANES)
        parts.append(jnp.pad(p, ((0, (-p.shape[0]) % SUBLANES), (0, 0))))
    return jnp.concatenate(parts, axis=0)


def _unpack_rows(buf, shapes):
    out, r0 = [], 0
    for shp in shapes:
        n = math.prod(shp) // LANES
        out.append(buf[r0:r0 + n].reshape(shp))
        r0 += n + (-n) % SUBLANES
    return out


def kernel(x, a_norm, a_w_in, a_conv_w, a_conv_b, a_w_r, a_b_r, a_w_i, a_b_i, a_lambda, a_w_out, kv_norm, w_kv, b_norm, b_w_in, b_w_out, final_norm, loss_target, m_a_norm, m_a_w_in, m_a_conv_w, m_a_conv_b, m_a_w_r, m_a_b_r, m_a_w_i, m_a_b_i, m_a_lambda, m_a_w_out, m_kv_norm, m_w_kv, m_b_norm, m_b_w_in, m_b_w_out, m_final_norm, v_a_norm, v_a_w_in, v_a_conv_w, v_a_conv_b, v_a_w_r, v_a_b_r, v_a_w_i, v_a_b_i, v_a_lambda, v_a_w_out, v_kv_norm, v_w_kv, v_b_norm, v_b_w_in, v_b_w_out, v_final_norm):
    weights = dict(a_norm=a_norm, a_w_in=a_w_in, a_conv_w=a_conv_w, a_conv_b=a_conv_b, a_w_r=a_w_r, a_b_r=a_b_r,
                   a_w_i=a_w_i, a_b_i=a_b_i, a_lambda=a_lambda, a_w_out=a_w_out, kv_norm=kv_norm, w_kv=w_kv,
                   b_norm=b_norm, b_w_in=b_w_in, b_w_out=b_w_out, final_norm=final_norm)
    mom1 = dict(a_norm=m_a_norm, a_w_in=m_a_w_in, a_conv_w=m_a_conv_w, a_conv_b=m_a_conv_b, a_w_r=m_a_w_r,
                a_b_r=m_a_b_r, a_w_i=m_a_w_i, a_b_i=m_a_b_i, a_lambda=m_a_lambda, a_w_out=m_a_w_out,
                kv_norm=m_kv_norm, w_kv=m_w_kv, b_norm=m_b_norm, b_w_in=m_b_w_in, b_w_out=m_b_w_out,
                final_norm=m_final_norm)
    mom2 = dict(a_norm=v_a_norm, a_w_in=v_a_w_in, a_conv_w=v_a_conv_w, a_conv_b=v_a_conv_b, a_w_r=v_a_w_r,
                a_b_r=v_a_b_r, a_w_i=v_a_w_i, a_b_i=v_a_b_i, a_lambda=v_a_lambda, a_w_out=v_a_w_out,
                kv_norm=v_kv_norm, w_kv=v_w_kv, b_norm=v_b_norm, b_w_in=v_b_w_in, b_w_out=v_b_w_out,
                final_norm=v_final_norm)
    order = list(weights)
    x0 = x[0]
    target = loss_target[0]
    d = x0.shape[1]
    chip = 2 * lax.axis_index("x") + lax.axis_index("y")

    big = ["a_w_in", "a_w_r", "a_w_i", "a_w_out", "w_kv", "b_w_in", "b_w_out"]
    big_axis = dict(a_w_in=1, a_w_r=1, a_w_i=1, a_w_out=0, w_kv=1, b_w_in=1, b_w_out=0)
    local = dict(a_w_in=a_w_in[0], a_w_r=a_w_r[0], a_w_i=a_w_i[0], a_w_out=a_w_out[0], w_kv=w_kv,
                 b_w_in=b_w_in[0], b_w_out=b_w_out[0])
    shards = [local[n].astype(BF16) for n in big] + [a_conv_w[0], b_norm]
    full = gather_weights(shards, [big_axis[n] for n in big] + [1, 1], name="gather_weights")
    wf = dict(zip(big + ["a_conv_w", "b_norm"], full))
    wf.update(a_norm=a_norm, a_conv_b=a_conv_b, a_b_r=a_b_r, a_b_i=a_b_i, a_lambda=a_lambda,
              kv_norm=kv_norm.reshape(1, d), final_norm=final_norm.reshape(1, d))
    loss_part, grad_x, gbig, gsmall = _local_grads(x0, target, wf)

    parts = scatter_grads([gbig[n] for n in big], [big_axis[n] for n in big], name="scatter_grads")
    sums = [sum_parts(p.reshape(4, *_as2d(p[0]).shape), name="sum_" + n) for n, p in zip(big, parts)]
    others = swap_cores(sums, name="swap_cores")

    small = ["a_norm", "a_conv_b", "a_b_r", "a_b_i", "a_lambda", "kv_norm", "final_norm", "a_conv_w", "b_norm"]
    buf = _pack_rows([gsmall[n] for n in small] + [loss_part])
    red = allreduce_small(buf, name="allreduce_small")
    red_list = _unpack_rows(red, [gsmall[n].shape for n in small] + [(1, LANES)])
    gs = dict(zip(small, red_list[:-1]))
    loss = red_list[-1][0, 0]
    n_conv = a_conv_w.shape[2]
    gs["a_conv_w"] = lax.dynamic_slice_in_dim(gs["a_conv_w"], chip * n_conv, n_conv, axis=1)
    n_bn = b_norm.shape[1]
    gs["b_norm"] = lax.dynamic_slice_in_dim(gs["b_norm"], chip * n_bn, n_bn, axis=1)

    grads, deltas, new_m, new_v = {}, {}, {}, {}
    for n, s_mine, s_other in zip(big, sums, others):
        shp = weights[n].shape
        g, dlt, mn, vn = adamw(_as2d(weights[n]), [s_mine, s_other], _as2d(mom1[n]), _as2d(mom2[n]),
                               name="adamw_" + n)
        grads[n], deltas[n], new_m[n], new_v[n] = (t.reshape(shp) for t in (g, dlt, mn, vn))
    shapes = [weights[n].shape for n in small]
    wpk, gpk, mpk, vpk = (_pack_rows([src[n] for n in small]) for src in (weights, gs, mom1, mom2))
    outs = adamw(wpk, [gpk], mpk, vpk, name="adamw_small")
    for dst, packed in zip((grads, deltas, new_m, new_v), outs):
        for n, val in zip(small, _unpack_rows(packed, shapes)):
            dst[n] = val

    return (loss, grad_x[None], *[grads[n] for n in order], *[deltas[n] for n in order],
            *[new_m[n] for n in order], *[new_v[n] for n in order])


def _local_grads(x0, target, wf):
    a_norm, a_conv_b, a_b_r, a_b_i, a_lambda = (wf[n] for n in ("a_norm", "a_conv_b", "a_b_r", "a_b_i", "a_lambda"))
    kv_norm, final_norm = wf["kv_norm"], wf["final_norm"]

    (h_a,) = rms_fwd(x0, [a_norm], name="norm_a")
    proj_a = matmul(h_a, wf["a_w_in"], name="a_in")
    m_a, hst = lru_fwd(proj_a, wf["a_conv_w"], a_conv_b, wf["a_w_r"], a_b_r, wf["a_w_i"], a_b_i, a_lambda,
                       name="lru_fwd")
    x1 = matmul(m_a, wf["a_w_out"], residual=x0, name="a_out")
    kvn, hb = rms_fwd(x1, [kv_norm, wf["b_norm"]], name="norm_kv_b")
    kv = matmul(kvn, wf["w_kv"], out_dtype=BF16, name="kv_proj")
    proj_b = matmul(hb, wf["b_w_in"], name="b_in")
    m_b, o, offs = attn_fwd(proj_b, kv, name="attn_fwd")
    x2 = matmul(m_b, wf["b_w_out"], residual=x1, name="b_out")
    loss_part, g_final, dx2, dx2b = loss_bwd(x2, target, final_norm, name="loss_bwd")

    dm_b = matmul(dx2b, wf["b_w_out"], tb=True, name="b_out_dx")
    g_b_w_out = matmul(m_b, dx2b, ta=True, out_dtype=BF16, name="b_out_dw")
    do, dgate_b = gate_bwd(dm_b, o, proj_b, name="gate_bwd")
    dq, dk, dv = attn_bwd(proj_b, do, kv, offs, name="attn_bwd")
    dproj_b = jnp.concatenate([dq, dgate_b], axis=1)
    dkv = jnp.concatenate([dk, dv], axis=1)
    dhb = matmul(dproj_b, wf["b_w_in"], tb=True, name="b_in_dx")
    g_b_w_in = matmul(hb, dproj_b, ta=True, out_dtype=BF16, name="b_in_dw")
    dkvn = matmul(dkv, wf["w_kv"], tb=True, name="kv_dx")
    g_w_kv = matmul(kvn, dkv, ta=True, out_dtype=BF16, name="kv_dw")
    dx1, dx1b, (g_kv_norm, g_b_norm) = rms_bwd(
        x1, dx2, [(kv_norm, dkvn), (wf["b_norm"], dhb)], name="norm_kv_b_bwd")

    dm_a = matmul(dx1b, wf["a_w_out"], tb=True, name="a_out_dx")
    g_a_w_out = matmul(m_a, dx1b, ta=True, out_dtype=BF16, name="a_out_dw")
    dxpre, dgate_a, g_conv_w, g_conv_b, g_w_r, g_b_r, g_w_i, g_b_i, g_lambda = lru_bwd(
        proj_a, hst, dm_a, wf["a_conv_w"], a_conv_b, wf["a_w_r"], a_b_r, wf["a_w_i"], a_b_i, a_lambda,
        name="lru_bwd")
    dproj_a = jnp.concatenate([dxpre, dgate_a], axis=1)
    dh_a = matmul(dproj_a, wf["a_w_in"], tb=True, name="a_in_dx")
    g_a_w_in = matmul(h_a, dproj_a, ta=True, out_dtype=BF16, name="a_in_dw")
    grad_x, _, (g_a_norm,) = rms_bwd(x0, dx1, [(a_norm, dh_a)], name="norm_a_bwd")

    gbig = dict(a_w_in=g_a_w_in, a_w_r=g_w_r.astype(BF16), a_w_i=g_w_i.astype(BF16), a_w_out=g_a_w_out,
                w_kv=g_w_kv, b_w_in=g_b_w_in, b_w_out=g_b_w_out)
    gsmall = dict(a_norm=g_a_norm, a_conv_b=g_conv_b, a_b_r=g_b_r, a_b_i=g_b_i, a_lambda=g_lambda,
                  kv_norm=g_kv_norm, final_norm=g_final, a_conv_w=g_conv_w, b_norm=g_b_norm)
    return loss_part, grad_x, gbig, gsmall
```

```python
import math

import jax
import jax.numpy as jnp
from jax import lax
from jax.experimental import pallas as pl
from jax.experimental.pallas import tpu as pltpu

F32 = jnp.float32
BF16 = jnp.bfloat16
MESH = pl.DeviceIdType.MESH

EPS = 1e-6
LRU_C = 8.0
CONV_W = 4
HEAD_DIM = 128
ADAM_LR = 0.001
ADAM_B1 = 0.9
ADAM_B2 = 0.999
ADAM_EPS = 1e-08
ADAM_WD = 0.01
ADAM_STEP = 10

V7X_VMEM_LIMIT = 56 * 1024 * 1024
LANES = 128
SUBLANES = 8
ATT_BLOCK = 256
ATT_QTILES = 2
ATT_HEADS_FWD = 2
ATT_HEADS_BWD = 2
LOG2E = 1.4426950408889634
SCAN_UNROLL = 4


def _pick(dim, cands):
    for c in cands:
        if dim % c == 0:
            return c
    return dim


def _params(sem, vmem=V7X_VMEM_LIMIT):
    return pltpu.CompilerParams(dimension_semantics=sem, vmem_limit_bytes=vmem)


def _sigmoid(x):
    return 1.0 / (1.0 + jnp.exp(-x))


def matmul(a, b, *, ta=False, tb=False, out_dtype=F32, residual=None, name):
    m = a.shape[1] if ta else a.shape[0]
    kdim = a.shape[0] if ta else a.shape[1]
    n = b.shape[0] if tb else b.shape[1]
    assert (b.shape[1] if tb else b.shape[0]) == kdim
    tm = _pick(m, (1024, 640, 512, 256, 128))
    tn = _pick(n, (1024, 640, 512, 256, 128))
    tk = _pick(kdim, (2560, 2048, 1024, 512, 256, 128))
    nk = kdim // tk
    dn = (((0 if ta else 1,), (1 if tb else 0,)), ((), ()))

    def body(*refs):
        if residual is None:
            a_ref, b_ref, o_ref, acc = refs
            r_ref = None
        else:
            a_ref, b_ref, r_ref, o_ref, acc = refs
        k = pl.program_id(2)

        @pl.when(k == 0)
        def _():
            acc[...] = jnp.zeros_like(acc)

        acc[...] += lax.dot_general(a_ref[...].astype(BF16), b_ref[...].astype(BF16), dn,
                                    preferred_element_type=F32)

        @pl.when(k == nk - 1)
        def _():
            r = acc[...]
            if r_ref is not None:
                r = r + r_ref[...]
            o_ref[...] = r.astype(out_dtype)

    a_spec = (pl.BlockSpec((tk, tm), lambda i, j, k: (k, i)) if ta
              else pl.BlockSpec((tm, tk), lambda i, j, k: (i, k)))
    b_spec = (pl.BlockSpec((tn, tk), lambda i, j, k: (j, k)) if tb
              else pl.BlockSpec((tk, tn), lambda i, j, k: (k, j)))
    o_spec = pl.BlockSpec((tm, tn), lambda i, j, k: (i, j))
    in_specs = [a_spec, b_spec]
    args = [a, b]
    if residual is not None:
        in_specs.append(o_spec)
        args.append(residual)
    return pl.pallas_call(
        body, name=name, grid=(m // tm, n // tn, nk),
        in_specs=in_specs, out_specs=o_spec,
        out_shape=jax.ShapeDtypeStruct((m, n), out_dtype),
        scratch_shapes=[pltpu.VMEM((tm, tn), F32)],
        compiler_params=_params(("parallel", "parallel", "arbitrary")),
    )(*args)


def rms_fwd(x, gains, *, name):
    s, d = x.shape
    tr = _pick(s, (512, 256, 128, 8))
    ng = len(gains)

    def body(*refs):
        x_ref = refs[0]
        g_refs = refs[1:1 + ng]
        o_refs = refs[1 + ng:]
        xv = x_ref[...]
        y = xv * lax.rsqrt(jnp.mean(xv * xv, axis=-1, keepdims=True) + EPS)
        for g_ref, o_ref in zip(g_refs, o_refs):
            o_ref[...] = (y * g_ref[...]).astype(BF16)

    row = pl.BlockSpec((tr, d), lambda i: (i, 0))
    vec = pl.BlockSpec((1, d), lambda i: (0, 0))
    return pl.pallas_call(
        body, name=name, grid=(s // tr,),
        in_specs=[row] + [vec] * ng, out_specs=[row] * ng,
        out_shape=[jax.ShapeDtypeStruct((s, d), BF16)] * ng,
        compiler_params=_params(("parallel",)),
    )(x, *gains)


def rms_bwd(x, dres, norms, *, name):
    s, d = x.shape
    tr = _pick(s, (256, 128, 8))
    ng = len(norms)

    def body(*refs):
        x_ref, dres_ref = refs[0], refs[1]
        g_refs = refs[2:2 + ng]
        dh_refs = refs[2 + ng:2 + 2 * ng]
        dx_ref, dxb_ref = refs[2 + 2 * ng], refs[3 + 2 * ng]
        dg_refs = refs[4 + 2 * ng:]
        i = pl.program_id(0)
        xv = x_ref[...]
        r = lax.rsqrt(jnp.mean(xv * xv, axis=-1, keepdims=True) + EPS)
        xhat = xv * r
        dx = dres_ref[...]
        for g_ref, dh_ref, dg_ref in zip(g_refs, dh_refs, dg_refs):
            dh = dh_ref[...]
            part = jnp.sum(dh * xhat, axis=0, keepdims=True)

            @pl.when(i == 0)
            def _():
                dg_ref[...] = part

            @pl.when(i > 0)
            def _():
                dg_ref[...] += part

            dxhat = dh * g_ref[...]
            dx = dx + r * (dxhat - xhat * jnp.mean(dxhat * xhat, axis=-1, keepdims=True))
        dx_ref[...] = dx
        dxb_ref[...] = dx.astype(BF16)

    row = pl.BlockSpec((tr, d), lambda i: (i, 0))
    vec = pl.BlockSpec((1, d), lambda i: (0, 0))
    outs = pl.pallas_call(
        body, name=name, grid=(s // tr,),
        in_specs=[row, row] + [vec] * ng + [row] * ng,
        out_specs=[row, row] + [vec] * ng,
        out_shape=[jax.ShapeDtypeStruct((s, d), F32), jax.ShapeDtypeStruct((s, d), BF16)]
        + [jax.ShapeDtypeStruct((1, d), F32)] * ng,
        compiler_params=_params(("arbitrary",)),
    )(x, dres, *[g for g, _ in norms], *[dh for _, dh in norms])
    return outs[0], outs[1], list(outs[2:])


def loss_bwd(x2, target, gain, *, name):
    s, d = x2.shape
    tr = _pick(s, (256, 128, 8))
    nsteps = s // tr

    def body(x_ref, t_ref, g_ref, loss_ref, dg_ref, dx_ref, dxb_ref, sq_acc):
        i = pl.program_id(0)
        xv = x_ref[...]
        r = lax.rsqrt(jnp.mean(xv * xv, axis=-1, keepdims=True) + EPS)
        xhat = xv * r
        g = g_ref[...]
        err = xhat * g - t_ref[...]
        dy = err * (1.0 / d)
        sq = jnp.sum(err * err, axis=0, keepdims=True)
        dgp = jnp.sum(dy * xhat, axis=0, keepdims=True)

        @pl.when(i == 0)
        def _():
            sq_acc[...] = sq
            dg_ref[...] = dgp

        @pl.when(i > 0)
        def _():
            sq_acc[...] += sq
            dg_ref[...] += dgp

        dxhat = dy * g
        dx = r * (dxhat - xhat * jnp.mean(dxhat * xhat, axis=-1, keepdims=True))
        dx_ref[...] = dx
        dxb_ref[...] = dx.astype(BF16)

        @pl.when(i == nsteps - 1)
        def _():
            tot = jnp.sum(sq_acc[...], axis=-1, keepdims=True) * (0.5 / d)
            loss_ref[...] = jnp.broadcast_to(tot, (1, LANES))

    row = pl.BlockSpec((tr, d), lambda i: (i, 0))
    vec = pl.BlockSpec((1, d), lambda i: (0, 0))
    return pl.pallas_call(
        body, name=name, grid=(nsteps,),
        in_specs=[row, row, vec],
        out_specs=[pl.BlockSpec((1, LANES), lambda i: (0, 0)), vec, row, row],
        out_shape=[jax.ShapeDtypeStruct((1, LANES), F32), jax.ShapeDtypeStruct((1, d), F32),
                   jax.ShapeDtypeStruct((s, d), F32), jax.ShapeDtypeStruct((s, d), BF16)],
        scratch_shapes=[pltpu.VMEM((1, d), F32)],
        compiler_params=_params(("arbitrary",)),
    )(x2, target, gain)


def _lru_gates(xb, wr, wi, br, bi, sp):
    xbb = xb.astype(BF16)
    r = _sigmoid(jnp.dot(xbb, wr, preferred_element_type=F32) + br)
    ig = _sigmoid(jnp.dot(xbb, wi, preferred_element_type=F32) + bi)
    log_a = (-LRU_C) * r * sp
    a = jnp.exp(log_a)
    mult = jnp.sqrt(jnp.maximum(-jnp.tanh(log_a) * (a * a + 1.0), 0.0))
    return r, ig, a, mult


def _softplus_neg(lam):
    e = jnp.exp(-jnp.abs(lam))
    sp = jnp.maximum(-lam, 0.0) + jnp.log(1.0 + e)
    sg = jnp.where(lam >= 0, e, 1.0) / (1.0 + e)
    return sp, sg


def _conv(pad_ref, w, b, t):
    acc = b + w[CONV_W - 1:CONV_W, :] * pad_ref[pl.ds(SUBLANES, t), :]
    for dlt in range(1, CONV_W):
        acc = acc + w[CONV_W - 1 - dlt:CONV_W - dlt, :] * pad_ref[pl.ds(SUBLANES - dlt, t), :]
    return acc


def _lru_specs(t, bw, nb, time_of):
    blk = lambda c0: pl.BlockSpec((t, bw), lambda n, i, c0=c0: (time_of(i), c0 + n))
    vec = pl.BlockSpec((1, bw), lambda n, i: (0, n))
    wspec = pl.BlockSpec((None, bw, bw), lambda n, i: (n, 0, 0))
    cwspec = pl.BlockSpec((CONV_W, bw), lambda n, i: (0, n))
    return blk, vec, wspec, cwspec


def lru_fwd(proj, conv_w, conv_b, w_r, b_r, w_i, b_i, lam, *, name):
    s, r2 = proj.shape
    rr = r2 // 2
    nb, bw, _ = w_r.shape
    t = _pick(s, (512, 256, 128, 64, 32))
    ngroups = t // SUBLANES

    def body(xp_ref, gate_ref, cw_ref, cb_ref, wr_ref, br_ref, wi_ref, bi_ref, lam_ref,
             m_ref, h_ref, pad, hcarry, a_scr, u_scr):
        i = pl.program_id(1)

        @pl.when(i == 0)
        def _():
            pad[0:SUBLANES, :] = jnp.zeros((SUBLANES, bw), F32)
            hcarry[...] = jnp.zeros_like(hcarry)

        xpre = xp_ref[...]
        pad[pl.ds(SUBLANES, t), :] = xpre
        xb = _conv(pad, cw_ref[...], cb_ref[...], t)
        pad[0:SUBLANES, :] = xpre[t - SUBLANES:, :]
        sp, _ = _softplus_neg(lam_ref[...])
        _, ig, a, mult = _lru_gates(xb, wr_ref[...], wi_ref[...], br_ref[...], bi_ref[...], sp)
        a_scr[...] = a
        u_scr[...] = mult * (ig * xb)
        row = lax.broadcasted_iota(jnp.int32, (SUBLANES, bw), 0)

        def groups(gi, hprev):
            offs = [pl.multiple_of((gi * SCAN_UNROLL + u) * SUBLANES, SUBLANES) for u in range(SCAN_UNROLL)]
            scanned = []
            for off in offs:
                av = a_scr[pl.ds(off, SUBLANES), :]
                uv = u_scr[pl.ds(off, SUBLANES), :]
                for dlt in (1, 2, 4):
                    keep = row >= dlt
                    uv = jnp.where(keep, av * pltpu.roll(uv, dlt, 0) + uv, uv)
                    av = jnp.where(keep, av * pltpu.roll(av, dlt, 0), av)
                scanned.append((av, uv))
            for off, (av, uv) in zip(offs, scanned):
                hv = av * hprev + uv
                h_ref[pl.ds(off, SUBLANES), :] = hv
                hprev = hv[SUBLANES - 1:SUBLANES, :]
            return hprev

        hcarry[...] = lax.fori_loop(0, ngroups // SCAN_UNROLL, groups, hcarry[...])
        gate = gate_ref[...]
        m_ref[...] = (h_ref[...] * (gate * _sigmoid(gate))).astype(BF16)

    blk, vec, wspec, cwspec = _lru_specs(t, bw, nb, lambda i: i)
    return pl.pallas_call(
        body, name=name, grid=(nb, s // t),
        in_specs=[blk(0), blk(nb), cwspec, vec, wspec, vec, wspec, vec, vec],
        out_specs=[blk(0), blk(0)],
        out_shape=[jax.ShapeDtypeStruct((s, rr), BF16), jax.ShapeDtypeStruct((s, rr), F32)],
        scratch_shapes=[pltpu.VMEM((t + SUBLANES, bw), F32), pltpu.VMEM((1, bw), F32),
                        pltpu.VMEM((t, bw), F32), pltpu.VMEM((t, bw), F32)],
        compiler_params=_params(("parallel", "arbitrary")),
    )(proj, proj, conv_w, conv_b, w_r, b_r, w_i, b_i, lam)


def lru_bwd(proj, hst, dm, conv_w, conv_b, w_r, b_r, w_i, b_i, lam, *, name):
    s, r2 = proj.shape
    rr = r2 // 2
    nb, bw, _ = w_r.shape
    t = _pick(s, (512, 256, 128, 64, 32))
    nt = s // t
    ngroups = t // SUBLANES
    nt_dims = (((1,), (1,)), ((), ()))
    tn_dims = (((0,), (0,)), ((), ()))

    def body(xp_ref, xhalo_ref, gate_ref, h_ref, hhalo_ref, dm_ref, cw_ref, cb_ref, wr_ref, br_ref, wi_ref,
             bi_ref, lam_ref,
             dxp_ref, dgate_ref, dcw_ref, dcb_ref, dwr_ref, dbr_ref, dwi_ref, dbi_ref, dlam_ref,
             pad, hpad, dpad, ecarry, a_scr, b_scr, d_scr):
        step = pl.program_id(1)

        @pl.when(step == 0)
        def _():
            dpad[pl.ds(t, SUBLANES), :] = jnp.zeros((SUBLANES, bw), F32)
            ecarry[...] = jnp.zeros_like(ecarry)
            dcw_ref[...] = jnp.zeros_like(dcw_ref)
            dcb_ref[...] = jnp.zeros_like(dcb_ref)
            dwr_ref[...] = jnp.zeros_like(dwr_ref)
            dbr_ref[...] = jnp.zeros_like(dbr_ref)
            dwi_ref[...] = jnp.zeros_like(dwi_ref)
            dbi_ref[...] = jnp.zeros_like(dbi_ref)
            dlam_ref[...] = jnp.zeros_like(dlam_ref)

        past = jnp.where(step == nt - 1, 0.0, 1.0)
        pad[0:SUBLANES, :] = xhalo_ref[...] * past
        pad[pl.ds(SUBLANES, t), :] = xp_ref[...]
        hpad[0:SUBLANES, :] = hhalo_ref[...] * past
        hpad[pl.ds(SUBLANES, t), :] = h_ref[...]
        cw = cw_ref[...]
        xb = _conv(pad, cw, cb_ref[...], t)
        sp, sg = _softplus_neg(lam_ref[...])
        wr = wr_ref[...]
        wi = wi_ref[...]
        r, ig, a, mult = _lru_gates(xb, wr, wi, br_ref[...], bi_ref[...], sp)
        gate = gate_ref[...]
        sgate = _sigmoid(gate)
        dmv = dm_ref[...]
        dgate_ref[...] = (dmv * h_ref[...] * (sgate * (1.0 + gate * (1.0 - sgate)))).astype(BF16)
        dy = dmv * (gate * sgate)
        a_scr[...] = a
        b_scr[...] = a * dy
        row = lax.broadcasted_iota(jnp.int32, (SUBLANES, bw), 0)

        def groups(gi, enext):
            offs = [pl.multiple_of((ngroups - 1 - gi * SCAN_UNROLL - u) * SUBLANES, SUBLANES)
                    for u in range(SCAN_UNROLL)]
            scanned = []
            for off in offs:
                av = a_scr[pl.ds(off, SUBLANES), :]
                bv = b_scr[pl.ds(off, SUBLANES), :]
                for dlt in (1, 2, 4):
                    keep = row < SUBLANES - dlt
                    bv = jnp.where(keep, av * pltpu.roll(bv, SUBLANES - dlt, 0) + bv, bv)
                    av = jnp.where(keep, av * pltpu.roll(av, SUBLANES - dlt, 0), av)
                scanned.append((av, bv))
            for off, (av, bv) in zip(offs, scanned):
                ev = av * enext + bv
                d_scr[pl.ds(off, SUBLANES), :] = jnp.where(row == SUBLANES - 1, enext,
                                                           pltpu.roll(ev, SUBLANES - 1, 0))
                enext = ev[0:1, :]
            return enext

        ecarry[...] = lax.fori_loop(0, ngroups // SCAN_UNROLL, groups, ecarry[...])
        dtot = dy + d_scr[...]
        da = dtot * hpad[pl.ds(SUBLANES - 1, t), :]
        dmult = dtot * (ig * xb)
        dlog_a = da * a - dmult * (a * a) / mult
        dr_pre = dlog_a * ((-LRU_C) * sp) * (r * (1.0 - r))
        di_pre = (dtot * mult * xb) * (ig * (1.0 - ig))
        dlam_ref[...] += jnp.sum(dlog_a * r, axis=0, keepdims=True) * (LRU_C * sg)
        dbr_ref[...] += jnp.sum(dr_pre, axis=0, keepdims=True)
        dbi_ref[...] += jnp.sum(di_pre, axis=0, keepdims=True)
        drb = dr_pre.astype(BF16)
        dib = di_pre.astype(BF16)
        xbb = xb.astype(BF16)
        dxb = (dtot * mult * ig
               + lax.dot_general(drb, wr, nt_dims, preferred_element_type=F32)
               + lax.dot_general(dib, wi, nt_dims, preferred_element_type=F32))
        dwr_ref[...] += lax.dot_general(xbb, drb, tn_dims, preferred_element_type=F32)
        dwi_ref[...] += lax.dot_general(xbb, dib, tn_dims, preferred_element_type=F32)
        dcb_ref[...] += jnp.sum(dxb, axis=0, keepdims=True)
        dpad[pl.ds(0, t), :] = dxb
        dxpre = cw[CONV_W - 1:CONV_W, :] * dxb
        dcw_ref[CONV_W - 1:CONV_W, :] += jnp.sum(dxb * pad[pl.ds(SUBLANES, t), :], axis=0, keepdims=True)
        for dlt in range(1, CONV_W):
            dxpre = dxpre + cw[CONV_W - 1 - dlt:CONV_W - dlt, :] * dpad[pl.ds(dlt, t), :]
            dcw_ref[CONV_W - 1 - dlt:CONV_W - dlt, :] += jnp.sum(
                dxb * pad[pl.ds(SUBLANES - dlt, t), :], axis=0, keepdims=True)
        dpad[pl.ds(t, SUBLANES), :] = dxb[0:SUBLANES, :]
        dxp_ref[...] = dxpre.astype(BF16)

    rev = lambda i: nt - 1 - i
    blk, vec, wspec, cwspec = _lru_specs(t, bw, nb, rev)
    halo = pl.BlockSpec((SUBLANES, bw), lambda n, i: (jnp.maximum(rev(i) * ngroups - 1, 0), n))
    return pl.pallas_call(
        body, name=name, grid=(nb, nt),
        in_specs=[blk(0), halo, blk(nb), blk(0), halo, blk(0), cwspec, vec, wspec, vec, wspec, vec, vec],
        out_specs=[blk(0), blk(0), cwspec, vec, wspec, vec, wspec, vec, vec],
        out_shape=[jax.ShapeDtypeStruct((s, rr), BF16), jax.ShapeDtypeStruct((s, rr), BF16),
                   jax.ShapeDtypeStruct((CONV_W, rr), F32), jax.ShapeDtypeStruct((1, rr), F32),
                   jax.ShapeDtypeStruct((nb, bw, bw), F32), jax.ShapeDtypeStruct((1, rr), F32),
                   jax.ShapeDtypeStruct((nb, bw, bw), F32), jax.ShapeDtypeStruct((1, rr), F32),
                   jax.ShapeDtypeStruct((1, rr), F32)],
        scratch_shapes=[pltpu.VMEM((t + SUBLANES, bw), F32), pltpu.VMEM((t + SUBLANES, bw), F32),
                        pltpu.VMEM((t + SUBLANES, bw), F32), pltpu.VMEM((1, bw), F32),
                        pltpu.VMEM((t, bw), F32), pltpu.VMEM((t, bw), F32), pltpu.VMEM((t, bw), F32)],
        compiler_params=_params(("parallel", "arbitrary")),
    )(proj, proj, proj, hst, hst, dm, conv_w, conv_b, w_r, b_r, w_i, b_i, lam)


def _softplus(z):
    return jnp.maximum(z, 0.0) + jnp.log(1.0 + jnp.exp2(jnp.abs(z) * (-LOG2E)))


def _att_blocks(s):
    bk = ATT_BLOCK if s % ATT_BLOCK == 0 else s
    bq = ATT_QTILES * bk if s % (ATT_QTILES * bk) == 0 else bk
    return bk, bq


def attn_fwd(projb, kv, *, name):
    s, a2 = projb.shape
    a = a2 // 2
    nh = a // HEAD_DIM
    bk, bq = _att_blocks(s)
    r = bq // bk
    nq = s // bq
    assert s // bk <= LANES
    scale = 1.0 / math.sqrt(HEAD_DIM)
    nt_dims = (((1,), (1,)), ((), ()))
    hp = ATT_HEADS_FWD if nh % ATT_HEADS_FWD == 0 else 1
    wd = hp * HEAD_DIM

    def body(q_ref, g_ref, k_ref, v_ref, m_ref, o_ref, off_ref, acc):
        i = pl.program_id(1)
        qb = (q_ref[...] * scale).astype(BF16)
        from_mat = (lax.broadcasted_iota(jnp.int32, (bk, bk), 0)
                    >= lax.broadcasted_iota(jnp.int32, (bk, bk), 1)).astype(BF16)
        rowi = lax.broadcasted_iota(jnp.int32, (bq, bk), 0)
        coli = lax.broadcasted_iota(jnp.int32, (bq, bk), 1)
        lane = lax.broadcasted_iota(jnp.int32, (bq, LANES), 1)
        cols = [slice(hh * HEAD_DIM, (hh + 1) * HEAD_DIM) for hh in range(hp)]

        def tile(j, carries, diag):
            causal = None if diag is None else (coli + diag * bk) < rowi
            rows = pl.ds(pl.multiple_of(j * bk, bk), bk)
            zs = [lax.dot_general(qb[:, c], k_ref[rows, c], nt_dims, preferred_element_type=F32) for c in cols]
            sums = []
            for z in zs:
                sp = _softplus(z)
                if causal is not None:
                    sp = jnp.where(causal, sp, 0.0)
                sums.append(jnp.dot(sp.astype(BF16), from_mat, preferred_element_type=F32))
            out = []
            for hh in range(hp):
                w = jnp.exp(zs[hh] - sums[hh] - carries[hh])
                if causal is not None:
                    w = jnp.where(causal, w, 0.0)
                acc[:, cols[hh]] += jnp.dot(w.astype(BF16), v_ref[rows, cols[hh]], preferred_element_type=F32)
                off_ref[hh] = jnp.where(lane == j, carries[hh], off_ref[hh])
                out.append(carries[hh] + sums[hh][:, 0:1])
            return tuple(out)

        acc[...] = jnp.zeros_like(acc)
        off_ref[...] = jnp.zeros_like(off_ref)
        carries = tuple(jnp.zeros((bq, 1), F32) for _ in range(hp))
        for dg in reversed(range(r)):
            carries = tile(r * i + dg, carries, dg)
        lax.fori_loop(0, r * i, lambda jj, c: tile(r * i - 1 - jj, c, None), carries)
        o = acc[...]
        o_ref[...] = o
        gate = g_ref[...]
        m_ref[...] = (o * (gate * _sigmoid(gate))).astype(BF16)

    ng = nh // hp
    qspec = lambda c0: pl.BlockSpec((bq, wd), lambda h, i, c0=c0: (i, c0 + h))
    kspec = lambda c0: pl.BlockSpec((s, wd), lambda h, i, c0=c0: (0, c0 + h), pipeline_mode=pl.Buffered(1))
    return pl.pallas_call(
        body, name=name, grid=(ng, nq),
        in_specs=[qspec(0), qspec(ng), kspec(0), kspec(ng)],
        out_specs=[qspec(0), qspec(0), pl.BlockSpec((hp, None, bq, LANES), lambda h, i: (h, i, 0, 0))],
        out_shape=[jax.ShapeDtypeStruct((s, a), BF16), jax.ShapeDtypeStruct((s, a), F32),
                   jax.ShapeDtypeStruct((nh, nq, bq, LANES), F32)],
        scratch_shapes=[pltpu.VMEM((bq, wd), F32)],
        compiler_params=_params(("parallel", "arbitrary")),
    )(projb, projb, kv, kv)


def attn_bwd(projb, do, kv, offs, *, name):
    s, a2 = projb.shape
    a = a2 // 2
    nh = a // HEAD_DIM
    bk, bq = _att_blocks(s)
    r = bq // bk
    nq = s // bq
    scale = 1.0 / math.sqrt(HEAD_DIM)
    nt_dims = (((1,), (1,)), ((), ()))
    tn_dims = (((0,), (0,)), ((), ()))
    hp = ATT_HEADS_BWD if nh % ATT_HEADS_BWD == 0 else 1
    wd = hp * HEAD_DIM

    def body(q_ref, do_ref, k_ref, v_ref, off_ref, dq_ref, dk_ref, dv_ref, dk_acc, dv_acc, dq_acc):
        i = pl.program_id(1)

        @pl.when(i == 0)
        def _():
            dk_acc[...] = jnp.zeros_like(dk_acc)
            dv_acc[...] = jnp.zeros_like(dv_acc)

        qb = (q_ref[...] * scale).astype(BF16)
        dob = do_ref[...]
        ki = lax.broadcasted_iota(jnp.int32, (bk, bk), 0)
        kj = lax.broadcasted_iota(jnp.int32, (bk, bk), 1)
        from_mat = (kj >= ki).astype(BF16)
        upto_mat = (kj <= ki).astype(BF16)
        rowi = lax.broadcasted_iota(jnp.int32, (bk, bq), 0)
        coli = lax.broadcasted_iota(jnp.int32, (bk, bq), 1)
        sub = lax.broadcasted_iota(jnp.int32, (LANES, bq), 0)
        offs_t = [off_ref[hh].T for hh in range(hp)]
        cols = [slice(hh * HEAD_DIM, (hh + 1) * HEAD_DIM) for hh in range(hp)]
        dq_acc[...] = jnp.zeros_like(dq_acc)

        def tile(j, gcarries, diag):
            causal = None if diag is None else (rowi + diag * bk) < coli
            rows = pl.ds(pl.multiple_of(j * bk, bk), bk)
            zs = [lax.dot_general(k_ref[rows, c], qb[:, c], nt_dims, preferred_element_type=F32) for c in cols]
            dws = [lax.dot_general(v_ref[rows, c], dob[:, c], nt_dims, preferred_element_type=F32) for c in cols]
            sigs, sums = [], []
            for z in zs:
                sp = _softplus(z)
                sigs.append(jnp.exp(z - sp))
                if causal is not None:
                    sp = jnp.where(causal, sp, 0.0)
                sums.append(jnp.dot(from_mat, sp.astype(BF16), preferred_element_type=F32))
            gs, totals = [], []
            for hh in range(hp):
                offj = jnp.sum(jnp.where(sub == j, offs_t[hh], 0.0), axis=0, keepdims=True)
                w = jnp.exp(zs[hh] - sums[hh] - offj)
                if causal is not None:
                    w = jnp.where(causal, w, 0.0)
                g = w * dws[hh]
                dv_acc[rows, cols[hh]] += jnp.dot(w.astype(BF16), dob[:, cols[hh]], preferred_element_type=F32)
                totals.append(jnp.dot(upto_mat, g.astype(BF16), preferred_element_type=F32))
                gs.append(g)
            out = []
            for hh in range(hp):
                dz = gs[hh] - (totals[hh] + gcarries[hh]) * sigs[hh]
                if causal is not None:
                    dz = jnp.where(causal, dz, 0.0)
                dzb = dz.astype(BF16)
                dk_acc[rows, cols[hh]] += jnp.dot(dzb, qb[:, cols[hh]], preferred_element_type=F32)
                dq_acc[:, cols[hh]] += lax.dot_general(dzb, k_ref[rows, cols[hh]], tn_dims,
                                                       preferred_element_type=F32)
                out.append(gcarries[hh] + totals[hh][bk - 1:bk, :])
            return tuple(out)

        init = tuple(jnp.zeros((1, bq), F32) for _ in range(hp))
        gcarries = lax.fori_loop(0, r * i, lambda j, c: tile(j, c, None), init)
        for dg in range(r):
            gcarries = tile(r * i + dg, gcarries, dg)
        dq_ref[...] = (dq_acc[...] * scale).astype(BF16)

        @pl.when(i == nq - 1)
        def _():
            dk_ref[...] = dk_acc[...].astype(BF16)
            dv_ref[...] = dv_acc[...].astype(BF16)

    ng = nh // hp
    once = pl.Buffered(1)
    qspec = lambda c0: pl.BlockSpec((bq, wd), lambda h, i, c0=c0: (i, c0 + h))
    kspec = lambda c0: pl.BlockSpec((s, wd), lambda h, i, c0=c0: (0, c0 + h), pipeline_mode=once)
    return pl.pallas_call(
        body, name=name, grid=(ng, nq),
        in_specs=[qspec(0), qspec(0), kspec(0), kspec(ng),
                  pl.BlockSpec((hp, None, bq, LANES), lambda h, i: (h, i, 0, 0))],
        out_specs=[qspec(0), kspec(0), kspec(0)],
        out_shape=[jax.ShapeDtypeStruct((s, a), BF16)] * 3,
        scratch_shapes=[pltpu.VMEM((s, wd), F32), pltpu.VMEM((s, wd), F32), pltpu.VMEM((bq, wd), F32)],
        compiler_params=_params(("parallel", "arbitrary")),
    )(projb, do, kv, kv, offs)


def gate_bwd(dm, o, projb, *, name):
    s, a = dm.shape
    tr = _pick(s, (512, 256, 128, 8))

    def body(dm_ref, o_ref, g_ref, do_ref, dg_ref):
        gate = g_ref[...]
        sg = _sigmoid(gate)
        dmv = dm_ref[...]
        do_ref[...] = (dmv * (gate * sg)).astype(BF16)
        dg_ref[...] = (dmv * o_ref[...] * (sg * (1.0 + gate * (1.0 - sg)))).astype(BF16)

    row = pl.BlockSpec((tr, a), lambda i: (i, 0))
    return pl.pallas_call(
        body, name=name, grid=(s // tr,),
        in_specs=[row, row, pl.BlockSpec((tr, a), lambda i: (i, 1))],
        out_specs=[row, row],
        out_shape=[jax.ShapeDtypeStruct((s, a), BF16)] * 2,
        compiler_params=_params(("parallel",)),
    )(dm, o, projb)


def _as2d(x):
    n = x.size
    cols = x.shape[-1]
    if cols % LANES != 0:
        cols = LANES
    return x.reshape(n // cols, cols)


def sum_parts(parts, *, name):
    p, rows, cols = parts.shape
    tr = _pick(rows, (512, 256, 128, 64, 32, 16))

    def body(p_ref, o_ref):
        acc = p_ref[0].astype(F32)
        for k in range(1, p):
            acc = acc + p_ref[k].astype(F32)
        o_ref[...] = acc

    return pl.pallas_call(
        body, name=name, grid=(rows // tr,),
        in_specs=[pl.BlockSpec((p, tr, cols), lambda i: (0, i, 0))],
        out_specs=pl.BlockSpec((tr, cols), lambda i: (i, 0)),
        out_shape=jax.ShapeDtypeStruct((rows, cols), F32),
        compiler_params=_params(("parallel",)),
    )(parts)


def adamw(w, g_parts, m, v, *, name):
    rows, cols = w.shape
    tr = _pick(rows, (128, 64, 32, 16, 8))
    np_ = len(g_parts)
    c1 = 1.0 / (1.0 - ADAM_B1 ** ADAM_STEP)
    c2 = 1.0 / (1.0 - ADAM_B2 ** ADAM_STEP)

    def body(*refs):
        w_ref, m_ref, v_ref = refs[0], refs[1], refs[2]
        g_refs = refs[3:3 + np_]
        go_ref, d_ref, mo_ref, vo_ref = refs[3 + np_:]
        g = g_refs[0][...]
        for gr in g_refs[1:]:
            g = g + gr[...]
        mn = ADAM_B1 * m_ref[...] + (1.0 - ADAM_B1) * g
        vn = ADAM_B2 * v_ref[...] + (1.0 - ADAM_B2) * (g * g)
        go_ref[...] = g
        mo_ref[...] = mn
        vo_ref[...] = vn
        d_ref[...] = (-ADAM_LR) * ((mn * c1) / (jnp.sqrt(vn * c2) + ADAM_EPS) + ADAM_WD * w_ref[...])

    spec = pl.BlockSpec((tr, cols), lambda i: (i, 0))
    return pl.pallas_call(
        body, name=name, grid=(rows // tr,),
        in_specs=[spec] * (3 + np_), out_specs=[spec] * 4,
        out_shape=[jax.ShapeDtypeStruct((rows, cols), F32)] * 4,
        compiler_params=_params(("parallel",)),
    )(w, m, v, *g_parts)


def _place():
    return lax.axis_index("x"), lax.axis_index("y"), lax.axis_index("c")


def _chip_peers(x, y, c):
    return [(1 - x, y, c), (x, 1 - y, c), (1 - x, 1 - y, c)]


def _shard_of(ref, axis, idx, n):
    start = pl.multiple_of(idx * n, n)
    sl = [slice(None)] * len(ref.shape)
    sl[axis] = pl.ds(start, n)
    return ref.at[tuple(sl)]


def gather_weights(shards, axes, *, name):
    na = len(shards)
    hbm = pl.BlockSpec(memory_space=pl.ANY)
    full_shapes = []
    for sh, ax in zip(shards, axes):
        shp = list(sh.shape)
        shp[ax] *= 4
        full_shapes.append(jax.ShapeDtypeStruct(tuple(shp), sh.dtype))

    def body(*refs):
        s_refs = refs[:na]
        f_refs = refs[na:2 * na]
        send_sems, recv_sems, local_sems = refs[2 * na:]
        x, y, c = _place()
        me = 2 * x + y
        peers = _chip_peers(x, y, c)
        copies = []
        for ai in range(na):
            n = s_refs[ai].shape[axes[ai]]
            mine = _shard_of(f_refs[ai], axes[ai], me, n)
            loc = pltpu.make_async_copy(s_refs[ai], mine, local_sems.at[ai])
            loc.start()
            copies.append((loc,))
            for k, peer in enumerate(peers):
                theirs = _shard_of(f_refs[ai], axes[ai], 2 * peer[0] + peer[1], n)
                snd = pltpu.make_async_remote_copy(
                    src_ref=s_refs[ai], dst_ref=mine, send_sem=send_sems.at[ai * 3 + k],
                    recv_sem=recv_sems.at[ai * 3 + k], device_id=peer, device_id_type=MESH)
                snd.start()
                rcv = pltpu.make_async_remote_copy(
                    src_ref=s_refs[ai], dst_ref=theirs, send_sem=send_sems.at[ai * 3 + k],
                    recv_sem=recv_sems.at[ai * 3 + k], device_id=peer, device_id_type=MESH)
                copies.append((snd, rcv))
        for cp in copies:
            if len(cp) == 1:
                cp[0].wait()
            else:
                cp[0].wait_send()
                cp[1].wait_recv()

    return pl.pallas_call(
        body, name=name,
        in_specs=[hbm] * na, out_specs=[hbm] * na, out_shape=full_shapes,
        scratch_shapes=[pltpu.SemaphoreType.DMA((3 * na,)), pltpu.SemaphoreType.DMA((3 * na,)),
                        pltpu.SemaphoreType.DMA((na,))],
    )(*shards)


def scatter_grads(grads, axes, *, name):
    na = len(grads)
    hbm = pl.BlockSpec(memory_space=pl.ANY)
    out_shapes = []
    for g, ax in zip(grads, axes):
        shp = list(g.shape)
        shp[ax] //= 4
        out_shapes.append(jax.ShapeDtypeStruct((4, *shp), g.dtype))

    def body(*refs):
        g_refs = refs[:na]
        p_refs = refs[na:2 * na]
        send_sems, recv_sems, local_sems = refs[2 * na:]
        x, y, c = _place()
        me = 2 * x + y
        peers = _chip_peers(x, y, c)
        copies = []
        for ai in range(na):
            n = p_refs[ai].shape[1 + axes[ai]]
            loc = pltpu.make_async_copy(_shard_of(g_refs[ai], axes[ai], me, n), p_refs[ai].at[0], local_sems.at[ai])
            loc.start()
            copies.append(loc)
            for k, peer in enumerate(peers):
                cp = pltpu.make_async_remote_copy(
                    src_ref=_shard_of(g_refs[ai], axes[ai], 2 * peer[0] + peer[1], n), dst_ref=p_refs[ai].at[1 + k],
                    send_sem=send_sems.at[ai * 3 + k], recv_sem=recv_sems.at[ai * 3 + k],
                    device_id=peer, device_id_type=MESH)
                cp.start()
                copies.append(cp)
        for cp in copies:
            cp.wait()

    return pl.pallas_call(
        body, name=name,
        in_specs=[hbm] * na, out_specs=[hbm] * na, out_shape=out_shapes,
        scratch_shapes=[pltpu.SemaphoreType.DMA((3 * na,)), pltpu.SemaphoreType.DMA((3 * na,)),
                        pltpu.SemaphoreType.DMA((na,))],
    )(*grads)


def swap_cores(arrs, *, name):
    na = len(arrs)
    hbm = pl.BlockSpec(memory_space=pl.ANY)

    def body(*refs):
        a_refs = refs[:na]
        o_refs = refs[na:2 * na]
        send_sems, recv_sems = refs[2 * na:]
        x, y, c = _place()
        copies = []
        for ai in range(na):
            cp = pltpu.make_async_remote_copy(
                src_ref=a_refs[ai], dst_ref=o_refs[ai], send_sem=send_sems.at[ai], recv_sem=recv_sems.at[ai],
                device_id=(x, y, 1 - c), device_id_type=MESH)
            cp.start()
            copies.append(cp)
        for cp in copies:
            cp.wait()

    return pl.pallas_call(
        body, name=name,
        in_specs=[hbm] * na, out_specs=[hbm] * na,
        out_shape=[jax.ShapeDtypeStruct(a.shape, a.dtype) for a in arrs],
        scratch_shapes=[pltpu.SemaphoreType.DMA((na,)), pltpu.SemaphoreType.DMA((na,))],
    )(*arrs)


def allreduce_small(buf, *, name):
    rows, cols = buf.shape

    def body(b_ref, o_ref, slots, send_sems, recv_sems):
        x, y, c = _place()
        me = 4 * x + 2 * y + c
        slots[0] = b_ref[...]
        copies = []
        for rel in range(1, 8):
            peer = (x ^ (rel >> 2), y ^ ((rel >> 1) & 1), c ^ (rel & 1))
            cp = pltpu.make_async_remote_copy(
                src_ref=b_ref, dst_ref=slots.at[rel], send_sem=send_sems.at[rel - 1],
                recv_sem=recv_sems.at[rel - 1], device_id=peer, device_id_type=MESH)
            cp.start()
            copies.append(cp)
        for cp in copies:
            cp.wait()
        acc = slots[me]
        for dev in range(1, 8):
            acc = acc + slots[dev ^ me]
        o_ref[...] = acc

    vm = pl.BlockSpec(memory_space=pltpu.VMEM)
    return pl.pallas_call(
        body, name=name, in_specs=[vm], out_specs=vm,
        out_shape=jax.ShapeDtypeStruct((rows, cols), F32),
        scratch_shapes=[pltpu.VMEM((8, rows, cols), F32), pltpu.SemaphoreType.DMA((7,)),
                        pltpu.SemaphoreType.DMA((7,))],
    )(buf)


def _pack_rows(arrs):
    parts = []
    for a in arrs:
        p = a.reshape(-1, LANES)
        parts.append(jnp.pad(p, ((0, (-p.shape[0]) % SUBLANES), (0, 0))))
    return jnp.concatenate(parts, axis=0)


def _unpack_rows(buf, shapes):
    out, r0 = [], 0
    for shp in shapes:
        n = math.prod(shp) // LANES
        out.append(buf[r0:r0 + n].reshape(shp))
        r0 += n + (-n) % SUBLANES
    return out


def kernel(x, a_norm, a_w_in, a_conv_w, a_conv_b, a_w_r, a_b_r, a_w_i, a_b_i, a_lambda, a_w_out, kv_norm, w_kv, b_norm, b_w_in, b_w_out, final_norm, loss_target, m_a_norm, m_a_w_in, m_a_conv_w, m_a_conv_b, m_a_w_r, m_a_b_r, m_a_w_i, m_a_b_i, m_a_lambda, m_a_w_out, m_kv_norm, m_w_kv, m_b_norm, m_b_w_in, m_b_w_out, m_final_norm, v_a_norm, v_a_w_in, v_a_conv_w, v_a_conv_b, v_a_w_r, v_a_b_r, v_a_w_i, v_a_b_i, v_a_lambda, v_a_w_out, v_kv_norm, v_w_kv, v_b_norm, v_b_w_in, v_b_w_out, v_final_norm):
    weights = dict(a_norm=a_norm, a_w_in=a_w_in, a_conv_w=a_conv_w, a_conv_b=a_conv_b, a_w_r=a_w_r, a_b_r=a_b_r,
                   a_w_i=a_w_i, a_b_i=a_b_i, a_lambda=a_lambda, a_w_out=a_w_out, kv_norm=kv_norm, w_kv=w_kv,
                   b_norm=b_norm, b_w_in=b_w_in, b_w_out=b_w_out, final_norm=final_norm)
    mom1 = dict(a_norm=m_a_norm, a_w_in=m_a_w_in, a_conv_w=m_a_conv_w, a_conv_b=m_a_conv_b, a_w_r=m_a_w_r,
                a_b_r=m_a_b_r, a_w_i=m_a_w_i, a_b_i=m_a_b_i, a_lambda=m_a_lambda, a_w_out=m_a_w_out,
                kv_norm=m_kv_norm, w_kv=m_w_kv, b_norm=m_b_norm, b_w_in=m_b_w_in, b_w_out=m_b_w_out,
                final_norm=m_final_norm)
    mom2 = dict(a_norm=v_a_norm, a_w_in=v_a_w_in, a_conv_w=v_a_conv_w, a_conv_b=v_a_conv_b, a_w_r=v_a_w_r,
                a_b_r=v_a_b_r, a_w_i=v_a_w_i, a_b_i=v_a_b_i, a_lambda=v_a_lambda, a_w_out=v_a_w_out,
                kv_norm=v_kv_norm, w_kv=v_w_kv, b_norm=v_b_norm, b_w_in=v_b_w_in, b_w_out=v_b_w_out,
                final_norm=v_final_norm)
    order = list(weights)
    x0 = x[0]
    target = loss_target[0]
    d = x0.shape[1]
    chip = 2 * lax.axis_index("x") + lax.axis_index("y")

    big = ["a_w_in", "a_w_r", "a_w_i", "a_w_out", "w_kv", "b_w_in", "b_w_out"]
    big_axis = dict(a_w_in=1, a_w_r=1, a_w_i=1, a_w_out=0, w_kv=1, b_w_in=1, b_w_out=0)
    local = dict(a_w_in=a_w_in[0], a_w_r=a_w_r[0], a_w_i=a_w_i[0], a_w_out=a_w_out[0], w_kv=w_kv,
                 b_w_in=b_w_in[0], b_w_out=b_w_out[0])
    shards = [local[n].astype(BF16) for n in big] + [a_conv_w[0], b_norm]
    full = gather_weights(shards, [big_axis[n] for n in big] + [1, 1], name="gather_weights")
    wf = dict(zip(big + ["a_conv_w", "b_norm"], full))
    wf.update(a_norm=a_norm, a_conv_b=a_conv_b, a_b_r=a_b_r, a_b_i=a_b_i, a_lambda=a_lambda,
              kv_norm=kv_norm.reshape(1, d), final_norm=final_norm.reshape(1, d))
    loss_part, grad_x, gbig, gsmall = _local_grads(x0, target, wf)

    parts = scatter_grads([gbig[n] for n in big], [big_axis[n] for n in big], name="scatter_grads")
    sums = [sum_parts(p.reshape(4, *_as2d(p[0]).shape), name="sum_" + n) for n, p in zip(big, parts)]
    others = swap_cores(sums, name="swap_cores")

    small = ["a_norm", "a_conv_b", "a_b_r", "a_b_i", "a_lambda", "kv_norm", "final_norm", "a_conv_w", "b_norm"]
    buf = _pack_rows([gsmall[n] for n in small] + [loss_part])
    red = allreduce_small(buf, name="allreduce_small")
    red_list = _unpack_rows(red, [gsmall[n].shape for n in small] + [(1, LANES)])
    gs = dict(zip(small, red_list[:-1]))
    loss = red_list[-1][0, 0]
    n_conv = a_conv_w.shape[2]
    gs["a_conv_w"] = lax.dynamic_slice_in_dim(gs["a_conv_w"], chip * n_conv, n_conv, axis=1)
    n_bn = b_norm.shape[1]
    gs["b_norm"] = lax.dynamic_slice_in_dim(gs["b_norm"], chip * n_bn, n_bn, axis=1)

    grads, deltas, new_m, new_v = {}, {}, {}, {}
    for n, s_mine, s_other in zip(big, sums, others):
        shp = weights[n].shape
        g, dlt, mn, vn = adamw(_as2d(weights[n]), [s_mine, s_other], _as2d(mom1[n]), _as2d(mom2[n]),
                               name="adamw_" + n)
        grads[n], deltas[n], new_m[n], new_v[n] = (t.reshape(shp) for t in (g, dlt, mn, vn))
    shapes = [weights[n].shape for n in small]
    wpk, gpk, mpk, vpk = (_pack_rows([src[n] for n in small]) for src in (weights, gs, mom1, mom2))
    outs = adamw(wpk, [gpk], mpk, vpk, name="adamw_small")
    for dst, packed in zip((grads, deltas, new_m, new_v), outs):
        for n, val in zip(small, _unpack_rows(packed, shapes)):
            dst[n] = val

    return (loss, grad_x[None], *[grads[n] for n in order], *[deltas[n] for n in order],
            *[new_m[n] for n in order], *[new_v[n] for n in order])


def _local_grads(x0, target, wf):
    a_norm, a_conv_b, a_b_r, a_b_i, a_lambda = (wf[n] for n in ("a_norm", "a_conv_b", "a_b_r", "a_b_i", "a_lambda"))
    kv_norm, final_norm = wf["kv_norm"], wf["final_norm"]

    (h_a,) = rms_fwd(x0, [a_norm], name="norm_a")
    proj_a = matmul(h_a, wf["a_w_in"], name="a_in")
    m_a, hst = lru_fwd(proj_a, wf["a_conv_w"], a_conv_b, wf["a_w_r"], a_b_r, wf["a_w_i"], a_b_i, a_lambda,
                       name="lru_fwd")
    x1 = matmul(m_a, wf["a_w_out"], residual=x0, name="a_out")
    kvn, hb = rms_fwd(x1, [kv_norm, wf["b_norm"]], name="norm_kv_b")
    kv = matmul(kvn, wf["w_kv"], out_dtype=BF16, name="kv_proj")
    proj_b = matmul(hb, wf["b_w_in"], name="b_in")
    m_b, o, offs = attn_fwd(proj_b, kv, name="attn_fwd")
    x2 = matmul(m_b, wf["b_w_out"], residual=x1, name="b_out")
    loss_part, g_final, dx2, dx2b = loss_bwd(x2, target, final_norm, name="loss_bwd")

    dm_b = matmul(dx2b, wf["b_w_out"], tb=True, name="b_out_dx")
    g_b_w_out = matmul(m_b, dx2b, ta=True, out_dtype=BF16, name="b_out_dw")
    do, dgate_b = gate_bwd(dm_b, o, proj_b, name="gate_bwd")
    dq, dk, dv = attn_bwd(proj_b, do, kv, offs, name="attn_bwd")
    dproj_b = jnp.concatenate([dq, dgate_b], axis=1)
    dkv = jnp.concatenate([dk, dv], axis=1)
    dhb = matmul(dproj_b, wf["b_w_in"], tb=True, name="b_in_dx")
    g_b_w_in = matmul(hb, dproj_b, ta=True, out_dtype=BF16, name="b_in_dw")
    dkvn = matmul(dkv, wf["w_kv"], tb=True, name="kv_dx")
    g_w_kv = matmul(kvn, dkv, ta=True, out_dtype=BF16, name="kv_dw")
    dx1, dx1b, (g_kv_norm, g_b_norm) = rms_bwd(
        x1, dx2, [(kv_norm, dkvn), (wf["b_norm"], dhb)], name="norm_kv_b_bwd")

    dm_a = matmul(dx1b, wf["a_w_out"], tb=True, name="a_out_dx")
    g_a_w_out = matmul(m_a, dx1b, ta=True, out_dtype=BF16, name="a_out_dw")
    dxpre, dgate_a, g_conv_w, g_conv_b, g_w_r, g_b_r, g_w_i, g_b_i, g_lambda = lru_bwd(
        proj_a, hst, dm_a, wf["a_conv_w"], a_conv_b, wf["a_w_r"], a_b_r, wf["a_w_i"], a_b_i, a_lambda,
        name="lru_bwd")
    dproj_a = jnp.concatenate([dxpre, dgate_a], axis=1)
    dh_a = matmul(dproj_a, wf["a_w_in"], tb=True, name="a_in_dx")
    g_a_w_in = matmul(h_a, dproj_a, ta=True, out_dtype=BF16, name="a_in_dw")
    grad_x, _, (g_a_norm,) = rms_bwd(x0, dx1, [(a_norm, dh_a)], name="norm_a_bwd")

    gbig = dict(a_w_in=g_a_w_in, a_w_r=g_w_r.astype(BF16), a_w_i=g_w_i.astype(BF16), a_w_out=g_a_w_out,
                w_kv=g_w_kv, b_w_in=g_b_w_in, b_w_out=g_b_w_out)
    gsmall = dict(a_norm=g_a_norm, a_conv_b=g_conv_b, a_b_r=g_b_r, a_b_i=g_b_i, a_lambda=g_lambda,
                  kv_norm=g_kv_norm, final_norm=g_final, a_conv_w=g_conv_w, b_norm=g_b_norm)
    return loss_part, grad_x, gbig, gsmall
```

```python
import math

import jax
import jax.numpy as jnp
from jax import lax
from jax.experimental import pallas as pl
from jax.experimental.pallas import tpu as pltpu

F32 = jnp.float32
BF16 = jnp.bfloat16
MESH = pl.DeviceIdType.MESH

EPS = 1e-6
LRU_C = 8.0
CONV_W = 4
HEAD_DIM = 128
ADAM_LR = 0.001
ADAM_B1 = 0.9
ADAM_B2 = 0.999
ADAM_EPS = 1e-08
ADAM_WD = 0.01
ADAM_STEP = 10

V7X_VMEM_LIMIT = 56 * 1024 * 1024
LANES = 128
SUBLANES = 8
ATT_BLOCK = 256
ATT_QTILES = 2
ATT_HEADS_FWD = 2
ATT_HEADS_BWD = 2
LOG2E = 1.4426950408889634
SCAN_UNROLL = 4


def _pick(dim, cands):
    for c in cands:
        if dim % c == 0:
            return c
    return dim


def _params(sem, vmem=V7X_VMEM_LIMIT):
    return pltpu.CompilerParams(dimension_semantics=sem, vmem_limit_bytes=vmem)


def _sigmoid(x):
    return 1.0 / (1.0 + jnp.exp(-x))


def _place():
    return lax.axis_index("x"), lax.axis_index("y"), lax.axis_index("c")


def _chip_peers(x, y, c):
    return [(1 - x, y, c), (x, 1 - y, c), (1 - x, 1 - y, c)]


def _shard_of(ref, axis, idx, n):
    start = pl.multiple_of(idx * n, n)
    sl = [slice(None)] * len(ref.shape)
    sl[axis] = pl.ds(start, n)
    return ref.at[tuple(sl)]


def _exchange(kind, in_refs, out_refs, axes, send_sems, recv_sems, local_sems):
    x, y, c = _place()
    me = 2 * x + y
    peers = _chip_peers(x, y, c)
    pairs = []
    for ai, (src, dst, ax) in enumerate(zip(in_refs, out_refs, axes)):
        if kind == "gather":
            n = src.shape[ax]
            mine = _shard_of(dst, ax, me, n)
            loc = pltpu.make_async_copy(src, mine, local_sems.at[ai])
        else:
            n = dst.shape[1 + ax]
            loc = pltpu.make_async_copy(_shard_of(src, ax, me, n), dst.at[0], local_sems.at[ai])
        pairs.append((loc.start, loc.wait))
        for k, peer in enumerate(peers):
            sem = dict(send_sem=send_sems.at[ai * 3 + k], recv_sem=recv_sems.at[ai * 3 + k],
                       device_id=peer, device_id_type=MESH)
            theirs = 2 * peer[0] + peer[1]
            if kind == "gather":
                snd = pltpu.make_async_remote_copy(src_ref=src, dst_ref=mine, **sem)
                rcv = pltpu.make_async_remote_copy(src_ref=src, dst_ref=_shard_of(dst, ax, theirs, n), **sem)
            else:
                snd = pltpu.make_async_remote_copy(src_ref=_shard_of(src, ax, theirs, n), dst_ref=dst.at[1 + k], **sem)
                rcv = snd
            pairs.append((snd.start, lambda snd=snd, rcv=rcv: (snd.wait_send(), rcv.wait_recv())))
    return pairs


def _exchange_shapes(kind, arrays, axes):
    out = []
    for arr, ax in zip(arrays, axes):
        shp = list(arr.shape)
        if kind == "gather":
            shp[ax] *= 4
            out.append(jax.ShapeDtypeStruct(tuple(shp), arr.dtype))
        else:
            shp[ax] //= 4
            out.append(jax.ShapeDtypeStruct((4, *shp), arr.dtype))
    return out


def _exchange_sems(n):
    return [pltpu.SemaphoreType.DMA((3 * n,)), pltpu.SemaphoreType.DMA((3 * n,)), pltpu.SemaphoreType.DMA((n,))]


def matmul(a, b, *, ta=False, tb=False, out_dtype=F32, residual=None, exchange=None, name):
    m = a.shape[1] if ta else a.shape[0]
    kdim = a.shape[0] if ta else a.shape[1]
    n = b.shape[0] if tb else b.shape[1]
    assert (b.shape[1] if tb else b.shape[0]) == kdim
    tm = _pick(m, (1024, 640, 512, 256, 128))
    tn = _pick(n, (1024, 640, 512, 256, 128))
    tk = _pick(kdim, (2560, 2048, 1024, 512, 256, 128))
    grid = (m // tm, n // tn, kdim // tk)
    nk = grid[2]
    dn = (((0 if ta else 1,), (1 if tb else 0,)), ((), ()))
    nres = 0 if residual is None else 1
    nex = 0 if exchange is None else len(exchange[1])

    def body(*refs):
        a_ref, b_ref = refs[0], refs[1]
        r_ref = refs[2] if nres else None
        ex_in = refs[2 + nres:2 + nres + nex]
        o_ref = refs[2 + nres + nex]
        ex_out = refs[3 + nres + nex:3 + nres + 2 * nex]
        acc = refs[3 + nres + 2 * nex]
        sems = refs[4 + nres + 2 * nex:]
        i, j, k = pl.program_id(0), pl.program_id(1), pl.program_id(2)
        if nex:
            @pl.when((i == 0) & (j == 0) & (k == 0))
            def _():
                for start, _ in _exchange(exchange[0], ex_in, ex_out, exchange[2], *sems):
                    start()

        @pl.when(k == 0)
        def _():
            acc[...] = jnp.zeros_like(acc)

        acc[...] += lax.dot_general(a_ref[...].astype(BF16), b_ref[...].astype(BF16), dn,
                                    preferred_element_type=F32)

        @pl.when(k == nk - 1)
        def _():
            r = acc[...]
            if r_ref is not None:
                r = r + r_ref[...]
            o_ref[...] = r.astype(out_dtype)

        if nex:
            @pl.when((i == grid[0] - 1) & (j == grid[1] - 1) & (k == nk - 1))
            def _():
                for _, finish in _exchange(exchange[0], ex_in, ex_out, exchange[2], *sems):
                    finish()

    a_spec = (pl.BlockSpec((tk, tm), lambda i, j, k: (k, i)) if ta
              else pl.BlockSpec((tm, tk), lambda i, j, k: (i, k)))
    b_spec = (pl.BlockSpec((tn, tk), lambda i, j, k: (j, k)) if tb
              else pl.BlockSpec((tk, tn), lambda i, j, k: (k, j)))
    o_spec = pl.BlockSpec((tm, tn), lambda i, j, k: (i, j))
    hbm = pl.BlockSpec(memory_space=pl.ANY)
    in_specs = [a_spec, b_spec] + [o_spec] * nres + [hbm] * nex
    args = [a, b] + ([residual] if nres else []) + (list(exchange[1]) if nex else [])
    out_shape = [jax.ShapeDtypeStruct((m, n), out_dtype)]
    scratch = [pltpu.VMEM((tm, tn), F32)]
    if nex:
        out_shape += _exchange_shapes(*exchange)
        scratch += _exchange_sems(nex)
    outs = pl.pallas_call(
        body, name=name, grid=grid,
        in_specs=in_specs, out_specs=[o_spec] + [hbm] * nex, out_shape=out_shape,
        scratch_shapes=scratch,
        compiler_params=_params(("arbitrary",) * 3 if nex else ("parallel", "parallel", "arbitrary")),
    )(*args)
    return (outs[0], list(outs[1:])) if nex else outs[0]


def rms_fwd(x, gains, *, name):
    s, d = x.shape
    tr = _pick(s, (512, 256, 128, 8))
    ng = len(gains)

    def body(*refs):
        x_ref = refs[0]
        g_refs = refs[1:1 + ng]
        o_refs = refs[1 + ng:]
        xv = x_ref[...]
        y = xv * lax.rsqrt(jnp.mean(xv * xv, axis=-1, keepdims=True) + EPS)
        for g_ref, o_ref in zip(g_refs, o_refs):
            o_ref[...] = (y * g_ref[...]).astype(BF16)

    row = pl.BlockSpec((tr, d), lambda i: (i, 0))
    vec = pl.BlockSpec((1, d), lambda i: (0, 0))
    return pl.pallas_call(
        body, name=name, grid=(s // tr,),
        in_specs=[row] + [vec] * ng, out_specs=[row] * ng,
        out_shape=[jax.ShapeDtypeStruct((s, d), BF16)] * ng,
        compiler_params=_params(("parallel",)),
    )(x, *gains)


def rms_bwd(x, dres, norms, *, name):
    s, d = x.shape
    tr = _pick(s, (256, 128, 8))
    ng = len(norms)

    def body(*refs):
        x_ref, dres_ref = refs[0], refs[1]
        g_refs = refs[2:2 + ng]
        dh_refs = refs[2 + ng:2 + 2 * ng]
        dx_ref, dxb_ref = refs[2 + 2 * ng], refs[3 + 2 * ng]
        dg_refs = refs[4 + 2 * ng:]
        i = pl.program_id(0)
        xv = x_ref[...]
        r = lax.rsqrt(jnp.mean(xv * xv, axis=-1, keepdims=True) + EPS)
        xhat = xv * r
        dx = dres_ref[...]
        for g_ref, dh_ref, dg_ref in zip(g_refs, dh_refs, dg_refs):
            dh = dh_ref[...]
            part = jnp.sum(dh * xhat, axis=0, keepdims=True)

            @pl.when(i == 0)
            def _():
                dg_ref[...] = part

            @pl.when(i > 0)
            def _():
                dg_ref[...] += part

            dxhat = dh * g_ref[...]
            dx = dx + r * (dxhat - xhat * jnp.mean(dxhat * xhat, axis=-1, keepdims=True))
        dx_ref[...] = dx
        dxb_ref[...] = dx.astype(BF16)

    row = pl.BlockSpec((tr, d), lambda i: (i, 0))
    vec = pl.BlockSpec((1, d), lambda i: (0, 0))
    outs = pl.pallas_call(
        body, name=name, grid=(s // tr,),
        in_specs=[row, row] + [vec] * ng + [row] * ng,
        out_specs=[row, row] + [vec] * ng,
        out_shape=[jax.ShapeDtypeStruct((s, d), F32), jax.ShapeDtypeStruct((s, d), BF16)]
        + [jax.ShapeDtypeStruct((1, d), F32)] * ng,
        compiler_params=_params(("arbitrary",)),
    )(x, dres, *[g for g, _ in norms], *[dh for _, dh in norms])
    return outs[0], outs[1], list(outs[2:])


def loss_bwd(x2, target, gain, *, name):
    s, d = x2.shape
    tr = _pick(s, (256, 128, 8))
    nsteps = s // tr

    def body(x_ref, t_ref, g_ref, loss_ref, dg_ref, dx_ref, dxb_ref, sq_acc):
        i = pl.program_id(0)
        xv = x_ref[...]
        r = lax.rsqrt(jnp.mean(xv * xv, axis=-1, keepdims=True) + EPS)
        xhat = xv * r
        g = g_ref[...]
        err = xhat * g - t_ref[...]
        dy = err * (1.0 / d)
        sq = jnp.sum(err * err, axis=0, keepdims=True)
        dgp = jnp.sum(dy * xhat, axis=0, keepdims=True)

        @pl.when(i == 0)
        def _():
            sq_acc[...] = sq
            dg_ref[...] = dgp

        @pl.when(i > 0)
        def _():
            sq_acc[...] += sq
            dg_ref[...] += dgp

        dxhat = dy * g
        dx = r * (dxhat - xhat * jnp.mean(dxhat * xhat, axis=-1, keepdims=True))
        dx_ref[...] = dx
        dxb_ref[...] = dx.astype(BF16)

        @pl.when(i == nsteps - 1)
        def _():
            tot = jnp.sum(sq_acc[...], axis=-1, keepdims=True) * (0.5 / d)
            loss_ref[...] = jnp.broadcast_to(tot, (1, LANES))

    row = pl.BlockSpec((tr, d), lambda i: (i, 0))
    vec = pl.BlockSpec((1, d), lambda i: (0, 0))
    return pl.pallas_call(
        body, name=name, grid=(nsteps,),
        in_specs=[row, row, vec],
        out_specs=[pl.BlockSpec((1, LANES), lambda i: (0, 0)), vec, row, row],
        out_shape=[jax.ShapeDtypeStruct((1, LANES), F32), jax.ShapeDtypeStruct((1, d), F32),
                   jax.ShapeDtypeStruct((s, d), F32), jax.ShapeDtypeStruct((s, d), BF16)],
        scratch_shapes=[pltpu.VMEM((1, d), F32)],
        compiler_params=_params(("arbitrary",)),
    )(x2, target, gain)


def _lru_gates(xb, wr, wi, br, bi, sp):
    xbb = xb.astype(BF16)
    r = _sigmoid(jnp.dot(xbb, wr, preferred_element_type=F32) + br)
    ig = _sigmoid(jnp.dot(xbb, wi, preferred_element_type=F32) + bi)
    log_a = (-LRU_C) * r * sp
    a = jnp.exp(log_a)
    mult = jnp.sqrt(jnp.maximum(-jnp.tanh(log_a) * (a * a + 1.0), 0.0))
    return r, ig, a, mult


def _softplus_neg(lam):
    e = jnp.exp(-jnp.abs(lam))
    sp = jnp.maximum(-lam, 0.0) + jnp.log(1.0 + e)
    sg = jnp.where(lam >= 0, e, 1.0) / (1.0 + e)
    return sp, sg


def _conv(pad_ref, w, b, t):
    acc = b + w[CONV_W - 1:CONV_W, :] * pad_ref[pl.ds(SUBLANES, t), :]
    for dlt in range(1, CONV_W):
        acc = acc + w[CONV_W - 1 - dlt:CONV_W - dlt, :] * pad_ref[pl.ds(SUBLANES - dlt, t), :]
    return acc


def _lru_specs(t, bw, nb, time_of):
    blk = lambda c0: pl.BlockSpec((t, bw), lambda n, i, c0=c0: (time_of(i), c0 + n))
    vec = pl.BlockSpec((1, bw), lambda n, i: (0, n))
    wspec = pl.BlockSpec((None, bw, bw), lambda n, i: (n, 0, 0))
    cwspec = pl.BlockSpec((CONV_W, bw), lambda n, i: (0, n))
    return blk, vec, wspec, cwspec


def lru_fwd(proj, conv_w, conv_b, w_r, b_r, w_i, b_i, lam, *, name):
    s, r2 = proj.shape
    rr = r2 // 2
    nb, bw, _ = w_r.shape
    t = _pick(s, (512, 256, 128, 64, 32))
    ngroups = t // SUBLANES

    def body(xp_ref, gate_ref, cw_ref, cb_ref, wr_ref, br_ref, wi_ref, bi_ref, lam_ref,
             m_ref, h_ref, pad, hcarry, a_scr, u_scr):
        i = pl.program_id(1)

        @pl.when(i == 0)
        def _():
            pad[0:SUBLANES, :] = jnp.zeros((SUBLANES, bw), F32)
            hcarry[...] = jnp.zeros_like(hcarry)

        xpre = xp_ref[...]
        pad[pl.ds(SUBLANES, t), :] = xpre
        xb = _conv(pad, cw_ref[...], cb_ref[...], t)
        pad[0:SUBLANES, :] = xpre[t - SUBLANES:, :]
        sp, _ = _softplus_neg(lam_ref[...])
        _, ig, a, mult = _lru_gates(xb, wr_ref[...], wi_ref[...], br_ref[...], bi_ref[...], sp)
        a_scr[...] = a
        u_scr[...] = mult * (ig * xb)
        row = lax.broadcasted_iota(jnp.int32, (SUBLANES, bw), 0)

        def groups(gi, hprev):
            offs = [pl.multiple_of((gi * SCAN_UNROLL + u) * SUBLANES, SUBLANES) for u in range(SCAN_UNROLL)]
            scanned = []
            for off in offs:
                av = a_scr[pl.ds(off, SUBLANES), :]
                uv = u_scr[pl.ds(off, SUBLANES), :]
                for dlt in (1, 2, 4):
                    keep = row >= dlt
                    uv = jnp.where(keep, av * pltpu.roll(uv, dlt, 0) + uv, uv)
                    av = jnp.where(keep, av * pltpu.roll(av, dlt, 0), av)
                scanned.append((av, uv))
            for off, (av, uv) in zip(offs, scanned):
                hv = av * hprev + uv
                h_ref[pl.ds(off, SUBLANES), :] = hv
                hprev = hv[SUBLANES - 1:SUBLANES, :]
            return hprev

        hcarry[...] = lax.fori_loop(0, ngroups // SCAN_UNROLL, groups, hcarry[...])
        gate = gate_ref[...]
        m_ref[...] = (h_ref[...] * (gate * _sigmoid(gate))).astype(BF16)

    blk, vec, wspec, cwspec = _lru_specs(t, bw, nb, lambda i: i)
    return pl.pallas_call(
        body, name=name, grid=(nb, s // t),
        in_specs=[blk(0), blk(nb), cwspec, vec, wspec, vec, wspec, vec, vec],
        out_specs=[blk(0), blk(0)],
        out_shape=[jax.ShapeDtypeStruct((s, rr), BF16), jax.ShapeDtypeStruct((s, rr), F32)],
        scratch_shapes=[pltpu.VMEM((t + SUBLANES, bw), F32), pltpu.VMEM((1, bw), F32),
                        pltpu.VMEM((t, bw), F32), pltpu.VMEM((t, bw), F32)],
        compiler_params=_params(("parallel", "arbitrary")),
    )(proj, proj, conv_w, conv_b, w_r, b_r, w_i, b_i, lam)


def lru_bwd(proj, hst, dm, conv_w, conv_b, w_r, b_r, w_i, b_i, lam, *, name):
    s, r2 = proj.shape
    rr = r2 // 2
    nb, bw, _ = w_r.shape
    t = _pick(s, (512, 256, 128, 64, 32))
    nt = s // t
    ngroups = t // SUBLANES
    nt_dims = (((1,), (1,)), ((), ()))
    tn_dims = (((0,), (0,)), ((), ()))

    def body(xp_ref, xhalo_ref, gate_ref, h_ref, hhalo_ref, dm_ref, cw_ref, cb_ref, wr_ref, br_ref, wi_ref,
             bi_ref, lam_ref,
             dxp_ref, dgate_ref, dcw_ref, dcb_ref, dwr_ref, dbr_ref, dwi_ref, dbi_ref, dlam_ref,
             pad, hpad, dpad, ecarry, a_scr, b_scr, d_scr):
        step = pl.program_id(1)

        @pl.when(step == 0)
        def _():
            dpad[pl.ds(t, SUBLANES), :] = jnp.zeros((SUBLANES, bw), F32)
            ecarry[...] = jnp.zeros_like(ecarry)
            dcw_ref[...] = jnp.zeros_like(dcw_ref)
            dcb_ref[...] = jnp.zeros_like(dcb_ref)
            dwr_ref[...] = jnp.zeros_like(dwr_ref)
            dbr_ref[...] = jnp.zeros_like(dbr_ref)
            dwi_ref[...] = jnp.zeros_like(dwi_ref)
            dbi_ref[...] = jnp.zeros_like(dbi_ref)
            dlam_ref[...] = jnp.zeros_like(dlam_ref)

        past = jnp.where(step == nt - 1, 0.0, 1.0)
        pad[0:SUBLANES, :] = xhalo_ref[...] * past
        pad[pl.ds(SUBLANES, t), :] = xp_ref[...]
        hpad[0:SUBLANES, :] = hhalo_ref[...] * past
        hpad[pl.ds(SUBLANES, t), :] = h_ref[...]
        cw = cw_ref[...]
        xb = _conv(pad, cw, cb_ref[...], t)
        sp, sg = _softplus_neg(lam_ref[...])
        wr = wr_ref[...]
        wi = wi_ref[...]
        r, ig, a, mult = _lru_gates(xb, wr, wi, br_ref[...], bi_ref[...], sp)
        gate = gate_ref[...]
        sgate = _sigmoid(gate)
        dmv = dm_ref[...]
        dgate_ref[...] = (dmv * h_ref[...] * (sgate * (1.0 + gate * (1.0 - sgate)))).astype(BF16)
        dy = dmv * (gate * sgate)
        a_scr[...] = a
        b_scr[...] = a * dy
        row = lax.broadcasted_iota(jnp.int32, (SUBLANES, bw), 0)

        def groups(gi, enext):
            offs = [pl.multiple_of((ngroups - 1 - gi * SCAN_UNROLL - u) * SUBLANES, SUBLANES)
                    for u in range(SCAN_UNROLL)]
            scanned = []
            for off in offs:
                av = a_scr[pl.ds(off, SUBLANES), :]
                bv = b_scr[pl.ds(off, SUBLANES), :]
                for dlt in (1, 2, 4):
                    keep = row < SUBLANES - dlt
                    bv = jnp.where(keep, av * pltpu.roll(bv, SUBLANES - dlt, 0) + bv, bv)
                    av = jnp.where(keep, av * pltpu.roll(av, SUBLANES - dlt, 0), av)
                scanned.append((av, bv))
            for off, (av, bv) in zip(offs, scanned):
                ev = av * enext + bv
                d_scr[pl.ds(off, SUBLANES), :] = jnp.where(row == SUBLANES - 1, enext,
                                                           pltpu.roll(ev, SUBLANES - 1, 0))
                enext = ev[0:1, :]
            return enext

        ecarry[...] = lax.fori_loop(0, ngroups // SCAN_UNROLL, groups, ecarry[...])
        dtot = dy + d_scr[...]
        da = dtot * hpad[pl.ds(SUBLANES - 1, t), :]
        dmult = dtot * (ig * xb)
        dlog_a = da * a - dmult * (a * a) / mult
        dr_pre = dlog_a * ((-LRU_C) * sp) * (r * (1.0 - r))
        di_pre = (dtot * mult * xb) * (ig * (1.0 - ig))
        dlam_ref[...] += jnp.sum(dlog_a * r, axis=0, keepdims=True) * (LRU_C * sg)
        dbr_ref[...] += jnp.sum(dr_pre, axis=0, keepdims=True)
        dbi_ref[...] += jnp.sum(di_pre, axis=0, keepdims=True)
        drb = dr_pre.astype(BF16)
        dib = di_pre.astype(BF16)
        xbb = xb.astype(BF16)
        dxb = (dtot * mult * ig
               + lax.dot_general(drb, wr, nt_dims, preferred_element_type=F32)
               + lax.dot_general(dib, wi, nt_dims, preferred_element_type=F32))
        dwr_ref[...] += lax.dot_general(xbb, drb, tn_dims, preferred_element_type=F32)
        dwi_ref[...] += lax.dot_general(xbb, dib, tn_dims, preferred_element_type=F32)
        dcb_ref[...] += jnp.sum(dxb, axis=0, keepdims=True)
        dpad[pl.ds(0, t), :] = dxb
        dxpre = cw[CONV_W - 1:CONV_W, :] * dxb
        dcw_ref[CONV_W - 1:CONV_W, :] += jnp.sum(dxb * pad[pl.ds(SUBLANES, t), :], axis=0, keepdims=True)
        for dlt in range(1, CONV_W):
            dxpre = dxpre + cw[CONV_W - 1 - dlt:CONV_W - dlt, :] * dpad[pl.ds(dlt, t), :]
            dcw_ref[CONV_W - 1 - dlt:CONV_W - dlt, :] += jnp.sum(
                dxb * pad[pl.ds(SUBLANES - dlt, t), :], axis=0, keepdims=True)
        dpad[pl.ds(t, SUBLANES), :] = dxb[0:SUBLANES, :]
        dxp_ref[...] = dxpre.astype(BF16)

    rev = lambda i: nt - 1 - i
    blk, vec, wspec, cwspec = _lru_specs(t, bw, nb, rev)
    halo = pl.BlockSpec((SUBLANES, bw), lambda n, i: (jnp.maximum(rev(i) * ngroups - 1, 0), n))
    return pl.pallas_call(
        body, name=name, grid=(nb, nt),
        in_specs=[blk(0), halo, blk(nb), blk(0), halo, blk(0), cwspec, vec, wspec, vec, wspec, vec, vec],
        out_specs=[blk(0), blk(0), cwspec, vec, wspec, vec, wspec, vec, vec],
        out_shape=[jax.ShapeDtypeStruct((s, rr), BF16), jax.ShapeDtypeStruct((s, rr), BF16),
                   jax.ShapeDtypeStruct((CONV_W, rr), F32), jax.ShapeDtypeStruct((1, rr), F32),
                   jax.ShapeDtypeStruct((nb, bw, bw), F32), jax.ShapeDtypeStruct((1, rr), F32),
                   jax.ShapeDtypeStruct((nb, bw, bw), F32), jax.ShapeDtypeStruct((1, rr), F32),
                   jax.ShapeDtypeStruct((1, rr), F32)],
        scratch_shapes=[pltpu.VMEM((t + SUBLANES, bw), F32), pltpu.VMEM((t + SUBLANES, bw), F32),
                        pltpu.VMEM((t + SUBLANES, bw), F32), pltpu.VMEM((1, bw), F32),
                        pltpu.VMEM((t, bw), F32), pltpu.VMEM((t, bw), F32), pltpu.VMEM((t, bw), F32)],
        compiler_params=_params(("parallel", "arbitrary")),
    )(proj, proj, proj, hst, hst, dm, conv_w, conv_b, w_r, b_r, w_i, b_i, lam)


def _softplus(z):
    return jnp.maximum(z, 0.0) + jnp.log(1.0 + jnp.exp2(jnp.abs(z) * (-LOG2E)))


def _att_blocks(s):
    bk = ATT_BLOCK if s % ATT_BLOCK == 0 else s
    bq = ATT_QTILES * bk if s % (ATT_QTILES * bk) == 0 else bk
    return bk, bq


def attn_fwd(projb, kv, *, name):
    s, a2 = projb.shape
    a = a2 // 2
    nh = a // HEAD_DIM
    bk, bq = _att_blocks(s)
    r = bq // bk
    nq = s // bq
    assert s // bk <= LANES
    scale = 1.0 / math.sqrt(HEAD_DIM)
    nt_dims = (((1,), (1,)), ((), ()))
    hp = ATT_HEADS_FWD if nh % ATT_HEADS_FWD == 0 else 1
    wd = hp * HEAD_DIM

    def body(q_ref, g_ref, k_ref, v_ref, m_ref, o_ref, off_ref, acc):
        i = pl.program_id(1)
        qb = (q_ref[...] * scale).astype(BF16)
        from_mat = (lax.broadcasted_iota(jnp.int32, (bk, bk), 0)
                    >= lax.broadcasted_iota(jnp.int32, (bk, bk), 1)).astype(BF16)
        rowi = lax.broadcasted_iota(jnp.int32, (bq, bk), 0)
        coli = lax.broadcasted_iota(jnp.int32, (bq, bk), 1)
        lane = lax.broadcasted_iota(jnp.int32, (bq, LANES), 1)
        cols = [slice(hh * HEAD_DIM, (hh + 1) * HEAD_DIM) for hh in range(hp)]

        def tile(j, carries, diag):
            causal = None if diag is None else (coli + diag * bk) < rowi
            rows = pl.ds(pl.multiple_of(j * bk, bk), bk)
            zs = [lax.dot_general(qb[:, c], k_ref[rows, c], nt_dims, preferred_element_type=F32) for c in cols]
            sums = []
            for z in zs:
                sp = _softplus(z)
                if causal is not None:
                    sp = jnp.where(causal, sp, 0.0)
                sums.append(jnp.dot(sp.astype(BF16), from_mat, preferred_element_type=F32))
            out = []
            for hh in range(hp):
                w = jnp.exp(zs[hh] - sums[hh] - carries[hh])
                if causal is not None:
                    w = jnp.where(causal, w, 0.0)
                acc[:, cols[hh]] += jnp.dot(w.astype(BF16), v_ref[rows, cols[hh]], preferred_element_type=F32)
                off_ref[hh] = jnp.where(lane == j, carries[hh], off_ref[hh])
                out.append(carries[hh] + sums[hh][:, 0:1])
            return tuple(out)

        acc[...] = jnp.zeros_like(acc)
        off_ref[...] = jnp.zeros_like(off_ref)
        carries = tuple(jnp.zeros((bq, 1), F32) for _ in range(hp))
        for dg in reversed(range(r)):
            carries = tile(r * i + dg, carries, dg)
        lax.fori_loop(0, r * i, lambda jj, c: tile(r * i - 1 - jj, c, None), carries)
        o = acc[...]
        o_ref[...] = o
        gate = g_ref[...]
        m_ref[...] = (o * (gate * _sigmoid(gate))).astype(BF16)

    ng = nh // hp
    qspec = lambda c0: pl.BlockSpec((bq, wd), lambda h, i, c0=c0: (i, c0 + h))
    kspec = lambda c0: pl.BlockSpec((s, wd), lambda h, i, c0=c0: (0, c0 + h), pipeline_mode=pl.Buffered(1))
    return pl.pallas_call(
        body, name=name, grid=(ng, nq),
        in_specs=[qspec(0), qspec(ng), kspec(0), kspec(ng)],
        out_specs=[qspec(0), qspec(0), pl.BlockSpec((hp, None, bq, LANES), lambda h, i: (h, i, 0, 0))],
        out_shape=[jax.ShapeDtypeStruct((s, a), BF16), jax.ShapeDtypeStruct((s, a), F32),
                   jax.ShapeDtypeStruct((nh, nq, bq, LANES), F32)],
        scratch_shapes=[pltpu.VMEM((bq, wd), F32)],
        compiler_params=_params(("parallel", "arbitrary")),
    )(projb, projb, kv, kv)


def attn_bwd(projb, do, kv, offs, *, name):
    s, a2 = projb.shape
    a = a2 // 2
    nh = a // HEAD_DIM
    bk, bq = _att_blocks(s)
    r = bq // bk
    nq = s // bq
    scale = 1.0 / math.sqrt(HEAD_DIM)
    nt_dims = (((1,), (1,)), ((), ()))
    tn_dims = (((0,), (0,)), ((), ()))
    hp = ATT_HEADS_BWD if nh % ATT_HEADS_BWD == 0 else 1
    wd = hp * HEAD_DIM

    def body(q_ref, do_ref, k_ref, v_ref, off_ref, dq_ref, dk_ref, dv_ref, dk_acc, dv_acc, dq_acc):
        i = pl.program_id(1)

        @pl.when(i == 0)
        def _():
            dk_acc[...] = jnp.zeros_like(dk_acc)
            dv_acc[...] = jnp.zeros_like(dv_acc)

        qb = (q_ref[...] * scale).astype(BF16)
        dob = do_ref[...]
        ki = lax.broadcasted_iota(jnp.int32, (bk, bk), 0)
        kj = lax.broadcasted_iota(jnp.int32, (bk, bk), 1)
        from_mat = (kj >= ki).astype(BF16)
        upto_mat = (kj <= ki).astype(BF16)
        rowi = lax.broadcasted_iota(jnp.int32, (bk, bq), 0)
        coli = lax.broadcasted_iota(jnp.int32, (bk, bq), 1)
        sub = lax.broadcasted_iota(jnp.int32, (LANES, bq), 0)
        offs_t = [off_ref[hh].T for hh in range(hp)]
        cols = [slice(hh * HEAD_DIM, (hh + 1) * HEAD_DIM) for hh in range(hp)]
        dq_acc[...] = jnp.zeros_like(dq_acc)

        def tile(j, gcarries, diag):
            causal = None if diag is None else (rowi + diag * bk) < coli
            rows = pl.ds(pl.multiple_of(j * bk, bk), bk)
            zs = [lax.dot_general(k_ref[rows, c], qb[:, c], nt_dims, preferred_element_type=F32) for c in cols]
            dws = [lax.dot_general(v_ref[rows, c], dob[:, c], nt_dims, preferred_element_type=F32) for c in cols]
            sigs, sums = [], []
            for z in zs:
                sp = _softplus(z)
                sigs.append(jnp.exp(z - sp))
                if causal is not None:
                    sp = jnp.where(causal, sp, 0.0)
                sums.append(jnp.dot(from_mat, sp.astype(BF16), preferred_element_type=F32))
            gs, totals = [], []
            for hh in range(hp):
                offj = jnp.sum(jnp.where(sub == j, offs_t[hh], 0.0), axis=0, keepdims=True)
                w = jnp.exp(zs[hh] - sums[hh] - offj)
                if causal is not None:
                    w = jnp.where(causal, w, 0.0)
                g = w * dws[hh]
                dv_acc[rows, cols[hh]] += jnp.dot(w.astype(BF16), dob[:, cols[hh]], preferred_element_type=F32)
                totals.append(jnp.dot(upto_mat, g.astype(BF16), preferred_element_type=F32))
                gs.append(g)
            out = []
            for hh in range(hp):
                dz = gs[hh] - (totals[hh] + gcarries[hh]) * sigs[hh]
                if causal is not None:
                    dz = jnp.where(causal, dz, 0.0)
                dzb = dz.astype(BF16)
                dk_acc[rows, cols[hh]] += jnp.dot(dzb, qb[:, cols[hh]], preferred_element_type=F32)
                dq_acc[:, cols[hh]] += lax.dot_general(dzb, k_ref[rows, cols[hh]], tn_dims,
                                                       preferred_element_type=F32)
                out.append(gcarries[hh] + totals[hh][bk - 1:bk, :])
            return tuple(out)

        init = tuple(jnp.zeros((1, bq), F32) for _ in range(hp))
        gcarries = lax.fori_loop(0, r * i, lambda j, c: tile(j, c, None), init)
        for dg in range(r):
            gcarries = tile(r * i + dg, gcarries, dg)
        dq_ref[...] = (dq_acc[...] * scale).astype(BF16)

        @pl.when(i == nq - 1)
        def _():
            dk_ref[...] = dk_acc[...].astype(BF16)
            dv_ref[...] = dv_acc[...].astype(BF16)

    ng = nh // hp
    once = pl.Buffered(1)
    qspec = lambda c0: pl.BlockSpec((bq, wd), lambda h, i, c0=c0: (i, c0 + h))
    kspec = lambda c0: pl.BlockSpec((s, wd), lambda h, i, c0=c0: (0, c0 + h), pipeline_mode=once)
    return pl.pallas_call(
        body, name=name, grid=(ng, nq),
        in_specs=[qspec(0), qspec(0), kspec(0), kspec(ng),
                  pl.BlockSpec((hp, None, bq, LANES), lambda h, i: (h, i, 0, 0))],
        out_specs=[qspec(0), kspec(0), kspec(0)],
        out_shape=[jax.ShapeDtypeStruct((s, a), BF16)] * 3,
        scratch_shapes=[pltpu.VMEM((s, wd), F32), pltpu.VMEM((s, wd), F32), pltpu.VMEM((bq, wd), F32)],
        compiler_params=_params(("parallel", "arbitrary")),
    )(projb, do, kv, kv, offs)


def gate_bwd(dm, o, projb, *, name):
    s, a = dm.shape
    tr = _pick(s, (512, 256, 128, 8))

    def body(dm_ref, o_ref, g_ref, do_ref, dg_ref):
        gate = g_ref[...]
        sg = _sigmoid(gate)
        dmv = dm_ref[...]
        do_ref[...] = (dmv * (gate * sg)).astype(BF16)
        dg_ref[...] = (dmv * o_ref[...] * (sg * (1.0 + gate * (1.0 - sg)))).astype(BF16)

    row = pl.BlockSpec((tr, a), lambda i: (i, 0))
    return pl.pallas_call(
        body, name=name, grid=(s // tr,),
        in_specs=[row, row, pl.BlockSpec((tr, a), lambda i: (i, 1))],
        out_specs=[row, row],
        out_shape=[jax.ShapeDtypeStruct((s, a), BF16)] * 2,
        compiler_params=_params(("parallel",)),
    )(dm, o, projb)


def _as2d(x):
    n = x.size
    cols = x.shape[-1]
    if cols % LANES != 0:
        cols = LANES
    return x.reshape(n // cols, cols)


def sum_parts(parts, *, name):
    p, rows, cols = parts.shape
    tr = _pick(rows, (512, 256, 128, 64, 32, 16))

    def body(p_ref, o_ref):
        acc = p_ref[0].astype(F32)
        for k in range(1, p):
            acc = acc + p_ref[k].astype(F32)
        o_ref[...] = acc

    return pl.pallas_call(
        body, name=name, grid=(rows // tr,),
        in_specs=[pl.BlockSpec((p, tr, cols), lambda i: (0, i, 0))],
        out_specs=pl.BlockSpec((tr, cols), lambda i: (i, 0)),
        out_shape=jax.ShapeDtypeStruct((rows, cols), F32),
        compiler_params=_params(("parallel",)),
    )(parts)


def adamw(w, g_parts, m, v, *, name):
    rows, cols = w.shape
    tr = _pick(rows, (128, 64, 32, 16, 8))
    np_ = len(g_parts)
    c1 = 1.0 / (1.0 - ADAM_B1 ** ADAM_STEP)
    c2 = 1.0 / (1.0 - ADAM_B2 ** ADAM_STEP)

    def body(*refs):
        w_ref, m_ref, v_ref = refs[0], refs[1], refs[2]
        g_refs = refs[3:3 + np_]
        go_ref, d_ref, mo_ref, vo_ref = refs[3 + np_:]
        g = g_refs[0][...]
        for gr in g_refs[1:]:
            g = g + gr[...]
        mn = ADAM_B1 * m_ref[...] + (1.0 - ADAM_B1) * g
        vn = ADAM_B2 * v_ref[...] + (1.0 - ADAM_B2) * (g * g)
        go_ref[...] = g
        mo_ref[...] = mn
        vo_ref[...] = vn
        d_ref[...] = (-ADAM_LR) * ((mn * c1) / (jnp.sqrt(vn * c2) + ADAM_EPS) + ADAM_WD * w_ref[...])

    spec = pl.BlockSpec((tr, cols), lambda i: (i, 0))
    return pl.pallas_call(
        body, name=name, grid=(rows // tr,),
        in_specs=[spec] * (3 + np_), out_specs=[spec] * 4,
        out_shape=[jax.ShapeDtypeStruct((rows, cols), F32)] * 4,
        compiler_params=_params(("parallel",)),
    )(w, m, v, *g_parts)


def exchange(kind, arrays, axes, *, name):
    na = len(arrays)
    hbm = pl.BlockSpec(memory_space=pl.ANY)

    def body(*refs):
        pairs = _exchange(kind, refs[:na], refs[na:2 * na], axes, *refs[2 * na:])
        for start, _ in pairs:
            start()
        for _, finish in pairs:
            finish()

    return pl.pallas_call(
        body, name=name, in_specs=[hbm] * na, out_specs=[hbm] * na,
        out_shape=_exchange_shapes(kind, arrays, axes), scratch_shapes=_exchange_sems(na),
    )(*arrays)


def swap_cores(arrs, *, name):
    na = len(arrs)
    hbm = pl.BlockSpec(memory_space=pl.ANY)

    def body(*refs):
        a_refs = refs[:na]
        o_refs = refs[na:2 * na]
        send_sems, recv_sems = refs[2 * na:]
        x, y, c = _place()
        copies = []
        for ai in range(na):
            cp = pltpu.make_async_remote_copy(
                src_ref=a_refs[ai], dst_ref=o_refs[ai], send_sem=send_sems.at[ai], recv_sem=recv_sems.at[ai],
                device_id=(x, y, 1 - c), device_id_type=MESH)
            cp.start()
            copies.append(cp)
        for cp in copies:
            cp.wait()

    return pl.pallas_call(
        body, name=name,
        in_specs=[hbm] * na, out_specs=[hbm] * na,
        out_shape=[jax.ShapeDtypeStruct(a.shape, a.dtype) for a in arrs],
        scratch_shapes=[pltpu.SemaphoreType.DMA((na,)), pltpu.SemaphoreType.DMA((na,))],
    )(*arrs)


def allreduce_small(buf, *, name):
    rows, cols = buf.shape

    def body(b_ref, o_ref, slots, send_sems, recv_sems):
        x, y, c = _place()
        me = 4 * x + 2 * y + c
        slots[0] = b_ref[...]
        copies = []
        for rel in range(1, 8):
            peer = (x ^ (rel >> 2), y ^ ((rel >> 1) & 1), c ^ (rel & 1))
            cp = pltpu.make_async_remote_copy(
                src_ref=b_ref, dst_ref=slots.at[rel], send_sem=send_sems.at[rel - 1],
                recv_sem=recv_sems.at[rel - 1], device_id=peer, device_id_type=MESH)
            cp.start()
            copies.append(cp)
        for cp in copies:
            cp.wait()
        acc = slots[me]
        for dev in range(1, 8):
            acc = acc + slots[dev ^ me]
        o_ref[...] = acc

    vm = pl.BlockSpec(memory_space=pltpu.VMEM)
    return pl.pallas_call(
        body, name=name, in_specs=[vm], out_specs=vm,
        out_shape=jax.ShapeDtypeStruct((rows, cols), F32),
        scratch_shapes=[pltpu.VMEM((8, rows, cols), F32), pltpu.SemaphoreType.DMA((7,)),
                        pltpu.SemaphoreType.DMA((7,))],
    )(buf)


def _pack_rows(arrs):
    parts = []
    for a in arrs:
        p = a.reshape(-1, LANES)
        parts.append(jnp.pad(p, ((0, (-p.shape[0]) % SUBLANES), (0, 0))))
    return jnp.concatenate(parts, axis=0)


def _unpack_rows(buf, shapes):
    out, r0 = [], 0
    for shp in shapes:
        n = math.prod(shp) // LANES
        out.append(buf[r0:r0 + n].reshape(shp))
        r0 += n + (-n) % SUBLANES
    return out


def kernel(x, a_norm, a_w_in, a_conv_w, a_conv_b, a_w_r, a_b_r, a_w_i, a_b_i, a_lambda, a_w_out, kv_norm, w_kv, b_norm, b_w_in, b_w_out, final_norm, loss_target, m_a_norm, m_a_w_in, m_a_conv_w, m_a_conv_b, m_a_w_r, m_a_b_r, m_a_w_i, m_a_b_i, m_a_lambda, m_a_w_out, m_kv_norm, m_w_kv, m_b_norm, m_b_w_in, m_b_w_out, m_final_norm, v_a_norm, v_a_w_in, v_a_conv_w, v_a_conv_b, v_a_w_r, v_a_b_r, v_a_w_i, v_a_b_i, v_a_lambda, v_a_w_out, v_kv_norm, v_w_kv, v_b_norm, v_b_w_in, v_b_w_out, v_final_norm):
    weights = dict(a_norm=a_norm, a_w_in=a_w_in, a_conv_w=a_conv_w, a_conv_b=a_conv_b, a_w_r=a_w_r, a_b_r=a_b_r,
                   a_w_i=a_w_i, a_b_i=a_b_i, a_lambda=a_lambda, a_w_out=a_w_out, kv_norm=kv_norm, w_kv=w_kv,
                   b_norm=b_norm, b_w_in=b_w_in, b_w_out=b_w_out, final_norm=final_norm)
    mom1 = dict(a_norm=m_a_norm, a_w_in=m_a_w_in, a_conv_w=m_a_conv_w, a_conv_b=m_a_conv_b, a_w_r=m_a_w_r,
                a_b_r=m_a_b_r, a_w_i=m_a_w_i, a_b_i=m_a_b_i, a_lambda=m_a_lambda, a_w_out=m_a_w_out,
                kv_norm=m_kv_norm, w_kv=m_w_kv, b_norm=m_b_norm, b_w_in=m_b_w_in, b_w_out=m_b_w_out,
                final_norm=m_final_norm)
    mom2 = dict(a_norm=v_a_norm, a_w_in=v_a_w_in, a_conv_w=v_a_conv_w, a_conv_b=v_a_conv_b, a_w_r=v_a_w_r,
                a_b_r=v_a_b_r, a_w_i=v_a_w_i, a_b_i=v_a_b_i, a_lambda=v_a_lambda, a_w_out=v_a_w_out,
                kv_norm=v_kv_norm, w_kv=v_w_kv, b_norm=v_b_norm, b_w_in=v_b_w_in, b_w_out=v_b_w_out,
                final_norm=v_final_norm)
    order = list(weights)
    x0 = x[0]
    target = loss_target[0]
    d = x0.shape[1]
    chip = 2 * lax.axis_index("x") + lax.axis_index("y")

    big = ["a_w_in", "a_w_r", "a_w_i", "a_w_out", "w_kv", "b_w_in", "b_w_out"]
    big_axis = dict(a_w_in=1, a_w_r=1, a_w_i=1, a_w_out=0, w_kv=1, b_w_in=1, b_w_out=0)
    local = dict(a_w_in=a_w_in[0], a_w_r=a_w_r[0], a_w_i=a_w_i[0], a_w_out=a_w_out[0], w_kv=w_kv,
                 b_w_in=b_w_in[0], b_w_out=b_w_out[0])
    shards = {n: local[n].astype(BF16) for n in big}
    first = ["a_w_in", "a_w_r", "a_w_i"]
    full = exchange("gather", [shards[n] for n in first] + [a_conv_w[0], b_norm],
                    [big_axis[n] for n in first] + [1, 1], name="gather_first")
    wf = dict(zip(first + ["a_conv_w", "b_norm"], full))
    wf.update(a_norm=a_norm, a_conv_b=a_conv_b, a_b_r=a_b_r, a_b_i=a_b_i, a_lambda=a_lambda,
              kv_norm=kv_norm.reshape(1, d), final_norm=final_norm.reshape(1, d))
    loss_part, grad_x, parts, gsmall = _local_grads(x0, target, wf, shards=shards, axes=big_axis)

    sums = [sum_parts(parts[n].reshape(4, *_as2d(parts[n][0]).shape), name="sum_" + n) for n in big]
    others = swap_cores(sums, name="swap_cores")

    small = ["a_norm", "a_conv_b", "a_b_r", "a_b_i", "a_lambda", "kv_norm", "final_norm", "a_conv_w", "b_norm"]
    buf = _pack_rows([gsmall[n] for n in small] + [loss_part])
    red = allreduce_small(buf, name="allreduce_small")
    red_list = _unpack_rows(red, [gsmall[n].shape for n in small] + [(1, LANES)])
    gs = dict(zip(small, red_list[:-1]))
    loss = red_list[-1][0, 0]
    n_conv = a_conv_w.shape[2]
    gs["a_conv_w"] = lax.dynamic_slice_in_dim(gs["a_conv_w"], chip * n_conv, n_conv, axis=1)
    n_bn = b_norm.shape[1]
    gs["b_norm"] = lax.dynamic_slice_in_dim(gs["b_norm"], chip * n_bn, n_bn, axis=1)

    grads, deltas, new_m, new_v = {}, {}, {}, {}
    for n, s_mine, s_other in zip(big, sums, others):
        shp = weights[n].shape
        g, dlt, mn, vn = adamw(_as2d(weights[n]), [s_mine, s_other], _as2d(mom1[n]), _as2d(mom2[n]),
                               name="adamw_" + n)
        grads[n], deltas[n], new_m[n], new_v[n] = (t.reshape(shp) for t in (g, dlt, mn, vn))
    shapes = [weights[n].shape for n in small]
    wpk, gpk, mpk, vpk = (_pack_rows([src[n] for n in small]) for src in (weights, gs, mom1, mom2))
    outs = adamw(wpk, [gpk], mpk, vpk, name="adamw_small")
    for dst, packed in zip((grads, deltas, new_m, new_v), outs):
        for n, val in zip(small, _unpack_rows(packed, shapes)):
            dst[n] = val

    return (loss, grad_x[None], *[grads[n] for n in order], *[deltas[n] for n in order],
            *[new_m[n] for n in order], *[new_v[n] for n in order])


def _local_grads(x0, target, wf, shards=None, axes=None):
    a_norm, a_conv_b, a_b_r, a_b_i, a_lambda = (wf[n] for n in ("a_norm", "a_conv_b", "a_b_r", "a_b_i", "a_lambda"))
    kv_norm, final_norm = wf["kv_norm"], wf["final_norm"]
    wf = dict(wf)
    parts = {}

    def mm(*args, gather=(), scatter=None, **kw):
        if shards is None or not (gather or scatter):
            return matmul(*args, **kw)
        if gather:
            out, got = matmul(*args, exchange=("gather", [shards[n] for n in gather], [axes[n] for n in gather]), **kw)
            wf.update(zip(gather, got))
        else:
            out, got = matmul(*args, exchange=("scatter", list(scatter.values()), [axes[n] for n in scatter]), **kw)
            parts.update(zip(scatter, got))
        return out

    (h_a,) = rms_fwd(x0, [a_norm], name="norm_a")
    proj_a = mm(h_a, wf["a_w_in"], gather=("a_w_out", "w_kv"), name="a_in")
    m_a, hst = lru_fwd(proj_a, wf["a_conv_w"], a_conv_b, wf["a_w_r"], a_b_r, wf["a_w_i"], a_b_i, a_lambda,
                       name="lru_fwd")
    x1 = mm(m_a, wf["a_w_out"], residual=x0, gather=("b_w_in",), name="a_out")
    kvn, hb = rms_fwd(x1, [kv_norm, wf["b_norm"]], name="norm_kv_b")
    kv = mm(kvn, wf["w_kv"], out_dtype=BF16, gather=("b_w_out",), name="kv_proj")
    proj_b = mm(hb, wf["b_w_in"], name="b_in")
    m_b, o, offs = attn_fwd(proj_b, kv, name="attn_fwd")
    x2 = mm(m_b, wf["b_w_out"], residual=x1, name="b_out")
    loss_part, g_final, dx2, dx2b = loss_bwd(x2, target, final_norm, name="loss_bwd")

    dm_b = mm(dx2b, wf["b_w_out"], tb=True, name="b_out_dx")
    g_b_w_out = mm(m_b, dx2b, ta=True, out_dtype=BF16, name="b_out_dw")
    do, dgate_b = gate_bwd(dm_b, o, proj_b, name="gate_bwd")
    dq, dk, dv = attn_bwd(proj_b, do, kv, offs, name="attn_bwd")
    dproj_b = jnp.concatenate([dq, dgate_b], axis=1)
    dkv = jnp.concatenate([dk, dv], axis=1)
    g_b_w_in = mm(hb, dproj_b, ta=True, out_dtype=BF16, scatter=dict(b_w_out=g_b_w_out), name="b_in_dw")
    g_w_kv = mm(kvn, dkv, ta=True, out_dtype=BF16, scatter=dict(b_w_in=g_b_w_in), name="kv_dw")
    dhb = mm(dproj_b, wf["b_w_in"], tb=True, scatter=dict(w_kv=g_w_kv), name="b_in_dx")
    dkvn = mm(dkv, wf["w_kv"], tb=True, name="kv_dx")
    dx1, dx1b, (g_kv_norm, g_b_norm) = rms_bwd(
        x1, dx2, [(kv_norm, dkvn), (wf["b_norm"], dhb)], name="norm_kv_b_bwd")

    g_a_w_out = mm(m_a, dx1b, ta=True, out_dtype=BF16, name="a_out_dw")
    dm_a = mm(dx1b, wf["a_w_out"], tb=True, scatter=dict(a_w_out=g_a_w_out), name="a_out_dx")
    dxpre, dgate_a, g_conv_w, g_conv_b, g_w_r, g_b_r, g_w_i, g_b_i, g_lambda = lru_bwd(
        proj_a, hst, dm_a, wf["a_conv_w"], a_conv_b, wf["a_w_r"], a_b_r, wf["a_w_i"], a_b_i, a_lambda,
        name="lru_bwd")
    dproj_a = jnp.concatenate([dxpre, dgate_a], axis=1)
    g_w_r, g_w_i = g_w_r.astype(BF16), g_w_i.astype(BF16)
    g_a_w_in = mm(h_a, dproj_a, ta=True, out_dtype=BF16, scatter=dict(a_w_r=g_w_r, a_w_i=g_w_i), name="a_in_dw")
    dh_a = mm(dproj_a, wf["a_w_in"], tb=True, scatter=dict(a_w_in=g_a_w_in), name="a_in_dx")
    grad_x, _, (g_a_norm,) = rms_bwd(x0, dx1, [(a_norm, dh_a)], name="norm_a_bwd")

    gbig = parts if shards is not None else dict(
        a_w_in=g_a_w_in, a_w_r=g_w_r, a_w_i=g_w_i, a_w_out=g_a_w_out, w_kv=g_w_kv, b_w_in=g_b_w_in, b_w_out=g_b_w_out)
    gsmall = dict(a_norm=g_a_norm, a_conv_b=g_conv_b, a_b_r=g_b_r, a_b_i=g_b_i, a_lambda=g_lambda,
                  kv_norm=g_kv_norm, final_norm=g_final, a_conv_w=g_conv_w, b_norm=g_b_norm)
    return loss_part, grad_x, gbig, gsmall
```

```python
import math

import jax
import jax.numpy as jnp
from jax import lax
from jax.experimental import pallas as pl
from jax.experimental.pallas import tpu as pltpu

F32 = jnp.float32
BF16 = jnp.bfloat16
MESH = pl.DeviceIdType.MESH

EPS = 1e-6
LRU_C = 8.0
CONV_W = 4
HEAD_DIM = 128
ADAM_LR = 0.001
ADAM_B1 = 0.9
ADAM_B2 = 0.999
ADAM_EPS = 1e-08
ADAM_WD = 0.01
ADAM_STEP = 10

V7X_VMEM_LIMIT = 56 * 1024 * 1024
LANES = 128
SUBLANES = 8
ATT_BLOCK = 256
ATT_QTILES = 2
ATT_HEADS_FWD = 2
ATT_HEADS_BWD = 2
LOG2E = 1.4426950408889634
SCAN_UNROLL = 4


def _pick(dim, cands):
    for c in cands:
        if dim % c == 0:
            return c
    return dim


def _params(sem, vmem=V7X_VMEM_LIMIT):
    return pltpu.CompilerParams(dimension_semantics=sem, vmem_limit_bytes=vmem)


def _sigmoid(x):
    return 1.0 / (1.0 + jnp.exp(-x))


def _place():
    return lax.axis_index("x"), lax.axis_index("y"), lax.axis_index("c")


def _chip_peers(x, y, c):
    return [(1 - x, y, c), (x, 1 - y, c), (1 - x, 1 - y, c)]


def _shard_of(ref, axis, idx, n):
    start = pl.multiple_of(idx * n, n)
    sl = [slice(None)] * len(ref.shape)
    sl[axis] = pl.ds(start, n)
    return ref.at[tuple(sl)]


def _exchange(kind, in_refs, out_refs, axes, send_sems, recv_sems, local_sems):
    x, y, c = _place()
    me = 2 * x + y
    peers = _chip_peers(x, y, c)
    pairs = []
    for ai, (src, dst, ax) in enumerate(zip(in_refs, out_refs, axes)):
        if kind == "gather":
            n = src.shape[ax]
            mine = _shard_of(dst, ax, me, n)
            loc = pltpu.make_async_copy(src, mine, local_sems.at[ai])
        else:
            n = dst.shape[1 + ax]
            loc = pltpu.make_async_copy(_shard_of(src, ax, me, n), dst.at[0], local_sems.at[ai])
        pairs.append((loc.start, loc.wait))
        for k, peer in enumerate(peers):
            sem = dict(send_sem=send_sems.at[ai * 3 + k], recv_sem=recv_sems.at[ai * 3 + k],
                       device_id=peer, device_id_type=MESH)
            theirs = 2 * peer[0] + peer[1]
            if kind == "gather":
                snd = pltpu.make_async_remote_copy(src_ref=src, dst_ref=mine, **sem)
                rcv = pltpu.make_async_remote_copy(src_ref=src, dst_ref=_shard_of(dst, ax, theirs, n), **sem)
            else:
                snd = pltpu.make_async_remote_copy(src_ref=_shard_of(src, ax, theirs, n), dst_ref=dst.at[1 + k], **sem)
                rcv = snd
            pairs.append((snd.start, lambda snd=snd, rcv=rcv: (snd.wait_send(), rcv.wait_recv())))
    return pairs


def _exchange_shapes(kind, arrays, axes):
    out = []
    for arr, ax in zip(arrays, axes):
        shp = list(arr.shape)
        if kind == "gather":
            shp[ax] *= 4
            out.append(jax.ShapeDtypeStruct(tuple(shp), arr.dtype))
        else:
            shp[ax] //= 4
            out.append(jax.ShapeDtypeStruct((4, *shp), arr.dtype))
    return out


def _exchange_sems(n):
    return [pltpu.SemaphoreType.DMA((3 * n,)), pltpu.SemaphoreType.DMA((3 * n,)), pltpu.SemaphoreType.DMA((n,))]


def matmul(a, b, *, ta=False, tb=False, out_dtype=F32, residual=None, exchange=None, name):
    a_pair = a if isinstance(a, (tuple, list)) else None
    b_pair = b if isinstance(b, (tuple, list)) else None
    assert not (a_pair and ta) and not (b_pair and tb) and not (a_pair and b_pair)
    a0 = a_pair[0] if a_pair else a
    b0 = b_pair[0] if b_pair else b
    m = a0.shape[1] if ta else a0.shape[0]
    kdim = (a0.shape[0] if ta else a0.shape[1]) * (2 if a_pair else 1)
    n = (b0.shape[0] if tb else b0.shape[1]) * (2 if b_pair else 1)
    assert (b0.shape[1] if tb else b0.shape[0]) == kdim
    tm = _pick(m, (1024, 640, 512, 256, 128))
    tn = _pick(n // 2 if b_pair else n, (1024, 640, 512, 256, 128))
    tk = _pick(kdim // 2 if a_pair else kdim, (2560, 2048, 1024, 512, 256, 128))
    grid = (m // tm, n // tn, kdim // tk)
    nk = grid[2]
    kh, jh = nk // 2, grid[1] // 2
    dn = (((0 if ta else 1,), (1 if tb else 0,)), ((), ()))
    na = 2 if a_pair else 1
    nb = 2 if b_pair else 1
    nres = 0 if residual is None else 1
    nex = 0 if exchange is None else len(exchange[1])

    def body(*refs):
        a_refs, b_refs = refs[:na], refs[na:na + nb]
        p = na + nb
        r_ref = refs[p] if nres else None
        ex_in = refs[p + nres:p + nres + nex]
        o_ref = refs[p + nres + nex]
        ex_out = refs[p + 1 + nres + nex:p + 1 + nres + 2 * nex]
        acc = refs[p + 1 + nres + 2 * nex]
        sems = refs[p + 2 + nres + 2 * nex:]
        i, j, k = pl.program_id(0), pl.program_id(1), pl.program_id(2)
        if nex:
            @pl.when((i == 0) & (j == 0) & (k == 0))
            def _():
                for start, _ in _exchange(exchange[0], ex_in, ex_out, exchange[2], *sems):
                    start()

        @pl.when(k == 0)
        def _():
            acc[...] = jnp.zeros_like(acc)

        def accumulate(a_ref, b_ref):
            acc[...] += lax.dot_general(a_ref[...].astype(BF16), b_ref[...].astype(BF16), dn,
                                        preferred_element_type=F32)

        if a_pair:
            pl.when(k < kh)(lambda: accumulate(a_refs[0], b_refs[0]))
            pl.when(k >= kh)(lambda: accumulate(a_refs[1], b_refs[0]))
        elif b_pair:
            pl.when(j < jh)(lambda: accumulate(a_refs[0], b_refs[0]))
            pl.when(j >= jh)(lambda: accumulate(a_refs[0], b_refs[1]))
        else:
            accumulate(a_refs[0], b_refs[0])

        @pl.when(k == nk - 1)
        def _():
            r = acc[...]
            if r_ref is not None:
                r = r + r_ref[...]
            o_ref[...] = r.astype(out_dtype)

        if nex:
            @pl.when((i == grid[0] - 1) & (j == grid[1] - 1) & (k == nk - 1))
            def _():
                for _, finish in _exchange(exchange[0], ex_in, ex_out, exchange[2], *sems):
                    finish()

    if a_pair:
        a_specs = [pl.BlockSpec((tm, tk), lambda i, j, k: (i, jnp.minimum(k, kh - 1))),
                   pl.BlockSpec((tm, tk), lambda i, j, k: (i, jnp.maximum(k - kh, 0)))]
    else:
        a_specs = [pl.BlockSpec((tk, tm), lambda i, j, k: (k, i)) if ta
                   else pl.BlockSpec((tm, tk), lambda i, j, k: (i, k))]
    if b_pair:
        b_specs = [pl.BlockSpec((tk, tn), lambda i, j, k: (jnp.where(j < jh, k, nk - 1), jnp.minimum(j, jh - 1))),
                   pl.BlockSpec((tk, tn), lambda i, j, k: (jnp.where(j >= jh, k, 0), jnp.maximum(j - jh, 0)))]
    else:
        b_specs = [pl.BlockSpec((tn, tk), lambda i, j, k: (j, k)) if tb
                   else pl.BlockSpec((tk, tn), lambda i, j, k: (k, j))]
    o_spec = pl.BlockSpec((tm, tn), lambda i, j, k: (i, j))
    hbm = pl.BlockSpec(memory_space=pl.ANY)
    in_specs = a_specs + b_specs + [o_spec] * nres + [hbm] * nex
    args = (list(a_pair) if a_pair else [a]) + (list(b_pair) if b_pair else [b])
    args += ([residual] if nres else []) + (list(exchange[1]) if nex else [])
    out_shape = [jax.ShapeDtypeStruct((m, n), out_dtype)]
    scratch = [pltpu.VMEM((tm, tn), F32)]
    if nex:
        out_shape += _exchange_shapes(*exchange)
        scratch += _exchange_sems(nex)
    outs = pl.pallas_call(
        body, name=name, grid=grid,
        in_specs=in_specs, out_specs=[o_spec] + [hbm] * nex, out_shape=out_shape,
        scratch_shapes=scratch,
        compiler_params=_params(("arbitrary",) * 3 if nex else ("parallel", "parallel", "arbitrary")),
    )(*args)
    return (outs[0], list(outs[1:])) if nex else outs[0]


def rms_fwd(x, gains, *, name):
    s, d = x.shape
    tr = _pick(s, (512, 256, 128, 8))
    ng = len(gains)

    def body(*refs):
        x_ref = refs[0]
        g_refs = refs[1:1 + ng]
        o_refs = refs[1 + ng:]
        xv = x_ref[...]
        y = xv * lax.rsqrt(jnp.mean(xv * xv, axis=-1, keepdims=True) + EPS)
        for g_ref, o_ref in zip(g_refs, o_refs):
            o_ref[...] = (y * g_ref[...]).astype(BF16)

    row = pl.BlockSpec((tr, d), lambda i: (i, 0))
    vec = pl.BlockSpec((1, d), lambda i: (0, 0))
    return pl.pallas_call(
        body, name=name, grid=(s // tr,),
        in_specs=[row] + [vec] * ng, out_specs=[row] * ng,
        out_shape=[jax.ShapeDtypeStruct((s, d), BF16)] * ng,
        compiler_params=_params(("parallel",)),
    )(x, *gains)


def rms_bwd(x, dres, norms, *, name):
    s, d = x.shape
    tr = _pick(s, (256, 128, 8))
    ng = len(norms)

    def body(*refs):
        x_ref, dres_ref = refs[0], refs[1]
        g_refs = refs[2:2 + ng]
        dh_refs = refs[2 + ng:2 + 2 * ng]
        dx_ref, dxb_ref = refs[2 + 2 * ng], refs[3 + 2 * ng]
        dg_refs = refs[4 + 2 * ng:]
        i = pl.program_id(0)
        xv = x_ref[...]
        r = lax.rsqrt(jnp.mean(xv * xv, axis=-1, keepdims=True) + EPS)
        xhat = xv * r
        dx = dres_ref[...]
        for g_ref, dh_ref, dg_ref in zip(g_refs, dh_refs, dg_refs):
            dh = dh_ref[...]
            part = jnp.sum(dh * xhat, axis=0, keepdims=True)

            @pl.when(i == 0)
            def _():
                dg_ref[...] = part

            @pl.when(i > 0)
            def _():
                dg_ref[...] += part

            dxhat = dh * g_ref[...]
            dx = dx + r * (dxhat - xhat * jnp.mean(dxhat * xhat, axis=-1, keepdims=True))
        dx_ref[...] = dx
        dxb_ref[...] = dx.astype(BF16)

    row = pl.BlockSpec((tr, d), lambda i: (i, 0))
    vec = pl.BlockSpec((1, d), lambda i: (0, 0))
    outs = pl.pallas_call(
        body, name=name, grid=(s // tr,),
        in_specs=[row, row] + [vec] * ng + [row] * ng,
        out_specs=[row, row] + [vec] * ng,
        out_shape=[jax.ShapeDtypeStruct((s, d), F32), jax.ShapeDtypeStruct((s, d), BF16)]
        + [jax.ShapeDtypeStruct((1, d), F32)] * ng,
        compiler_params=_params(("arbitrary",)),
    )(x, dres, *[g for g, _ in norms], *[dh for _, dh in norms])
    return outs[0], outs[1], list(outs[2:])


def loss_bwd(x2, target, gain, *, name):
    s, d = x2.shape
    tr = _pick(s, (256, 128, 8))
    nsteps = s // tr

    def body(x_ref, t_ref, g_ref, loss_ref, dg_ref, dx_ref, dxb_ref, sq_acc):
        i = pl.program_id(0)
        xv = x_ref[...]
        r = lax.rsqrt(jnp.mean(xv * xv, axis=-1, keepdims=True) + EPS)
        xhat = xv * r
        g = g_ref[...]
        err = xhat * g - t_ref[...]
        dy = err * (1.0 / d)
        sq = jnp.sum(err * err, axis=0, keepdims=True)
        dgp = jnp.sum(dy * xhat, axis=0, keepdims=True)

        @pl.when(i == 0)
        def _():
            sq_acc[...] = sq
            dg_ref[...] = dgp

        @pl.when(i > 0)
        def _():
            sq_acc[...] += sq
            dg_ref[...] += dgp

        dxhat = dy * g
        dx = r * (dxhat - xhat * jnp.mean(dxhat * xhat, axis=-1, keepdims=True))
        dx_ref[...] = dx
        dxb_ref[...] = dx.astype(BF16)

        @pl.when(i == nsteps - 1)
        def _():
            tot = jnp.sum(sq_acc[...], axis=-1, keepdims=True) * (0.5 / d)
            loss_ref[...] = jnp.broadcast_to(tot, (1, LANES))

    row = pl.BlockSpec((tr, d), lambda i: (i, 0))
    vec = pl.BlockSpec((1, d), lambda i: (0, 0))
    return pl.pallas_call(
        body, name=name, grid=(nsteps,),
        in_specs=[row, row, vec],
        out_specs=[pl.BlockSpec((1, LANES), lambda i: (0, 0)), vec, row, row],
        out_shape=[jax.ShapeDtypeStruct((1, LANES), F32), jax.ShapeDtypeStruct((1, d), F32),
                   jax.ShapeDtypeStruct((s, d), F32), jax.ShapeDtypeStruct((s, d), BF16)],
        scratch_shapes=[pltpu.VMEM((1, d), F32)],
        compiler_params=_params(("arbitrary",)),
    )(x2, target, gain)


def _lru_gates(xb, wr, wi, br, bi, sp):
    xbb = xb.astype(BF16)
    r = _sigmoid(jnp.dot(xbb, wr, preferred_element_type=F32) + br)
    ig = _sigmoid(jnp.dot(xbb, wi, preferred_element_type=F32) + bi)
    log_a = (-LRU_C) * r * sp
    a = jnp.exp(log_a)
    mult = jnp.sqrt(jnp.maximum(-jnp.tanh(log_a) * (a * a + 1.0), 0.0))
    return r, ig, a, mult


def _softplus_neg(lam):
    e = jnp.exp(-jnp.abs(lam))
    sp = jnp.maximum(-lam, 0.0) + jnp.log(1.0 + e)
    sg = jnp.where(lam >= 0, e, 1.0) / (1.0 + e)
    return sp, sg


def _conv(pad_ref, w, b, t):
    acc = b + w[CONV_W - 1:CONV_W, :] * pad_ref[pl.ds(SUBLANES, t), :]
    for dlt in range(1, CONV_W):
        acc = acc + w[CONV_W - 1 - dlt:CONV_W - dlt, :] * pad_ref[pl.ds(SUBLANES - dlt, t), :]
    return acc


def _lru_specs(t, bw, nb, time_of):
    blk = lambda c0: pl.BlockSpec((t, bw), lambda n, i, c0=c0: (time_of(i), c0 + n))
    vec = pl.BlockSpec((1, bw), lambda n, i: (0, n))
    wspec = pl.BlockSpec((None, bw, bw), lambda n, i: (n, 0, 0))
    cwspec = pl.BlockSpec((CONV_W, bw), lambda n, i: (0, n))
    return blk, vec, wspec, cwspec


def lru_fwd(proj, conv_w, conv_b, w_r, b_r, w_i, b_i, lam, *, name):
    s, r2 = proj.shape
    rr = r2 // 2
    nb, bw, _ = w_r.shape
    t = _pick(s, (512, 256, 128, 64, 32))
    ngroups = t // SUBLANES

    def body(xp_ref, gate_ref, cw_ref, cb_ref, wr_ref, br_ref, wi_ref, bi_ref, lam_ref,
             m_ref, h_ref, pad, hcarry, a_scr, u_scr):
        i = pl.program_id(1)

        @pl.when(i == 0)
        def _():
            pad[0:SUBLANES, :] = jnp.zeros((SUBLANES, bw), F32)
            hcarry[...] = jnp.zeros_like(hcarry)

        xpre = xp_ref[...]
        pad[pl.ds(SUBLANES, t), :] = xpre
        xb = _conv(pad, cw_ref[...], cb_ref[...], t)
        pad[0:SUBLANES, :] = xpre[t - SUBLANES:, :]
        sp, _ = _softplus_neg(lam_ref[...])
        _, ig, a, mult = _lru_gates(xb, wr_ref[...], wi_ref[...], br_ref[...], bi_ref[...], sp)
        a_scr[...] = a
        u_scr[...] = mult * (ig * xb)
        row = lax.broadcasted_iota(jnp.int32, (SUBLANES, bw), 0)

        def groups(gi, hprev):
            offs = [pl.multiple_of((gi * SCAN_UNROLL + u) * SUBLANES, SUBLANES) for u in range(SCAN_UNROLL)]
            scanned = []
            for off in offs:
                av = a_scr[pl.ds(off, SUBLANES), :]
                uv = u_scr[pl.ds(off, SUBLANES), :]
                for dlt in (1, 2, 4):
                    keep = row >= dlt
                    uv = jnp.where(keep, av * pltpu.roll(uv, dlt, 0) + uv, uv)
                    av = jnp.where(keep, av * pltpu.roll(av, dlt, 0), av)
                scanned.append((av, uv))
            for off, (av, uv) in zip(offs, scanned):
                hv = av * hprev + uv
                h_ref[pl.ds(off, SUBLANES), :] = hv
                hprev = hv[SUBLANES - 1:SUBLANES, :]
            return hprev

        hcarry[...] = lax.fori_loop(0, ngroups // SCAN_UNROLL, groups, hcarry[...])
        gate = gate_ref[...]
        m_ref[...] = (h_ref[...] * (gate * _sigmoid(gate))).astype(BF16)

    blk, vec, wspec, cwspec = _lru_specs(t, bw, nb, lambda i: i)
    return pl.pallas_call(
        body, name=name, grid=(nb, s // t),
        in_specs=[blk(0), blk(nb), cwspec, vec, wspec, vec, wspec, vec, vec],
        out_specs=[blk(0), blk(0)],
        out_shape=[jax.ShapeDtypeStruct((s, rr), BF16), jax.ShapeDtypeStruct((s, rr), F32)],
        scratch_shapes=[pltpu.VMEM((t + SUBLANES, bw), F32), pltpu.VMEM((1, bw), F32),
                        pltpu.VMEM((t, bw), F32), pltpu.VMEM((t, bw), F32)],
        compiler_params=_params(("parallel", "arbitrary")),
    )(proj, proj, conv_w, conv_b, w_r, b_r, w_i, b_i, lam)


def lru_bwd(proj, hst, dm, conv_w, conv_b, w_r, b_r, w_i, b_i, lam, *, name):
    s, r2 = proj.shape
    rr = r2 // 2
    nb, bw, _ = w_r.shape
    t = _pick(s, (512, 256, 128, 64, 32))
    nt = s // t
    ngroups = t // SUBLANES
    nt_dims = (((1,), (1,)), ((), ()))
    tn_dims = (((0,), (0,)), ((), ()))

    def body(xp_ref, xhalo_ref, gate_ref, h_ref, hhalo_ref, dm_ref, cw_ref, cb_ref, wr_ref, br_ref, wi_ref,
             bi_ref, lam_ref,
             dxp_ref, dgate_ref, dcw_ref, dcb_ref, dwr_ref, dbr_ref, dwi_ref, dbi_ref, dlam_ref,
             pad, hpad, dpad, ecarry, a_scr, b_scr, d_scr):
        step = pl.program_id(1)

        @pl.when(step == 0)
        def _():
            dpad[pl.ds(t, SUBLANES), :] = jnp.zeros((SUBLANES, bw), F32)
            ecarry[...] = jnp.zeros_like(ecarry)
            dcw_ref[...] = jnp.zeros_like(dcw_ref)
            dcb_ref[...] = jnp.zeros_like(dcb_ref)
            dwr_ref[...] = jnp.zeros_like(dwr_ref)
            dbr_ref[...] = jnp.zeros_like(dbr_ref)
            dwi_ref[...] = jnp.zeros_like(dwi_ref)
            dbi_ref[...] = jnp.zeros_like(dbi_ref)
            dlam_ref[...] = jnp.zeros_like(dlam_ref)

        past = jnp.where(step == nt - 1, 0.0, 1.0)
        pad[0:SUBLANES, :] = xhalo_ref[...] * past
        pad[pl.ds(SUBLANES, t), :] = xp_ref[...]
        hpad[0:SUBLANES, :] = hhalo_ref[...] * past
        hpad[pl.ds(SUBLANES, t), :] = h_ref[...]
        cw = cw_ref[...]
        xb = _conv(pad, cw, cb_ref[...], t)
        sp, sg = _softplus_neg(lam_ref[...])
        wr = wr_ref[...]
        wi = wi_ref[...]
        r, ig, a, mult = _lru_gates(xb, wr, wi, br_ref[...], bi_ref[...], sp)
        gate = gate_ref[...]
        sgate = _sigmoid(gate)
        dmv = dm_ref[...]
        dgate_ref[...] = (dmv * h_ref[...] * (sgate * (1.0 + gate * (1.0 - sgate)))).astype(BF16)
        dy = dmv * (gate * sgate)
        a_scr[...] = a
        b_scr[...] = a * dy
        row = lax.broadcasted_iota(jnp.int32, (SUBLANES, bw), 0)

        def groups(gi, enext):
            offs = [pl.multiple_of((ngroups - 1 - gi * SCAN_UNROLL - u) * SUBLANES, SUBLANES)
                    for u in range(SCAN_UNROLL)]
            scanned = []
            for off in offs:
                av = a_scr[pl.ds(off, SUBLANES), :]
                bv = b_scr[pl.ds(off, SUBLANES), :]
                for dlt in (1, 2, 4):
                    keep = row < SUBLANES - dlt
                    bv = jnp.where(keep, av * pltpu.roll(bv, SUBLANES - dlt, 0) + bv, bv)
                    av = jnp.where(keep, av * pltpu.roll(av, SUBLANES - dlt, 0), av)
                scanned.append((av, bv))
            for off, (av, bv) in zip(offs, scanned):
                ev = av * enext + bv
                d_scr[pl.ds(off, SUBLANES), :] = jnp.where(row == SUBLANES - 1, enext,
                                                           pltpu.roll(ev, SUBLANES - 1, 0))
                enext = ev[0:1, :]
            return enext

        ecarry[...] = lax.fori_loop(0, ngroups // SCAN_UNROLL, groups, ecarry[...])
        dtot = dy + d_scr[...]
        da = dtot * hpad[pl.ds(SUBLANES - 1, t), :]
        dmult = dtot * (ig * xb)
        dlog_a = da * a - dmult * (a * a) / mult
        dr_pre = dlog_a * ((-LRU_C) * sp) * (r * (1.0 - r))
        di_pre = (dtot * mult * xb) * (ig * (1.0 - ig))
        dlam_ref[...] += jnp.sum(dlog_a * r, axis=0, keepdims=True) * (LRU_C * sg)
        dbr_ref[...] += jnp.sum(dr_pre, axis=0, keepdims=True)
        dbi_ref[...] += jnp.sum(di_pre, axis=0, keepdims=True)
        drb = dr_pre.astype(BF16)
        dib = di_pre.astype(BF16)
        xbb = xb.astype(BF16)
        dxb = (dtot * mult * ig
               + lax.dot_general(drb, wr, nt_dims, preferred_element_type=F32)
               + lax.dot_general(dib, wi, nt_dims, preferred_element_type=F32))
        dwr_ref[...] += lax.dot_general(xbb, drb, tn_dims, preferred_element_type=F32)
        dwi_ref[...] += lax.dot_general(xbb, dib, tn_dims, preferred_element_type=F32)
        dcb_ref[...] += jnp.sum(dxb, axis=0, keepdims=True)
        dpad[pl.ds(0, t), :] = dxb
        dxpre = cw[CONV_W - 1:CONV_W, :] * dxb
        dcw_ref[CONV_W - 1:CONV_W, :] += jnp.sum(dxb * pad[pl.ds(SUBLANES, t), :], axis=0, keepdims=True)
        for dlt in range(1, CONV_W):
            dxpre = dxpre + cw[CONV_W - 1 - dlt:CONV_W - dlt, :] * dpad[pl.ds(dlt, t), :]
            dcw_ref[CONV_W - 1 - dlt:CONV_W - dlt, :] += jnp.sum(
                dxb * pad[pl.ds(SUBLANES - dlt, t), :], axis=0, keepdims=True)
        dpad[pl.ds(t, SUBLANES), :] = dxb[0:SUBLANES, :]
        dxp_ref[...] = dxpre.astype(BF16)

    rev = lambda i: nt - 1 - i
    blk, vec, wspec, cwspec = _lru_specs(t, bw, nb, rev)
    halo = pl.BlockSpec((SUBLANES, bw), lambda n, i: (jnp.maximum(rev(i) * ngroups - 1, 0), n))
    return pl.pallas_call(
        body, name=name, grid=(nb, nt),
        in_specs=[blk(0), halo, blk(nb), blk(0), halo, blk(0), cwspec, vec, wspec, vec, wspec, vec, vec],
        out_specs=[blk(0), blk(0), cwspec, vec, wspec, vec, wspec, vec, vec],
        out_shape=[jax.ShapeDtypeStruct((s, rr), BF16), jax.ShapeDtypeStruct((s, rr), BF16),
                   jax.ShapeDtypeStruct((CONV_W, rr), F32), jax.ShapeDtypeStruct((1, rr), F32),
                   jax.ShapeDtypeStruct((nb, bw, bw), F32), jax.ShapeDtypeStruct((1, rr), F32),
                   jax.ShapeDtypeStruct((nb, bw, bw), F32), jax.ShapeDtypeStruct((1, rr), F32),
                   jax.ShapeDtypeStruct((1, rr), F32)],
        scratch_shapes=[pltpu.VMEM((t + SUBLANES, bw), F32), pltpu.VMEM((t + SUBLANES, bw), F32),
                        pltpu.VMEM((t + SUBLANES, bw), F32), pltpu.VMEM((1, bw), F32),
                        pltpu.VMEM((t, bw), F32), pltpu.VMEM((t, bw), F32), pltpu.VMEM((t, bw), F32)],
        compiler_params=_params(("parallel", "arbitrary")),
    )(proj, proj, proj, hst, hst, dm, conv_w, conv_b, w_r, b_r, w_i, b_i, lam)


def _softplus(z):
    return jnp.maximum(z, 0.0) + jnp.log(1.0 + jnp.exp2(jnp.abs(z) * (-LOG2E)))


def _att_blocks(s):
    bk = ATT_BLOCK if s % ATT_BLOCK == 0 else s
    bq = ATT_QTILES * bk if s % (ATT_QTILES * bk) == 0 else bk
    return bk, bq


def attn_fwd(projb, kv, *, name):
    s, a2 = projb.shape
    a = a2 // 2
    nh = a // HEAD_DIM
    bk, bq = _att_blocks(s)
    r = bq // bk
    nq = s // bq
    assert s // bk <= LANES
    scale = 1.0 / math.sqrt(HEAD_DIM)
    nt_dims = (((1,), (1,)), ((), ()))
    hp = ATT_HEADS_FWD if nh % ATT_HEADS_FWD == 0 else 1
    wd = hp * HEAD_DIM

    def body(q_ref, g_ref, k_ref, v_ref, m_ref, o_ref, off_ref, acc):
        i = pl.program_id(1)
        qb = (q_ref[...] * scale).astype(BF16)
        from_mat = (lax.broadcasted_iota(jnp.int32, (bk, bk), 0)
                    >= lax.broadcasted_iota(jnp.int32, (bk, bk), 1)).astype(BF16)
        rowi = lax.broadcasted_iota(jnp.int32, (bq, bk), 0)
        coli = lax.broadcasted_iota(jnp.int32, (bq, bk), 1)
        lane = lax.broadcasted_iota(jnp.int32, (bq, LANES), 1)
        cols = [slice(hh * HEAD_DIM, (hh + 1) * HEAD_DIM) for hh in range(hp)]

        def tile(j, carries, diag):
            causal = None if diag is None else (coli + diag * bk) < rowi
            rows = pl.ds(pl.multiple_of(j * bk, bk), bk)
            zs = [lax.dot_general(qb[:, c], k_ref[rows, c], nt_dims, preferred_element_type=F32) for c in cols]
            sums = []
            for z in zs:
                sp = _softplus(z)
                if causal is not None:
                    sp = jnp.where(causal, sp, 0.0)
                sums.append(jnp.dot(sp.astype(BF16), from_mat, preferred_element_type=F32))
            out = []
            for hh in range(hp):
                w = jnp.exp(zs[hh] - sums[hh] - carries[hh])
                if causal is not None:
                    w = jnp.where(causal, w, 0.0)
                acc[:, cols[hh]] += jnp.dot(w.astype(BF16), v_ref[rows, cols[hh]], preferred_element_type=F32)
                off_ref[hh] = jnp.where(lane == j, carries[hh], off_ref[hh])
                out.append(carries[hh] + sums[hh][:, 0:1])
            return tuple(out)

        acc[...] = jnp.zeros_like(acc)
        off_ref[...] = jnp.zeros_like(off_ref)
        carries = tuple(jnp.zeros((bq, 1), F32) for _ in range(hp))
        for dg in reversed(range(r)):
            carries = tile(r * i + dg, carries, dg)
        lax.fori_loop(0, r * i, lambda jj, c: tile(r * i - 1 - jj, c, None), carries)
        o = acc[...]
        o_ref[...] = o
        gate = g_ref[...]
        m_ref[...] = (o * (gate * _sigmoid(gate))).astype(BF16)

    ng = nh // hp
    qspec = lambda c0: pl.BlockSpec((bq, wd), lambda h, i, c0=c0: (i, c0 + h))
    kspec = lambda c0: pl.BlockSpec((s, wd), lambda h, i, c0=c0: (0, c0 + h), pipeline_mode=pl.Buffered(1))
    return pl.pallas_call(
        body, name=name, grid=(ng, nq),
        in_specs=[qspec(0), qspec(ng), kspec(0), kspec(ng)],
        out_specs=[qspec(0), qspec(0), pl.BlockSpec((hp, None, bq, LANES), lambda h, i: (h, i, 0, 0))],
        out_shape=[jax.ShapeDtypeStruct((s, a), BF16), jax.ShapeDtypeStruct((s, a), F32),
                   jax.ShapeDtypeStruct((nh, nq, bq, LANES), F32)],
        scratch_shapes=[pltpu.VMEM((bq, wd), F32)],
        compiler_params=_params(("parallel", "arbitrary")),
    )(projb, projb, kv, kv)


def attn_bwd(projb, dm, o, kv, offs, *, name):
    s, a2 = projb.shape
    a = a2 // 2
    nh = a // HEAD_DIM
    bk, bq = _att_blocks(s)
    r = bq // bk
    nq = s // bq
    scale = 1.0 / math.sqrt(HEAD_DIM)
    nt_dims = (((1,), (1,)), ((), ()))
    tn_dims = (((0,), (0,)), ((), ()))
    hp = ATT_HEADS_BWD if nh % ATT_HEADS_BWD == 0 else 1
    wd = hp * HEAD_DIM

    def body(q_ref, g_ref, dm_ref, o_ref, k_ref, v_ref, off_ref, dq_ref, dg_ref, dk_ref, dv_ref,
             dk_acc, dv_acc, dq_acc):
        i = pl.program_id(1)

        @pl.when(i == 0)
        def _():
            dk_acc[...] = jnp.zeros_like(dk_acc)
            dv_acc[...] = jnp.zeros_like(dv_acc)

        qb = (q_ref[...] * scale).astype(BF16)
        gate = g_ref[...]
        sgate = _sigmoid(gate)
        dmv = dm_ref[...]
        dob = (dmv * (gate * sgate)).astype(BF16)
        dg_ref[...] = (dmv * o_ref[...] * (sgate * (1.0 + gate * (1.0 - sgate)))).astype(BF16)
        ki = lax.broadcasted_iota(jnp.int32, (bk, bk), 0)
        kj = lax.broadcasted_iota(jnp.int32, (bk, bk), 1)
        from_mat = (kj >= ki).astype(BF16)
        upto_mat = (kj <= ki).astype(BF16)
        rowi = lax.broadcasted_iota(jnp.int32, (bk, bq), 0)
        coli = lax.broadcasted_iota(jnp.int32, (bk, bq), 1)
        sub = lax.broadcasted_iota(jnp.int32, (LANES, bq), 0)
        offs_t = [off_ref[hh].T for hh in range(hp)]
        cols = [slice(hh * HEAD_DIM, (hh + 1) * HEAD_DIM) for hh in range(hp)]
        dq_acc[...] = jnp.zeros_like(dq_acc)

        def tile(j, gcarries, diag):
            causal = None if diag is None else (rowi + diag * bk) < coli
            rows = pl.ds(pl.multiple_of(j * bk, bk), bk)
            zs = [lax.dot_general(k_ref[rows, c], qb[:, c], nt_dims, preferred_element_type=F32) for c in cols]
            dws = [lax.dot_general(v_ref[rows, c], dob[:, c], nt_dims, preferred_element_type=F32) for c in cols]
            sigs, sums = [], []
            for z in zs:
                sp = _softplus(z)
                sigs.append(jnp.exp(z - sp))
                if causal is not None:
                    sp = jnp.where(causal, sp, 0.0)
                sums.append(jnp.dot(from_mat, sp.astype(BF16), preferred_element_type=F32))
            gs, totals = [], []
            for hh in range(hp):
                offj = jnp.sum(jnp.where(sub == j, offs_t[hh], 0.0), axis=0, keepdims=True)
                w = jnp.exp(zs[hh] - sums[hh] - offj)
                if causal is not None:
                    w = jnp.where(causal, w, 0.0)
                g = w * dws[hh]
                dv_acc[rows, cols[hh]] += jnp.dot(w.astype(BF16), dob[:, cols[hh]], preferred_element_type=F32)
                totals.append(jnp.dot(upto_mat, g.astype(BF16), preferred_element_type=F32))
                gs.append(g)
            out = []
            for hh in range(hp):
                dz = gs[hh] - (totals[hh] + gcarries[hh]) * sigs[hh]
                if causal is not None:
                    dz = jnp.where(causal, dz, 0.0)
                dzb = dz.astype(BF16)
                dk_acc[rows, cols[hh]] += jnp.dot(dzb, qb[:, cols[hh]], preferred_element_type=F32)
                dq_acc[:, cols[hh]] += lax.dot_general(dzb, k_ref[rows, cols[hh]], tn_dims,
                                                       preferred_element_type=F32)
                out.append(gcarries[hh] + totals[hh][bk - 1:bk, :])
            return tuple(out)

        init = tuple(jnp.zeros((1, bq), F32) for _ in range(hp))
        gcarries = lax.fori_loop(0, r * i, lambda j, c: tile(j, c, None), init)
        for dg in range(r):
            gcarries = tile(r * i + dg, gcarries, dg)
        dq_ref[...] = (dq_acc[...] * scale).astype(BF16)

        @pl.when(i == nq - 1)
        def _():
            dk_ref[...] = dk_acc[...].astype(BF16)
            dv_ref[...] = dv_acc[...].astype(BF16)

    ng = nh // hp
    once = pl.Buffered(1)
    qspec = lambda c0: pl.BlockSpec((bq, wd), lambda h, i, c0=c0: (i, c0 + h))
    kspec = lambda c0: pl.BlockSpec((s, wd), lambda h, i, c0=c0: (0, c0 + h), pipeline_mode=once)
    return pl.pallas_call(
        body, name=name, grid=(ng, nq),
        in_specs=[qspec(0), qspec(ng), qspec(0), qspec(0), kspec(0), kspec(ng),
                  pl.BlockSpec((hp, None, bq, LANES), lambda h, i: (h, i, 0, 0))],
        out_specs=[qspec(0), qspec(0), kspec(0), kspec(0)],
        out_shape=[jax.ShapeDtypeStruct((s, a), BF16)] * 4,
        scratch_shapes=[pltpu.VMEM((s, wd), F32), pltpu.VMEM((s, wd), F32), pltpu.VMEM((bq, wd), F32)],
        compiler_params=_params(("parallel", "arbitrary")),
    )(projb, projb, dm, o, kv, kv, offs)


def _as2d(x):
    n = x.size
    cols = x.shape[-1]
    if cols % LANES != 0:
        cols = LANES
    return x.reshape(n // cols, cols)


def sum_parts(parts, *, name):
    p, rows, cols = parts.shape
    tr = _pick(rows, (512, 256, 128, 64, 32, 16))

    def body(p_ref, o_ref):
        acc = p_ref[0].astype(F32)
        for k in range(1, p):
            acc = acc + p_ref[k].astype(F32)
        o_ref[...] = acc

    return pl.pallas_call(
        body, name=name, grid=(rows // tr,),
        in_specs=[pl.BlockSpec((p, tr, cols), lambda i: (0, i, 0))],
        out_specs=pl.BlockSpec((tr, cols), lambda i: (i, 0)),
        out_shape=jax.ShapeDtypeStruct((rows, cols), F32),
        compiler_params=_params(("parallel",)),
    )(parts)


def adamw(w, g_parts, m, v, *, name):
    rows, cols = w.shape
    tr = _pick(rows, (128, 64, 32, 16, 8))
    np_ = len(g_parts)
    c1 = 1.0 / (1.0 - ADAM_B1 ** ADAM_STEP)
    c2 = 1.0 / (1.0 - ADAM_B2 ** ADAM_STEP)

    def body(*refs):
        w_ref, m_ref, v_ref = refs[0], refs[1], refs[2]
        g_refs = refs[3:3 + np_]
        go_ref, d_ref, mo_ref, vo_ref = refs[3 + np_:]
        g = g_refs[0][...]
        for gr in g_refs[1:]:
            g = g + gr[...]
        mn = ADAM_B1 * m_ref[...] + (1.0 - ADAM_B1) * g
        vn = ADAM_B2 * v_ref[...] + (1.0 - ADAM_B2) * (g * g)
        go_ref[...] = g
        mo_ref[...] = mn
        vo_ref[...] = vn
        d_ref[...] = (-ADAM_LR) * ((mn * c1) / (jnp.sqrt(vn * c2) + ADAM_EPS) + ADAM_WD * w_ref[...])

    spec = pl.BlockSpec((tr, cols), lambda i: (i, 0))
    return pl.pallas_call(
        body, name=name, grid=(rows // tr,),
        in_specs=[spec] * (3 + np_), out_specs=[spec] * 4,
        out_shape=[jax.ShapeDtypeStruct((rows, cols), F32)] * 4,
        compiler_params=_params(("parallel",)),
    )(w, m, v, *g_parts)


def exchange(kind, arrays, axes, *, name):
    na = len(arrays)
    hbm = pl.BlockSpec(memory_space=pl.ANY)

    def body(*refs):
        pairs = _exchange(kind, refs[:na], refs[na:2 * na], axes, *refs[2 * na:])
        for start, _ in pairs:
            start()
        for _, finish in pairs:
            finish()

    return pl.pallas_call(
        body, name=name, in_specs=[hbm] * na, out_specs=[hbm] * na,
        out_shape=_exchange_shapes(kind, arrays, axes), scratch_shapes=_exchange_sems(na),
    )(*arrays)


def swap_cores(arrs, *, name):
    na = len(arrs)
    hbm = pl.BlockSpec(memory_space=pl.ANY)

    def body(*refs):
        a_refs = refs[:na]
        o_refs = refs[na:2 * na]
        send_sems, recv_sems = refs[2 * na:]
        x, y, c = _place()
        copies = []
        for ai in range(na):
            cp = pltpu.make_async_remote_copy(
                src_ref=a_refs[ai], dst_ref=o_refs[ai], send_sem=send_sems.at[ai], recv_sem=recv_sems.at[ai],
                device_id=(x, y, 1 - c), device_id_type=MESH)
            cp.start()
            copies.append(cp)
        for cp in copies:
            cp.wait()

    return pl.pallas_call(
        body, name=name,
        in_specs=[hbm] * na, out_specs=[hbm] * na,
        out_shape=[jax.ShapeDtypeStruct(a.shape, a.dtype) for a in arrs],
        scratch_shapes=[pltpu.SemaphoreType.DMA((na,)), pltpu.SemaphoreType.DMA((na,))],
    )(*arrs)


def allreduce_small(buf, *, name):
    rows, cols = buf.shape

    def body(b_ref, o_ref, slots, send_sems, recv_sems):
        x, y, c = _place()
        me = 4 * x + 2 * y + c
        slots[0] = b_ref[...]
        copies = []
        for rel in range(1, 8):
            peer = (x ^ (rel >> 2), y ^ ((rel >> 1) & 1), c ^ (rel & 1))
            cp = pltpu.make_async_remote_copy(
                src_ref=b_ref, dst_ref=slots.at[rel], send_sem=send_sems.at[rel - 1],
                recv_sem=recv_sems.at[rel - 1], device_id=peer, device_id_type=MESH)
            cp.start()
            copies.append(cp)
        for cp in copies:
            cp.wait()
        acc = slots[me]
        for dev in range(1, 8):
            acc = acc + slots[dev ^ me]
        o_ref[...] = acc

    vm = pl.BlockSpec(memory_space=pltpu.VMEM)
    return pl.pallas_call(
        body, name=name, in_specs=[vm], out_specs=vm,
        out_shape=jax.ShapeDtypeStruct((rows, cols), F32),
        scratch_shapes=[pltpu.VMEM((8, rows, cols), F32), pltpu.SemaphoreType.DMA((7,)),
                        pltpu.SemaphoreType.DMA((7,))],
    )(buf)


def _pack_rows(arrs):
    parts = []
    for a in arrs:
        p = a.reshape(-1, LANES)
        parts.append(jnp.pad(p, ((0, (-p.shape[0]) % SUBLANES), (0, 0))))
    return jnp.concatenate(parts, axis=0)


def _unpack_rows(buf, shapes):
    out, r0 = [], 0
    for shp in shapes:
        n = math.prod(shp) // LANES
        out.append(buf[r0:r0 + n].reshape(shp))
        r0 += n + (-n) % SUBLANES
    return out


def kernel(x, a_norm, a_w_in, a_conv_w, a_conv_b, a_w_r, a_b_r, a_w_i, a_b_i, a_lambda, a_w_out, kv_norm, w_kv, b_norm, b_w_in, b_w_out, final_norm, loss_target, m_a_norm, m_a_w_in, m_a_conv_w, m_a_conv_b, m_a_w_r, m_a_b_r, m_a_w_i, m_a_b_i, m_a_lambda, m_a_w_out, m_kv_norm, m_w_kv, m_b_norm, m_b_w_in, m_b_w_out, m_final_norm, v_a_norm, v_a_w_in, v_a_conv_w, v_a_conv_b, v_a_w_r, v_a_b_r, v_a_w_i, v_a_b_i, v_a_lambda, v_a_w_out, v_kv_norm, v_w_kv, v_b_norm, v_b_w_in, v_b_w_out, v_final_norm):
    weights = dict(a_norm=a_norm, a_w_in=a_w_in, a_conv_w=a_conv_w, a_conv_b=a_conv_b, a_w_r=a_w_r, a_b_r=a_b_r,
                   a_w_i=a_w_i, a_b_i=a_b_i, a_lambda=a_lambda, a_w_out=a_w_out, kv_norm=kv_norm, w_kv=w_kv,
                   b_norm=b_norm, b_w_in=b_w_in, b_w_out=b_w_out, final_norm=final_norm)
    mom1 = dict(a_norm=m_a_norm, a_w_in=m_a_w_in, a_conv_w=m_a_conv_w, a_conv_b=m_a_conv_b, a_w_r=m_a_w_r,
                a_b_r=m_a_b_r, a_w_i=m_a_w_i, a_b_i=m_a_b_i, a_lambda=m_a_lambda, a_w_out=m_a_w_out,
                kv_norm=m_kv_norm, w_kv=m_w_kv, b_norm=m_b_norm, b_w_in=m_b_w_in, b_w_out=m_b_w_out,
                final_norm=m_final_norm)
    mom2 = dict(a_norm=v_a_norm, a_w_in=v_a_w_in, a_conv_w=v_a_conv_w, a_conv_b=v_a_conv_b, a_w_r=v_a_w_r,
                a_b_r=v_a_b_r, a_w_i=v_a_w_i, a_b_i=v_a_b_i, a_lambda=v_a_lambda, a_w_out=v_a_w_out,
                kv_norm=v_kv_norm, w_kv=v_w_kv, b_norm=v_b_norm, b_w_in=v_b_w_in, b_w_out=v_b_w_out,
                final_norm=v_final_norm)
    order = list(weights)
    x0 = x[0]
    target = loss_target[0]
    d = x0.shape[1]
    chip = 2 * lax.axis_index("x") + lax.axis_index("y")

    big = ["a_w_in", "a_w_r", "a_w_i", "a_w_out", "w_kv", "b_w_in", "b_w_out"]
    big_axis = dict(a_w_in=1, a_w_r=1, a_w_i=1, a_w_out=0, w_kv=1, b_w_in=1, b_w_out=0)
    local = dict(a_w_in=a_w_in[0], a_w_r=a_w_r[0], a_w_i=a_w_i[0], a_w_out=a_w_out[0], w_kv=w_kv,
                 b_w_in=b_w_in[0], b_w_out=b_w_out[0])
    shards = {n: local[n].astype(BF16) for n in big}
    first = ["a_w_in", "a_w_r", "a_w_i"]
    full = exchange("gather", [shards[n] for n in first] + [a_conv_w[0], b_norm],
                    [big_axis[n] for n in first] + [1, 1], name="gather_first")
    wf = dict(zip(first + ["a_conv_w", "b_norm"], full))
    wf.update(a_norm=a_norm, a_conv_b=a_conv_b, a_b_r=a_b_r, a_b_i=a_b_i, a_lambda=a_lambda,
              kv_norm=kv_norm.reshape(1, d), final_norm=final_norm.reshape(1, d))
    loss_part, grad_x, parts, gsmall = _local_grads(x0, target, wf, shards=shards, axes=big_axis)

    sums = [sum_parts(parts[n].reshape(4, *_as2d(parts[n][0]).shape), name="sum_" + n) for n in big]
    others = swap_cores(sums, name="swap_cores")

    small = ["a_norm", "a_conv_b", "a_b_r", "a_b_i", "a_lambda", "kv_norm", "final_norm", "a_conv_w", "b_norm"]
    buf = _pack_rows([gsmall[n] for n in small] + [loss_part])
    red = allreduce_small(buf, name="allreduce_small")
    red_list = _unpack_rows(red, [gsmall[n].shape for n in small] + [(1, LANES)])
    gs = dict(zip(small, red_list[:-1]))
    loss = red_list[-1][0, 0]
    n_conv = a_conv_w.shape[2]
    gs["a_conv_w"] = lax.dynamic_slice_in_dim(gs["a_conv_w"], chip * n_conv, n_conv, axis=1)
    n_bn = b_norm.shape[1]
    gs["b_norm"] = lax.dynamic_slice_in_dim(gs["b_norm"], chip * n_bn, n_bn, axis=1)

    grads, deltas, new_m, new_v = {}, {}, {}, {}
    for n, s_mine, s_other in zip(big, sums, others):
        shp = weights[n].shape
        g, dlt, mn, vn = adamw(_as2d(weights[n]), [s_mine, s_other], _as2d(mom1[n]), _as2d(mom2[n]),
                               name="adamw_" + n)
        grads[n], deltas[n], new_m[n], new_v[n] = (t.reshape(shp) for t in (g, dlt, mn, vn))
    shapes = [weights[n].shape for n in small]
    wpk, gpk, mpk, vpk = (_pack_rows([src[n] for n in small]) for src in (weights, gs, mom1, mom2))
    outs = adamw(wpk, [gpk], mpk, vpk, name="adamw_small")
    for dst, packed in zip((grads, deltas, new_m, new_v), outs):
        for n, val in zip(small, _unpack_rows(packed, shapes)):
            dst[n] = val

    return (loss, grad_x[None], *[grads[n] for n in order], *[deltas[n] for n in order],
            *[new_m[n] for n in order], *[new_v[n] for n in order])


def _local_grads(x0, target, wf, shards=None, axes=None):
    a_norm, a_conv_b, a_b_r, a_b_i, a_lambda = (wf[n] for n in ("a_norm", "a_conv_b", "a_b_r", "a_b_i", "a_lambda"))
    kv_norm, final_norm = wf["kv_norm"], wf["final_norm"]
    wf = dict(wf)
    parts = {}

    def mm(*args, gather=(), scatter=None, **kw):
        if shards is None or not (gather or scatter):
            return matmul(*args, **kw)
        if gather:
            out, got = matmul(*args, exchange=("gather", [shards[n] for n in gather], [axes[n] for n in gather]), **kw)
            wf.update(zip(gather, got))
        else:
            out, got = matmul(*args, exchange=("scatter", list(scatter.values()), [axes[n] for n in scatter]), **kw)
            parts.update(zip(scatter, got))
        return out

    (h_a,) = rms_fwd(x0, [a_norm], name="norm_a")
    proj_a = mm(h_a, wf["a_w_in"], gather=("a_w_out",), name="a_in")
    m_a, hst = lru_fwd(proj_a, wf["a_conv_w"], a_conv_b, wf["a_w_r"], a_b_r, wf["a_w_i"], a_b_i, a_lambda,
                       name="lru_fwd")
    x1 = mm(m_a, wf["a_w_out"], residual=x0, gather=("w_kv",), name="a_out")
    kvn, hb = rms_fwd(x1, [kv_norm, wf["b_norm"]], name="norm_kv_b")
    kv = mm(kvn, wf["w_kv"], out_dtype=BF16, gather=("b_w_in",), name="kv_proj")
    proj_b = mm(hb, wf["b_w_in"], gather=("b_w_out",), name="b_in")
    m_b, o, offs = attn_fwd(proj_b, kv, name="attn_fwd")
    x2 = mm(m_b, wf["b_w_out"], residual=x1, name="b_out")
    loss_part, g_final, dx2, dx2b = loss_bwd(x2, target, final_norm, name="loss_bwd")

    dm_b = mm(dx2b, wf["b_w_out"], tb=True, name="b_out_dx")
    g_b_w_out = mm(m_b, dx2b, ta=True, out_dtype=BF16, name="b_out_dw")
    dq, dgate_b, dk, dv = attn_bwd(proj_b, dm_b, o, kv, offs, name="attn_bwd")
    dproj_b = (dq, dgate_b)
    dkv = (dk, dv)
    g_b_w_in = mm(hb, dproj_b, ta=True, out_dtype=BF16, scatter=dict(b_w_out=g_b_w_out), name="b_in_dw")
    g_w_kv = mm(kvn, dkv, ta=True, out_dtype=BF16, scatter=dict(b_w_in=g_b_w_in), name="kv_dw")
    dhb = mm(dproj_b, wf["b_w_in"], tb=True, scatter=dict(w_kv=g_w_kv), name="b_in_dx")
    dkvn = mm(dkv, wf["w_kv"], tb=True, name="kv_dx")
    dx1, dx1b, (g_kv_norm, g_b_norm) = rms_bwd(
        x1, dx2, [(kv_norm, dkvn), (wf["b_norm"], dhb)], name="norm_kv_b_bwd")

    g_a_w_out = mm(m_a, dx1b, ta=True, out_dtype=BF16, name="a_out_dw")
    dm_a = mm(dx1b, wf["a_w_out"], tb=True, scatter=dict(a_w_out=g_a_w_out), name="a_out_dx")
    dxpre, dgate_a, g_conv_w, g_conv_b, g_w_r, g_b_r, g_w_i, g_b_i, g_lambda = lru_bwd(
        proj_a, hst, dm_a, wf["a_conv_w"], a_conv_b, wf["a_w_r"], a_b_r, wf["a_w_i"], a_b_i, a_lambda,
        name="lru_bwd")
    dproj_a = (dxpre, dgate_a)
    g_w_r, g_w_i = g_w_r.astype(BF16), g_w_i.astype(BF16)
    g_a_w_in = mm(h_a, dproj_a, ta=True, out_dtype=BF16, scatter=dict(a_w_r=g_w_r, a_w_i=g_w_i), name="a_in_dw")
    dh_a = mm(dproj_a, wf["a_w_in"], tb=True, scatter=dict(a_w_in=g_a_w_in), name="a_in_dx")
    grad_x, _, (g_a_norm,) = rms_bwd(x0, dx1, [(a_norm, dh_a)], name="norm_a_bwd")

    gbig = parts if shards is not None else dict(
        a_w_in=g_a_w_in, a_w_r=g_w_r, a_w_i=g_w_i, a_w_out=g_a_w_out, w_kv=g_w_kv, b_w_in=g_b_w_in, b_w_out=g_b_w_out)
    gsmall = dict(a_norm=g_a_norm, a_conv_b=g_conv_b, a_b_r=g_b_r, a_b_i=g_b_i, a_lambda=g_lambda,
                  kv_norm=g_kv_norm, final_norm=g_final, a_conv_w=g_conv_w, b_norm=g_b_norm)
    return loss_part, grad_x, gbig, gsmall
```

```python
import math

import jax
import jax.numpy as jnp
from jax import lax
from jax.experimental import pallas as pl
from jax.experimental.pallas import tpu as pltpu

F32 = jnp.float32
BF16 = jnp.bfloat16
MESH = pl.DeviceIdType.MESH

EPS = 1e-6
LRU_C = 8.0
CONV_W = 4
HEAD_DIM = 128
ADAM_LR = 0.001
ADAM_B1 = 0.9
ADAM_B2 = 0.999
ADAM_EPS = 1e-08
ADAM_WD = 0.01
ADAM_STEP = 10

V7X_VMEM_LIMIT = 56 * 1024 * 1024
LANES = 128
SUBLANES = 8
ATT_BLOCK = 256
ATT_QTILES = 2
ATT_HEADS_FWD = 2
ATT_HEADS_BWD = 2
LOG2E = 1.4426950408889634
SCAN_UNROLL = 4


def _pick(dim, cands):
    for c in cands:
        if dim % c == 0:
            return c
    return dim


def _params(sem, vmem=V7X_VMEM_LIMIT):
    return pltpu.CompilerParams(dimension_semantics=sem, vmem_limit_bytes=vmem)


def _sigmoid(x):
    return 1.0 / (1.0 + jnp.exp(-x))


def _place():
    return lax.axis_index("x"), lax.axis_index("y"), lax.axis_index("c")


def _chip_peers(x, y, c):
    return [(1 - x, y, c), (x, 1 - y, c), (1 - x, 1 - y, c)]


def _shard_of(ref, axis, idx, n):
    start = pl.multiple_of(idx * n, n)
    sl = [slice(None)] * len(ref.shape)
    sl[axis] = pl.ds(start, n)
    return ref.at[tuple(sl)]


def _exchange(kind, in_refs, out_refs, axes, send_sems, recv_sems, local_sems):
    x, y, c = _place()
    me = 2 * x + y
    peers = _chip_peers(x, y, c)
    pairs = []
    for ai, (src, dst, ax) in enumerate(zip(in_refs, out_refs, axes)):
        if kind == "gather":
            n = src.shape[ax]
            mine = _shard_of(dst, ax, me, n)
            loc = pltpu.make_async_copy(src, mine, local_sems.at[ai])
        else:
            n = dst.shape[1 + ax]
            loc = pltpu.make_async_copy(_shard_of(src, ax, me, n), dst.at[0], local_sems.at[ai])
        pairs.append((loc.start, loc.wait))
        for k, peer in enumerate(peers):
            sem = dict(send_sem=send_sems.at[ai * 3 + k], recv_sem=recv_sems.at[ai * 3 + k],
                       device_id=peer, device_id_type=MESH)
            theirs = 2 * peer[0] + peer[1]
            if kind == "gather":
                snd = pltpu.make_async_remote_copy(src_ref=src, dst_ref=mine, **sem)
                rcv = pltpu.make_async_remote_copy(src_ref=src, dst_ref=_shard_of(dst, ax, theirs, n), **sem)
            else:
                snd = pltpu.make_async_remote_copy(src_ref=_shard_of(src, ax, theirs, n), dst_ref=dst.at[1 + k], **sem)
                rcv = snd
            pairs.append((snd.start, lambda snd=snd, rcv=rcv: (snd.wait_send(), rcv.wait_recv())))
    return pairs


def _exchange_shapes(kind, arrays, axes):
    out = []
    for arr, ax in zip(arrays, axes):
        shp = list(arr.shape)
        if kind == "gather":
            shp[ax] *= 4
            out.append(jax.ShapeDtypeStruct(tuple(shp), arr.dtype))
        else:
            shp[ax] //= 4
            out.append(jax.ShapeDtypeStruct((4, *shp), arr.dtype))
    return out


def _exchange_sems(n):
    return [pltpu.SemaphoreType.DMA((3 * n,)), pltpu.SemaphoreType.DMA((3 * n,)), pltpu.SemaphoreType.DMA((n,))]


def matmul(a, b, *, ta=False, tb=False, out_dtype=F32, residual=None, exchange=None, name):
    a_pair = a if isinstance(a, (tuple, list)) else None
    b_pair = b if isinstance(b, (tuple, list)) else None
    assert not (a_pair and ta) and not (b_pair and tb) and not (a_pair and b_pair)
    a0 = a_pair[0] if a_pair else a
    b0 = b_pair[0] if b_pair else b
    m = a0.shape[1] if ta else a0.shape[0]
    kdim = (a0.shape[0] if ta else a0.shape[1]) * (2 if a_pair else 1)
    n = (b0.shape[0] if tb else b0.shape[1]) * (2 if b_pair else 1)
    assert (b0.shape[1] if tb else b0.shape[0]) == kdim
    tm = _pick(m, (1024, 640, 512, 256, 128))
    tn = _pick(n // 2 if b_pair else n, (1024, 640, 512, 256, 128))
    tk = _pick(kdim // 2 if a_pair else kdim, (2560, 2048, 1024, 512, 256, 128))
    grid = (m // tm, n // tn, kdim // tk)
    nk = grid[2]
    kh, jh = nk // 2, grid[1] // 2
    dn = (((0 if ta else 1,), (1 if tb else 0,)), ((), ()))
    na = 2 if a_pair else 1
    nb = 2 if b_pair else 1
    nres = 0 if residual is None else 1
    nex = 0 if exchange is None else len(exchange[1])

    def body(*refs):
        a_refs, b_refs = refs[:na], refs[na:na + nb]
        p = na + nb
        r_ref = refs[p] if nres else None
        ex_in = refs[p + nres:p + nres + nex]
        o_ref = refs[p + nres + nex]
        ex_out = refs[p + 1 + nres + nex:p + 1 + nres + 2 * nex]
        acc = refs[p + 1 + nres + 2 * nex]
        sems = refs[p + 2 + nres + 2 * nex:]
        i, j, k = pl.program_id(0), pl.program_id(1), pl.program_id(2)
        if nex:
            @pl.when((i == 0) & (j == 0) & (k == 0))
            def _():
                for start, _ in _exchange(exchange[0], ex_in, ex_out, exchange[2], *sems):
                    start()

        @pl.when(k == 0)
        def _():
            acc[...] = jnp.zeros_like(acc)

        def accumulate(a_ref, b_ref):
            acc[...] += lax.dot_general(a_ref[...].astype(BF16), b_ref[...].astype(BF16), dn,
                                        preferred_element_type=F32)

        if a_pair:
            pl.when(k < kh)(lambda: accumulate(a_refs[0], b_refs[0]))
            pl.when(k >= kh)(lambda: accumulate(a_refs[1], b_refs[0]))
        elif b_pair:
            pl.when(j < jh)(lambda: accumulate(a_refs[0], b_refs[0]))
            pl.when(j >= jh)(lambda: accumulate(a_refs[0], b_refs[1]))
        else:
            accumulate(a_refs[0], b_refs[0])

        @pl.when(k == nk - 1)
        def _():
            r = acc[...]
            if r_ref is not None:
                r = r + r_ref[...]
            o_ref[...] = r.astype(out_dtype)

        if nex:
            @pl.when((i == grid[0] - 1) & (j == grid[1] - 1) & (k == nk - 1))
            def _():
                for _, finish in _exchange(exchange[0], ex_in, ex_out, exchange[2], *sems):
                    finish()

    if a_pair:
        a_specs = [pl.BlockSpec((tm, tk), lambda i, j, k: (i, jnp.minimum(k, kh - 1))),
                   pl.BlockSpec((tm, tk), lambda i, j, k: (i, jnp.maximum(k - kh, 0)))]
    else:
        a_specs = [pl.BlockSpec((tk, tm), lambda i, j, k: (k, i)) if ta
                   else pl.BlockSpec((tm, tk), lambda i, j, k: (i, k))]
    if b_pair:
        b_specs = [pl.BlockSpec((tk, tn), lambda i, j, k: (jnp.where(j < jh, k, nk - 1), jnp.minimum(j, jh - 1))),
                   pl.BlockSpec((tk, tn), lambda i, j, k: (jnp.where(j >= jh, k, 0), jnp.maximum(j - jh, 0)))]
    else:
        b_specs = [pl.BlockSpec((tn, tk), lambda i, j, k: (j, k)) if tb
                   else pl.BlockSpec((tk, tn), lambda i, j, k: (k, j))]
    o_spec = pl.BlockSpec((tm, tn), lambda i, j, k: (i, j))
    hbm = pl.BlockSpec(memory_space=pl.ANY)
    in_specs = a_specs + b_specs + [o_spec] * nres + [hbm] * nex
    args = (list(a_pair) if a_pair else [a]) + (list(b_pair) if b_pair else [b])
    args += ([residual] if nres else []) + (list(exchange[1]) if nex else [])
    out_shape = [jax.ShapeDtypeStruct((m, n), out_dtype)]
    scratch = [pltpu.VMEM((tm, tn), F32)]
    if nex:
        out_shape += _exchange_shapes(*exchange)
        scratch += _exchange_sems(nex)
    outs = pl.pallas_call(
        body, name=name, grid=grid,
        in_specs=in_specs, out_specs=[o_spec] + [hbm] * nex, out_shape=out_shape,
        scratch_shapes=scratch,
        compiler_params=_params(("arbitrary",) * 3 if nex else ("parallel", "parallel", "arbitrary")),
    )(*args)
    return (outs[0], list(outs[1:])) if nex else outs[0]


def rms_fwd(x, gains, *, name):
    s, d = x.shape
    tr = _pick(s, (512, 256, 128, 8))
    ng = len(gains)

    def body(*refs):
        x_ref = refs[0]
        g_refs = refs[1:1 + ng]
        o_refs = refs[1 + ng:]
        xv = x_ref[...]
        y = xv * lax.rsqrt(jnp.mean(xv * xv, axis=-1, keepdims=True) + EPS)
        for g_ref, o_ref in zip(g_refs, o_refs):
            o_ref[...] = (y * g_ref[...]).astype(BF16)

    row = pl.BlockSpec((tr, d), lambda i: (i, 0))
    vec = pl.BlockSpec((1, d), lambda i: (0, 0))
    return pl.pallas_call(
        body, name=name, grid=(s // tr,),
        in_specs=[row] + [vec] * ng, out_specs=[row] * ng,
        out_shape=[jax.ShapeDtypeStruct((s, d), BF16)] * ng,
        compiler_params=_params(("parallel",)),
    )(x, *gains)


def rms_bwd(x, dres, norms, *, name):
    s, d = x.shape
    tr = _pick(s, (256, 128, 8))
    ng = len(norms)

    def body(*refs):
        x_ref, dres_ref = refs[0], refs[1]
        g_refs = refs[2:2 + ng]
        dh_refs = refs[2 + ng:2 + 2 * ng]
        dx_ref, dxb_ref = refs[2 + 2 * ng], refs[3 + 2 * ng]
        dg_refs = refs[4 + 2 * ng:]
        i = pl.program_id(0)
        xv = x_ref[...]
        r = lax.rsqrt(jnp.mean(xv * xv, axis=-1, keepdims=True) + EPS)
        xhat = xv * r
        dx = dres_ref[...]
        for g_ref, dh_ref, dg_ref in zip(g_refs, dh_refs, dg_refs):
            dh = dh_ref[...]
            part = jnp.sum(dh * xhat, axis=0, keepdims=True)

            @pl.when(i == 0)
            def _():
                dg_ref[...] = part

            @pl.when(i > 0)
            def _():
                dg_ref[...] += part

            dxhat = dh * g_ref[...]
            dx = dx + r * (dxhat - xhat * jnp.mean(dxhat * xhat, axis=-1, keepdims=True))
        dx_ref[...] = dx
        dxb_ref[...] = dx.astype(BF16)

    row = pl.BlockSpec((tr, d), lambda i: (i, 0))
    vec = pl.BlockSpec((1, d), lambda i: (0, 0))
    outs = pl.pallas_call(
        body, name=name, grid=(s // tr,),
        in_specs=[row, row] + [vec] * ng + [row] * ng,
        out_specs=[row, row] + [vec] * ng,
        out_shape=[jax.ShapeDtypeStruct((s, d), F32), jax.ShapeDtypeStruct((s, d), BF16)]
        + [jax.ShapeDtypeStruct((1, d), F32)] * ng,
        compiler_params=_params(("arbitrary",)),
    )(x, dres, *[g for g, _ in norms], *[dh for _, dh in norms])
    return outs[0], outs[1], list(outs[2:])


def loss_bwd(x2, target, gain, *, name):
    s, d = x2.shape
    tr = _pick(s, (256, 128, 8))
    nsteps = s // tr

    def body(x_ref, t_ref, g_ref, loss_ref, dg_ref, dx_ref, dxb_ref, sq_acc):
        i = pl.program_id(0)
        xv = x_ref[...]
        r = lax.rsqrt(jnp.mean(xv * xv, axis=-1, keepdims=True) + EPS)
        xhat = xv * r
        g = g_ref[...]
        err = xhat * g - t_ref[...]
        dy = err * (1.0 / d)
        sq = jnp.sum(err * err, axis=0, keepdims=True)
        dgp = jnp.sum(dy * xhat, axis=0, keepdims=True)

        @pl.when(i == 0)
        def _():
            sq_acc[...] = sq
            dg_ref[...] = dgp

        @pl.when(i > 0)
        def _():
            sq_acc[...] += sq
            dg_ref[...] += dgp

        dxhat = dy * g
        dx = r * (dxhat - xhat * jnp.mean(dxhat * xhat, axis=-1, keepdims=True))
        dx_ref[...] = dx
        dxb_ref[...] = dx.astype(BF16)

        @pl.when(i == nsteps - 1)
        def _():
            tot = jnp.sum(sq_acc[...], axis=-1, keepdims=True) * (0.5 / d)
            loss_ref[...] = jnp.broadcast_to(tot, (1, LANES))

    row = pl.BlockSpec((tr, d), lambda i: (i, 0))
    vec = pl.BlockSpec((1, d), lambda i: (0, 0))
    return pl.pallas_call(
        body, name=name, grid=(nsteps,),
        in_specs=[row, row, vec],
        out_specs=[pl.BlockSpec((1, LANES), lambda i: (0, 0)), vec, row, row],
        out_shape=[jax.ShapeDtypeStruct((1, LANES), F32), jax.ShapeDtypeStruct((1, d), F32),
                   jax.ShapeDtypeStruct((s, d), F32), jax.ShapeDtypeStruct((s, d), BF16)],
        scratch_shapes=[pltpu.VMEM((1, d), F32)],
        compiler_params=_params(("arbitrary",)),
    )(x2, target, gain)


def _lru_gates(xb, wr, wi, br, bi, sp):
    xbb = xb.astype(BF16)
    r = _sigmoid(jnp.dot(xbb, wr, preferred_element_type=F32) + br)
    ig = _sigmoid(jnp.dot(xbb, wi, preferred_element_type=F32) + bi)
    log_a = (-LRU_C) * r * sp
    a = jnp.exp(log_a)
    mult = jnp.sqrt(jnp.maximum(-jnp.tanh(log_a) * (a * a + 1.0), 0.0))
    return r, ig, a, mult


def _softplus_neg(lam):
    e = jnp.exp(-jnp.abs(lam))
    sp = jnp.maximum(-lam, 0.0) + jnp.log(1.0 + e)
    sg = jnp.where(lam >= 0, e, 1.0) / (1.0 + e)
    return sp, sg


def _conv(pad_ref, w, b, t):
    acc = b + w[CONV_W - 1:CONV_W, :] * pad_ref[pl.ds(SUBLANES, t), :]
    for dlt in range(1, CONV_W):
        acc = acc + w[CONV_W - 1 - dlt:CONV_W - dlt, :] * pad_ref[pl.ds(SUBLANES - dlt, t), :]
    return acc


def _lru_specs(t, bw, nb, time_of):
    blk = lambda c0: pl.BlockSpec((t, bw), lambda n, i, c0=c0: (time_of(i), c0 + n))
    vec = pl.BlockSpec((1, bw), lambda n, i: (0, n))
    wspec = pl.BlockSpec((None, bw, bw), lambda n, i: (n, 0, 0))
    cwspec = pl.BlockSpec((CONV_W, bw), lambda n, i: (0, n))
    return blk, vec, wspec, cwspec


def lru_fwd(proj, conv_w, conv_b, w_r, b_r, w_i, b_i, lam, *, name):
    s, r2 = proj.shape
    rr = r2 // 2
    nb, bw, _ = w_r.shape
    t = _pick(s, (512, 256, 128, 64, 32))
    ngroups = t // SUBLANES

    def body(xp_ref, gate_ref, cw_ref, cb_ref, wr_ref, br_ref, wi_ref, bi_ref, lam_ref,
             m_ref, h_ref, pad, hcarry, a_scr, u_scr):
        i = pl.program_id(1)

        @pl.when(i == 0)
        def _():
            pad[0:SUBLANES, :] = jnp.zeros((SUBLANES, bw), F32)
            hcarry[...] = jnp.zeros_like(hcarry)

        xpre = xp_ref[...]
        pad[pl.ds(SUBLANES, t), :] = xpre
        xb = _conv(pad, cw_ref[...], cb_ref[...], t)
        pad[0:SUBLANES, :] = xpre[t - SUBLANES:, :]
        sp, _ = _softplus_neg(lam_ref[...])
        _, ig, a, mult = _lru_gates(xb, wr_ref[...], wi_ref[...], br_ref[...], bi_ref[...], sp)
        a_scr[...] = a
        u_scr[...] = mult * (ig * xb)
        row = lax.broadcasted_iota(jnp.int32, (SUBLANES, bw), 0)

        def groups(gi, hprev):
            offs = [pl.multiple_of((gi * SCAN_UNROLL + u) * SUBLANES, SUBLANES) for u in range(SCAN_UNROLL)]
            scanned = []
            for off in offs:
                av = a_scr[pl.ds(off, SUBLANES), :]
                uv = u_scr[pl.ds(off, SUBLANES), :]
                for dlt in (1, 2, 4):
                    keep = row >= dlt
                    uv = jnp.where(keep, av * pltpu.roll(uv, dlt, 0) + uv, uv)
                    av = jnp.where(keep, av * pltpu.roll(av, dlt, 0), av)
                scanned.append((av, uv))
            for off, (av, uv) in zip(offs, scanned):
                hv = av * hprev + uv
                h_ref[pl.ds(off, SUBLANES), :] = hv
                hprev = hv[SUBLANES - 1:SUBLANES, :]
            return hprev

        hcarry[...] = lax.fori_loop(0, ngroups // SCAN_UNROLL, groups, hcarry[...])
        gate = gate_ref[...]
        m_ref[...] = (h_ref[...] * (gate * _sigmoid(gate))).astype(BF16)

    blk, vec, wspec, cwspec = _lru_specs(t, bw, nb, lambda i: i)
    return pl.pallas_call(
        body, name=name, grid=(nb, s // t),
        in_specs=[blk(0), blk(nb), cwspec, vec, wspec, vec, wspec, vec, vec],
        out_specs=[blk(0), blk(0)],
        out_shape=[jax.ShapeDtypeStruct((s, rr), BF16), jax.ShapeDtypeStruct((s, rr), F32)],
        scratch_shapes=[pltpu.VMEM((t + SUBLANES, bw), F32), pltpu.VMEM((1, bw), F32),
                        pltpu.VMEM((t, bw), F32), pltpu.VMEM((t, bw), F32)],
        compiler_params=_params(("parallel", "arbitrary")),
    )(proj, proj, conv_w, conv_b, w_r, b_r, w_i, b_i, lam)


def lru_bwd(proj, hst, dm, conv_w, conv_b, w_r, b_r, w_i, b_i, lam, *, name):
    s, r2 = proj.shape
    rr = r2 // 2
    nb, bw, _ = w_r.shape
    t = _pick(s, (512, 256, 128, 64, 32))
    nt = s // t
    ngroups = t // SUBLANES
    nt_dims = (((1,), (1,)), ((), ()))
    tn_dims = (((0,), (0,)), ((), ()))

    def body(xp_ref, xhalo_ref, gate_ref, h_ref, hhalo_ref, dm_ref, cw_ref, cb_ref, wr_ref, br_ref, wi_ref,
             bi_ref, lam_ref,
             dxp_ref, dgate_ref, dcw_ref, dcb_ref, dwr_ref, dbr_ref, dwi_ref, dbi_ref, dlam_ref,
             pad, hpad, dpad, ecarry, a_scr, b_scr, d_scr):
        step = pl.program_id(1)

        @pl.when(step == 0)
        def _():
            dpad[pl.ds(t, SUBLANES), :] = jnp.zeros((SUBLANES, bw), F32)
            ecarry[...] = jnp.zeros_like(ecarry)
            dcw_ref[...] = jnp.zeros_like(dcw_ref)
            dcb_ref[...] = jnp.zeros_like(dcb_ref)
            dwr_ref[...] = jnp.zeros_like(dwr_ref)
            dbr_ref[...] = jnp.zeros_like(dbr_ref)
            dwi_ref[...] = jnp.zeros_like(dwi_ref)
            dbi_ref[...] = jnp.zeros_like(dbi_ref)
            dlam_ref[...] = jnp.zeros_like(dlam_ref)

        past = jnp.where(step == nt - 1, 0.0, 1.0)
        pad[0:SUBLANES, :] = xhalo_ref[...] * past
        pad[pl.ds(SUBLANES, t), :] = xp_ref[...]
        hpad[0:SUBLANES, :] = hhalo_ref[...] * past
        hpad[pl.ds(SUBLANES, t), :] = h_ref[...]
        cw = cw_ref[...]
        xb = _conv(pad, cw, cb_ref[...], t)
        sp, sg = _softplus_neg(lam_ref[...])
        wr = wr_ref[...]
        wi = wi_ref[...]
        r, ig, a, mult = _lru_gates(xb, wr, wi, br_ref[...], bi_ref[...], sp)
        gate = gate_ref[...]
        sgate = _sigmoid(gate)
        dmv = dm_ref[...]
        dgate_ref[...] = (dmv * h_ref[...] * (sgate * (1.0 + gate * (1.0 - sgate)))).astype(BF16)
        dy = dmv * (gate * sgate)
        a_scr[...] = a
        b_scr[...] = a * dy
        row = lax.broadcasted_iota(jnp.int32, (SUBLANES, bw), 0)

        def groups(gi, enext):
            offs = [pl.multiple_of((ngroups - 1 - gi * SCAN_UNROLL - u) * SUBLANES, SUBLANES)
                    for u in range(SCAN_UNROLL)]
            scanned = []
            for off in offs:
                av = a_scr[pl.ds(off, SUBLANES), :]
                bv = b_scr[pl.ds(off, SUBLANES), :]
                for dlt in (1, 2, 4):
                    keep = row < SUBLANES - dlt
                    bv = jnp.where(keep, av * pltpu.roll(bv, SUBLANES - dlt, 0) + bv, bv)
                    av = jnp.where(keep, av * pltpu.roll(av, SUBLANES - dlt, 0), av)
                scanned.append((av, bv))
            for off, (av, bv) in zip(offs, scanned):
                ev = av * enext + bv
                d_scr[pl.ds(off, SUBLANES), :] = jnp.where(row == SUBLANES - 1, enext,
                                                           pltpu.roll(ev, SUBLANES - 1, 0))
                enext = ev[0:1, :]
            return enext

        ecarry[...] = lax.fori_loop(0, ngroups // SCAN_UNROLL, groups, ecarry[...])
        dtot = dy + d_scr[...]
        da = dtot * hpad[pl.ds(SUBLANES - 1, t), :]
        dmult = dtot * (ig * xb)
        dlog_a = da * a - dmult * (a * a) / mult
        dr_pre = dlog_a * ((-LRU_C) * sp) * (r * (1.0 - r))
        di_pre = (dtot * mult * xb) * (ig * (1.0 - ig))
        dlam_ref[...] += jnp.sum(dlog_a * r, axis=0, keepdims=True) * (LRU_C * sg)
        dbr_ref[...] += jnp.sum(dr_pre, axis=0, keepdims=True)
        dbi_ref[...] += jnp.sum(di_pre, axis=0, keepdims=True)
        drb = dr_pre.astype(BF16)
        dib = di_pre.astype(BF16)
        xbb = xb.astype(BF16)
        dxb = (dtot * mult * ig
               + lax.dot_general(drb, wr, nt_dims, preferred_element_type=F32)
               + lax.dot_general(dib, wi, nt_dims, preferred_element_type=F32))
        dwr_ref[...] += lax.dot_general(xbb, drb, tn_dims, preferred_element_type=F32)
        dwi_ref[...] += lax.dot_general(xbb, dib, tn_dims, preferred_element_type=F32)
        dcb_ref[...] += jnp.sum(dxb, axis=0, keepdims=True)
        dpad[pl.ds(0, t), :] = dxb
        dxpre = cw[CONV_W - 1:CONV_W, :] * dxb
        dcw_ref[CONV_W - 1:CONV_W, :] += jnp.sum(dxb * pad[pl.ds(SUBLANES, t), :], axis=0, keepdims=True)
        for dlt in range(1, CONV_W):
            dxpre = dxpre + cw[CONV_W - 1 - dlt:CONV_W - dlt, :] * dpad[pl.ds(dlt, t), :]
            dcw_ref[CONV_W - 1 - dlt:CONV_W - dlt, :] += jnp.sum(
                dxb * pad[pl.ds(SUBLANES - dlt, t), :], axis=0, keepdims=True)
        dpad[pl.ds(t, SUBLANES), :] = dxb[0:SUBLANES, :]
        dxp_ref[...] = dxpre.astype(BF16)

    rev = lambda i: nt - 1 - i
    blk, vec, wspec, cwspec = _lru_specs(t, bw, nb, rev)
    halo = pl.BlockSpec((SUBLANES, bw), lambda n, i: (jnp.maximum(rev(i) * ngroups - 1, 0), n))
    return pl.pallas_call(
        body, name=name, grid=(nb, nt),
        in_specs=[blk(0), halo, blk(nb), blk(0), halo, blk(0), cwspec, vec, wspec, vec, wspec, vec, vec],
        out_specs=[blk(0), blk(0), cwspec, vec, wspec, vec, wspec, vec, vec],
        out_shape=[jax.ShapeDtypeStruct((s, rr), BF16), jax.ShapeDtypeStruct((s, rr), BF16),
                   jax.ShapeDtypeStruct((CONV_W, rr), F32), jax.ShapeDtypeStruct((1, rr), F32),
                   jax.ShapeDtypeStruct((nb, bw, bw), F32), jax.ShapeDtypeStruct((1, rr), F32),
                   jax.ShapeDtypeStruct((nb, bw, bw), F32), jax.ShapeDtypeStruct((1, rr), F32),
                   jax.ShapeDtypeStruct((1, rr), F32)],
        scratch_shapes=[pltpu.VMEM((t + SUBLANES, bw), F32), pltpu.VMEM((t + SUBLANES, bw), F32),
                        pltpu.VMEM((t + SUBLANES, bw), F32), pltpu.VMEM((1, bw), F32),
                        pltpu.VMEM((t, bw), F32), pltpu.VMEM((t, bw), F32), pltpu.VMEM((t, bw), F32)],
        compiler_params=_params(("parallel", "arbitrary")),
    )(proj, proj, proj, hst, hst, dm, conv_w, conv_b, w_r, b_r, w_i, b_i, lam)


def _softplus(z):
    return jnp.maximum(z, 0.0) + jnp.log(1.0 + jnp.exp2(jnp.abs(z) * (-LOG2E)))


def _att_blocks(s):
    bk = ATT_BLOCK if s % ATT_BLOCK == 0 else s
    bq = ATT_QTILES * bk if s % (ATT_QTILES * bk) == 0 else bk
    return bk, bq


def _tile_base(i, r):
    return r * ((i * (i + 1)) // 2)


def attn_fwd(projb, kv, *, name):
    s, a2 = projb.shape
    a = a2 // 2
    nh = a // HEAD_DIM
    bk, bq = _att_blocks(s)
    r = bq // bk
    nq = s // bq
    ntiles = _tile_base(nq, r)
    scale = 1.0 / math.sqrt(HEAD_DIM)
    nt_dims = (((1,), (1,)), ((), ()))
    hp = ATT_HEADS_FWD if nh % ATT_HEADS_FWD == 0 else 1
    wd = hp * HEAD_DIM

    def body(q_ref, g_ref, k_ref, v_ref, m_ref, o_ref, w_hbm, s_hbm, acc, stage, sems):
        hgrp, i = pl.program_id(0), pl.program_id(1)
        base = _tile_base(i, r)
        qb = (q_ref[...] * scale).astype(BF16)
        from_mat = (lax.broadcasted_iota(jnp.int32, (bk, bk), 0)
                    >= lax.broadcasted_iota(jnp.int32, (bk, bk), 1)).astype(BF16)
        rowi = lax.broadcasted_iota(jnp.int32, (bq, bk), 0)
        coli = lax.broadcasted_iota(jnp.int32, (bq, bk), 1)
        cols = [slice(hh * HEAD_DIM, (hh + 1) * HEAD_DIM) for hh in range(hp)]

        def saves(slot, hh, j):
            head = hgrp * hp + hh
            return [pltpu.make_async_copy(stage.at[slot, which, hh], dst.at[head, base + j], sems.at[slot, which, hh])
                    for which, dst in enumerate((w_hbm, s_hbm))]

        def tile(j, n, carries, diag):
            causal = None if diag is None else (coli + diag * bk) < rowi
            slot = n % 2

            def free_slot():
                for hh in range(hp):
                    for cp in saves(slot, hh, 0):
                        cp.wait()

            if isinstance(n, int):
                if n >= 2:
                    free_slot()
            elif r >= 2:
                free_slot()
            else:
                pl.when(n >= 2)(free_slot)
            rows = pl.ds(pl.multiple_of(j * bk, bk), bk)
            zs = [lax.dot_general(qb[:, c], k_ref[rows, c], nt_dims, preferred_element_type=F32) for c in cols]
            sums, sigs = [], []
            for z in zs:
                sp = _softplus(z)
                sig = jnp.exp(z - sp)
                if causal is not None:
                    sp = jnp.where(causal, sp, 0.0)
                    sig = jnp.where(causal, sig, 0.0)
                sums.append(jnp.dot(sp.astype(BF16), from_mat, preferred_element_type=F32))
                sigs.append(sig.astype(BF16))
            out = []
            for hh in range(hp):
                w = jnp.exp(zs[hh] - sums[hh] - carries[hh])
                if causal is not None:
                    w = jnp.where(causal, w, 0.0)
                wb = w.astype(BF16)
                acc[:, cols[hh]] += jnp.dot(wb, v_ref[rows, cols[hh]], preferred_element_type=F32)
                stage[slot, 0, hh] = wb
                stage[slot, 1, hh] = sigs[hh]
                out.append(carries[hh] + sums[hh][:, 0:1])
            for hh in range(hp):
                for cp in saves(slot, hh, j):
                    cp.start()
            return tuple(out)

        acc[...] = jnp.zeros_like(acc)
        carries = tuple(jnp.zeros((bq, 1), F32) for _ in range(hp))
        for n, dg in enumerate(reversed(range(r))):
            carries = tile(r * i + dg, n, carries, dg)
        lax.fori_loop(0, r * i, lambda jj, c: tile(r * i - 1 - jj, r + jj, c, None), carries)
        ntile = r * (i + 1)
        for back in (1, 2):
            def drain(back=back):
                for hh in range(hp):
                    for cp in saves((ntile - back) % 2, hh, 0):
                        cp.wait()
            if r >= back:
                drain()
            else:
                pl.when(ntile >= back)(drain)
        o = acc[...]
        o_ref[...] = o
        gate = g_ref[...]
        m_ref[...] = (o * (gate * _sigmoid(gate))).astype(BF16)

    ng = nh // hp
    qspec = lambda c0: pl.BlockSpec((bq, wd), lambda h, i, c0=c0: (i, c0 + h))
    kspec = lambda c0: pl.BlockSpec((s, wd), lambda h, i, c0=c0: (0, c0 + h), pipeline_mode=pl.Buffered(1))
    hbm = pl.BlockSpec(memory_space=pl.ANY)
    saved = jax.ShapeDtypeStruct((nh, ntiles, bq, bk), BF16)
    return pl.pallas_call(
        body, name=name, grid=(ng, nq),
        in_specs=[qspec(0), qspec(ng), kspec(0), kspec(ng)],
        out_specs=[qspec(0), qspec(0), hbm, hbm],
        out_shape=[jax.ShapeDtypeStruct((s, a), BF16), jax.ShapeDtypeStruct((s, a), F32), saved, saved],
        scratch_shapes=[pltpu.VMEM((bq, wd), F32), pltpu.VMEM((2, 2, hp, bq, bk), BF16),
                        pltpu.SemaphoreType.DMA((2, 2, hp))],
        compiler_params=_params(("arbitrary", "arbitrary")),
    )(projb, projb, kv, kv)


def attn_bwd(projb, dm, o, kv, wsave, ssave, *, name):
    s, a2 = projb.shape
    a = a2 // 2
    nh = a // HEAD_DIM
    bk, bq = _att_blocks(s)
    r = bq // bk
    nq = s // bq
    scale = 1.0 / math.sqrt(HEAD_DIM)
    nt_dims = (((1,), (1,)), ((), ()))
    tn_dims = (((0,), (0,)), ((), ()))
    hp = ATT_HEADS_BWD if nh % ATT_HEADS_BWD == 0 else 1
    wd = hp * HEAD_DIM

    def body(q_ref, g_ref, dm_ref, o_ref, k_ref, v_ref, w_hbm, s_hbm, dq_ref, dg_ref, dk_ref, dv_ref,
             dk_acc, dv_acc, dq_acc, stage, sems):
        hgrp, i = pl.program_id(0), pl.program_id(1)
        base = _tile_base(i, r)
        ntile = r * (i + 1)

        @pl.when(i == 0)
        def _():
            dk_acc[...] = jnp.zeros_like(dk_acc)
            dv_acc[...] = jnp.zeros_like(dv_acc)

        def fetches(slot, j):
            return [pltpu.make_async_copy(src.at[hgrp * hp + hh, base + j], stage.at[slot, which, hh],
                                          sems.at[slot, which, hh])
                    for hh in range(hp) for which, src in enumerate((w_hbm, s_hbm))]

        for cp in fetches(0, 0):
            cp.start()
        qb = (q_ref[...] * scale).astype(BF16)
        gate = g_ref[...]
        sgate = _sigmoid(gate)
        dmv = dm_ref[...]
        dob = (dmv * (gate * sgate)).astype(BF16)
        dg_ref[...] = (dmv * o_ref[...] * (sgate * (1.0 + gate * (1.0 - sgate)))).astype(BF16)
        upto_mat = (lax.broadcasted_iota(jnp.int32, (bk, bk), 0)
                    <= lax.broadcasted_iota(jnp.int32, (bk, bk), 1)).astype(BF16)
        cols = [slice(hh * HEAD_DIM, (hh + 1) * HEAD_DIM) for hh in range(hp)]
        dq_acc[...] = jnp.zeros_like(dq_acc)

        def tile(j, gcarries):
            slot = j % 2

            @pl.when(j + 1 < ntile)
            def _():
                for cp in fetches(1 - slot, j + 1):
                    cp.start()

            for cp in fetches(slot, j):
                cp.wait()
            rows = pl.ds(pl.multiple_of(j * bk, bk), bk)
            dws = [lax.dot_general(dob[:, c], v_ref[rows, c], nt_dims, preferred_element_type=F32) for c in cols]
            gs, totals = [], []
            for hh in range(hp):
                wb = stage[slot, 0, hh]
                g = wb.astype(F32) * dws[hh]
                dv_acc[rows, cols[hh]] += lax.dot_general(wb, dob[:, cols[hh]], tn_dims, preferred_element_type=F32)
                totals.append(jnp.dot(g.astype(BF16), upto_mat, preferred_element_type=F32))
                gs.append(g)
            out = []
            for hh in range(hp):
                dz = gs[hh] - (totals[hh] + gcarries[hh]) * stage[slot, 1, hh].astype(F32)
                dzb = dz.astype(BF16)
                dq_acc[:, cols[hh]] += jnp.dot(dzb, k_ref[rows, cols[hh]], preferred_element_type=F32)
                dk_acc[rows, cols[hh]] += lax.dot_general(dzb, qb[:, cols[hh]], tn_dims, preferred_element_type=F32)
                out.append(gcarries[hh] + totals[hh][:, bk - 1:bk])
            return tuple(out)

        lax.fori_loop(0, ntile, tile, tuple(jnp.zeros((bq, 1), F32) for _ in range(hp)))
        dq_ref[...] = (dq_acc[...] * scale).astype(BF16)

        @pl.when(i == nq - 1)
        def _():
            dk_ref[...] = dk_acc[...].astype(BF16)
            dv_ref[...] = dv_acc[...].astype(BF16)

    ng = nh // hp
    once = pl.Buffered(1)
    qspec = lambda c0: pl.BlockSpec((bq, wd), lambda h, i, c0=c0: (i, c0 + h))
    kspec = lambda c0: pl.BlockSpec((s, wd), lambda h, i, c0=c0: (0, c0 + h), pipeline_mode=once)
    hbm = pl.BlockSpec(memory_space=pl.ANY)
    return pl.pallas_call(
        body, name=name, grid=(ng, nq),
        in_specs=[qspec(0), qspec(ng), qspec(0), qspec(0), kspec(0), kspec(ng), hbm, hbm],
        out_specs=[qspec(0), qspec(0), kspec(0), kspec(0)],
        out_shape=[jax.ShapeDtypeStruct((s, a), BF16)] * 4,
        scratch_shapes=[pltpu.VMEM((s, wd), F32), pltpu.VMEM((s, wd), F32), pltpu.VMEM((bq, wd), F32),
                        pltpu.VMEM((2, 2, hp, bq, bk), BF16), pltpu.SemaphoreType.DMA((2, 2, hp))],
        compiler_params=_params(("arbitrary", "arbitrary")),
    )(projb, projb, dm, o, kv, kv, wsave, ssave)


def _as2d(x):
    n = x.size
    cols = x.shape[-1]
    if cols % LANES != 0:
        cols = LANES
    return x.reshape(n // cols, cols)


def sum_parts(parts, *, name):
    p, rows, cols = parts.shape
    tr = _pick(rows, (512, 256, 128, 64, 32, 16))

    def body(p_ref, o_ref):
        acc = p_ref[0].astype(F32)
        for k in range(1, p):
            acc = acc + p_ref[k].astype(F32)
        o_ref[...] = acc

    return pl.pallas_call(
        body, name=name, grid=(rows // tr,),
        in_specs=[pl.BlockSpec((p, tr, cols), lambda i: (0, i, 0))],
        out_specs=pl.BlockSpec((tr, cols), lambda i: (i, 0)),
        out_shape=jax.ShapeDtypeStruct((rows, cols), F32),
        compiler_params=_params(("parallel",)),
    )(parts)


def adamw(w, g_parts, m, v, *, name):
    rows, cols = w.shape
    tr = _pick(rows, (128, 64, 32, 16, 8))
    np_ = len(g_parts)
    c1 = 1.0 / (1.0 - ADAM_B1 ** ADAM_STEP)
    c2 = 1.0 / (1.0 - ADAM_B2 ** ADAM_STEP)

    def body(*refs):
        w_ref, m_ref, v_ref = refs[0], refs[1], refs[2]
        g_refs = refs[3:3 + np_]
        go_ref, d_ref, mo_ref, vo_ref = refs[3 + np_:]
        g = g_refs[0][...]
        for gr in g_refs[1:]:
            g = g + gr[...]
        mn = ADAM_B1 * m_ref[...] + (1.0 - ADAM_B1) * g
        vn = ADAM_B2 * v_ref[...] + (1.0 - ADAM_B2) * (g * g)
        go_ref[...] = g
        mo_ref[...] = mn
        vo_ref[...] = vn
        d_ref[...] = (-ADAM_LR) * ((mn * c1) / (jnp.sqrt(vn * c2) + ADAM_EPS) + ADAM_WD * w_ref[...])

    spec = pl.BlockSpec((tr, cols), lambda i: (i, 0))
    return pl.pallas_call(
        body, name=name, grid=(rows // tr,),
        in_specs=[spec] * (3 + np_), out_specs=[spec] * 4,
        out_shape=[jax.ShapeDtypeStruct((rows, cols), F32)] * 4,
        compiler_params=_params(("parallel",)),
    )(w, m, v, *g_parts)


def exchange(kind, arrays, axes, *, name):
    na = len(arrays)
    hbm = pl.BlockSpec(memory_space=pl.ANY)

    def body(*refs):
        pairs = _exchange(kind, refs[:na], refs[na:2 * na], axes, *refs[2 * na:])
        for start, _ in pairs:
            start()
        for _, finish in pairs:
            finish()

    return pl.pallas_call(
        body, name=name, in_specs=[hbm] * na, out_specs=[hbm] * na,
        out_shape=_exchange_shapes(kind, arrays, axes), scratch_shapes=_exchange_sems(na),
    )(*arrays)


def swap_cores(arrs, *, name):
    na = len(arrs)
    hbm = pl.BlockSpec(memory_space=pl.ANY)

    def body(*refs):
        a_refs = refs[:na]
        o_refs = refs[na:2 * na]
        send_sems, recv_sems = refs[2 * na:]
        x, y, c = _place()
        copies = []
        for ai in range(na):
            cp = pltpu.make_async_remote_copy(
                src_ref=a_refs[ai], dst_ref=o_refs[ai], send_sem=send_sems.at[ai], recv_sem=recv_sems.at[ai],
                device_id=(x, y, 1 - c), device_id_type=MESH)
            cp.start()
            copies.append(cp)
        for cp in copies:
            cp.wait()

    return pl.pallas_call(
        body, name=name,
        in_specs=[hbm] * na, out_specs=[hbm] * na,
        out_shape=[jax.ShapeDtypeStruct(a.shape, a.dtype) for a in arrs],
        scratch_shapes=[pltpu.SemaphoreType.DMA((na,)), pltpu.SemaphoreType.DMA((na,))],
    )(*arrs)


def allreduce_small(buf, *, name):
    rows, cols = buf.shape

    def body(b_ref, o_ref, slots, send_sems, recv_sems):
        x, y, c = _place()
        me = 4 * x + 2 * y + c
        slots[0] = b_ref[...]
        copies = []
        for rel in range(1, 8):
            peer = (x ^ (rel >> 2), y ^ ((rel >> 1) & 1), c ^ (rel & 1))
            cp = pltpu.make_async_remote_copy(
                src_ref=b_ref, dst_ref=slots.at[rel], send_sem=send_sems.at[rel - 1],
                recv_sem=recv_sems.at[rel - 1], device_id=peer, device_id_type=MESH)
            cp.start()
            copies.append(cp)
        for cp in copies:
            cp.wait()
        acc = slots[me]
        for dev in range(1, 8):
            acc = acc + slots[dev ^ me]
        o_ref[...] = acc

    vm = pl.BlockSpec(memory_space=pltpu.VMEM)
    return pl.pallas_call(
        body, name=name, in_specs=[vm], out_specs=vm,
        out_shape=jax.ShapeDtypeStruct((rows, cols), F32),
        scratch_shapes=[pltpu.VMEM((8, rows, cols), F32), pltpu.SemaphoreType.DMA((7,)),
                        pltpu.SemaphoreType.DMA((7,))],
    )(buf)


def _pack_rows(arrs):
    parts = []
    for a in arrs:
        p = a.reshape(-1, LANES)
        parts.append(jnp.pad(p, ((0, (-p.shape[0]) % SUBLANES), (0, 0))))
    return jnp.concatenate(parts, axis=0)


def _unpack_rows(buf, shapes):
    out, r0 = [], 0
    for shp in shapes:
        n = math.prod(shp) // LANES
        out.append(buf[r0:r0 + n].reshape(shp))
        r0 += n + (-n) % SUBLANES
    return out


def kernel(x, a_norm, a_w_in, a_conv_w, a_conv_b, a_w_r, a_b_r, a_w_i, a_b_i, a_lambda, a_w_out, kv_norm, w_kv, b_norm, b_w_in, b_w_out, final_norm, loss_target, m_a_norm, m_a_w_in, m_a_conv_w, m_a_conv_b, m_a_w_r, m_a_b_r, m_a_w_i, m_a_b_i, m_a_lambda, m_a_w_out, m_kv_norm, m_w_kv, m_b_norm, m_b_w_in, m_b_w_out, m_final_norm, v_a_norm, v_a_w_in, v_a_conv_w, v_a_conv_b, v_a_w_r, v_a_b_r, v_a_w_i, v_a_b_i, v_a_lambda, v_a_w_out, v_kv_norm, v_w_kv, v_b_norm, v_b_w_in, v_b_w_out, v_final_norm):
    weights = dict(a_norm=a_norm, a_w_in=a_w_in, a_conv_w=a_conv_w, a_conv_b=a_conv_b, a_w_r=a_w_r, a_b_r=a_b_r,
                   a_w_i=a_w_i, a_b_i=a_b_i, a_lambda=a_lambda, a_w_out=a_w_out, kv_norm=kv_norm, w_kv=w_kv,
                   b_norm=b_norm, b_w_in=b_w_in, b_w_out=b_w_out, final_norm=final_norm)
    mom1 = dict(a_norm=m_a_norm, a_w_in=m_a_w_in, a_conv_w=m_a_conv_w, a_conv_b=m_a_conv_b, a_w_r=m_a_w_r,
                a_b_r=m_a_b_r, a_w_i=m_a_w_i, a_b_i=m_a_b_i, a_lambda=m_a_lambda, a_w_out=m_a_w_out,
                kv_norm=m_kv_norm, w_kv=m_w_kv, b_norm=m_b_norm, b_w_in=m_b_w_in, b_w_out=m_b_w_out,
                final_norm=m_final_norm)
    mom2 = dict(a_norm=v_a_norm, a_w_in=v_a_w_in, a_conv_w=v_a_conv_w, a_conv_b=v_a_conv_b, a_w_r=v_a_w_r,
                a_b_r=v_a_b_r, a_w_i=v_a_w_i, a_b_i=v_a_b_i, a_lambda=v_a_lambda, a_w_out=v_a_w_out,
                kv_norm=v_kv_norm, w_kv=v_w_kv, b_norm=v_b_norm, b_w_in=v_b_w_in, b_w_out=v_b_w_out,
                final_norm=v_final_norm)
    order = list(weights)
    x0 = x[0]
    target = loss_target[0]
    d = x0.shape[1]
    chip = 2 * lax.axis_index("x") + lax.axis_index("y")

    big = ["a_w_in", "a_w_r", "a_w_i", "a_w_out", "w_kv", "b_w_in", "b_w_out"]
    big_axis = dict(a_w_in=1, a_w_r=1, a_w_i=1, a_w_out=0, w_kv=1, b_w_in=1, b_w_out=0)
    local = dict(a_w_in=a_w_in[0], a_w_r=a_w_r[0], a_w_i=a_w_i[0], a_w_out=a_w_out[0], w_kv=w_kv,
                 b_w_in=b_w_in[0], b_w_out=b_w_out[0])
    shards = {n: local[n].astype(BF16) for n in big}
    first = ["a_w_in", "a_w_r", "a_w_i"]
    full = exchange("gather", [shards[n] for n in first] + [a_conv_w[0], b_norm],
                    [big_axis[n] for n in first] + [1, 1], name="gather_first")
    wf = dict(zip(first + ["a_conv_w", "b_norm"], full))
    wf.update(a_norm=a_norm, a_conv_b=a_conv_b, a_b_r=a_b_r, a_b_i=a_b_i, a_lambda=a_lambda,
              kv_norm=kv_norm.reshape(1, d), final_norm=final_norm.reshape(1, d))
    loss_part, grad_x, parts, gsmall = _local_grads(x0, target, wf, shards=shards, axes=big_axis)

    sums = [sum_parts(parts[n].reshape(4, *_as2d(parts[n][0]).shape), name="sum_" + n) for n in big]
    others = swap_cores(sums, name="swap_cores")

    small = ["a_norm", "a_conv_b", "a_b_r", "a_b_i", "a_lambda", "kv_norm", "final_norm", "a_conv_w", "b_norm"]
    buf = _pack_rows([gsmall[n] for n in small] + [loss_part])
    red = allreduce_small(buf, name="allreduce_small")
    red_list = _unpack_rows(red, [gsmall[n].shape for n in small] + [(1, LANES)])
    gs = dict(zip(small, red_list[:-1]))
    loss = red_list[-1][0, 0]
    n_conv = a_conv_w.shape[2]
    gs["a_conv_w"] = lax.dynamic_slice_in_dim(gs["a_conv_w"], chip * n_conv, n_conv, axis=1)
    n_bn = b_norm.shape[1]
    gs["b_norm"] = lax.dynamic_slice_in_dim(gs["b_norm"], chip * n_bn, n_bn, axis=1)

    grads, deltas, new_m, new_v = {}, {}, {}, {}
    for n, s_mine, s_other in zip(big, sums, others):
        shp = weights[n].shape
        g, dlt, mn, vn = adamw(_as2d(weights[n]), [s_mine, s_other], _as2d(mom1[n]), _as2d(mom2[n]),
                               name="adamw_" + n)
        grads[n], deltas[n], new_m[n], new_v[n] = (t.reshape(shp) for t in (g, dlt, mn, vn))
    shapes = [weights[n].shape for n in small]
    wpk, gpk, mpk, vpk = (_pack_rows([src[n] for n in small]) for src in (weights, gs, mom1, mom2))
    outs = adamw(wpk, [gpk], mpk, vpk, name="adamw_small")
    for dst, packed in zip((grads, deltas, new_m, new_v), outs):
        for n, val in zip(small, _unpack_rows(packed, shapes)):
            dst[n] = val

    return (loss, grad_x[None], *[grads[n] for n in order], *[deltas[n] for n in order],
            *[new_m[n] for n in order], *[new_v[n] for n in order])


def _local_grads(x0, target, wf, shards=None, axes=None):
    a_norm, a_conv_b, a_b_r, a_b_i, a_lambda = (wf[n] for n in ("a_norm", "a_conv_b", "a_b_r", "a_b_i", "a_lambda"))
    kv_norm, final_norm = wf["kv_norm"], wf["final_norm"]
    wf = dict(wf)
    parts = {}

    def mm(*args, gather=(), scatter=None, **kw):
        if shards is None or not (gather or scatter):
            return matmul(*args, **kw)
        if gather:
            out, got = matmul(*args, exchange=("gather", [shards[n] for n in gather], [axes[n] for n in gather]), **kw)
            wf.update(zip(gather, got))
        else:
            out, got = matmul(*args, exchange=("scatter", list(scatter.values()), [axes[n] for n in scatter]), **kw)
            parts.update(zip(scatter, got))
        return out

    (h_a,) = rms_fwd(x0, [a_norm], name="norm_a")
    proj_a = mm(h_a, wf["a_w_in"], gather=("a_w_out",), name="a_in")
    m_a, hst = lru_fwd(proj_a, wf["a_conv_w"], a_conv_b, wf["a_w_r"], a_b_r, wf["a_w_i"], a_b_i, a_lambda,
                       name="lru_fwd")
    x1 = mm(m_a, wf["a_w_out"], residual=x0, gather=("w_kv",), name="a_out")
    kvn, hb = rms_fwd(x1, [kv_norm, wf["b_norm"]], name="norm_kv_b")
    kv = mm(kvn, wf["w_kv"], out_dtype=BF16, gather=("b_w_in",), name="kv_proj")
    proj_b = mm(hb, wf["b_w_in"], gather=("b_w_out",), name="b_in")
    m_b, o, wsave, ssave = attn_fwd(proj_b, kv, name="attn_fwd")
    x2 = mm(m_b, wf["b_w_out"], residual=x1, name="b_out")
    loss_part, g_final, dx2, dx2b = loss_bwd(x2, target, final_norm, name="loss_bwd")

    dm_b = mm(dx2b, wf["b_w_out"], tb=True, name="b_out_dx")
    g_b_w_out = mm(m_b, dx2b, ta=True, out_dtype=BF16, name="b_out_dw")
    dq, dgate_b, dk, dv = attn_bwd(proj_b, dm_b, o, kv, wsave, ssave, name="attn_bwd")
    dproj_b = (dq, dgate_b)
    dkv = (dk, dv)
    g_b_w_in = mm(hb, dproj_b, ta=True, out_dtype=BF16, scatter=dict(b_w_out=g_b_w_out), name="b_in_dw")
    g_w_kv = mm(kvn, dkv, ta=True, out_dtype=BF16, scatter=dict(b_w_in=g_b_w_in), name="kv_dw")
    dhb = mm(dproj_b, wf["b_w_in"], tb=True, scatter=dict(w_kv=g_w_kv), name="b_in_dx")
    dkvn = mm(dkv, wf["w_kv"], tb=True, name="kv_dx")
    dx1, dx1b, (g_kv_norm, g_b_norm) = rms_bwd(
        x1, dx2, [(kv_norm, dkvn), (wf["b_norm"], dhb)], name="norm_kv_b_bwd")

    g_a_w_out = mm(m_a, dx1b, ta=True, out_dtype=BF16, name="a_out_dw")
    dm_a = mm(dx1b, wf["a_w_out"], tb=True, scatter=dict(a_w_out=g_a_w_out), name="a_out_dx")
    dxpre, dgate_a, g_conv_w, g_conv_b, g_w_r, g_b_r, g_w_i, g_b_i, g_lambda = lru_bwd(
        proj_a, hst, dm_a, wf["a_conv_w"], a_conv_b, wf["a_w_r"], a_b_r, wf["a_w_i"], a_b_i, a_lambda,
        name="lru_bwd")
    dproj_a = (dxpre, dgate_a)
    g_w_r, g_w_i = g_w_r.astype(BF16), g_w_i.astype(BF16)
    g_a_w_in = mm(h_a, dproj_a, ta=True, out_dtype=BF16, scatter=dict(a_w_r=g_w_r, a_w_i=g_w_i), name="a_in_dw")
    dh_a = mm(dproj_a, wf["a_w_in"], tb=True, scatter=dict(a_w_in=g_a_w_in), name="a_in_dx")
    grad_x, _, (g_a_norm,) = rms_bwd(x0, dx1, [(a_norm, dh_a)], name="norm_a_bwd")

    gbig = parts if shards is not None else dict(
        a_w_in=g_a_w_in, a_w_r=g_w_r, a_w_i=g_w_i, a_w_out=g_a_w_out, w_kv=g_w_kv, b_w_in=g_b_w_in, b_w_out=g_b_w_out)
    gsmall = dict(a_norm=g_a_norm, a_conv_b=g_conv_b, a_b_r=g_b_r, a_b_i=g_b_i, a_lambda=g_lambda,
                  kv_norm=g_kv_norm, final_norm=g_final, a_conv_w=g_conv_w, b_norm=g_b_norm)
    return loss_part, grad_x, gbig, gsmall
```

```python
import math

import jax
import jax.numpy as jnp
from jax import lax
from jax.experimental import pallas as pl
from jax.experimental.pallas import tpu as pltpu

F32 = jnp.float32
BF16 = jnp.bfloat16
MESH = pl.DeviceIdType.MESH

EPS = 1e-6
LRU_C = 8.0
CONV_W = 4
HEAD_DIM = 128
ADAM_LR = 0.001
ADAM_B1 = 0.9
ADAM_B2 = 0.999
ADAM_EPS = 1e-08
ADAM_WD = 0.01
ADAM_STEP = 10

V7X_VMEM_LIMIT = 56 * 1024 * 1024
LANES = 128
SUBLANES = 8
ATT_BLOCK = 256
ATT_QTILES = 2
ATT_HEADS = 2
ATT_FETCH_AHEAD = 2
LOG2E = 1.4426950408889634
SCAN_UNROLL = 4


def _pick(dim, cands):
    for c in cands:
        if dim % c == 0:
            return c
    return dim


def _params(sem, vmem=V7X_VMEM_LIMIT):
    return pltpu.CompilerParams(dimension_semantics=sem, vmem_limit_bytes=vmem)


def _sigmoid(x):
    return 1.0 / (1.0 + jnp.exp(-x))


def _place():
    return lax.axis_index("x"), lax.axis_index("y"), lax.axis_index("c")


def _chip_peers(x, y, c):
    return [(1 - x, y, c), (x, 1 - y, c), (1 - x, 1 - y, c)]


def _shard_of(ref, axis, idx, n):
    start = pl.multiple_of(idx * n, n)
    sl = [slice(None)] * len(ref.shape)
    sl[axis] = pl.ds(start, n)
    return ref.at[tuple(sl)]


def _half_rows(ref, h):
    n = ref.shape[0] // 2
    return ref.at[pl.ds(pl.multiple_of(h * n, n), n)]


def _block_half(ref, axis, idx, n, h):
    if axis == 0:
        return ref.at[pl.ds(pl.multiple_of(idx * n + h * (n // 2), n // 2), n // 2)]
    return _half_rows(_shard_of(ref, axis, idx, n), h)


def _exchange(kind, in_refs, out_refs, axes, send_sems, recv_sems, local_sems, send2_sems=None, recv2_sems=None):
    x, y, c = _place()
    me = 2 * x + y
    peers = _chip_peers(x, y, c)
    sibling = (x, y, 1 - c)
    triples = []
    nothing = lambda: None
    for ai, (src, dst, ax) in enumerate(zip(in_refs, out_refs, axes)):
        if kind == "scatter":
            n = dst.shape[1 + ax]
            loc = pltpu.make_async_copy(_shard_of(src, ax, me, n), dst.at[0], local_sems.at[ai])
        else:
            n = src.shape[ax]
            mine = _shard_of(dst, ax, me, n)
            loc = pltpu.make_async_copy(src, mine, local_sems.at[ai])
        triples.append((loc.start, nothing, loc.wait))
        for k, peer in enumerate(peers):
            sem = dict(send_sem=send_sems.at[ai * 3 + k], recv_sem=recv_sems.at[ai * 3 + k],
                       device_id=peer, device_id_type=MESH)
            theirs = 2 * peer[0] + peer[1]
            if kind == "gather":
                snd = pltpu.make_async_remote_copy(src_ref=src, dst_ref=mine, **sem)
                rcv = pltpu.make_async_remote_copy(src_ref=src, dst_ref=_shard_of(dst, ax, theirs, n), **sem)
                triples.append((snd.start, nothing, lambda snd=snd, rcv=rcv: (snd.wait_send(), rcv.wait_recv())))
            elif kind == "scatter":
                snd = pltpu.make_async_remote_copy(src_ref=_shard_of(src, ax, theirs, n), dst_ref=dst.at[1 + k], **sem)
                triples.append((snd.start, nothing, snd.wait))
            else:
                landed = _block_half(dst, ax, theirs, n, c)
                snd = pltpu.make_async_remote_copy(src_ref=_half_rows(src, c), dst_ref=_block_half(dst, ax, me, n, c), **sem)
                rcv = pltpu.make_async_remote_copy(src_ref=_half_rows(src, c), dst_ref=landed, **sem)
                sem2 = dict(send_sem=send2_sems.at[ai * 3 + k], recv_sem=recv2_sems.at[ai * 3 + k],
                            device_id=sibling, device_id_type=MESH)
                fwd = pltpu.make_async_remote_copy(src_ref=landed, dst_ref=landed, **sem2)
                got = pltpu.make_async_remote_copy(src_ref=landed, dst_ref=_block_half(dst, ax, theirs, n, 1 - c), **sem2)
                triples.append((snd.start, lambda rcv=rcv, fwd=fwd: (rcv.wait_recv(), fwd.start()),
                                lambda snd=snd, fwd=fwd, got=got: (snd.wait_send(), fwd.wait_send(), got.wait_recv())))
    return triples


def _exchange_shapes(kind, arrays, axes):
    out = []
    for arr, ax in zip(arrays, axes):
        shp = list(arr.shape)
        if kind == "scatter":
            shp[ax] //= 4
            out.append(jax.ShapeDtypeStruct((4, *shp), arr.dtype))
        else:
            shp[ax] *= 4
            out.append(jax.ShapeDtypeStruct(tuple(shp), arr.dtype))
    return out


def _exchange_sems(kind, n):
    sems = [pltpu.SemaphoreType.DMA((3 * n,)), pltpu.SemaphoreType.DMA((3 * n,)), pltpu.SemaphoreType.DMA((n,))]
    if kind == "gather_halves":
        sems += [pltpu.SemaphoreType.DMA((3 * n,)), pltpu.SemaphoreType.DMA((3 * n,))]
    return sems


def matmul(a, b, *, ta=False, tb=False, out_dtype=F32, residual=None, exchange=None, name):
    a_pair = a if isinstance(a, (tuple, list)) else None
    b_pair = b if isinstance(b, (tuple, list)) else None
    assert not (a_pair and ta) and not (b_pair and tb) and not (a_pair and b_pair)
    a0 = a_pair[0] if a_pair else a
    b0 = b_pair[0] if b_pair else b
    m = a0.shape[1] if ta else a0.shape[0]
    kdim = (a0.shape[0] if ta else a0.shape[1]) * (2 if a_pair else 1)
    n = (b0.shape[0] if tb else b0.shape[1]) * (2 if b_pair else 1)
    assert (b0.shape[1] if tb else b0.shape[0]) == kdim
    tm = _pick(m, (1024, 640, 512, 256, 128))
    tn = _pick(n // 2 if b_pair else n, (1024, 640, 512, 256, 128))
    tk = _pick(kdim // 2 if a_pair else kdim, (2560, 2048, 1024, 512, 256, 128))
    grid = (m // tm, n // tn, kdim // tk)
    nk = grid[2]
    kh, jh = nk // 2, grid[1] // 2
    dn = (((0 if ta else 1,), (1 if tb else 0,)), ((), ()))
    na = 2 if a_pair else 1
    nb = 2 if b_pair else 1
    nres = 0 if residual is None else 1
    nex = 0 if exchange is None else len(exchange[1])

    def body(*refs):
        a_refs, b_refs = refs[:na], refs[na:na + nb]
        p = na + nb
        r_ref = refs[p] if nres else None
        ex_in = refs[p + nres:p + nres + nex]
        o_ref = refs[p + nres + nex]
        ex_out = refs[p + 1 + nres + nex:p + 1 + nres + 2 * nex]
        acc = refs[p + 1 + nres + 2 * nex]
        sems = refs[p + 2 + nres + 2 * nex:]
        i, j, k = pl.program_id(0), pl.program_id(1), pl.program_id(2)
        if nex:
            @pl.when((i == 0) & (j == 0) & (k == 0))
            def _():
                for start, _, _ in _exchange(exchange[0], ex_in, ex_out, exchange[2], *sems):
                    start()

        @pl.when(k == 0)
        def _():
            acc[...] = jnp.zeros_like(acc)

        def accumulate(a_ref, b_ref):
            acc[...] += lax.dot_general(a_ref[...].astype(BF16), b_ref[...].astype(BF16), dn,
                                        preferred_element_type=F32)

        if a_pair:
            pl.when(k < kh)(lambda: accumulate(a_refs[0], b_refs[0]))
            pl.when(k >= kh)(lambda: accumulate(a_refs[1], b_refs[0]))
        elif b_pair:
            pl.when(j < jh)(lambda: accumulate(a_refs[0], b_refs[0]))
            pl.when(j >= jh)(lambda: accumulate(a_refs[0], b_refs[1]))
        else:
            accumulate(a_refs[0], b_refs[0])

        @pl.when(k == nk - 1)
        def _():
            r = acc[...]
            if r_ref is not None:
                r = r + r_ref[...]
            o_ref[...] = r.astype(out_dtype)

        if nex:
            @pl.when((i == grid[0] - 1) & (j == grid[1] - 1) & (k == nk - 1))
            def _():
                triples = _exchange(exchange[0], ex_in, ex_out, exchange[2], *sems)
                for _, relay, _ in triples:
                    relay()
                for _, _, finish in triples:
                    finish()

    if a_pair:
        a_specs = [pl.BlockSpec((tm, tk), lambda i, j, k: (i, jnp.minimum(k, kh - 1))),
                   pl.BlockSpec((tm, tk), lambda i, j, k: (i, jnp.maximum(k - kh, 0)))]
    else:
        a_specs = [pl.BlockSpec((tk, tm), lambda i, j, k: (k, i)) if ta
                   else pl.BlockSpec((tm, tk), lambda i, j, k: (i, k))]
    if b_pair:
        b_specs = [pl.BlockSpec((tk, tn), lambda i, j, k: (jnp.where(j < jh, k, nk - 1), jnp.minimum(j, jh - 1))),
                   pl.BlockSpec((tk, tn), lambda i, j, k: (jnp.where(j >= jh, k, 0), jnp.maximum(j - jh, 0)))]
    else:
        b_specs = [pl.BlockSpec((tn, tk), lambda i, j, k: (j, k)) if tb
                   else pl.BlockSpec((tk, tn), lambda i, j, k: (k, j))]
    o_spec = pl.BlockSpec((tm, tn), lambda i, j, k: (i, j))
    hbm = pl.BlockSpec(memory_space=pl.ANY)
    in_specs = a_specs + b_specs + [o_spec] * nres + [hbm] * nex
    args = (list(a_pair) if a_pair else [a]) + (list(b_pair) if b_pair else [b])
    args += ([residual] if nres else []) + (list(exchange[1]) if nex else [])
    out_shape = [jax.ShapeDtypeStruct((m, n), out_dtype)]
    scratch = [pltpu.VMEM((tm, tn), F32)]
    if nex:
        out_shape += _exchange_shapes(*exchange)
        scratch += _exchange_sems(exchange[0], nex)
    outs = pl.pallas_call(
        body, name=name, grid=grid,
        in_specs=in_specs, out_specs=[o_spec] + [hbm] * nex, out_shape=out_shape,
        scratch_shapes=scratch,
        compiler_params=_params(("arbitrary",) * 3 if nex else ("parallel", "parallel", "arbitrary")),
    )(*args)
    return (outs[0], list(outs[1:])) if nex else outs[0]


def rms_fwd(x, gains, *, name):
    s, d = x.shape
    tr = _pick(s, (512, 256, 128, 8))
    ng = len(gains)

    def body(*refs):
        x_ref = refs[0]
        g_refs = refs[1:1 + ng]
        o_refs = refs[1 + ng:]
        xv = x_ref[...]
        y = xv * lax.rsqrt(jnp.mean(xv * xv, axis=-1, keepdims=True) + EPS)
        for g_ref, o_ref in zip(g_refs, o_refs):
            o_ref[...] = (y * g_ref[...]).astype(BF16)

    row = pl.BlockSpec((tr, d), lambda i: (i, 0))
    vec = pl.BlockSpec((1, d), lambda i: (0, 0))
    return pl.pallas_call(
        body, name=name, grid=(s // tr,),
        in_specs=[row] + [vec] * ng, out_specs=[row] * ng,
        out_shape=[jax.ShapeDtypeStruct((s, d), BF16)] * ng,
        compiler_params=_params(("parallel",)),
    )(x, *gains)


def rms_bwd(x, dres, norms, *, name):
    s, d = x.shape
    tr = _pick(s, (256, 128, 8))
    ng = len(norms)

    def body(*refs):
        x_ref, dres_ref = refs[0], refs[1]
        g_refs = refs[2:2 + ng]
        dh_refs = refs[2 + ng:2 + 2 * ng]
        dx_ref, dxb_ref = refs[2 + 2 * ng], refs[3 + 2 * ng]
        dg_refs = refs[4 + 2 * ng:]
        i = pl.program_id(0)
        xv = x_ref[...]
        r = lax.rsqrt(jnp.mean(xv * xv, axis=-1, keepdims=True) + EPS)
        xhat = xv * r
        dx = dres_ref[...]
        for g_ref, dh_ref, dg_ref in zip(g_refs, dh_refs, dg_refs):
            dh = dh_ref[...]
            part = jnp.sum(dh * xhat, axis=0, keepdims=True)

            @pl.when(i == 0)
            def _():
                dg_ref[...] = part

            @pl.when(i > 0)
            def _():
                dg_ref[...] += part

            dxhat = dh * g_ref[...]
            dx = dx + r * (dxhat - xhat * jnp.mean(dxhat * xhat, axis=-1, keepdims=True))
        dx_ref[...] = dx
        dxb_ref[...] = dx.astype(BF16)

    row = pl.BlockSpec((tr, d), lambda i: (i, 0))
    vec = pl.BlockSpec((1, d), lambda i: (0, 0))
    outs = pl.pallas_call(
        body, name=name, grid=(s // tr,),
        in_specs=[row, row] + [vec] * ng + [row] * ng,
        out_specs=[row, row] + [vec] * ng,
        out_shape=[jax.ShapeDtypeStruct((s, d), F32), jax.ShapeDtypeStruct((s, d), BF16)]
        + [jax.ShapeDtypeStruct((1, d), F32)] * ng,
        compiler_params=_params(("arbitrary",)),
    )(x, dres, *[g for g, _ in norms], *[dh for _, dh in norms])
    return outs[0], outs[1], list(outs[2:])


def loss_bwd(x2, target, gain, *, name):
    s, d = x2.shape
    tr = _pick(s, (256, 128, 8))
    nsteps = s // tr

    def body(x_ref, t_ref, g_ref, loss_ref, dg_ref, dx_ref, dxb_ref, sq_acc):
        i = pl.program_id(0)
        xv = x_ref[...]
        r = lax.rsqrt(jnp.mean(xv * xv, axis=-1, keepdims=True) + EPS)
        xhat = xv * r
        g = g_ref[...]
        err = xhat * g - t_ref[...]
        dy = err * (1.0 / d)
        sq = jnp.sum(err * err, axis=0, keepdims=True)
        dgp = jnp.sum(dy * xhat, axis=0, keepdims=True)

        @pl.when(i == 0)
        def _():
            sq_acc[...] = sq
            dg_ref[...] = dgp

        @pl.when(i > 0)
        def _():
            sq_acc[...] += sq
            dg_ref[...] += dgp

        dxhat = dy * g
        dx = r * (dxhat - xhat * jnp.mean(dxhat * xhat, axis=-1, keepdims=True))
        dx_ref[...] = dx
        dxb_ref[...] = dx.astype(BF16)

        @pl.when(i == nsteps - 1)
        def _():
            tot = jnp.sum(sq_acc[...], axis=-1, keepdims=True) * (0.5 / d)
            loss_ref[...] = jnp.broadcast_to(tot, (1, LANES))

    row = pl.BlockSpec((tr, d), lambda i: (i, 0))
    vec = pl.BlockSpec((1, d), lambda i: (0, 0))
    return pl.pallas_call(
        body, name=name, grid=(nsteps,),
        in_specs=[row, row, vec],
        out_specs=[pl.BlockSpec((1, LANES), lambda i: (0, 0)), vec, row, row],
        out_shape=[jax.ShapeDtypeStruct((1, LANES), F32), jax.ShapeDtypeStruct((1, d), F32),
                   jax.ShapeDtypeStruct((s, d), F32), jax.ShapeDtypeStruct((s, d), BF16)],
        scratch_shapes=[pltpu.VMEM((1, d), F32)],
        compiler_params=_params(("arbitrary",)),
    )(x2, target, gain)


def _lru_gates(xb, wr, wi, br, bi, sp):
    xbb = xb.astype(BF16)
    r = _sigmoid(jnp.dot(xbb, wr, preferred_element_type=F32) + br)
    ig = _sigmoid(jnp.dot(xbb, wi, preferred_element_type=F32) + bi)
    log_a = (-LRU_C) * r * sp
    a = jnp.exp(log_a)
    mult = jnp.sqrt(jnp.maximum(-jnp.tanh(log_a) * (a * a + 1.0), 0.0))
    return r, ig, a, mult


def _softplus_neg(lam):
    e = jnp.exp(-jnp.abs(lam))
    sp = jnp.maximum(-lam, 0.0) + jnp.log(1.0 + e)
    sg = jnp.where(lam >= 0, e, 1.0) / (1.0 + e)
    return sp, sg


def _conv(pad_ref, w, b, t):
    acc = b + w[CONV_W - 1:CONV_W, :] * pad_ref[pl.ds(SUBLANES, t), :]
    for dlt in range(1, CONV_W):
        acc = acc + w[CONV_W - 1 - dlt:CONV_W - dlt, :] * pad_ref[pl.ds(SUBLANES - dlt, t), :]
    return acc


def _lru_specs(t, bw, nb, time_of):
    blk = lambda c0: pl.BlockSpec((t, bw), lambda n, i, c0=c0: (time_of(i), c0 + n))
    vec = pl.BlockSpec((1, bw), lambda n, i: (0, n))
    wspec = pl.BlockSpec((None, bw, bw), lambda n, i: (n, 0, 0))
    cwspec = pl.BlockSpec((CONV_W, bw), lambda n, i: (0, n))
    return blk, vec, wspec, cwspec


def lru_fwd(proj, conv_w, conv_b, w_r, b_r, w_i, b_i, lam, *, name):
    s, r2 = proj.shape
    rr = r2 // 2
    nb, bw, _ = w_r.shape
    t = _pick(s, (512, 256, 128, 64, 32))
    ngroups = t // SUBLANES

    def body(xp_ref, gate_ref, cw_ref, cb_ref, wr_ref, br_ref, wi_ref, bi_ref, lam_ref,
             m_ref, h_ref, pad, hcarry, a_scr, u_scr):
        i = pl.program_id(1)

        @pl.when(i == 0)
        def _():
            pad[0:SUBLANES, :] = jnp.zeros((SUBLANES, bw), F32)
            hcarry[...] = jnp.zeros_like(hcarry)

        xpre = xp_ref[...]
        pad[pl.ds(SUBLANES, t), :] = xpre
        xb = _conv(pad, cw_ref[...], cb_ref[...], t)
        pad[0:SUBLANES, :] = xpre[t - SUBLANES:, :]
        sp, _ = _softplus_neg(lam_ref[...])
        _, ig, a, mult = _lru_gates(xb, wr_ref[...], wi_ref[...], br_ref[...], bi_ref[...], sp)
        a_scr[...] = a
        u_scr[...] = mult * (ig * xb)
        row = lax.broadcasted_iota(jnp.int32, (SUBLANES, bw), 0)

        def groups(gi, hprev):
            offs = [pl.multiple_of((gi * SCAN_UNROLL + u) * SUBLANES, SUBLANES) for u in range(SCAN_UNROLL)]
            scanned = []
            for off in offs:
                av = a_scr[pl.ds(off, SUBLANES), :]
                uv = u_scr[pl.ds(off, SUBLANES), :]
                for dlt in (1, 2, 4):
                    keep = row >= dlt
                    uv = jnp.where(keep, av * pltpu.roll(uv, dlt, 0) + uv, uv)
                    av = jnp.where(keep, av * pltpu.roll(av, dlt, 0), av)
                scanned.append((av, uv))
            for off, (av, uv) in zip(offs, scanned):
                hv = av * hprev + uv
                h_ref[pl.ds(off, SUBLANES), :] = hv
                hprev = hv[SUBLANES - 1:SUBLANES, :]
            return hprev

        hcarry[...] = lax.fori_loop(0, ngroups // SCAN_UNROLL, groups, hcarry[...])
        gate = gate_ref[...]
        m_ref[...] = (h_ref[...] * (gate * _sigmoid(gate))).astype(BF16)

    blk, vec, wspec, cwspec = _lru_specs(t, bw, nb, lambda i: i)
    return pl.pallas_call(
        body, name=name, grid=(nb, s // t),
        in_specs=[blk(0), blk(nb), cwspec, vec, wspec, vec, wspec, vec, vec],
        out_specs=[blk(0), blk(0)],
        out_shape=[jax.ShapeDtypeStruct((s, rr), BF16), jax.ShapeDtypeStruct((s, rr), F32)],
        scratch_shapes=[pltpu.VMEM((t + SUBLANES, bw), F32), pltpu.VMEM((1, bw), F32),
                        pltpu.VMEM((t, bw), F32), pltpu.VMEM((t, bw), F32)],
        compiler_params=_params(("parallel", "arbitrary")),
    )(proj, proj, conv_w, conv_b, w_r, b_r, w_i, b_i, lam)


def lru_bwd(proj, hst, dm, conv_w, conv_b, w_r, b_r, w_i, b_i, lam, *, name):
    s, r2 = proj.shape
    rr = r2 // 2
    nb, bw, _ = w_r.shape
    t = _pick(s, (512, 256, 128, 64, 32))
    nt = s // t
    ngroups = t // SUBLANES
    nt_dims = (((1,), (1,)), ((), ()))
    tn_dims = (((0,), (0,)), ((), ()))

    def body(xp_ref, xhalo_ref, gate_ref, h_ref, hhalo_ref, dm_ref, cw_ref, cb_ref, wr_ref, br_ref, wi_ref,
             bi_ref, lam_ref,
             dxp_ref, dgate_ref, dcw_ref, dcb_ref, dwr_ref, dbr_ref, dwi_ref, dbi_ref, dlam_ref,
             pad, hpad, dpad, ecarry, a_scr, b_scr, d_scr):
        step = pl.program_id(1)

        @pl.when(step == 0)
        def _():
            dpad[pl.ds(t, SUBLANES), :] = jnp.zeros((SUBLANES, bw), F32)
            ecarry[...] = jnp.zeros_like(ecarry)
            dcw_ref[...] = jnp.zeros_like(dcw_ref)
            dcb_ref[...] = jnp.zeros_like(dcb_ref)
            dwr_ref[...] = jnp.zeros_like(dwr_ref)
            dbr_ref[...] = jnp.zeros_like(dbr_ref)
            dwi_ref[...] = jnp.zeros_like(dwi_ref)
            dbi_ref[...] = jnp.zeros_like(dbi_ref)
            dlam_ref[...] = jnp.zeros_like(dlam_ref)

        past = jnp.where(step == nt - 1, 0.0, 1.0)
        pad[0:SUBLANES, :] = xhalo_ref[...] * past
        pad[pl.ds(SUBLANES, t), :] = xp_ref[...]
        hpad[0:SUBLANES, :] = hhalo_ref[...] * past
        hpad[pl.ds(SUBLANES, t), :] = h_ref[...]
        cw = cw_ref[...]
        xb = _conv(pad, cw, cb_ref[...], t)
        sp, sg = _softplus_neg(lam_ref[...])
        wr = wr_ref[...]
        wi = wi_ref[...]
        r, ig, a, mult = _lru_gates(xb, wr, wi, br_ref[...], bi_ref[...], sp)
        gate = gate_ref[...]
        sgate = _sigmoid(gate)
        dmv = dm_ref[...]
        dgate_ref[...] = (dmv * h_ref[...] * (sgate * (1.0 + gate * (1.0 - sgate)))).astype(BF16)
        dy = dmv * (gate * sgate)
        a_scr[...] = a
        b_scr[...] = a * dy
        row = lax.broadcasted_iota(jnp.int32, (SUBLANES, bw), 0)

        def groups(gi, enext):
            offs = [pl.multiple_of((ngroups - 1 - gi * SCAN_UNROLL - u) * SUBLANES, SUBLANES)
                    for u in range(SCAN_UNROLL)]
            scanned = []
            for off in offs:
                av = a_scr[pl.ds(off, SUBLANES), :]
                bv = b_scr[pl.ds(off, SUBLANES), :]
                for dlt in (1, 2, 4):
                    keep = row < SUBLANES - dlt
                    bv = jnp.where(keep, av * pltpu.roll(bv, SUBLANES - dlt, 0) + bv, bv)
                    av = jnp.where(keep, av * pltpu.roll(av, SUBLANES - dlt, 0), av)
                scanned.append((av, bv))
            for off, (av, bv) in zip(offs, scanned):
                ev = av * enext + bv
                d_scr[pl.ds(off, SUBLANES), :] = jnp.where(row == SUBLANES - 1, enext,
                                                           pltpu.roll(ev, SUBLANES - 1, 0))
                enext = ev[0:1, :]
            return enext

        ecarry[...] = lax.fori_loop(0, ngroups // SCAN_UNROLL, groups, ecarry[...])
        dtot = dy + d_scr[...]
        da = dtot * hpad[pl.ds(SUBLANES - 1, t), :]
        dmult = dtot * (ig * xb)
        dlog_a = da * a - dmult * (a * a) / mult
        dr_pre = dlog_a * ((-LRU_C) * sp) * (r * (1.0 - r))
        di_pre = (dtot * mult * xb) * (ig * (1.0 - ig))
        dlam_ref[...] += jnp.sum(dlog_a * r, axis=0, keepdims=True) * (LRU_C * sg)
        dbr_ref[...] += jnp.sum(dr_pre, axis=0, keepdims=True)
        dbi_ref[...] += jnp.sum(di_pre, axis=0, keepdims=True)
        drb = dr_pre.astype(BF16)
        dib = di_pre.astype(BF16)
        xbb = xb.astype(BF16)
        dxb = (dtot * mult * ig
               + lax.dot_general(drb, wr, nt_dims, preferred_element_type=F32)
               + lax.dot_general(dib, wi, nt_dims, preferred_element_type=F32))
        dwr_ref[...] += lax.dot_general(xbb, drb, tn_dims, preferred_element_type=F32)
        dwi_ref[...] += lax.dot_general(xbb, dib, tn_dims, preferred_element_type=F32)
        dcb_ref[...] += jnp.sum(dxb, axis=0, keepdims=True)
        dpad[pl.ds(0, t), :] = dxb
        dxpre = cw[CONV_W - 1:CONV_W, :] * dxb
        dcw_ref[CONV_W - 1:CONV_W, :] += jnp.sum(dxb * pad[pl.ds(SUBLANES, t), :], axis=0, keepdims=True)
        for dlt in range(1, CONV_W):
            dxpre = dxpre + cw[CONV_W - 1 - dlt:CONV_W - dlt, :] * dpad[pl.ds(dlt, t), :]
            dcw_ref[CONV_W - 1 - dlt:CONV_W - dlt, :] += jnp.sum(
                dxb * pad[pl.ds(SUBLANES - dlt, t), :], axis=0, keepdims=True)
        dpad[pl.ds(t, SUBLANES), :] = dxb[0:SUBLANES, :]
        dxp_ref[...] = dxpre.astype(BF16)

    rev = lambda i: nt - 1 - i
    blk, vec, wspec, cwspec = _lru_specs(t, bw, nb, rev)
    halo = pl.BlockSpec((SUBLANES, bw), lambda n, i: (jnp.maximum(rev(i) * ngroups - 1, 0), n))
    return pl.pallas_call(
        body, name=name, grid=(nb, nt),
        in_specs=[blk(0), halo, blk(nb), blk(0), halo, blk(0), cwspec, vec, wspec, vec, wspec, vec, vec],
        out_specs=[blk(0), blk(0), cwspec, vec, wspec, vec, wspec, vec, vec],
        out_shape=[jax.ShapeDtypeStruct((s, rr), BF16), jax.ShapeDtypeStruct((s, rr), BF16),
                   jax.ShapeDtypeStruct((CONV_W, rr), F32), jax.ShapeDtypeStruct((1, rr), F32),
                   jax.ShapeDtypeStruct((nb, bw, bw), F32), jax.ShapeDtypeStruct((1, rr), F32),
                   jax.ShapeDtypeStruct((nb, bw, bw), F32), jax.ShapeDtypeStruct((1, rr), F32),
                   jax.ShapeDtypeStruct((1, rr), F32)],
        scratch_shapes=[pltpu.VMEM((t + SUBLANES, bw), F32), pltpu.VMEM((t + SUBLANES, bw), F32),
                        pltpu.VMEM((t + SUBLANES, bw), F32), pltpu.VMEM((1, bw), F32),
                        pltpu.VMEM((t, bw), F32), pltpu.VMEM((t, bw), F32), pltpu.VMEM((t, bw), F32)],
        compiler_params=_params(("parallel", "arbitrary")),
    )(proj, proj, proj, hst, hst, dm, conv_w, conv_b, w_r, b_r, w_i, b_i, lam)


def _softplus(z):
    return jnp.maximum(z, 0.0) + jnp.log(1.0 + jnp.exp2(jnp.abs(z) * (-LOG2E)))


def _att_blocks(s):
    bk = ATT_BLOCK if s % ATT_BLOCK == 0 else s
    bq = ATT_QTILES * bk if s % (ATT_QTILES * bk) == 0 else bk
    return bk, bq


def _tile_base(i, r):
    return r * ((i * (i + 1)) // 2)


def attn_fwd(projb, kv, *, name):
    s, a2 = projb.shape
    a = a2 // 2
    nh = a // HEAD_DIM
    bk, bq = _att_blocks(s)
    r = bq // bk
    nq = s // bq
    ntiles = _tile_base(nq, r)
    scale = 1.0 / math.sqrt(HEAD_DIM)
    nt_dims = (((1,), (1,)), ((), ()))
    hp = ATT_HEADS if nh % ATT_HEADS == 0 else 1
    wd = hp * HEAD_DIM

    def body(q_ref, g_ref, k_ref, v_ref, m_ref, o_ref, saved_hbm, acc, stage, sems):
        hgrp, i = pl.program_id(0), pl.program_id(1)
        base = _tile_base(i, r)
        qb = (q_ref[...] * scale).astype(BF16)
        from_mat = (lax.broadcasted_iota(jnp.int32, (bk, bk), 0)
                    >= lax.broadcasted_iota(jnp.int32, (bk, bk), 1)).astype(BF16)
        rowi = lax.broadcasted_iota(jnp.int32, (bq, bk), 0)
        coli = lax.broadcasted_iota(jnp.int32, (bq, bk), 1)
        cols = [slice(hh * HEAD_DIM, (hh + 1) * HEAD_DIM) for hh in range(hp)]

        def save(slot, j):
            return pltpu.make_async_copy(stage.at[slot], saved_hbm.at[hgrp, base + j], sems.at[slot])

        def tile(j, n, carries, diag):
            causal = None if diag is None else (coli + diag * bk) < rowi
            slot = n % 2

            def free_slot():
                save(slot, 0).wait()

            if isinstance(n, int):
                if n >= 2:
                    free_slot()
            elif r >= 2:
                free_slot()
            else:
                pl.when(n >= 2)(free_slot)
            rows = pl.ds(pl.multiple_of(j * bk, bk), bk)
            zs = [lax.dot_general(qb[:, c], k_ref[rows, c], nt_dims, preferred_element_type=F32) for c in cols]
            sums, sigs = [], []
            for z in zs:
                sp = _softplus(z)
                sig = jnp.exp(z - sp)
                if causal is not None:
                    sp = jnp.where(causal, sp, 0.0)
                    sig = jnp.where(causal, sig, 0.0)
                sums.append(jnp.dot(sp.astype(BF16), from_mat, preferred_element_type=F32))
                sigs.append(sig.astype(BF16))
            out = []
            for hh in range(hp):
                w = jnp.exp(zs[hh] - sums[hh] - carries[hh])
                if causal is not None:
                    w = jnp.where(causal, w, 0.0)
                wb = w.astype(BF16)
                acc[:, cols[hh]] += jnp.dot(wb, v_ref[rows, cols[hh]], preferred_element_type=F32)
                stage[slot, 0, hh] = wb
                stage[slot, 1, hh] = sigs[hh]
                out.append(carries[hh] + sums[hh][:, 0:1])
            save(slot, j).start()
            return tuple(out)

        acc[...] = jnp.zeros_like(acc)
        carries = tuple(jnp.zeros((bq, 1), F32) for _ in range(hp))
        for n, dg in enumerate(reversed(range(r))):
            carries = tile(r * i + dg, n, carries, dg)
        lax.fori_loop(0, r * i, lambda jj, c: tile(r * i - 1 - jj, r + jj, c, None), carries)
        ntile = r * (i + 1)
        for back in (1, 2):
            def drain(back=back):
                save((ntile - back) % 2, 0).wait()
            if r >= back:
                drain()
            else:
                pl.when(ntile >= back)(drain)
        o = acc[...]
        o_ref[...] = o
        gate = g_ref[...]
        m_ref[...] = (o * (gate * _sigmoid(gate))).astype(BF16)

    ng = nh // hp
    qspec = lambda c0: pl.BlockSpec((bq, wd), lambda h, i, c0=c0: (i, c0 + h))
    kspec = lambda c0: pl.BlockSpec((s, wd), lambda h, i, c0=c0: (0, c0 + h), pipeline_mode=pl.Buffered(1))
    hbm = pl.BlockSpec(memory_space=pl.ANY)
    saved = jax.ShapeDtypeStruct((ng, ntiles, 2, hp, bq, bk), BF16)
    return pl.pallas_call(
        body, name=name, grid=(ng, nq),
        in_specs=[qspec(0), qspec(ng), kspec(0), kspec(ng)],
        out_specs=[qspec(0), qspec(0), hbm],
        out_shape=[jax.ShapeDtypeStruct((s, a), BF16), jax.ShapeDtypeStruct((s, a), F32), saved],
        scratch_shapes=[pltpu.VMEM((bq, wd), F32), pltpu.VMEM((2, 2, hp, bq, bk), BF16),
                        pltpu.SemaphoreType.DMA((2,))],
        compiler_params=_params(("arbitrary", "arbitrary")),
    )(projb, projb, kv, kv)


def attn_bwd(projb, dm, o, kv, saved, *, name):
    s, a2 = projb.shape
    a = a2 // 2
    nh = a // HEAD_DIM
    bk, bq = _att_blocks(s)
    r = bq // bk
    nq = s // bq
    scale = 1.0 / math.sqrt(HEAD_DIM)
    nt_dims = (((1,), (1,)), ((), ()))
    tn_dims = (((0,), (0,)), ((), ()))
    hp = saved.shape[3]
    wd = hp * HEAD_DIM
    ahead = ATT_FETCH_AHEAD

    def body(q_ref, g_ref, dm_ref, o_ref, k_ref, v_ref, saved_hbm, dq_ref, dg_ref, dk_ref, dv_ref,
             dk_acc, dv_acc, dq_acc, stage, sems):
        hgrp, i = pl.program_id(0), pl.program_id(1)
        base = _tile_base(i, r)
        ntile = r * (i + 1)

        @pl.when(i == 0)
        def _():
            dk_acc[...] = jnp.zeros_like(dk_acc)
            dv_acc[...] = jnp.zeros_like(dv_acc)

        def fetch(j):
            slot = j % (ahead + 1)
            return pltpu.make_async_copy(saved_hbm.at[hgrp, base + j], stage.at[slot], sems.at[slot])

        for j0 in range(ahead):
            pl.when(j0 < ntile)(lambda j0=j0: fetch(j0).start())
        qb = (q_ref[...] * scale).astype(BF16)
        gate = g_ref[...]
        sgate = _sigmoid(gate)
        dmv = dm_ref[...]
        dob = (dmv * (gate * sgate)).astype(BF16)
        dg_ref[...] = (dmv * o_ref[...] * (sgate * (1.0 + gate * (1.0 - sgate)))).astype(BF16)
        upto_mat = (lax.broadcasted_iota(jnp.int32, (bk, bk), 0)
                    <= lax.broadcasted_iota(jnp.int32, (bk, bk), 1)).astype(BF16)
        cols = [slice(hh * HEAD_DIM, (hh + 1) * HEAD_DIM) for hh in range(hp)]
        dq_acc[...] = jnp.zeros_like(dq_acc)

        def tile(j, gcarries):
            slot = j % (ahead + 1)
            pl.when(j + ahead < ntile)(lambda: fetch(j + ahead).start())
            fetch(j).wait()
            rows = pl.ds(pl.multiple_of(j * bk, bk), bk)
            dws = [lax.dot_general(dob[:, c], v_ref[rows, c], nt_dims, preferred_element_type=F32) for c in cols]
            gs, totals = [], []
            for hh in range(hp):
                wb = stage[slot, 0, hh]
                g = wb.astype(F32) * dws[hh]
                dv_acc[rows, cols[hh]] += lax.dot_general(wb, dob[:, cols[hh]], tn_dims, preferred_element_type=F32)
                totals.append(jnp.dot(g.astype(BF16), upto_mat, preferred_element_type=F32))
                gs.append(g)
            out = []
            for hh in range(hp):
                dz = gs[hh] - (totals[hh] + gcarries[hh]) * stage[slot, 1, hh].astype(F32)
                dzb = dz.astype(BF16)
                dq_acc[:, cols[hh]] += jnp.dot(dzb, k_ref[rows, cols[hh]], preferred_element_type=F32)
                dk_acc[rows, cols[hh]] += lax.dot_general(dzb, qb[:, cols[hh]], tn_dims, preferred_element_type=F32)
                out.append(gcarries[hh] + totals[hh][:, bk - 1:bk])
            return tuple(out)

        lax.fori_loop(0, ntile, tile, tuple(jnp.zeros((bq, 1), F32) for _ in range(hp)))
        dq_ref[...] = (dq_acc[...] * scale).astype(BF16)

        @pl.when(i == nq - 1)
        def _():
            dk_ref[...] = dk_acc[...].astype(BF16)
            dv_ref[...] = dv_acc[...].astype(BF16)

    ng = nh // hp
    once = pl.Buffered(1)
    qspec = lambda c0: pl.BlockSpec((bq, wd), lambda h, i, c0=c0: (i, c0 + h))
    kspec = lambda c0: pl.BlockSpec((s, wd), lambda h, i, c0=c0: (0, c0 + h), pipeline_mode=once)
    hbm = pl.BlockSpec(memory_space=pl.ANY)
    return pl.pallas_call(
        body, name=name, grid=(ng, nq),
        in_specs=[qspec(0), qspec(ng), qspec(0), qspec(0), kspec(0), kspec(ng), hbm],
        out_specs=[qspec(0), qspec(0), kspec(0), kspec(0)],
        out_shape=[jax.ShapeDtypeStruct((s, a), BF16)] * 4,
        scratch_shapes=[pltpu.VMEM((s, wd), F32), pltpu.VMEM((s, wd), F32), pltpu.VMEM((bq, wd), F32),
                        pltpu.VMEM((ahead + 1, 2, hp, bq, bk), BF16), pltpu.SemaphoreType.DMA((ahead + 1,))],
        compiler_params=_params(("arbitrary", "arbitrary")),
    )(projb, projb, dm, o, kv, kv, saved)


def _as2d(x):
    n = x.size
    cols = x.shape[-1]
    if cols % LANES != 0:
        cols = LANES
    return x.reshape(n // cols, cols)


def sum_parts(parts, *, name):
    p, rows, cols = parts.shape
    tr = _pick(rows, (512, 256, 128, 64, 32, 16))

    def body(p_ref, o_ref):
        acc = p_ref[0].astype(F32)
        for k in range(1, p):
            acc = acc + p_ref[k].astype(F32)
        o_ref[...] = acc

    return pl.pallas_call(
        body, name=name, grid=(rows // tr,),
        in_specs=[pl.BlockSpec((p, tr, cols), lambda i: (0, i, 0))],
        out_specs=pl.BlockSpec((tr, cols), lambda i: (i, 0)),
        out_shape=jax.ShapeDtypeStruct((rows, cols), F32),
        compiler_params=_params(("parallel",)),
    )(parts)


def adamw(w, g_parts, m, v, *, name):
    rows, cols = w.shape
    tr = _pick(rows, (128, 64, 32, 16, 8))
    np_ = len(g_parts)
    c1 = 1.0 / (1.0 - ADAM_B1 ** ADAM_STEP)
    c2 = 1.0 / (1.0 - ADAM_B2 ** ADAM_STEP)

    def body(*refs):
        w_ref, m_ref, v_ref = refs[0], refs[1], refs[2]
        g_refs = refs[3:3 + np_]
        go_ref, d_ref, mo_ref, vo_ref = refs[3 + np_:]
        g = g_refs[0][...]
        for gr in g_refs[1:]:
            g = g + gr[...]
        mn = ADAM_B1 * m_ref[...] + (1.0 - ADAM_B1) * g
        vn = ADAM_B2 * v_ref[...] + (1.0 - ADAM_B2) * (g * g)
        go_ref[...] = g
        mo_ref[...] = mn
        vo_ref[...] = vn
        d_ref[...] = (-ADAM_LR) * ((mn * c1) / (jnp.sqrt(vn * c2) + ADAM_EPS) + ADAM_WD * w_ref[...])

    spec = pl.BlockSpec((tr, cols), lambda i: (i, 0))
    return pl.pallas_call(
        body, name=name, grid=(rows // tr,),
        in_specs=[spec] * (3 + np_), out_specs=[spec] * 4,
        out_shape=[jax.ShapeDtypeStruct((rows, cols), F32)] * 4,
        compiler_params=_params(("parallel",)),
    )(w, m, v, *g_parts)


def exchange(kind, arrays, axes, *, name):
    na = len(arrays)
    hbm = pl.BlockSpec(memory_space=pl.ANY)

    def body(*refs):
        triples = _exchange(kind, refs[:na], refs[na:2 * na], axes, *refs[2 * na:])
        for step in range(3):
            for triple in triples:
                triple[step]()

    return pl.pallas_call(
        body, name=name, in_specs=[hbm] * na, out_specs=[hbm] * na,
        out_shape=_exchange_shapes(kind, arrays, axes), scratch_shapes=_exchange_sems(kind, na),
    )(*arrays)


def swap_cores(arrs, *, name):
    na = len(arrs)
    hbm = pl.BlockSpec(memory_space=pl.ANY)

    def body(*refs):
        a_refs = refs[:na]
        o_refs = refs[na:2 * na]
        send_sems, recv_sems = refs[2 * na:]
        x, y, c = _place()
        copies = []
        for ai in range(na):
            cp = pltpu.make_async_remote_copy(
                src_ref=a_refs[ai], dst_ref=o_refs[ai], send_sem=send_sems.at[ai], recv_sem=recv_sems.at[ai],
                device_id=(x, y, 1 - c), device_id_type=MESH)
            cp.start()
            copies.append(cp)
        for cp in copies:
            cp.wait()

    return pl.pallas_call(
        body, name=name,
        in_specs=[hbm] * na, out_specs=[hbm] * na,
        out_shape=[jax.ShapeDtypeStruct(a.shape, a.dtype) for a in arrs],
        scratch_shapes=[pltpu.SemaphoreType.DMA((na,)), pltpu.SemaphoreType.DMA((na,))],
    )(*arrs)


def allreduce_small(buf, *, name):
    rows, cols = buf.shape

    def body(b_ref, o_ref, slots, send_sems, recv_sems):
        x, y, c = _place()
        me = 4 * x + 2 * y + c
        slots[0] = b_ref[...]
        copies = []
        for rel in range(1, 8):
            peer = (x ^ (rel >> 2), y ^ ((rel >> 1) & 1), c ^ (rel & 1))
            cp = pltpu.make_async_remote_copy(
                src_ref=b_ref, dst_ref=slots.at[rel], send_sem=send_sems.at[rel - 1],
                recv_sem=recv_sems.at[rel - 1], device_id=peer, device_id_type=MESH)
            cp.start()
            copies.append(cp)
        for cp in copies:
            cp.wait()
        acc = slots[me]
        for dev in range(1, 8):
            acc = acc + slots[dev ^ me]
        o_ref[...] = acc

    vm = pl.BlockSpec(memory_space=pltpu.VMEM)
    return pl.pallas_call(
        body, name=name, in_specs=[vm], out_specs=vm,
        out_shape=jax.ShapeDtypeStruct((rows, cols), F32),
        scratch_shapes=[pltpu.VMEM((8, rows, cols), F32), pltpu.SemaphoreType.DMA((7,)),
                        pltpu.SemaphoreType.DMA((7,))],
    )(buf)


def _pack_rows(arrs):
    parts = []
    for a in arrs:
        p = a.reshape(-1, LANES)
        parts.append(jnp.pad(p, ((0, (-p.shape[0]) % SUBLANES), (0, 0))))
    return jnp.concatenate(parts, axis=0)


def _unpack_rows(buf, shapes):
    out, r0 = [], 0
    for shp in shapes:
        n = math.prod(shp) // LANES
        out.append(buf[r0:r0 + n].reshape(shp))
        r0 += n + (-n) % SUBLANES
    return out


def kernel(x, a_norm, a_w_in, a_conv_w, a_conv_b, a_w_r, a_b_r, a_w_i, a_b_i, a_lambda, a_w_out, kv_norm, w_kv, b_norm, b_w_in, b_w_out, final_norm, loss_target, m_a_norm, m_a_w_in, m_a_conv_w, m_a_conv_b, m_a_w_r, m_a_b_r, m_a_w_i, m_a_b_i, m_a_lambda, m_a_w_out, m_kv_norm, m_w_kv, m_b_norm, m_b_w_in, m_b_w_out, m_final_norm, v_a_norm, v_a_w_in, v_a_conv_w, v_a_conv_b, v_a_w_r, v_a_b_r, v_a_w_i, v_a_b_i, v_a_lambda, v_a_w_out, v_kv_norm, v_w_kv, v_b_norm, v_b_w_in, v_b_w_out, v_final_norm):
    weights = dict(a_norm=a_norm, a_w_in=a_w_in, a_conv_w=a_conv_w, a_conv_b=a_conv_b, a_w_r=a_w_r, a_b_r=a_b_r,
                   a_w_i=a_w_i, a_b_i=a_b_i, a_lambda=a_lambda, a_w_out=a_w_out, kv_norm=kv_norm, w_kv=w_kv,
                   b_norm=b_norm, b_w_in=b_w_in, b_w_out=b_w_out, final_norm=final_norm)
    mom1 = dict(a_norm=m_a_norm, a_w_in=m_a_w_in, a_conv_w=m_a_conv_w, a_conv_b=m_a_conv_b, a_w_r=m_a_w_r,
                a_b_r=m_a_b_r, a_w_i=m_a_w_i, a_b_i=m_a_b_i, a_lambda=m_a_lambda, a_w_out=m_a_w_out,
                kv_norm=m_kv_norm, w_kv=m_w_kv, b_norm=m_b_norm, b_w_in=m_b_w_in, b_w_out=m_b_w_out,
                final_norm=m_final_norm)
    mom2 = dict(a_norm=v_a_norm, a_w_in=v_a_w_in, a_conv_w=v_a_conv_w, a_conv_b=v_a_conv_b, a_w_r=v_a_w_r,
                a_b_r=v_a_b_r, a_w_i=v_a_w_i, a_b_i=v_a_b_i, a_lambda=v_a_lambda, a_w_out=v_a_w_out,
                kv_norm=v_kv_norm, w_kv=v_w_kv, b_norm=v_b_norm, b_w_in=v_b_w_in, b_w_out=v_b_w_out,
                final_norm=v_final_norm)
    order = list(weights)
    x0 = x[0]
    target = loss_target[0]
    d = x0.shape[1]
    chip = 2 * lax.axis_index("x") + lax.axis_index("y")

    big = ["a_w_in", "a_w_r", "a_w_i", "a_w_out", "w_kv", "b_w_in", "b_w_out"]
    big_axis = dict(a_w_in=1, a_w_r=1, a_w_i=1, a_w_out=0, w_kv=1, b_w_in=1, b_w_out=0)
    local = dict(a_w_in=a_w_in[0], a_w_r=a_w_r[0], a_w_i=a_w_i[0], a_w_out=a_w_out[0], w_kv=w_kv,
                 b_w_in=b_w_in[0], b_w_out=b_w_out[0])
    shards = {n: local[n].astype(BF16) for n in big}
    first = ["a_w_in", "a_w_r", "a_w_i"]
    full = exchange("gather_halves", [shards[n] for n in first], [big_axis[n] for n in first], name="gather_first")
    full += exchange("gather", [a_conv_w[0], b_norm], [1, 1], name="gather_small")
    wf = dict(zip(first + ["a_conv_w", "b_norm"], full))
    wf.update(a_norm=a_norm, a_conv_b=a_conv_b, a_b_r=a_b_r, a_b_i=a_b_i, a_lambda=a_lambda,
              kv_norm=kv_norm.reshape(1, d), final_norm=final_norm.reshape(1, d))
    loss_part, grad_x, parts, gsmall = _local_grads(x0, target, wf, shards=shards, axes=big_axis)

    sums = [sum_parts(parts[n].reshape(4, *_as2d(parts[n][0]).shape), name="sum_" + n) for n in big]
    others = swap_cores(sums, name="swap_cores")

    small = ["a_norm", "a_conv_b", "a_b_r", "a_b_i", "a_lambda", "kv_norm", "final_norm", "a_conv_w", "b_norm"]
    buf = _pack_rows([gsmall[n] for n in small] + [loss_part])
    red = allreduce_small(buf, name="allreduce_small")
    red_list = _unpack_rows(red, [gsmall[n].shape for n in small] + [(1, LANES)])
    gs = dict(zip(small, red_list[:-1]))
    loss = red_list[-1][0, 0]
    n_conv = a_conv_w.shape[2]
    gs["a_conv_w"] = lax.dynamic_slice_in_dim(gs["a_conv_w"], chip * n_conv, n_conv, axis=1)
    n_bn = b_norm.shape[1]
    gs["b_norm"] = lax.dynamic_slice_in_dim(gs["b_norm"], chip * n_bn, n_bn, axis=1)

    grads, deltas, new_m, new_v = {}, {}, {}, {}
    for n, s_mine, s_other in zip(big, sums, others):
        shp = weights[n].shape
        g, dlt, mn, vn = adamw(_as2d(weights[n]), [s_mine, s_other], _as2d(mom1[n]), _as2d(mom2[n]),
                               name="adamw_" + n)
        grads[n], deltas[n], new_m[n], new_v[n] = (t.reshape(shp) for t in (g, dlt, mn, vn))
    shapes = [weights[n].shape for n in small]
    wpk, gpk, mpk, vpk = (_pack_rows([src[n] for n in small]) for src in (weights, gs, mom1, mom2))
    outs = adamw(wpk, [gpk], mpk, vpk, name="adamw_small")
    for dst, packed in zip((grads, deltas, new_m, new_v), outs):
        for n, val in zip(small, _unpack_rows(packed, shapes)):
            dst[n] = val

    return (loss, grad_x[None], *[grads[n] for n in order], *[deltas[n] for n in order],
            *[new_m[n] for n in order], *[new_v[n] for n in order])


def _local_grads(x0, target, wf, shards=None, axes=None):
    a_norm, a_conv_b, a_b_r, a_b_i, a_lambda = (wf[n] for n in ("a_norm", "a_conv_b", "a_b_r", "a_b_i", "a_lambda"))
    kv_norm, final_norm = wf["kv_norm"], wf["final_norm"]
    wf = dict(wf)
    parts = {}

    def mm(*args, gather=(), scatter=None, **kw):
        if shards is None or not (gather or scatter):
            return matmul(*args, **kw)
        if gather:
            out, got = matmul(*args, exchange=("gather_halves", [shards[n] for n in gather], [axes[n] for n in gather]),
                              **kw)
            wf.update(zip(gather, got))
        else:
            out, got = matmul(*args, exchange=("scatter", list(scatter.values()), [axes[n] for n in scatter]), **kw)
            parts.update(zip(scatter, got))
        return out

    (h_a,) = rms_fwd(x0, [a_norm], name="norm_a")
    proj_a = mm(h_a, wf["a_w_in"], gather=("a_w_out",), name="a_in")
    m_a, hst = lru_fwd(proj_a, wf["a_conv_w"], a_conv_b, wf["a_w_r"], a_b_r, wf["a_w_i"], a_b_i, a_lambda,
                       name="lru_fwd")
    x1 = mm(m_a, wf["a_w_out"], residual=x0, gather=("w_kv",), name="a_out")
    kvn, hb = rms_fwd(x1, [kv_norm, wf["b_norm"]], name="norm_kv_b")
    kv = mm(kvn, wf["w_kv"], out_dtype=BF16, gather=("b_w_in",), name="kv_proj")
    proj_b = mm(hb, wf["b_w_in"], gather=("b_w_out",), name="b_in")
    m_b, o, saved = attn_fwd(proj_b, kv, name="attn_fwd")
    x2 = mm(m_b, wf["b_w_out"], residual=x1, name="b_out")
    loss_part, g_final, dx2, dx2b = loss_bwd(x2, target, final_norm, name="loss_bwd")

    dm_b = mm(dx2b, wf["b_w_out"], tb=True, name="b_out_dx")
    g_b_w_out = mm(m_b, dx2b, ta=True, out_dtype=BF16, name="b_out_dw")
    dq, dgate_b, dk, dv = attn_bwd(proj_b, dm_b, o, kv, saved, name="attn_bwd")
    dproj_b = (dq, dgate_b)
    dkv = (dk, dv)
    g_b_w_in = mm(hb, dproj_b, ta=True, out_dtype=BF16, scatter=dict(b_w_out=g_b_w_out), name="b_in_dw")
    g_w_kv = mm(kvn, dkv, ta=True, out_dtype=BF16, scatter=dict(b_w_in=g_b_w_in), name="kv_dw")
    dhb = mm(dproj_b, wf["b_w_in"], tb=True, scatter=dict(w_kv=g_w_kv), name="b_in_dx")
    dkvn = mm(dkv, wf["w_kv"], tb=True, name="kv_dx")
    dx1, dx1b, (g_kv_norm, g_b_norm) = rms_bwd(
        x1, dx2, [(kv_norm, dkvn), (wf["b_norm"], dhb)], name="norm_kv_b_bwd")

    g_a_w_out = mm(m_a, dx1b, ta=True, out_dtype=BF16, name="a_out_dw")
    dm_a = mm(dx1b, wf["a_w_out"], tb=True, scatter=dict(a_w_out=g_a_w_out), name="a_out_dx")
    dxpre, dgate_a, g_conv_w, g_conv_b, g_w_r, g_b_r, g_w_i, g_b_i, g_lambda = lru_bwd(
        proj_a, hst, dm_a, wf["a_conv_w"], a_conv_b, wf["a_w_r"], a_b_r, wf["a_w_i"], a_b_i, a_lambda,
        name="lru_bwd")
    dproj_a = (dxpre, dgate_a)
    g_w_r, g_w_i = g_w_r.astype(BF16), g_w_i.astype(BF16)
    g_a_w_in = mm(h_a, dproj_a, ta=True, out_dtype=BF16, scatter=dict(a_w_r=g_w_r, a_w_i=g_w_i), name="a_in_dw")
    dh_a = mm(dproj_a, wf["a_w_in"], tb=True, scatter=dict(a_w_in=g_a_w_in), name="a_in_dx")
    grad_x, _, (g_a_norm,) = rms_bwd(x0, dx1, [(a_norm, dh_a)], name="norm_a_bwd")

    gbig = parts if shards is not None else dict(
        a_w_in=g_a_w_in, a_w_r=g_w_r, a_w_i=g_w_i, a_w_out=g_a_w_out, w_kv=g_w_kv, b_w_in=g_b_w_in, b_w_out=g_b_w_out)
    gsmall = dict(a_norm=g_a_norm, a_conv_b=g_conv_b, a_b_r=g_b_r, a_b_i=g_b_i, a_lambda=g_lambda,
                  kv_norm=g_kv_norm, final_norm=g_final, a_conv_w=g_conv_w, b_norm=g_b_norm)
    return loss_part, grad_x, gbig, gsmall
```

```python
import math

import jax
import jax.numpy as jnp
from jax import lax
from jax.experimental import pallas as pl
from jax.experimental.pallas import tpu as pltpu

F32 = jnp.float32
BF16 = jnp.bfloat16
MESH = pl.DeviceIdType.MESH

EPS = 1e-6
LRU_C = 8.0
CONV_W = 4
HEAD_DIM = 128
ADAM_LR = 0.001
ADAM_B1 = 0.9
ADAM_B2 = 0.999
ADAM_EPS = 1e-08
ADAM_WD = 0.01
ADAM_STEP = 10

V7X_VMEM_LIMIT = 56 * 1024 * 1024
LANES = 128
SUBLANES = 8
ATT_BLOCK = 256
ATT_QTILES = 4
ATT_HEADS = 2
ATT_FETCH_AHEAD = 2
LOG2E = 1.4426950408889634
SCAN_UNROLL = 4


def _pick(dim, cands):
    for c in cands:
        if dim % c == 0:
            return c
    return dim


def _params(sem, vmem=V7X_VMEM_LIMIT):
    return pltpu.CompilerParams(dimension_semantics=sem, vmem_limit_bytes=vmem)


def _sigmoid(x):
    return 1.0 / (1.0 + jnp.exp(-x))


def _place():
    return lax.axis_index("x"), lax.axis_index("y"), lax.axis_index("c")


def _chip_peers(x, y, c):
    return [(1 - x, y, c), (x, 1 - y, c), (1 - x, 1 - y, c)]


def _shard_of(ref, axis, idx, n):
    start = pl.multiple_of(idx * n, n)
    sl = [slice(None)] * len(ref.shape)
    sl[axis] = pl.ds(start, n)
    return ref.at[tuple(sl)]


def _half_rows(ref, h):
    n = ref.shape[0] // 2
    return ref.at[pl.ds(pl.multiple_of(h * n, n), n)]


def _block_half(ref, axis, idx, n, h):
    if axis == 0:
        return ref.at[pl.ds(pl.multiple_of(idx * n + h * (n // 2), n // 2), n // 2)]
    return _half_rows(_shard_of(ref, axis, idx, n), h)


def _exchange(kind, in_refs, out_refs, axes, send_sems, recv_sems, local_sems, send2_sems=None, recv2_sems=None):
    x, y, c = _place()
    me = 2 * x + y
    peers = _chip_peers(x, y, c)
    sibling = (x, y, 1 - c)
    triples = []
    nothing = lambda: None
    for ai, (src, dst, ax) in enumerate(zip(in_refs, out_refs, axes)):
        if kind == "scatter":
            n = dst.shape[1 + ax]
            loc = pltpu.make_async_copy(_shard_of(src, ax, me, n), dst.at[0], local_sems.at[ai])
        else:
            n = src.shape[ax]
            mine = _shard_of(dst, ax, me, n)
            loc = pltpu.make_async_copy(src, mine, local_sems.at[ai])
        triples.append((loc.start, nothing, loc.wait))
        for k, peer in enumerate(peers):
            sem = dict(send_sem=send_sems.at[ai * 3 + k], recv_sem=recv_sems.at[ai * 3 + k],
                       device_id=peer, device_id_type=MESH)
            theirs = 2 * peer[0] + peer[1]
            if kind == "gather":
                snd = pltpu.make_async_remote_copy(src_ref=src, dst_ref=mine, **sem)
                rcv = pltpu.make_async_remote_copy(src_ref=src, dst_ref=_shard_of(dst, ax, theirs, n), **sem)
                triples.append((snd.start, nothing, lambda snd=snd, rcv=rcv: (snd.wait_send(), rcv.wait_recv())))
            elif kind == "scatter":
                snd = pltpu.make_async_remote_copy(src_ref=_shard_of(src, ax, theirs, n), dst_ref=dst.at[1 + k], **sem)
                triples.append((snd.start, nothing, snd.wait))
            else:
                landed = _block_half(dst, ax, theirs, n, c)
                snd = pltpu.make_async_remote_copy(src_ref=_half_rows(src, c), dst_ref=_block_half(dst, ax, me, n, c), **sem)
                rcv = pltpu.make_async_remote_copy(src_ref=_half_rows(src, c), dst_ref=landed, **sem)
                sem2 = dict(send_sem=send2_sems.at[ai * 3 + k], recv_sem=recv2_sems.at[ai * 3 + k],
                            device_id=sibling, device_id_type=MESH)
                fwd = pltpu.make_async_remote_copy(src_ref=landed, dst_ref=landed, **sem2)
                got = pltpu.make_async_remote_copy(src_ref=landed, dst_ref=_block_half(dst, ax, theirs, n, 1 - c), **sem2)
                triples.append((snd.start, lambda rcv=rcv, fwd=fwd: (rcv.wait_recv(), fwd.start()),
                                lambda snd=snd, fwd=fwd, got=got: (snd.wait_send(), fwd.wait_send(), got.wait_recv())))
    return triples


def _exchange_shapes(kind, arrays, axes):
    out = []
    for arr, ax in zip(arrays, axes):
        shp = list(arr.shape)
        if kind == "scatter":
            shp[ax] //= 4
            out.append(jax.ShapeDtypeStruct((4, *shp), arr.dtype))
        else:
            shp[ax] *= 4
            out.append(jax.ShapeDtypeStruct(tuple(shp), arr.dtype))
    return out


def _exchange_sems(kind, n):
    sems = [pltpu.SemaphoreType.DMA((3 * n,)), pltpu.SemaphoreType.DMA((3 * n,)), pltpu.SemaphoreType.DMA((n,))]
    if kind == "gather_halves":
        sems += [pltpu.SemaphoreType.DMA((3 * n,)), pltpu.SemaphoreType.DMA((3 * n,))]
    return sems


def matmul(a, b, *, ta=False, tb=False, out_dtype=F32, residual=None, exchange=None, name):
    a_pair = a if isinstance(a, (tuple, list)) else None
    b_pair = b if isinstance(b, (tuple, list)) else None
    assert not (a_pair and ta) and not (b_pair and tb) and not (a_pair and b_pair)
    a0 = a_pair[0] if a_pair else a
    b0 = b_pair[0] if b_pair else b
    m = a0.shape[1] if ta else a0.shape[0]
    kdim = (a0.shape[0] if ta else a0.shape[1]) * (2 if a_pair else 1)
    n = (b0.shape[0] if tb else b0.shape[1]) * (2 if b_pair else 1)
    assert (b0.shape[1] if tb else b0.shape[0]) == kdim
    tm = _pick(m, (1024, 640, 512, 256, 128))
    tn = _pick(n // 2 if b_pair else n, (1024, 1280, 640, 512, 256, 128))
    tk = _pick(kdim // 2 if a_pair else kdim, (2560, 2048, 1024, 512, 256, 128))
    grid = (m // tm, n // tn, kdim // tk)
    nk = grid[2]
    kh, jh = nk // 2, grid[1] // 2
    dn = (((0 if ta else 1,), (1 if tb else 0,)), ((), ()))
    na = 2 if a_pair else 1
    nb = 2 if b_pair else 1
    nres = 0 if residual is None else 1
    nex = 0 if exchange is None else len(exchange[1])

    def body(*refs):
        a_refs, b_refs = refs[:na], refs[na:na + nb]
        p = na + nb
        r_ref = refs[p] if nres else None
        ex_in = refs[p + nres:p + nres + nex]
        o_ref = refs[p + nres + nex]
        ex_out = refs[p + 1 + nres + nex:p + 1 + nres + 2 * nex]
        acc = refs[p + 1 + nres + 2 * nex]
        sems = refs[p + 2 + nres + 2 * nex:]
        i, j, k = pl.program_id(0), pl.program_id(1), pl.program_id(2)
        if nex:
            @pl.when((i == 0) & (j == 0) & (k == 0))
            def _():
                for start, _, _ in _exchange(exchange[0], ex_in, ex_out, exchange[2], *sems):
                    start()

        @pl.when(k == 0)
        def _():
            acc[...] = jnp.zeros_like(acc)

        def accumulate(a_ref, b_ref):
            acc[...] += lax.dot_general(a_ref[...].astype(BF16), b_ref[...].astype(BF16), dn,
                                        preferred_element_type=F32)

        if a_pair:
            pl.when(k < kh)(lambda: accumulate(a_refs[0], b_refs[0]))
            pl.when(k >= kh)(lambda: accumulate(a_refs[1], b_refs[0]))
        elif b_pair:
            pl.when(j < jh)(lambda: accumulate(a_refs[0], b_refs[0]))
            pl.when(j >= jh)(lambda: accumulate(a_refs[0], b_refs[1]))
        else:
            accumulate(a_refs[0], b_refs[0])

        @pl.when(k == nk - 1)
        def _():
            r = acc[...]
            if r_ref is not None:
                r = r + r_ref[...]
            o_ref[...] = r.astype(out_dtype)

        if nex:
            @pl.when((i == grid[0] - 1) & (j == grid[1] - 1) & (k == nk - 1))
            def _():
                triples = _exchange(exchange[0], ex_in, ex_out, exchange[2], *sems)
                for _, relay, _ in triples:
                    relay()
                for _, _, finish in triples:
                    finish()

    if a_pair:
        a_specs = [pl.BlockSpec((tm, tk), lambda i, j, k: (i, jnp.minimum(k, kh - 1))),
                   pl.BlockSpec((tm, tk), lambda i, j, k: (i, jnp.maximum(k - kh, 0)))]
    else:
        a_specs = [pl.BlockSpec((tk, tm), lambda i, j, k: (k, i)) if ta
                   else pl.BlockSpec((tm, tk), lambda i, j, k: (i, k))]
    if b_pair:
        b_specs = [pl.BlockSpec((tk, tn), lambda i, j, k: (jnp.where(j < jh, k, nk - 1), jnp.minimum(j, jh - 1))),
                   pl.BlockSpec((tk, tn), lambda i, j, k: (jnp.where(j >= jh, k, 0), jnp.maximum(j - jh, 0)))]
    else:
        b_specs = [pl.BlockSpec((tn, tk), lambda i, j, k: (j, k)) if tb
                   else pl.BlockSpec((tk, tn), lambda i, j, k: (k, j))]
    o_spec = pl.BlockSpec((tm, tn), lambda i, j, k: (i, j))
    hbm = pl.BlockSpec(memory_space=pl.ANY)
    in_specs = a_specs + b_specs + [o_spec] * nres + [hbm] * nex
    args = (list(a_pair) if a_pair else [a]) + (list(b_pair) if b_pair else [b])
    args += ([residual] if nres else []) + (list(exchange[1]) if nex else [])
    out_shape = [jax.ShapeDtypeStruct((m, n), out_dtype)]
    scratch = [pltpu.VMEM((tm, tn), F32)]
    if nex:
        out_shape += _exchange_shapes(*exchange)
        scratch += _exchange_sems(exchange[0], nex)
    outs = pl.pallas_call(
        body, name=name, grid=grid,
        in_specs=in_specs, out_specs=[o_spec] + [hbm] * nex, out_shape=out_shape,
        scratch_shapes=scratch,
        compiler_params=_params(("arbitrary",) * 3 if nex else ("parallel", "parallel", "arbitrary")),
    )(*args)
    return (outs[0], list(outs[1:])) if nex else outs[0]


def rms_fwd(x, gains, *, name):
    s, d = x.shape
    tr = _pick(s, (512, 256, 128, 8))
    ng = len(gains)

    def body(*refs):
        x_ref = refs[0]
        g_refs = refs[1:1 + ng]
        o_refs = refs[1 + ng:]
        xv = x_ref[...]
        y = xv * lax.rsqrt(jnp.mean(xv * xv, axis=-1, keepdims=True) + EPS)
        for g_ref, o_ref in zip(g_refs, o_refs):
            o_ref[...] = (y * g_ref[...]).astype(BF16)

    row = pl.BlockSpec((tr, d), lambda i: (i, 0))
    vec = pl.BlockSpec((1, d), lambda i: (0, 0))
    return pl.pallas_call(
        body, name=name, grid=(s // tr,),
        in_specs=[row] + [vec] * ng, out_specs=[row] * ng,
        out_shape=[jax.ShapeDtypeStruct((s, d), BF16)] * ng,
        compiler_params=_params(("parallel",)),
    )(x, *gains)


def rms_bwd(x, dres, norms, *, name):
    s, d = x.shape
    tr = _pick(s, (256, 128, 8))
    ng = len(norms)

    def body(*refs):
        x_ref, dres_ref = refs[0], refs[1]
        g_refs = refs[2:2 + ng]
        dh_refs = refs[2 + ng:2 + 2 * ng]
        dx_ref, dxb_ref = refs[2 + 2 * ng], refs[3 + 2 * ng]
        dg_refs = refs[4 + 2 * ng:]
        i = pl.program_id(0)
        xv = x_ref[...]
        r = lax.rsqrt(jnp.mean(xv * xv, axis=-1, keepdims=True) + EPS)
        xhat = xv * r
        dx = dres_ref[...]
        for g_ref, dh_ref, dg_ref in zip(g_refs, dh_refs, dg_refs):
            dh = dh_ref[...]
            part = jnp.sum(dh * xhat, axis=0, keepdims=True)

            @pl.when(i == 0)
            def _():
                dg_ref[...] = part

            @pl.when(i > 0)
            def _():
                dg_ref[...] += part

            dxhat = dh * g_ref[...]
            dx = dx + r * (dxhat - xhat * jnp.mean(dxhat * xhat, axis=-1, keepdims=True))
        dx_ref[...] = dx
        dxb_ref[...] = dx.astype(BF16)

    row = pl.BlockSpec((tr, d), lambda i: (i, 0))
    vec = pl.BlockSpec((1, d), lambda i: (0, 0))
    outs = pl.pallas_call(
        body, name=name, grid=(s // tr,),
        in_specs=[row, row] + [vec] * ng + [row] * ng,
        out_specs=[row, row] + [vec] * ng,
        out_shape=[jax.ShapeDtypeStruct((s, d), F32), jax.ShapeDtypeStruct((s, d), BF16)]
        + [jax.ShapeDtypeStruct((1, d), F32)] * ng,
        compiler_params=_params(("arbitrary",)),
    )(x, dres, *[g for g, _ in norms], *[dh for _, dh in norms])
    return outs[0], outs[1], list(outs[2:])


def loss_bwd(x2, target, gain, *, name):
    s, d = x2.shape
    tr = _pick(s, (256, 128, 8))
    nsteps = s // tr

    def body(x_ref, t_ref, g_ref, loss_ref, dg_ref, dx_ref, dxb_ref, sq_acc):
        i = pl.program_id(0)
        xv = x_ref[...]
        r = lax.rsqrt(jnp.mean(xv * xv, axis=-1, keepdims=True) + EPS)
        xhat = xv * r
        g = g_ref[...]
        err = xhat * g - t_ref[...]
        dy = err * (1.0 / d)
        sq = jnp.sum(err * err, axis=0, keepdims=True)
        dgp = jnp.sum(dy * xhat, axis=0, keepdims=True)

        @pl.when(i == 0)
        def _():
            sq_acc[...] = sq
            dg_ref[...] = dgp

        @pl.when(i > 0)
        def _():
            sq_acc[...] += sq
            dg_ref[...] += dgp

        dxhat = dy * g
        dx = r * (dxhat - xhat * jnp.mean(dxhat * xhat, axis=-1, keepdims=True))
        dx_ref[...] = dx
        dxb_ref[...] = dx.astype(BF16)

        @pl.when(i == nsteps - 1)
        def _():
            tot = jnp.sum(sq_acc[...], axis=-1, keepdims=True) * (0.5 / d)
            loss_ref[...] = jnp.broadcast_to(tot, (1, LANES))

    row = pl.BlockSpec((tr, d), lambda i: (i, 0))
    vec = pl.BlockSpec((1, d), lambda i: (0, 0))
    return pl.pallas_call(
        body, name=name, grid=(nsteps,),
        in_specs=[row, row, vec],
        out_specs=[pl.BlockSpec((1, LANES), lambda i: (0, 0)), vec, row, row],
        out_shape=[jax.ShapeDtypeStruct((1, LANES), F32), jax.ShapeDtypeStruct((1, d), F32),
                   jax.ShapeDtypeStruct((s, d), F32), jax.ShapeDtypeStruct((s, d), BF16)],
        scratch_shapes=[pltpu.VMEM((1, d), F32)],
        compiler_params=_params(("arbitrary",)),
    )(x2, target, gain)


def _lru_gates(xb, wr, wi, br, bi, sp):
    xbb = xb.astype(BF16)
    r = _sigmoid(jnp.dot(xbb, wr, preferred_element_type=F32) + br)
    ig = _sigmoid(jnp.dot(xbb, wi, preferred_element_type=F32) + bi)
    log_a = (-LRU_C) * r * sp
    a = jnp.exp(log_a)
    mult = jnp.sqrt(jnp.maximum(-jnp.tanh(log_a) * (a * a + 1.0), 0.0))
    return r, ig, a, mult


def _softplus_neg(lam):
    e = jnp.exp(-jnp.abs(lam))
    sp = jnp.maximum(-lam, 0.0) + jnp.log(1.0 + e)
    sg = jnp.where(lam >= 0, e, 1.0) / (1.0 + e)
    return sp, sg


def _conv(pad_ref, w, b, t):
    acc = b + w[CONV_W - 1:CONV_W, :] * pad_ref[pl.ds(SUBLANES, t), :]
    for dlt in range(1, CONV_W):
        acc = acc + w[CONV_W - 1 - dlt:CONV_W - dlt, :] * pad_ref[pl.ds(SUBLANES - dlt, t), :]
    return acc


def _lru_specs(t, bw, nb, time_of):
    blk = lambda c0: pl.BlockSpec((t, bw), lambda n, i, c0=c0: (time_of(i), c0 + n))
    vec = pl.BlockSpec((1, bw), lambda n, i: (0, n))
    wspec = pl.BlockSpec((None, bw, bw), lambda n, i: (n, 0, 0))
    cwspec = pl.BlockSpec((CONV_W, bw), lambda n, i: (0, n))
    return blk, vec, wspec, cwspec


def lru_fwd(proj, conv_w, conv_b, w_r, b_r, w_i, b_i, lam, *, name):
    s, r2 = proj.shape
    rr = r2 // 2
    nb, bw, _ = w_r.shape
    t = _pick(s, (512, 256, 128, 64, 32))
    ngroups = t // SUBLANES

    def body(xp_ref, gate_ref, cw_ref, cb_ref, wr_ref, br_ref, wi_ref, bi_ref, lam_ref,
             m_ref, h_ref, pad, hcarry, a_scr, u_scr):
        i = pl.program_id(1)

        @pl.when(i == 0)
        def _():
            pad[0:SUBLANES, :] = jnp.zeros((SUBLANES, bw), F32)
            hcarry[...] = jnp.zeros_like(hcarry)

        xpre = xp_ref[...]
        pad[pl.ds(SUBLANES, t), :] = xpre
        xb = _conv(pad, cw_ref[...], cb_ref[...], t)
        pad[0:SUBLANES, :] = xpre[t - SUBLANES:, :]
        sp, _ = _softplus_neg(lam_ref[...])
        _, ig, a, mult = _lru_gates(xb, wr_ref[...], wi_ref[...], br_ref[...], bi_ref[...], sp)
        a_scr[...] = a
        u_scr[...] = mult * (ig * xb)
        row = lax.broadcasted_iota(jnp.int32, (SUBLANES, bw), 0)

        def groups(gi, hprev):
            offs = [pl.multiple_of((gi * SCAN_UNROLL + u) * SUBLANES, SUBLANES) for u in range(SCAN_UNROLL)]
            scanned = []
            for off in offs:
                av = a_scr[pl.ds(off, SUBLANES), :]
                uv = u_scr[pl.ds(off, SUBLANES), :]
                for dlt in (1, 2, 4):
                    keep = row >= dlt
                    uv = jnp.where(keep, av * pltpu.roll(uv, dlt, 0) + uv, uv)
                    av = jnp.where(keep, av * pltpu.roll(av, dlt, 0), av)
                scanned.append((av, uv))
            for off, (av, uv) in zip(offs, scanned):
                hv = av * hprev + uv
                h_ref[pl.ds(off, SUBLANES), :] = hv
                hprev = hv[SUBLANES - 1:SUBLANES, :]
            return hprev

        hcarry[...] = lax.fori_loop(0, ngroups // SCAN_UNROLL, groups, hcarry[...])
        gate = gate_ref[...]
        m_ref[...] = (h_ref[...] * (gate * _sigmoid(gate))).astype(BF16)

    blk, vec, wspec, cwspec = _lru_specs(t, bw, nb, lambda i: i)
    return pl.pallas_call(
        body, name=name, grid=(nb, s // t),
        in_specs=[blk(0), blk(nb), cwspec, vec, wspec, vec, wspec, vec, vec],
        out_specs=[blk(0), blk(0)],
        out_shape=[jax.ShapeDtypeStruct((s, rr), BF16), jax.ShapeDtypeStruct((s, rr), F32)],
        scratch_shapes=[pltpu.VMEM((t + SUBLANES, bw), F32), pltpu.VMEM((1, bw), F32),
                        pltpu.VMEM((t, bw), F32), pltpu.VMEM((t, bw), F32)],
        compiler_params=_params(("parallel", "arbitrary")),
    )(proj, proj, conv_w, conv_b, w_r, b_r, w_i, b_i, lam)


def lru_bwd(proj, hst, dm, conv_w, conv_b, w_r, b_r, w_i, b_i, lam, *, name):
    s, r2 = proj.shape
    rr = r2 // 2
    nb, bw, _ = w_r.shape
    t = _pick(s, (512, 256, 128, 64, 32))
    nt = s // t
    ngroups = t // SUBLANES
    nt_dims = (((1,), (1,)), ((), ()))
    tn_dims = (((0,), (0,)), ((), ()))

    def body(xp_ref, xhalo_ref, gate_ref, h_ref, hhalo_ref, dm_ref, cw_ref, cb_ref, wr_ref, br_ref, wi_ref,
             bi_ref, lam_ref,
             dxp_ref, dgate_ref, dcw_ref, dcb_ref, dwr_ref, dbr_ref, dwi_ref, dbi_ref, dlam_ref,
             pad, hpad, dpad, ecarry, a_scr, b_scr, d_scr):
        step = pl.program_id(1)

        @pl.when(step == 0)
        def _():
            dpad[pl.ds(t, SUBLANES), :] = jnp.zeros((SUBLANES, bw), F32)
            ecarry[...] = jnp.zeros_like(ecarry)
            dcw_ref[...] = jnp.zeros_like(dcw_ref)
            dcb_ref[...] = jnp.zeros_like(dcb_ref)
            dwr_ref[...] = jnp.zeros_like(dwr_ref)
            dbr_ref[...] = jnp.zeros_like(dbr_ref)
            dwi_ref[...] = jnp.zeros_like(dwi_ref)
            dbi_ref[...] = jnp.zeros_like(dbi_ref)
            dlam_ref[...] = jnp.zeros_like(dlam_ref)

        past = jnp.where(step == nt - 1, 0.0, 1.0)
        pad[0:SUBLANES, :] = xhalo_ref[...] * past
        pad[pl.ds(SUBLANES, t), :] = xp_ref[...]
        hpad[0:SUBLANES, :] = hhalo_ref[...] * past
        hpad[pl.ds(SUBLANES, t), :] = h_ref[...]
        cw = cw_ref[...]
        xb = _conv(pad, cw, cb_ref[...], t)
        sp, sg = _softplus_neg(lam_ref[...])
        wr = wr_ref[...]
        wi = wi_ref[...]
        r, ig, a, mult = _lru_gates(xb, wr, wi, br_ref[...], bi_ref[...], sp)
        gate = gate_ref[...]
        sgate = _sigmoid(gate)
        dmv = dm_ref[...]
        dgate_ref[...] = (dmv * h_ref[...] * (sgate * (1.0 + gate * (1.0 - sgate)))).astype(BF16)
        dy = dmv * (gate * sgate)
        a_scr[...] = a
        b_scr[...] = a * dy
        row = lax.broadcasted_iota(jnp.int32, (SUBLANES, bw), 0)

        def groups(gi, enext):
            offs = [pl.multiple_of((ngroups - 1 - gi * SCAN_UNROLL - u) * SUBLANES, SUBLANES)
                    for u in range(SCAN_UNROLL)]
            scanned = []
            for off in offs:
                av = a_scr[pl.ds(off, SUBLANES), :]
                bv = b_scr[pl.ds(off, SUBLANES), :]
                for dlt in (1, 2, 4):
                    keep = row < SUBLANES - dlt
                    bv = jnp.where(keep, av * pltpu.roll(bv, SUBLANES - dlt, 0) + bv, bv)
                    av = jnp.where(keep, av * pltpu.roll(av, SUBLANES - dlt, 0), av)
                scanned.append((av, bv))
            for off, (av, bv) in zip(offs, scanned):
                ev = av * enext + bv
                d_scr[pl.ds(off, SUBLANES), :] = jnp.where(row == SUBLANES - 1, enext,
                                                           pltpu.roll(ev, SUBLANES - 1, 0))
                enext = ev[0:1, :]
            return enext

        ecarry[...] = lax.fori_loop(0, ngroups // SCAN_UNROLL, groups, ecarry[...])
        dtot = dy + d_scr[...]
        da = dtot * hpad[pl.ds(SUBLANES - 1, t), :]
        dmult = dtot * (ig * xb)
        dlog_a = da * a - dmult * (a * a) / mult
        dr_pre = dlog_a * ((-LRU_C) * sp) * (r * (1.0 - r))
        di_pre = (dtot * mult * xb) * (ig * (1.0 - ig))
        dlam_ref[...] += jnp.sum(dlog_a * r, axis=0, keepdims=True) * (LRU_C * sg)
        dbr_ref[...] += jnp.sum(dr_pre, axis=0, keepdims=True)
        dbi_ref[...] += jnp.sum(di_pre, axis=0, keepdims=True)
        drb = dr_pre.astype(BF16)
        dib = di_pre.astype(BF16)
        xbb = xb.astype(BF16)
        dxb = (dtot * mult * ig
               + lax.dot_general(drb, wr, nt_dims, preferred_element_type=F32)
               + lax.dot_general(dib, wi, nt_dims, preferred_element_type=F32))
        dwr_ref[...] += lax.dot_general(xbb, drb, tn_dims, preferred_element_type=F32)
        dwi_ref[...] += lax.dot_general(xbb, dib, tn_dims, preferred_element_type=F32)
        dcb_ref[...] += jnp.sum(dxb, axis=0, keepdims=True)
        dpad[pl.ds(0, t), :] = dxb
        dxpre = cw[CONV_W - 1:CONV_W, :] * dxb
        dcw_ref[CONV_W - 1:CONV_W, :] += jnp.sum(dxb * pad[pl.ds(SUBLANES, t), :], axis=0, keepdims=True)
        for dlt in range(1, CONV_W):
            dxpre = dxpre + cw[CONV_W - 1 - dlt:CONV_W - dlt, :] * dpad[pl.ds(dlt, t), :]
            dcw_ref[CONV_W - 1 - dlt:CONV_W - dlt, :] += jnp.sum(
                dxb * pad[pl.ds(SUBLANES - dlt, t), :], axis=0, keepdims=True)
        dpad[pl.ds(t, SUBLANES), :] = dxb[0:SUBLANES, :]
        dxp_ref[...] = dxpre.astype(BF16)

    rev = lambda i: nt - 1 - i
    blk, vec, wspec, cwspec = _lru_specs(t, bw, nb, rev)
    halo = pl.BlockSpec((SUBLANES, bw), lambda n, i: (jnp.maximum(rev(i) * ngroups - 1, 0), n))
    return pl.pallas_call(
        body, name=name, grid=(nb, nt),
        in_specs=[blk(0), halo, blk(nb), blk(0), halo, blk(0), cwspec, vec, wspec, vec, wspec, vec, vec],
        out_specs=[blk(0), blk(0), cwspec, vec, wspec, vec, wspec, vec, vec],
        out_shape=[jax.ShapeDtypeStruct((s, rr), BF16), jax.ShapeDtypeStruct((s, rr), BF16),
                   jax.ShapeDtypeStruct((CONV_W, rr), F32), jax.ShapeDtypeStruct((1, rr), F32),
                   jax.ShapeDtypeStruct((nb, bw, bw), F32), jax.ShapeDtypeStruct((1, rr), F32),
                   jax.ShapeDtypeStruct((nb, bw, bw), F32), jax.ShapeDtypeStruct((1, rr), F32),
                   jax.ShapeDtypeStruct((1, rr), F32)],
        scratch_shapes=[pltpu.VMEM((t + SUBLANES, bw), F32), pltpu.VMEM((t + SUBLANES, bw), F32),
                        pltpu.VMEM((t + SUBLANES, bw), F32), pltpu.VMEM((1, bw), F32),
                        pltpu.VMEM((t, bw), F32), pltpu.VMEM((t, bw), F32), pltpu.VMEM((t, bw), F32)],
        compiler_params=_params(("parallel", "arbitrary")),
    )(proj, proj, proj, hst, hst, dm, conv_w, conv_b, w_r, b_r, w_i, b_i, lam)


def _softplus(z):
    return jnp.maximum(z, 0.0) + jnp.log(1.0 + jnp.exp2(jnp.abs(z) * (-LOG2E)))


def _att_blocks(s):
    bk = ATT_BLOCK if s % ATT_BLOCK == 0 else s
    bq = ATT_QTILES * bk if s % (ATT_QTILES * bk) == 0 else bk
    return bk, bq


def _tile_base(i, r):
    return r * ((i * (i + 1)) // 2)


def attn_fwd(projb, kv, *, name):
    s, a2 = projb.shape
    a = a2 // 2
    nh = a // HEAD_DIM
    bk, bq = _att_blocks(s)
    r = bq // bk
    nq = s // bq
    ntiles = _tile_base(nq, r)
    scale = 1.0 / math.sqrt(HEAD_DIM)
    nt_dims = (((1,), (1,)), ((), ()))
    hp = ATT_HEADS if nh % ATT_HEADS == 0 else 1
    wd = hp * HEAD_DIM

    def body(q_ref, g_ref, k_ref, v_ref, m_ref, o_ref, saved_hbm, acc, stage, sems):
        hgrp, i = pl.program_id(0), pl.program_id(1)
        base = _tile_base(i, r)
        qb = (q_ref[...] * scale).astype(BF16)
        from_mat = (lax.broadcasted_iota(jnp.int32, (bk, bk), 0)
                    >= lax.broadcasted_iota(jnp.int32, (bk, bk), 1)).astype(BF16)
        rowi = lax.broadcasted_iota(jnp.int32, (bq, bk), 0)
        coli = lax.broadcasted_iota(jnp.int32, (bq, bk), 1)
        cols = [slice(hh * HEAD_DIM, (hh + 1) * HEAD_DIM) for hh in range(hp)]

        def save(slot, j):
            return pltpu.make_async_copy(stage.at[slot], saved_hbm.at[hgrp, base + j], sems.at[slot])

        def tile(j, n, carries, diag):
            causal = None if diag is None else (coli + diag * bk) < rowi
            slot = n % 2

            def free_slot():
                save(slot, 0).wait()

            if isinstance(n, int):
                if n >= 2:
                    free_slot()
            elif r >= 2:
                free_slot()
            else:
                pl.when(n >= 2)(free_slot)
            rows = pl.ds(pl.multiple_of(j * bk, bk), bk)
            zs = [lax.dot_general(qb[:, c], k_ref[rows, c], nt_dims, preferred_element_type=F32) for c in cols]
            sums, sigs = [], []
            for z in zs:
                sp = _softplus(z)
                sig = jnp.exp(z - sp)
                if causal is not None:
                    sp = jnp.where(causal, sp, 0.0)
                    sig = jnp.where(causal, sig, 0.0)
                sums.append(jnp.dot(sp.astype(BF16), from_mat, preferred_element_type=F32))
                sigs.append(sig.astype(BF16))
            out = []
            for hh in range(hp):
                w = jnp.exp(zs[hh] - sums[hh] - carries[hh])
                if causal is not None:
                    w = jnp.where(causal, w, 0.0)
                wb = w.astype(BF16)
                acc[:, cols[hh]] += jnp.dot(wb, v_ref[rows, cols[hh]], preferred_element_type=F32)
                stage[slot, 0, hh] = wb
                stage[slot, 1, hh] = sigs[hh]
                out.append(carries[hh] + sums[hh][:, 0:1])
            save(slot, j).start()
            return tuple(out)

        acc[...] = jnp.zeros_like(acc)
        carries = tuple(jnp.zeros((bq, 1), F32) for _ in range(hp))
        for n, dg in enumerate(reversed(range(r))):
            carries = tile(r * i + dg, n, carries, dg)
        lax.fori_loop(0, r * i, lambda jj, c: tile(r * i - 1 - jj, r + jj, c, None), carries)
        ntile = r * (i + 1)
        for back in (1, 2):
            def drain(back=back):
                save((ntile - back) % 2, 0).wait()
            if r >= back:
                drain()
            else:
                pl.when(ntile >= back)(drain)
        o = acc[...]
        o_ref[...] = o
        gate = g_ref[...]
        m_ref[...] = (o * (gate * _sigmoid(gate))).astype(BF16)

    ng = nh // hp
    qspec = lambda c0: pl.BlockSpec((bq, wd), lambda h, i, c0=c0: (i, c0 + h))
    kspec = lambda c0: pl.BlockSpec((s, wd), lambda h, i, c0=c0: (0, c0 + h), pipeline_mode=pl.Buffered(1))
    hbm = pl.BlockSpec(memory_space=pl.ANY)
    saved = jax.ShapeDtypeStruct((ng, ntiles, 2, hp, bq, bk), BF16)
    return pl.pallas_call(
        body, name=name, grid=(ng, nq),
        in_specs=[qspec(0), qspec(ng), kspec(0), kspec(ng)],
        out_specs=[qspec(0), qspec(0), hbm],
        out_shape=[jax.ShapeDtypeStruct((s, a), BF16), jax.ShapeDtypeStruct((s, a), F32), saved],
        scratch_shapes=[pltpu.VMEM((bq, wd), F32), pltpu.VMEM((2, 2, hp, bq, bk), BF16),
                        pltpu.SemaphoreType.DMA((2,))],
        compiler_params=_params(("arbitrary", "arbitrary")),
    )(projb, projb, kv, kv)


def attn_bwd(projb, dm, o, kv, saved, *, name):
    s, a2 = projb.shape
    a = a2 // 2
    nh = a // HEAD_DIM
    bk, bq = _att_blocks(s)
    r = bq // bk
    nq = s // bq
    scale = 1.0 / math.sqrt(HEAD_DIM)
    nt_dims = (((1,), (1,)), ((), ()))
    tn_dims = (((0,), (0,)), ((), ()))
    hp = saved.shape[3]
    wd = hp * HEAD_DIM
    ahead = ATT_FETCH_AHEAD

    def body(q_ref, g_ref, dm_ref, o_ref, k_ref, v_ref, saved_hbm, dq_ref, dg_ref, dk_ref, dv_ref,
             dk_acc, dv_acc, dq_acc, stage, sems):
        hgrp, i = pl.program_id(0), pl.program_id(1)
        base = _tile_base(i, r)
        ntile = r * (i + 1)

        @pl.when(i == 0)
        def _():
            dk_acc[...] = jnp.zeros_like(dk_acc)
            dv_acc[...] = jnp.zeros_like(dv_acc)

        def fetch(j):
            slot = j % (ahead + 1)
            return pltpu.make_async_copy(saved_hbm.at[hgrp, base + j], stage.at[slot], sems.at[slot])

        for j0 in range(ahead):
            pl.when(j0 < ntile)(lambda j0=j0: fetch(j0).start())
        qb = (q_ref[...] * scale).astype(BF16)
        gate = g_ref[...]
        sgate = _sigmoid(gate)
        dmv = dm_ref[...]
        dob = (dmv * (gate * sgate)).astype(BF16)
        dg_ref[...] = (dmv * o_ref[...] * (sgate * (1.0 + gate * (1.0 - sgate)))).astype(BF16)
        upto_mat = (lax.broadcasted_iota(jnp.int32, (bk, bk), 0)
                    <= lax.broadcasted_iota(jnp.int32, (bk, bk), 1)).astype(BF16)
        cols = [slice(hh * HEAD_DIM, (hh + 1) * HEAD_DIM) for hh in range(hp)]
        dq_acc[...] = jnp.zeros_like(dq_acc)

        def tile(j, gcarries):
            slot = j % (ahead + 1)
            pl.when(j + ahead < ntile)(lambda: fetch(j + ahead).start())
            fetch(j).wait()
            rows = pl.ds(pl.multiple_of(j * bk, bk), bk)
            dws = [lax.dot_general(dob[:, c], v_ref[rows, c], nt_dims, preferred_element_type=F32) for c in cols]
            gs, totals = [], []
            for hh in range(hp):
                wb = stage[slot, 0, hh]
                g = wb.astype(F32) * dws[hh]
                dv_acc[rows, cols[hh]] += lax.dot_general(wb, dob[:, cols[hh]], tn_dims, preferred_element_type=F32)
                totals.append(jnp.dot(g.astype(BF16), upto_mat, preferred_element_type=F32))
                gs.append(g)
            out = []
            for hh in range(hp):
                dz = gs[hh] - (totals[hh] + gcarries[hh]) * stage[slot, 1, hh].astype(F32)
                dzb = dz.astype(BF16)
                dq_acc[:, cols[hh]] += jnp.dot(dzb, k_ref[rows, cols[hh]], preferred_element_type=F32)
                dk_acc[rows, cols[hh]] += lax.dot_general(dzb, qb[:, cols[hh]], tn_dims, preferred_element_type=F32)
                out.append(gcarries[hh] + totals[hh][:, bk - 1:bk])
            return tuple(out)

        lax.fori_loop(0, ntile, tile, tuple(jnp.zeros((bq, 1), F32) for _ in range(hp)))
        dq_ref[...] = (dq_acc[...] * scale).astype(BF16)

        @pl.when(i == nq - 1)
        def _():
            dk_ref[...] = dk_acc[...].astype(BF16)
            dv_ref[...] = dv_acc[...].astype(BF16)

    ng = nh // hp
    once = pl.Buffered(1)
    qspec = lambda c0: pl.BlockSpec((bq, wd), lambda h, i, c0=c0: (i, c0 + h))
    kspec = lambda c0: pl.BlockSpec((s, wd), lambda h, i, c0=c0: (0, c0 + h), pipeline_mode=once)
    hbm = pl.BlockSpec(memory_space=pl.ANY)
    return pl.pallas_call(
        body, name=name, grid=(ng, nq),
        in_specs=[qspec(0), qspec(ng), qspec(0), qspec(0), kspec(0), kspec(ng), hbm],
        out_specs=[qspec(0), qspec(0), kspec(0), kspec(0)],
        out_shape=[jax.ShapeDtypeStruct((s, a), BF16)] * 4,
        scratch_shapes=[pltpu.VMEM((s, wd), F32), pltpu.VMEM((s, wd), F32), pltpu.VMEM((bq, wd), F32),
                        pltpu.VMEM((ahead + 1, 2, hp, bq, bk), BF16), pltpu.SemaphoreType.DMA((ahead + 1,))],
        compiler_params=_params(("arbitrary", "arbitrary")),
    )(projb, projb, dm, o, kv, kv, saved)


def _as2d(x):
    n = x.size
    cols = x.shape[-1]
    if cols % LANES != 0:
        cols = LANES
    return x.reshape(n // cols, cols)


def sum_parts(parts, *, name):
    p, rows, cols = parts.shape
    tr = _pick(rows, (512, 256, 128, 64, 32, 16))

    def body(p_ref, o_ref):
        acc = p_ref[0].astype(F32)
        for k in range(1, p):
            acc = acc + p_ref[k].astype(F32)
        o_ref[...] = acc

    return pl.pallas_call(
        body, name=name, grid=(rows // tr,),
        in_specs=[pl.BlockSpec((p, tr, cols), lambda i: (0, i, 0))],
        out_specs=pl.BlockSpec((tr, cols), lambda i: (i, 0)),
        out_shape=jax.ShapeDtypeStruct((rows, cols), F32),
        compiler_params=_params(("parallel",)),
    )(parts)


def adamw(w, g_parts, m, v, *, name):
    rows, cols = w.shape
    tr = _pick(rows, (128, 64, 32, 16, 8))
    np_ = len(g_parts)
    c1 = 1.0 / (1.0 - ADAM_B1 ** ADAM_STEP)
    c2 = 1.0 / (1.0 - ADAM_B2 ** ADAM_STEP)

    def body(*refs):
        w_ref, m_ref, v_ref = refs[0], refs[1], refs[2]
        g_refs = refs[3:3 + np_]
        go_ref, d_ref, mo_ref, vo_ref = refs[3 + np_:]
        g = g_refs[0][...]
        for gr in g_refs[1:]:
            g = g + gr[...]
        mn = ADAM_B1 * m_ref[...] + (1.0 - ADAM_B1) * g
        vn = ADAM_B2 * v_ref[...] + (1.0 - ADAM_B2) * (g * g)
        go_ref[...] = g
        mo_ref[...] = mn
        vo_ref[...] = vn
        d_ref[...] = (-ADAM_LR) * ((mn * c1) / (jnp.sqrt(vn * c2) + ADAM_EPS) + ADAM_WD * w_ref[...])

    spec = pl.BlockSpec((tr, cols), lambda i: (i, 0))
    return pl.pallas_call(
        body, name=name, grid=(rows // tr,),
        in_specs=[spec] * (3 + np_), out_specs=[spec] * 4,
        out_shape=[jax.ShapeDtypeStruct((rows, cols), F32)] * 4,
        compiler_params=_params(("parallel",)),
    )(w, m, v, *g_parts)


def exchange(kind, arrays, axes, *, name):
    na = len(arrays)
    hbm = pl.BlockSpec(memory_space=pl.ANY)

    def body(*refs):
        triples = _exchange(kind, refs[:na], refs[na:2 * na], axes, *refs[2 * na:])
        for step in range(3):
            for triple in triples:
                triple[step]()

    return pl.pallas_call(
        body, name=name, in_specs=[hbm] * na, out_specs=[hbm] * na,
        out_shape=_exchange_shapes(kind, arrays, axes), scratch_shapes=_exchange_sems(kind, na),
    )(*arrays)


def swap_cores(arrs, *, name):
    na = len(arrs)
    hbm = pl.BlockSpec(memory_space=pl.ANY)

    def body(*refs):
        a_refs = refs[:na]
        o_refs = refs[na:2 * na]
        send_sems, recv_sems = refs[2 * na:]
        x, y, c = _place()
        copies = []
        for ai in range(na):
            cp = pltpu.make_async_remote_copy(
                src_ref=a_refs[ai], dst_ref=o_refs[ai], send_sem=send_sems.at[ai], recv_sem=recv_sems.at[ai],
                device_id=(x, y, 1 - c), device_id_type=MESH)
            cp.start()
            copies.append(cp)
        for cp in copies:
            cp.wait()

    return pl.pallas_call(
        body, name=name,
        in_specs=[hbm] * na, out_specs=[hbm] * na,
        out_shape=[jax.ShapeDtypeStruct(a.shape, a.dtype) for a in arrs],
        scratch_shapes=[pltpu.SemaphoreType.DMA((na,)), pltpu.SemaphoreType.DMA((na,))],
    )(*arrs)


def allreduce_small(buf, *, name):
    rows, cols = buf.shape

    def body(b_ref, o_ref, slots, send_sems, recv_sems):
        x, y, c = _place()
        me = 4 * x + 2 * y + c
        slots[0] = b_ref[...]
        copies = []
        for rel in range(1, 8):
            peer = (x ^ (rel >> 2), y ^ ((rel >> 1) & 1), c ^ (rel & 1))
            cp = pltpu.make_async_remote_copy(
                src_ref=b_ref, dst_ref=slots.at[rel], send_sem=send_sems.at[rel - 1],
                recv_sem=recv_sems.at[rel - 1], device_id=peer, device_id_type=MESH)
            cp.start()
            copies.append(cp)
        for cp in copies:
            cp.wait()
        acc = slots[me]
        for dev in range(1, 8):
            acc = acc + slots[dev ^ me]
        o_ref[...] = acc

    vm = pl.BlockSpec(memory_space=pltpu.VMEM)
    return pl.pallas_call(
        body, name=name, in_specs=[vm], out_specs=vm,
        out_shape=jax.ShapeDtypeStruct((rows, cols), F32),
        scratch_shapes=[pltpu.VMEM((8, rows, cols), F32), pltpu.SemaphoreType.DMA((7,)),
                        pltpu.SemaphoreType.DMA((7,))],
    )(buf)


def _pack_rows(arrs):
    parts = []
    for a in arrs:
        p = a.reshape(-1, LANES)
        parts.append(jnp.pad(p, ((0, (-p.shape[0]) % SUBLANES), (0, 0))))
    return jnp.concatenate(parts, axis=0)


def _unpack_rows(buf, shapes):
    out, r0 = [], 0
    for shp in shapes:
        n = math.prod(shp) // LANES
        out.append(buf[r0:r0 + n].reshape(shp))
        r0 += n + (-n) % SUBLANES
    return out


def kernel(x, a_norm, a_w_in, a_conv_w, a_conv_b, a_w_r, a_b_r, a_w_i, a_b_i, a_lambda, a_w_out, kv_norm, w_kv, b_norm, b_w_in, b_w_out, final_norm, loss_target, m_a_norm, m_a_w_in, m_a_conv_w, m_a_conv_b, m_a_w_r, m_a_b_r, m_a_w_i, m_a_b_i, m_a_lambda, m_a_w_out, m_kv_norm, m_w_kv, m_b_norm, m_b_w_in, m_b_w_out, m_final_norm, v_a_norm, v_a_w_in, v_a_conv_w, v_a_conv_b, v_a_w_r, v_a_b_r, v_a_w_i, v_a_b_i, v_a_lambda, v_a_w_out, v_kv_norm, v_w_kv, v_b_norm, v_b_w_in, v_b_w_out, v_final_norm):
    weights = dict(a_norm=a_norm, a_w_in=a_w_in, a_conv_w=a_conv_w, a_conv_b=a_conv_b, a_w_r=a_w_r, a_b_r=a_b_r,
                   a_w_i=a_w_i, a_b_i=a_b_i, a_lambda=a_lambda, a_w_out=a_w_out, kv_norm=kv_norm, w_kv=w_kv,
                   b_norm=b_norm, b_w_in=b_w_in, b_w_out=b_w_out, final_norm=final_norm)
    mom1 = dict(a_norm=m_a_norm, a_w_in=m_a_w_in, a_conv_w=m_a_conv_w, a_conv_b=m_a_conv_b, a_w_r=m_a_w_r,
                a_b_r=m_a_b_r, a_w_i=m_a_w_i, a_b_i=m_a_b_i, a_lambda=m_a_lambda, a_w_out=m_a_w_out,
                kv_norm=m_kv_norm, w_kv=m_w_kv, b_norm=m_b_norm, b_w_in=m_b_w_in, b_w_out=m_b_w_out,
                final_norm=m_final_norm)
    mom2 = dict(a_norm=v_a_norm, a_w_in=v_a_w_in, a_conv_w=v_a_conv_w, a_conv_b=v_a_conv_b, a_w_r=v_a_w_r,
                a_b_r=v_a_b_r, a_w_i=v_a_w_i, a_b_i=v_a_b_i, a_lambda=v_a_lambda, a_w_out=v_a_w_out,
                kv_norm=v_kv_norm, w_kv=v_w_kv, b_norm=v_b_norm, b_w_in=v_b_w_in, b_w_out=v_b_w_out,
                final_norm=v_final_norm)
    order = list(weights)
    x0 = x[0]
    target = loss_target[0]
    d = x0.shape[1]
    chip = 2 * lax.axis_index("x") + lax.axis_index("y")

    big = ["a_w_in", "a_w_r", "a_w_i", "a_w_out", "w_kv", "b_w_in", "b_w_out"]
    big_axis = dict(a_w_in=1, a_w_r=1, a_w_i=1, a_w_out=0, w_kv=1, b_w_in=1, b_w_out=0)
    local = dict(a_w_in=a_w_in[0], a_w_r=a_w_r[0], a_w_i=a_w_i[0], a_w_out=a_w_out[0], w_kv=w_kv,
                 b_w_in=b_w_in[0], b_w_out=b_w_out[0])
    shards = {n: local[n].astype(BF16) for n in big}
    first = ["a_w_in", "a_w_r", "a_w_i"]
    full = exchange("gather_halves", [shards[n] for n in first], [big_axis[n] for n in first], name="gather_first")
    full += exchange("gather", [a_conv_w[0], b_norm], [1, 1], name="gather_small")
    wf = dict(zip(first + ["a_conv_w", "b_norm"], full))
    wf.update(a_norm=a_norm, a_conv_b=a_conv_b, a_b_r=a_b_r, a_b_i=a_b_i, a_lambda=a_lambda,
              kv_norm=kv_norm.reshape(1, d), final_norm=final_norm.reshape(1, d))
    loss_part, grad_x, parts, gsmall = _local_grads(x0, target, wf, shards=shards, axes=big_axis)

    sums = [sum_parts(parts[n].reshape(4, *_as2d(parts[n][0]).shape), name="sum_" + n) for n in big]
    others = swap_cores(sums, name="swap_cores")

    small = ["a_norm", "a_conv_b", "a_b_r", "a_b_i", "a_lambda", "kv_norm", "final_norm", "a_conv_w", "b_norm"]
    buf = _pack_rows([gsmall[n] for n in small] + [loss_part])
    red = allreduce_small(buf, name="allreduce_small")
    red_list = _unpack_rows(red, [gsmall[n].shape for n in small] + [(1, LANES)])
    gs = dict(zip(small, red_list[:-1]))
    loss = red_list[-1][0, 0]
    n_conv = a_conv_w.shape[2]
    gs["a_conv_w"] = lax.dynamic_slice_in_dim(gs["a_conv_w"], chip * n_conv, n_conv, axis=1)
    n_bn = b_norm.shape[1]
    gs["b_norm"] = lax.dynamic_slice_in_dim(gs["b_norm"], chip * n_bn, n_bn, axis=1)

    grads, deltas, new_m, new_v = {}, {}, {}, {}
    for n, s_mine, s_other in zip(big, sums, others):
        shp = weights[n].shape
        g, dlt, mn, vn = adamw(_as2d(weights[n]), [s_mine, s_other], _as2d(mom1[n]), _as2d(mom2[n]),
                               name="adamw_" + n)
        grads[n], deltas[n], new_m[n], new_v[n] = (t.reshape(shp) for t in (g, dlt, mn, vn))
    shapes = [weights[n].shape for n in small]
    wpk, gpk, mpk, vpk = (_pack_rows([src[n] for n in small]) for src in (weights, gs, mom1, mom2))
    outs = adamw(wpk, [gpk], mpk, vpk, name="adamw_small")
    for dst, packed in zip((grads, deltas, new_m, new_v), outs):
        for n, val in zip(small, _unpack_rows(packed, shapes)):
            dst[n] = val

    return (loss, grad_x[None], *[grads[n] for n in order], *[deltas[n] for n in order],
            *[new_m[n] for n in order], *[new_v[n] for n in order])


def _local_grads(x0, target, wf, shards=None, axes=None):
    a_norm, a_conv_b, a_b_r, a_b_i, a_lambda = (wf[n] for n in ("a_norm", "a_conv_b", "a_b_r", "a_b_i", "a_lambda"))
    kv_norm, final_norm = wf["kv_norm"], wf["final_norm"]
    wf = dict(wf)
    parts = {}

    def mm(*args, gather=(), scatter=None, **kw):
        if shards is None or not (gather or scatter):
            return matmul(*args, **kw)
        if gather:
            out, got = matmul(*args, exchange=("gather_halves", [shards[n] for n in gather], [axes[n] for n in gather]),
                              **kw)
            wf.update(zip(gather, got))
        else:
            out, got = matmul(*args, exchange=("scatter", list(scatter.values()), [axes[n] for n in scatter]), **kw)
            parts.update(zip(scatter, got))
        return out

    (h_a,) = rms_fwd(x0, [a_norm], name="norm_a")
    proj_a = mm(h_a, wf["a_w_in"], gather=("a_w_out",), name="a_in")
    m_a, hst = lru_fwd(proj_a, wf["a_conv_w"], a_conv_b, wf["a_w_r"], a_b_r, wf["a_w_i"], a_b_i, a_lambda,
                       name="lru_fwd")
    x1 = mm(m_a, wf["a_w_out"], residual=x0, gather=("w_kv",), name="a_out")
    kvn, hb = rms_fwd(x1, [kv_norm, wf["b_norm"]], name="norm_kv_b")
    kv = mm(kvn, wf["w_kv"], out_dtype=BF16, gather=("b_w_in",), name="kv_proj")
    proj_b = mm(hb, wf["b_w_in"], gather=("b_w_out",), name="b_in")
    m_b, o, saved = attn_fwd(proj_b, kv, name="attn_fwd")
    x2 = mm(m_b, wf["b_w_out"], residual=x1, name="b_out")
    loss_part, g_final, dx2, dx2b = loss_bwd(x2, target, final_norm, name="loss_bwd")

    dm_b = mm(dx2b, wf["b_w_out"], tb=True, name="b_out_dx")
    g_b_w_out = mm(m_b, dx2b, ta=True, out_dtype=BF16, name="b_out_dw")
    dq, dgate_b, dk, dv = attn_bwd(proj_b, dm_b, o, kv, saved, name="attn_bwd")
    dproj_b = (dq, dgate_b)
    dkv = (dk, dv)
    g_b_w_in = mm(hb, dproj_b, ta=True, out_dtype=BF16, scatter=dict(b_w_out=g_b_w_out), name="b_in_dw")
    g_w_kv = mm(kvn, dkv, ta=True, out_dtype=BF16, scatter=dict(b_w_in=g_b_w_in), name="kv_dw")
    dhb = mm(dproj_b, wf["b_w_in"], tb=True, scatter=dict(w_kv=g_w_kv), name="b_in_dx")
    dkvn = mm(dkv, wf["w_kv"], tb=True, name="kv_dx")
    dx1, dx1b, (g_kv_norm, g_b_norm) = rms_bwd(
        x1, dx2, [(kv_norm, dkvn), (wf["b_norm"], dhb)], name="norm_kv_b_bwd")

    g_a_w_out = mm(m_a, dx1b, ta=True, out_dtype=BF16, name="a_out_dw")
    dm_a = mm(dx1b, wf["a_w_out"], tb=True, scatter=dict(a_w_out=g_a_w_out), name="a_out_dx")
    dxpre, dgate_a, g_conv_w, g_conv_b, g_w_r, g_b_r, g_w_i, g_b_i, g_lambda = lru_bwd(
        proj_a, hst, dm_a, wf["a_conv_w"], a_conv_b, wf["a_w_r"], a_b_r, wf["a_w_i"], a_b_i, a_lambda,
        name="lru_bwd")
    dproj_a = (dxpre, dgate_a)
    g_w_r, g_w_i = g_w_r.astype(BF16), g_w_i.astype(BF16)
    g_a_w_in = mm(h_a, dproj_a, ta=True, out_dtype=BF16, scatter=dict(a_w_r=g_w_r, a_w_i=g_w_i), name="a_in_dw")
    dh_a = mm(dproj_a, wf["a_w_in"], tb=True, scatter=dict(a_w_in=g_a_w_in), name="a_in_dx")
    grad_x, _, (g_a_norm,) = rms_bwd(x0, dx1, [(a_norm, dh_a)], name="norm_a_bwd")

    gbig = parts if shards is not None else dict(
        a_w_in=g_a_w_in, a_w_r=g_w_r, a_w_i=g_w_i, a_w_out=g_a_w_out, w_kv=g_w_kv, b_w_in=g_b_w_in, b_w_out=g_b_w_out)
    gsmall = dict(a_norm=g_a_norm, a_conv_b=g_conv_b, a_b_r=g_b_r, a_b_i=g_b_i, a_lambda=g_lambda,
                  kv_norm=g_kv_norm, final_norm=g_final, a_conv_w=g_conv_w, b_norm=g_b_norm)
    return loss_part, grad_x, gbig, gsmall
```

```python
import math

import jax
import jax.numpy as jnp
from jax import lax
from jax.experimental import pallas as pl
from jax.experimental.pallas import tpu as pltpu

F32 = jnp.float32
BF16 = jnp.bfloat16
MESH = pl.DeviceIdType.MESH

EPS = 1e-6
LRU_C = 8.0
CONV_W = 4
HEAD_DIM = 128
ADAM_LR = 0.001
ADAM_B1 = 0.9
ADAM_B2 = 0.999
ADAM_EPS = 1e-08
ADAM_WD = 0.01
ADAM_STEP = 10

V7X_VMEM_LIMIT = 56 * 1024 * 1024
V7X_VMEM_LIMIT_HIGH = 60 * 1024 * 1024
LANES = 128
SUBLANES = 8
ATT_BLOCK = 256
ATT_QTILES = 4
ATT_HEADS = 2
ATT_FETCH_AHEAD = 2
LOG2E = 1.4426950408889634
SCAN_UNROLL = 4


def _pick(dim, cands):
    for c in cands:
        if dim % c == 0:
            return c
    return dim


def _params(sem, vmem=V7X_VMEM_LIMIT):
    return pltpu.CompilerParams(dimension_semantics=sem, vmem_limit_bytes=vmem)


def _sigmoid(x):
    return 1.0 / (1.0 + jnp.exp(-x))


def _place():
    return lax.axis_index("x"), lax.axis_index("y"), lax.axis_index("c")


def _chip_peers(x, y, c):
    return [(1 - x, y, c), (x, 1 - y, c), (1 - x, 1 - y, c)]


def _shard_of(ref, axis, idx, n):
    start = pl.multiple_of(idx * n, n)
    sl = [slice(None)] * len(ref.shape)
    sl[axis] = pl.ds(start, n)
    return ref.at[tuple(sl)]


def _half_rows(ref, h):
    n = ref.shape[0] // 2
    return ref.at[pl.ds(pl.multiple_of(h * n, n), n)]


def _block_half(ref, axis, idx, n, h):
    if axis == 0:
        return ref.at[pl.ds(pl.multiple_of(idx * n + h * (n // 2), n // 2), n // 2)]
    return _half_rows(_shard_of(ref, axis, idx, n), h)


def _exchange(kind, in_refs, out_refs, axes, send_sems, recv_sems, local_sems, send2_sems=None, recv2_sems=None):
    x, y, c = _place()
    me = 2 * x + y
    peers = _chip_peers(x, y, c)
    sibling = (x, y, 1 - c)
    triples = []
    nothing = lambda: None
    for ai, (src, dst, ax) in enumerate(zip(in_refs, out_refs, axes)):
        if kind == "scatter":
            n = dst.shape[1 + ax]
            loc = pltpu.make_async_copy(_shard_of(src, ax, me, n), dst.at[0], local_sems.at[ai])
        else:
            n = src.shape[ax]
            mine = _shard_of(dst, ax, me, n)
            loc = pltpu.make_async_copy(src, mine, local_sems.at[ai])
        triples.append((loc.start, nothing, loc.wait))
        for k, peer in enumerate(peers):
            sem = dict(send_sem=send_sems.at[ai * 3 + k], recv_sem=recv_sems.at[ai * 3 + k],
                       device_id=peer, device_id_type=MESH)
            theirs = 2 * peer[0] + peer[1]
            if kind == "gather":
                snd = pltpu.make_async_remote_copy(src_ref=src, dst_ref=mine, **sem)
                rcv = pltpu.make_async_remote_copy(src_ref=src, dst_ref=_shard_of(dst, ax, theirs, n), **sem)
                triples.append((snd.start, nothing, lambda snd=snd, rcv=rcv: (snd.wait_send(), rcv.wait_recv())))
            elif kind == "scatter":
                snd = pltpu.make_async_remote_copy(src_ref=_shard_of(src, ax, theirs, n), dst_ref=dst.at[1 + k], **sem)
                triples.append((snd.start, nothing, snd.wait))
            else:
                landed = _block_half(dst, ax, theirs, n, c)
                snd = pltpu.make_async_remote_copy(src_ref=_half_rows(src, c), dst_ref=_block_half(dst, ax, me, n, c), **sem)
                rcv = pltpu.make_async_remote_copy(src_ref=_half_rows(src, c), dst_ref=landed, **sem)
                sem2 = dict(send_sem=send2_sems.at[ai * 3 + k], recv_sem=recv2_sems.at[ai * 3 + k],
                            device_id=sibling, device_id_type=MESH)
                fwd = pltpu.make_async_remote_copy(src_ref=landed, dst_ref=landed, **sem2)
                got = pltpu.make_async_remote_copy(src_ref=landed, dst_ref=_block_half(dst, ax, theirs, n, 1 - c), **sem2)
                triples.append((snd.start, lambda rcv=rcv, fwd=fwd: (rcv.wait_recv(), fwd.start()),
                                lambda snd=snd, fwd=fwd, got=got: (snd.wait_send(), fwd.wait_send(), got.wait_recv())))
    return triples


def _exchange_shapes(kind, arrays, axes):
    out = []
    for arr, ax in zip(arrays, axes):
        shp = list(arr.shape)
        if kind == "scatter":
            shp[ax] //= 4
            out.append(jax.ShapeDtypeStruct((4, *shp), arr.dtype))
        else:
            shp[ax] *= 4
            out.append(jax.ShapeDtypeStruct(tuple(shp), arr.dtype))
    return out


def _exchange_sems(kind, n):
    sems = [pltpu.SemaphoreType.DMA((3 * n,)), pltpu.SemaphoreType.DMA((3 * n,)), pltpu.SemaphoreType.DMA((n,))]
    if kind == "gather_halves":
        sems += [pltpu.SemaphoreType.DMA((3 * n,)), pltpu.SemaphoreType.DMA((3 * n,))]
    return sems


def matmul(a, b, *, ta=False, tb=False, out_dtype=F32, residual=None, exchange=None, name):
    a_pair = a if isinstance(a, (tuple, list)) else None
    b_pair = b if isinstance(b, (tuple, list)) else None
    assert not (a_pair and ta) and not (b_pair and tb) and not (a_pair and b_pair)
    a0 = a_pair[0] if a_pair else a
    b0 = b_pair[0] if b_pair else b
    m = a0.shape[1] if ta else a0.shape[0]
    kdim = (a0.shape[0] if ta else a0.shape[1]) * (2 if a_pair else 1)
    n = (b0.shape[0] if tb else b0.shape[1]) * (2 if b_pair else 1)
    assert (b0.shape[1] if tb else b0.shape[0]) == kdim
    tm = _pick(m, (1024, 640, 512, 256, 128))
    tn = _pick(n // 2 if b_pair else n, (1024, 1280, 640, 512, 256, 128))
    tk = _pick(kdim // 2 if a_pair else kdim, (2560, 2048, 1024, 512, 256, 128))
    grid = (m // tm, n // tn, kdim // tk)
    nk = grid[2]
    kh, jh = nk // 2, grid[1] // 2
    dn = (((0 if ta else 1,), (1 if tb else 0,)), ((), ()))
    na = 2 if a_pair else 1
    nb = 2 if b_pair else 1
    nres = 0 if residual is None else 1
    nex = 0 if exchange is None else len(exchange[1])

    def body(*refs):
        a_refs, b_refs = refs[:na], refs[na:na + nb]
        p = na + nb
        r_ref = refs[p] if nres else None
        ex_in = refs[p + nres:p + nres + nex]
        o_ref = refs[p + nres + nex]
        ex_out = refs[p + 1 + nres + nex:p + 1 + nres + 2 * nex]
        acc = refs[p + 1 + nres + 2 * nex]
        sems = refs[p + 2 + nres + 2 * nex:]
        i, j, k = pl.program_id(0), pl.program_id(1), pl.program_id(2)
        if nex:
            @pl.when((i == 0) & (j == 0) & (k == 0))
            def _():
                for start, _, _ in _exchange(exchange[0], ex_in, ex_out, exchange[2], *sems):
                    start()

        @pl.when(k == 0)
        def _():
            acc[...] = jnp.zeros_like(acc)

        def accumulate(a_ref, b_ref):
            acc[...] += lax.dot_general(a_ref[...].astype(BF16), b_ref[...].astype(BF16), dn,
                                        preferred_element_type=F32)

        if a_pair:
            pl.when(k < kh)(lambda: accumulate(a_refs[0], b_refs[0]))
            pl.when(k >= kh)(lambda: accumulate(a_refs[1], b_refs[0]))
        elif b_pair:
            pl.when(j < jh)(lambda: accumulate(a_refs[0], b_refs[0]))
            pl.when(j >= jh)(lambda: accumulate(a_refs[0], b_refs[1]))
        else:
            accumulate(a_refs[0], b_refs[0])

        @pl.when(k == nk - 1)
        def _():
            r = acc[...]
            if r_ref is not None:
                r = r + r_ref[...]
            o_ref[...] = r.astype(out_dtype)

        if nex:
            @pl.when((i == grid[0] - 1) & (j == grid[1] - 1) & (k == nk - 1))
            def _():
                triples = _exchange(exchange[0], ex_in, ex_out, exchange[2], *sems)
                for _, relay, _ in triples:
                    relay()
                for _, _, finish in triples:
                    finish()

    if a_pair:
        a_specs = [pl.BlockSpec((tm, tk), lambda i, j, k: (i, jnp.minimum(k, kh - 1))),
                   pl.BlockSpec((tm, tk), lambda i, j, k: (i, jnp.maximum(k - kh, 0)))]
    else:
        a_specs = [pl.BlockSpec((tk, tm), lambda i, j, k: (k, i)) if ta
                   else pl.BlockSpec((tm, tk), lambda i, j, k: (i, k))]
    if b_pair:
        b_specs = [pl.BlockSpec((tk, tn), lambda i, j, k: (jnp.where(j < jh, k, nk - 1), jnp.minimum(j, jh - 1))),
                   pl.BlockSpec((tk, tn), lambda i, j, k: (jnp.where(j >= jh, k, 0), jnp.maximum(j - jh, 0)))]
    else:
        b_specs = [pl.BlockSpec((tn, tk), lambda i, j, k: (j, k)) if tb
                   else pl.BlockSpec((tk, tn), lambda i, j, k: (k, j))]
    o_spec = pl.BlockSpec((tm, tn), lambda i, j, k: (i, j))
    hbm = pl.BlockSpec(memory_space=pl.ANY)
    in_specs = a_specs + b_specs + [o_spec] * nres + [hbm] * nex
    args = (list(a_pair) if a_pair else [a]) + (list(b_pair) if b_pair else [b])
    args += ([residual] if nres else []) + (list(exchange[1]) if nex else [])
    out_shape = [jax.ShapeDtypeStruct((m, n), out_dtype)]
    scratch = [pltpu.VMEM((tm, tn), F32)]
    if nex:
        out_shape += _exchange_shapes(*exchange)
        scratch += _exchange_sems(exchange[0], nex)
    outs = pl.pallas_call(
        body, name=name, grid=grid,
        in_specs=in_specs, out_specs=[o_spec] + [hbm] * nex, out_shape=out_shape,
        scratch_shapes=scratch,
        compiler_params=_params(("arbitrary",) * 3 if nex else ("parallel", "parallel", "arbitrary")),
    )(*args)
    return (outs[0], list(outs[1:])) if nex else outs[0]


def rms_fwd(x, gains, *, name):
    s, d = x.shape
    tr = _pick(s, (512, 256, 128, 8))
    ng = len(gains)

    def body(*refs):
        x_ref = refs[0]
        g_refs = refs[1:1 + ng]
        o_refs = refs[1 + ng:]
        xv = x_ref[...]
        y = xv * lax.rsqrt(jnp.mean(xv * xv, axis=-1, keepdims=True) + EPS)
        for g_ref, o_ref in zip(g_refs, o_refs):
            o_ref[...] = (y * g_ref[...]).astype(BF16)

    row = pl.BlockSpec((tr, d), lambda i: (i, 0))
    vec = pl.BlockSpec((1, d), lambda i: (0, 0))
    return pl.pallas_call(
        body, name=name, grid=(s // tr,),
        in_specs=[row] + [vec] * ng, out_specs=[row] * ng,
        out_shape=[jax.ShapeDtypeStruct((s, d), BF16)] * ng,
        compiler_params=_params(("parallel",)),
    )(x, *gains)


def rms_bwd(x, dres, norms, *, name):
    s, d = x.shape
    tr = _pick(s, (256, 128, 8))
    ng = len(norms)

    def body(*refs):
        x_ref, dres_ref = refs[0], refs[1]
        g_refs = refs[2:2 + ng]
        dh_refs = refs[2 + ng:2 + 2 * ng]
        dx_ref, dxb_ref = refs[2 + 2 * ng], refs[3 + 2 * ng]
        dg_refs = refs[4 + 2 * ng:]
        i = pl.program_id(0)
        xv = x_ref[...]
        r = lax.rsqrt(jnp.mean(xv * xv, axis=-1, keepdims=True) + EPS)
        xhat = xv * r
        dx = dres_ref[...]
        for g_ref, dh_ref, dg_ref in zip(g_refs, dh_refs, dg_refs):
            dh = dh_ref[...]
            part = jnp.sum(dh * xhat, axis=0, keepdims=True)

            @pl.when(i == 0)
            def _():
                dg_ref[...] = part

            @pl.when(i > 0)
            def _():
                dg_ref[...] += part

            dxhat = dh * g_ref[...]
            dx = dx + r * (dxhat - xhat * jnp.mean(dxhat * xhat, axis=-1, keepdims=True))
        dx_ref[...] = dx
        dxb_ref[...] = dx.astype(BF16)

    row = pl.BlockSpec((tr, d), lambda i: (i, 0))
    vec = pl.BlockSpec((1, d), lambda i: (0, 0))
    outs = pl.pallas_call(
        body, name=name, grid=(s // tr,),
        in_specs=[row, row] + [vec] * ng + [row] * ng,
        out_specs=[row, row] + [vec] * ng,
        out_shape=[jax.ShapeDtypeStruct((s, d), F32), jax.ShapeDtypeStruct((s, d), BF16)]
        + [jax.ShapeDtypeStruct((1, d), F32)] * ng,
        compiler_params=_params(("arbitrary",)),
    )(x, dres, *[g for g, _ in norms], *[dh for _, dh in norms])
    return outs[0], outs[1], list(outs[2:])


def loss_bwd(x2, target, gain, *, name):
    s, d = x2.shape
    tr = _pick(s, (256, 128, 8))
    nsteps = s // tr

    def body(x_ref, t_ref, g_ref, loss_ref, dg_ref, dx_ref, dxb_ref, sq_acc):
        i = pl.program_id(0)
        xv = x_ref[...]
        r = lax.rsqrt(jnp.mean(xv * xv, axis=-1, keepdims=True) + EPS)
        xhat = xv * r
        g = g_ref[...]
        err = xhat * g - t_ref[...]
        dy = err * (1.0 / d)
        sq = jnp.sum(err * err, axis=0, keepdims=True)
        dgp = jnp.sum(dy * xhat, axis=0, keepdims=True)

        @pl.when(i == 0)
        def _():
            sq_acc[...] = sq
            dg_ref[...] = dgp

        @pl.when(i > 0)
        def _():
            sq_acc[...] += sq
            dg_ref[...] += dgp

        dxhat = dy * g
        dx = r * (dxhat - xhat * jnp.mean(dxhat * xhat, axis=-1, keepdims=True))
        dx_ref[...] = dx
        dxb_ref[...] = dx.astype(BF16)

        @pl.when(i == nsteps - 1)
        def _():
            tot = jnp.sum(sq_acc[...], axis=-1, keepdims=True) * (0.5 / d)
            loss_ref[...] = jnp.broadcast_to(tot, (1, LANES))

    row = pl.BlockSpec((tr, d), lambda i: (i, 0))
    vec = pl.BlockSpec((1, d), lambda i: (0, 0))
    return pl.pallas_call(
        body, name=name, grid=(nsteps,),
        in_specs=[row, row, vec],
        out_specs=[pl.BlockSpec((1, LANES), lambda i: (0, 0)), vec, row, row],
        out_shape=[jax.ShapeDtypeStruct((1, LANES), F32), jax.ShapeDtypeStruct((1, d), F32),
                   jax.ShapeDtypeStruct((s, d), F32), jax.ShapeDtypeStruct((s, d), BF16)],
        scratch_shapes=[pltpu.VMEM((1, d), F32)],
        compiler_params=_params(("arbitrary",)),
    )(x2, target, gain)


def _lru_gates(xb, wr, wi, br, bi, sp):
    xbb = xb.astype(BF16)
    r = _sigmoid(jnp.dot(xbb, wr, preferred_element_type=F32) + br)
    ig = _sigmoid(jnp.dot(xbb, wi, preferred_element_type=F32) + bi)
    log_a = (-LRU_C) * r * sp
    a = jnp.exp(log_a)
    mult = jnp.sqrt(jnp.maximum(-jnp.tanh(log_a) * (a * a + 1.0), 0.0))
    return r, ig, a, mult


def _softplus_neg(lam):
    e = jnp.exp(-jnp.abs(lam))
    sp = jnp.maximum(-lam, 0.0) + jnp.log(1.0 + e)
    sg = jnp.where(lam >= 0, e, 1.0) / (1.0 + e)
    return sp, sg


def _conv(pad_ref, w, b, t):
    acc = b + w[CONV_W - 1:CONV_W, :] * pad_ref[pl.ds(SUBLANES, t), :]
    for dlt in range(1, CONV_W):
        acc = acc + w[CONV_W - 1 - dlt:CONV_W - dlt, :] * pad_ref[pl.ds(SUBLANES - dlt, t), :]
    return acc


def _lru_specs(t, bw, nb, time_of):
    blk = lambda c0: pl.BlockSpec((t, bw), lambda n, i, c0=c0: (time_of(i), c0 + n))
    vec = pl.BlockSpec((1, bw), lambda n, i: (0, n))
    wspec = pl.BlockSpec((None, bw, bw), lambda n, i: (n, 0, 0))
    cwspec = pl.BlockSpec((CONV_W, bw), lambda n, i: (0, n))
    return blk, vec, wspec, cwspec


def lru_fwd(proj, conv_w, conv_b, w_r, b_r, w_i, b_i, lam, *, name):
    s, r2 = proj.shape
    rr = r2 // 2
    nb, bw, _ = w_r.shape
    t = _pick(s, (512, 256, 128, 64, 32))
    ngroups = t // SUBLANES

    def body(xp_ref, gate_ref, cw_ref, cb_ref, wr_ref, br_ref, wi_ref, bi_ref, lam_ref,
             m_ref, h_ref, pad, hcarry, a_scr, u_scr):
        i = pl.program_id(1)

        @pl.when(i == 0)
        def _():
            pad[0:SUBLANES, :] = jnp.zeros((SUBLANES, bw), F32)
            hcarry[...] = jnp.zeros_like(hcarry)

        xpre = xp_ref[...]
        pad[pl.ds(SUBLANES, t), :] = xpre
        xb = _conv(pad, cw_ref[...], cb_ref[...], t)
        pad[0:SUBLANES, :] = xpre[t - SUBLANES:, :]
        sp, _ = _softplus_neg(lam_ref[...])
        _, ig, a, mult = _lru_gates(xb, wr_ref[...], wi_ref[...], br_ref[...], bi_ref[...], sp)
        a_scr[...] = a
        u_scr[...] = mult * (ig * xb)
        row = lax.broadcasted_iota(jnp.int32, (SUBLANES, bw), 0)

        def groups(gi, hprev):
            offs = [pl.multiple_of((gi * SCAN_UNROLL + u) * SUBLANES, SUBLANES) for u in range(SCAN_UNROLL)]
            scanned = []
            for off in offs:
                av = a_scr[pl.ds(off, SUBLANES), :]
                uv = u_scr[pl.ds(off, SUBLANES), :]
                for dlt in (1, 2, 4):
                    keep = row >= dlt
                    uv = jnp.where(keep, av * pltpu.roll(uv, dlt, 0) + uv, uv)
                    av = jnp.where(keep, av * pltpu.roll(av, dlt, 0), av)
                scanned.append((av, uv))
            for off, (av, uv) in zip(offs, scanned):
                hv = av * hprev + uv
                h_ref[pl.ds(off, SUBLANES), :] = hv
                hprev = hv[SUBLANES - 1:SUBLANES, :]
            return hprev

        hcarry[...] = lax.fori_loop(0, ngroups // SCAN_UNROLL, groups, hcarry[...])
        gate = gate_ref[...]
        m_ref[...] = (h_ref[...] * (gate * _sigmoid(gate))).astype(BF16)

    blk, vec, wspec, cwspec = _lru_specs(t, bw, nb, lambda i: i)
    return pl.pallas_call(
        body, name=name, grid=(nb, s // t),
        in_specs=[blk(0), blk(nb), cwspec, vec, wspec, vec, wspec, vec, vec],
        out_specs=[blk(0), blk(0)],
        out_shape=[jax.ShapeDtypeStruct((s, rr), BF16), jax.ShapeDtypeStruct((s, rr), F32)],
        scratch_shapes=[pltpu.VMEM((t + SUBLANES, bw), F32), pltpu.VMEM((1, bw), F32),
                        pltpu.VMEM((t, bw), F32), pltpu.VMEM((t, bw), F32)],
        compiler_params=_params(("parallel", "arbitrary")),
    )(proj, proj, conv_w, conv_b, w_r, b_r, w_i, b_i, lam)


def lru_bwd(proj, hst, dm, conv_w, conv_b, w_r, b_r, w_i, b_i, lam, *, name):
    s, r2 = proj.shape
    rr = r2 // 2
    nb, bw, _ = w_r.shape
    t = _pick(s, (512, 256, 128, 64, 32))
    nt = s // t
    ngroups = t // SUBLANES
    nt_dims = (((1,), (1,)), ((), ()))
    tn_dims = (((0,), (0,)), ((), ()))

    def body(xp_ref, xhalo_ref, gate_ref, h_ref, hhalo_ref, dm_ref, cw_ref, cb_ref, wr_ref, br_ref, wi_ref,
             bi_ref, lam_ref,
             dxp_ref, dgate_ref, dcw_ref, dcb_ref, dwr_ref, dbr_ref, dwi_ref, dbi_ref, dlam_ref,
             pad, hpad, dpad, ecarry, a_scr, b_scr, d_scr):
        step = pl.program_id(1)

        @pl.when(step == 0)
        def _():
            dpad[pl.ds(t, SUBLANES), :] = jnp.zeros((SUBLANES, bw), F32)
            ecarry[...] = jnp.zeros_like(ecarry)
            dcw_ref[...] = jnp.zeros_like(dcw_ref)
            dcb_ref[...] = jnp.zeros_like(dcb_ref)
            dwr_ref[...] = jnp.zeros_like(dwr_ref)
            dbr_ref[...] = jnp.zeros_like(dbr_ref)
            dwi_ref[...] = jnp.zeros_like(dwi_ref)
            dbi_ref[...] = jnp.zeros_like(dbi_ref)
            dlam_ref[...] = jnp.zeros_like(dlam_ref)

        past = jnp.where(step == nt - 1, 0.0, 1.0)
        pad[0:SUBLANES, :] = xhalo_ref[...] * past
        pad[pl.ds(SUBLANES, t), :] = xp_ref[...]
        hpad[0:SUBLANES, :] = hhalo_ref[...] * past
        hpad[pl.ds(SUBLANES, t), :] = h_ref[...]
        cw = cw_ref[...]
        xb = _conv(pad, cw, cb_ref[...], t)
        sp, sg = _softplus_neg(lam_ref[...])
        wr = wr_ref[...]
        wi = wi_ref[...]
        r, ig, a, mult = _lru_gates(xb, wr, wi, br_ref[...], bi_ref[...], sp)
        gate = gate_ref[...]
        sgate = _sigmoid(gate)
        dmv = dm_ref[...]
        dgate_ref[...] = (dmv * h_ref[...] * (sgate * (1.0 + gate * (1.0 - sgate)))).astype(BF16)
        dy = dmv * (gate * sgate)
        a_scr[...] = a
        b_scr[...] = a * dy
        row = lax.broadcasted_iota(jnp.int32, (SUBLANES, bw), 0)

        def groups(gi, enext):
            offs = [pl.multiple_of((ngroups - 1 - gi * SCAN_UNROLL - u) * SUBLANES, SUBLANES)
                    for u in range(SCAN_UNROLL)]
            scanned = []
            for off in offs:
                av = a_scr[pl.ds(off, SUBLANES), :]
                bv = b_scr[pl.ds(off, SUBLANES), :]
                for dlt in (1, 2, 4):
                    keep = row < SUBLANES - dlt
                    bv = jnp.where(keep, av * pltpu.roll(bv, SUBLANES - dlt, 0) + bv, bv)
                    av = jnp.where(keep, av * pltpu.roll(av, SUBLANES - dlt, 0), av)
                scanned.append((av, bv))
            for off, (av, bv) in zip(offs, scanned):
                ev = av * enext + bv
                d_scr[pl.ds(off, SUBLANES), :] = jnp.where(row == SUBLANES - 1, enext,
                                                           pltpu.roll(ev, SUBLANES - 1, 0))
                enext = ev[0:1, :]
            return enext

        ecarry[...] = lax.fori_loop(0, ngroups // SCAN_UNROLL, groups, ecarry[...])
        dtot = dy + d_scr[...]
        da = dtot * hpad[pl.ds(SUBLANES - 1, t), :]
        dmult = dtot * (ig * xb)
        dlog_a = da * a - dmult * (a * a) / mult
        dr_pre = dlog_a * ((-LRU_C) * sp) * (r * (1.0 - r))
        di_pre = (dtot * mult * xb) * (ig * (1.0 - ig))
        dlam_ref[...] += jnp.sum(dlog_a * r, axis=0, keepdims=True) * (LRU_C * sg)
        dbr_ref[...] += jnp.sum(dr_pre, axis=0, keepdims=True)
        dbi_ref[...] += jnp.sum(di_pre, axis=0, keepdims=True)
        drb = dr_pre.astype(BF16)
        dib = di_pre.astype(BF16)
        xbb = xb.astype(BF16)
        dxb = (dtot * mult * ig
               + lax.dot_general(drb, wr, nt_dims, preferred_element_type=F32)
               + lax.dot_general(dib, wi, nt_dims, preferred_element_type=F32))
        dwr_ref[...] += lax.dot_general(xbb, drb, tn_dims, preferred_element_type=F32)
        dwi_ref[...] += lax.dot_general(xbb, dib, tn_dims, preferred_element_type=F32)
        dcb_ref[...] += jnp.sum(dxb, axis=0, keepdims=True)
        dpad[pl.ds(0, t), :] = dxb
        dxpre = cw[CONV_W - 1:CONV_W, :] * dxb
        dcw_ref[CONV_W - 1:CONV_W, :] += jnp.sum(dxb * pad[pl.ds(SUBLANES, t), :], axis=0, keepdims=True)
        for dlt in range(1, CONV_W):
            dxpre = dxpre + cw[CONV_W - 1 - dlt:CONV_W - dlt, :] * dpad[pl.ds(dlt, t), :]
            dcw_ref[CONV_W - 1 - dlt:CONV_W - dlt, :] += jnp.sum(
                dxb * pad[pl.ds(SUBLANES - dlt, t), :], axis=0, keepdims=True)
        dpad[pl.ds(t, SUBLANES), :] = dxb[0:SUBLANES, :]
        dxp_ref[...] = dxpre.astype(BF16)

    rev = lambda i: nt - 1 - i
    blk, vec, wspec, cwspec = _lru_specs(t, bw, nb, rev)
    halo = pl.BlockSpec((SUBLANES, bw), lambda n, i: (jnp.maximum(rev(i) * ngroups - 1, 0), n))
    return pl.pallas_call(
        body, name=name, grid=(nb, nt),
        in_specs=[blk(0), halo, blk(nb), blk(0), halo, blk(0), cwspec, vec, wspec, vec, wspec, vec, vec],
        out_specs=[blk(0), blk(0), cwspec, vec, wspec, vec, wspec, vec, vec],
        out_shape=[jax.ShapeDtypeStruct((s, rr), BF16), jax.ShapeDtypeStruct((s, rr), BF16),
                   jax.ShapeDtypeStruct((CONV_W, rr), F32), jax.ShapeDtypeStruct((1, rr), F32),
                   jax.ShapeDtypeStruct((nb, bw, bw), F32), jax.ShapeDtypeStruct((1, rr), F32),
                   jax.ShapeDtypeStruct((nb, bw, bw), F32), jax.ShapeDtypeStruct((1, rr), F32),
                   jax.ShapeDtypeStruct((1, rr), F32)],
        scratch_shapes=[pltpu.VMEM((t + SUBLANES, bw), F32), pltpu.VMEM((t + SUBLANES, bw), F32),
                        pltpu.VMEM((t + SUBLANES, bw), F32), pltpu.VMEM((1, bw), F32),
                        pltpu.VMEM((t, bw), F32), pltpu.VMEM((t, bw), F32), pltpu.VMEM((t, bw), F32)],
        compiler_params=_params(("parallel", "arbitrary")),
    )(proj, proj, proj, hst, hst, dm, conv_w, conv_b, w_r, b_r, w_i, b_i, lam)


def _softplus(z):
    return jnp.maximum(z, 0.0) + jnp.log(1.0 + jnp.exp2(jnp.abs(z) * (-LOG2E)))


def _att_blocks(s):
    bk = ATT_BLOCK if s % ATT_BLOCK == 0 else s
    bq = ATT_QTILES * bk if s % (ATT_QTILES * bk) == 0 else bk
    return bk, bq


def _tile_base(i, r):
    return r * ((i * (i + 1)) // 2)


def attn_fwd(projb, kv, *, name):
    s, a2 = projb.shape
    a = a2 // 2
    nh = a // HEAD_DIM
    bk, bq = _att_blocks(s)
    r = bq // bk
    nq = s // bq
    ntiles = _tile_base(nq, r)
    scale = 1.0 / math.sqrt(HEAD_DIM)
    nt_dims = (((1,), (1,)), ((), ()))
    hp = ATT_HEADS if nh % ATT_HEADS == 0 else 1
    wd = hp * HEAD_DIM

    def body(q_ref, g_ref, k_ref, v_ref, m_ref, o_ref, saved_hbm, acc, stage, sems):
        hgrp, i = pl.program_id(0), pl.program_id(1)
        base = _tile_base(i, r)
        qb = (q_ref[...] * scale).astype(BF16)
        from_mat = (lax.broadcasted_iota(jnp.int32, (bk, bk), 0)
                    >= lax.broadcasted_iota(jnp.int32, (bk, bk), 1)).astype(BF16)
        rowi = lax.broadcasted_iota(jnp.int32, (bq, bk), 0)
        coli = lax.broadcasted_iota(jnp.int32, (bq, bk), 1)
        cols = [slice(hh * HEAD_DIM, (hh + 1) * HEAD_DIM) for hh in range(hp)]

        def save(slot, j):
            return pltpu.make_async_copy(stage.at[slot], saved_hbm.at[hgrp, base + j], sems.at[slot])

        def tile(j, n, carries, diag):
            r0 = 0 if diag is None else diag * bk
            live = slice(r0, bq)
            causal = None if diag is None else coli[live] < rowi[:bq - r0]
            slot = n % 2

            def free_slot():
                save(slot, 0).wait()

            if isinstance(n, int):
                if n >= 2:
                    free_slot()
            elif r >= 2:
                free_slot()
            else:
                pl.when(n >= 2)(free_slot)
            rows = pl.ds(pl.multiple_of(j * bk, bk), bk)
            zs = [lax.dot_general(qb[live, c], k_ref[rows, c], nt_dims, preferred_element_type=F32) for c in cols]
            sums, sigs = [], []
            for z in zs:
                sp = _softplus(z)
                sig = jnp.exp(z - sp)
                if causal is not None:
                    sp = jnp.where(causal, sp, 0.0)
                    sig = jnp.where(causal, sig, 0.0)
                sums.append(jnp.dot(sp.astype(BF16), from_mat, preferred_element_type=F32))
                sigs.append(sig.astype(BF16))
            out = []
            for hh in range(hp):
                w = jnp.exp(zs[hh] - sums[hh] - carries[hh][live])
                if causal is not None:
                    w = jnp.where(causal, w, 0.0)
                wb = w.astype(BF16)
                acc[live, cols[hh]] += jnp.dot(wb, v_ref[rows, cols[hh]], preferred_element_type=F32)
                stage[slot, 0, hh, live] = wb
                stage[slot, 1, hh, live] = sigs[hh]
                if r0:
                    stage[slot, :, hh, :r0] = jnp.zeros((2, r0, bk), BF16)
                grown = carries[hh][live] + sums[hh][:, 0:1]
                out.append(jnp.concatenate([carries[hh][:r0], grown], axis=0) if r0 else grown)
            save(slot, j).start()
            return tuple(out)

        acc[...] = jnp.zeros_like(acc)
        carries = tuple(jnp.zeros((bq, 1), F32) for _ in range(hp))
        for n, dg in enumerate(reversed(range(r))):
            carries = tile(r * i + dg, n, carries, dg)
        lax.fori_loop(0, r * i, lambda jj, c: tile(r * i - 1 - jj, r + jj, c, None), carries)
        ntile = r * (i + 1)
        for back in (1, 2):
            def drain(back=back):
                save((ntile - back) % 2, 0).wait()
            if r >= back:
                drain()
            else:
                pl.when(ntile >= back)(drain)
        o = acc[...]
        o_ref[...] = o
        gate = g_ref[...]
        m_ref[...] = (o * (gate * _sigmoid(gate))).astype(BF16)

    ng = nh // hp
    qspec = lambda c0: pl.BlockSpec((bq, wd), lambda h, i, c0=c0: (i, c0 + h))
    kspec = lambda c0: pl.BlockSpec((s, wd), lambda h, i, c0=c0: (0, c0 + h), pipeline_mode=pl.Buffered(1))
    hbm = pl.BlockSpec(memory_space=pl.ANY)
    saved = jax.ShapeDtypeStruct((ng, ntiles, 2, hp, bq, bk), BF16)
    return pl.pallas_call(
        body, name=name, grid=(ng, nq),
        in_specs=[qspec(0), qspec(ng), kspec(0), kspec(ng)],
        out_specs=[qspec(0), qspec(0), hbm],
        out_shape=[jax.ShapeDtypeStruct((s, a), BF16), jax.ShapeDtypeStruct((s, a), F32), saved],
        scratch_shapes=[pltpu.VMEM((bq, wd), F32), pltpu.VMEM((2, 2, hp, bq, bk), BF16),
                        pltpu.SemaphoreType.DMA((2,))],
        compiler_params=_params(("arbitrary", "arbitrary")),
    )(projb, projb, kv, kv)


def attn_bwd(projb, dm, o, kv, saved, *, name):
    s, a2 = projb.shape
    a = a2 // 2
    nh = a // HEAD_DIM
    bk, bq = _att_blocks(s)
    r = bq // bk
    nq = s // bq
    scale = 1.0 / math.sqrt(HEAD_DIM)
    nt_dims = (((1,), (1,)), ((), ()))
    tn_dims = (((0,), (0,)), ((), ()))
    hp = saved.shape[3]
    wd = hp * HEAD_DIM
    ahead = ATT_FETCH_AHEAD

    def body(q_ref, g_ref, dm_ref, o_ref, k_ref, v_ref, saved_hbm, dq_ref, dg_ref, dk_ref, dv_ref,
             dk_acc, dv_acc, dq_acc, stage, sems):
        hgrp, i = pl.program_id(0), pl.program_id(1)
        base = _tile_base(i, r)
        ntile = r * (i + 1)

        @pl.when(i == 0)
        def _():
            dk_acc[...] = jnp.zeros_like(dk_acc)
            dv_acc[...] = jnp.zeros_like(dv_acc)

        def fetch(j):
            slot = j % (ahead + 1)
            return pltpu.make_async_copy(saved_hbm.at[hgrp, base + j], stage.at[slot], sems.at[slot])

        for j0 in range(ahead):
            pl.when(j0 < ntile)(lambda j0=j0: fetch(j0).start())
        qb = (q_ref[...] * scale).astype(BF16)
        gate = g_ref[...]
        sgate = _sigmoid(gate)
        dmv = dm_ref[...]
        dob = (dmv * (gate * sgate)).astype(BF16)
        dg_ref[...] = (dmv * o_ref[...] * (sgate * (1.0 + gate * (1.0 - sgate)))).astype(BF16)
        upto_mat = (lax.broadcasted_iota(jnp.int32, (bk, bk), 0)
                    <= lax.broadcasted_iota(jnp.int32, (bk, bk), 1)).astype(BF16)
        cols = [slice(hh * HEAD_DIM, (hh + 1) * HEAD_DIM) for hh in range(hp)]
        dq_acc[...] = jnp.zeros_like(dq_acc)

        def tile(j, gcarries, r0=0, more=None):
            live = slice(r0, bq)
            slot = j % (ahead + 1)
            if more is None:
                pl.when(j + ahead < ntile)(lambda: fetch(j + ahead).start())
            elif more:
                fetch(j + ahead).start()
            fetch(j).wait()
            rows = pl.ds(pl.multiple_of(j * bk, bk), bk)
            dws = [lax.dot_general(dob[live, c], v_ref[rows, c], nt_dims, preferred_element_type=F32) for c in cols]
            gs, totals = [], []
            for hh in range(hp):
                wb = stage[slot, 0, hh, live]
                g = wb.astype(F32) * dws[hh]
                dv_acc[rows, cols[hh]] += lax.dot_general(wb, dob[live, cols[hh]], tn_dims,
                                                          preferred_element_type=F32)
                totals.append(jnp.dot(g.astype(BF16), upto_mat, preferred_element_type=F32))
                gs.append(g)
            out = []
            for hh in range(hp):
                dz = gs[hh] - (totals[hh] + gcarries[hh][live]) * stage[slot, 1, hh, live].astype(F32)
                dzb = dz.astype(BF16)
                dq_acc[live, cols[hh]] += jnp.dot(dzb, k_ref[rows, cols[hh]], preferred_element_type=F32)
                dk_acc[rows, cols[hh]] += lax.dot_general(dzb, qb[live, cols[hh]], tn_dims,
                                                          preferred_element_type=F32)
                grown = gcarries[hh][live] + totals[hh][:, bk - 1:bk]
                out.append(jnp.concatenate([gcarries[hh][:r0], grown], axis=0) if r0 else grown)
            return tuple(out)

        gcarries = lax.fori_loop(0, r * i, tile, tuple(jnp.zeros((bq, 1), F32) for _ in range(hp)))
        for dg in range(r):
            gcarries = tile(r * i + dg, gcarries, dg * bk, dg + ahead < r)
        dq_ref[...] = (dq_acc[...] * scale).astype(BF16)

        @pl.when(i == nq - 1)
        def _():
            dk_ref[...] = dk_acc[...].astype(BF16)
            dv_ref[...] = dv_acc[...].astype(BF16)

    ng = nh // hp
    once = pl.Buffered(1)
    qspec = lambda c0: pl.BlockSpec((bq, wd), lambda h, i, c0=c0: (i, c0 + h))
    kspec = lambda c0: pl.BlockSpec((s, wd), lambda h, i, c0=c0: (0, c0 + h), pipeline_mode=once)
    hbm = pl.BlockSpec(memory_space=pl.ANY)
    return pl.pallas_call(
        body, name=name, grid=(ng, nq),
        in_specs=[qspec(0), qspec(ng), qspec(0), qspec(0), kspec(0), kspec(ng), hbm],
        out_specs=[qspec(0), qspec(0), kspec(0), kspec(0)],
        out_shape=[jax.ShapeDtypeStruct((s, a), BF16)] * 4,
        scratch_shapes=[pltpu.VMEM((s, wd), F32), pltpu.VMEM((s, wd), F32), pltpu.VMEM((bq, wd), F32),
                        pltpu.VMEM((ahead + 1, 2, hp, bq, bk), BF16), pltpu.SemaphoreType.DMA((ahead + 1,))],
        compiler_params=_params(("arbitrary", "arbitrary"), vmem=V7X_VMEM_LIMIT_HIGH),
    )(projb, projb, dm, o, kv, kv, saved)


def _as2d(x):
    n = x.size
    cols = x.shape[-1]
    if cols % LANES != 0:
        cols = LANES
    return x.reshape(n // cols, cols)


def sum_parts(parts, *, name):
    p, rows, cols = parts.shape
    tr = _pick(rows, (512, 256, 128, 64, 32, 16))

    def body(p_ref, o_ref):
        acc = p_ref[0].astype(F32)
        for k in range(1, p):
            acc = acc + p_ref[k].astype(F32)
        o_ref[...] = acc

    return pl.pallas_call(
        body, name=name, grid=(rows // tr,),
        in_specs=[pl.BlockSpec((p, tr, cols), lambda i: (0, i, 0))],
        out_specs=pl.BlockSpec((tr, cols), lambda i: (i, 0)),
        out_shape=jax.ShapeDtypeStruct((rows, cols), F32),
        compiler_params=_params(("parallel",)),
    )(parts)


def adamw(w, g_parts, m, v, *, name):
    rows, cols = w.shape
    tr = _pick(rows, (128, 64, 32, 16, 8))
    np_ = len(g_parts)
    c1 = 1.0 / (1.0 - ADAM_B1 ** ADAM_STEP)
    c2 = 1.0 / (1.0 - ADAM_B2 ** ADAM_STEP)

    def body(*refs):
        w_ref, m_ref, v_ref = refs[0], refs[1], refs[2]
        g_refs = refs[3:3 + np_]
        go_ref, d_ref, mo_ref, vo_ref = refs[3 + np_:]
        g = g_refs[0][...]
        for gr in g_refs[1:]:
            g = g + gr[...]
        mn = ADAM_B1 * m_ref[...] + (1.0 - ADAM_B1) * g
        vn = ADAM_B2 * v_ref[...] + (1.0 - ADAM_B2) * (g * g)
        go_ref[...] = g
        mo_ref[...] = mn
        vo_ref[...] = vn
        d_ref[...] = (-ADAM_LR) * ((mn * c1) / (jnp.sqrt(vn * c2) + ADAM_EPS) + ADAM_WD * w_ref[...])

    spec = pl.BlockSpec((tr, cols), lambda i: (i, 0))
    return pl.pallas_call(
        body, name=name, grid=(rows // tr,),
        in_specs=[spec] * (3 + np_), out_specs=[spec] * 4,
        out_shape=[jax.ShapeDtypeStruct((rows, cols), F32)] * 4,
        compiler_params=_params(("parallel",)),
    )(w, m, v, *g_parts)


def exchange(kind, arrays, axes, *, name):
    na = len(arrays)
    hbm = pl.BlockSpec(memory_space=pl.ANY)

    def body(*refs):
        triples = _exchange(kind, refs[:na], refs[na:2 * na], axes, *refs[2 * na:])
        for step in range(3):
            for triple in triples:
                triple[step]()

    return pl.pallas_call(
        body, name=name, in_specs=[hbm] * na, out_specs=[hbm] * na,
        out_shape=_exchange_shapes(kind, arrays, axes), scratch_shapes=_exchange_sems(kind, na),
    )(*arrays)


def swap_cores(arrs, *, name):
    na = len(arrs)
    hbm = pl.BlockSpec(memory_space=pl.ANY)

    def body(*refs):
        a_refs = refs[:na]
        o_refs = refs[na:2 * na]
        send_sems, recv_sems = refs[2 * na:]
        x, y, c = _place()
        copies = []
        for ai in range(na):
            cp = pltpu.make_async_remote_copy(
                src_ref=a_refs[ai], dst_ref=o_refs[ai], send_sem=send_sems.at[ai], recv_sem=recv_sems.at[ai],
                device_id=(x, y, 1 - c), device_id_type=MESH)
            cp.start()
            copies.append(cp)
        for cp in copies:
            cp.wait()

    return pl.pallas_call(
        body, name=name,
        in_specs=[hbm] * na, out_specs=[hbm] * na,
        out_shape=[jax.ShapeDtypeStruct(a.shape, a.dtype) for a in arrs],
        scratch_shapes=[pltpu.SemaphoreType.DMA((na,)), pltpu.SemaphoreType.DMA((na,))],
    )(*arrs)


def allreduce_small(buf, *, name):
    rows, cols = buf.shape

    def body(b_ref, o_ref, slots, send_sems, recv_sems):
        x, y, c = _place()
        me = 4 * x + 2 * y + c
        slots[0] = b_ref[...]
        copies = []
        for rel in range(1, 8):
            peer = (x ^ (rel >> 2), y ^ ((rel >> 1) & 1), c ^ (rel & 1))
            cp = pltpu.make_async_remote_copy(
                src_ref=b_ref, dst_ref=slots.at[rel], send_sem=send_sems.at[rel - 1],
                recv_sem=recv_sems.at[rel - 1], device_id=peer, device_id_type=MESH)
            cp.start()
            copies.append(cp)
        for cp in copies:
            cp.wait()
        acc = slots[me]
        for dev in range(1, 8):
            acc = acc + slots[dev ^ me]
        o_ref[...] = acc

    vm = pl.BlockSpec(memory_space=pltpu.VMEM)
    return pl.pallas_call(
        body, name=name, in_specs=[vm], out_specs=vm,
        out_shape=jax.ShapeDtypeStruct((rows, cols), F32),
        scratch_shapes=[pltpu.VMEM((8, rows, cols), F32), pltpu.SemaphoreType.DMA((7,)),
                        pltpu.SemaphoreType.DMA((7,))],
    )(buf)


def _pack_rows(arrs):
    parts = []
    for a in arrs:
        p = a.reshape(-1, LANES)
        parts.append(jnp.pad(p, ((0, (-p.shape[0]) % SUBLANES), (0, 0))))
    return jnp.concatenate(parts, axis=0)


def _unpack_rows(buf, shapes):
    out, r0 = [], 0
    for shp in shapes:
        n = math.prod(shp) // LANES
        out.append(buf[r0:r0 + n].reshape(shp))
        r0 += n + (-n) % SUBLANES
    return out


def kernel(x, a_norm, a_w_in, a_conv_w, a_conv_b, a_w_r, a_b_r, a_w_i, a_b_i, a_lambda, a_w_out, kv_norm, w_kv, b_norm, b_w_in, b_w_out, final_norm, loss_target, m_a_norm, m_a_w_in, m_a_conv_w, m_a_conv_b, m_a_w_r, m_a_b_r, m_a_w_i, m_a_b_i, m_a_lambda, m_a_w_out, m_kv_norm, m_w_kv, m_b_norm, m_b_w_in, m_b_w_out, m_final_norm, v_a_norm, v_a_w_in, v_a_conv_w, v_a_conv_b, v_a_w_r, v_a_b_r, v_a_w_i, v_a_b_i, v_a_lambda, v_a_w_out, v_kv_norm, v_w_kv, v_b_norm, v_b_w_in, v_b_w_out, v_final_norm):
    weights = dict(a_norm=a_norm, a_w_in=a_w_in, a_conv_w=a_conv_w, a_conv_b=a_conv_b, a_w_r=a_w_r, a_b_r=a_b_r,
                   a_w_i=a_w_i, a_b_i=a_b_i, a_lambda=a_lambda, a_w_out=a_w_out, kv_norm=kv_norm, w_kv=w_kv,
                   b_norm=b_norm, b_w_in=b_w_in, b_w_out=b_w_out, final_norm=final_norm)
    mom1 = dict(a_norm=m_a_norm, a_w_in=m_a_w_in, a_conv_w=m_a_conv_w, a_conv_b=m_a_conv_b, a_w_r=m_a_w_r,
                a_b_r=m_a_b_r, a_w_i=m_a_w_i, a_b_i=m_a_b_i, a_lambda=m_a_lambda, a_w_out=m_a_w_out,
                kv_norm=m_kv_norm, w_kv=m_w_kv, b_norm=m_b_norm, b_w_in=m_b_w_in, b_w_out=m_b_w_out,
                final_norm=m_final_norm)
    mom2 = dict(a_norm=v_a_norm, a_w_in=v_a_w_in, a_conv_w=v_a_conv_w, a_conv_b=v_a_conv_b, a_w_r=v_a_w_r,
                a_b_r=v_a_b_r, a_w_i=v_a_w_i, a_b_i=v_a_b_i, a_lambda=v_a_lambda, a_w_out=v_a_w_out,
                kv_norm=v_kv_norm, w_kv=v_w_kv, b_norm=v_b_norm, b_w_in=v_b_w_in, b_w_out=v_b_w_out,
                final_norm=v_final_norm)
    order = list(weights)
    x0 = x[0]
    target = loss_target[0]
    d = x0.shape[1]
    chip = 2 * lax.axis_index("x") + lax.axis_index("y")

    big = ["a_w_in", "a_w_r", "a_w_i", "a_w_out", "w_kv", "b_w_in", "b_w_out"]
    big_axis = dict(a_w_in=1, a_w_r=1, a_w_i=1, a_w_out=0, w_kv=1, b_w_in=1, b_w_out=0)
    local = dict(a_w_in=a_w_in[0], a_w_r=a_w_r[0], a_w_i=a_w_i[0], a_w_out=a_w_out[0], w_kv=w_kv,
                 b_w_in=b_w_in[0], b_w_out=b_w_out[0])
    shards = {n: local[n].astype(BF16) for n in big}
    first = ["a_w_in", "a_w_r", "a_w_i"]
    full = exchange("gather_halves", [shards[n] for n in first], [big_axis[n] for n in first], name="gather_first")
    full += exchange("gather", [a_conv_w[0], b_norm], [1, 1], name="gather_small")
    wf = dict(zip(first + ["a_conv_w", "b_norm"], full))
    wf.update(a_norm=a_norm, a_conv_b=a_conv_b, a_b_r=a_b_r, a_b_i=a_b_i, a_lambda=a_lambda,
              kv_norm=kv_norm.reshape(1, d), final_norm=final_norm.reshape(1, d))
    loss_part, grad_x, parts, gsmall = _local_grads(x0, target, wf, shards=shards, axes=big_axis)

    sums = [sum_parts(parts[n].reshape(4, *_as2d(parts[n][0]).shape), name="sum_" + n) for n in big]
    others = swap_cores(sums, name="swap_cores")

    small = ["a_norm", "a_conv_b", "a_b_r", "a_b_i", "a_lambda", "kv_norm", "final_norm", "a_conv_w", "b_norm"]
    buf = _pack_rows([gsmall[n] for n in small] + [loss_part])
    red = allreduce_small(buf, name="allreduce_small")
    red_list = _unpack_rows(red, [gsmall[n].shape for n in small] + [(1, LANES)])
    gs = dict(zip(small, red_list[:-1]))
    loss = red_list[-1][0, 0]
    n_conv = a_conv_w.shape[2]
    gs["a_conv_w"] = lax.dynamic_slice_in_dim(gs["a_conv_w"], chip * n_conv, n_conv, axis=1)
    n_bn = b_norm.shape[1]
    gs["b_norm"] = lax.dynamic_slice_in_dim(gs["b_norm"], chip * n_bn, n_bn, axis=1)

    grads, deltas, new_m, new_v = {}, {}, {}, {}
    for n, s_mine, s_other in zip(big, sums, others):
        shp = weights[n].shape
        g, dlt, mn, vn = adamw(_as2d(weights[n]), [s_mine, s_other], _as2d(mom1[n]), _as2d(mom2[n]),
                               name="adamw_" + n)
        grads[n], deltas[n], new_m[n], new_v[n] = (t.reshape(shp) for t in (g, dlt, mn, vn))
    shapes = [weights[n].shape for n in small]
    wpk, gpk, mpk, vpk = (_pack_rows([src[n] for n in small]) for src in (weights, gs, mom1, mom2))
    outs = adamw(wpk, [gpk], mpk, vpk, name="adamw_small")
    for dst, packed in zip((grads, deltas, new_m, new_v), outs):
        for n, val in zip(small, _unpack_rows(packed, shapes)):
            dst[n] = val

    return (loss, grad_x[None], *[grads[n] for n in order], *[deltas[n] for n in order],
            *[new_m[n] for n in order], *[new_v[n] for n in order])


def _local_grads(x0, target, wf, shards=None, axes=None):
    a_norm, a_conv_b, a_b_r, a_b_i, a_lambda = (wf[n] for n in ("a_norm", "a_conv_b", "a_b_r", "a_b_i", "a_lambda"))
    kv_norm, final_norm = wf["kv_norm"], wf["final_norm"]
    wf = dict(wf)
    parts = {}

    def mm(*args, gather=(), scatter=None, **kw):
        if shards is None or not (gather or scatter):
            return matmul(*args, **kw)
        if gather:
            out, got = matmul(*args, exchange=("gather_halves", [shards[n] for n in gather], [axes[n] for n in gather]),
                              **kw)
            wf.update(zip(gather, got))
        else:
            out, got = matmul(*args, exchange=("scatter", list(scatter.values()), [axes[n] for n in scatter]), **kw)
            parts.update(zip(scatter, got))
        return out

    (h_a,) = rms_fwd(x0, [a_norm], name="norm_a")
    proj_a = mm(h_a, wf["a_w_in"], gather=("a_w_out",), name="a_in")
    m_a, hst = lru_fwd(proj_a, wf["a_conv_w"], a_conv_b, wf["a_w_r"], a_b_r, wf["a_w_i"], a_b_i, a_lambda,
                       name="lru_fwd")
    x1 = mm(m_a, wf["a_w_out"], residual=x0, gather=("w_kv",), name="a_out")
    kvn, hb = rms_fwd(x1, [kv_norm, wf["b_norm"]], name="norm_kv_b")
    kv = mm(kvn, wf["w_kv"], out_dtype=BF16, gather=("b_w_in",), name="kv_proj")
    proj_b = mm(hb, wf["b_w_in"], gather=("b_w_out",), name="b_in")
    m_b, o, saved = attn_fwd(proj_b, kv, name="attn_fwd")
    x2 = mm(m_b, wf["b_w_out"], residual=x1, name="b_out")
    loss_part, g_final, dx2, dx2b = loss_bwd(x2, target, final_norm, name="loss_bwd")

    dm_b = mm(dx2b, wf["b_w_out"], tb=True, name="b_out_dx")
    g_b_w_out = mm(m_b, dx2b, ta=True, out_dtype=BF16, name="b_out_dw")
    dq, dgate_b, dk, dv = attn_bwd(proj_b, dm_b, o, kv, saved, name="attn_bwd")
    dproj_b = (dq, dgate_b)
    dkv = (dk, dv)
    g_b_w_in = mm(hb, dproj_b, ta=True, out_dtype=BF16, scatter=dict(b_w_out=g_b_w_out), name="b_in_dw")
    g_w_kv = mm(kvn, dkv, ta=True, out_dtype=BF16, scatter=dict(b_w_in=g_b_w_in), name="kv_dw")
    dhb = mm(dproj_b, wf["b_w_in"], tb=True, scatter=dict(w_kv=g_w_kv), name="b_in_dx")
    dkvn = mm(dkv, wf["w_kv"], tb=True, name="kv_dx")
    dx1, dx1b, (g_kv_norm, g_b_norm) = rms_bwd(
        x1, dx2, [(kv_norm, dkvn), (wf["b_norm"], dhb)], name="norm_kv_b_bwd")

    g_a_w_out = mm(m_a, dx1b, ta=True, out_dtype=BF16, name="a_out_dw")
    dm_a = mm(dx1b, wf["a_w_out"], tb=True, scatter=dict(a_w_out=g_a_w_out), name="a_out_dx")
    dxpre, dgate_a, g_conv_w, g_conv_b, g_w_r, g_b_r, g_w_i, g_b_i, g_lambda = lru_bwd(
        proj_a, hst, dm_a, wf["a_conv_w"], a_conv_b, wf["a_w_r"], a_b_r, wf["a_w_i"], a_b_i, a_lambda,
        name="lru_bwd")
    dproj_a = (dxpre, dgate_a)
    g_w_r, g_w_i = g_w_r.astype(BF16), g_w_i.astype(BF16)
    g_a_w_in = mm(h_a, dproj_a, ta=True, out_dtype=BF16, scatter=dict(a_w_r=g_w_r, a_w_i=g_w_i), name="a_in_dw")
    dh_a = mm(dproj_a, wf["a_w_in"], tb=True, scatter=dict(a_w_in=g_a_w_in), name="a_in_dx")
    grad_x, _, (g_a_norm,) = rms_bwd(x0, dx1, [(a_norm, dh_a)], name="norm_a_bwd")

    gbig = parts if shards is not None else dict(
        a_w_in=g_a_w_in, a_w_r=g_w_r, a_w_i=g_w_i, a_w_out=g_a_w_out, w_kv=g_w_kv, b_w_in=g_b_w_in, b_w_out=g_b_w_out)
    gsmall = dict(a_norm=g_a_norm, a_conv_b=g_conv_b, a_b_r=g_b_r, a_b_i=g_b_i, a_lambda=g_lambda,
                  kv_norm=g_kv_norm, final_norm=g_final, a_conv_w=g_conv_w, b_norm=g_b_norm)
    return loss_part, grad_x, gbig, gsmall
```

```python
import math

import jax
import jax.numpy as jnp
from jax import lax
from jax.experimental import pallas as pl
from jax.experimental.pallas import tpu as pltpu

F32 = jnp.float32
BF16 = jnp.bfloat16
MESH = pl.DeviceIdType.MESH

EPS = 1e-6
LRU_C = 8.0
CONV_W = 4
HEAD_DIM = 128
ADAM_LR = 0.001
ADAM_B1 = 0.9
ADAM_B2 = 0.999
ADAM_EPS = 1e-08
ADAM_WD = 0.01
ADAM_STEP = 10

V7X_VMEM_LIMIT = 56 * 1024 * 1024
V7X_VMEM_LIMIT_HIGH = 60 * 1024 * 1024
LANES = 128
SUBLANES = 8
ATT_BLOCK = 256
ATT_QTILES = 4
ATT_HEADS = 2
ATT_FETCH_AHEAD = 2
LOG2E = 1.4426950408889634
SCAN_UNROLL = 4


def _pick(dim, cands):
    for c in cands:
        if dim % c == 0:
            return c
    return dim


def _params(sem, vmem=V7X_VMEM_LIMIT):
    return pltpu.CompilerParams(dimension_semantics=sem, vmem_limit_bytes=vmem)


def _sigmoid(x):
    return 1.0 / (1.0 + jnp.exp(-x))


def _place():
    return lax.axis_index("x"), lax.axis_index("y"), lax.axis_index("c")


def _chip_peers(x, y, c):
    return [(1 - x, y, c), (x, 1 - y, c), (1 - x, 1 - y, c)]


def _shard_of(ref, axis, idx, n):
    start = pl.multiple_of(idx * n, n)
    sl = [slice(None)] * len(ref.shape)
    sl[axis] = pl.ds(start, n)
    return ref.at[tuple(sl)]


def _half_rows(ref, h):
    n = ref.shape[0] // 2
    return ref.at[pl.ds(pl.multiple_of(h * n, n), n)]


def _block_half(ref, axis, idx, n, h):
    if axis == 0:
        return ref.at[pl.ds(pl.multiple_of(idx * n + h * (n // 2), n // 2), n // 2)]
    return _half_rows(_shard_of(ref, axis, idx, n), h)


def _exchange(kind, in_refs, out_refs, axes, send_sems, recv_sems, local_sems, send2_sems=None, recv2_sems=None):
    x, y, c = _place()
    me = 2 * x + y
    peers = _chip_peers(x, y, c)
    sibling = (x, y, 1 - c)
    triples = []
    nothing = lambda: None
    for ai, (src, dst, ax) in enumerate(zip(in_refs, out_refs, axes)):
        if kind == "scatter":
            n = dst.shape[1 + ax]
            loc = pltpu.make_async_copy(_shard_of(src, ax, me, n), dst.at[0], local_sems.at[ai])
        else:
            n = src.shape[ax]
            mine = _shard_of(dst, ax, me, n)
            loc = pltpu.make_async_copy(src, mine, local_sems.at[ai])
        triples.append((loc.start, nothing, loc.wait))
        for k, peer in enumerate(peers):
            sem = dict(send_sem=send_sems.at[ai * 3 + k], recv_sem=recv_sems.at[ai * 3 + k],
                       device_id=peer, device_id_type=MESH)
            theirs = 2 * peer[0] + peer[1]
            if kind == "gather":
                snd = pltpu.make_async_remote_copy(src_ref=src, dst_ref=mine, **sem)
                rcv = pltpu.make_async_remote_copy(src_ref=src, dst_ref=_shard_of(dst, ax, theirs, n), **sem)
                triples.append((snd.start, nothing, lambda snd=snd, rcv=rcv: (snd.wait_send(), rcv.wait_recv())))
            elif kind == "scatter":
                snd = pltpu.make_async_remote_copy(src_ref=_shard_of(src, ax, theirs, n), dst_ref=dst.at[1 + k], **sem)
                triples.append((snd.start, nothing, snd.wait))
            else:
                landed = _block_half(dst, ax, theirs, n, c)
                snd = pltpu.make_async_remote_copy(src_ref=_half_rows(src, c), dst_ref=_block_half(dst, ax, me, n, c), **sem)
                rcv = pltpu.make_async_remote_copy(src_ref=_half_rows(src, c), dst_ref=landed, **sem)
                sem2 = dict(send_sem=send2_sems.at[ai * 3 + k], recv_sem=recv2_sems.at[ai * 3 + k],
                            device_id=sibling, device_id_type=MESH)
                fwd = pltpu.make_async_remote_copy(src_ref=landed, dst_ref=landed, **sem2)
                got = pltpu.make_async_remote_copy(src_ref=landed, dst_ref=_block_half(dst, ax, theirs, n, 1 - c), **sem2)
                triples.append((snd.start, lambda rcv=rcv, fwd=fwd: (rcv.wait_recv(), fwd.start()),
                                lambda snd=snd, fwd=fwd, got=got: (snd.wait_send(), fwd.wait_send(), got.wait_recv())))
    return triples


def _exchange_shapes(kind, arrays, axes):
    out = []
    for arr, ax in zip(arrays, axes):
        shp = list(arr.shape)
        if kind == "scatter":
            shp[ax] //= 4
            out.append(jax.ShapeDtypeStruct((4, *shp), arr.dtype))
        else:
            shp[ax] *= 4
            out.append(jax.ShapeDtypeStruct(tuple(shp), arr.dtype))
    return out


def _exchange_sems(kind, n):
    sems = [pltpu.SemaphoreType.DMA((3 * n,)), pltpu.SemaphoreType.DMA((3 * n,)), pltpu.SemaphoreType.DMA((n,))]
    if kind == "gather_halves":
        sems += [pltpu.SemaphoreType.DMA((3 * n,)), pltpu.SemaphoreType.DMA((3 * n,))]
    return sems


def matmul(a, b, *, ta=False, tb=False, out_dtype=F32, residual=None, exchange=None, name):
    a_pair = a if isinstance(a, (tuple, list)) else None
    b_pair = b if isinstance(b, (tuple, list)) else None
    assert not (a_pair and ta) and not (b_pair and tb) and not (a_pair and b_pair)
    a0 = a_pair[0] if a_pair else a
    b0 = b_pair[0] if b_pair else b
    m = a0.shape[1] if ta else a0.shape[0]
    kdim = (a0.shape[0] if ta else a0.shape[1]) * (2 if a_pair else 1)
    n = (b0.shape[0] if tb else b0.shape[1]) * (2 if b_pair else 1)
    assert (b0.shape[1] if tb else b0.shape[0]) == kdim
    tm = _pick(m, (1024, 640, 512, 256, 128))
    tn = _pick(n // 2 if b_pair else n, (1024, 1280, 640, 512, 256, 128))
    tk = _pick(kdim // 2 if a_pair else kdim, (2560, 2048, 1024, 512, 256, 128))
    grid = (m // tm, n // tn, kdim // tk)
    nk = grid[2]
    kh, jh = nk // 2, grid[1] // 2
    dn = (((0 if ta else 1,), (1 if tb else 0,)), ((), ()))
    na = 2 if a_pair else 1
    nb = 2 if b_pair else 1
    nres = 0 if residual is None else 1
    nex = 0 if exchange is None else len(exchange[1])

    def body(*refs):
        a_refs, b_refs = refs[:na], refs[na:na + nb]
        p = na + nb
        r_ref = refs[p] if nres else None
        ex_in = refs[p + nres:p + nres + nex]
        o_ref = refs[p + nres + nex]
        ex_out = refs[p + 1 + nres + nex:p + 1 + nres + 2 * nex]
        acc = refs[p + 1 + nres + 2 * nex]
        sems = refs[p + 2 + nres + 2 * nex:]
        i, j, k = pl.program_id(0), pl.program_id(1), pl.program_id(2)
        if nex:
            @pl.when((i == 0) & (j == 0) & (k == 0))
            def _():
                for start, _, _ in _exchange(exchange[0], ex_in, ex_out, exchange[2], *sems):
                    start()

        @pl.when(k == 0)
        def _():
            acc[...] = jnp.zeros_like(acc)

        def accumulate(a_ref, b_ref):
            acc[...] += lax.dot_general(a_ref[...].astype(BF16), b_ref[...].astype(BF16), dn,
                                        preferred_element_type=F32)

        if a_pair:
            pl.when(k < kh)(lambda: accumulate(a_refs[0], b_refs[0]))
            pl.when(k >= kh)(lambda: accumulate(a_refs[1], b_refs[0]))
        elif b_pair:
            pl.when(j < jh)(lambda: accumulate(a_refs[0], b_refs[0]))
            pl.when(j >= jh)(lambda: accumulate(a_refs[0], b_refs[1]))
        else:
            accumulate(a_refs[0], b_refs[0])

        @pl.when(k == nk - 1)
        def _():
            r = acc[...]
            if r_ref is not None:
                r = r + r_ref[...]
            o_ref[...] = r.astype(out_dtype)

        if nex:
            @pl.when((i == grid[0] - 1) & (j == grid[1] - 1) & (k == nk - 1))
            def _():
                triples = _exchange(exchange[0], ex_in, ex_out, exchange[2], *sems)
                for _, relay, _ in triples:
                    relay()
                for _, _, finish in triples:
                    finish()

    if a_pair:
        a_specs = [pl.BlockSpec((tm, tk), lambda i, j, k: (i, jnp.minimum(k, kh - 1))),
                   pl.BlockSpec((tm, tk), lambda i, j, k: (i, jnp.maximum(k - kh, 0)))]
    else:
        a_specs = [pl.BlockSpec((tk, tm), lambda i, j, k: (k, i)) if ta
                   else pl.BlockSpec((tm, tk), lambda i, j, k: (i, k))]
    if b_pair:
        b_specs = [pl.BlockSpec((tk, tn), lambda i, j, k: (jnp.where(j < jh, k, nk - 1), jnp.minimum(j, jh - 1))),
                   pl.BlockSpec((tk, tn), lambda i, j, k: (jnp.where(j >= jh, k, 0), jnp.maximum(j - jh, 0)))]
    else:
        b_specs = [pl.BlockSpec((tn, tk), lambda i, j, k: (j, k)) if tb
                   else pl.BlockSpec((tk, tn), lambda i, j, k: (k, j))]
    o_spec = pl.BlockSpec((tm, tn), lambda i, j, k: (i, j))
    hbm = pl.BlockSpec(memory_space=pl.ANY)
    in_specs = a_specs + b_specs + [o_spec] * nres + [hbm] * nex
    args = (list(a_pair) if a_pair else [a]) + (list(b_pair) if b_pair else [b])
    args += ([residual] if nres else []) + (list(exchange[1]) if nex else [])
    out_shape = [jax.ShapeDtypeStruct((m, n), out_dtype)]
    scratch = [pltpu.VMEM((tm, tn), F32)]
    if nex:
        out_shape += _exchange_shapes(*exchange)
        scratch += _exchange_sems(exchange[0], nex)
    outs = pl.pallas_call(
        body, name=name, grid=grid,
        in_specs=in_specs, out_specs=[o_spec] + [hbm] * nex, out_shape=out_shape,
        scratch_shapes=scratch,
        compiler_params=_params(("arbitrary",) * 3 if nex else ("parallel", "parallel", "arbitrary")),
    )(*args)
    return (outs[0], list(outs[1:])) if nex else outs[0]


def rms_fwd(x, gains, *, name):
    s, d = x.shape
    tr = _pick(s, (512, 256, 128, 8))
    ng = len(gains)

    def body(*refs):
        x_ref = refs[0]
        g_refs = refs[1:1 + ng]
        o_refs = refs[1 + ng:]
        xv = x_ref[...]
        y = xv * lax.rsqrt(jnp.mean(xv * xv, axis=-1, keepdims=True) + EPS)
        for g_ref, o_ref in zip(g_refs, o_refs):
            o_ref[...] = (y * g_ref[...]).astype(BF16)

    row = pl.BlockSpec((tr, d), lambda i: (i, 0))
    vec = pl.BlockSpec((1, d), lambda i: (0, 0))
    return pl.pallas_call(
        body, name=name, grid=(s // tr,),
        in_specs=[row] + [vec] * ng, out_specs=[row] * ng,
        out_shape=[jax.ShapeDtypeStruct((s, d), BF16)] * ng,
        compiler_params=_params(("parallel",)),
    )(x, *gains)


def rms_bwd(x, dres, norms, *, name):
    s, d = x.shape
    tr = _pick(s, (256, 128, 8))
    ng = len(norms)

    def body(*refs):
        x_ref, dres_ref = refs[0], refs[1]
        g_refs = refs[2:2 + ng]
        dh_refs = refs[2 + ng:2 + 2 * ng]
        dx_ref, dxb_ref = refs[2 + 2 * ng], refs[3 + 2 * ng]
        dg_refs = refs[4 + 2 * ng:]
        i = pl.program_id(0)
        xv = x_ref[...]
        r = lax.rsqrt(jnp.mean(xv * xv, axis=-1, keepdims=True) + EPS)
        xhat = xv * r
        dx = dres_ref[...]
        for g_ref, dh_ref, dg_ref in zip(g_refs, dh_refs, dg_refs):
            dh = dh_ref[...]
            part = jnp.sum(dh * xhat, axis=0, keepdims=True)

            @pl.when(i == 0)
            def _():
                dg_ref[...] = part

            @pl.when(i > 0)
            def _():
                dg_ref[...] += part

            dxhat = dh * g_ref[...]
            dx = dx + r * (dxhat - xhat * jnp.mean(dxhat * xhat, axis=-1, keepdims=True))
        dx_ref[...] = dx
        dxb_ref[...] = dx.astype(BF16)

    row = pl.BlockSpec((tr, d), lambda i: (i, 0))
    vec = pl.BlockSpec((1, d), lambda i: (0, 0))
    outs = pl.pallas_call(
        body, name=name, grid=(s // tr,),
        in_specs=[row, row] + [vec] * ng + [row] * ng,
        out_specs=[row, row] + [vec] * ng,
        out_shape=[jax.ShapeDtypeStruct((s, d), F32), jax.ShapeDtypeStruct((s, d), BF16)]
        + [jax.ShapeDtypeStruct((1, d), F32)] * ng,
        compiler_params=_params(("arbitrary",)),
    )(x, dres, *[g for g, _ in norms], *[dh for _, dh in norms])
    return outs[0], outs[1], list(outs[2:])


def loss_bwd(x2, target, gain, *, name):
    s, d = x2.shape
    tr = _pick(s, (256, 128, 8))
    nsteps = s // tr

    def body(x_ref, t_ref, g_ref, loss_ref, dg_ref, dx_ref, dxb_ref, sq_acc):
        i = pl.program_id(0)
        xv = x_ref[...]
        r = lax.rsqrt(jnp.mean(xv * xv, axis=-1, keepdims=True) + EPS)
        xhat = xv * r
        g = g_ref[...]
        err = xhat * g - t_ref[...]
        dy = err * (1.0 / d)
        sq = jnp.sum(err * err, axis=0, keepdims=True)
        dgp = jnp.sum(dy * xhat, axis=0, keepdims=True)

        @pl.when(i == 0)
        def _():
            sq_acc[...] = sq
            dg_ref[...] = dgp

        @pl.when(i > 0)
        def _():
            sq_acc[...] += sq
            dg_ref[...] += dgp

        dxhat = dy * g
        dx = r * (dxhat - xhat * jnp.mean(dxhat * xhat, axis=-1, keepdims=True))
        dx_ref[...] = dx
        dxb_ref[...] = dx.astype(BF16)

        @pl.when(i == nsteps - 1)
        def _():
            tot = jnp.sum(sq_acc[...], axis=-1, keepdims=True) * (0.5 / d)
            loss_ref[...] = jnp.broadcast_to(tot, (1, LANES))

    row = pl.BlockSpec((tr, d), lambda i: (i, 0))
    vec = pl.BlockSpec((1, d), lambda i: (0, 0))
    return pl.pallas_call(
        body, name=name, grid=(nsteps,),
        in_specs=[row, row, vec],
        out_specs=[pl.BlockSpec((1, LANES), lambda i: (0, 0)), vec, row, row],
        out_shape=[jax.ShapeDtypeStruct((1, LANES), F32), jax.ShapeDtypeStruct((1, d), F32),
                   jax.ShapeDtypeStruct((s, d), F32), jax.ShapeDtypeStruct((s, d), BF16)],
        scratch_shapes=[pltpu.VMEM((1, d), F32)],
        compiler_params=_params(("arbitrary",)),
    )(x2, target, gain)


def _lru_gates(xb, wr, wi, br, bi, sp):
    xbb = xb.astype(BF16)
    r = _sigmoid(jnp.dot(xbb, wr, preferred_element_type=F32) + br)
    ig = _sigmoid(jnp.dot(xbb, wi, preferred_element_type=F32) + bi)
    log_a = (-LRU_C) * r * sp
    a = jnp.exp(log_a)
    mult = jnp.sqrt(jnp.maximum(-jnp.tanh(log_a) * (a * a + 1.0), 0.0))
    return r, ig, a, mult


def _softplus_neg(lam):
    e = jnp.exp(-jnp.abs(lam))
    sp = jnp.maximum(-lam, 0.0) + jnp.log(1.0 + e)
    sg = jnp.where(lam >= 0, e, 1.0) / (1.0 + e)
    return sp, sg


def _conv(pad_ref, w, b, t):
    acc = b + w[CONV_W - 1:CONV_W, :] * pad_ref[pl.ds(SUBLANES, t), :]
    for dlt in range(1, CONV_W):
        acc = acc + w[CONV_W - 1 - dlt:CONV_W - dlt, :] * pad_ref[pl.ds(SUBLANES - dlt, t), :]
    return acc


def _lru_specs(t, bw, nb, time_of):
    blk = lambda c0: pl.BlockSpec((t, bw), lambda n, i, c0=c0: (time_of(i), c0 + n))
    vec = pl.BlockSpec((1, bw), lambda n, i: (0, n))
    wspec = pl.BlockSpec((None, bw, bw), lambda n, i: (n, 0, 0))
    cwspec = pl.BlockSpec((CONV_W, bw), lambda n, i: (0, n))
    return blk, vec, wspec, cwspec


def lru_fwd(proj, conv_w, conv_b, w_r, b_r, w_i, b_i, lam, *, name):
    s, r2 = proj.shape
    rr = r2 // 2
    nb, bw, _ = w_r.shape
    t = _pick(s, (512, 256, 128, 64, 32))
    ngroups = t // SUBLANES

    def body(xp_ref, gate_ref, cw_ref, cb_ref, wr_ref, br_ref, wi_ref, bi_ref, lam_ref,
             m_ref, h_ref, xb_ref, r_ref, i_ref, a_ref, mult_ref, pad, hcarry, u_scr):
        i = pl.program_id(1)

        @pl.when(i == 0)
        def _():
            pad[0:SUBLANES, :] = jnp.zeros((SUBLANES, bw), F32)
            hcarry[...] = jnp.zeros_like(hcarry)

        xpre = xp_ref[...]
        pad[pl.ds(SUBLANES, t), :] = xpre
        xb = _conv(pad, cw_ref[...], cb_ref[...], t)
        pad[0:SUBLANES, :] = xpre[t - SUBLANES:, :]
        sp, _ = _softplus_neg(lam_ref[...])
        r, ig, a, mult = _lru_gates(xb, wr_ref[...], wi_ref[...], br_ref[...], bi_ref[...], sp)
        xb_ref[...] = xb
        r_ref[...] = r
        i_ref[...] = ig
        a_ref[...] = a
        mult_ref[...] = mult
        u_scr[...] = mult * (ig * xb)
        row = lax.broadcasted_iota(jnp.int32, (SUBLANES, bw), 0)

        def groups(gi, hprev):
            offs = [pl.multiple_of((gi * SCAN_UNROLL + u) * SUBLANES, SUBLANES) for u in range(SCAN_UNROLL)]
            scanned = []
            for off in offs:
                av = a_ref[pl.ds(off, SUBLANES), :]
                uv = u_scr[pl.ds(off, SUBLANES), :]
                for dlt in (1, 2, 4):
                    keep = row >= dlt
                    uv = jnp.where(keep, av * pltpu.roll(uv, dlt, 0) + uv, uv)
                    av = jnp.where(keep, av * pltpu.roll(av, dlt, 0), av)
                scanned.append((av, uv))
            for off, (av, uv) in zip(offs, scanned):
                hv = av * hprev + uv
                h_ref[pl.ds(off, SUBLANES), :] = hv
                hprev = hv[SUBLANES - 1:SUBLANES, :]
            return hprev

        hcarry[...] = lax.fori_loop(0, ngroups // SCAN_UNROLL, groups, hcarry[...])
        gate = gate_ref[...]
        m_ref[...] = (h_ref[...] * (gate * _sigmoid(gate))).astype(BF16)

    blk, vec, wspec, cwspec = _lru_specs(t, bw, nb, lambda i: i)
    return pl.pallas_call(
        body, name=name, grid=(nb, s // t),
        in_specs=[blk(0), blk(nb), cwspec, vec, wspec, vec, wspec, vec, vec],
        out_specs=[blk(0)] * 7,
        out_shape=[jax.ShapeDtypeStruct((s, rr), BF16)] + [jax.ShapeDtypeStruct((s, rr), F32)] * 6,
        scratch_shapes=[pltpu.VMEM((t + SUBLANES, bw), F32), pltpu.VMEM((1, bw), F32), pltpu.VMEM((t, bw), F32)],
        compiler_params=_params(("parallel", "arbitrary")),
    )(proj, proj, conv_w, conv_b, w_r, b_r, w_i, b_i, lam)


def lru_bwd(proj, hst, saved, dm, conv_w, w_r, w_i, lam, *, name):
    s, r2 = proj.shape
    rr = r2 // 2
    nb, bw, _ = w_r.shape
    t = _pick(s, (512, 256, 128, 64, 32))
    nt = s // t
    ngroups = t // SUBLANES
    nt_dims = (((1,), (1,)), ((), ()))
    tn_dims = (((0,), (0,)), ((), ()))

    def body(xp_ref, xhalo_ref, gate_ref, h_ref, hhalo_ref, xb_ref, r_ref, i_ref, a_ref, mult_ref, dm_ref,
             cw_ref, wr_ref, wi_ref, lam_ref,
             dxp_ref, dgate_ref, dcw_ref, dcb_ref, dwr_ref, dbr_ref, dwi_ref, dbi_ref, dlam_ref,
             pad, hpad, dpad, ecarry, b_scr, d_scr):
        step = pl.program_id(1)

        @pl.when(step == 0)
        def _():
            dpad[pl.ds(t, SUBLANES), :] = jnp.zeros((SUBLANES, bw), F32)
            ecarry[...] = jnp.zeros_like(ecarry)
            dcw_ref[...] = jnp.zeros_like(dcw_ref)
            dcb_ref[...] = jnp.zeros_like(dcb_ref)
            dwr_ref[...] = jnp.zeros_like(dwr_ref)
            dbr_ref[...] = jnp.zeros_like(dbr_ref)
            dwi_ref[...] = jnp.zeros_like(dwi_ref)
            dbi_ref[...] = jnp.zeros_like(dbi_ref)
            dlam_ref[...] = jnp.zeros_like(dlam_ref)

        past = jnp.where(step == nt - 1, 0.0, 1.0)
        pad[0:SUBLANES, :] = xhalo_ref[...] * past
        pad[pl.ds(SUBLANES, t), :] = xp_ref[...]
        hpad[0:SUBLANES, :] = hhalo_ref[...] * past
        hpad[pl.ds(SUBLANES, t), :] = h_ref[...]
        cw = cw_ref[...]
        sp, sg = _softplus_neg(lam_ref[...])
        wr = wr_ref[...]
        wi = wi_ref[...]
        xb, r, ig, a, mult = xb_ref[...], r_ref[...], i_ref[...], a_ref[...], mult_ref[...]
        gate = gate_ref[...]
        sgate = _sigmoid(gate)
        dmv = dm_ref[...]
        dgate_ref[...] = (dmv * h_ref[...] * (sgate * (1.0 + gate * (1.0 - sgate)))).astype(BF16)
        dy = dmv * (gate * sgate)
        b_scr[...] = a * dy
        row = lax.broadcasted_iota(jnp.int32, (SUBLANES, bw), 0)

        def groups(gi, enext):
            offs = [pl.multiple_of((ngroups - 1 - gi * SCAN_UNROLL - u) * SUBLANES, SUBLANES)
                    for u in range(SCAN_UNROLL)]
            scanned = []
            for off in offs:
                av = a_ref[pl.ds(off, SUBLANES), :]
                bv = b_scr[pl.ds(off, SUBLANES), :]
                for dlt in (1, 2, 4):
                    keep = row < SUBLANES - dlt
                    bv = jnp.where(keep, av * pltpu.roll(bv, SUBLANES - dlt, 0) + bv, bv)
                    av = jnp.where(keep, av * pltpu.roll(av, SUBLANES - dlt, 0), av)
                scanned.append((av, bv))
            for off, (av, bv) in zip(offs, scanned):
                ev = av * enext + bv
                d_scr[pl.ds(off, SUBLANES), :] = jnp.where(row == SUBLANES - 1, enext,
                                                           pltpu.roll(ev, SUBLANES - 1, 0))
                enext = ev[0:1, :]
            return enext

        ecarry[...] = lax.fori_loop(0, ngroups // SCAN_UNROLL, groups, ecarry[...])
        dtot = dy + d_scr[...]
        da = dtot * hpad[pl.ds(SUBLANES - 1, t), :]
        dmult = dtot * (ig * xb)
        dlog_a = da * a - dmult * (a * a) / mult
        dr_pre = dlog_a * ((-LRU_C) * sp) * (r * (1.0 - r))
        di_pre = (dtot * mult * xb) * (ig * (1.0 - ig))
        dlam_ref[...] += jnp.sum(dlog_a * r, axis=0, keepdims=True) * (LRU_C * sg)
        dbr_ref[...] += jnp.sum(dr_pre, axis=0, keepdims=True)
        dbi_ref[...] += jnp.sum(di_pre, axis=0, keepdims=True)
        drb = dr_pre.astype(BF16)
        dib = di_pre.astype(BF16)
        xbb = xb.astype(BF16)
        dxb = (dtot * mult * ig
               + lax.dot_general(drb, wr, nt_dims, preferred_element_type=F32)
               + lax.dot_general(dib, wi, nt_dims, preferred_element_type=F32))
        dwr_ref[...] += lax.dot_general(xbb, drb, tn_dims, preferred_element_type=F32)
        dwi_ref[...] += lax.dot_general(xbb, dib, tn_dims, preferred_element_type=F32)
        dcb_ref[...] += jnp.sum(dxb, axis=0, keepdims=True)
        dpad[pl.ds(0, t), :] = dxb
        dxpre = cw[CONV_W - 1:CONV_W, :] * dxb
        dcw_ref[CONV_W - 1:CONV_W, :] += jnp.sum(dxb * pad[pl.ds(SUBLANES, t), :], axis=0, keepdims=True)
        for dlt in range(1, CONV_W):
            dxpre = dxpre + cw[CONV_W - 1 - dlt:CONV_W - dlt, :] * dpad[pl.ds(dlt, t), :]
            dcw_ref[CONV_W - 1 - dlt:CONV_W - dlt, :] += jnp.sum(
                dxb * pad[pl.ds(SUBLANES - dlt, t), :], axis=0, keepdims=True)
        dpad[pl.ds(t, SUBLANES), :] = dxb[0:SUBLANES, :]
        dxp_ref[...] = dxpre.astype(BF16)

    rev = lambda i: nt - 1 - i
    blk, vec, wspec, cwspec = _lru_specs(t, bw, nb, rev)
    halo = pl.BlockSpec((SUBLANES, bw), lambda n, i: (jnp.maximum(rev(i) * ngroups - 1, 0), n))
    return pl.pallas_call(
        body, name=name, grid=(nb, nt),
        in_specs=[blk(0), halo, blk(nb), blk(0), halo] + [blk(0)] * 6 + [cwspec, wspec, wspec, vec],
        out_specs=[blk(0), blk(0), cwspec, vec, wspec, vec, wspec, vec, vec],
        out_shape=[jax.ShapeDtypeStruct((s, rr), BF16), jax.ShapeDtypeStruct((s, rr), BF16),
                   jax.ShapeDtypeStruct((CONV_W, rr), F32), jax.ShapeDtypeStruct((1, rr), F32),
                   jax.ShapeDtypeStruct((nb, bw, bw), F32), jax.ShapeDtypeStruct((1, rr), F32),
                   jax.ShapeDtypeStruct((nb, bw, bw), F32), jax.ShapeDtypeStruct((1, rr), F32),
                   jax.ShapeDtypeStruct((1, rr), F32)],
        scratch_shapes=[pltpu.VMEM((t + SUBLANES, bw), F32), pltpu.VMEM((t + SUBLANES, bw), F32),
                        pltpu.VMEM((t + SUBLANES, bw), F32), pltpu.VMEM((1, bw), F32),
                        pltpu.VMEM((t, bw), F32), pltpu.VMEM((t, bw), F32)],
        compiler_params=_params(("parallel", "arbitrary")),
    )(proj, proj, proj, hst, hst, *saved, dm, conv_w, w_r, w_i, lam)


def _softplus(z):
    return jnp.maximum(z, 0.0) + jnp.log(1.0 + jnp.exp2(jnp.abs(z) * (-LOG2E)))


def _att_blocks(s):
    bk = ATT_BLOCK if s % ATT_BLOCK == 0 else s
    bq = ATT_QTILES * bk if s % (ATT_QTILES * bk) == 0 else bk
    return bk, bq


def _tile_base(i, r):
    return r * ((i * (i + 1)) // 2)


def attn_fwd(projb, kv, *, name):
    s, a2 = projb.shape
    a = a2 // 2
    nh = a // HEAD_DIM
    bk, bq = _att_blocks(s)
    r = bq // bk
    nq = s // bq
    ntiles = _tile_base(nq, r)
    scale = 1.0 / math.sqrt(HEAD_DIM)
    nt_dims = (((1,), (1,)), ((), ()))
    hp = ATT_HEADS if nh % ATT_HEADS == 0 else 1
    wd = hp * HEAD_DIM

    def body(q_ref, g_ref, k_ref, v_ref, m_ref, o_ref, saved_hbm, acc, stage, sems):
        hgrp, i = pl.program_id(0), pl.program_id(1)
        base = _tile_base(i, r)
        qb = (q_ref[...] * scale).astype(BF16)
        from_mat = (lax.broadcasted_iota(jnp.int32, (bk, bk), 0)
                    >= lax.broadcasted_iota(jnp.int32, (bk, bk), 1)).astype(BF16)
        rowi = lax.broadcasted_iota(jnp.int32, (bq, bk), 0)
        coli = lax.broadcasted_iota(jnp.int32, (bq, bk), 1)
        cols = [slice(hh * HEAD_DIM, (hh + 1) * HEAD_DIM) for hh in range(hp)]

        def save(slot, j):
            return pltpu.make_async_copy(stage.at[slot], saved_hbm.at[hgrp, base + j], sems.at[slot])

        def tile(j, n, carries, diag):
            r0 = 0 if diag is None else diag * bk
            live = slice(r0, bq)
            causal = None if diag is None else coli[live] < rowi[:bq - r0]
            slot = n % 2

            def free_slot():
                save(slot, 0).wait()

            if isinstance(n, int):
                if n >= 2:
                    free_slot()
            elif r >= 2:
                free_slot()
            else:
                pl.when(n >= 2)(free_slot)
            rows = pl.ds(pl.multiple_of(j * bk, bk), bk)
            zs = [lax.dot_general(qb[live, c], k_ref[rows, c], nt_dims, preferred_element_type=F32) for c in cols]
            sums, sigs = [], []
            for z in zs:
                sp = _softplus(z)
                sig = jnp.exp(z - sp)
                if causal is not None:
                    sp = jnp.where(causal, sp, 0.0)
                    sig = jnp.where(causal, sig, 0.0)
                sums.append(jnp.dot(sp.astype(BF16), from_mat, preferred_element_type=F32))
                sigs.append(sig.astype(BF16))
            out = []
            for hh in range(hp):
                w = jnp.exp(zs[hh] - sums[hh] - carries[hh][live])
                if causal is not None:
                    w = jnp.where(causal, w, 0.0)
                wb = w.astype(BF16)
                acc[live, cols[hh]] += jnp.dot(wb, v_ref[rows, cols[hh]], preferred_element_type=F32)
                stage[slot, 0, hh, live] = wb
                stage[slot, 1, hh, live] = sigs[hh]
                if r0:
                    stage[slot, :, hh, :r0] = jnp.zeros((2, r0, bk), BF16)
                grown = carries[hh][live] + sums[hh][:, 0:1]
                out.append(jnp.concatenate([carries[hh][:r0], grown], axis=0) if r0 else grown)
            save(slot, j).start()
            return tuple(out)

        acc[...] = jnp.zeros_like(acc)
        carries = tuple(jnp.zeros((bq, 1), F32) for _ in range(hp))
        for n, dg in enumerate(reversed(range(r))):
            carries = tile(r * i + dg, n, carries, dg)
        lax.fori_loop(0, r * i, lambda jj, c: tile(r * i - 1 - jj, r + jj, c, None), carries)
        ntile = r * (i + 1)
        for back in (1, 2):
            def drain(back=back):
                save((ntile - back) % 2, 0).wait()
            if r >= back:
                drain()
            else:
                pl.when(ntile >= back)(drain)
        o = acc[...]
        o_ref[...] = o
        gate = g_ref[...]
        m_ref[...] = (o * (gate * _sigmoid(gate))).astype(BF16)

    ng = nh // hp
    qspec = lambda c0: pl.BlockSpec((bq, wd), lambda h, i, c0=c0: (i, c0 + h))
    kspec = lambda c0: pl.BlockSpec((s, wd), lambda h, i, c0=c0: (0, c0 + h), pipeline_mode=pl.Buffered(1))
    hbm = pl.BlockSpec(memory_space=pl.ANY)
    saved = jax.ShapeDtypeStruct((ng, ntiles, 2, hp, bq, bk), BF16)
    return pl.pallas_call(
        body, name=name, grid=(ng, nq),
        in_specs=[qspec(0), qspec(ng), kspec(0), kspec(ng)],
        out_specs=[qspec(0), qspec(0), hbm],
        out_shape=[jax.ShapeDtypeStruct((s, a), BF16), jax.ShapeDtypeStruct((s, a), F32), saved],
        scratch_shapes=[pltpu.VMEM((bq, wd), F32), pltpu.VMEM((2, 2, hp, bq, bk), BF16),
                        pltpu.SemaphoreType.DMA((2,))],
        compiler_params=_params(("arbitrary", "arbitrary")),
    )(projb, projb, kv, kv)


def attn_bwd(projb, dm, o, kv, saved, *, name):
    s, a2 = projb.shape
    a = a2 // 2
    nh = a // HEAD_DIM
    bk, bq = _att_blocks(s)
    r = bq // bk
    nq = s // bq
    scale = 1.0 / math.sqrt(HEAD_DIM)
    nt_dims = (((1,), (1,)), ((), ()))
    tn_dims = (((0,), (0,)), ((), ()))
    hp = saved.shape[3]
    wd = hp * HEAD_DIM
    ahead = ATT_FETCH_AHEAD

    def body(q_ref, g_ref, dm_ref, o_ref, k_ref, v_ref, saved_hbm, dq_ref, dg_ref, dk_ref, dv_ref,
             dk_acc, dv_acc, dq_acc, stage, sems):
        hgrp, i = pl.program_id(0), pl.program_id(1)
        base = _tile_base(i, r)
        ntile = r * (i + 1)

        @pl.when(i == 0)
        def _():
            dk_acc[...] = jnp.zeros_like(dk_acc)
            dv_acc[...] = jnp.zeros_like(dv_acc)

        def fetch(j):
            slot = j % (ahead + 1)
            return pltpu.make_async_copy(saved_hbm.at[hgrp, base + j], stage.at[slot], sems.at[slot])

        for j0 in range(ahead):
            pl.when(j0 < ntile)(lambda j0=j0: fetch(j0).start())
        qb = (q_ref[...] * scale).astype(BF16)
        gate = g_ref[...]
        sgate = _sigmoid(gate)
        dmv = dm_ref[...]
        dob = (dmv * (gate * sgate)).astype(BF16)
        dg_ref[...] = (dmv * o_ref[...] * (sgate * (1.0 + gate * (1.0 - sgate)))).astype(BF16)
        upto_mat = (lax.broadcasted_iota(jnp.int32, (bk, bk), 0)
                    <= lax.broadcasted_iota(jnp.int32, (bk, bk), 1)).astype(BF16)
        cols = [slice(hh * HEAD_DIM, (hh + 1) * HEAD_DIM) for hh in range(hp)]
        dq_acc[...] = jnp.zeros_like(dq_acc)

        def tile(j, gcarries, r0=0, more=None):
            live = slice(r0, bq)
            slot = j % (ahead + 1)
            if more is None:
                pl.when(j + ahead < ntile)(lambda: fetch(j + ahead).start())
            elif more:
                fetch(j + ahead).start()
            fetch(j).wait()
            rows = pl.ds(pl.multiple_of(j * bk, bk), bk)
            dws = [lax.dot_general(dob[live, c], v_ref[rows, c], nt_dims, preferred_element_type=F32) for c in cols]
            gs, totals = [], []
            for hh in range(hp):
                wb = stage[slot, 0, hh, live]
                g = wb.astype(F32) * dws[hh]
                dv_acc[rows, cols[hh]] += lax.dot_general(wb, dob[live, cols[hh]], tn_dims,
                                                          preferred_element_type=F32)
                totals.append(jnp.dot(g.astype(BF16), upto_mat, preferred_element_type=F32))
                gs.append(g)
            out = []
            for hh in range(hp):
                dz = gs[hh] - (totals[hh] + gcarries[hh][live]) * stage[slot, 1, hh, live].astype(F32)
                dzb = dz.astype(BF16)
                dq_acc[live, cols[hh]] += jnp.dot(dzb, k_ref[rows, cols[hh]], preferred_element_type=F32)
                dk_acc[rows, cols[hh]] += lax.dot_general(dzb, qb[live, cols[hh]], tn_dims,
                                                          preferred_element_type=F32)
                grown = gcarries[hh][live] + totals[hh][:, bk - 1:bk]
                out.append(jnp.concatenate([gcarries[hh][:r0], grown], axis=0) if r0 else grown)
            return tuple(out)

        gcarries = lax.fori_loop(0, r * i, tile, tuple(jnp.zeros((bq, 1), F32) for _ in range(hp)))
        for dg in range(r):
            gcarries = tile(r * i + dg, gcarries, dg * bk, dg + ahead < r)
        dq_ref[...] = (dq_acc[...] * scale).astype(BF16)

        @pl.when(i == nq - 1)
        def _():
            dk_ref[...] = dk_acc[...].astype(BF16)
            dv_ref[...] = dv_acc[...].astype(BF16)

    ng = nh // hp
    once = pl.Buffered(1)
    qspec = lambda c0: pl.BlockSpec((bq, wd), lambda h, i, c0=c0: (i, c0 + h))
    kspec = lambda c0: pl.BlockSpec((s, wd), lambda h, i, c0=c0: (0, c0 + h), pipeline_mode=once)
    hbm = pl.BlockSpec(memory_space=pl.ANY)
    return pl.pallas_call(
        body, name=name, grid=(ng, nq),
        in_specs=[qspec(0), qspec(ng), qspec(0), qspec(0), kspec(0), kspec(ng), hbm],
        out_specs=[qspec(0), qspec(0), kspec(0), kspec(0)],
        out_shape=[jax.ShapeDtypeStruct((s, a), BF16)] * 4,
        scratch_shapes=[pltpu.VMEM((s, wd), F32), pltpu.VMEM((s, wd), F32), pltpu.VMEM((bq, wd), F32),
                        pltpu.VMEM((ahead + 1, 2, hp, bq, bk), BF16), pltpu.SemaphoreType.DMA((ahead + 1,))],
        compiler_params=_params(("arbitrary", "arbitrary"), vmem=V7X_VMEM_LIMIT_HIGH),
    )(projb, projb, dm, o, kv, kv, saved)


def _as2d(x):
    n = x.size
    cols = x.shape[-1]
    if cols % LANES != 0:
        cols = LANES
    return x.reshape(n // cols, cols)


def sum_parts(parts, *, name):
    p, rows, cols = parts.shape
    tr = _pick(rows, (512, 256, 128, 64, 32, 16))

    def body(p_ref, o_ref):
        acc = p_ref[0].astype(F32)
        for k in range(1, p):
            acc = acc + p_ref[k].astype(F32)
        o_ref[...] = acc

    return pl.pallas_call(
        body, name=name, grid=(rows // tr,),
        in_specs=[pl.BlockSpec((p, tr, cols), lambda i: (0, i, 0))],
        out_specs=pl.BlockSpec((tr, cols), lambda i: (i, 0)),
        out_shape=jax.ShapeDtypeStruct((rows, cols), F32),
        compiler_params=_params(("parallel",)),
    )(parts)


def adamw(w, g_parts, m, v, *, name):
    rows, cols = w.shape
    tr = _pick(rows, (128, 64, 32, 16, 8))
    np_ = len(g_parts)
    c1 = 1.0 / (1.0 - ADAM_B1 ** ADAM_STEP)
    c2 = 1.0 / (1.0 - ADAM_B2 ** ADAM_STEP)

    def body(*refs):
        w_ref, m_ref, v_ref = refs[0], refs[1], refs[2]
        g_refs = refs[3:3 + np_]
        go_ref, d_ref, mo_ref, vo_ref = refs[3 + np_:]
        g = g_refs[0][...]
        for gr in g_refs[1:]:
            g = g + gr[...]
        mn = ADAM_B1 * m_ref[...] + (1.0 - ADAM_B1) * g
        vn = ADAM_B2 * v_ref[...] + (1.0 - ADAM_B2) * (g * g)
        go_ref[...] = g
        mo_ref[...] = mn
        vo_ref[...] = vn
        d_ref[...] = (-ADAM_LR) * ((mn * c1) / (jnp.sqrt(vn * c2) + ADAM_EPS) + ADAM_WD * w_ref[...])

    spec = pl.BlockSpec((tr, cols), lambda i: (i, 0))
    return pl.pallas_call(
        body, name=name, grid=(rows // tr,),
        in_specs=[spec] * (3 + np_), out_specs=[spec] * 4,
        out_shape=[jax.ShapeDtypeStruct((rows, cols), F32)] * 4,
        compiler_params=_params(("parallel",)),
    )(w, m, v, *g_parts)


def exchange(kind, arrays, axes, *, name):
    na = len(arrays)
    hbm = pl.BlockSpec(memory_space=pl.ANY)

    def body(*refs):
        triples = _exchange(kind, refs[:na], refs[na:2 * na], axes, *refs[2 * na:])
        for step in range(3):
            for triple in triples:
                triple[step]()

    return pl.pallas_call(
        body, name=name, in_specs=[hbm] * na, out_specs=[hbm] * na,
        out_shape=_exchange_shapes(kind, arrays, axes), scratch_shapes=_exchange_sems(kind, na),
    )(*arrays)


def swap_cores(arrs, *, name):
    na = len(arrs)
    hbm = pl.BlockSpec(memory_space=pl.ANY)

    def body(*refs):
        a_refs = refs[:na]
        o_refs = refs[na:2 * na]
        send_sems, recv_sems = refs[2 * na:]
        x, y, c = _place()
        copies = []
        for ai in range(na):
            cp = pltpu.make_async_remote_copy(
                src_ref=a_refs[ai], dst_ref=o_refs[ai], send_sem=send_sems.at[ai], recv_sem=recv_sems.at[ai],
                device_id=(x, y, 1 - c), device_id_type=MESH)
            cp.start()
            copies.append(cp)
        for cp in copies:
            cp.wait()

    return pl.pallas_call(
        body, name=name,
        in_specs=[hbm] * na, out_specs=[hbm] * na,
        out_shape=[jax.ShapeDtypeStruct(a.shape, a.dtype) for a in arrs],
        scratch_shapes=[pltpu.SemaphoreType.DMA((na,)), pltpu.SemaphoreType.DMA((na,))],
    )(*arrs)


def allreduce_small(buf, *, name):
    rows, cols = buf.shape

    def body(b_ref, o_ref, slots, send_sems, recv_sems):
        x, y, c = _place()
        me = 4 * x + 2 * y + c
        slots[0] = b_ref[...]
        copies = []
        for rel in range(1, 8):
            peer = (x ^ (rel >> 2), y ^ ((rel >> 1) & 1), c ^ (rel & 1))
            cp = pltpu.make_async_remote_copy(
                src_ref=b_ref, dst_ref=slots.at[rel], send_sem=send_sems.at[rel - 1],
                recv_sem=recv_sems.at[rel - 1], device_id=peer, device_id_type=MESH)
            cp.start()
            copies.append(cp)
        for cp in copies:
            cp.wait()
        acc = slots[me]
        for dev in range(1, 8):
            acc = acc + slots[dev ^ me]
        o_ref[...] = acc

    vm = pl.BlockSpec(memory_space=pltpu.VMEM)
    return pl.pallas_call(
        body, name=name, in_specs=[vm], out_specs=vm,
        out_shape=jax.ShapeDtypeStruct((rows, cols), F32),
        scratch_shapes=[pltpu.VMEM((8, rows, cols), F32), pltpu.SemaphoreType.DMA((7,)),
                        pltpu.SemaphoreType.DMA((7,))],
    )(buf)


def _pack_rows(arrs):
    parts = []
    for a in arrs:
        p = a.reshape(-1, LANES)
        parts.append(jnp.pad(p, ((0, (-p.shape[0]) % SUBLANES), (0, 0))))
    return jnp.concatenate(parts, axis=0)


def _unpack_rows(buf, shapes):
    out, r0 = [], 0
    for shp in shapes:
        n = math.prod(shp) // LANES
        out.append(buf[r0:r0 + n].reshape(shp))
        r0 += n + (-n) % SUBLANES
    return out


def kernel(x, a_norm, a_w_in, a_conv_w, a_conv_b, a_w_r, a_b_r, a_w_i, a_b_i, a_lambda, a_w_out, kv_norm, w_kv, b_norm, b_w_in, b_w_out, final_norm, loss_target, m_a_norm, m_a_w_in, m_a_conv_w, m_a_conv_b, m_a_w_r, m_a_b_r, m_a_w_i, m_a_b_i, m_a_lambda, m_a_w_out, m_kv_norm, m_w_kv, m_b_norm, m_b_w_in, m_b_w_out, m_final_norm, v_a_norm, v_a_w_in, v_a_conv_w, v_a_conv_b, v_a_w_r, v_a_b_r, v_a_w_i, v_a_b_i, v_a_lambda, v_a_w_out, v_kv_norm, v_w_kv, v_b_norm, v_b_w_in, v_b_w_out, v_final_norm):
    weights = dict(a_norm=a_norm, a_w_in=a_w_in, a_conv_w=a_conv_w, a_conv_b=a_conv_b, a_w_r=a_w_r, a_b_r=a_b_r,
                   a_w_i=a_w_i, a_b_i=a_b_i, a_lambda=a_lambda, a_w_out=a_w_out, kv_norm=kv_norm, w_kv=w_kv,
                   b_norm=b_norm, b_w_in=b_w_in, b_w_out=b_w_out, final_norm=final_norm)
    mom1 = dict(a_norm=m_a_norm, a_w_in=m_a_w_in, a_conv_w=m_a_conv_w, a_conv_b=m_a_conv_b, a_w_r=m_a_w_r,
                a_b_r=m_a_b_r, a_w_i=m_a_w_i, a_b_i=m_a_b_i, a_lambda=m_a_lambda, a_w_out=m_a_w_out,
                kv_norm=m_kv_norm, w_kv=m_w_kv, b_norm=m_b_norm, b_w_in=m_b_w_in, b_w_out=m_b_w_out,
                final_norm=m_final_norm)
    mom2 = dict(a_norm=v_a_norm, a_w_in=v_a_w_in, a_conv_w=v_a_conv_w, a_conv_b=v_a_conv_b, a_w_r=v_a_w_r,
                a_b_r=v_a_b_r, a_w_i=v_a_w_i, a_b_i=v_a_b_i, a_lambda=v_a_lambda, a_w_out=v_a_w_out,
                kv_norm=v_kv_norm, w_kv=v_w_kv, b_norm=v_b_norm, b_w_in=v_b_w_in, b_w_out=v_b_w_out,
                final_norm=v_final_norm)
    order = list(weights)
    x0 = x[0]
    target = loss_target[0]
    d = x0.shape[1]
    chip = 2 * lax.axis_index("x") + lax.axis_index("y")

    big = ["a_w_in", "a_w_r", "a_w_i", "a_w_out", "w_kv", "b_w_in", "b_w_out"]
    big_axis = dict(a_w_in=1, a_w_r=1, a_w_i=1, a_w_out=0, w_kv=1, b_w_in=1, b_w_out=0)
    local = dict(a_w_in=a_w_in[0], a_w_r=a_w_r[0], a_w_i=a_w_i[0], a_w_out=a_w_out[0], w_kv=w_kv,
                 b_w_in=b_w_in[0], b_w_out=b_w_out[0])
    shards = {n: local[n].astype(BF16) for n in big}
    first = ["a_w_in", "a_w_r", "a_w_i"]
    full = exchange("gather_halves", [shards[n] for n in first], [big_axis[n] for n in first], name="gather_first")
    full += exchange("gather", [a_conv_w[0], b_norm], [1, 1], name="gather_small")
    wf = dict(zip(first + ["a_conv_w", "b_norm"], full))
    wf.update(a_norm=a_norm, a_conv_b=a_conv_b, a_b_r=a_b_r, a_b_i=a_b_i, a_lambda=a_lambda,
              kv_norm=kv_norm.reshape(1, d), final_norm=final_norm.reshape(1, d))
    loss_part, grad_x, parts, gsmall = _local_grads(x0, target, wf, shards=shards, axes=big_axis)

    sums = [sum_parts(parts[n].reshape(4, *_as2d(parts[n][0]).shape), name="sum_" + n) for n in big]
    others = swap_cores(sums, name="swap_cores")

    small = ["a_norm", "a_conv_b", "a_b_r", "a_b_i", "a_lambda", "kv_norm", "final_norm", "a_conv_w", "b_norm"]
    buf = _pack_rows([gsmall[n] for n in small] + [loss_part])
    red = allreduce_small(buf, name="allreduce_small")
    red_list = _unpack_rows(red, [gsmall[n].shape for n in small] + [(1, LANES)])
    gs = dict(zip(small, red_list[:-1]))
    loss = red_list[-1][0, 0]
    n_conv = a_conv_w.shape[2]
    gs["a_conv_w"] = lax.dynamic_slice_in_dim(gs["a_conv_w"], chip * n_conv, n_conv, axis=1)
    n_bn = b_norm.shape[1]
    gs["b_norm"] = lax.dynamic_slice_in_dim(gs["b_norm"], chip * n_bn, n_bn, axis=1)

    grads, deltas, new_m, new_v = {}, {}, {}, {}
    for n, s_mine, s_other in zip(big, sums, others):
        shp = weights[n].shape
        g, dlt, mn, vn = adamw(_as2d(weights[n]), [s_mine, s_other], _as2d(mom1[n]), _as2d(mom2[n]),
                               name="adamw_" + n)
        grads[n], deltas[n], new_m[n], new_v[n] = (t.reshape(shp) for t in (g, dlt, mn, vn))
    shapes = [weights[n].shape for n in small]
    wpk, gpk, mpk, vpk = (_pack_rows([src[n] for n in small]) for src in (weights, gs, mom1, mom2))
    outs = adamw(wpk, [gpk], mpk, vpk, name="adamw_small")
    for dst, packed in zip((grads, deltas, new_m, new_v), outs):
        for n, val in zip(small, _unpack_rows(packed, shapes)):
            dst[n] = val

    return (loss, grad_x[None], *[grads[n] for n in order], *[deltas[n] for n in order],
            *[new_m[n] for n in order], *[new_v[n] for n in order])


def _local_grads(x0, target, wf, shards=None, axes=None):
    a_norm, a_conv_b, a_b_r, a_b_i, a_lambda = (wf[n] for n in ("a_norm", "a_conv_b", "a_b_r", "a_b_i", "a_lambda"))
    kv_norm, final_norm = wf["kv_norm"], wf["final_norm"]
    wf = dict(wf)
    parts = {}

    def mm(*args, gather=(), scatter=None, **kw):
        if shards is None or not (gather or scatter):
            return matmul(*args, **kw)
        if gather:
            out, got = matmul(*args, exchange=("gather_halves", [shards[n] for n in gather], [axes[n] for n in gather]),
                              **kw)
            wf.update(zip(gather, got))
        else:
            out, got = matmul(*args, exchange=("scatter", list(scatter.values()), [axes[n] for n in scatter]), **kw)
            parts.update(zip(scatter, got))
        return out

    (h_a,) = rms_fwd(x0, [a_norm], name="norm_a")
    proj_a = mm(h_a, wf["a_w_in"], gather=("a_w_out",), name="a_in")
    m_a, hst, *lru_saved = lru_fwd(proj_a, wf["a_conv_w"], a_conv_b, wf["a_w_r"], a_b_r, wf["a_w_i"], a_b_i,
                                   a_lambda, name="lru_fwd")
    x1 = mm(m_a, wf["a_w_out"], residual=x0, gather=("w_kv",), name="a_out")
    kvn, hb = rms_fwd(x1, [kv_norm, wf["b_norm"]], name="norm_kv_b")
    kv = mm(kvn, wf["w_kv"], out_dtype=BF16, gather=("b_w_in",), name="kv_proj")
    proj_b = mm(hb, wf["b_w_in"], gather=("b_w_out",), name="b_in")
    m_b, o, saved = attn_fwd(proj_b, kv, name="attn_fwd")
    x2 = mm(m_b, wf["b_w_out"], residual=x1, name="b_out")
    loss_part, g_final, dx2, dx2b = loss_bwd(x2, target, final_norm, name="loss_bwd")

    dm_b = mm(dx2b, wf["b_w_out"], tb=True, name="b_out_dx")
    g_b_w_out = mm(m_b, dx2b, ta=True, out_dtype=BF16, name="b_out_dw")
    dq, dgate_b, dk, dv = attn_bwd(proj_b, dm_b, o, kv, saved, name="attn_bwd")
    dproj_b = (dq, dgate_b)
    dkv = (dk, dv)
    g_b_w_in = mm(hb, dproj_b, ta=True, out_dtype=BF16, scatter=dict(b_w_out=g_b_w_out), name="b_in_dw")
    g_w_kv = mm(kvn, dkv, ta=True, out_dtype=BF16, scatter=dict(b_w_in=g_b_w_in), name="kv_dw")
    dhb = mm(dproj_b, wf["b_w_in"], tb=True, scatter=dict(w_kv=g_w_kv), name="b_in_dx")
    dkvn = mm(dkv, wf["w_kv"], tb=True, name="kv_dx")
    dx1, dx1b, (g_kv_norm, g_b_norm) = rms_bwd(
        x1, dx2, [(kv_norm, dkvn), (wf["b_norm"], dhb)], name="norm_kv_b_bwd")

    g_a_w_out = mm(m_a, dx1b, ta=True, out_dtype=BF16, name="a_out_dw")
    dm_a = mm(dx1b, wf["a_w_out"], tb=True, scatter=dict(a_w_out=g_a_w_out), name="a_out_dx")
    dxpre, dgate_a, g_conv_w, g_conv_b, g_w_r, g_b_r, g_w_i, g_b_i, g_lambda = lru_bwd(
        proj_a, hst, lru_saved, dm_a, wf["a_conv_w"], wf["a_w_r"], wf["a_w_i"], a_lambda, name="lru_bwd")
    dproj_a = (dxpre, dgate_a)
    g_w_r, g_w_i = g_w_r.astype(BF16), g_w_i.astype(BF16)
    g_a_w_in = mm(h_a, dproj_a, ta=True, out_dtype=BF16, scatter=dict(a_w_r=g_w_r, a_w_i=g_w_i), name="a_in_dw")
    dh_a = mm(dproj_a, wf["a_w_in"], tb=True, scatter=dict(a_w_in=g_a_w_in), name="a_in_dx")
    grad_x, _, (g_a_norm,) = rms_bwd(x0, dx1, [(a_norm, dh_a)], name="norm_a_bwd")

    gbig = parts if shards is not None else dict(
        a_w_in=g_a_w_in, a_w_r=g_w_r, a_w_i=g_w_i, a_w_out=g_a_w_out, w_kv=g_w_kv, b_w_in=g_b_w_in, b_w_out=g_b_w_out)
    gsmall = dict(a_norm=g_a_norm, a_conv_b=g_conv_b, a_b_r=g_b_r, a_b_i=g_b_i, a_lambda=g_lambda,
                  kv_norm=g_kv_norm, final_norm=g_final, a_conv_w=g_conv_w, b_norm=g_b_norm)
    return loss_part, grad_x, gbig, gsmall
```

```python
import math

import jax
import jax.numpy as jnp
from jax import lax
from jax.experimental import pallas as pl
from jax.experimental.pallas import tpu as pltpu

F32 = jnp.float32
BF16 = jnp.bfloat16
MESH = pl.DeviceIdType.MESH

EPS = 1e-6
LRU_C = 8.0
CONV_W = 4
HEAD_DIM = 128
ADAM_LR = 0.001
ADAM_B1 = 0.9
ADAM_B2 = 0.999
ADAM_EPS = 1e-08
ADAM_WD = 0.01
ADAM_STEP = 10

V7X_VMEM_LIMIT = 56 * 1024 * 1024
V7X_VMEM_LIMIT_HIGH = 60 * 1024 * 1024
LANES = 128
SUBLANES = 8
ATT_BLOCK = 256
ATT_QTILES = 4
ATT_HEADS = 2
ATT_FETCH_AHEAD = 2
LOG2E = 1.4426950408889634
SCAN_UNROLL = 4


def _pick(dim, cands):
    for c in cands:
        if dim % c == 0:
            return c
    return dim


def _params(sem, vmem=V7X_VMEM_LIMIT):
    return pltpu.CompilerParams(dimension_semantics=sem, vmem_limit_bytes=vmem)


def _sigmoid(x):
    return 1.0 / (1.0 + jnp.exp(-x))


def _place():
    return lax.axis_index("x"), lax.axis_index("y"), lax.axis_index("c")


def _chip_peers(x, y, c):
    return [(1 - x, y, c), (x, 1 - y, c), (1 - x, 1 - y, c)]


def _shard_of(ref, axis, idx, n):
    start = pl.multiple_of(idx * n, n)
    sl = [slice(None)] * len(ref.shape)
    sl[axis] = pl.ds(start, n)
    return ref.at[tuple(sl)]


def _half_rows(ref, h):
    n = ref.shape[0] // 2
    return ref.at[pl.ds(pl.multiple_of(h * n, n), n)]


def _block_half(ref, axis, idx, n, h):
    if axis == 0:
        return ref.at[pl.ds(pl.multiple_of(idx * n + h * (n // 2), n // 2), n // 2)]
    return _half_rows(_shard_of(ref, axis, idx, n), h)


def _exchange(kind, in_refs, out_refs, axes, send_sems, recv_sems, local_sems, send2_sems=None, recv2_sems=None):
    x, y, c = _place()
    me = 2 * x + y
    peers = _chip_peers(x, y, c)
    sibling = (x, y, 1 - c)
    triples = []
    nothing = lambda: None
    for ai, (src, dst, ax) in enumerate(zip(in_refs, out_refs, axes)):
        if kind == "scatter":
            n = dst.shape[1 + ax]
            loc = pltpu.make_async_copy(_shard_of(src, ax, me, n), dst.at[0], local_sems.at[ai])
        else:
            n = src.shape[ax]
            mine = _shard_of(dst, ax, me, n)
            loc = pltpu.make_async_copy(src, mine, local_sems.at[ai])
        triples.append((loc.start, nothing, loc.wait))
        for k, peer in enumerate(peers):
            sem = dict(send_sem=send_sems.at[ai * 3 + k], recv_sem=recv_sems.at[ai * 3 + k],
                       device_id=peer, device_id_type=MESH)
            theirs = 2 * peer[0] + peer[1]
            if kind == "gather":
                snd = pltpu.make_async_remote_copy(src_ref=src, dst_ref=mine, **sem)
                rcv = pltpu.make_async_remote_copy(src_ref=src, dst_ref=_shard_of(dst, ax, theirs, n), **sem)
                triples.append((snd.start, nothing, lambda snd=snd, rcv=rcv: (snd.wait_send(), rcv.wait_recv())))
            elif kind == "scatter":
                snd = pltpu.make_async_remote_copy(src_ref=_shard_of(src, ax, theirs, n), dst_ref=dst.at[1 + k], **sem)
                triples.append((snd.start, nothing, snd.wait))
            else:
                landed = _block_half(dst, ax, theirs, n, c)
                snd = pltpu.make_async_remote_copy(src_ref=_half_rows(src, c), dst_ref=_block_half(dst, ax, me, n, c), **sem)
                rcv = pltpu.make_async_remote_copy(src_ref=_half_rows(src, c), dst_ref=landed, **sem)
                sem2 = dict(send_sem=send2_sems.at[ai * 3 + k], recv_sem=recv2_sems.at[ai * 3 + k],
                            device_id=sibling, device_id_type=MESH)
                fwd = pltpu.make_async_remote_copy(src_ref=landed, dst_ref=landed, **sem2)
                got = pltpu.make_async_remote_copy(src_ref=landed, dst_ref=_block_half(dst, ax, theirs, n, 1 - c), **sem2)
                triples.append((snd.start, lambda rcv=rcv, fwd=fwd: (rcv.wait_recv(), fwd.start()),
                                lambda snd=snd, fwd=fwd, got=got: (snd.wait_send(), fwd.wait_send(), got.wait_recv())))
    return triples


def _exchange_shapes(kind, arrays, axes):
    out = []
    for arr, ax in zip(arrays, axes):
        shp = list(arr.shape)
        if kind == "scatter":
            shp[ax] //= 4
            out.append(jax.ShapeDtypeStruct((4, *shp), arr.dtype))
        else:
            shp[ax] *= 4
            out.append(jax.ShapeDtypeStruct(tuple(shp), arr.dtype))
    return out


def _exchange_sems(kind, n):
    sems = [pltpu.SemaphoreType.DMA((3 * n,)), pltpu.SemaphoreType.DMA((3 * n,)), pltpu.SemaphoreType.DMA((n,))]
    if kind == "gather_halves":
        sems += [pltpu.SemaphoreType.DMA((3 * n,)), pltpu.SemaphoreType.DMA((3 * n,))]
    return sems


def matmul(a, b, *, ta=False, tb=False, out_dtype=F32, residual=None, exchange=None, name):
    a_pair = a if isinstance(a, (tuple, list)) else None
    b_pair = b if isinstance(b, (tuple, list)) else None
    assert not (a_pair and ta) and not (b_pair and tb) and not (a_pair and b_pair)
    a0 = a_pair[0] if a_pair else a
    b0 = b_pair[0] if b_pair else b
    m = a0.shape[1] if ta else a0.shape[0]
    kdim = (a0.shape[0] if ta else a0.shape[1]) * (2 if a_pair else 1)
    n = (b0.shape[0] if tb else b0.shape[1]) * (2 if b_pair else 1)
    assert (b0.shape[1] if tb else b0.shape[0]) == kdim
    tm = _pick(m, (1024, 640, 512, 256, 128))
    tn = _pick(n // 2 if b_pair else n, (1024, 1280, 640, 512, 256, 128))
    tk = _pick(kdim // 2 if a_pair else kdim, (2560, 2048, 1024, 512, 256, 128))
    grid = (m // tm, n // tn, kdim // tk)
    nk = grid[2]
    kh, jh = nk // 2, grid[1] // 2
    dn = (((0 if ta else 1,), (1 if tb else 0,)), ((), ()))
    na = 2 if a_pair else 1
    nb = 2 if b_pair else 1
    nres = 0 if residual is None else 1
    nex = 0 if exchange is None else len(exchange[1])

    def body(*refs):
        a_refs, b_refs = refs[:na], refs[na:na + nb]
        p = na + nb
        r_ref = refs[p] if nres else None
        ex_in = refs[p + nres:p + nres + nex]
        o_ref = refs[p + nres + nex]
        ex_out = refs[p + 1 + nres + nex:p + 1 + nres + 2 * nex]
        acc = refs[p + 1 + nres + 2 * nex]
        sems = refs[p + 2 + nres + 2 * nex:]
        i, j, k = pl.program_id(0), pl.program_id(1), pl.program_id(2)
        if nex:
            @pl.when((i == 0) & (j == 0) & (k == 0))
            def _():
                for start, _, _ in _exchange(exchange[0], ex_in, ex_out, exchange[2], *sems):
                    start()

        @pl.when(k == 0)
        def _():
            acc[...] = jnp.zeros_like(acc)

        def accumulate(a_ref, b_ref):
            acc[...] += lax.dot_general(a_ref[...].astype(BF16), b_ref[...].astype(BF16), dn,
                                        preferred_element_type=F32)

        if a_pair:
            pl.when(k < kh)(lambda: accumulate(a_refs[0], b_refs[0]))
            pl.when(k >= kh)(lambda: accumulate(a_refs[1], b_refs[0]))
        elif b_pair:
            pl.when(j < jh)(lambda: accumulate(a_refs[0], b_refs[0]))
            pl.when(j >= jh)(lambda: accumulate(a_refs[0], b_refs[1]))
        else:
            accumulate(a_refs[0], b_refs[0])

        @pl.when(k == nk - 1)
        def _():
            r = acc[...]
            if r_ref is not None:
                r = r + r_ref[...]
            o_ref[...] = r.astype(out_dtype)

        if nex:
            @pl.when((i == grid[0] - 1) & (j == grid[1] - 1) & (k == nk - 1))
            def _():
                triples = _exchange(exchange[0], ex_in, ex_out, exchange[2], *sems)
                for _, relay, _ in triples:
                    relay()
                for _, _, finish in triples:
                    finish()

    if a_pair:
        a_specs = [pl.BlockSpec((tm, tk), lambda i, j, k: (i, jnp.minimum(k, kh - 1))),
                   pl.BlockSpec((tm, tk), lambda i, j, k: (i, jnp.maximum(k - kh, 0)))]
    else:
        a_specs = [pl.BlockSpec((tk, tm), lambda i, j, k: (k, i)) if ta
                   else pl.BlockSpec((tm, tk), lambda i, j, k: (i, k))]
    if b_pair:
        b_specs = [pl.BlockSpec((tk, tn), lambda i, j, k: (jnp.where(j < jh, k, nk - 1), jnp.minimum(j, jh - 1))),
                   pl.BlockSpec((tk, tn), lambda i, j, k: (jnp.where(j >= jh, k, 0), jnp.maximum(j - jh, 0)))]
    else:
        b_specs = [pl.BlockSpec((tn, tk), lambda i, j, k: (j, k)) if tb
                   else pl.BlockSpec((tk, tn), lambda i, j, k: (k, j))]
    o_spec = pl.BlockSpec((tm, tn), lambda i, j, k: (i, j))
    hbm = pl.BlockSpec(memory_space=pl.ANY)
    in_specs = a_specs + b_specs + [o_spec] * nres + [hbm] * nex
    args = (list(a_pair) if a_pair else [a]) + (list(b_pair) if b_pair else [b])
    args += ([residual] if nres else []) + (list(exchange[1]) if nex else [])
    out_shape = [jax.ShapeDtypeStruct((m, n), out_dtype)]
    scratch = [pltpu.VMEM((tm, tn), F32)]
    if nex:
        out_shape += _exchange_shapes(*exchange)
        scratch += _exchange_sems(exchange[0], nex)
    outs = pl.pallas_call(
        body, name=name, grid=grid,
        in_specs=in_specs, out_specs=[o_spec] + [hbm] * nex, out_shape=out_shape,
        scratch_shapes=scratch,
        compiler_params=_params(("arbitrary",) * 3 if nex else ("parallel", "parallel", "arbitrary")),
    )(*args)
    return (outs[0], list(outs[1:])) if nex else outs[0]


def rms_fwd(x, gains, *, name):
    s, d = x.shape
    tr = _pick(s, (512, 256, 128, 8))
    ng = len(gains)

    def body(*refs):
        x_ref = refs[0]
        g_refs = refs[1:1 + ng]
        o_refs = refs[1 + ng:]
        xv = x_ref[...]
        y = xv * lax.rsqrt(jnp.mean(xv * xv, axis=-1, keepdims=True) + EPS)
        for g_ref, o_ref in zip(g_refs, o_refs):
            o_ref[...] = (y * g_ref[...]).astype(BF16)

    row = pl.BlockSpec((tr, d), lambda i: (i, 0))
    vec = pl.BlockSpec((1, d), lambda i: (0, 0))
    return pl.pallas_call(
        body, name=name, grid=(s // tr,),
        in_specs=[row] + [vec] * ng, out_specs=[row] * ng,
        out_shape=[jax.ShapeDtypeStruct((s, d), BF16)] * ng,
        compiler_params=_params(("parallel",)),
    )(x, *gains)


def rms_bwd(x, dres, norms, *, name):
    s, d = x.shape
    tr = _pick(s, (256, 128, 8))
    ng = len(norms)

    def body(*refs):
        x_ref, dres_ref = refs[0], refs[1]
        g_refs = refs[2:2 + ng]
        dh_refs = refs[2 + ng:2 + 2 * ng]
        dx_ref, dxb_ref = refs[2 + 2 * ng], refs[3 + 2 * ng]
        dg_refs = refs[4 + 2 * ng:]
        i = pl.program_id(0)
        xv = x_ref[...]
        r = lax.rsqrt(jnp.mean(xv * xv, axis=-1, keepdims=True) + EPS)
        xhat = xv * r
        dx = dres_ref[...]
        for g_ref, dh_ref, dg_ref in zip(g_refs, dh_refs, dg_refs):
            dh = dh_ref[...]
            part = jnp.sum(dh * xhat, axis=0, keepdims=True)

            @pl.when(i == 0)
            def _():
                dg_ref[...] = part

            @pl.when(i > 0)
            def _():
                dg_ref[...] += part

            dxhat = dh * g_ref[...]
            dx = dx + r * (dxhat - xhat * jnp.mean(dxhat * xhat, axis=-1, keepdims=True))
        dx_ref[...] = dx
        dxb_ref[...] = dx.astype(BF16)

    row = pl.BlockSpec((tr, d), lambda i: (i, 0))
    vec = pl.BlockSpec((1, d), lambda i: (0, 0))
    outs = pl.pallas_call(
        body, name=name, grid=(s // tr,),
        in_specs=[row, row] + [vec] * ng + [row] * ng,
        out_specs=[row, row] + [vec] * ng,
        out_shape=[jax.ShapeDtypeStruct((s, d), F32), jax.ShapeDtypeStruct((s, d), BF16)]
        + [jax.ShapeDtypeStruct((1, d), F32)] * ng,
        compiler_params=_params(("arbitrary",)),
    )(x, dres, *[g for g, _ in norms], *[dh for _, dh in norms])
    return outs[0], outs[1], list(outs[2:])


def loss_bwd(x2, target, gain, *, name):
    s, d = x2.shape
    tr = _pick(s, (256, 128, 8))
    nsteps = s // tr

    def body(x_ref, t_ref, g_ref, loss_ref, dg_ref, dx_ref, dxb_ref, sq_acc):
        i = pl.program_id(0)
        xv = x_ref[...]
        r = lax.rsqrt(jnp.mean(xv * xv, axis=-1, keepdims=True) + EPS)
        xhat = xv * r
        g = g_ref[...]
        err = xhat * g - t_ref[...]
        dy = err * (1.0 / d)
        sq = jnp.sum(err * err, axis=0, keepdims=True)
        dgp = jnp.sum(dy * xhat, axis=0, keepdims=True)

        @pl.when(i == 0)
        def _():
            sq_acc[...] = sq
            dg_ref[...] = dgp

        @pl.when(i > 0)
        def _():
            sq_acc[...] += sq
            dg_ref[...] += dgp

        dxhat = dy * g
        dx = r * (dxhat - xhat * jnp.mean(dxhat * xhat, axis=-1, keepdims=True))
        dx_ref[...] = dx
        dxb_ref[...] = dx.astype(BF16)

        @pl.when(i == nsteps - 1)
        def _():
            tot = jnp.sum(sq_acc[...], axis=-1, keepdims=True) * (0.5 / d)
            loss_ref[...] = jnp.broadcast_to(tot, (1, LANES))

    row = pl.BlockSpec((tr, d), lambda i: (i, 0))
    vec = pl.BlockSpec((1, d), lambda i: (0, 0))
    return pl.pallas_call(
        body, name=name, grid=(nsteps,),
        in_specs=[row, row, vec],
        out_specs=[pl.BlockSpec((1, LANES), lambda i: (0, 0)), vec, row, row],
        out_shape=[jax.ShapeDtypeStruct((1, LANES), F32), jax.ShapeDtypeStruct((1, d), F32),
                   jax.ShapeDtypeStruct((s, d), F32), jax.ShapeDtypeStruct((s, d), BF16)],
        scratch_shapes=[pltpu.VMEM((1, d), F32)],
        compiler_params=_params(("arbitrary",)),
    )(x2, target, gain)


def _lru_gates(xb, wr, wi, br, bi, sp):
    xbb = xb.astype(BF16)
    r = _sigmoid(jnp.dot(xbb, wr, preferred_element_type=F32) + br)
    ig = _sigmoid(jnp.dot(xbb, wi, preferred_element_type=F32) + bi)
    log_a = (-LRU_C) * r * sp
    a = jnp.exp(log_a)
    mult = jnp.sqrt(jnp.maximum(-jnp.tanh(log_a) * (a * a + 1.0), 0.0))
    return r, ig, a, mult


def _softplus_neg(lam):
    e = jnp.exp(-jnp.abs(lam))
    sp = jnp.maximum(-lam, 0.0) + jnp.log(1.0 + e)
    sg = jnp.where(lam >= 0, e, 1.0) / (1.0 + e)
    return sp, sg


def _conv(pad_ref, w, b, t):
    acc = b + w[CONV_W - 1:CONV_W, :] * pad_ref[pl.ds(SUBLANES, t), :]
    for dlt in range(1, CONV_W):
        acc = acc + w[CONV_W - 1 - dlt:CONV_W - dlt, :] * pad_ref[pl.ds(SUBLANES - dlt, t), :]
    return acc


def _lru_specs(t, bw, nb, time_of):
    blk = lambda c0: pl.BlockSpec((t, bw), lambda n, i, c0=c0: (time_of(i), c0 + n))
    vec = pl.BlockSpec((1, bw), lambda n, i: (0, n))
    wspec = pl.BlockSpec((None, bw, bw), lambda n, i: (n, 0, 0))
    cwspec = pl.BlockSpec((CONV_W, bw), lambda n, i: (0, n))
    return blk, vec, wspec, cwspec


def lru_fwd(proj, conv_w, conv_b, w_r, b_r, w_i, b_i, lam, *, name):
    s, r2 = proj.shape
    rr = r2 // 2
    nb, bw, _ = w_r.shape
    t = _pick(s, (512, 256, 128, 64, 32))
    ngroups = t // SUBLANES

    def body(xp_ref, gate_ref, cw_ref, cb_ref, wr_ref, br_ref, wi_ref, bi_ref, lam_ref,
             m_ref, h_ref, xb_ref, r_ref, i_ref, a_ref, mult_ref, pad, hcarry, u_scr):
        i = pl.program_id(1)

        @pl.when(i == 0)
        def _():
            pad[0:SUBLANES, :] = jnp.zeros((SUBLANES, bw), F32)
            hcarry[...] = jnp.zeros_like(hcarry)

        xpre = xp_ref[...]
        pad[pl.ds(SUBLANES, t), :] = xpre
        xb = _conv(pad, cw_ref[...], cb_ref[...], t)
        pad[0:SUBLANES, :] = xpre[t - SUBLANES:, :]
        sp, _ = _softplus_neg(lam_ref[...])
        r, ig, a, mult = _lru_gates(xb, wr_ref[...], wi_ref[...], br_ref[...], bi_ref[...], sp)
        xb_ref[...] = xb
        r_ref[...] = r
        i_ref[...] = ig
        a_ref[...] = a
        mult_ref[...] = mult
        u_scr[...] = mult * (ig * xb)
        row = lax.broadcasted_iota(jnp.int32, (SUBLANES, bw), 0)

        def groups(gi, hprev):
            offs = [pl.multiple_of((gi * SCAN_UNROLL + u) * SUBLANES, SUBLANES) for u in range(SCAN_UNROLL)]
            scanned = []
            for off in offs:
                av = a_ref[pl.ds(off, SUBLANES), :]
                uv = u_scr[pl.ds(off, SUBLANES), :]
                for dlt in (1, 2, 4):
                    keep = row >= dlt
                    uv = jnp.where(keep, av * pltpu.roll(uv, dlt, 0) + uv, uv)
                    av = jnp.where(keep, av * pltpu.roll(av, dlt, 0), av)
                scanned.append((av, uv))
            for off, (av, uv) in zip(offs, scanned):
                hv = av * hprev + uv
                h_ref[pl.ds(off, SUBLANES), :] = hv
                hprev = hv[SUBLANES - 1:SUBLANES, :]
            return hprev

        hcarry[...] = lax.fori_loop(0, ngroups // SCAN_UNROLL, groups, hcarry[...])
        gate = gate_ref[...]
        m_ref[...] = (h_ref[...] * (gate * _sigmoid(gate))).astype(BF16)

    blk, vec, wspec, cwspec = _lru_specs(t, bw, nb, lambda i: i)
    return pl.pallas_call(
        body, name=name, grid=(nb, s // t),
        in_specs=[blk(0), blk(nb), cwspec, vec, wspec, vec, wspec, vec, vec],
        out_specs=[blk(0)] * 7,
        out_shape=[jax.ShapeDtypeStruct((s, rr), BF16)] + [jax.ShapeDtypeStruct((s, rr), F32)] * 6,
        scratch_shapes=[pltpu.VMEM((t + SUBLANES, bw), F32), pltpu.VMEM((1, bw), F32), pltpu.VMEM((t, bw), F32)],
        compiler_params=_params(("parallel", "arbitrary")),
    )(proj, proj, conv_w, conv_b, w_r, b_r, w_i, b_i, lam)


def lru_bwd(proj, hst, saved, dm, conv_w, w_r, w_i, lam, *, name):
    s, r2 = proj.shape
    rr = r2 // 2
    nb, bw, _ = w_r.shape
    t = _pick(s, (512, 256, 128, 64, 32))
    nt = s // t
    ngroups = t // SUBLANES
    nt_dims = (((1,), (1,)), ((), ()))
    tn_dims = (((0,), (0,)), ((), ()))

    def body(xp_ref, xhalo_ref, gate_ref, h_ref, hhalo_ref, xb_ref, r_ref, i_ref, a_ref, mult_ref, dm_ref,
             cw_ref, wr_ref, wi_ref, lam_ref,
             dxp_ref, dgate_ref, dcw_ref, dcb_ref, dwr_ref, dbr_ref, dwi_ref, dbi_ref, dlam_ref,
             pad, hpad, dpad, ecarry, b_scr, d_scr):
        step = pl.program_id(1)

        @pl.when(step == 0)
        def _():
            dpad[pl.ds(t, SUBLANES), :] = jnp.zeros((SUBLANES, bw), F32)
            ecarry[...] = jnp.zeros_like(ecarry)
            dcw_ref[...] = jnp.zeros_like(dcw_ref)
            dcb_ref[...] = jnp.zeros_like(dcb_ref)
            dwr_ref[...] = jnp.zeros_like(dwr_ref)
            dbr_ref[...] = jnp.zeros_like(dbr_ref)
            dwi_ref[...] = jnp.zeros_like(dwi_ref)
            dbi_ref[...] = jnp.zeros_like(dbi_ref)
            dlam_ref[...] = jnp.zeros_like(dlam_ref)

        past = jnp.where(step == nt - 1, 0.0, 1.0)
        pad[0:SUBLANES, :] = xhalo_ref[...] * past
        pad[pl.ds(SUBLANES, t), :] = xp_ref[...]
        hpad[0:SUBLANES, :] = hhalo_ref[...] * past
        hpad[pl.ds(SUBLANES, t), :] = h_ref[...]
        cw = cw_ref[...]
        sp, sg = _softplus_neg(lam_ref[...])
        wr = wr_ref[...]
        wi = wi_ref[...]
        xb, r, ig, a, mult = xb_ref[...], r_ref[...], i_ref[...], a_ref[...], mult_ref[...]
        gate = gate_ref[...]
        sgate = _sigmoid(gate)
        dmv = dm_ref[...]
        dgate_ref[...] = (dmv * h_ref[...] * (sgate * (1.0 + gate * (1.0 - sgate)))).astype(BF16)
        dy = dmv * (gate * sgate)
        b_scr[...] = a * dy
        row = lax.broadcasted_iota(jnp.int32, (SUBLANES, bw), 0)

        def groups(gi, enext):
            offs = [pl.multiple_of((ngroups - 1 - gi * SCAN_UNROLL - u) * SUBLANES, SUBLANES)
                    for u in range(SCAN_UNROLL)]
            scanned = []
            for off in offs:
                av = a_ref[pl.ds(off, SUBLANES), :]
                bv = b_scr[pl.ds(off, SUBLANES), :]
                for dlt in (1, 2, 4):
                    keep = row < SUBLANES - dlt
                    bv = jnp.where(keep, av * pltpu.roll(bv, SUBLANES - dlt, 0) + bv, bv)
                    av = jnp.where(keep, av * pltpu.roll(av, SUBLANES - dlt, 0), av)
                scanned.append((av, bv))
            for off, (av, bv) in zip(offs, scanned):
                ev = av * enext + bv
                d_scr[pl.ds(off, SUBLANES), :] = jnp.where(row == SUBLANES - 1, enext,
                                                           pltpu.roll(ev, SUBLANES - 1, 0))
                enext = ev[0:1, :]
            return enext

        ecarry[...] = lax.fori_loop(0, ngroups // SCAN_UNROLL, groups, ecarry[...])
        dtot = dy + d_scr[...]
        da = dtot * hpad[pl.ds(SUBLANES - 1, t), :]
        dmult = dtot * (ig * xb)
        dlog_a = da * a - dmult * (a * a) / mult
        dr_pre = dlog_a * ((-LRU_C) * sp) * (r * (1.0 - r))
        di_pre = (dtot * mult * xb) * (ig * (1.0 - ig))
        dlam_ref[...] += jnp.sum(dlog_a * r, axis=0, keepdims=True) * (LRU_C * sg)
        dbr_ref[...] += jnp.sum(dr_pre, axis=0, keepdims=True)
        dbi_ref[...] += jnp.sum(di_pre, axis=0, keepdims=True)
        drb = dr_pre.astype(BF16)
        dib = di_pre.astype(BF16)
        xbb = xb.astype(BF16)
        dxb = (dtot * mult * ig
               + lax.dot_general(drb, wr, nt_dims, preferred_element_type=F32)
               + lax.dot_general(dib, wi, nt_dims, preferred_element_type=F32))
        dwr_ref[...] += lax.dot_general(xbb, drb, tn_dims, preferred_element_type=F32)
        dwi_ref[...] += lax.dot_general(xbb, dib, tn_dims, preferred_element_type=F32)
        dcb_ref[...] += jnp.sum(dxb, axis=0, keepdims=True)
        dpad[pl.ds(0, t), :] = dxb
        dxpre = cw[CONV_W - 1:CONV_W, :] * dxb
        dcw_ref[CONV_W - 1:CONV_W, :] += jnp.sum(dxb * pad[pl.ds(SUBLANES, t), :], axis=0, keepdims=True)
        for dlt in range(1, CONV_W):
            dxpre = dxpre + cw[CONV_W - 1 - dlt:CONV_W - dlt, :] * dpad[pl.ds(dlt, t), :]
            dcw_ref[CONV_W - 1 - dlt:CONV_W - dlt, :] += jnp.sum(
                dxb * pad[pl.ds(SUBLANES - dlt, t), :], axis=0, keepdims=True)
        dpad[pl.ds(t, SUBLANES), :] = dxb[0:SUBLANES, :]
        dxp_ref[...] = dxpre.astype(BF16)

    rev = lambda i: nt - 1 - i
    blk, vec, wspec, cwspec = _lru_specs(t, bw, nb, rev)
    halo = pl.BlockSpec((SUBLANES, bw), lambda n, i: (jnp.maximum(rev(i) * ngroups - 1, 0), n))
    return pl.pallas_call(
        body, name=name, grid=(nb, nt),
        in_specs=[blk(0), halo, blk(nb), blk(0), halo] + [blk(0)] * 6 + [cwspec, wspec, wspec, vec],
        out_specs=[blk(0), blk(0), cwspec, vec, wspec, vec, wspec, vec, vec],
        out_shape=[jax.ShapeDtypeStruct((s, rr), BF16), jax.ShapeDtypeStruct((s, rr), BF16),
                   jax.ShapeDtypeStruct((CONV_W, rr), F32), jax.ShapeDtypeStruct((1, rr), F32),
                   jax.ShapeDtypeStruct((nb, bw, bw), F32), jax.ShapeDtypeStruct((1, rr), F32),
                   jax.ShapeDtypeStruct((nb, bw, bw), F32), jax.ShapeDtypeStruct((1, rr), F32),
                   jax.ShapeDtypeStruct((1, rr), F32)],
        scratch_shapes=[pltpu.VMEM((t + SUBLANES, bw), F32), pltpu.VMEM((t + SUBLANES, bw), F32),
                        pltpu.VMEM((t + SUBLANES, bw), F32), pltpu.VMEM((1, bw), F32),
                        pltpu.VMEM((t, bw), F32), pltpu.VMEM((t, bw), F32)],
        compiler_params=_params(("parallel", "arbitrary")),
    )(proj, proj, proj, hst, hst, *saved, dm, conv_w, w_r, w_i, lam)


def _softplus(z):
    return jnp.maximum(z, 0.0) + jnp.log(1.0 + jnp.exp2(jnp.abs(z) * (-LOG2E)))


def _att_blocks(s):
    bk = ATT_BLOCK if s % ATT_BLOCK == 0 else s
    bq = ATT_QTILES * bk if s % (ATT_QTILES * bk) == 0 else bk
    return bk, bq


def _tile_base(i, r):
    return r * ((i * (i + 1)) // 2)


def attn_fwd(projb, kv, *, name):
    s, a2 = projb.shape
    a = a2 // 2
    nh = a // HEAD_DIM
    bk, bq = _att_blocks(s)
    r = bq // bk
    nq = s // bq
    ntiles = _tile_base(nq, r)
    scale = 1.0 / math.sqrt(HEAD_DIM)
    nt_dims = (((1,), (1,)), ((), ()))
    hp = ATT_HEADS if nh % ATT_HEADS == 0 else 1
    wd = hp * HEAD_DIM

    def body(q_ref, g_ref, k_ref, v_ref, m_ref, o_ref, saved_hbm, acc, stage, sems):
        hgrp, i = pl.program_id(0), pl.program_id(1)
        base = _tile_base(i, r)
        qb = (q_ref[...] * scale).astype(BF16)
        from_mat = (lax.broadcasted_iota(jnp.int32, (bk, bk), 0)
                    >= lax.broadcasted_iota(jnp.int32, (bk, bk), 1)).astype(BF16)
        rowi = lax.broadcasted_iota(jnp.int32, (bq, bk), 0)
        coli = lax.broadcasted_iota(jnp.int32, (bq, bk), 1)
        cols = [slice(hh * HEAD_DIM, (hh + 1) * HEAD_DIM) for hh in range(hp)]

        def save(slot, j):
            return pltpu.make_async_copy(stage.at[slot], saved_hbm.at[hgrp, base + j], sems.at[slot])

        def tile(j, n, carries, diag):
            r0 = 0 if diag is None else diag * bk
            live = slice(r0, bq)
            causal = None if diag is None else coli[live] < rowi[:bq - r0]
            slot = n % 2

            def free_slot():
                save(slot, 0).wait()

            if isinstance(n, int):
                if n >= 2:
                    free_slot()
            elif r >= 2:
                free_slot()
            else:
                pl.when(n >= 2)(free_slot)
            rows = pl.ds(pl.multiple_of(j * bk, bk), bk)
            zs = [lax.dot_general(qb[live, c], k_ref[rows, c], nt_dims, preferred_element_type=F32) for c in cols]
            sums, sigs = [], []
            for z in zs:
                sp = _softplus(z)
                sig = jnp.exp(z - sp)
                if causal is not None:
                    sp = jnp.where(causal, sp, 0.0)
                    sig = jnp.where(causal, sig, 0.0)
                sums.append(jnp.dot(sp.astype(BF16), from_mat, preferred_element_type=F32))
                sigs.append(sig.astype(BF16))
            out = []
            for hh in range(hp):
                w = jnp.exp(zs[hh] - sums[hh] - carries[hh][live])
                if causal is not None:
                    w = jnp.where(causal, w, 0.0)
                wb = w.astype(BF16)
                acc[live, cols[hh]] += jnp.dot(wb, v_ref[rows, cols[hh]], preferred_element_type=F32)
                stage[slot, 0, hh, live] = wb
                stage[slot, 1, hh, live] = sigs[hh]
                if r0:
                    stage[slot, :, hh, :r0] = jnp.zeros((2, r0, bk), BF16)
                grown = carries[hh][live] + sums[hh][:, 0:1]
                out.append(jnp.concatenate([carries[hh][:r0], grown], axis=0) if r0 else grown)
            save(slot, j).start()
            return tuple(out)

        acc[...] = jnp.zeros_like(acc)
        carries = tuple(jnp.zeros((bq, 1), F32) for _ in range(hp))
        for n, dg in enumerate(reversed(range(r))):
            carries = tile(r * i + dg, n, carries, dg)
        lax.fori_loop(0, r * i, lambda jj, c: tile(r * i - 1 - jj, r + jj, c, None), carries)
        ntile = r * (i + 1)
        for back in (1, 2):
            def drain(back=back):
                save((ntile - back) % 2, 0).wait()
            if r >= back:
                drain()
            else:
                pl.when(ntile >= back)(drain)
        o = acc[...]
        o_ref[...] = o
        gate = g_ref[...]
        m_ref[...] = (o * (gate * _sigmoid(gate))).astype(BF16)

    ng = nh // hp
    qspec = lambda c0: pl.BlockSpec((bq, wd), lambda h, i, c0=c0: (i, c0 + h))
    kspec = lambda c0: pl.BlockSpec((s, wd), lambda h, i, c0=c0: (0, c0 + h), pipeline_mode=pl.Buffered(1))
    hbm = pl.BlockSpec(memory_space=pl.ANY)
    saved = jax.ShapeDtypeStruct((ng, ntiles, 2, hp, bq, bk), BF16)
    return pl.pallas_call(
        body, name=name, grid=(ng, nq),
        in_specs=[qspec(0), qspec(ng), kspec(0), kspec(ng)],
        out_specs=[qspec(0), qspec(0), hbm],
        out_shape=[jax.ShapeDtypeStruct((s, a), BF16), jax.ShapeDtypeStruct((s, a), F32), saved],
        scratch_shapes=[pltpu.VMEM((bq, wd), F32), pltpu.VMEM((2, 2, hp, bq, bk), BF16),
                        pltpu.SemaphoreType.DMA((2,))],
        compiler_params=_params(("arbitrary", "arbitrary")),
    )(projb, projb, kv, kv)


def attn_bwd(projb, dm, o, kv, saved, *, name):
    s, a2 = projb.shape
    a = a2 // 2
    nh = a // HEAD_DIM
    bk, bq = _att_blocks(s)
    r = bq // bk
    nq = s // bq
    scale = 1.0 / math.sqrt(HEAD_DIM)
    nt_dims = (((1,), (1,)), ((), ()))
    tn_dims = (((0,), (0,)), ((), ()))
    hp = saved.shape[3]
    wd = hp * HEAD_DIM
    ahead = ATT_FETCH_AHEAD

    def body(q_ref, g_ref, dm_ref, o_ref, k_ref, v_ref, saved_hbm, dq_ref, dg_ref, dk_ref, dv_ref,
             dk_acc, dv_acc, dq_acc, stage, sems):
        hgrp, i = pl.program_id(0), pl.program_id(1)
        base = _tile_base(i, r)
        ntile = r * (i + 1)

        @pl.when(i == 0)
        def _():
            dk_acc[...] = jnp.zeros_like(dk_acc)
            dv_acc[...] = jnp.zeros_like(dv_acc)

        def fetch(j):
            slot = j % (ahead + 1)
            return pltpu.make_async_copy(saved_hbm.at[hgrp, base + j], stage.at[slot], sems.at[slot])

        for j0 in range(ahead):
            pl.when(j0 < ntile)(lambda j0=j0: fetch(j0).start())
        qb = (q_ref[...] * scale).astype(BF16)
        gate = g_ref[...]
        sgate = _sigmoid(gate)
        dmv = dm_ref[...]
        dob = (dmv * (gate * sgate)).astype(BF16)
        dg_ref[...] = (dmv * o_ref[...] * (sgate * (1.0 + gate * (1.0 - sgate)))).astype(BF16)
        upto_mat = (lax.broadcasted_iota(jnp.int32, (bk, bk), 0)
                    <= lax.broadcasted_iota(jnp.int32, (bk, bk), 1)).astype(BF16)
        cols = [slice(hh * HEAD_DIM, (hh + 1) * HEAD_DIM) for hh in range(hp)]
        dq_acc[...] = jnp.zeros_like(dq_acc)

        def tile(j, gcarries, r0=0, more=None):
            live = slice(r0, bq)
            slot = j % (ahead + 1)
            if more is None:
                pl.when(j + ahead < ntile)(lambda: fetch(j + ahead).start())
            elif more:
                fetch(j + ahead).start()
            fetch(j).wait()
            rows = pl.ds(pl.multiple_of(j * bk, bk), bk)
            dws = [lax.dot_general(dob[live, c], v_ref[rows, c], nt_dims, preferred_element_type=F32) for c in cols]
            gs, totals = [], []
            for hh in range(hp):
                wb = stage[slot, 0, hh, live]
                g = wb.astype(F32) * dws[hh]
                dv_acc[rows, cols[hh]] += lax.dot_general(wb, dob[live, cols[hh]], tn_dims,
                                                          preferred_element_type=F32)
                totals.append(jnp.dot(g.astype(BF16), upto_mat, preferred_element_type=F32))
                gs.append(g)
            out = []
            for hh in range(hp):
                dz = gs[hh] - (totals[hh] + gcarries[hh][live]) * stage[slot, 1, hh, live].astype(F32)
                dzb = dz.astype(BF16)
                dq_acc[live, cols[hh]] += jnp.dot(dzb, k_ref[rows, cols[hh]], preferred_element_type=F32)
                dk_acc[rows, cols[hh]] += lax.dot_general(dzb, qb[live, cols[hh]], tn_dims,
                                                          preferred_element_type=F32)
                grown = gcarries[hh][live] + totals[hh][:, bk - 1:bk]
                out.append(jnp.concatenate([gcarries[hh][:r0], grown], axis=0) if r0 else grown)
            return tuple(out)

        gcarries = lax.fori_loop(0, r * i, tile, tuple(jnp.zeros((bq, 1), F32) for _ in range(hp)))
        for dg in range(r):
            gcarries = tile(r * i + dg, gcarries, dg * bk, dg + ahead < r)
        dq_ref[...] = (dq_acc[...] * scale).astype(BF16)

        @pl.when(i == nq - 1)
        def _():
            dk_ref[...] = dk_acc[...].astype(BF16)
            dv_ref[...] = dv_acc[...].astype(BF16)

    ng = nh // hp
    once = pl.Buffered(1)
    qspec = lambda c0: pl.BlockSpec((bq, wd), lambda h, i, c0=c0: (i, c0 + h))
    kspec = lambda c0: pl.BlockSpec((s, wd), lambda h, i, c0=c0: (0, c0 + h), pipeline_mode=once)
    hbm = pl.BlockSpec(memory_space=pl.ANY)
    return pl.pallas_call(
        body, name=name, grid=(ng, nq),
        in_specs=[qspec(0), qspec(ng), qspec(0), qspec(0), kspec(0), kspec(ng), hbm],
        out_specs=[qspec(0), qspec(0), kspec(0), kspec(0)],
        out_shape=[jax.ShapeDtypeStruct((s, a), BF16)] * 4,
        scratch_shapes=[pltpu.VMEM((s, wd), F32), pltpu.VMEM((s, wd), F32), pltpu.VMEM((bq, wd), F32),
                        pltpu.VMEM((ahead + 1, 2, hp, bq, bk), BF16), pltpu.SemaphoreType.DMA((ahead + 1,))],
        compiler_params=_params(("arbitrary", "arbitrary"), vmem=V7X_VMEM_LIMIT_HIGH),
    )(projb, projb, dm, o, kv, kv, saved)


def _as2d(x):
    n = x.size
    cols = x.shape[-1]
    if cols % LANES != 0:
        cols = LANES
    return x.reshape(n // cols, cols)


def sum_parts(parts, *, name):
    p, rows, cols = parts.shape
    tr = _pick(rows, (512, 256, 128, 64, 32, 16))

    def body(p_ref, o_ref):
        acc = p_ref[0].astype(F32)
        for k in range(1, p):
            acc = acc + p_ref[k].astype(F32)
        o_ref[...] = acc

    return pl.pallas_call(
        body, name=name, grid=(rows // tr,),
        in_specs=[pl.BlockSpec((p, tr, cols), lambda i: (0, i, 0))],
        out_specs=pl.BlockSpec((tr, cols), lambda i: (i, 0)),
        out_shape=jax.ShapeDtypeStruct((rows, cols), F32),
        compiler_params=_params(("parallel",)),
    )(parts)


def adamw(w, g_parts, m, v, *, name):
    rows, cols = w.shape
    tr = _pick(rows, (128, 64, 32, 16, 8))
    np_ = len(g_parts)
    c1 = 1.0 / (1.0 - ADAM_B1 ** ADAM_STEP)
    c2 = 1.0 / (1.0 - ADAM_B2 ** ADAM_STEP)

    def body(*refs):
        w_ref, m_ref, v_ref = refs[0], refs[1], refs[2]
        g_refs = refs[3:3 + np_]
        go_ref, d_ref, mo_ref, vo_ref = refs[3 + np_:]
        g = g_refs[0][...]
        for gr in g_refs[1:]:
            g = g + gr[...]
        mn = ADAM_B1 * m_ref[...] + (1.0 - ADAM_B1) * g
        vn = ADAM_B2 * v_ref[...] + (1.0 - ADAM_B2) * (g * g)
        go_ref[...] = g
        mo_ref[...] = mn
        vo_ref[...] = vn
        d_ref[...] = (-ADAM_LR) * ((mn * c1) / (jnp.sqrt(vn * c2) + ADAM_EPS) + ADAM_WD * w_ref[...])

    spec = pl.BlockSpec((tr, cols), lambda i: (i, 0))
    return pl.pallas_call(
        body, name=name, grid=(rows // tr,),
        in_specs=[spec] * (3 + np_), out_specs=[spec] * 4,
        out_shape=[jax.ShapeDtypeStruct((rows, cols), F32)] * 4,
        compiler_params=_params(("parallel",)),
    )(w, m, v, *g_parts)


def exchange(kind, arrays, axes, *, name):
    na = len(arrays)
    hbm = pl.BlockSpec(memory_space=pl.ANY)

    def body(*refs):
        triples = _exchange(kind, refs[:na], refs[na:2 * na], axes, *refs[2 * na:])
        for step in range(3):
            for triple in triples:
                triple[step]()

    return pl.pallas_call(
        body, name=name, in_specs=[hbm] * na, out_specs=[hbm] * na,
        out_shape=_exchange_shapes(kind, arrays, axes), scratch_shapes=_exchange_sems(kind, na),
    )(*arrays)


def swap_cores(arrs, *, name):
    na = len(arrs)
    hbm = pl.BlockSpec(memory_space=pl.ANY)

    def body(*refs):
        a_refs = refs[:na]
        o_refs = refs[na:2 * na]
        send_sems, recv_sems = refs[2 * na:]
        x, y, c = _place()
        copies = []
        for ai in range(na):
            cp = pltpu.make_async_remote_copy(
                src_ref=a_refs[ai], dst_ref=o_refs[ai], send_sem=send_sems.at[ai], recv_sem=recv_sems.at[ai],
                device_id=(x, y, 1 - c), device_id_type=MESH)
            cp.start()
            copies.append(cp)
        for cp in copies:
            cp.wait()

    return pl.pallas_call(
        body, name=name,
        in_specs=[hbm] * na, out_specs=[hbm] * na,
        out_shape=[jax.ShapeDtypeStruct(a.shape, a.dtype) for a in arrs],
        scratch_shapes=[pltpu.SemaphoreType.DMA((na,)), pltpu.SemaphoreType.DMA((na,))],
    )(*arrs)


def allreduce_small(arrs, *, name):
    nar = len(arrs)
    width = max(a.shape[1] for a in arrs)
    starts, total = [], 0
    for a in arrs:
        starts.append(total)
        total += a.shape[0]
    total += (-total) % SUBLANES

    def body(*refs):
        in_refs, out_refs = refs[:nar], refs[nar:2 * nar]
        buf, slots, send_sems, recv_sems = refs[2 * nar:]
        x, y, c = _place()
        me = 4 * x + 2 * y + c
        buf[...] = jnp.zeros_like(buf)
        for ref, st in zip(in_refs, starts):
            buf[st:st + ref.shape[0], 0:ref.shape[1]] = ref[...]
        slots[0] = buf[...]
        copies = []
        for rel in range(1, 8):
            peer = (x ^ (rel >> 2), y ^ ((rel >> 1) & 1), c ^ (rel & 1))
            cp = pltpu.make_async_remote_copy(
                src_ref=buf, dst_ref=slots.at[rel], send_sem=send_sems.at[rel - 1],
                recv_sem=recv_sems.at[rel - 1], device_id=peer, device_id_type=MESH)
            cp.start()
            copies.append(cp)
        for cp in copies:
            cp.wait()
        acc = slots[me]
        for dev in range(1, 8):
            acc = acc + slots[dev ^ me]
        buf[...] = acc
        for ref, st in zip(out_refs, starts):
            ref[...] = buf[st:st + ref.shape[0], 0:ref.shape[1]]

    vm = pl.BlockSpec(memory_space=pltpu.VMEM)
    return pl.pallas_call(
        body, name=name, in_specs=[vm] * nar, out_specs=[vm] * nar,
        out_shape=[jax.ShapeDtypeStruct(a.shape, F32) for a in arrs],
        scratch_shapes=[pltpu.VMEM((total, width), F32), pltpu.VMEM((8, total, width), F32),
                        pltpu.SemaphoreType.DMA((7,)), pltpu.SemaphoreType.DMA((7,))],
    )(*arrs)


def adamw_small(ws, gs, ms, vs, *, name):
    n = len(ws)
    c1 = 1.0 / (1.0 - ADAM_B1 ** ADAM_STEP)
    c2 = 1.0 / (1.0 - ADAM_B2 ** ADAM_STEP)

    def body(*refs):
        w_refs, g_refs, m_refs, v_refs = (refs[k * n:(k + 1) * n] for k in range(4))
        d_refs, mo_refs, vo_refs = (refs[(4 + k) * n:(5 + k) * n] for k in range(3))
        for w_ref, g_ref, m_ref, v_ref, d_ref, mo_ref, vo_ref in zip(w_refs, g_refs, m_refs, v_refs,
                                                                     d_refs, mo_refs, vo_refs):
            g = g_ref[...]
            mn = ADAM_B1 * m_ref[...] + (1.0 - ADAM_B1) * g
            vn = ADAM_B2 * v_ref[...] + (1.0 - ADAM_B2) * (g * g)
            mo_ref[...] = mn
            vo_ref[...] = vn
            d_ref[...] = (-ADAM_LR) * ((mn * c1) / (jnp.sqrt(vn * c2) + ADAM_EPS) + ADAM_WD * w_ref[...])

    vm = pl.BlockSpec(memory_space=pltpu.VMEM)
    outs = pl.pallas_call(
        body, name=name, in_specs=[vm] * (4 * n), out_specs=[vm] * (3 * n),
        out_shape=[jax.ShapeDtypeStruct(w.shape, F32) for w in ws] * 3,
    )(*ws, *gs, *ms, *vs)
    return outs[:n], outs[n:2 * n], outs[2 * n:]


def kernel(x, a_norm, a_w_in, a_conv_w, a_conv_b, a_w_r, a_b_r, a_w_i, a_b_i, a_lambda, a_w_out, kv_norm, w_kv, b_norm, b_w_in, b_w_out, final_norm, loss_target, m_a_norm, m_a_w_in, m_a_conv_w, m_a_conv_b, m_a_w_r, m_a_b_r, m_a_w_i, m_a_b_i, m_a_lambda, m_a_w_out, m_kv_norm, m_w_kv, m_b_norm, m_b_w_in, m_b_w_out, m_final_norm, v_a_norm, v_a_w_in, v_a_conv_w, v_a_conv_b, v_a_w_r, v_a_b_r, v_a_w_i, v_a_b_i, v_a_lambda, v_a_w_out, v_kv_norm, v_w_kv, v_b_norm, v_b_w_in, v_b_w_out, v_final_norm):
    weights = dict(a_norm=a_norm, a_w_in=a_w_in, a_conv_w=a_conv_w, a_conv_b=a_conv_b, a_w_r=a_w_r, a_b_r=a_b_r,
                   a_w_i=a_w_i, a_b_i=a_b_i, a_lambda=a_lambda, a_w_out=a_w_out, kv_norm=kv_norm, w_kv=w_kv,
                   b_norm=b_norm, b_w_in=b_w_in, b_w_out=b_w_out, final_norm=final_norm)
    mom1 = dict(a_norm=m_a_norm, a_w_in=m_a_w_in, a_conv_w=m_a_conv_w, a_conv_b=m_a_conv_b, a_w_r=m_a_w_r,
                a_b_r=m_a_b_r, a_w_i=m_a_w_i, a_b_i=m_a_b_i, a_lambda=m_a_lambda, a_w_out=m_a_w_out,
                kv_norm=m_kv_norm, w_kv=m_w_kv, b_norm=m_b_norm, b_w_in=m_b_w_in, b_w_out=m_b_w_out,
                final_norm=m_final_norm)
    mom2 = dict(a_norm=v_a_norm, a_w_in=v_a_w_in, a_conv_w=v_a_conv_w, a_conv_b=v_a_conv_b, a_w_r=v_a_w_r,
                a_b_r=v_a_b_r, a_w_i=v_a_w_i, a_b_i=v_a_b_i, a_lambda=v_a_lambda, a_w_out=v_a_w_out,
                kv_norm=v_kv_norm, w_kv=v_w_kv, b_norm=v_b_norm, b_w_in=v_b_w_in, b_w_out=v_b_w_out,
                final_norm=v_final_norm)
    order = list(weights)
    x0 = x[0]
    target = loss_target[0]
    d = x0.shape[1]
    chip = 2 * lax.axis_index("x") + lax.axis_index("y")

    big = ["a_w_in", "a_w_r", "a_w_i", "a_w_out", "w_kv", "b_w_in", "b_w_out"]
    big_axis = dict(a_w_in=1, a_w_r=1, a_w_i=1, a_w_out=0, w_kv=1, b_w_in=1, b_w_out=0)
    local = dict(a_w_in=a_w_in[0], a_w_r=a_w_r[0], a_w_i=a_w_i[0], a_w_out=a_w_out[0], w_kv=w_kv,
                 b_w_in=b_w_in[0], b_w_out=b_w_out[0])
    shards = {n: local[n].astype(BF16) for n in big}
    first = ["a_w_in", "a_w_r", "a_w_i"]
    full = exchange("gather_halves", [shards[n] for n in first], [big_axis[n] for n in first], name="gather_first")
    full += exchange("gather", [a_conv_w[0], b_norm], [1, 1], name="gather_small")
    wf = dict(zip(first + ["a_conv_w", "b_norm"], full))
    wf.update(a_norm=a_norm, a_conv_b=a_conv_b, a_b_r=a_b_r, a_b_i=a_b_i, a_lambda=a_lambda,
              kv_norm=kv_norm.reshape(1, d), final_norm=final_norm.reshape(1, d))
    loss_part, grad_x, parts, gsmall = _local_grads(x0, target, wf, shards=shards, axes=big_axis)

    sums = [sum_parts(parts[n].reshape(4, *_as2d(parts[n][0]).shape), name="sum_" + n) for n in big]
    others = swap_cores(sums, name="swap_cores")

    small = ["a_norm", "a_conv_b", "a_b_r", "a_b_i", "a_lambda", "kv_norm", "final_norm", "a_conv_w", "b_norm"]
    *red, loss_sum = allreduce_small([gsmall[n] for n in small] + [loss_part], name="allreduce_small")
    gs = dict(zip(small, red))
    loss = loss_sum[0, 0]
    n_conv = a_conv_w.shape[2]
    gs["a_conv_w"] = lax.dynamic_slice_in_dim(gs["a_conv_w"], chip * n_conv, n_conv, axis=1)
    n_bn = b_norm.shape[1]
    gs["b_norm"] = lax.dynamic_slice_in_dim(gs["b_norm"], chip * n_bn, n_bn, axis=1)

    grads, deltas, new_m, new_v = {}, {}, {}, {}
    for n, s_mine, s_other in zip(big, sums, others):
        shp = weights[n].shape
        g, dlt, mn, vn = adamw(_as2d(weights[n]), [s_mine, s_other], _as2d(mom1[n]), _as2d(mom2[n]),
                               name="adamw_" + n)
        grads[n], deltas[n], new_m[n], new_v[n] = (t.reshape(shp) for t in (g, dlt, mn, vn))
    as_g = lambda src: [src[n].reshape(gs[n].shape) for n in small]
    outs = adamw_small(as_g(weights), [gs[n] for n in small], as_g(mom1), as_g(mom2), name="adamw_small")
    for dst, vals in zip((deltas, new_m, new_v), outs):
        for n, val in zip(small, vals):
            dst[n] = val.reshape(weights[n].shape)
    for n in small:
        grads[n] = gs[n].reshape(weights[n].shape)

    return (loss, grad_x[None], *[grads[n] for n in order], *[deltas[n] for n in order],
            *[new_m[n] for n in order], *[new_v[n] for n in order])


def _local_grads(x0, target, wf, shards=None, axes=None):
    a_norm, a_conv_b, a_b_r, a_b_i, a_lambda = (wf[n] for n in ("a_norm", "a_conv_b", "a_b_r", "a_b_i", "a_lambda"))
    kv_norm, final_norm = wf["kv_norm"], wf["final_norm"]
    wf = dict(wf)
    parts = {}

    def mm(*args, gather=(), scatter=None, **kw):
        if shards is None or not (gather or scatter):
            return matmul(*args, **kw)
        if gather:
            out, got = matmul(*args, exchange=("gather_halves", [shards[n] for n in gather], [axes[n] for n in gather]),
                              **kw)
            wf.update(zip(gather, got))
        else:
            out, got = matmul(*args, exchange=("scatter", list(scatter.values()), [axes[n] for n in scatter]), **kw)
            parts.update(zip(scatter, got))
        return out

    (h_a,) = rms_fwd(x0, [a_norm], name="norm_a")
    proj_a = mm(h_a, wf["a_w_in"], gather=("a_w_out",), name="a_in")
    m_a, hst, *lru_saved = lru_fwd(proj_a, wf["a_conv_w"], a_conv_b, wf["a_w_r"], a_b_r, wf["a_w_i"], a_b_i,
                                   a_lambda, name="lru_fwd")
    x1 = mm(m_a, wf["a_w_out"], residual=x0, gather=("w_kv",), name="a_out")
    kvn, hb = rms_fwd(x1, [kv_norm, wf["b_norm"]], name="norm_kv_b")
    kv = mm(kvn, wf["w_kv"], out_dtype=BF16, gather=("b_w_in",), name="kv_proj")
    proj_b = mm(hb, wf["b_w_in"], gather=("b_w_out",), name="b_in")
    m_b, o, saved = attn_fwd(proj_b, kv, name="attn_fwd")
    x2 = mm(m_b, wf["b_w_out"], residual=x1, name="b_out")
    loss_part, g_final, dx2, dx2b = loss_bwd(x2, target, final_norm, name="loss_bwd")

    dm_b = mm(dx2b, wf["b_w_out"], tb=True, name="b_out_dx")
    g_b_w_out = mm(m_b, dx2b, ta=True, out_dtype=BF16, name="b_out_dw")
    dq, dgate_b, dk, dv = attn_bwd(proj_b, dm_b, o, kv, saved, name="attn_bwd")
    dproj_b = (dq, dgate_b)
    dkv = (dk, dv)
    g_b_w_in = mm(hb, dproj_b, ta=True, out_dtype=BF16, scatter=dict(b_w_out=g_b_w_out), name="b_in_dw")
    g_w_kv = mm(kvn, dkv, ta=True, out_dtype=BF16, scatter=dict(b_w_in=g_b_w_in), name="kv_dw")
    dhb = mm(dproj_b, wf["b_w_in"], tb=True, scatter=dict(w_kv=g_w_kv), name="b_in_dx")
    dkvn = mm(dkv, wf["w_kv"], tb=True, name="kv_dx")
    dx1, dx1b, (g_kv_norm, g_b_norm) = rms_bwd(
        x1, dx2, [(kv_norm, dkvn), (wf["b_norm"], dhb)], name="norm_kv_b_bwd")

    g_a_w_out = mm(m_a, dx1b, ta=True, out_dtype=BF16, name="a_out_dw")
    dm_a = mm(dx1b, wf["a_w_out"], tb=True, scatter=dict(a_w_out=g_a_w_out), name="a_out_dx")
    dxpre, dgate_a, g_conv_w, g_conv_b, g_w_r, g_b_r, g_w_i, g_b_i, g_lambda = lru_bwd(
        proj_a, hst, lru_saved, dm_a, wf["a_conv_w"], wf["a_w_r"], wf["a_w_i"], a_lambda, name="lru_bwd")
    dproj_a = (dxpre, dgate_a)
    g_w_r, g_w_i = g_w_r.astype(BF16), g_w_i.astype(BF16)
    g_a_w_in = mm(h_a, dproj_a, ta=True, out_dtype=BF16, scatter=dict(a_w_r=g_w_r, a_w_i=g_w_i), name="a_in_dw")
    dh_a = mm(dproj_a, wf["a_w_in"], tb=True, scatter=dict(a_w_in=g_a_w_in), name="a_in_dx")
    grad_x, _, (g_a_norm,) = rms_bwd(x0, dx1, [(a_norm, dh_a)], name="norm_a_bwd")

    gbig = parts if shards is not None else dict(
        a_w_in=g_a_w_in, a_w_r=g_w_r, a_w_i=g_w_i, a_w_out=g_a_w_out, w_kv=g_w_kv, b_w_in=g_b_w_in, b_w_out=g_b_w_out)
    gsmall = dict(a_norm=g_a_norm, a_conv_b=g_conv_b, a_b_r=g_b_r, a_b_i=g_b_i, a_lambda=g_lambda,
                  kv_norm=g_kv_norm, final_norm=g_final, a_conv_w=g_conv_w, b_norm=g_b_norm)
    return loss_part, grad_x, gbig, gsmall
```

```python
import math

import jax
import jax.numpy as jnp
from jax import lax
from jax.experimental import pallas as pl
from jax.experimental.pallas import tpu as pltpu

F32 = jnp.float32
BF16 = jnp.bfloat16
MESH = pl.DeviceIdType.MESH

EPS = 1e-6
LRU_C = 8.0
CONV_W = 4
HEAD_DIM = 128
ADAM_LR = 0.001
ADAM_B1 = 0.9
ADAM_B2 = 0.999
ADAM_EPS = 1e-08
ADAM_WD = 0.01
ADAM_STEP = 10

V7X_VMEM_LIMIT = 56 * 1024 * 1024
V7X_VMEM_LIMIT_HIGH = 60 * 1024 * 1024
LANES = 128
SUBLANES = 8
ATT_BLOCK = 256
ATT_QTILES = 4
ATT_HEADS = 2
ATT_FETCH_AHEAD = 2
LOG2E = 1.4426950408889634
LOG_ZERO = -1e30
SCAN_UNROLL = 4


def _pick(dim, cands):
    for c in cands:
        if dim % c == 0:
            return c
    return dim


def _params(sem, vmem=V7X_VMEM_LIMIT):
    return pltpu.CompilerParams(dimension_semantics=sem, vmem_limit_bytes=vmem)


def _sigmoid(x):
    return 1.0 / (1.0 + jnp.exp(-x))


def _place():
    return lax.axis_index("x"), lax.axis_index("y"), lax.axis_index("c")


def _chip_peers(x, y, c):
    return [(1 - x, y, c), (x, 1 - y, c), (1 - x, 1 - y, c)]


def _shard_of(ref, axis, idx, n):
    start = pl.multiple_of(idx * n, n)
    sl = [slice(None)] * len(ref.shape)
    sl[axis] = pl.ds(start, n)
    return ref.at[tuple(sl)]


def _half_rows(ref, h):
    n = ref.shape[0] // 2
    return ref.at[pl.ds(pl.multiple_of(h * n, n), n)]


def _block_half(ref, axis, idx, n, h):
    if axis == 0:
        return ref.at[pl.ds(pl.multiple_of(idx * n + h * (n // 2), n // 2), n // 2)]
    return _half_rows(_shard_of(ref, axis, idx, n), h)


def _exchange(kind, in_refs, out_refs, axes, send_sems, recv_sems, local_sems, send2_sems=None, recv2_sems=None):
    x, y, c = _place()
    me = 2 * x + y
    peers = _chip_peers(x, y, c)
    sibling = (x, y, 1 - c)
    triples = []
    nothing = lambda: None
    for ai, (src, dst, ax) in enumerate(zip(in_refs, out_refs, axes)):
        if kind == "scatter":
            n = dst.shape[1 + ax]
            loc = pltpu.make_async_copy(_shard_of(src, ax, me, n), dst.at[0], local_sems.at[ai])
        else:
            n = src.shape[ax]
            mine = _shard_of(dst, ax, me, n)
            loc = pltpu.make_async_copy(src, mine, local_sems.at[ai])
        triples.append((loc.start, nothing, loc.wait))
        for k, peer in enumerate(peers):
            sem = dict(send_sem=send_sems.at[ai * 3 + k], recv_sem=recv_sems.at[ai * 3 + k],
                       device_id=peer, device_id_type=MESH)
            theirs = 2 * peer[0] + peer[1]
            if kind == "gather":
                snd = pltpu.make_async_remote_copy(src_ref=src, dst_ref=mine, **sem)
                rcv = pltpu.make_async_remote_copy(src_ref=src, dst_ref=_shard_of(dst, ax, theirs, n), **sem)
                triples.append((snd.start, nothing, lambda snd=snd, rcv=rcv: (snd.wait_send(), rcv.wait_recv())))
            elif kind == "scatter":
                snd = pltpu.make_async_remote_copy(src_ref=_shard_of(src, ax, theirs, n), dst_ref=dst.at[1 + k], **sem)
                triples.append((snd.start, nothing, snd.wait))
            else:
                landed = _block_half(dst, ax, theirs, n, c)
                snd = pltpu.make_async_remote_copy(src_ref=_half_rows(src, c), dst_ref=_block_half(dst, ax, me, n, c), **sem)
                rcv = pltpu.make_async_remote_copy(src_ref=_half_rows(src, c), dst_ref=landed, **sem)
                sem2 = dict(send_sem=send2_sems.at[ai * 3 + k], recv_sem=recv2_sems.at[ai * 3 + k],
                            device_id=sibling, device_id_type=MESH)
                fwd = pltpu.make_async_remote_copy(src_ref=landed, dst_ref=landed, **sem2)
                got = pltpu.make_async_remote_copy(src_ref=landed, dst_ref=_block_half(dst, ax, theirs, n, 1 - c), **sem2)
                triples.append((snd.start, lambda rcv=rcv, fwd=fwd: (rcv.wait_recv(), fwd.start()),
                                lambda snd=snd, fwd=fwd, got=got: (snd.wait_send(), fwd.wait_send(), got.wait_recv())))
    return triples


def _exchange_shapes(kind, arrays, axes):
    out = []
    for arr, ax in zip(arrays, axes):
        shp = list(arr.shape)
        if kind == "scatter":
            shp[ax] //= 4
            out.append(jax.ShapeDtypeStruct((4, *shp), arr.dtype))
        else:
            shp[ax] *= 4
            out.append(jax.ShapeDtypeStruct(tuple(shp), arr.dtype))
    return out


def _exchange_sems(kind, n):
    sems = [pltpu.SemaphoreType.DMA((3 * n,)), pltpu.SemaphoreType.DMA((3 * n,)), pltpu.SemaphoreType.DMA((n,))]
    if kind == "gather_halves":
        sems += [pltpu.SemaphoreType.DMA((3 * n,)), pltpu.SemaphoreType.DMA((3 * n,))]
    return sems


def matmul(a, b, *, ta=False, tb=False, out_dtype=F32, residual=None, exchange=None, name):
    a_pair = a if isinstance(a, (tuple, list)) else None
    b_pair = b if isinstance(b, (tuple, list)) else None
    assert not (a_pair and ta) and not (b_pair and tb) and not (a_pair and b_pair)
    a0 = a_pair[0] if a_pair else a
    b0 = b_pair[0] if b_pair else b
    m = a0.shape[1] if ta else a0.shape[0]
    kdim = (a0.shape[0] if ta else a0.shape[1]) * (2 if a_pair else 1)
    n = (b0.shape[0] if tb else b0.shape[1]) * (2 if b_pair else 1)
    assert (b0.shape[1] if tb else b0.shape[0]) == kdim
    tm = _pick(m, (1024, 640, 512, 256, 128))
    tn = _pick(n // 2 if b_pair else n, (1024, 1280, 640, 512, 256, 128))
    tk = _pick(kdim // 2 if a_pair else kdim, (2560, 2048, 1024, 512, 256, 128))
    grid = (m // tm, n // tn, kdim // tk)
    nk = grid[2]
    kh, jh = nk // 2, grid[1] // 2
    dn = (((0 if ta else 1,), (1 if tb else 0,)), ((), ()))
    na = 2 if a_pair else 1
    nb = 2 if b_pair else 1
    nres = 0 if residual is None else 1
    nex = 0 if exchange is None else len(exchange[1])

    def body(*refs):
        a_refs, b_refs = refs[:na], refs[na:na + nb]
        p = na + nb
        r_ref = refs[p] if nres else None
        ex_in = refs[p + nres:p + nres + nex]
        o_ref = refs[p + nres + nex]
        ex_out = refs[p + 1 + nres + nex:p + 1 + nres + 2 * nex]
        acc = refs[p + 1 + nres + 2 * nex]
        sems = refs[p + 2 + nres + 2 * nex:]
        i, j, k = pl.program_id(0), pl.program_id(1), pl.program_id(2)
        if nex:
            @pl.when((i == 0) & (j == 0) & (k == 0))
            def _():
                for start, _, _ in _exchange(exchange[0], ex_in, ex_out, exchange[2], *sems):
                    start()

        @pl.when(k == 0)
        def _():
            acc[...] = jnp.zeros_like(acc)

        def accumulate(a_ref, b_ref):
            acc[...] += lax.dot_general(a_ref[...].astype(BF16), b_ref[...].astype(BF16), dn,
                                        preferred_element_type=F32)

        if a_pair:
            pl.when(k < kh)(lambda: accumulate(a_refs[0], b_refs[0]))
            pl.when(k >= kh)(lambda: accumulate(a_refs[1], b_refs[0]))
        elif b_pair:
            pl.when(j < jh)(lambda: accumulate(a_refs[0], b_refs[0]))
            pl.when(j >= jh)(lambda: accumulate(a_refs[0], b_refs[1]))
        else:
            accumulate(a_refs[0], b_refs[0])

        @pl.when(k == nk - 1)
        def _():
            r = acc[...]
            if r_ref is not None:
                r = r + r_ref[...]
            o_ref[...] = r.astype(out_dtype)

        if nex:
            @pl.when((i == grid[0] - 1) & (j == grid[1] - 1) & (k == nk - 1))
            def _():
                triples = _exchange(exchange[0], ex_in, ex_out, exchange[2], *sems)
                for _, relay, _ in triples:
                    relay()
                for _, _, finish in triples:
                    finish()

    if a_pair:
        a_specs = [pl.BlockSpec((tm, tk), lambda i, j, k: (i, jnp.minimum(k, kh - 1))),
                   pl.BlockSpec((tm, tk), lambda i, j, k: (i, jnp.maximum(k - kh, 0)))]
    else:
        a_specs = [pl.BlockSpec((tk, tm), lambda i, j, k: (k, i)) if ta
                   else pl.BlockSpec((tm, tk), lambda i, j, k: (i, k))]
    if b_pair:
        b_specs = [pl.BlockSpec((tk, tn), lambda i, j, k: (jnp.where(j < jh, k, nk - 1), jnp.minimum(j, jh - 1))),
                   pl.BlockSpec((tk, tn), lambda i, j, k: (jnp.where(j >= jh, k, 0), jnp.maximum(j - jh, 0)))]
    else:
        b_specs = [pl.BlockSpec((tn, tk), lambda i, j, k: (j, k)) if tb
                   else pl.BlockSpec((tk, tn), lambda i, j, k: (k, j))]
    o_spec = pl.BlockSpec((tm, tn), lambda i, j, k: (i, j))
    hbm = pl.BlockSpec(memory_space=pl.ANY)
    in_specs = a_specs + b_specs + [o_spec] * nres + [hbm] * nex
    args = (list(a_pair) if a_pair else [a]) + (list(b_pair) if b_pair else [b])
    args += ([residual] if nres else []) + (list(exchange[1]) if nex else [])
    out_shape = [jax.ShapeDtypeStruct((m, n), out_dtype)]
    scratch = [pltpu.VMEM((tm, tn), F32)]
    if nex:
        out_shape += _exchange_shapes(*exchange)
        scratch += _exchange_sems(exchange[0], nex)
    outs = pl.pallas_call(
        body, name=name, grid=grid,
        in_specs=in_specs, out_specs=[o_spec] + [hbm] * nex, out_shape=out_shape,
        scratch_shapes=scratch,
        compiler_params=_params(("arbitrary",) * 3 if nex else ("parallel", "parallel", "arbitrary")),
    )(*args)
    return (outs[0], list(outs[1:])) if nex else outs[0]


def rms_fwd(x, gains, *, name):
    s, d = x.shape
    tr = _pick(s, (512, 256, 128, 8))
    ng = len(gains)

    def body(*refs):
        x_ref = refs[0]
        g_refs = refs[1:1 + ng]
        o_refs = refs[1 + ng:]
        xv = x_ref[...]
        y = xv * lax.rsqrt(jnp.mean(xv * xv, axis=-1, keepdims=True) + EPS)
        for g_ref, o_ref in zip(g_refs, o_refs):
            o_ref[...] = (y * g_ref[...]).astype(BF16)

    row = pl.BlockSpec((tr, d), lambda i: (i, 0))
    vec = pl.BlockSpec((1, d), lambda i: (0, 0))
    return pl.pallas_call(
        body, name=name, grid=(s // tr,),
        in_specs=[row] + [vec] * ng, out_specs=[row] * ng,
        out_shape=[jax.ShapeDtypeStruct((s, d), BF16)] * ng,
        compiler_params=_params(("parallel",)),
    )(x, *gains)


def rms_bwd(x, dres, norms, *, name):
    s, d = x.shape
    tr = _pick(s, (256, 128, 8))
    ng = len(norms)

    def body(*refs):
        x_ref, dres_ref = refs[0], refs[1]
        g_refs = refs[2:2 + ng]
        dh_refs = refs[2 + ng:2 + 2 * ng]
        dx_ref, dxb_ref = refs[2 + 2 * ng], refs[3 + 2 * ng]
        dg_refs = refs[4 + 2 * ng:]
        i = pl.program_id(0)
        xv = x_ref[...]
        r = lax.rsqrt(jnp.mean(xv * xv, axis=-1, keepdims=True) + EPS)
        xhat = xv * r
        dx = dres_ref[...]
        for g_ref, dh_ref, dg_ref in zip(g_refs, dh_refs, dg_refs):
            dh = dh_ref[...]
            part = jnp.sum(dh * xhat, axis=0, keepdims=True)

            @pl.when(i == 0)
            def _():
                dg_ref[...] = part

            @pl.when(i > 0)
            def _():
                dg_ref[...] += part

            dxhat = dh * g_ref[...]
            dx = dx + r * (dxhat - xhat * jnp.mean(dxhat * xhat, axis=-1, keepdims=True))
        dx_ref[...] = dx
        dxb_ref[...] = dx.astype(BF16)

    row = pl.BlockSpec((tr, d), lambda i: (i, 0))
    vec = pl.BlockSpec((1, d), lambda i: (0, 0))
    outs = pl.pallas_call(
        body, name=name, grid=(s // tr,),
        in_specs=[row, row] + [vec] * ng + [row] * ng,
        out_specs=[row, row] + [vec] * ng,
        out_shape=[jax.ShapeDtypeStruct((s, d), F32), jax.ShapeDtypeStruct((s, d), BF16)]
        + [jax.ShapeDtypeStruct((1, d), F32)] * ng,
        compiler_params=_params(("arbitrary",)),
    )(x, dres, *[g for g, _ in norms], *[dh for _, dh in norms])
    return outs[0], outs[1], list(outs[2:])


def loss_bwd(x2, target, gain, *, name):
    s, d = x2.shape
    tr = _pick(s, (256, 128, 8))
    nsteps = s // tr

    def body(x_ref, t_ref, g_ref, loss_ref, dg_ref, dx_ref, dxb_ref, sq_acc):
        i = pl.program_id(0)
        xv = x_ref[...]
        r = lax.rsqrt(jnp.mean(xv * xv, axis=-1, keepdims=True) + EPS)
        xhat = xv * r
        g = g_ref[...]
        err = xhat * g - t_ref[...]
        dy = err * (1.0 / d)
        sq = jnp.sum(err * err, axis=0, keepdims=True)
        dgp = jnp.sum(dy * xhat, axis=0, keepdims=True)

        @pl.when(i == 0)
        def _():
            sq_acc[...] = sq
            dg_ref[...] = dgp

        @pl.when(i > 0)
        def _():
            sq_acc[...] += sq
            dg_ref[...] += dgp

        dxhat = dy * g
        dx = r * (dxhat - xhat * jnp.mean(dxhat * xhat, axis=-1, keepdims=True))
        dx_ref[...] = dx
        dxb_ref[...] = dx.astype(BF16)

        @pl.when(i == nsteps - 1)
        def _():
            tot = jnp.sum(sq_acc[...], axis=-1, keepdims=True) * (0.5 / d)
            loss_ref[...] = jnp.broadcast_to(tot, (1, LANES))

    row = pl.BlockSpec((tr, d), lambda i: (i, 0))
    vec = pl.BlockSpec((1, d), lambda i: (0, 0))
    return pl.pallas_call(
        body, name=name, grid=(nsteps,),
        in_specs=[row, row, vec],
        out_specs=[pl.BlockSpec((1, LANES), lambda i: (0, 0)), vec, row, row],
        out_shape=[jax.ShapeDtypeStruct((1, LANES), F32), jax.ShapeDtypeStruct((1, d), F32),
                   jax.ShapeDtypeStruct((s, d), F32), jax.ShapeDtypeStruct((s, d), BF16)],
        scratch_shapes=[pltpu.VMEM((1, d), F32)],
        compiler_params=_params(("arbitrary",)),
    )(x2, target, gain)


def _lru_gates(xb, wr, wi, br, bi, sp):
    xbb = xb.astype(BF16)
    r = _sigmoid(jnp.dot(xbb, wr, preferred_element_type=F32) + br)
    ig = _sigmoid(jnp.dot(xbb, wi, preferred_element_type=F32) + bi)
    log_a = (-LRU_C) * r * sp
    a = jnp.exp(log_a)
    mult = jnp.sqrt(jnp.maximum(-jnp.tanh(log_a) * (a * a + 1.0), 0.0))
    return r, ig, a, mult


def _softplus_neg(lam):
    e = jnp.exp(-jnp.abs(lam))
    sp = jnp.maximum(-lam, 0.0) + jnp.log(1.0 + e)
    sg = jnp.where(lam >= 0, e, 1.0) / (1.0 + e)
    return sp, sg


def _conv(pad_ref, w, b, t):
    acc = b + w[CONV_W - 1:CONV_W, :] * pad_ref[pl.ds(SUBLANES, t), :]
    for dlt in range(1, CONV_W):
        acc = acc + w[CONV_W - 1 - dlt:CONV_W - dlt, :] * pad_ref[pl.ds(SUBLANES - dlt, t), :]
    return acc


def _lru_specs(t, bw, nb, time_of):
    blk = lambda c0: pl.BlockSpec((t, bw), lambda n, i, c0=c0: (time_of(i), c0 + n))
    vec = pl.BlockSpec((1, bw), lambda n, i: (0, n))
    wspec = pl.BlockSpec((None, bw, bw), lambda n, i: (n, 0, 0))
    cwspec = pl.BlockSpec((CONV_W, bw), lambda n, i: (0, n))
    return blk, vec, wspec, cwspec


def lru_fwd(proj, conv_w, conv_b, w_r, b_r, w_i, b_i, lam, *, name):
    s, r2 = proj.shape
    rr = r2 // 2
    nb, bw, _ = w_r.shape
    t = _pick(s, (512, 256, 128, 64, 32))
    ngroups = t // SUBLANES

    def body(xp_ref, gate_ref, cw_ref, cb_ref, wr_ref, br_ref, wi_ref, bi_ref, lam_ref,
             m_ref, h_ref, xb_ref, r_ref, i_ref, a_ref, mult_ref, pad, hcarry, u_scr):
        i = pl.program_id(1)

        @pl.when(i == 0)
        def _():
            pad[0:SUBLANES, :] = jnp.zeros((SUBLANES, bw), F32)
            hcarry[...] = jnp.zeros_like(hcarry)

        xpre = xp_ref[...]
        pad[pl.ds(SUBLANES, t), :] = xpre
        xb = _conv(pad, cw_ref[...], cb_ref[...], t)
        pad[0:SUBLANES, :] = xpre[t - SUBLANES:, :]
        sp, _ = _softplus_neg(lam_ref[...])
        r, ig, a, mult = _lru_gates(xb, wr_ref[...], wi_ref[...], br_ref[...], bi_ref[...], sp)
        xb_ref[...] = xb
        r_ref[...] = r
        i_ref[...] = ig
        a_ref[...] = a
        mult_ref[...] = mult
        u_scr[...] = mult * (ig * xb)
        row = lax.broadcasted_iota(jnp.int32, (SUBLANES, bw), 0)

        def groups(gi, hprev):
            offs = [pl.multiple_of((gi * SCAN_UNROLL + u) * SUBLANES, SUBLANES) for u in range(SCAN_UNROLL)]
            scanned = []
            for off in offs:
                av = a_ref[pl.ds(off, SUBLANES), :]
                uv = u_scr[pl.ds(off, SUBLANES), :]
                for dlt in (1, 2, 4):
                    keep = row >= dlt
                    uv = jnp.where(keep, av * pltpu.roll(uv, dlt, 0) + uv, uv)
                    av = jnp.where(keep, av * pltpu.roll(av, dlt, 0), av)
                scanned.append((av, uv))
            for off, (av, uv) in zip(offs, scanned):
                hv = av * hprev + uv
                h_ref[pl.ds(off, SUBLANES), :] = hv
                hprev = hv[SUBLANES - 1:SUBLANES, :]
            return hprev

        hcarry[...] = lax.fori_loop(0, ngroups // SCAN_UNROLL, groups, hcarry[...])
        gate = gate_ref[...]
        m_ref[...] = (h_ref[...] * (gate * _sigmoid(gate))).astype(BF16)

    blk, vec, wspec, cwspec = _lru_specs(t, bw, nb, lambda i: i)
    return pl.pallas_call(
        body, name=name, grid=(nb, s // t),
        in_specs=[blk(0), blk(nb), cwspec, vec, wspec, vec, wspec, vec, vec],
        out_specs=[blk(0)] * 7,
        out_shape=[jax.ShapeDtypeStruct((s, rr), BF16)] + [jax.ShapeDtypeStruct((s, rr), F32)] * 6,
        scratch_shapes=[pltpu.VMEM((t + SUBLANES, bw), F32), pltpu.VMEM((1, bw), F32), pltpu.VMEM((t, bw), F32)],
        compiler_params=_params(("parallel", "arbitrary")),
    )(proj, proj, conv_w, conv_b, w_r, b_r, w_i, b_i, lam)


def lru_bwd(proj, hst, saved, dm, conv_w, w_r, w_i, lam, *, name):
    s, r2 = proj.shape
    rr = r2 // 2
    nb, bw, _ = w_r.shape
    t = _pick(s, (512, 256, 128, 64, 32))
    nt = s // t
    ngroups = t // SUBLANES
    nt_dims = (((1,), (1,)), ((), ()))
    tn_dims = (((0,), (0,)), ((), ()))

    def body(xp_ref, xhalo_ref, gate_ref, h_ref, hhalo_ref, xb_ref, r_ref, i_ref, a_ref, mult_ref, dm_ref,
             cw_ref, wr_ref, wi_ref, lam_ref,
             dxp_ref, dgate_ref, dcw_ref, dcb_ref, dwr_ref, dbr_ref, dwi_ref, dbi_ref, dlam_ref,
             pad, hpad, dpad, ecarry, b_scr, d_scr):
        step = pl.program_id(1)

        @pl.when(step == 0)
        def _():
            dpad[pl.ds(t, SUBLANES), :] = jnp.zeros((SUBLANES, bw), F32)
            ecarry[...] = jnp.zeros_like(ecarry)
            dcw_ref[...] = jnp.zeros_like(dcw_ref)
            dcb_ref[...] = jnp.zeros_like(dcb_ref)
            dwr_ref[...] = jnp.zeros_like(dwr_ref)
            dbr_ref[...] = jnp.zeros_like(dbr_ref)
            dwi_ref[...] = jnp.zeros_like(dwi_ref)
            dbi_ref[...] = jnp.zeros_like(dbi_ref)
            dlam_ref[...] = jnp.zeros_like(dlam_ref)

        past = jnp.where(step == nt - 1, 0.0, 1.0)
        pad[0:SUBLANES, :] = xhalo_ref[...] * past
        pad[pl.ds(SUBLANES, t), :] = xp_ref[...]
        hpad[0:SUBLANES, :] = hhalo_ref[...] * past
        hpad[pl.ds(SUBLANES, t), :] = h_ref[...]
        cw = cw_ref[...]
        sp, sg = _softplus_neg(lam_ref[...])
        wr = wr_ref[...]
        wi = wi_ref[...]
        xb, r, ig, a, mult = xb_ref[...], r_ref[...], i_ref[...], a_ref[...], mult_ref[...]
        gate = gate_ref[...]
        sgate = _sigmoid(gate)
        dmv = dm_ref[...]
        dgate_ref[...] = (dmv * h_ref[...] * (sgate * (1.0 + gate * (1.0 - sgate)))).astype(BF16)
        dy = dmv * (gate * sgate)
        b_scr[...] = a * dy
        row = lax.broadcasted_iota(jnp.int32, (SUBLANES, bw), 0)

        def groups(gi, enext):
            offs = [pl.multiple_of((ngroups - 1 - gi * SCAN_UNROLL - u) * SUBLANES, SUBLANES)
                    for u in range(SCAN_UNROLL)]
            scanned = []
            for off in offs:
                av = a_ref[pl.ds(off, SUBLANES), :]
                bv = b_scr[pl.ds(off, SUBLANES), :]
                for dlt in (1, 2, 4):
                    keep = row < SUBLANES - dlt
                    bv = jnp.where(keep, av * pltpu.roll(bv, SUBLANES - dlt, 0) + bv, bv)
                    av = jnp.where(keep, av * pltpu.roll(av, SUBLANES - dlt, 0), av)
                scanned.append((av, bv))
            for off, (av, bv) in zip(offs, scanned):
                ev = av * enext + bv
                d_scr[pl.ds(off, SUBLANES), :] = jnp.where(row == SUBLANES - 1, enext,
                                                           pltpu.roll(ev, SUBLANES - 1, 0))
                enext = ev[0:1, :]
            return enext

        ecarry[...] = lax.fori_loop(0, ngroups // SCAN_UNROLL, groups, ecarry[...])
        dtot = dy + d_scr[...]
        da = dtot * hpad[pl.ds(SUBLANES - 1, t), :]
        dmult = dtot * (ig * xb)
        dlog_a = da * a - dmult * (a * a) / mult
        dr_pre = dlog_a * ((-LRU_C) * sp) * (r * (1.0 - r))
        di_pre = (dtot * mult * xb) * (ig * (1.0 - ig))
        dlam_ref[...] += jnp.sum(dlog_a * r, axis=0, keepdims=True) * (LRU_C * sg)
        dbr_ref[...] += jnp.sum(dr_pre, axis=0, keepdims=True)
        dbi_ref[...] += jnp.sum(di_pre, axis=0, keepdims=True)
        drb = dr_pre.astype(BF16)
        dib = di_pre.astype(BF16)
        xbb = xb.astype(BF16)
        dxb = (dtot * mult * ig
               + lax.dot_general(drb, wr, nt_dims, preferred_element_type=F32)
               + lax.dot_general(dib, wi, nt_dims, preferred_element_type=F32))
        dwr_ref[...] += lax.dot_general(xbb, drb, tn_dims, preferred_element_type=F32)
        dwi_ref[...] += lax.dot_general(xbb, dib, tn_dims, preferred_element_type=F32)
        dcb_ref[...] += jnp.sum(dxb, axis=0, keepdims=True)
        dpad[pl.ds(0, t), :] = dxb
        dxpre = cw[CONV_W - 1:CONV_W, :] * dxb
        dcw_ref[CONV_W - 1:CONV_W, :] += jnp.sum(dxb * pad[pl.ds(SUBLANES, t), :], axis=0, keepdims=True)
        for dlt in range(1, CONV_W):
            dxpre = dxpre + cw[CONV_W - 1 - dlt:CONV_W - dlt, :] * dpad[pl.ds(dlt, t), :]
            dcw_ref[CONV_W - 1 - dlt:CONV_W - dlt, :] += jnp.sum(
                dxb * pad[pl.ds(SUBLANES - dlt, t), :], axis=0, keepdims=True)
        dpad[pl.ds(t, SUBLANES), :] = dxb[0:SUBLANES, :]
        dxp_ref[...] = dxpre.astype(BF16)

    rev = lambda i: nt - 1 - i
    blk, vec, wspec, cwspec = _lru_specs(t, bw, nb, rev)
    halo = pl.BlockSpec((SUBLANES, bw), lambda n, i: (jnp.maximum(rev(i) * ngroups - 1, 0), n))
    return pl.pallas_call(
        body, name=name, grid=(nb, nt),
        in_specs=[blk(0), halo, blk(nb), blk(0), halo] + [blk(0)] * 6 + [cwspec, wspec, wspec, vec],
        out_specs=[blk(0), blk(0), cwspec, vec, wspec, vec, wspec, vec, vec],
        out_shape=[jax.ShapeDtypeStruct((s, rr), BF16), jax.ShapeDtypeStruct((s, rr), BF16),
                   jax.ShapeDtypeStruct((CONV_W, rr), F32), jax.ShapeDtypeStruct((1, rr), F32),
                   jax.ShapeDtypeStruct((nb, bw, bw), F32), jax.ShapeDtypeStruct((1, rr), F32),
                   jax.ShapeDtypeStruct((nb, bw, bw), F32), jax.ShapeDtypeStruct((1, rr), F32),
                   jax.ShapeDtypeStruct((1, rr), F32)],
        scratch_shapes=[pltpu.VMEM((t + SUBLANES, bw), F32), pltpu.VMEM((t + SUBLANES, bw), F32),
                        pltpu.VMEM((t + SUBLANES, bw), F32), pltpu.VMEM((1, bw), F32),
                        pltpu.VMEM((t, bw), F32), pltpu.VMEM((t, bw), F32)],
        compiler_params=_params(("parallel", "arbitrary")),
    )(proj, proj, proj, hst, hst, *saved, dm, conv_w, w_r, w_i, lam)


def _softplus(z):
    return jnp.maximum(z, 0.0) + jnp.log(1.0 + jnp.exp2(jnp.abs(z) * (-LOG2E)))


def _att_blocks(s):
    bk = ATT_BLOCK if s % ATT_BLOCK == 0 else s
    bq = ATT_QTILES * bk if s % (ATT_QTILES * bk) == 0 else bk
    return bk, bq


def _tile_base(i, r):
    return r * ((i * (i + 1)) // 2)


def attn_fwd(projb, kv, *, name):
    s, a2 = projb.shape
    a = a2 // 2
    nh = a // HEAD_DIM
    bk, bq = _att_blocks(s)
    r = bq // bk
    nq = s // bq
    ntiles = _tile_base(nq, r)
    scale = 1.0 / math.sqrt(HEAD_DIM)
    nt_dims = (((1,), (1,)), ((), ()))
    hp = ATT_HEADS if nh % ATT_HEADS == 0 else 1
    wd = hp * HEAD_DIM

    def body(q_ref, g_ref, k_ref, v_ref, m_ref, o_ref, saved_hbm, acc, stage, sems):
        hgrp, i = pl.program_id(0), pl.program_id(1)
        base = _tile_base(i, r)
        qb = (q_ref[...] * scale).astype(BF16)
        from_mat = (lax.broadcasted_iota(jnp.int32, (bk, bk), 0)
                    >= lax.broadcasted_iota(jnp.int32, (bk, bk), 1)).astype(BF16)
        rowi = lax.broadcasted_iota(jnp.int32, (bq, bk), 0)
        coli = lax.broadcasted_iota(jnp.int32, (bq, bk), 1)
        cols = [slice(hh * HEAD_DIM, (hh + 1) * HEAD_DIM) for hh in range(hp)]

        def save(slot, j):
            return pltpu.make_async_copy(stage.at[slot], saved_hbm.at[hgrp, base + j], sems.at[slot])

        def tile(j, n, carries, diag):
            r0 = 0 if diag is None else diag * bk
            live = slice(r0, bq)
            causal = None if diag is None else coli[live] < rowi[:bq - r0]
            slot = n % 2

            def free_slot():
                save(slot, 0).wait()

            if isinstance(n, int):
                if n >= 2:
                    free_slot()
            elif r >= 2:
                free_slot()
            else:
                pl.when(n >= 2)(free_slot)
            rows = pl.ds(pl.multiple_of(j * bk, bk), bk)
            zs = [lax.dot_general(qb[live, c], k_ref[rows, c], nt_dims, preferred_element_type=F32) for c in cols]
            sums, sigs = [], []
            for z in zs:
                sp = _softplus(z)
                sig = z - sp
                if causal is not None:
                    sp = jnp.where(causal, sp, 0.0)
                    sig = jnp.where(causal, sig, LOG_ZERO)
                sums.append(jnp.dot(sp.astype(BF16), from_mat, preferred_element_type=F32))
                sigs.append(sig.astype(BF16))
            out = []
            for hh in range(hp):
                w = jnp.exp(zs[hh] - sums[hh] - carries[hh][live])
                if causal is not None:
                    w = jnp.where(causal, w, 0.0)
                wb = w.astype(BF16)
                acc[live, cols[hh]] += jnp.dot(wb, v_ref[rows, cols[hh]], preferred_element_type=F32)
                stage[slot, 0, hh, live] = wb
                stage[slot, 1, hh, live] = sigs[hh]
                if r0:
                    stage[slot, :, hh, :r0] = jnp.zeros((2, r0, bk), BF16)
                grown = carries[hh][live] + sums[hh][:, 0:1]
                out.append(jnp.concatenate([carries[hh][:r0], grown], axis=0) if r0 else grown)
            save(slot, j).start()
            return tuple(out)

        acc[...] = jnp.zeros_like(acc)
        carries = tuple(jnp.zeros((bq, 1), F32) for _ in range(hp))
        for n, dg in enumerate(reversed(range(r))):
            carries = tile(r * i + dg, n, carries, dg)
        lax.fori_loop(0, r * i, lambda jj, c: tile(r * i - 1 - jj, r + jj, c, None), carries)
        ntile = r * (i + 1)
        for back in (1, 2):
            def drain(back=back):
                save((ntile - back) % 2, 0).wait()
            if r >= back:
                drain()
            else:
                pl.when(ntile >= back)(drain)
        o = acc[...]
        o_ref[...] = o
        gate = g_ref[...]
        m_ref[...] = (o * (gate * _sigmoid(gate))).astype(BF16)

    ng = nh // hp
    qspec = lambda c0: pl.BlockSpec((bq, wd), lambda h, i, c0=c0: (i, c0 + h))
    kspec = lambda c0: pl.BlockSpec((s, wd), lambda h, i, c0=c0: (0, c0 + h), pipeline_mode=pl.Buffered(1))
    hbm = pl.BlockSpec(memory_space=pl.ANY)
    saved = jax.ShapeDtypeStruct((ng, ntiles, 2, hp, bq, bk), BF16)
    return pl.pallas_call(
        body, name=name, grid=(ng, nq),
        in_specs=[qspec(0), qspec(ng), kspec(0), kspec(ng)],
        out_specs=[qspec(0), qspec(0), hbm],
        out_shape=[jax.ShapeDtypeStruct((s, a), BF16), jax.ShapeDtypeStruct((s, a), F32), saved],
        scratch_shapes=[pltpu.VMEM((bq, wd), F32), pltpu.VMEM((2, 2, hp, bq, bk), BF16),
                        pltpu.SemaphoreType.DMA((2,))],
        compiler_params=_params(("arbitrary", "arbitrary")),
    )(projb, projb, kv, kv)


def attn_bwd(projb, dm, o, kv, saved, *, name):
    s, a2 = projb.shape
    a = a2 // 2
    nh = a // HEAD_DIM
    bk, bq = _att_blocks(s)
    r = bq // bk
    nq = s // bq
    scale = 1.0 / math.sqrt(HEAD_DIM)
    nt_dims = (((1,), (1,)), ((), ()))
    tn_dims = (((0,), (0,)), ((), ()))
    hp = saved.shape[3]
    wd = hp * HEAD_DIM
    ahead = ATT_FETCH_AHEAD

    def body(q_ref, g_ref, dm_ref, o_ref, k_ref, v_ref, saved_hbm, dq_ref, dg_ref, dk_ref, dv_ref,
             dk_acc, dv_acc, dq_acc, stage, sems):
        hgrp, i = pl.program_id(0), pl.program_id(1)
        base = _tile_base(i, r)
        ntile = r * (i + 1)

        @pl.when(i == 0)
        def _():
            dk_acc[...] = jnp.zeros_like(dk_acc)
            dv_acc[...] = jnp.zeros_like(dv_acc)

        def fetch(j):
            slot = j % (ahead + 1)
            return pltpu.make_async_copy(saved_hbm.at[hgrp, base + j], stage.at[slot], sems.at[slot])

        for j0 in range(ahead):
            pl.when(j0 < ntile)(lambda j0=j0: fetch(j0).start())
        qb = (q_ref[...] * scale).astype(BF16)
        gate = g_ref[...]
        sgate = _sigmoid(gate)
        dmv = dm_ref[...]
        dob = (dmv * (gate * sgate)).astype(BF16)
        dg_ref[...] = (dmv * o_ref[...] * (sgate * (1.0 + gate * (1.0 - sgate)))).astype(BF16)
        upto_mat = (lax.broadcasted_iota(jnp.int32, (bk, bk), 0)
                    <= lax.broadcasted_iota(jnp.int32, (bk, bk), 1)).astype(BF16)
        cols = [slice(hh * HEAD_DIM, (hh + 1) * HEAD_DIM) for hh in range(hp)]
        dq_acc[...] = jnp.zeros_like(dq_acc)

        def tile(j, gcarries, r0=0, more=None):
            live = slice(r0, bq)
            slot = j % (ahead + 1)
            if more is None:
                pl.when(j + ahead < ntile)(lambda: fetch(j + ahead).start())
            elif more:
                fetch(j + ahead).start()
            fetch(j).wait()
            rows = pl.ds(pl.multiple_of(j * bk, bk), bk)
            dws = [lax.dot_general(dob[live, c], v_ref[rows, c], nt_dims, preferred_element_type=F32) for c in cols]
            gs, totals = [], []
            for hh in range(hp):
                wb = stage[slot, 0, hh, live]
                g = wb.astype(F32) * dws[hh]
                dv_acc[rows, cols[hh]] += lax.dot_general(wb, dob[live, cols[hh]], tn_dims,
                                                          preferred_element_type=F32)
                totals.append(jnp.dot(g.astype(BF16), upto_mat, preferred_element_type=F32))
                gs.append(g)
            out = []
            for hh in range(hp):
                dz = gs[hh] - (totals[hh] + gcarries[hh][live]) * jnp.exp(stage[slot, 1, hh, live]).astype(F32)
                dzb = dz.astype(BF16)
                dq_acc[live, cols[hh]] += jnp.dot(dzb, k_ref[rows, cols[hh]], preferred_element_type=F32)
                dk_acc[rows, cols[hh]] += lax.dot_general(dzb, qb[live, cols[hh]], tn_dims,
                                                          preferred_element_type=F32)
                grown = gcarries[hh][live] + totals[hh][:, bk - 1:bk]
                out.append(jnp.concatenate([gcarries[hh][:r0], grown], axis=0) if r0 else grown)
            return tuple(out)

        gcarries = lax.fori_loop(0, r * i, tile, tuple(jnp.zeros((bq, 1), F32) for _ in range(hp)))
        for dg in range(r):
            gcarries = tile(r * i + dg, gcarries, dg * bk, dg + ahead < r)
        dq_ref[...] = (dq_acc[...] * scale).astype(BF16)

        @pl.when(i == nq - 1)
        def _():
            dk_ref[...] = dk_acc[...].astype(BF16)
            dv_ref[...] = dv_acc[...].astype(BF16)

    ng = nh // hp
    once = pl.Buffered(1)
    qspec = lambda c0: pl.BlockSpec((bq, wd), lambda h, i, c0=c0: (i, c0 + h))
    kspec = lambda c0: pl.BlockSpec((s, wd), lambda h, i, c0=c0: (0, c0 + h), pipeline_mode=once)
    hbm = pl.BlockSpec(memory_space=pl.ANY)
    return pl.pallas_call(
        body, name=name, grid=(ng, nq),
        in_specs=[qspec(0), qspec(ng), qspec(0), qspec(0), kspec(0), kspec(ng), hbm],
        out_specs=[qspec(0), qspec(0), kspec(0), kspec(0)],
        out_shape=[jax.ShapeDtypeStruct((s, a), BF16)] * 4,
        scratch_shapes=[pltpu.VMEM((s, wd), F32), pltpu.VMEM((s, wd), F32), pltpu.VMEM((bq, wd), F32),
                        pltpu.VMEM((ahead + 1, 2, hp, bq, bk), BF16), pltpu.SemaphoreType.DMA((ahead + 1,))],
        compiler_params=_params(("arbitrary", "arbitrary"), vmem=V7X_VMEM_LIMIT_HIGH),
    )(projb, projb, dm, o, kv, kv, saved)


def _as2d(x):
    n = x.size
    cols = x.shape[-1]
    if cols % LANES != 0:
        cols = LANES
    return x.reshape(n // cols, cols)


def sum_parts(parts, *, name):
    p, rows, cols = parts.shape
    tr = _pick(rows, (512, 256, 128, 64, 32, 16))

    def body(p_ref, o_ref):
        acc = p_ref[0].astype(F32)
        for k in range(1, p):
            acc = acc + p_ref[k].astype(F32)
        o_ref[...] = acc

    return pl.pallas_call(
        body, name=name, grid=(rows // tr,),
        in_specs=[pl.BlockSpec((p, tr, cols), lambda i: (0, i, 0))],
        out_specs=pl.BlockSpec((tr, cols), lambda i: (i, 0)),
        out_shape=jax.ShapeDtypeStruct((rows, cols), F32),
        compiler_params=_params(("parallel",)),
    )(parts)


def adamw(w, g_parts, m, v, *, name):
    rows, cols = w.shape
    tr = _pick(rows, (128, 64, 32, 16, 8))
    np_ = len(g_parts)
    c1 = 1.0 / (1.0 - ADAM_B1 ** ADAM_STEP)
    c2 = 1.0 / (1.0 - ADAM_B2 ** ADAM_STEP)

    def body(*refs):
        w_ref, m_ref, v_ref = refs[0], refs[1], refs[2]
        g_refs = refs[3:3 + np_]
        go_ref, d_ref, mo_ref, vo_ref = refs[3 + np_:]
        g = g_refs[0][...]
        for gr in g_refs[1:]:
            g = g + gr[...]
        mn = ADAM_B1 * m_ref[...] + (1.0 - ADAM_B1) * g
        vn = ADAM_B2 * v_ref[...] + (1.0 - ADAM_B2) * (g * g)
        go_ref[...] = g
        mo_ref[...] = mn
        vo_ref[...] = vn
        d_ref[...] = (-ADAM_LR) * ((mn * c1) / (jnp.sqrt(vn * c2) + ADAM_EPS) + ADAM_WD * w_ref[...])

    spec = pl.BlockSpec((tr, cols), lambda i: (i, 0))
    return pl.pallas_call(
        body, name=name, grid=(rows // tr,),
        in_specs=[spec] * (3 + np_), out_specs=[spec] * 4,
        out_shape=[jax.ShapeDtypeStruct((rows, cols), F32)] * 4,
        compiler_params=_params(("parallel",)),
    )(w, m, v, *g_parts)


def exchange(kind, arrays, axes, *, name):
    na = len(arrays)
    hbm = pl.BlockSpec(memory_space=pl.ANY)

    def body(*refs):
        triples = _exchange(kind, refs[:na], refs[na:2 * na], axes, *refs[2 * na:])
        for step in range(3):
            for triple in triples:
                triple[step]()

    return pl.pallas_call(
        body, name=name, in_specs=[hbm] * na, out_specs=[hbm] * na,
        out_shape=_exchange_shapes(kind, arrays, axes), scratch_shapes=_exchange_sems(kind, na),
    )(*arrays)


def swap_cores(arrs, *, name):
    na = len(arrs)
    hbm = pl.BlockSpec(memory_space=pl.ANY)

    def body(*refs):
        a_refs = refs[:na]
        o_refs = refs[na:2 * na]
        send_sems, recv_sems = refs[2 * na:]
        x, y, c = _place()
        copies = []
        for ai in range(na):
            cp = pltpu.make_async_remote_copy(
                src_ref=a_refs[ai], dst_ref=o_refs[ai], send_sem=send_sems.at[ai], recv_sem=recv_sems.at[ai],
                device_id=(x, y, 1 - c), device_id_type=MESH)
            cp.start()
            copies.append(cp)
        for cp in copies:
            cp.wait()

    return pl.pallas_call(
        body, name=name,
        in_specs=[hbm] * na, out_specs=[hbm] * na,
        out_shape=[jax.ShapeDtypeStruct(a.shape, a.dtype) for a in arrs],
        scratch_shapes=[pltpu.SemaphoreType.DMA((na,)), pltpu.SemaphoreType.DMA((na,))],
    )(*arrs)


def allreduce_small(arrs, *, name):
    nar = len(arrs)
    width = max(a.shape[1] for a in arrs)
    starts, total = [], 0
    for a in arrs:
        starts.append(total)
        total += a.shape[0]
    total += (-total) % SUBLANES

    def body(*refs):
        in_refs, out_refs = refs[:nar], refs[nar:2 * nar]
        buf, slots, send_sems, recv_sems = refs[2 * nar:]
        x, y, c = _place()
        me = 4 * x + 2 * y + c
        buf[...] = jnp.zeros_like(buf)
        for ref, st in zip(in_refs, starts):
            buf[st:st + ref.shape[0], 0:ref.shape[1]] = ref[...]
        slots[0] = buf[...]
        copies = []
        for rel in range(1, 8):
            peer = (x ^ (rel >> 2), y ^ ((rel >> 1) & 1), c ^ (rel & 1))
            cp = pltpu.make_async_remote_copy(
                src_ref=buf, dst_ref=slots.at[rel], send_sem=send_sems.at[rel - 1],
                recv_sem=recv_sems.at[rel - 1], device_id=peer, device_id_type=MESH)
            cp.start()
            copies.append(cp)
        for cp in copies:
            cp.wait()
        acc = slots[me]
        for dev in range(1, 8):
            acc = acc + slots[dev ^ me]
        buf[...] = acc
        for ref, st in zip(out_refs, starts):
            ref[...] = buf[st:st + ref.shape[0], 0:ref.shape[1]]

    vm = pl.BlockSpec(memory_space=pltpu.VMEM)
    return pl.pallas_call(
        body, name=name, in_specs=[vm] * nar, out_specs=[vm] * nar,
        out_shape=[jax.ShapeDtypeStruct(a.shape, F32) for a in arrs],
        scratch_shapes=[pltpu.VMEM((total, width), F32), pltpu.VMEM((8, total, width), F32),
                        pltpu.SemaphoreType.DMA((7,)), pltpu.SemaphoreType.DMA((7,))],
    )(*arrs)


def adamw_small(ws, gs, ms, vs, *, name):
    n = len(ws)
    c1 = 1.0 / (1.0 - ADAM_B1 ** ADAM_STEP)
    c2 = 1.0 / (1.0 - ADAM_B2 ** ADAM_STEP)

    def body(*refs):
        w_refs, g_refs, m_refs, v_refs = (refs[k * n:(k + 1) * n] for k in range(4))
        d_refs, mo_refs, vo_refs = (refs[(4 + k) * n:(5 + k) * n] for k in range(3))
        for w_ref, g_ref, m_ref, v_ref, d_ref, mo_ref, vo_ref in zip(w_refs, g_refs, m_refs, v_refs,
                                                                     d_refs, mo_refs, vo_refs):
            g = g_ref[...]
            mn = ADAM_B1 * m_ref[...] + (1.0 - ADAM_B1) * g
            vn = ADAM_B2 * v_ref[...] + (1.0 - ADAM_B2) * (g * g)
            mo_ref[...] = mn
            vo_ref[...] = vn
            d_ref[...] = (-ADAM_LR) * ((mn * c1) / (jnp.sqrt(vn * c2) + ADAM_EPS) + ADAM_WD * w_ref[...])

    vm = pl.BlockSpec(memory_space=pltpu.VMEM)
    outs = pl.pallas_call(
        body, name=name, in_specs=[vm] * (4 * n), out_specs=[vm] * (3 * n),
        out_shape=[jax.ShapeDtypeStruct(w.shape, F32) for w in ws] * 3,
    )(*ws, *gs, *ms, *vs)
    return outs[:n], outs[n:2 * n], outs[2 * n:]


def kernel(x, a_norm, a_w_in, a_conv_w, a_conv_b, a_w_r, a_b_r, a_w_i, a_b_i, a_lambda, a_w_out, kv_norm, w_kv, b_norm, b_w_in, b_w_out, final_norm, loss_target, m_a_norm, m_a_w_in, m_a_conv_w, m_a_conv_b, m_a_w_r, m_a_b_r, m_a_w_i, m_a_b_i, m_a_lambda, m_a_w_out, m_kv_norm, m_w_kv, m_b_norm, m_b_w_in, m_b_w_out, m_final_norm, v_a_norm, v_a_w_in, v_a_conv_w, v_a_conv_b, v_a_w_r, v_a_b_r, v_a_w_i, v_a_b_i, v_a_lambda, v_a_w_out, v_kv_norm, v_w_kv, v_b_norm, v_b_w_in, v_b_w_out, v_final_norm):
    weights = dict(a_norm=a_norm, a_w_in=a_w_in, a_conv_w=a_conv_w, a_conv_b=a_conv_b, a_w_r=a_w_r, a_b_r=a_b_r,
                   a_w_i=a_w_i, a_b_i=a_b_i, a_lambda=a_lambda, a_w_out=a_w_out, kv_norm=kv_norm, w_kv=w_kv,
                   b_norm=b_norm, b_w_in=b_w_in, b_w_out=b_w_out, final_norm=final_norm)
    mom1 = dict(a_norm=m_a_norm, a_w_in=m_a_w_in, a_conv_w=m_a_conv_w, a_conv_b=m_a_conv_b, a_w_r=m_a_w_r,
                a_b_r=m_a_b_r, a_w_i=m_a_w_i, a_b_i=m_a_b_i, a_lambda=m_a_lambda, a_w_out=m_a_w_out,
                kv_norm=m_kv_norm, w_kv=m_w_kv, b_norm=m_b_norm, b_w_in=m_b_w_in, b_w_out=m_b_w_out,
                final_norm=m_final_norm)
    mom2 = dict(a_norm=v_a_norm, a_w_in=v_a_w_in, a_conv_w=v_a_conv_w, a_conv_b=v_a_conv_b, a_w_r=v_a_w_r,
                a_b_r=v_a_b_r, a_w_i=v_a_w_i, a_b_i=v_a_b_i, a_lambda=v_a_lambda, a_w_out=v_a_w_out,
                kv_norm=v_kv_norm, w_kv=v_w_kv, b_norm=v_b_norm, b_w_in=v_b_w_in, b_w_out=v_b_w_out,
                final_norm=v_final_norm)
    order = list(weights)
    x0 = x[0]
    target = loss_target[0]
    d = x0.shape[1]
    chip = 2 * lax.axis_index("x") + lax.axis_index("y")

    big = ["a_w_in", "a_w_r", "a_w_i", "a_w_out", "w_kv", "b_w_in", "b_w_out"]
    big_axis = dict(a_w_in=1, a_w_r=1, a_w_i=1, a_w_out=0, w_kv=1, b_w_in=1, b_w_out=0)
    local = dict(a_w_in=a_w_in[0], a_w_r=a_w_r[0], a_w_i=a_w_i[0], a_w_out=a_w_out[0], w_kv=w_kv,
                 b_w_in=b_w_in[0], b_w_out=b_w_out[0])
    shards = {n: local[n].astype(BF16) for n in big}
    first = ["a_w_in", "a_w_r", "a_w_i"]
    full = exchange("gather_halves", [shards[n] for n in first], [big_axis[n] for n in first], name="gather_first")
    full += exchange("gather", [a_conv_w[0], b_norm], [1, 1], name="gather_small")
    wf = dict(zip(first + ["a_conv_w", "b_norm"], full))
    wf.update(a_norm=a_norm, a_conv_b=a_conv_b, a_b_r=a_b_r, a_b_i=a_b_i, a_lambda=a_lambda,
              kv_norm=kv_norm.reshape(1, d), final_norm=final_norm.reshape(1, d))
    loss_part, grad_x, parts, gsmall = _local_grads(x0, target, wf, shards=shards, axes=big_axis)

    sums = [sum_parts(parts[n].reshape(4, *_as2d(parts[n][0]).shape), name="sum_" + n) for n in big]
    others = swap_cores(sums, name="swap_cores")

    small = ["a_norm", "a_conv_b", "a_b_r", "a_b_i", "a_lambda", "kv_norm", "final_norm", "a_conv_w", "b_norm"]
    *red, loss_sum = allreduce_small([gsmall[n] for n in small] + [loss_part], name="allreduce_small")
    gs = dict(zip(small, red))
    loss = loss_sum[0, 0]
    n_conv = a_conv_w.shape[2]
    gs["a_conv_w"] = lax.dynamic_slice_in_dim(gs["a_conv_w"], chip * n_conv, n_conv, axis=1)
    n_bn = b_norm.shape[1]
    gs["b_norm"] = lax.dynamic_slice_in_dim(gs["b_norm"], chip * n_bn, n_bn, axis=1)

    grads, deltas, new_m, new_v = {}, {}, {}, {}
    for n, s_mine, s_other in zip(big, sums, others):
        shp = weights[n].shape
        g, dlt, mn, vn = adamw(_as2d(weights[n]), [s_mine, s_other], _as2d(mom1[n]), _as2d(mom2[n]),
                               name="adamw_" + n)
        grads[n], deltas[n], new_m[n], new_v[n] = (t.reshape(shp) for t in (g, dlt, mn, vn))
    as_g = lambda src: [src[n].reshape(gs[n].shape) for n in small]
    outs = adamw_small(as_g(weights), [gs[n] for n in small], as_g(mom1), as_g(mom2), name="adamw_small")
    for dst, vals in zip((deltas, new_m, new_v), outs):
        for n, val in zip(small, vals):
            dst[n] = val.reshape(weights[n].shape)
    for n in small:
        grads[n] = gs[n].reshape(weights[n].shape)

    return (loss, grad_x[None], *[grads[n] for n in order], *[deltas[n] for n in order],
            *[new_m[n] for n in order], *[new_v[n] for n in order])


def _local_grads(x0, target, wf, shards=None, axes=None):
    a_norm, a_conv_b, a_b_r, a_b_i, a_lambda = (wf[n] for n in ("a_norm", "a_conv_b", "a_b_r", "a_b_i", "a_lambda"))
    kv_norm, final_norm = wf["kv_norm"], wf["final_norm"]
    wf = dict(wf)
    parts = {}

    def mm(*args, gather=(), scatter=None, **kw):
        if shards is None or not (gather or scatter):
            return matmul(*args, **kw)
        if gather:
            out, got = matmul(*args, exchange=("gather_halves", [shards[n] for n in gather], [axes[n] for n in gather]),
                              **kw)
            wf.update(zip(gather, got))
        else:
            out, got = matmul(*args, exchange=("scatter", list(scatter.values()), [axes[n] for n in scatter]), **kw)
            parts.update(zip(scatter, got))
        return out

    (h_a,) = rms_fwd(x0, [a_norm], name="norm_a")
    proj_a = mm(h_a, wf["a_w_in"], gather=("a_w_out",), name="a_in")
    m_a, hst, *lru_saved = lru_fwd(proj_a, wf["a_conv_w"], a_conv_b, wf["a_w_r"], a_b_r, wf["a_w_i"], a_b_i,
                                   a_lambda, name="lru_fwd")
    x1 = mm(m_a, wf["a_w_out"], residual=x0, gather=("w_kv",), name="a_out")
    kvn, hb = rms_fwd(x1, [kv_norm, wf["b_norm"]], name="norm_kv_b")
    kv = mm(kvn, wf["w_kv"], out_dtype=BF16, gather=("b_w_in",), name="kv_proj")
    proj_b = mm(hb, wf["b_w_in"], gather=("b_w_out",), name="b_in")
    m_b, o, saved = attn_fwd(proj_b, kv, name="attn_fwd")
    x2 = mm(m_b, wf["b_w_out"], residual=x1, name="b_out")
    loss_part, g_final, dx2, dx2b = loss_bwd(x2, target, final_norm, name="loss_bwd")

    dm_b = mm(dx2b, wf["b_w_out"], tb=True, name="b_out_dx")
    g_b_w_out = mm(m_b, dx2b, ta=True, out_dtype=BF16, name="b_out_dw")
    dq, dgate_b, dk, dv = attn_bwd(proj_b, dm_b, o, kv, saved, name="attn_bwd")
    dproj_b = (dq, dgate_b)
    dkv = (dk, dv)
    g_b_w_in = mm(hb, dproj_b, ta=True, out_dtype=BF16, scatter=dict(b_w_out=g_b_w_out), name="b_in_dw")
    g_w_kv = mm(kvn, dkv, ta=True, out_dtype=BF16, scatter=dict(b_w_in=g_b_w_in), name="kv_dw")
    dhb = mm(dproj_b, wf["b_w_in"], tb=True, scatter=dict(w_kv=g_w_kv), name="b_in_dx")
    dkvn = mm(dkv, wf["w_kv"], tb=True, name="kv_dx")
    dx1, dx1b, (g_kv_norm, g_b_norm) = rms_bwd(
        x1, dx2, [(kv_norm, dkvn), (wf["b_norm"], dhb)], name="norm_kv_b_bwd")

    g_a_w_out = mm(m_a, dx1b, ta=True, out_dtype=BF16, name="a_out_dw")
    dm_a = mm(dx1b, wf["a_w_out"], tb=True, scatter=dict(a_w_out=g_a_w_out), name="a_out_dx")
    dxpre, dgate_a, g_conv_w, g_conv_b, g_w_r, g_b_r, g_w_i, g_b_i, g_lambda = lru_bwd(
        proj_a, hst, lru_saved, dm_a, wf["a_conv_w"], wf["a_w_r"], wf["a_w_i"], a_lambda, name="lru_bwd")
    dproj_a = (dxpre, dgate_a)
    g_w_r, g_w_i = g_w_r.astype(BF16), g_w_i.astype(BF16)
    g_a_w_in = mm(h_a, dproj_a, ta=True, out_dtype=BF16, scatter=dict(a_w_r=g_w_r, a_w_i=g_w_i), name="a_in_dw")
    dh_a = mm(dproj_a, wf["a_w_in"], tb=True, scatter=dict(a_w_in=g_a_w_in), name="a_in_dx")
    grad_x, _, (g_a_norm,) = rms_bwd(x0, dx1, [(a_norm, dh_a)], name="norm_a_bwd")

    gbig = parts if shards is not None else dict(
        a_w_in=g_a_w_in, a_w_r=g_w_r, a_w_i=g_w_i, a_w_out=g_a_w_out, w_kv=g_w_kv, b_w_in=g_b_w_in, b_w_out=g_b_w_out)
    gsmall = dict(a_norm=g_a_norm, a_conv_b=g_conv_b, a_b_r=g_b_r, a_b_i=g_b_i, a_lambda=g_lambda,
                  kv_norm=g_kv_norm, final_norm=g_final, a_conv_w=g_conv_w, b_norm=g_b_norm)
    return loss_part, grad_x, gbig, gsmall
```

```python
import math

import jax
import jax.numpy as jnp
from jax import lax
from jax.experimental import pallas as pl
from jax.experimental.pallas import tpu as pltpu

F32 = jnp.float32
BF16 = jnp.bfloat16
MESH = pl.DeviceIdType.MESH

EPS = 1e-6
LRU_C = 8.0
CONV_W = 4
HEAD_DIM = 128
ADAM_LR = 0.001
ADAM_B1 = 0.9
ADAM_B2 = 0.999
ADAM_EPS = 1e-08
ADAM_WD = 0.01
ADAM_STEP = 10

V7X_VMEM_LIMIT = 56 * 1024 * 1024
V7X_VMEM_LIMIT_HIGH = 60 * 1024 * 1024
LANES = 128
SUBLANES = 8
ATT_BLOCK = 256
ATT_QTILES = 4
ATT_HEADS = 2
ATT_FETCH_AHEAD = 2
LOG2E = 1.4426950408889634
LOG_ZERO = -1e30
SCAN_UNROLL = 4


def _pick(dim, cands):
    for c in cands:
        if dim % c == 0:
            return c
    return dim


def _params(sem, vmem=V7X_VMEM_LIMIT):
    return pltpu.CompilerParams(dimension_semantics=sem, vmem_limit_bytes=vmem)


def _sigmoid(x):
    return 1.0 / (1.0 + jnp.exp(-x))


def _place():
    return lax.axis_index("x"), lax.axis_index("y"), lax.axis_index("c")


def _chip_peers(x, y, c):
    return [(1 - x, y, c), (x, 1 - y, c), (1 - x, 1 - y, c)]


def _shard_of(ref, axis, idx, n):
    start = pl.multiple_of(idx * n, n)
    sl = [slice(None)] * len(ref.shape)
    sl[axis] = pl.ds(start, n)
    return ref.at[tuple(sl)]


def _half_rows(ref, h):
    n = ref.shape[0] // 2
    return ref.at[pl.ds(pl.multiple_of(h * n, n), n)]


def _block_half(ref, axis, idx, n, h):
    if axis == 0:
        return ref.at[pl.ds(pl.multiple_of(idx * n + h * (n // 2), n // 2), n // 2)]
    return _half_rows(_shard_of(ref, axis, idx, n), h)


def _exchange(kind, in_refs, out_refs, axes, send_sems, recv_sems, local_sems, send2_sems=None, recv2_sems=None):
    x, y, c = _place()
    me = 2 * x + y
    peers = _chip_peers(x, y, c)
    sibling = (x, y, 1 - c)
    triples = []
    nothing = lambda: None
    for ai, (src, dst, ax) in enumerate(zip(in_refs, out_refs, axes)):
        if kind == "scatter":
            n = dst.shape[1 + ax]
            loc = pltpu.make_async_copy(_shard_of(src, ax, me, n), dst.at[0], local_sems.at[ai])
        else:
            n = src.shape[ax]
            mine = _shard_of(dst, ax, me, n)
            loc = pltpu.make_async_copy(src, mine, local_sems.at[ai])
        triples.append((loc.start, nothing, loc.wait))
        for k, peer in enumerate(peers):
            sem = dict(send_sem=send_sems.at[ai * 3 + k], recv_sem=recv_sems.at[ai * 3 + k],
                       device_id=peer, device_id_type=MESH)
            theirs = 2 * peer[0] + peer[1]
            if kind == "gather":
                snd = pltpu.make_async_remote_copy(src_ref=src, dst_ref=mine, **sem)
                rcv = pltpu.make_async_remote_copy(src_ref=src, dst_ref=_shard_of(dst, ax, theirs, n), **sem)
                triples.append((snd.start, nothing, lambda snd=snd, rcv=rcv: (snd.wait_send(), rcv.wait_recv())))
            elif kind == "scatter":
                snd = pltpu.make_async_remote_copy(src_ref=_shard_of(src, ax, theirs, n), dst_ref=dst.at[1 + k], **sem)
                triples.append((snd.start, nothing, snd.wait))
            else:
                landed = _block_half(dst, ax, theirs, n, c)
                snd = pltpu.make_async_remote_copy(src_ref=_half_rows(src, c), dst_ref=_block_half(dst, ax, me, n, c), **sem)
                rcv = pltpu.make_async_remote_copy(src_ref=_half_rows(src, c), dst_ref=landed, **sem)
                sem2 = dict(send_sem=send2_sems.at[ai * 3 + k], recv_sem=recv2_sems.at[ai * 3 + k],
                            device_id=sibling, device_id_type=MESH)
                fwd = pltpu.make_async_remote_copy(src_ref=landed, dst_ref=landed, **sem2)
                got = pltpu.make_async_remote_copy(src_ref=landed, dst_ref=_block_half(dst, ax, theirs, n, 1 - c), **sem2)
                triples.append((snd.start, lambda rcv=rcv, fwd=fwd: (rcv.wait_recv(), fwd.start()),
                                lambda snd=snd, fwd=fwd, got=got: (snd.wait_send(), fwd.wait_send(), got.wait_recv())))
    return triples


def _exchange_shapes(kind, arrays, axes):
    out = []
    for arr, ax in zip(arrays, axes):
        shp = list(arr.shape)
        if kind == "scatter":
            shp[ax] //= 4
            out.append(jax.ShapeDtypeStruct((4, *shp), arr.dtype))
        else:
            shp[ax] *= 4
            out.append(jax.ShapeDtypeStruct(tuple(shp), arr.dtype))
    return out


def _exchange_sems(kind, n):
    sems = [pltpu.SemaphoreType.DMA((3 * n,)), pltpu.SemaphoreType.DMA((3 * n,)), pltpu.SemaphoreType.DMA((n,))]
    if kind == "gather_halves":
        sems += [pltpu.SemaphoreType.DMA((3 * n,)), pltpu.SemaphoreType.DMA((3 * n,))]
    return sems


def matmul(a, b, *, ta=False, tb=False, out_dtype=F32, residual=None, exchange=None, name):
    a_pair = a if isinstance(a, (tuple, list)) else None
    b_pair = b if isinstance(b, (tuple, list)) else None
    assert not (a_pair and ta) and not (b_pair and tb) and not (a_pair and b_pair)
    a0 = a_pair[0] if a_pair else a
    b0 = b_pair[0] if b_pair else b
    m = a0.shape[1] if ta else a0.shape[0]
    kdim = (a0.shape[0] if ta else a0.shape[1]) * (2 if a_pair else 1)
    n = (b0.shape[0] if tb else b0.shape[1]) * (2 if b_pair else 1)
    assert (b0.shape[1] if tb else b0.shape[0]) == kdim
    tm = _pick(m, (1024, 1280, 640, 512, 256, 128))
    tn = _pick(n // 2 if b_pair else n, (1024, 1280, 640, 512, 256, 128))
    tk = _pick(kdim // 2 if a_pair else kdim, (2560, 2048, 1024, 512, 256, 128))
    grid = (m // tm, n // tn, kdim // tk)
    nk = grid[2]
    kh, jh = nk // 2, grid[1] // 2
    dn = (((0 if ta else 1,), (1 if tb else 0,)), ((), ()))
    na = 2 if a_pair else 1
    nb = 2 if b_pair else 1
    nres = 0 if residual is None else 1
    nex = 0 if exchange is None else len(exchange[1])

    def body(*refs):
        a_refs, b_refs = refs[:na], refs[na:na + nb]
        p = na + nb
        r_ref = refs[p] if nres else None
        ex_in = refs[p + nres:p + nres + nex]
        o_ref = refs[p + nres + nex]
        ex_out = refs[p + 1 + nres + nex:p + 1 + nres + 2 * nex]
        acc = refs[p + 1 + nres + 2 * nex]
        sems = refs[p + 2 + nres + 2 * nex:]
        i, j, k = pl.program_id(0), pl.program_id(1), pl.program_id(2)
        if nex:
            @pl.when((i == 0) & (j == 0) & (k == 0))
            def _():
                for start, _, _ in _exchange(exchange[0], ex_in, ex_out, exchange[2], *sems):
                    start()

        @pl.when(k == 0)
        def _():
            acc[...] = jnp.zeros_like(acc)

        def accumulate(a_ref, b_ref):
            acc[...] += lax.dot_general(a_ref[...].astype(BF16), b_ref[...].astype(BF16), dn,
                                        preferred_element_type=F32)

        if a_pair:
            pl.when(k < kh)(lambda: accumulate(a_refs[0], b_refs[0]))
            pl.when(k >= kh)(lambda: accumulate(a_refs[1], b_refs[0]))
        elif b_pair:
            pl.when(j < jh)(lambda: accumulate(a_refs[0], b_refs[0]))
            pl.when(j >= jh)(lambda: accumulate(a_refs[0], b_refs[1]))
        else:
            accumulate(a_refs[0], b_refs[0])

        @pl.when(k == nk - 1)
        def _():
            r = acc[...]
            if r_ref is not None:
                r = r + r_ref[...]
            o_ref[...] = r.astype(out_dtype)

        if nex:
            @pl.when((i == grid[0] - 1) & (j == grid[1] - 1) & (k == nk - 1))
            def _():
                triples = _exchange(exchange[0], ex_in, ex_out, exchange[2], *sems)
                for _, relay, _ in triples:
                    relay()
                for _, _, finish in triples:
                    finish()

    if a_pair:
        a_specs = [pl.BlockSpec((tm, tk), lambda i, j, k: (i, jnp.minimum(k, kh - 1))),
                   pl.BlockSpec((tm, tk), lambda i, j, k: (i, jnp.maximum(k - kh, 0)))]
    else:
        a_specs = [pl.BlockSpec((tk, tm), lambda i, j, k: (k, i)) if ta
                   else pl.BlockSpec((tm, tk), lambda i, j, k: (i, k))]
    if b_pair:
        b_specs = [pl.BlockSpec((tk, tn), lambda i, j, k: (jnp.where(j < jh, k, nk - 1), jnp.minimum(j, jh - 1))),
                   pl.BlockSpec((tk, tn), lambda i, j, k: (jnp.where(j >= jh, k, 0), jnp.maximum(j - jh, 0)))]
    else:
        b_specs = [pl.BlockSpec((tn, tk), lambda i, j, k: (j, k)) if tb
                   else pl.BlockSpec((tk, tn), lambda i, j, k: (k, j))]
    o_spec = pl.BlockSpec((tm, tn), lambda i, j, k: (i, j))
    hbm = pl.BlockSpec(memory_space=pl.ANY)
    in_specs = a_specs + b_specs + [o_spec] * nres + [hbm] * nex
    args = (list(a_pair) if a_pair else [a]) + (list(b_pair) if b_pair else [b])
    args += ([residual] if nres else []) + (list(exchange[1]) if nex else [])
    out_shape = [jax.ShapeDtypeStruct((m, n), out_dtype)]
    scratch = [pltpu.VMEM((tm, tn), F32)]
    if nex:
        out_shape += _exchange_shapes(*exchange)
        scratch += _exchange_sems(exchange[0], nex)
    outs = pl.pallas_call(
        body, name=name, grid=grid,
        in_specs=in_specs, out_specs=[o_spec] + [hbm] * nex, out_shape=out_shape,
        scratch_shapes=scratch,
        compiler_params=_params(("arbitrary",) * 3 if nex else ("parallel", "parallel", "arbitrary")),
    )(*args)
    return (outs[0], list(outs[1:])) if nex else outs[0]


def rms_fwd(x, gains, *, name):
    s, d = x.shape
    tr = _pick(s, (512, 256, 128, 8))
    ng = len(gains)

    def body(*refs):
        x_ref = refs[0]
        g_refs = refs[1:1 + ng]
        o_refs = refs[1 + ng:]
        xv = x_ref[...]
        y = xv * lax.rsqrt(jnp.mean(xv * xv, axis=-1, keepdims=True) + EPS)
        for g_ref, o_ref in zip(g_refs, o_refs):
            o_ref[...] = (y * g_ref[...]).astype(BF16)

    row = pl.BlockSpec((tr, d), lambda i: (i, 0))
    vec = pl.BlockSpec((1, d), lambda i: (0, 0))
    return pl.pallas_call(
        body, name=name, grid=(s // tr,),
        in_specs=[row] + [vec] * ng, out_specs=[row] * ng,
        out_shape=[jax.ShapeDtypeStruct((s, d), BF16)] * ng,
        compiler_params=_params(("parallel",)),
    )(x, *gains)


def rms_bwd(x, dres, norms, *, name):
    s, d = x.shape
    tr = _pick(s, (256, 128, 8))
    ng = len(norms)

    def body(*refs):
        x_ref, dres_ref = refs[0], refs[1]
        g_refs = refs[2:2 + ng]
        dh_refs = refs[2 + ng:2 + 2 * ng]
        dx_ref, dxb_ref = refs[2 + 2 * ng], refs[3 + 2 * ng]
        dg_refs = refs[4 + 2 * ng:]
        i = pl.program_id(0)
        xv = x_ref[...]
        r = lax.rsqrt(jnp.mean(xv * xv, axis=-1, keepdims=True) + EPS)
        xhat = xv * r
        dx = dres_ref[...]
        for g_ref, dh_ref, dg_ref in zip(g_refs, dh_refs, dg_refs):
            dh = dh_ref[...]
            part = jnp.sum(dh * xhat, axis=0, keepdims=True)

            @pl.when(i == 0)
            def _():
                dg_ref[...] = part

            @pl.when(i > 0)
            def _():
                dg_ref[...] += part

            dxhat = dh * g_ref[...]
            dx = dx + r * (dxhat - xhat * jnp.mean(dxhat * xhat, axis=-1, keepdims=True))
        dx_ref[...] = dx
        dxb_ref[...] = dx.astype(BF16)

    row = pl.BlockSpec((tr, d), lambda i: (i, 0))
    vec = pl.BlockSpec((1, d), lambda i: (0, 0))
    outs = pl.pallas_call(
        body, name=name, grid=(s // tr,),
        in_specs=[row, row] + [vec] * ng + [row] * ng,
        out_specs=[row, row] + [vec] * ng,
        out_shape=[jax.ShapeDtypeStruct((s, d), F32), jax.ShapeDtypeStruct((s, d), BF16)]
        + [jax.ShapeDtypeStruct((1, d), F32)] * ng,
        compiler_params=_params(("arbitrary",)),
    )(x, dres, *[g for g, _ in norms], *[dh for _, dh in norms])
    return outs[0], outs[1], list(outs[2:])


def loss_bwd(x2, target, gain, *, name):
    s, d = x2.shape
    tr = _pick(s, (256, 128, 8))
    nsteps = s // tr

    def body(x_ref, t_ref, g_ref, loss_ref, dg_ref, dx_ref, dxb_ref, sq_acc):
        i = pl.program_id(0)
        xv = x_ref[...]
        r = lax.rsqrt(jnp.mean(xv * xv, axis=-1, keepdims=True) + EPS)
        xhat = xv * r
        g = g_ref[...]
        err = xhat * g - t_ref[...]
        dy = err * (1.0 / d)
        sq = jnp.sum(err * err, axis=0, keepdims=True)
        dgp = jnp.sum(dy * xhat, axis=0, keepdims=True)

        @pl.when(i == 0)
        def _():
            sq_acc[...] = sq
            dg_ref[...] = dgp

        @pl.when(i > 0)
        def _():
            sq_acc[...] += sq
            dg_ref[...] += dgp

        dxhat = dy * g
        dx = r * (dxhat - xhat * jnp.mean(dxhat * xhat, axis=-1, keepdims=True))
        dx_ref[...] = dx
        dxb_ref[...] = dx.astype(BF16)

        @pl.when(i == nsteps - 1)
        def _():
            tot = jnp.sum(sq_acc[...], axis=-1, keepdims=True) * (0.5 / d)
            loss_ref[...] = jnp.broadcast_to(tot, (1, LANES))

    row = pl.BlockSpec((tr, d), lambda i: (i, 0))
    vec = pl.BlockSpec((1, d), lambda i: (0, 0))
    return pl.pallas_call(
        body, name=name, grid=(nsteps,),
        in_specs=[row, row, vec],
        out_specs=[pl.BlockSpec((1, LANES), lambda i: (0, 0)), vec, row, row],
        out_shape=[jax.ShapeDtypeStruct((1, LANES), F32), jax.ShapeDtypeStruct((1, d), F32),
                   jax.ShapeDtypeStruct((s, d), F32), jax.ShapeDtypeStruct((s, d), BF16)],
        scratch_shapes=[pltpu.VMEM((1, d), F32)],
        compiler_params=_params(("arbitrary",)),
    )(x2, target, gain)


def _lru_gates(xb, wr, wi, br, bi, sp):
    xbb = xb.astype(BF16)
    r = _sigmoid(jnp.dot(xbb, wr, preferred_element_type=F32) + br)
    ig = _sigmoid(jnp.dot(xbb, wi, preferred_element_type=F32) + bi)
    log_a = (-LRU_C) * r * sp
    a = jnp.exp(log_a)
    mult = jnp.sqrt(jnp.maximum(-jnp.tanh(log_a) * (a * a + 1.0), 0.0))
    return r, ig, a, mult


def _softplus_neg(lam):
    e = jnp.exp(-jnp.abs(lam))
    sp = jnp.maximum(-lam, 0.0) + jnp.log(1.0 + e)
    sg = jnp.where(lam >= 0, e, 1.0) / (1.0 + e)
    return sp, sg


def _conv(pad_ref, w, b, t):
    acc = b + w[CONV_W - 1:CONV_W, :] * pad_ref[pl.ds(SUBLANES, t), :]
    for dlt in range(1, CONV_W):
        acc = acc + w[CONV_W - 1 - dlt:CONV_W - dlt, :] * pad_ref[pl.ds(SUBLANES - dlt, t), :]
    return acc


def _lru_specs(t, bw, nb, time_of):
    blk = lambda c0: pl.BlockSpec((t, bw), lambda n, i, c0=c0: (time_of(i), c0 + n))
    vec = pl.BlockSpec((1, bw), lambda n, i: (0, n))
    wspec = pl.BlockSpec((None, bw, bw), lambda n, i: (n, 0, 0))
    cwspec = pl.BlockSpec((CONV_W, bw), lambda n, i: (0, n))
    return blk, vec, wspec, cwspec


def lru_fwd(proj, conv_w, conv_b, w_r, b_r, w_i, b_i, lam, *, name):
    s, r2 = proj.shape
    rr = r2 // 2
    nb, bw, _ = w_r.shape
    t = _pick(s, (512, 256, 128, 64, 32))
    ngroups = t // SUBLANES

    def body(xp_ref, gate_ref, cw_ref, cb_ref, wr_ref, br_ref, wi_ref, bi_ref, lam_ref,
             m_ref, h_ref, pad, hcarry, a_scr, u_scr):
        i = pl.program_id(1)

        @pl.when(i == 0)
        def _():
            pad[0:SUBLANES, :] = jnp.zeros((SUBLANES, bw), F32)
            hcarry[...] = jnp.zeros_like(hcarry)

        xpre = xp_ref[...]
        pad[pl.ds(SUBLANES, t), :] = xpre
        xb = _conv(pad, cw_ref[...], cb_ref[...], t)
        pad[0:SUBLANES, :] = xpre[t - SUBLANES:, :]
        sp, _ = _softplus_neg(lam_ref[...])
        _, ig, a, mult = _lru_gates(xb, wr_ref[...], wi_ref[...], br_ref[...], bi_ref[...], sp)
        a_scr[...] = a
        u_scr[...] = mult * (ig * xb)
        row = lax.broadcasted_iota(jnp.int32, (SUBLANES, bw), 0)

        def groups(gi, hprev):
            offs = [pl.multiple_of((gi * SCAN_UNROLL + u) * SUBLANES, SUBLANES) for u in range(SCAN_UNROLL)]
            scanned = []
            for off in offs:
                av = a_scr[pl.ds(off, SUBLANES), :]
                uv = u_scr[pl.ds(off, SUBLANES), :]
                for dlt in (1, 2, 4):
                    keep = row >= dlt
                    uv = jnp.where(keep, av * pltpu.roll(uv, dlt, 0) + uv, uv)
                    av = jnp.where(keep, av * pltpu.roll(av, dlt, 0), av)
                scanned.append((av, uv))
            for off, (av, uv) in zip(offs, scanned):
                hv = av * hprev + uv
                h_ref[pl.ds(off, SUBLANES), :] = hv
                hprev = hv[SUBLANES - 1:SUBLANES, :]
            return hprev

        hcarry[...] = lax.fori_loop(0, ngroups // SCAN_UNROLL, groups, hcarry[...])
        gate = gate_ref[...]
        m_ref[...] = (h_ref[...] * (gate * _sigmoid(gate))).astype(BF16)

    blk, vec, wspec, cwspec = _lru_specs(t, bw, nb, lambda i: i)
    return pl.pallas_call(
        body, name=name, grid=(nb, s // t),
        in_specs=[blk(0), blk(nb), cwspec, vec, wspec, vec, wspec, vec, vec],
        out_specs=[blk(0), blk(0)],
        out_shape=[jax.ShapeDtypeStruct((s, rr), BF16), jax.ShapeDtypeStruct((s, rr), F32)],
        scratch_shapes=[pltpu.VMEM((t + SUBLANES, bw), F32), pltpu.VMEM((1, bw), F32),
                        pltpu.VMEM((t, bw), F32), pltpu.VMEM((t, bw), F32)],
        compiler_params=_params(("parallel", "arbitrary")),
    )(proj, proj, conv_w, conv_b, w_r, b_r, w_i, b_i, lam)


def lru_bwd(proj, hst, dm, conv_w, conv_b, w_r, b_r, w_i, b_i, lam, *, name):
    s, r2 = proj.shape
    rr = r2 // 2
    nb, bw, _ = w_r.shape
    t = _pick(s, (512, 256, 128, 64, 32))
    nt = s // t
    ngroups = t // SUBLANES
    nt_dims = (((1,), (1,)), ((), ()))
    tn_dims = (((0,), (0,)), ((), ()))

    def body(xp_ref, xhalo_ref, gate_ref, h_ref, hhalo_ref, dm_ref, cw_ref, cb_ref, wr_ref, br_ref, wi_ref,
             bi_ref, lam_ref,
             dxp_ref, dgate_ref, dcw_ref, dcb_ref, dwr_ref, dbr_ref, dwi_ref, dbi_ref, dlam_ref,
             pad, hpad, dpad, ecarry, a_scr, b_scr, d_scr):
        step = pl.program_id(1)

        @pl.when(step == 0)
        def _():
            dpad[pl.ds(t, SUBLANES), :] = jnp.zeros((SUBLANES, bw), F32)
            ecarry[...] = jnp.zeros_like(ecarry)
            dcw_ref[...] = jnp.zeros_like(dcw_ref)
            dcb_ref[...] = jnp.zeros_like(dcb_ref)
            dwr_ref[...] = jnp.zeros_like(dwr_ref)
            dbr_ref[...] = jnp.zeros_like(dbr_ref)
            dwi_ref[...] = jnp.zeros_like(dwi_ref)
            dbi_ref[...] = jnp.zeros_like(dbi_ref)
            dlam_ref[...] = jnp.zeros_like(dlam_ref)

        past = jnp.where(step == nt - 1, 0.0, 1.0)
        pad[0:SUBLANES, :] = xhalo_ref[...] * past
        pad[pl.ds(SUBLANES, t), :] = xp_ref[...]
        hpad[0:SUBLANES, :] = hhalo_ref[...] * past
        hpad[pl.ds(SUBLANES, t), :] = h_ref[...]
        cw = cw_ref[...]
        xb = _conv(pad, cw, cb_ref[...], t)
        sp, sg = _softplus_neg(lam_ref[...])
        wr = wr_ref[...]
        wi = wi_ref[...]
        r, ig, a, mult = _lru_gates(xb, wr, wi, br_ref[...], bi_ref[...], sp)
        gate = gate_ref[...]
        sgate = _sigmoid(gate)
        dmv = dm_ref[...]
        dgate_ref[...] = (dmv * h_ref[...] * (sgate * (1.0 + gate * (1.0 - sgate)))).astype(BF16)
        dy = dmv * (gate * sgate)
        a_scr[...] = a
        b_scr[...] = a * dy
        row = lax.broadcasted_iota(jnp.int32, (SUBLANES, bw), 0)

        def groups(gi, enext):
            offs = [pl.multiple_of((ngroups - 1 - gi * SCAN_UNROLL - u) * SUBLANES, SUBLANES)
                    for u in range(SCAN_UNROLL)]
            scanned = []
            for off in offs:
                av = a_scr[pl.ds(off, SUBLANES), :]
                bv = b_scr[pl.ds(off, SUBLANES), :]
                for dlt in (1, 2, 4):
                    keep = row < SUBLANES - dlt
                    bv = jnp.where(keep, av * pltpu.roll(bv, SUBLANES - dlt, 0) + bv, bv)
                    av = jnp.where(keep, av * pltpu.roll(av, SUBLANES - dlt, 0), av)
                scanned.append((av, bv))
            for off, (av, bv) in zip(offs, scanned):
                ev = av * enext + bv
                d_scr[pl.ds(off, SUBLANES), :] = jnp.where(row == SUBLANES - 1, enext,
                                                           pltpu.roll(ev, SUBLANES - 1, 0))
                enext = ev[0:1, :]
            return enext

        ecarry[...] = lax.fori_loop(0, ngroups // SCAN_UNROLL, groups, ecarry[...])
        dtot = dy + d_scr[...]
        da = dtot * hpad[pl.ds(SUBLANES - 1, t), :]
        dmult = dtot * (ig * xb)
        dlog_a = da * a - dmult * (a * a) / mult
        dr_pre = dlog_a * ((-LRU_C) * sp) * (r * (1.0 - r))
        di_pre = (dtot * mult * xb) * (ig * (1.0 - ig))
        dlam_ref[...] += jnp.sum(dlog_a * r, axis=0, keepdims=True) * (LRU_C * sg)
        dbr_ref[...] += jnp.sum(dr_pre, axis=0, keepdims=True)
        dbi_ref[...] += jnp.sum(di_pre, axis=0, keepdims=True)
        drb = dr_pre.astype(BF16)
        dib = di_pre.astype(BF16)
        xbb = xb.astype(BF16)
        dxb = (dtot * mult * ig
               + lax.dot_general(drb, wr, nt_dims, preferred_element_type=F32)
               + lax.dot_general(dib, wi, nt_dims, preferred_element_type=F32))
        dwr_ref[...] += lax.dot_general(xbb, drb, tn_dims, preferred_element_type=F32)
        dwi_ref[...] += lax.dot_general(xbb, dib, tn_dims, preferred_element_type=F32)
        dcb_ref[...] += jnp.sum(dxb, axis=0, keepdims=True)
        dpad[pl.ds(0, t), :] = dxb
        dxpre = cw[CONV_W - 1:CONV_W, :] * dxb
        dcw_ref[CONV_W - 1:CONV_W, :] += jnp.sum(dxb * pad[pl.ds(SUBLANES, t), :], axis=0, keepdims=True)
        for dlt in range(1, CONV_W):
            dxpre = dxpre + cw[CONV_W - 1 - dlt:CONV_W - dlt, :] * dpad[pl.ds(dlt, t), :]
            dcw_ref[CONV_W - 1 - dlt:CONV_W - dlt, :] += jnp.sum(
                dxb * pad[pl.ds(SUBLANES - dlt, t), :], axis=0, keepdims=True)
        dpad[pl.ds(t, SUBLANES), :] = dxb[0:SUBLANES, :]
        dxp_ref[...] = dxpre.astype(BF16)

    rev = lambda i: nt - 1 - i
    blk, vec, wspec, cwspec = _lru_specs(t, bw, nb, rev)
    halo = pl.BlockSpec((SUBLANES, bw), lambda n, i: (jnp.maximum(rev(i) * ngroups - 1, 0), n))
    return pl.pallas_call(
        body, name=name, grid=(nb, nt),
        in_specs=[blk(0), halo, blk(nb), blk(0), halo, blk(0), cwspec, vec, wspec, vec, wspec, vec, vec],
        out_specs=[blk(0), blk(0), cwspec, vec, wspec, vec, wspec, vec, vec],
        out_shape=[jax.ShapeDtypeStruct((s, rr), BF16), jax.ShapeDtypeStruct((s, rr), BF16),
                   jax.ShapeDtypeStruct((CONV_W, rr), F32), jax.ShapeDtypeStruct((1, rr), F32),
                   jax.ShapeDtypeStruct((nb, bw, bw), F32), jax.ShapeDtypeStruct((1, rr), F32),
                   jax.ShapeDtypeStruct((nb, bw, bw), F32), jax.ShapeDtypeStruct((1, rr), F32),
                   jax.ShapeDtypeStruct((1, rr), F32)],
        scratch_shapes=[pltpu.VMEM((t + SUBLANES, bw), F32), pltpu.VMEM((t + SUBLANES, bw), F32),
                        pltpu.VMEM((t + SUBLANES, bw), F32), pltpu.VMEM((1, bw), F32),
                        pltpu.VMEM((t, bw), F32), pltpu.VMEM((t, bw), F32), pltpu.VMEM((t, bw), F32)],
        compiler_params=_params(("parallel", "arbitrary")),
    )(proj, proj, proj, hst, hst, dm, conv_w, conv_b, w_r, b_r, w_i, b_i, lam)


def _softplus(z):
    return jnp.maximum(z, 0.0) + jnp.log(1.0 + jnp.exp2(jnp.abs(z) * (-LOG2E)))


def _att_blocks(s):
    bk = ATT_BLOCK if s % ATT_BLOCK == 0 else s
    bq = ATT_QTILES * bk if s % (ATT_QTILES * bk) == 0 else bk
    return bk, bq


def _tile_base(i, r):
    return r * ((i * (i + 1)) // 2)


def attn_fwd(projb, kv, *, name):
    s, a2 = projb.shape
    a = a2 // 2
    nh = a // HEAD_DIM
    bk, bq = _att_blocks(s)
    r = bq // bk
    nq = s // bq
    ntiles = _tile_base(nq, r)
    scale = 1.0 / math.sqrt(HEAD_DIM)
    nt_dims = (((1,), (1,)), ((), ()))
    hp = ATT_HEADS if nh % ATT_HEADS == 0 else 1
    wd = hp * HEAD_DIM

    def body(q_ref, g_ref, k_ref, v_ref, m_ref, o_ref, saved_hbm, acc, stage, sems):
        hgrp, i = pl.program_id(0), pl.program_id(1)
        base = _tile_base(i, r)
        qb = (q_ref[...] * scale).astype(BF16)
        from_mat = (lax.broadcasted_iota(jnp.int32, (bk, bk), 0)
                    >= lax.broadcasted_iota(jnp.int32, (bk, bk), 1)).astype(BF16)
        rowi = lax.broadcasted_iota(jnp.int32, (bq, bk), 0)
        coli = lax.broadcasted_iota(jnp.int32, (bq, bk), 1)
        cols = [slice(hh * HEAD_DIM, (hh + 1) * HEAD_DIM) for hh in range(hp)]

        def save(slot, j):
            return pltpu.make_async_copy(stage.at[slot], saved_hbm.at[hgrp, base + j], sems.at[slot])

        def tile(j, n, carries, diag):
            r0 = 0 if diag is None else diag * bk
            live = slice(r0, bq)
            causal = None if diag is None else coli[live] < rowi[:bq - r0]
            slot = n % 2

            def free_slot():
                save(slot, 0).wait()

            if isinstance(n, int):
                if n >= 2:
                    free_slot()
            elif r >= 2:
                free_slot()
            else:
                pl.when(n >= 2)(free_slot)
            rows = pl.ds(pl.multiple_of(j * bk, bk), bk)
            zs = [lax.dot_general(qb[live, c], k_ref[rows, c], nt_dims, preferred_element_type=F32) for c in cols]
            sums, sigs = [], []
            for z in zs:
                sp = _softplus(z)
                sig = z - sp
                if causal is not None:
                    sp = jnp.where(causal, sp, 0.0)
                    sig = jnp.where(causal, sig, LOG_ZERO)
                sums.append(jnp.dot(sp.astype(BF16), from_mat, preferred_element_type=F32))
                sigs.append(sig.astype(BF16))
            out = []
            for hh in range(hp):
                w = jnp.exp(zs[hh] - sums[hh] - carries[hh][live])
                if causal is not None:
                    w = jnp.where(causal, w, 0.0)
                wb = w.astype(BF16)
                acc[live, cols[hh]] += jnp.dot(wb, v_ref[rows, cols[hh]], preferred_element_type=F32)
                stage[slot, 0, hh, live] = wb
                stage[slot, 1, hh, live] = sigs[hh]
                if r0:
                    stage[slot, :, hh, :r0] = jnp.zeros((2, r0, bk), BF16)
                grown = carries[hh][live] + sums[hh][:, 0:1]
                out.append(jnp.concatenate([carries[hh][:r0], grown], axis=0) if r0 else grown)
            save(slot, j).start()
            return tuple(out)

        acc[...] = jnp.zeros_like(acc)
        carries = tuple(jnp.zeros((bq, 1), F32) for _ in range(hp))
        for n, dg in enumerate(reversed(range(r))):
            carries = tile(r * i + dg, n, carries, dg)
        lax.fori_loop(0, r * i, lambda jj, c: tile(r * i - 1 - jj, r + jj, c, None), carries)
        ntile = r * (i + 1)
        for back in (1, 2):
            def drain(back=back):
                save((ntile - back) % 2, 0).wait()
            if r >= back:
                drain()
            else:
                pl.when(ntile >= back)(drain)
        o = acc[...]
        o_ref[...] = o
        gate = g_ref[...]
        m_ref[...] = (o * (gate * _sigmoid(gate))).astype(BF16)

    ng = nh // hp
    qspec = lambda c0: pl.BlockSpec((bq, wd), lambda h, i, c0=c0: (i, c0 + h))
    kspec = lambda c0: pl.BlockSpec((s, wd), lambda h, i, c0=c0: (0, c0 + h), pipeline_mode=pl.Buffered(1))
    hbm = pl.BlockSpec(memory_space=pl.ANY)
    saved = jax.ShapeDtypeStruct((ng, ntiles, 2, hp, bq, bk), BF16)
    return pl.pallas_call(
        body, name=name, grid=(ng, nq),
        in_specs=[qspec(0), qspec(ng), kspec(0), kspec(ng)],
        out_specs=[qspec(0), qspec(0), hbm],
        out_shape=[jax.ShapeDtypeStruct((s, a), BF16), jax.ShapeDtypeStruct((s, a), F32), saved],
        scratch_shapes=[pltpu.VMEM((bq, wd), F32), pltpu.VMEM((2, 2, hp, bq, bk), BF16),
                        pltpu.SemaphoreType.DMA((2,))],
        compiler_params=_params(("arbitrary", "arbitrary")),
    )(projb, projb, kv, kv)


def attn_bwd(projb, dm, o, kv, saved, *, name):
    s, a2 = projb.shape
    a = a2 // 2
    nh = a // HEAD_DIM
    bk, bq = _att_blocks(s)
    r = bq // bk
    nq = s // bq
    scale = 1.0 / math.sqrt(HEAD_DIM)
    nt_dims = (((1,), (1,)), ((), ()))
    tn_dims = (((0,), (0,)), ((), ()))
    hp = saved.shape[3]
    wd = hp * HEAD_DIM
    ahead = ATT_FETCH_AHEAD

    def body(q_ref, g_ref, dm_ref, o_ref, k_ref, v_ref, saved_hbm, dq_ref, dg_ref, dk_ref, dv_ref,
             dk_acc, dv_acc, dq_acc, stage, sems):
        hgrp, i = pl.program_id(0), pl.program_id(1)
        base = _tile_base(i, r)
        ntile = r * (i + 1)

        @pl.when(i == 0)
        def _():
            dk_acc[...] = jnp.zeros_like(dk_acc)
            dv_acc[...] = jnp.zeros_like(dv_acc)

        def fetch(j):
            slot = j % (ahead + 1)
            return pltpu.make_async_copy(saved_hbm.at[hgrp, base + j], stage.at[slot], sems.at[slot])

        for j0 in range(ahead):
            pl.when(j0 < ntile)(lambda j0=j0: fetch(j0).start())
        qb = (q_ref[...] * scale).astype(BF16)
        gate = g_ref[...]
        sgate = _sigmoid(gate)
        dmv = dm_ref[...]
        dob = (dmv * (gate * sgate)).astype(BF16)
        dg_ref[...] = (dmv * o_ref[...] * (sgate * (1.0 + gate * (1.0 - sgate)))).astype(BF16)
        upto_mat = (lax.broadcasted_iota(jnp.int32, (bk, bk), 0)
                    <= lax.broadcasted_iota(jnp.int32, (bk, bk), 1)).astype(BF16)
        cols = [slice(hh * HEAD_DIM, (hh + 1) * HEAD_DIM) for hh in range(hp)]
        dq_acc[...] = jnp.zeros_like(dq_acc)

        def tile(j, gcarries, r0=0, more=None):
            live = slice(r0, bq)
            slot = j % (ahead + 1)
            if more is None:
                pl.when(j + ahead < ntile)(lambda: fetch(j + ahead).start())
            elif more:
                fetch(j + ahead).start()
            fetch(j).wait()
            rows = pl.ds(pl.multiple_of(j * bk, bk), bk)
            dws = [lax.dot_general(dob[live, c], v_ref[rows, c], nt_dims, preferred_element_type=F32) for c in cols]
            gs, totals = [], []
            for hh in range(hp):
                wb = stage[slot, 0, hh, live]
                g = wb.astype(F32) * dws[hh]
                dv_acc[rows, cols[hh]] += lax.dot_general(wb, dob[live, cols[hh]], tn_dims,
                                                          preferred_element_type=F32)
                totals.append(jnp.dot(g.astype(BF16), upto_mat, preferred_element_type=F32))
                gs.append(g)
            out = []
            for hh in range(hp):
                dz = gs[hh] - (totals[hh] + gcarries[hh][live]) * jnp.exp(stage[slot, 1, hh, live]).astype(F32)
                dzb = dz.astype(BF16)
                dq_acc[live, cols[hh]] += jnp.dot(dzb, k_ref[rows, cols[hh]], preferred_element_type=F32)
                dk_acc[rows, cols[hh]] += lax.dot_general(dzb, qb[live, cols[hh]], tn_dims,
                                                          preferred_element_type=F32)
                grown = gcarries[hh][live] + totals[hh][:, bk - 1:bk]
                out.append(jnp.concatenate([gcarries[hh][:r0], grown], axis=0) if r0 else grown)
            return tuple(out)

        gcarries = lax.fori_loop(0, r * i, tile, tuple(jnp.zeros((bq, 1), F32) for _ in range(hp)))
        for dg in range(r):
            gcarries = tile(r * i + dg, gcarries, dg * bk, dg + ahead < r)
        dq_ref[...] = (dq_acc[...] * scale).astype(BF16)

        @pl.when(i == nq - 1)
        def _():
            dk_ref[...] = dk_acc[...].astype(BF16)
            dv_ref[...] = dv_acc[...].astype(BF16)

    ng = nh // hp
    once = pl.Buffered(1)
    qspec = lambda c0: pl.BlockSpec((bq, wd), lambda h, i, c0=c0: (i, c0 + h))
    kspec = lambda c0: pl.BlockSpec((s, wd), lambda h, i, c0=c0: (0, c0 + h), pipeline_mode=once)
    hbm = pl.BlockSpec(memory_space=pl.ANY)
    return pl.pallas_call(
        body, name=name, grid=(ng, nq),
        in_specs=[qspec(0), qspec(ng), qspec(0), qspec(0), kspec(0), kspec(ng), hbm],
        out_specs=[qspec(0), qspec(0), kspec(0), kspec(0)],
        out_shape=[jax.ShapeDtypeStruct((s, a), BF16)] * 4,
        scratch_shapes=[pltpu.VMEM((s, wd), F32), pltpu.VMEM((s, wd), F32), pltpu.VMEM((bq, wd), F32),
                        pltpu.VMEM((ahead + 1, 2, hp, bq, bk), BF16), pltpu.SemaphoreType.DMA((ahead + 1,))],
        compiler_params=_params(("arbitrary", "arbitrary"), vmem=V7X_VMEM_LIMIT_HIGH),
    )(projb, projb, dm, o, kv, kv, saved)


def _as2d(x):
    n = x.size
    cols = x.shape[-1]
    if cols % LANES != 0:
        cols = LANES
    return x.reshape(n // cols, cols)


def sum_parts(parts, *, name):
    p, rows, cols = parts.shape
    tr = _pick(rows, (512, 256, 128, 64, 32, 16))

    def body(p_ref, o_ref):
        acc = p_ref[0].astype(F32)
        for k in range(1, p):
            acc = acc + p_ref[k].astype(F32)
        o_ref[...] = acc

    return pl.pallas_call(
        body, name=name, grid=(rows // tr,),
        in_specs=[pl.BlockSpec((p, tr, cols), lambda i: (0, i, 0))],
        out_specs=pl.BlockSpec((tr, cols), lambda i: (i, 0)),
        out_shape=jax.ShapeDtypeStruct((rows, cols), F32),
        compiler_params=_params(("parallel",)),
    )(parts)


def adamw(w, g_parts, m, v, *, name):
    rows, cols = w.shape
    tr = _pick(rows, (128, 64, 32, 16, 8))
    np_ = len(g_parts)
    c1 = 1.0 / (1.0 - ADAM_B1 ** ADAM_STEP)
    c2 = 1.0 / (1.0 - ADAM_B2 ** ADAM_STEP)

    def body(*refs):
        w_ref, m_ref, v_ref = refs[0], refs[1], refs[2]
        g_refs = refs[3:3 + np_]
        go_ref, d_ref, mo_ref, vo_ref = refs[3 + np_:]
        g = g_refs[0][...]
        for gr in g_refs[1:]:
            g = g + gr[...]
        mn = ADAM_B1 * m_ref[...] + (1.0 - ADAM_B1) * g
        vn = ADAM_B2 * v_ref[...] + (1.0 - ADAM_B2) * (g * g)
        go_ref[...] = g
        mo_ref[...] = mn
        vo_ref[...] = vn
        d_ref[...] = (-ADAM_LR) * ((mn * c1) / (jnp.sqrt(vn * c2) + ADAM_EPS) + ADAM_WD * w_ref[...])

    spec = pl.BlockSpec((tr, cols), lambda i: (i, 0))
    return pl.pallas_call(
        body, name=name, grid=(rows // tr,),
        in_specs=[spec] * (3 + np_), out_specs=[spec] * 4,
        out_shape=[jax.ShapeDtypeStruct((rows, cols), F32)] * 4,
        compiler_params=_params(("parallel",)),
    )(w, m, v, *g_parts)


def exchange(kind, arrays, axes, *, name):
    na = len(arrays)
    hbm = pl.BlockSpec(memory_space=pl.ANY)

    def body(*refs):
        triples = _exchange(kind, refs[:na], refs[na:2 * na], axes, *refs[2 * na:])
        for step in range(3):
            for triple in triples:
                triple[step]()

    return pl.pallas_call(
        body, name=name, in_specs=[hbm] * na, out_specs=[hbm] * na,
        out_shape=_exchange_shapes(kind, arrays, axes), scratch_shapes=_exchange_sems(kind, na),
    )(*arrays)


def swap_cores(arrs, *, name):
    na = len(arrs)
    hbm = pl.BlockSpec(memory_space=pl.ANY)

    def body(*refs):
        a_refs = refs[:na]
        o_refs = refs[na:2 * na]
        send_sems, recv_sems = refs[2 * na:]
        x, y, c = _place()
        copies = []
        for ai in range(na):
            cp = pltpu.make_async_remote_copy(
                src_ref=a_refs[ai], dst_ref=o_refs[ai], send_sem=send_sems.at[ai], recv_sem=recv_sems.at[ai],
                device_id=(x, y, 1 - c), device_id_type=MESH)
            cp.start()
            copies.append(cp)
        for cp in copies:
            cp.wait()

    return pl.pallas_call(
        body, name=name,
        in_specs=[hbm] * na, out_specs=[hbm] * na,
        out_shape=[jax.ShapeDtypeStruct(a.shape, a.dtype) for a in arrs],
        scratch_shapes=[pltpu.SemaphoreType.DMA((na,)), pltpu.SemaphoreType.DMA((na,))],
    )(*arrs)


def allreduce_small(arrs, *, name):
    nar = len(arrs)
    width = max(a.shape[1] for a in arrs)
    starts, total = [], 0
    for a in arrs:
        starts.append(total)
        total += a.shape[0]
    total += (-total) % SUBLANES

    def body(*refs):
        in_refs, out_refs = refs[:nar], refs[nar:2 * nar]
        buf, slots, send_sems, recv_sems = refs[2 * nar:]
        x, y, c = _place()
        me = 4 * x + 2 * y + c
        buf[...] = jnp.zeros_like(buf)
        for ref, st in zip(in_refs, starts):
            buf[st:st + ref.shape[0], 0:ref.shape[1]] = ref[...]
        slots[0] = buf[...]
        copies = []
        for rel in range(1, 8):
            peer = (x ^ (rel >> 2), y ^ ((rel >> 1) & 1), c ^ (rel & 1))
            cp = pltpu.make_async_remote_copy(
                src_ref=buf, dst_ref=slots.at[rel], send_sem=send_sems.at[rel - 1],
                recv_sem=recv_sems.at[rel - 1], device_id=peer, device_id_type=MESH)
            cp.start()
            copies.append(cp)
        for cp in copies:
            cp.wait()
        acc = slots[me]
        for dev in range(1, 8):
            acc = acc + slots[dev ^ me]
        buf[...] = acc
        for ref, st in zip(out_refs, starts):
            ref[...] = buf[st:st + ref.shape[0], 0:ref.shape[1]]

    vm = pl.BlockSpec(memory_space=pltpu.VMEM)
    return pl.pallas_call(
        body, name=name, in_specs=[vm] * nar, out_specs=[vm] * nar,
        out_shape=[jax.ShapeDtypeStruct(a.shape, F32) for a in arrs],
        scratch_shapes=[pltpu.VMEM((total, width), F32), pltpu.VMEM((8, total, width), F32),
                        pltpu.SemaphoreType.DMA((7,)), pltpu.SemaphoreType.DMA((7,))],
    )(*arrs)


def adamw_small(ws, gs, ms, vs, *, name):
    n = len(ws)
    c1 = 1.0 / (1.0 - ADAM_B1 ** ADAM_STEP)
    c2 = 1.0 / (1.0 - ADAM_B2 ** ADAM_STEP)

    def body(*refs):
        w_refs, g_refs, m_refs, v_refs = (refs[k * n:(k + 1) * n] for k in range(4))
        d_refs, mo_refs, vo_refs = (refs[(4 + k) * n:(5 + k) * n] for k in range(3))
        for w_ref, g_ref, m_ref, v_ref, d_ref, mo_ref, vo_ref in zip(w_refs, g_refs, m_refs, v_refs,
                                                                     d_refs, mo_refs, vo_refs):
            g = g_ref[...]
            mn = ADAM_B1 * m_ref[...] + (1.0 - ADAM_B1) * g
            vn = ADAM_B2 * v_ref[...] + (1.0 - ADAM_B2) * (g * g)
            mo_ref[...] = mn
            vo_ref[...] = vn
            d_ref[...] = (-ADAM_LR) * ((mn * c1) / (jnp.sqrt(vn * c2) + ADAM_EPS) + ADAM_WD * w_ref[...])

    vm = pl.BlockSpec(memory_space=pltpu.VMEM)
    outs = pl.pallas_call(
        body, name=name, in_specs=[vm] * (4 * n), out_specs=[vm] * (3 * n),
        out_shape=[jax.ShapeDtypeStruct(w.shape, F32) for w in ws] * 3,
    )(*ws, *gs, *ms, *vs)
    return outs[:n], outs[n:2 * n], outs[2 * n:]


def kernel(x, a_norm, a_w_in, a_conv_w, a_conv_b, a_w_r, a_b_r, a_w_i, a_b_i, a_lambda, a_w_out, kv_norm, w_kv, b_norm, b_w_in, b_w_out, final_norm, loss_target, m_a_norm, m_a_w_in, m_a_conv_w, m_a_conv_b, m_a_w_r, m_a_b_r, m_a_w_i, m_a_b_i, m_a_lambda, m_a_w_out, m_kv_norm, m_w_kv, m_b_norm, m_b_w_in, m_b_w_out, m_final_norm, v_a_norm, v_a_w_in, v_a_conv_w, v_a_conv_b, v_a_w_r, v_a_b_r, v_a_w_i, v_a_b_i, v_a_lambda, v_a_w_out, v_kv_norm, v_w_kv, v_b_norm, v_b_w_in, v_b_w_out, v_final_norm):
    weights = dict(a_norm=a_norm, a_w_in=a_w_in, a_conv_w=a_conv_w, a_conv_b=a_conv_b, a_w_r=a_w_r, a_b_r=a_b_r,
                   a_w_i=a_w_i, a_b_i=a_b_i, a_lambda=a_lambda, a_w_out=a_w_out, kv_norm=kv_norm, w_kv=w_kv,
                   b_norm=b_norm, b_w_in=b_w_in, b_w_out=b_w_out, final_norm=final_norm)
    mom1 = dict(a_norm=m_a_norm, a_w_in=m_a_w_in, a_conv_w=m_a_conv_w, a_conv_b=m_a_conv_b, a_w_r=m_a_w_r,
                a_b_r=m_a_b_r, a_w_i=m_a_w_i, a_b_i=m_a_b_i, a_lambda=m_a_lambda, a_w_out=m_a_w_out,
                kv_norm=m_kv_norm, w_kv=m_w_kv, b_norm=m_b_norm, b_w_in=m_b_w_in, b_w_out=m_b_w_out,
                final_norm=m_final_norm)
    mom2 = dict(a_norm=v_a_norm, a_w_in=v_a_w_in, a_conv_w=v_a_conv_w, a_conv_b=v_a_conv_b, a_w_r=v_a_w_r,
                a_b_r=v_a_b_r, a_w_i=v_a_w_i, a_b_i=v_a_b_i, a_lambda=v_a_lambda, a_w_out=v_a_w_out,
                kv_norm=v_kv_norm, w_kv=v_w_kv, b_norm=v_b_norm, b_w_in=v_b_w_in, b_w_out=v_b_w_out,
                final_norm=v_final_norm)
    order = list(weights)
    x0 = x[0]
    target = loss_target[0]
    d = x0.shape[1]
    chip = 2 * lax.axis_index("x") + lax.axis_index("y")

    big = ["a_w_in", "a_w_r", "a_w_i", "a_w_out", "w_kv", "b_w_in", "b_w_out"]
    big_axis = dict(a_w_in=1, a_w_r=1, a_w_i=1, a_w_out=0, w_kv=1, b_w_in=1, b_w_out=0)
    local = dict(a_w_in=a_w_in[0], a_w_r=a_w_r[0], a_w_i=a_w_i[0], a_w_out=a_w_out[0], w_kv=w_kv,
                 b_w_in=b_w_in[0], b_w_out=b_w_out[0])
    shards = {n: local[n].astype(BF16) for n in big}
    first = ["a_w_in", "a_w_r", "a_w_i"]
    full = exchange("gather_halves", [shards[n] for n in first], [big_axis[n] for n in first], name="gather_first")
    full += exchange("gather", [a_conv_w[0], b_norm], [1, 1], name="gather_small")
    wf = dict(zip(first + ["a_conv_w", "b_norm"], full))
    wf.update(a_norm=a_norm, a_conv_b=a_conv_b, a_b_r=a_b_r, a_b_i=a_b_i, a_lambda=a_lambda,
              kv_norm=kv_norm.reshape(1, d), final_norm=final_norm.reshape(1, d))
    loss_part, grad_x, parts, gsmall = _local_grads(x0, target, wf, shards=shards, axes=big_axis)

    sums = [sum_parts(parts[n].reshape(4, *_as2d(parts[n][0]).shape), name="sum_" + n) for n in big]
    others = swap_cores(sums, name="swap_cores")

    small = ["a_norm", "a_conv_b", "a_b_r", "a_b_i", "a_lambda", "kv_norm", "final_norm", "a_conv_w", "b_norm"]
    *red, loss_sum = allreduce_small([gsmall[n] for n in small] + [loss_part], name="allreduce_small")
    gs = dict(zip(small, red))
    loss = loss_sum[0, 0]
    n_conv = a_conv_w.shape[2]
    gs["a_conv_w"] = lax.dynamic_slice_in_dim(gs["a_conv_w"], chip * n_conv, n_conv, axis=1)
    n_bn = b_norm.shape[1]
    gs["b_norm"] = lax.dynamic_slice_in_dim(gs["b_norm"], chip * n_bn, n_bn, axis=1)

    grads, deltas, new_m, new_v = {}, {}, {}, {}
    for n, s_mine, s_other in zip(big, sums, others):
        shp = weights[n].shape
        g, dlt, mn, vn = adamw(_as2d(weights[n]), [s_mine, s_other], _as2d(mom1[n]), _as2d(mom2[n]),
                               name="adamw_" + n)
        grads[n], deltas[n], new_m[n], new_v[n] = (t.reshape(shp) for t in (g, dlt, mn, vn))
    as_g = lambda src: [src[n].reshape(gs[n].shape) for n in small]
    outs = adamw_small(as_g(weights), [gs[n] for n in small], as_g(mom1), as_g(mom2), name="adamw_small")
    for dst, vals in zip((deltas, new_m, new_v), outs):
        for n, val in zip(small, vals):
            dst[n] = val.reshape(weights[n].shape)
    for n in small:
        grads[n] = gs[n].reshape(weights[n].shape)

    return (loss, grad_x[None], *[grads[n] for n in order], *[deltas[n] for n in order],
            *[new_m[n] for n in order], *[new_v[n] for n in order])


def _local_grads(x0, target, wf, shards=None, axes=None):
    a_norm, a_conv_b, a_b_r, a_b_i, a_lambda = (wf[n] for n in ("a_norm", "a_conv_b", "a_b_r", "a_b_i", "a_lambda"))
    kv_norm, final_norm = wf["kv_norm"], wf["final_norm"]
    wf = dict(wf)
    parts = {}

    def mm(*args, gather=(), scatter=None, **kw):
        if shards is None or not (gather or scatter):
            return matmul(*args, **kw)
        if gather:
            out, got = matmul(*args, exchange=("gather_halves", [shards[n] for n in gather], [axes[n] for n in gather]),
                              **kw)
            wf.update(zip(gather, got))
        else:
            out, got = matmul(*args, exchange=("scatter", list(scatter.values()), [axes[n] for n in scatter]), **kw)
            parts.update(zip(scatter, got))
        return out

    (h_a,) = rms_fwd(x0, [a_norm], name="norm_a")
    proj_a = mm(h_a, wf["a_w_in"], gather=("a_w_out",), name="a_in")
    m_a, hst = lru_fwd(proj_a, wf["a_conv_w"], a_conv_b, wf["a_w_r"], a_b_r, wf["a_w_i"], a_b_i, a_lambda,
                       name="lru_fwd")
    x1 = mm(m_a, wf["a_w_out"], residual=x0, gather=("w_kv",), name="a_out")
    kvn, hb = rms_fwd(x1, [kv_norm, wf["b_norm"]], name="norm_kv_b")
    kv = mm(kvn, wf["w_kv"], out_dtype=BF16, gather=("b_w_in",), name="kv_proj")
    proj_b = mm(hb, wf["b_w_in"], gather=("b_w_out",), name="b_in")
    m_b, o, saved = attn_fwd(proj_b, kv, name="attn_fwd")
    x2 = mm(m_b, wf["b_w_out"], residual=x1, name="b_out")
    loss_part, g_final, dx2, dx2b = loss_bwd(x2, target, final_norm, name="loss_bwd")

    dm_b = mm(dx2b, wf["b_w_out"], tb=True, name="b_out_dx")
    g_b_w_out = mm(m_b, dx2b, ta=True, out_dtype=BF16, name="b_out_dw")
    dq, dgate_b, dk, dv = attn_bwd(proj_b, dm_b, o, kv, saved, name="attn_bwd")
    dproj_b = (dq, dgate_b)
    dkv = (dk, dv)
    g_b_w_in = mm(hb, dproj_b, ta=True, out_dtype=BF16, scatter=dict(b_w_out=g_b_w_out), name="b_in_dw")
    g_w_kv = mm(kvn, dkv, ta=True, out_dtype=BF16, scatter=dict(b_w_in=g_b_w_in), name="kv_dw")
    dhb = mm(dproj_b, wf["b_w_in"], tb=True, scatter=dict(w_kv=g_w_kv), name="b_in_dx")
    dkvn = mm(dkv, wf["w_kv"], tb=True, name="kv_dx")
    dx1, dx1b, (g_kv_norm, g_b_norm) = rms_bwd(
        x1, dx2, [(kv_norm, dkvn), (wf["b_norm"], dhb)], name="norm_kv_b_bwd")

    g_a_w_out = mm(m_a, dx1b, ta=True, out_dtype=BF16, name="a_out_dw")
    dm_a = mm(dx1b, wf["a_w_out"], tb=True, scatter=dict(a_w_out=g_a_w_out), name="a_out_dx")
    dxpre, dgate_a, g_conv_w, g_conv_b, g_w_r, g_b_r, g_w_i, g_b_i, g_lambda = lru_bwd(
        proj_a, hst, dm_a, wf["a_conv_w"], a_conv_b, wf["a_w_r"], a_b_r, wf["a_w_i"], a_b_i, a_lambda,
        name="lru_bwd")
    dproj_a = (dxpre, dgate_a)
    g_w_r, g_w_i = g_w_r.astype(BF16), g_w_i.astype(BF16)
    g_a_w_in = mm(h_a, dproj_a, ta=True, out_dtype=BF16, scatter=dict(a_w_r=g_w_r, a_w_i=g_w_i), name="a_in_dw")
    dh_a = mm(dproj_a, wf["a_w_in"], tb=True, scatter=dict(a_w_in=g_a_w_in), name="a_in_dx")
    grad_x, _, (g_a_norm,) = rms_bwd(x0, dx1, [(a_norm, dh_a)], name="norm_a_bwd")

    gbig = parts if shards is not None else dict(
        a_w_in=g_a_w_in, a_w_r=g_w_r, a_w_i=g_w_i, a_w_out=g_a_w_out, w_kv=g_w_kv, b_w_in=g_b_w_in, b_w_out=g_b_w_out)
    gsmall = dict(a_norm=g_a_norm, a_conv_b=g_conv_b, a_b_r=g_b_r, a_b_i=g_b_i, a_lambda=g_lambda,
                  kv_norm=g_kv_norm, final_norm=g_final, a_conv_w=g_conv_w, b_norm=g_b_norm)
    return loss_part, grad_x, gbig, gsmall
```

```python
import math

import jax
import jax.numpy as jnp
from jax import lax
from jax.experimental import pallas as pl
from jax.experimental.pallas import tpu as pltpu

F32 = jnp.float32
BF16 = jnp.bfloat16
MESH = pl.DeviceIdType.MESH

EPS = 1e-6
LRU_C = 8.0
CONV_W = 4
HEAD_DIM = 128
ADAM_LR = 0.001
ADAM_B1 = 0.9
ADAM_B2 = 0.999
ADAM_EPS = 1e-08
ADAM_WD = 0.01
ADAM_STEP = 10

V7X_VMEM_LIMIT = 56 * 1024 * 1024
V7X_VMEM_LIMIT_HIGH = 60 * 1024 * 1024
LANES = 128
SUBLANES = 8
ATT_BLOCK = 256
ATT_QTILES = 4
ATT_HEADS = 2
ATT_FETCH_AHEAD = 2
LOG2E = 1.4426950408889634
LOG_ZERO = -1e30
SCAN_UNROLL = 4


def _pick(dim, cands):
    for c in cands:
        if dim % c == 0:
            return c
    return dim


def _params(sem, vmem=V7X_VMEM_LIMIT):
    return pltpu.CompilerParams(dimension_semantics=sem, vmem_limit_bytes=vmem)


def _sigmoid(x):
    return 1.0 / (1.0 + jnp.exp(-x))


def _place():
    return lax.axis_index("x"), lax.axis_index("y"), lax.axis_index("c")


def _chip_peers(x, y, c):
    return [(1 - x, y, c), (x, 1 - y, c), (1 - x, 1 - y, c)]


def _shard_of(ref, axis, idx, n):
    start = pl.multiple_of(idx * n, n)
    sl = [slice(None)] * len(ref.shape)
    sl[axis] = pl.ds(start, n)
    return ref.at[tuple(sl)]


def _half_rows(ref, h):
    n = ref.shape[0] // 2
    return ref.at[pl.ds(pl.multiple_of(h * n, n), n)]


def _block_half(ref, axis, idx, n, h):
    if axis == 0:
        return ref.at[pl.ds(pl.multiple_of(idx * n + h * (n // 2), n // 2), n // 2)]
    return _half_rows(_shard_of(ref, axis, idx, n), h)


def _exchange(kind, in_refs, out_refs, axes, send_sems, recv_sems, local_sems, send2_sems=None, recv2_sems=None):
    x, y, c = _place()
    me = 2 * x + y
    peers = _chip_peers(x, y, c)
    sibling = (x, y, 1 - c)
    triples = []
    nothing = lambda: None
    for ai, (src, dst, ax) in enumerate(zip(in_refs, out_refs, axes)):
        if kind == "scatter":
            n = dst.shape[1 + ax]
            loc = pltpu.make_async_copy(_shard_of(src, ax, me, n), dst.at[0], local_sems.at[ai])
        else:
            n = src.shape[ax]
            mine = _shard_of(dst, ax, me, n)
            loc = pltpu.make_async_copy(src, mine, local_sems.at[ai])
        triples.append((loc.start, nothing, loc.wait))
        for k, peer in enumerate(peers):
            sem = dict(send_sem=send_sems.at[ai * 3 + k], recv_sem=recv_sems.at[ai * 3 + k],
                       device_id=peer, device_id_type=MESH)
            theirs = 2 * peer[0] + peer[1]
            if kind == "gather":
                snd = pltpu.make_async_remote_copy(src_ref=src, dst_ref=mine, **sem)
                rcv = pltpu.make_async_remote_copy(src_ref=src, dst_ref=_shard_of(dst, ax, theirs, n), **sem)
                triples.append((snd.start, nothing, lambda snd=snd, rcv=rcv: (snd.wait_send(), rcv.wait_recv())))
            elif kind == "scatter":
                snd = pltpu.make_async_remote_copy(src_ref=_shard_of(src, ax, theirs, n), dst_ref=dst.at[1 + k], **sem)
                triples.append((snd.start, nothing, snd.wait))
            else:
                landed = _block_half(dst, ax, theirs, n, c)
                snd = pltpu.make_async_remote_copy(src_ref=_half_rows(src, c), dst_ref=_block_half(dst, ax, me, n, c), **sem)
                rcv = pltpu.make_async_remote_copy(src_ref=_half_rows(src, c), dst_ref=landed, **sem)
                sem2 = dict(send_sem=send2_sems.at[ai * 3 + k], recv_sem=recv2_sems.at[ai * 3 + k],
                            device_id=sibling, device_id_type=MESH)
                fwd = pltpu.make_async_remote_copy(src_ref=landed, dst_ref=landed, **sem2)
                got = pltpu.make_async_remote_copy(src_ref=landed, dst_ref=_block_half(dst, ax, theirs, n, 1 - c), **sem2)
                triples.append((snd.start, lambda rcv=rcv, fwd=fwd: (rcv.wait_recv(), fwd.start()),
                                lambda snd=snd, fwd=fwd, got=got: (snd.wait_send(), fwd.wait_send(), got.wait_recv())))
    return triples


def _exchange_shapes(kind, arrays, axes):
    out = []
    for arr, ax in zip(arrays, axes):
        shp = list(arr.shape)
        if kind == "scatter":
            shp[ax] //= 4
            out.append(jax.ShapeDtypeStruct((4, *shp), arr.dtype))
        else:
            shp[ax] *= 4
            out.append(jax.ShapeDtypeStruct(tuple(shp), arr.dtype))
    return out


def _exchange_sems(kind, n):
    sems = [pltpu.SemaphoreType.DMA((3 * n,)), pltpu.SemaphoreType.DMA((3 * n,)), pltpu.SemaphoreType.DMA((n,))]
    if kind == "gather_halves":
        sems += [pltpu.SemaphoreType.DMA((3 * n,)), pltpu.SemaphoreType.DMA((3 * n,))]
    return sems


def matmul(a, b, *, ta=False, tb=False, out_dtype=F32, residual=None, exchange=None, name):
    a_pair = a if isinstance(a, (tuple, list)) else None
    b_pair = b if isinstance(b, (tuple, list)) else None
    assert not (a_pair and ta) and not (b_pair and tb) and not (a_pair and b_pair)
    a0 = a_pair[0] if a_pair else a
    b0 = b_pair[0] if b_pair else b
    m = a0.shape[1] if ta else a0.shape[0]
    kdim = (a0.shape[0] if ta else a0.shape[1]) * (2 if a_pair else 1)
    n = (b0.shape[0] if tb else b0.shape[1]) * (2 if b_pair else 1)
    assert (b0.shape[1] if tb else b0.shape[0]) == kdim
    tm = _pick(m, (1024, 1280, 640, 512, 256, 128))
    tn = _pick(n // 2 if b_pair else n, (1024, 1280, 640, 512, 256, 128))
    tk = _pick(kdim // 2 if a_pair else kdim, (2560, 2048, 1024, 512, 256, 128))
    grid = (m // tm, n // tn, kdim // tk)
    nk = grid[2]
    kh, jh = nk // 2, grid[1] // 2
    dn = (((0 if ta else 1,), (1 if tb else 0,)), ((), ()))
    na = 2 if a_pair else 1
    nb = 2 if b_pair else 1
    nres = 0 if residual is None else 1
    nex = 0 if exchange is None else len(exchange[1])

    def body(*refs):
        a_refs, b_refs = refs[:na], refs[na:na + nb]
        p = na + nb
        r_ref = refs[p] if nres else None
        ex_in = refs[p + nres:p + nres + nex]
        o_ref = refs[p + nres + nex]
        ex_out = refs[p + 1 + nres + nex:p + 1 + nres + 2 * nex]
        acc = refs[p + 1 + nres + 2 * nex]
        sems = refs[p + 2 + nres + 2 * nex:]
        i, j, k = pl.program_id(0), pl.program_id(1), pl.program_id(2)
        if nex:
            @pl.when((i == 0) & (j == 0) & (k == 0))
            def _():
                for start, _, _ in _exchange(exchange[0], ex_in, ex_out, exchange[2], *sems):
                    start()

        @pl.when(k == 0)
        def _():
            acc[...] = jnp.zeros_like(acc)

        def accumulate(a_ref, b_ref):
            acc[...] += lax.dot_general(a_ref[...].astype(BF16), b_ref[...].astype(BF16), dn,
                                        preferred_element_type=F32)

        if a_pair:
            pl.when(k < kh)(lambda: accumulate(a_refs[0], b_refs[0]))
            pl.when(k >= kh)(lambda: accumulate(a_refs[1], b_refs[0]))
        elif b_pair:
            pl.when(j < jh)(lambda: accumulate(a_refs[0], b_refs[0]))
            pl.when(j >= jh)(lambda: accumulate(a_refs[0], b_refs[1]))
        else:
            accumulate(a_refs[0], b_refs[0])

        @pl.when(k == nk - 1)
        def _():
            r = acc[...]
            if r_ref is not None:
                r = r + r_ref[...]
            o_ref[...] = r.astype(out_dtype)

        if nex:
            @pl.when((i == grid[0] - 1) & (j == grid[1] - 1) & (k == nk - 1))
            def _():
                triples = _exchange(exchange[0], ex_in, ex_out, exchange[2], *sems)
                for _, relay, _ in triples:
                    relay()
                for _, _, finish in triples:
                    finish()

    if a_pair:
        a_specs = [pl.BlockSpec((tm, tk), lambda i, j, k: (i, jnp.minimum(k, kh - 1))),
                   pl.BlockSpec((tm, tk), lambda i, j, k: (i, jnp.maximum(k - kh, 0)))]
    else:
        a_specs = [pl.BlockSpec((tk, tm), lambda i, j, k: (k, i)) if ta
                   else pl.BlockSpec((tm, tk), lambda i, j, k: (i, k))]
    if b_pair:
        b_specs = [pl.BlockSpec((tk, tn), lambda i, j, k: (jnp.where(j < jh, k, nk - 1), jnp.minimum(j, jh - 1))),
                   pl.BlockSpec((tk, tn), lambda i, j, k: (jnp.where(j >= jh, k, 0), jnp.maximum(j - jh, 0)))]
    else:
        b_specs = [pl.BlockSpec((tn, tk), lambda i, j, k: (j, k)) if tb
                   else pl.BlockSpec((tk, tn), lambda i, j, k: (k, j))]
    o_spec = pl.BlockSpec((tm, tn), lambda i, j, k: (i, j))
    hbm = pl.BlockSpec(memory_space=pl.ANY)
    in_specs = a_specs + b_specs + [o_spec] * nres + [hbm] * nex
    args = (list(a_pair) if a_pair else [a]) + (list(b_pair) if b_pair else [b])
    args += ([residual] if nres else []) + (list(exchange[1]) if nex else [])
    out_shape = [jax.ShapeDtypeStruct((m, n), out_dtype)]
    scratch = [pltpu.VMEM((tm, tn), F32)]
    if nex:
        out_shape += _exchange_shapes(*exchange)
        scratch += _exchange_sems(exchange[0], nex)
    outs = pl.pallas_call(
        body, name=name, grid=grid,
        in_specs=in_specs, out_specs=[o_spec] + [hbm] * nex, out_shape=out_shape,
        scratch_shapes=scratch,
        compiler_params=_params(("arbitrary",) * 3 if nex else ("parallel", "parallel", "arbitrary")),
    )(*args)
    return (outs[0], list(outs[1:])) if nex else outs[0]


def rms_fwd(x, gains, *, name):
    s, d = x.shape
    tr = _pick(s, (512, 256, 128, 8))
    ng = len(gains)

    def body(*refs):
        x_ref = refs[0]
        g_refs = refs[1:1 + ng]
        o_refs = refs[1 + ng:]
        xv = x_ref[...]
        y = xv * lax.rsqrt(jnp.mean(xv * xv, axis=-1, keepdims=True) + EPS)
        for g_ref, o_ref in zip(g_refs, o_refs):
            o_ref[...] = (y * g_ref[...]).astype(BF16)

    row = pl.BlockSpec((tr, d), lambda i: (i, 0))
    vec = pl.BlockSpec((1, d), lambda i: (0, 0))
    return pl.pallas_call(
        body, name=name, grid=(s // tr,),
        in_specs=[row] + [vec] * ng, out_specs=[row] * ng,
        out_shape=[jax.ShapeDtypeStruct((s, d), BF16)] * ng,
        compiler_params=_params(("parallel",)),
    )(x, *gains)


def rms_bwd(x, dres, norms, *, name):
    s, d = x.shape
    tr = _pick(s, (256, 128, 8))
    ng = len(norms)

    def body(*refs):
        x_ref, dres_ref = refs[0], refs[1]
        g_refs = refs[2:2 + ng]
        dh_refs = refs[2 + ng:2 + 2 * ng]
        dx_ref, dxb_ref = refs[2 + 2 * ng], refs[3 + 2 * ng]
        dg_refs = refs[4 + 2 * ng:]
        i = pl.program_id(0)
        xv = x_ref[...]
        r = lax.rsqrt(jnp.mean(xv * xv, axis=-1, keepdims=True) + EPS)
        xhat = xv * r
        dx = dres_ref[...]
        for g_ref, dh_ref, dg_ref in zip(g_refs, dh_refs, dg_refs):
            dh = dh_ref[...].astype(F32)
            part = jnp.sum(dh * xhat, axis=0, keepdims=True)

            @pl.when(i == 0)
            def _():
                dg_ref[...] = part

            @pl.when(i > 0)
            def _():
                dg_ref[...] += part

            dxhat = dh * g_ref[...]
            dx = dx + r * (dxhat - xhat * jnp.mean(dxhat * xhat, axis=-1, keepdims=True))
        dx_ref[...] = dx
        dxb_ref[...] = dx.astype(BF16)

    row = pl.BlockSpec((tr, d), lambda i: (i, 0))
    vec = pl.BlockSpec((1, d), lambda i: (0, 0))
    outs = pl.pallas_call(
        body, name=name, grid=(s // tr,),
        in_specs=[row, row] + [vec] * ng + [row] * ng,
        out_specs=[row, row] + [vec] * ng,
        out_shape=[jax.ShapeDtypeStruct((s, d), F32), jax.ShapeDtypeStruct((s, d), BF16)]
        + [jax.ShapeDtypeStruct((1, d), F32)] * ng,
        compiler_params=_params(("arbitrary",)),
    )(x, dres, *[g for g, _ in norms], *[dh for _, dh in norms])
    return outs[0], outs[1], list(outs[2:])


def loss_bwd(x2, target, gain, *, name):
    s, d = x2.shape
    tr = _pick(s, (256, 128, 8))
    nsteps = s // tr

    def body(x_ref, t_ref, g_ref, loss_ref, dg_ref, dx_ref, dxb_ref, sq_acc):
        i = pl.program_id(0)
        xv = x_ref[...]
        r = lax.rsqrt(jnp.mean(xv * xv, axis=-1, keepdims=True) + EPS)
        xhat = xv * r
        g = g_ref[...]
        err = xhat * g - t_ref[...]
        dy = err * (1.0 / d)
        sq = jnp.sum(err * err, axis=0, keepdims=True)
        dgp = jnp.sum(dy * xhat, axis=0, keepdims=True)

        @pl.when(i == 0)
        def _():
            sq_acc[...] = sq
            dg_ref[...] = dgp

        @pl.when(i > 0)
        def _():
            sq_acc[...] += sq
            dg_ref[...] += dgp

        dxhat = dy * g
        dx = r * (dxhat - xhat * jnp.mean(dxhat * xhat, axis=-1, keepdims=True))
        dx_ref[...] = dx
        dxb_ref[...] = dx.astype(BF16)

        @pl.when(i == nsteps - 1)
        def _():
            tot = jnp.sum(sq_acc[...], axis=-1, keepdims=True) * (0.5 / d)
            loss_ref[...] = jnp.broadcast_to(tot, (1, LANES))

    row = pl.BlockSpec((tr, d), lambda i: (i, 0))
    vec = pl.BlockSpec((1, d), lambda i: (0, 0))
    return pl.pallas_call(
        body, name=name, grid=(nsteps,),
        in_specs=[row, row, vec],
        out_specs=[pl.BlockSpec((1, LANES), lambda i: (0, 0)), vec, row, row],
        out_shape=[jax.ShapeDtypeStruct((1, LANES), F32), jax.ShapeDtypeStruct((1, d), F32),
                   jax.ShapeDtypeStruct((s, d), F32), jax.ShapeDtypeStruct((s, d), BF16)],
        scratch_shapes=[pltpu.VMEM((1, d), F32)],
        compiler_params=_params(("arbitrary",)),
    )(x2, target, gain)


def _lru_gates(xb, wr, wi, br, bi, sp):
    xbb = xb.astype(BF16)
    r = _sigmoid(jnp.dot(xbb, wr, preferred_element_type=F32) + br)
    ig = _sigmoid(jnp.dot(xbb, wi, preferred_element_type=F32) + bi)
    log_a = (-LRU_C) * r * sp
    a = jnp.exp(log_a)
    mult = jnp.sqrt(jnp.maximum(-jnp.tanh(log_a) * (a * a + 1.0), 0.0))
    return r, ig, a, mult


def _softplus_neg(lam):
    e = jnp.exp(-jnp.abs(lam))
    sp = jnp.maximum(-lam, 0.0) + jnp.log(1.0 + e)
    sg = jnp.where(lam >= 0, e, 1.0) / (1.0 + e)
    return sp, sg


def _conv(pad_ref, w, b, t):
    acc = b + w[CONV_W - 1:CONV_W, :] * pad_ref[pl.ds(SUBLANES, t), :]
    for dlt in range(1, CONV_W):
        acc = acc + w[CONV_W - 1 - dlt:CONV_W - dlt, :] * pad_ref[pl.ds(SUBLANES - dlt, t), :]
    return acc


def _lru_specs(t, bw, nb, time_of):
    blk = lambda c0: pl.BlockSpec((t, bw), lambda n, i, c0=c0: (time_of(i), c0 + n))
    vec = pl.BlockSpec((1, bw), lambda n, i: (0, n))
    wspec = pl.BlockSpec((None, bw, bw), lambda n, i: (n, 0, 0))
    cwspec = pl.BlockSpec((CONV_W, bw), lambda n, i: (0, n))
    return blk, vec, wspec, cwspec


def lru_fwd(proj, conv_w, conv_b, w_r, b_r, w_i, b_i, lam, *, name):
    s, r2 = proj.shape
    rr = r2 // 2
    nb, bw, _ = w_r.shape
    t = _pick(s, (512, 256, 128, 64, 32))
    ngroups = t // SUBLANES

    def body(xp_ref, gate_ref, cw_ref, cb_ref, wr_ref, br_ref, wi_ref, bi_ref, lam_ref,
             m_ref, h_ref, pad, hcarry, a_scr, u_scr):
        i = pl.program_id(1)

        @pl.when(i == 0)
        def _():
            pad[0:SUBLANES, :] = jnp.zeros((SUBLANES, bw), F32)
            hcarry[...] = jnp.zeros_like(hcarry)

        xpre = xp_ref[...]
        pad[pl.ds(SUBLANES, t), :] = xpre
        xb = _conv(pad, cw_ref[...], cb_ref[...], t)
        pad[0:SUBLANES, :] = xpre[t - SUBLANES:, :]
        sp, _ = _softplus_neg(lam_ref[...])
        _, ig, a, mult = _lru_gates(xb, wr_ref[...], wi_ref[...], br_ref[...], bi_ref[...], sp)
        a_scr[...] = a
        u_scr[...] = mult * (ig * xb)
        row = lax.broadcasted_iota(jnp.int32, (SUBLANES, bw), 0)

        def groups(gi, hprev):
            offs = [pl.multiple_of((gi * SCAN_UNROLL + u) * SUBLANES, SUBLANES) for u in range(SCAN_UNROLL)]
            scanned = []
            for off in offs:
                av = a_scr[pl.ds(off, SUBLANES), :]
                uv = u_scr[pl.ds(off, SUBLANES), :]
                for dlt in (1, 2, 4):
                    keep = row >= dlt
                    uv = jnp.where(keep, av * pltpu.roll(uv, dlt, 0) + uv, uv)
                    av = jnp.where(keep, av * pltpu.roll(av, dlt, 0), av)
                scanned.append((av, uv))
            for off, (av, uv) in zip(offs, scanned):
                hv = av * hprev + uv
                h_ref[pl.ds(off, SUBLANES), :] = hv
                hprev = hv[SUBLANES - 1:SUBLANES, :]
            return hprev

        hcarry[...] = lax.fori_loop(0, ngroups // SCAN_UNROLL, groups, hcarry[...])
        gate = gate_ref[...]
        m_ref[...] = (h_ref[...] * (gate * _sigmoid(gate))).astype(BF16)

    blk, vec, wspec, cwspec = _lru_specs(t, bw, nb, lambda i: i)
    return pl.pallas_call(
        body, name=name, grid=(nb, s // t),
        in_specs=[blk(0), blk(nb), cwspec, vec, wspec, vec, wspec, vec, vec],
        out_specs=[blk(0), blk(0)],
        out_shape=[jax.ShapeDtypeStruct((s, rr), BF16), jax.ShapeDtypeStruct((s, rr), F32)],
        scratch_shapes=[pltpu.VMEM((t + SUBLANES, bw), F32), pltpu.VMEM((1, bw), F32),
                        pltpu.VMEM((t, bw), F32), pltpu.VMEM((t, bw), F32)],
        compiler_params=_params(("parallel", "arbitrary")),
    )(proj, proj, conv_w, conv_b, w_r, b_r, w_i, b_i, lam)


def lru_bwd(proj, hst, dm, conv_w, conv_b, w_r, b_r, w_i, b_i, lam, *, name):
    s, r2 = proj.shape
    rr = r2 // 2
    nb, bw, _ = w_r.shape
    t = _pick(s, (512, 256, 128, 64, 32))
    nt = s // t
    ngroups = t // SUBLANES
    nt_dims = (((1,), (1,)), ((), ()))
    tn_dims = (((0,), (0,)), ((), ()))

    def body(xp_ref, xhalo_ref, gate_ref, h_ref, hhalo_ref, dm_ref, cw_ref, cb_ref, wr_ref, br_ref, wi_ref,
             bi_ref, lam_ref,
             dxp_ref, dgate_ref, dcw_ref, dcb_ref, dwr_ref, dbr_ref, dwi_ref, dbi_ref, dlam_ref,
             pad, hpad, dpad, ecarry, a_scr, b_scr, d_scr):
        step = pl.program_id(1)

        @pl.when(step == 0)
        def _():
            dpad[pl.ds(t, SUBLANES), :] = jnp.zeros((SUBLANES, bw), F32)
            ecarry[...] = jnp.zeros_like(ecarry)
            dcw_ref[...] = jnp.zeros_like(dcw_ref)
            dcb_ref[...] = jnp.zeros_like(dcb_ref)
            dwr_ref[...] = jnp.zeros_like(dwr_ref)
            dbr_ref[...] = jnp.zeros_like(dbr_ref)
            dwi_ref[...] = jnp.zeros_like(dwi_ref)
            dbi_ref[...] = jnp.zeros_like(dbi_ref)
            dlam_ref[...] = jnp.zeros_like(dlam_ref)

        past = jnp.where(step == nt - 1, 0.0, 1.0)
        pad[0:SUBLANES, :] = xhalo_ref[...] * past
        pad[pl.ds(SUBLANES, t), :] = xp_ref[...]
        hpad[0:SUBLANES, :] = hhalo_ref[...] * past
        hpad[pl.ds(SUBLANES, t), :] = h_ref[...]
        cw = cw_ref[...]
        xb = _conv(pad, cw, cb_ref[...], t)
        sp, sg = _softplus_neg(lam_ref[...])
        wr = wr_ref[...]
        wi = wi_ref[...]
        r, ig, a, mult = _lru_gates(xb, wr, wi, br_ref[...], bi_ref[...], sp)
        gate = gate_ref[...]
        sgate = _sigmoid(gate)
        dmv = dm_ref[...]
        dgate_ref[...] = (dmv * h_ref[...] * (sgate * (1.0 + gate * (1.0 - sgate)))).astype(BF16)
        dy = dmv * (gate * sgate)
        a_scr[...] = a
        b_scr[...] = a * dy
        row = lax.broadcasted_iota(jnp.int32, (SUBLANES, bw), 0)

        def groups(gi, enext):
            offs = [pl.multiple_of((ngroups - 1 - gi * SCAN_UNROLL - u) * SUBLANES, SUBLANES)
                    for u in range(SCAN_UNROLL)]
            scanned = []
            for off in offs:
                av = a_scr[pl.ds(off, SUBLANES), :]
                bv = b_scr[pl.ds(off, SUBLANES), :]
                for dlt in (1, 2, 4):
                    keep = row < SUBLANES - dlt
                    bv = jnp.where(keep, av * pltpu.roll(bv, SUBLANES - dlt, 0) + bv, bv)
                    av = jnp.where(keep, av * pltpu.roll(av, SUBLANES - dlt, 0), av)
                scanned.append((av, bv))
            for off, (av, bv) in zip(offs, scanned):
                ev = av * enext + bv
                d_scr[pl.ds(off, SUBLANES), :] = jnp.where(row == SUBLANES - 1, enext,
                                                           pltpu.roll(ev, SUBLANES - 1, 0))
                enext = ev[0:1, :]
            return enext

        ecarry[...] = lax.fori_loop(0, ngroups // SCAN_UNROLL, groups, ecarry[...])
        dtot = dy + d_scr[...]
        da = dtot * hpad[pl.ds(SUBLANES - 1, t), :]
        dmult = dtot * (ig * xb)
        dlog_a = da * a - dmult * (a * a) / mult
        dr_pre = dlog_a * ((-LRU_C) * sp) * (r * (1.0 - r))
        di_pre = (dtot * mult * xb) * (ig * (1.0 - ig))
        dlam_ref[...] += jnp.sum(dlog_a * r, axis=0, keepdims=True) * (LRU_C * sg)
        dbr_ref[...] += jnp.sum(dr_pre, axis=0, keepdims=True)
        dbi_ref[...] += jnp.sum(di_pre, axis=0, keepdims=True)
        drb = dr_pre.astype(BF16)
        dib = di_pre.astype(BF16)
        xbb = xb.astype(BF16)
        dxb = (dtot * mult * ig
               + lax.dot_general(drb, wr, nt_dims, preferred_element_type=F32)
               + lax.dot_general(dib, wi, nt_dims, preferred_element_type=F32))
        dwr_ref[...] += lax.dot_general(xbb, drb, tn_dims, preferred_element_type=F32)
        dwi_ref[...] += lax.dot_general(xbb, dib, tn_dims, preferred_element_type=F32)
        dcb_ref[...] += jnp.sum(dxb, axis=0, keepdims=True)
        dpad[pl.ds(0, t), :] = dxb
        dxpre = cw[CONV_W - 1:CONV_W, :] * dxb
        dcw_ref[CONV_W - 1:CONV_W, :] += jnp.sum(dxb * pad[pl.ds(SUBLANES, t), :], axis=0, keepdims=True)
        for dlt in range(1, CONV_W):
            dxpre = dxpre + cw[CONV_W - 1 - dlt:CONV_W - dlt, :] * dpad[pl.ds(dlt, t), :]
            dcw_ref[CONV_W - 1 - dlt:CONV_W - dlt, :] += jnp.sum(
                dxb * pad[pl.ds(SUBLANES - dlt, t), :], axis=0, keepdims=True)
        dpad[pl.ds(t, SUBLANES), :] = dxb[0:SUBLANES, :]
        dxp_ref[...] = dxpre.astype(BF16)

    rev = lambda i: nt - 1 - i
    blk, vec, wspec, cwspec = _lru_specs(t, bw, nb, rev)
    halo = pl.BlockSpec((SUBLANES, bw), lambda n, i: (jnp.maximum(rev(i) * ngroups - 1, 0), n))
    return pl.pallas_call(
        body, name=name, grid=(nb, nt),
        in_specs=[blk(0), halo, blk(nb), blk(0), halo, blk(0), cwspec, vec, wspec, vec, wspec, vec, vec],
        out_specs=[blk(0), blk(0), cwspec, vec, wspec, vec, wspec, vec, vec],
        out_shape=[jax.ShapeDtypeStruct((s, rr), BF16), jax.ShapeDtypeStruct((s, rr), BF16),
                   jax.ShapeDtypeStruct((CONV_W, rr), F32), jax.ShapeDtypeStruct((1, rr), F32),
                   jax.ShapeDtypeStruct((nb, bw, bw), F32), jax.ShapeDtypeStruct((1, rr), F32),
                   jax.ShapeDtypeStruct((nb, bw, bw), F32), jax.ShapeDtypeStruct((1, rr), F32),
                   jax.ShapeDtypeStruct((1, rr), F32)],
        scratch_shapes=[pltpu.VMEM((t + SUBLANES, bw), F32), pltpu.VMEM((t + SUBLANES, bw), F32),
                        pltpu.VMEM((t + SUBLANES, bw), F32), pltpu.VMEM((1, bw), F32),
                        pltpu.VMEM((t, bw), F32), pltpu.VMEM((t, bw), F32), pltpu.VMEM((t, bw), F32)],
        compiler_params=_params(("parallel", "arbitrary")),
    )(proj, proj, proj, hst, hst, dm, conv_w, conv_b, w_r, b_r, w_i, b_i, lam)


def _softplus(z):
    return jnp.maximum(z, 0.0) + jnp.log(1.0 + jnp.exp2(jnp.abs(z) * (-LOG2E)))


def _att_blocks(s):
    bk = ATT_BLOCK if s % ATT_BLOCK == 0 else s
    bq = ATT_QTILES * bk if s % (ATT_QTILES * bk) == 0 else bk
    return bk, bq


def _tile_base(i, r):
    return r * ((i * (i + 1)) // 2)


def attn_fwd(projb, kv, *, name):
    s, a2 = projb.shape
    a = a2 // 2
    nh = a // HEAD_DIM
    bk, bq = _att_blocks(s)
    r = bq // bk
    nq = s // bq
    ntiles = _tile_base(nq, r)
    scale = 1.0 / math.sqrt(HEAD_DIM)
    nt_dims = (((1,), (1,)), ((), ()))
    hp = ATT_HEADS if nh % ATT_HEADS == 0 else 1
    wd = hp * HEAD_DIM

    def body(q_ref, g_ref, k_ref, v_ref, m_ref, o_ref, saved_hbm, acc, stage, sems):
        hgrp, i = pl.program_id(0), pl.program_id(1)
        base = _tile_base(i, r)
        qb = (q_ref[...] * scale).astype(BF16)
        from_mat = (lax.broadcasted_iota(jnp.int32, (bk, bk), 0)
                    >= lax.broadcasted_iota(jnp.int32, (bk, bk), 1)).astype(BF16)
        rowi = lax.broadcasted_iota(jnp.int32, (bq, bk), 0)
        coli = lax.broadcasted_iota(jnp.int32, (bq, bk), 1)
        cols = [slice(hh * HEAD_DIM, (hh + 1) * HEAD_DIM) for hh in range(hp)]

        def save(slot, j):
            return pltpu.make_async_copy(stage.at[slot], saved_hbm.at[hgrp, base + j], sems.at[slot])

        def tile(j, n, carries, diag):
            r0 = 0 if diag is None else diag * bk
            live = slice(r0, bq)
            causal = None if diag is None else coli[live] < rowi[:bq - r0]
            slot = n % 2

            def free_slot():
                save(slot, 0).wait()

            if isinstance(n, int):
                if n >= 2:
                    free_slot()
            elif r >= 2:
                free_slot()
            else:
                pl.when(n >= 2)(free_slot)
            rows = pl.ds(pl.multiple_of(j * bk, bk), bk)
            zs = [lax.dot_general(qb[live, c], k_ref[rows, c], nt_dims, preferred_element_type=F32) for c in cols]
            sums, sigs = [], []
            for z in zs:
                sp = _softplus(z)
                sig = z - sp
                if causal is not None:
                    sp = jnp.where(causal, sp, 0.0)
                    sig = jnp.where(causal, sig, LOG_ZERO)
                sums.append(jnp.dot(sp.astype(BF16), from_mat, preferred_element_type=F32))
                sigs.append(sig.astype(BF16))
            out = []
            for hh in range(hp):
                w = jnp.exp(zs[hh] - sums[hh] - carries[hh][live])
                if causal is not None:
                    w = jnp.where(causal, w, 0.0)
                wb = w.astype(BF16)
                acc[live, cols[hh]] += jnp.dot(wb, v_ref[rows, cols[hh]], preferred_element_type=F32)
                stage[slot, 0, hh, live] = wb
                stage[slot, 1, hh, live] = sigs[hh]
                if r0:
                    stage[slot, :, hh, :r0] = jnp.zeros((2, r0, bk), BF16)
                grown = carries[hh][live] + sums[hh][:, 0:1]
                out.append(jnp.concatenate([carries[hh][:r0], grown], axis=0) if r0 else grown)
            save(slot, j).start()
            return tuple(out)

        acc[...] = jnp.zeros_like(acc)
        carries = tuple(jnp.zeros((bq, 1), F32) for _ in range(hp))
        for n, dg in enumerate(reversed(range(r))):
            carries = tile(r * i + dg, n, carries, dg)
        lax.fori_loop(0, r * i, lambda jj, c: tile(r * i - 1 - jj, r + jj, c, None), carries)
        ntile = r * (i + 1)
        for back in (1, 2):
            def drain(back=back):
                save((ntile - back) % 2, 0).wait()
            if r >= back:
                drain()
            else:
                pl.when(ntile >= back)(drain)
        o = acc[...]
        o_ref[...] = o
        gate = g_ref[...]
        m_ref[...] = (o * (gate * _sigmoid(gate))).astype(BF16)

    ng = nh // hp
    qspec = lambda c0: pl.BlockSpec((bq, wd), lambda h, i, c0=c0: (i, c0 + h))
    kspec = lambda c0: pl.BlockSpec((s, wd), lambda h, i, c0=c0: (0, c0 + h), pipeline_mode=pl.Buffered(1))
    hbm = pl.BlockSpec(memory_space=pl.ANY)
    saved = jax.ShapeDtypeStruct((ng, ntiles, 2, hp, bq, bk), BF16)
    return pl.pallas_call(
        body, name=name, grid=(ng, nq),
        in_specs=[qspec(0), qspec(ng), kspec(0), kspec(ng)],
        out_specs=[qspec(0), qspec(0), hbm],
        out_shape=[jax.ShapeDtypeStruct((s, a), BF16), jax.ShapeDtypeStruct((s, a), F32), saved],
        scratch_shapes=[pltpu.VMEM((bq, wd), F32), pltpu.VMEM((2, 2, hp, bq, bk), BF16),
                        pltpu.SemaphoreType.DMA((2,))],
        compiler_params=_params(("arbitrary", "arbitrary")),
    )(projb, projb, kv, kv)


def attn_bwd(projb, dm, o, kv, saved, *, name):
    s, a2 = projb.shape
    a = a2 // 2
    nh = a // HEAD_DIM
    bk, bq = _att_blocks(s)
    r = bq // bk
    nq = s // bq
    scale = 1.0 / math.sqrt(HEAD_DIM)
    nt_dims = (((1,), (1,)), ((), ()))
    tn_dims = (((0,), (0,)), ((), ()))
    hp = saved.shape[3]
    wd = hp * HEAD_DIM
    ahead = ATT_FETCH_AHEAD

    def body(q_ref, g_ref, dm_ref, o_ref, k_ref, v_ref, saved_hbm, dq_ref, dg_ref, dk_ref, dv_ref,
             dk_acc, dv_acc, dq_acc, stage, sems):
        hgrp, i = pl.program_id(0), pl.program_id(1)
        base = _tile_base(i, r)
        ntile = r * (i + 1)

        @pl.when(i == 0)
        def _():
            dk_acc[...] = jnp.zeros_like(dk_acc)
            dv_acc[...] = jnp.zeros_like(dv_acc)

        def fetch(j):
            slot = j % (ahead + 1)
            return pltpu.make_async_copy(saved_hbm.at[hgrp, base + j], stage.at[slot], sems.at[slot])

        for j0 in range(ahead):
            pl.when(j0 < ntile)(lambda j0=j0: fetch(j0).start())
        qb = (q_ref[...] * scale).astype(BF16)
        gate = g_ref[...]
        sgate = _sigmoid(gate)
        dmv = dm_ref[...]
        dob = (dmv * (gate * sgate)).astype(BF16)
        dg_ref[...] = (dmv * o_ref[...] * (sgate * (1.0 + gate * (1.0 - sgate)))).astype(BF16)
        upto_mat = (lax.broadcasted_iota(jnp.int32, (bk, bk), 0)
                    <= lax.broadcasted_iota(jnp.int32, (bk, bk), 1)).astype(BF16)
        cols = [slice(hh * HEAD_DIM, (hh + 1) * HEAD_DIM) for hh in range(hp)]
        dq_acc[...] = jnp.zeros_like(dq_acc)

        def tile(j, gcarries, r0=0, more=None):
            live = slice(r0, bq)
            slot = j % (ahead + 1)
            if more is None:
                pl.when(j + ahead < ntile)(lambda: fetch(j + ahead).start())
            elif more:
                fetch(j + ahead).start()
            fetch(j).wait()
            rows = pl.ds(pl.multiple_of(j * bk, bk), bk)
            dws = [lax.dot_general(dob[live, c], v_ref[rows, c], nt_dims, preferred_element_type=F32) for c in cols]
            gs, totals = [], []
            for hh in range(hp):
                wb = stage[slot, 0, hh, live]
                g = wb.astype(F32) * dws[hh]
                dv_acc[rows, cols[hh]] += lax.dot_general(wb, dob[live, cols[hh]], tn_dims,
                                                          preferred_element_type=F32)
                totals.append(jnp.dot(g.astype(BF16), upto_mat, preferred_element_type=F32))
                gs.append(g)
            out = []
            for hh in range(hp):
                dz = gs[hh] - (totals[hh] + gcarries[hh][live]) * jnp.exp(stage[slot, 1, hh, live]).astype(F32)
                dzb = dz.astype(BF16)
                dq_acc[live, cols[hh]] += jnp.dot(dzb, k_ref[rows, cols[hh]], preferred_element_type=F32)
                dk_acc[rows, cols[hh]] += lax.dot_general(dzb, qb[live, cols[hh]], tn_dims,
                                                          preferred_element_type=F32)
                grown = gcarries[hh][live] + totals[hh][:, bk - 1:bk]
                out.append(jnp.concatenate([gcarries[hh][:r0], grown], axis=0) if r0 else grown)
            return tuple(out)

        gcarries = lax.fori_loop(0, r * i, tile, tuple(jnp.zeros((bq, 1), F32) for _ in range(hp)))
        for dg in range(r):
            gcarries = tile(r * i + dg, gcarries, dg * bk, dg + ahead < r)
        dq_ref[...] = (dq_acc[...] * scale).astype(BF16)

        @pl.when(i == nq - 1)
        def _():
            dk_ref[...] = dk_acc[...].astype(BF16)
            dv_ref[...] = dv_acc[...].astype(BF16)

    ng = nh // hp
    once = pl.Buffered(1)
    qspec = lambda c0: pl.BlockSpec((bq, wd), lambda h, i, c0=c0: (i, c0 + h))
    kspec = lambda c0: pl.BlockSpec((s, wd), lambda h, i, c0=c0: (0, c0 + h), pipeline_mode=once)
    hbm = pl.BlockSpec(memory_space=pl.ANY)
    return pl.pallas_call(
        body, name=name, grid=(ng, nq),
        in_specs=[qspec(0), qspec(ng), qspec(0), qspec(0), kspec(0), kspec(ng), hbm],
        out_specs=[qspec(0), qspec(0), kspec(0), kspec(0)],
        out_shape=[jax.ShapeDtypeStruct((s, a), BF16)] * 4,
        scratch_shapes=[pltpu.VMEM((s, wd), F32), pltpu.VMEM((s, wd), F32), pltpu.VMEM((bq, wd), F32),
                        pltpu.VMEM((ahead + 1, 2, hp, bq, bk), BF16), pltpu.SemaphoreType.DMA((ahead + 1,))],
        compiler_params=_params(("arbitrary", "arbitrary"), vmem=V7X_VMEM_LIMIT_HIGH),
    )(projb, projb, dm, o, kv, kv, saved)


def _as2d(x):
    n = x.size
    cols = x.shape[-1]
    if cols % LANES != 0:
        cols = LANES
    return x.reshape(n // cols, cols)


def sum_parts(parts, *, name):
    p, rows, cols = parts.shape
    tr = _pick(rows, (512, 256, 128, 64, 32, 16))

    def body(p_ref, o_ref):
        acc = p_ref[0].astype(F32)
        for k in range(1, p):
            acc = acc + p_ref[k].astype(F32)
        o_ref[...] = acc

    return pl.pallas_call(
        body, name=name, grid=(rows // tr,),
        in_specs=[pl.BlockSpec((p, tr, cols), lambda i: (0, i, 0))],
        out_specs=pl.BlockSpec((tr, cols), lambda i: (i, 0)),
        out_shape=jax.ShapeDtypeStruct((rows, cols), F32),
        compiler_params=_params(("parallel",)),
    )(parts)


def adamw(w, g_parts, m, v, *, name):
    rows, cols = w.shape
    tr = _pick(rows, (128, 64, 32, 16, 8))
    np_ = len(g_parts)
    c1 = 1.0 / (1.0 - ADAM_B1 ** ADAM_STEP)
    c2 = 1.0 / (1.0 - ADAM_B2 ** ADAM_STEP)

    def body(*refs):
        w_ref, m_ref, v_ref = refs[0], refs[1], refs[2]
        g_refs = refs[3:3 + np_]
        go_ref, d_ref, mo_ref, vo_ref = refs[3 + np_:]
        g = g_refs[0][...]
        for gr in g_refs[1:]:
            g = g + gr[...]
        mn = ADAM_B1 * m_ref[...] + (1.0 - ADAM_B1) * g
        vn = ADAM_B2 * v_ref[...] + (1.0 - ADAM_B2) * (g * g)
        go_ref[...] = g
        mo_ref[...] = mn
        vo_ref[...] = vn
        d_ref[...] = (-ADAM_LR) * ((mn * c1) / (jnp.sqrt(vn * c2) + ADAM_EPS) + ADAM_WD * w_ref[...])

    spec = pl.BlockSpec((tr, cols), lambda i: (i, 0))
    return pl.pallas_call(
        body, name=name, grid=(rows // tr,),
        in_specs=[spec] * (3 + np_), out_specs=[spec] * 4,
        out_shape=[jax.ShapeDtypeStruct((rows, cols), F32)] * 4,
        compiler_params=_params(("parallel",)),
    )(w, m, v, *g_parts)


def exchange(kind, arrays, axes, *, name):
    na = len(arrays)
    hbm = pl.BlockSpec(memory_space=pl.ANY)

    def body(*refs):
        triples = _exchange(kind, refs[:na], refs[na:2 * na], axes, *refs[2 * na:])
        for step in range(3):
            for triple in triples:
                triple[step]()

    return pl.pallas_call(
        body, name=name, in_specs=[hbm] * na, out_specs=[hbm] * na,
        out_shape=_exchange_shapes(kind, arrays, axes), scratch_shapes=_exchange_sems(kind, na),
    )(*arrays)


def swap_cores(arrs, *, name):
    na = len(arrs)
    hbm = pl.BlockSpec(memory_space=pl.ANY)

    def body(*refs):
        a_refs = refs[:na]
        o_refs = refs[na:2 * na]
        send_sems, recv_sems = refs[2 * na:]
        x, y, c = _place()
        copies = []
        for ai in range(na):
            cp = pltpu.make_async_remote_copy(
                src_ref=a_refs[ai], dst_ref=o_refs[ai], send_sem=send_sems.at[ai], recv_sem=recv_sems.at[ai],
                device_id=(x, y, 1 - c), device_id_type=MESH)
            cp.start()
            copies.append(cp)
        for cp in copies:
            cp.wait()

    return pl.pallas_call(
        body, name=name,
        in_specs=[hbm] * na, out_specs=[hbm] * na,
        out_shape=[jax.ShapeDtypeStruct(a.shape, a.dtype) for a in arrs],
        scratch_shapes=[pltpu.SemaphoreType.DMA((na,)), pltpu.SemaphoreType.DMA((na,))],
    )(*arrs)


def allreduce_small(arrs, *, name):
    nar = len(arrs)
    width = max(a.shape[1] for a in arrs)
    starts, total = [], 0
    for a in arrs:
        starts.append(total)
        total += a.shape[0]
    total += (-total) % SUBLANES

    def body(*refs):
        in_refs, out_refs = refs[:nar], refs[nar:2 * nar]
        buf, slots, send_sems, recv_sems = refs[2 * nar:]
        x, y, c = _place()
        me = 4 * x + 2 * y + c
        buf[...] = jnp.zeros_like(buf)
        for ref, st in zip(in_refs, starts):
            buf[st:st + ref.shape[0], 0:ref.shape[1]] = ref[...]
        slots[0] = buf[...]
        copies = []
        for rel in range(1, 8):
            peer = (x ^ (rel >> 2), y ^ ((rel >> 1) & 1), c ^ (rel & 1))
            cp = pltpu.make_async_remote_copy(
                src_ref=buf, dst_ref=slots.at[rel], send_sem=send_sems.at[rel - 1],
                recv_sem=recv_sems.at[rel - 1], device_id=peer, device_id_type=MESH)
            cp.start()
            copies.append(cp)
        for cp in copies:
            cp.wait()
        acc = slots[me]
        for dev in range(1, 8):
            acc = acc + slots[dev ^ me]
        buf[...] = acc
        for ref, st in zip(out_refs, starts):
            ref[...] = buf[st:st + ref.shape[0], 0:ref.shape[1]]

    vm = pl.BlockSpec(memory_space=pltpu.VMEM)
    return pl.pallas_call(
        body, name=name, in_specs=[vm] * nar, out_specs=[vm] * nar,
        out_shape=[jax.ShapeDtypeStruct(a.shape, F32) for a in arrs],
        scratch_shapes=[pltpu.VMEM((total, width), F32), pltpu.VMEM((8, total, width), F32),
                        pltpu.SemaphoreType.DMA((7,)), pltpu.SemaphoreType.DMA((7,))],
    )(*arrs)


def adamw_small(ws, gs, ms, vs, *, name):
    n = len(ws)
    c1 = 1.0 / (1.0 - ADAM_B1 ** ADAM_STEP)
    c2 = 1.0 / (1.0 - ADAM_B2 ** ADAM_STEP)

    def body(*refs):
        w_refs, g_refs, m_refs, v_refs = (refs[k * n:(k + 1) * n] for k in range(4))
        d_refs, mo_refs, vo_refs = (refs[(4 + k) * n:(5 + k) * n] for k in range(3))
        for w_ref, g_ref, m_ref, v_ref, d_ref, mo_ref, vo_ref in zip(w_refs, g_refs, m_refs, v_refs,
                                                                     d_refs, mo_refs, vo_refs):
            g = g_ref[...]
            mn = ADAM_B1 * m_ref[...] + (1.0 - ADAM_B1) * g
            vn = ADAM_B2 * v_ref[...] + (1.0 - ADAM_B2) * (g * g)
            mo_ref[...] = mn
            vo_ref[...] = vn
            d_ref[...] = (-ADAM_LR) * ((mn * c1) / (jnp.sqrt(vn * c2) + ADAM_EPS) + ADAM_WD * w_ref[...])

    vm = pl.BlockSpec(memory_space=pltpu.VMEM)
    outs = pl.pallas_call(
        body, name=name, in_specs=[vm] * (4 * n), out_specs=[vm] * (3 * n),
        out_shape=[jax.ShapeDtypeStruct(w.shape, F32) for w in ws] * 3,
    )(*ws, *gs, *ms, *vs)
    return outs[:n], outs[n:2 * n], outs[2 * n:]


def kernel(x, a_norm, a_w_in, a_conv_w, a_conv_b, a_w_r, a_b_r, a_w_i, a_b_i, a_lambda, a_w_out, kv_norm, w_kv, b_norm, b_w_in, b_w_out, final_norm, loss_target, m_a_norm, m_a_w_in, m_a_conv_w, m_a_conv_b, m_a_w_r, m_a_b_r, m_a_w_i, m_a_b_i, m_a_lambda, m_a_w_out, m_kv_norm, m_w_kv, m_b_norm, m_b_w_in, m_b_w_out, m_final_norm, v_a_norm, v_a_w_in, v_a_conv_w, v_a_conv_b, v_a_w_r, v_a_b_r, v_a_w_i, v_a_b_i, v_a_lambda, v_a_w_out, v_kv_norm, v_w_kv, v_b_norm, v_b_w_in, v_b_w_out, v_final_norm):
    weights = dict(a_norm=a_norm, a_w_in=a_w_in, a_conv_w=a_conv_w, a_conv_b=a_conv_b, a_w_r=a_w_r, a_b_r=a_b_r,
                   a_w_i=a_w_i, a_b_i=a_b_i, a_lambda=a_lambda, a_w_out=a_w_out, kv_norm=kv_norm, w_kv=w_kv,
                   b_norm=b_norm, b_w_in=b_w_in, b_w_out=b_w_out, final_norm=final_norm)
    mom1 = dict(a_norm=m_a_norm, a_w_in=m_a_w_in, a_conv_w=m_a_conv_w, a_conv_b=m_a_conv_b, a_w_r=m_a_w_r,
                a_b_r=m_a_b_r, a_w_i=m_a_w_i, a_b_i=m_a_b_i, a_lambda=m_a_lambda, a_w_out=m_a_w_out,
                kv_norm=m_kv_norm, w_kv=m_w_kv, b_norm=m_b_norm, b_w_in=m_b_w_in, b_w_out=m_b_w_out,
                final_norm=m_final_norm)
    mom2 = dict(a_norm=v_a_norm, a_w_in=v_a_w_in, a_conv_w=v_a_conv_w, a_conv_b=v_a_conv_b, a_w_r=v_a_w_r,
                a_b_r=v_a_b_r, a_w_i=v_a_w_i, a_b_i=v_a_b_i, a_lambda=v_a_lambda, a_w_out=v_a_w_out,
                kv_norm=v_kv_norm, w_kv=v_w_kv, b_norm=v_b_norm, b_w_in=v_b_w_in, b_w_out=v_b_w_out,
                final_norm=v_final_norm)
    order = list(weights)
    x0 = x[0]
    target = loss_target[0]
    d = x0.shape[1]
    chip = 2 * lax.axis_index("x") + lax.axis_index("y")

    big = ["a_w_in", "a_w_r", "a_w_i", "a_w_out", "w_kv", "b_w_in", "b_w_out"]
    big_axis = dict(a_w_in=1, a_w_r=1, a_w_i=1, a_w_out=0, w_kv=1, b_w_in=1, b_w_out=0)
    local = dict(a_w_in=a_w_in[0], a_w_r=a_w_r[0], a_w_i=a_w_i[0], a_w_out=a_w_out[0], w_kv=w_kv,
                 b_w_in=b_w_in[0], b_w_out=b_w_out[0])
    shards = {n: local[n].astype(BF16) for n in big}
    first = ["a_w_in", "a_w_r", "a_w_i"]
    full = exchange("gather_halves", [shards[n] for n in first], [big_axis[n] for n in first], name="gather_first")
    full += exchange("gather", [a_conv_w[0], b_norm], [1, 1], name="gather_small")
    wf = dict(zip(first + ["a_conv_w", "b_norm"], full))
    wf.update(a_norm=a_norm, a_conv_b=a_conv_b, a_b_r=a_b_r, a_b_i=a_b_i, a_lambda=a_lambda,
              kv_norm=kv_norm.reshape(1, d), final_norm=final_norm.reshape(1, d))
    loss_part, grad_x, parts, gsmall = _local_grads(x0, target, wf, shards=shards, axes=big_axis)

    sums = [sum_parts(parts[n].reshape(4, *_as2d(parts[n][0]).shape), name="sum_" + n) for n in big]
    others = swap_cores(sums, name="swap_cores")

    small = ["a_norm", "a_conv_b", "a_b_r", "a_b_i", "a_lambda", "kv_norm", "final_norm", "a_conv_w", "b_norm"]
    *red, loss_sum = allreduce_small([gsmall[n] for n in small] + [loss_part], name="allreduce_small")
    gs = dict(zip(small, red))
    loss = loss_sum[0, 0]
    n_conv = a_conv_w.shape[2]
    gs["a_conv_w"] = lax.dynamic_slice_in_dim(gs["a_conv_w"], chip * n_conv, n_conv, axis=1)
    n_bn = b_norm.shape[1]
    gs["b_norm"] = lax.dynamic_slice_in_dim(gs["b_norm"], chip * n_bn, n_bn, axis=1)

    grads, deltas, new_m, new_v = {}, {}, {}, {}
    for n, s_mine, s_other in zip(big, sums, others):
        shp = weights[n].shape
        g, dlt, mn, vn = adamw(_as2d(weights[n]), [s_mine, s_other], _as2d(mom1[n]), _as2d(mom2[n]),
                               name="adamw_" + n)
        grads[n], deltas[n], new_m[n], new_v[n] = (t.reshape(shp) for t in (g, dlt, mn, vn))
    as_g = lambda src: [src[n].reshape(gs[n].shape) for n in small]
    outs = adamw_small(as_g(weights), [gs[n] for n in small], as_g(mom1), as_g(mom2), name="adamw_small")
    for dst, vals in zip((deltas, new_m, new_v), outs):
        for n, val in zip(small, vals):
            dst[n] = val.reshape(weights[n].shape)
    for n in small:
        grads[n] = gs[n].reshape(weights[n].shape)

    return (loss, grad_x[None], *[grads[n] for n in order], *[deltas[n] for n in order],
            *[new_m[n] for n in order], *[new_v[n] for n in order])


def _local_grads(x0, target, wf, shards=None, axes=None):
    a_norm, a_conv_b, a_b_r, a_b_i, a_lambda = (wf[n] for n in ("a_norm", "a_conv_b", "a_b_r", "a_b_i", "a_lambda"))
    kv_norm, final_norm = wf["kv_norm"], wf["final_norm"]
    wf = dict(wf)
    parts = {}

    def mm(*args, gather=(), scatter=None, **kw):
        if shards is None or not (gather or scatter):
            return matmul(*args, **kw)
        if gather:
            out, got = matmul(*args, exchange=("gather_halves", [shards[n] for n in gather], [axes[n] for n in gather]),
                              **kw)
            wf.update(zip(gather, got))
        else:
            out, got = matmul(*args, exchange=("scatter", list(scatter.values()), [axes[n] for n in scatter]), **kw)
            parts.update(zip(scatter, got))
        return out

    (h_a,) = rms_fwd(x0, [a_norm], name="norm_a")
    proj_a = mm(h_a, wf["a_w_in"], gather=("a_w_out",), name="a_in")
    m_a, hst = lru_fwd(proj_a, wf["a_conv_w"], a_conv_b, wf["a_w_r"], a_b_r, wf["a_w_i"], a_b_i, a_lambda,
                       name="lru_fwd")
    x1 = mm(m_a, wf["a_w_out"], residual=x0, gather=("w_kv",), name="a_out")
    kvn, hb = rms_fwd(x1, [kv_norm, wf["b_norm"]], name="norm_kv_b")
    kv = mm(kvn, wf["w_kv"], out_dtype=BF16, gather=("b_w_in",), name="kv_proj")
    proj_b = mm(hb, wf["b_w_in"], gather=("b_w_out",), name="b_in")
    m_b, o, saved = attn_fwd(proj_b, kv, name="attn_fwd")
    x2 = mm(m_b, wf["b_w_out"], residual=x1, name="b_out")
    loss_part, g_final, dx2, dx2b = loss_bwd(x2, target, final_norm, name="loss_bwd")

    dm_b = mm(dx2b, wf["b_w_out"], tb=True, name="b_out_dx")
    g_b_w_out = mm(m_b, dx2b, ta=True, out_dtype=BF16, name="b_out_dw")
    dq, dgate_b, dk, dv = attn_bwd(proj_b, dm_b, o, kv, saved, name="attn_bwd")
    dproj_b = (dq, dgate_b)
    dkv = (dk, dv)
    g_b_w_in = mm(hb, dproj_b, ta=True, out_dtype=BF16, scatter=dict(b_w_out=g_b_w_out), name="b_in_dw")
    g_w_kv = mm(kvn, dkv, ta=True, out_dtype=BF16, scatter=dict(b_w_in=g_b_w_in), name="kv_dw")
    dhb = mm(dproj_b, wf["b_w_in"], tb=True, out_dtype=BF16, scatter=dict(w_kv=g_w_kv), name="b_in_dx")
    dkvn = mm(dkv, wf["w_kv"], tb=True, out_dtype=BF16, name="kv_dx")
    dx1, dx1b, (g_kv_norm, g_b_norm) = rms_bwd(
        x1, dx2, [(kv_norm, dkvn), (wf["b_norm"], dhb)], name="norm_kv_b_bwd")

    g_a_w_out = mm(m_a, dx1b, ta=True, out_dtype=BF16, name="a_out_dw")
    dm_a = mm(dx1b, wf["a_w_out"], tb=True, scatter=dict(a_w_out=g_a_w_out), name="a_out_dx")
    dxpre, dgate_a, g_conv_w, g_conv_b, g_w_r, g_b_r, g_w_i, g_b_i, g_lambda = lru_bwd(
        proj_a, hst, dm_a, wf["a_conv_w"], a_conv_b, wf["a_w_r"], a_b_r, wf["a_w_i"], a_b_i, a_lambda,
        name="lru_bwd")
    dproj_a = (dxpre, dgate_a)
    g_w_r, g_w_i = g_w_r.astype(BF16), g_w_i.astype(BF16)
    g_a_w_in = mm(h_a, dproj_a, ta=True, out_dtype=BF16, scatter=dict(a_w_r=g_w_r, a_w_i=g_w_i), name="a_in_dw")
    dh_a = mm(dproj_a, wf["a_w_in"], tb=True, out_dtype=BF16, scatter=dict(a_w_in=g_a_w_in), name="a_in_dx")
    grad_x, _, (g_a_norm,) = rms_bwd(x0, dx1, [(a_norm, dh_a)], name="norm_a_bwd")

    gbig = parts if shards is not None else dict(
        a_w_in=g_a_w_in, a_w_r=g_w_r, a_w_i=g_w_i, a_w_out=g_a_w_out, w_kv=g_w_kv, b_w_in=g_b_w_in, b_w_out=g_b_w_out)
    gsmall = dict(a_norm=g_a_norm, a_conv_b=g_conv_b, a_b_r=g_b_r, a_b_i=g_b_i, a_lambda=g_lambda,
                  kv_norm=g_kv_norm, final_norm=g_final, a_conv_w=g_conv_w, b_norm=g_b_norm)
    return loss_part, grad_x, gbig, gsmall
```

```python
import math

import jax
import jax.numpy as jnp
from jax import lax
from jax.experimental import pallas as pl
from jax.experimental.pallas import tpu as pltpu

F32 = jnp.float32
BF16 = jnp.bfloat16
MESH = pl.DeviceIdType.MESH

EPS = 1e-6
LRU_C = 8.0
CONV_W = 4
HEAD_DIM = 128
ADAM_LR = 0.001
ADAM_B1 = 0.9
ADAM_B2 = 0.999
ADAM_EPS = 1e-08
ADAM_WD = 0.01
ADAM_STEP = 10

V7X_VMEM_LIMIT = 56 * 1024 * 1024
V7X_VMEM_LIMIT_HIGH = 60 * 1024 * 1024
LANES = 128
SUBLANES = 8
ATT_BLOCK = 256
ATT_QTILES = 4
ATT_HEADS = 2
ATT_FETCH_AHEAD = 2
LOG2E = 1.4426950408889634
LOG_ZERO = -1e30
SCAN_UNROLL = 4


def _pick(dim, cands):
    for c in cands:
        if dim % c == 0:
            return c
    return dim


def _params(sem, vmem=V7X_VMEM_LIMIT):
    return pltpu.CompilerParams(dimension_semantics=sem, vmem_limit_bytes=vmem)


def _sigmoid(x):
    return 1.0 / (1.0 + jnp.exp(-x))


def _place():
    return lax.axis_index("x"), lax.axis_index("y"), lax.axis_index("c")


def _chip_peers(x, y, c):
    return [(1 - x, y, c), (x, 1 - y, c), (1 - x, 1 - y, c)]


def _shard_of(ref, axis, idx, n):
    start = pl.multiple_of(idx * n, n)
    sl = [slice(None)] * len(ref.shape)
    sl[axis] = pl.ds(start, n)
    return ref.at[tuple(sl)]


def _half_rows(ref, h):
    n = ref.shape[0] // 2
    return ref.at[pl.ds(pl.multiple_of(h * n, n), n)]


def _block_half(ref, axis, idx, n, h):
    if axis == 0:
        return ref.at[pl.ds(pl.multiple_of(idx * n + h * (n // 2), n // 2), n // 2)]
    return _half_rows(_shard_of(ref, axis, idx, n), h)


def _exchange(kind, in_refs, out_refs, axes, send_sems, recv_sems, local_sems, send2_sems=None, recv2_sems=None):
    x, y, c = _place()
    me = 2 * x + y
    peers = _chip_peers(x, y, c)
    sibling = (x, y, 1 - c)
    triples = []
    nothing = lambda: None
    for ai, (src, dst, ax) in enumerate(zip(in_refs, out_refs, axes)):
        if kind == "scatter":
            n = dst.shape[1 + ax]
            loc = pltpu.make_async_copy(_shard_of(src, ax, me, n), dst.at[0], local_sems.at[ai])
        else:
            n = src.shape[ax]
            mine = _shard_of(dst, ax, me, n)
            loc = pltpu.make_async_copy(src, mine, local_sems.at[ai])
        triples.append((loc.start, nothing, loc.wait))
        for k, peer in enumerate(peers):
            sem = dict(send_sem=send_sems.at[ai * 3 + k], recv_sem=recv_sems.at[ai * 3 + k],
                       device_id=peer, device_id_type=MESH)
            theirs = 2 * peer[0] + peer[1]
            if kind == "gather":
                snd = pltpu.make_async_remote_copy(src_ref=src, dst_ref=mine, **sem)
                rcv = pltpu.make_async_remote_copy(src_ref=src, dst_ref=_shard_of(dst, ax, theirs, n), **sem)
                triples.append((snd.start, nothing, lambda snd=snd, rcv=rcv: (snd.wait_send(), rcv.wait_recv())))
            elif kind == "scatter":
                snd = pltpu.make_async_remote_copy(src_ref=_shard_of(src, ax, theirs, n), dst_ref=dst.at[1 + k], **sem)
                triples.append((snd.start, nothing, snd.wait))
            else:
                landed = _block_half(dst, ax, theirs, n, c)
                snd = pltpu.make_async_remote_copy(src_ref=_half_rows(src, c), dst_ref=_block_half(dst, ax, me, n, c), **sem)
                rcv = pltpu.make_async_remote_copy(src_ref=_half_rows(src, c), dst_ref=landed, **sem)
                sem2 = dict(send_sem=send2_sems.at[ai * 3 + k], recv_sem=recv2_sems.at[ai * 3 + k],
                            device_id=sibling, device_id_type=MESH)
                fwd = pltpu.make_async_remote_copy(src_ref=landed, dst_ref=landed, **sem2)
                got = pltpu.make_async_remote_copy(src_ref=landed, dst_ref=_block_half(dst, ax, theirs, n, 1 - c), **sem2)
                triples.append((snd.start, lambda rcv=rcv, fwd=fwd: (rcv.wait_recv(), fwd.start()),
                                lambda snd=snd, fwd=fwd, got=got: (snd.wait_send(), fwd.wait_send(), got.wait_recv())))
    return triples


def _exchange_shapes(kind, arrays, axes):
    out = []
    for arr, ax in zip(arrays, axes):
        shp = list(arr.shape)
        if kind == "scatter":
            shp[ax] //= 4
            out.append(jax.ShapeDtypeStruct((4, *shp), arr.dtype))
        else:
            shp[ax] *= 4
            out.append(jax.ShapeDtypeStruct(tuple(shp), arr.dtype))
    return out


def _exchange_sems(kind, n):
    sems = [pltpu.SemaphoreType.DMA((3 * n,)), pltpu.SemaphoreType.DMA((3 * n,)), pltpu.SemaphoreType.DMA((n,))]
    if kind == "gather_halves":
        sems += [pltpu.SemaphoreType.DMA((3 * n,)), pltpu.SemaphoreType.DMA((3 * n,))]
    return sems


def matmul(a, b, *, ta=False, tb=False, out_dtype=F32, residual=None, exchange=None, name):
    a_pair = a if isinstance(a, (tuple, list)) else None
    b_pair = b if isinstance(b, (tuple, list)) else None
    assert not (a_pair and ta) and not (b_pair and tb) and not (a_pair and b_pair)
    a0 = a_pair[0] if a_pair else a
    b0 = b_pair[0] if b_pair else b
    m = a0.shape[1] if ta else a0.shape[0]
    kdim = (a0.shape[0] if ta else a0.shape[1]) * (2 if a_pair else 1)
    n = (b0.shape[0] if tb else b0.shape[1]) * (2 if b_pair else 1)
    assert (b0.shape[1] if tb else b0.shape[0]) == kdim
    tm = _pick(m, (1024, 1280, 640, 512, 256, 128))
    tn = _pick(n // 2 if b_pair else n, (1024, 1280, 640, 512, 256, 128))
    tk = _pick(kdim // 2 if a_pair else kdim, (2560, 2048, 1024, 512, 256, 128))
    grid = (m // tm, n // tn, kdim // tk)
    nk = grid[2]
    kh, jh = nk // 2, grid[1] // 2
    dn = (((0 if ta else 1,), (1 if tb else 0,)), ((), ()))
    na = 2 if a_pair else 1
    nb = 2 if b_pair else 1
    nres = 0 if residual is None else 1
    nex = 0 if exchange is None else len(exchange[1])

    def body(*refs):
        a_refs, b_refs = refs[:na], refs[na:na + nb]
        p = na + nb
        r_ref = refs[p] if nres else None
        ex_in = refs[p + nres:p + nres + nex]
        o_ref = refs[p + nres + nex]
        ex_out = refs[p + 1 + nres + nex:p + 1 + nres + 2 * nex]
        acc = refs[p + 1 + nres + 2 * nex]
        sems = refs[p + 2 + nres + 2 * nex:]
        i, j, k = pl.program_id(0), pl.program_id(1), pl.program_id(2)
        if nex:
            @pl.when((i == 0) & (j == 0) & (k == 0))
            def _():
                for start, _, _ in _exchange(exchange[0], ex_in, ex_out, exchange[2], *sems):
                    start()

        @pl.when(k == 0)
        def _():
            acc[...] = jnp.zeros_like(acc)

        def accumulate(a_ref, b_ref):
            acc[...] += lax.dot_general(a_ref[...].astype(BF16), b_ref[...].astype(BF16), dn,
                                        preferred_element_type=F32)

        if a_pair:
            pl.when(k < kh)(lambda: accumulate(a_refs[0], b_refs[0]))
            pl.when(k >= kh)(lambda: accumulate(a_refs[1], b_refs[0]))
        elif b_pair:
            pl.when(j < jh)(lambda: accumulate(a_refs[0], b_refs[0]))
            pl.when(j >= jh)(lambda: accumulate(a_refs[0], b_refs[1]))
        else:
            accumulate(a_refs[0], b_refs[0])

        @pl.when(k == nk - 1)
        def _():
            r = acc[...]
            if r_ref is not None:
                r = r + r_ref[...]
            o_ref[...] = r.astype(out_dtype)

        if nex:
            @pl.when((i == grid[0] - 1) & (j == grid[1] - 1) & (k == nk - 1))
            def _():
                triples = _exchange(exchange[0], ex_in, ex_out, exchange[2], *sems)
                for _, relay, _ in triples:
                    relay()
                for _, _, finish in triples:
                    finish()

    if a_pair:
        a_specs = [pl.BlockSpec((tm, tk), lambda i, j, k: (i, jnp.minimum(k, kh - 1))),
                   pl.BlockSpec((tm, tk), lambda i, j, k: (i, jnp.maximum(k - kh, 0)))]
    else:
        a_specs = [pl.BlockSpec((tk, tm), lambda i, j, k: (k, i)) if ta
                   else pl.BlockSpec((tm, tk), lambda i, j, k: (i, k))]
    if b_pair:
        b_specs = [pl.BlockSpec((tk, tn), lambda i, j, k: (jnp.where(j < jh, k, nk - 1), jnp.minimum(j, jh - 1))),
                   pl.BlockSpec((tk, tn), lambda i, j, k: (jnp.where(j >= jh, k, 0), jnp.maximum(j - jh, 0)))]
    else:
        b_specs = [pl.BlockSpec((tn, tk), lambda i, j, k: (j, k)) if tb
                   else pl.BlockSpec((tk, tn), lambda i, j, k: (k, j))]
    o_spec = pl.BlockSpec((tm, tn), lambda i, j, k: (i, j))
    hbm = pl.BlockSpec(memory_space=pl.ANY)
    in_specs = a_specs + b_specs + [o_spec] * nres + [hbm] * nex
    args = (list(a_pair) if a_pair else [a]) + (list(b_pair) if b_pair else [b])
    args += ([residual] if nres else []) + (list(exchange[1]) if nex else [])
    out_shape = [jax.ShapeDtypeStruct((m, n), out_dtype)]
    scratch = [pltpu.VMEM((tm, tn), F32)]
    if nex:
        out_shape += _exchange_shapes(*exchange)
        scratch += _exchange_sems(exchange[0], nex)
    outs = pl.pallas_call(
        body, name=name, grid=grid,
        in_specs=in_specs, out_specs=[o_spec] + [hbm] * nex, out_shape=out_shape,
        scratch_shapes=scratch,
        compiler_params=_params(("arbitrary",) * 3 if nex else ("parallel", "parallel", "arbitrary")),
    )(*args)
    return (outs[0], list(outs[1:])) if nex else outs[0]


def rms_fwd(x, gains, *, name):
    s, d = x.shape
    tr = _pick(s, (512, 256, 128, 8))
    ng = len(gains)

    def body(*refs):
        x_ref = refs[0]
        g_refs = refs[1:1 + ng]
        o_refs = refs[1 + ng:]
        xv = x_ref[...]
        y = xv * lax.rsqrt(jnp.mean(xv * xv, axis=-1, keepdims=True) + EPS)
        for g_ref, o_ref in zip(g_refs, o_refs):
            o_ref[...] = (y * g_ref[...]).astype(BF16)

    row = pl.BlockSpec((tr, d), lambda i: (i, 0))
    vec = pl.BlockSpec((1, d), lambda i: (0, 0))
    return pl.pallas_call(
        body, name=name, grid=(s // tr,),
        in_specs=[row] + [vec] * ng, out_specs=[row] * ng,
        out_shape=[jax.ShapeDtypeStruct((s, d), BF16)] * ng,
        compiler_params=_params(("parallel",)),
    )(x, *gains)


def rms_bwd(x, dres, norms, *, name):
    s, d = x.shape
    tr = _pick(s, (256, 128, 8))
    ng = len(norms)

    def body(*refs):
        x_ref, dres_ref = refs[0], refs[1]
        g_refs = refs[2:2 + ng]
        dh_refs = refs[2 + ng:2 + 2 * ng]
        dx_ref, dxb_ref = refs[2 + 2 * ng], refs[3 + 2 * ng]
        dg_refs = refs[4 + 2 * ng:]
        i = pl.program_id(0)
        xv = x_ref[...]
        r = lax.rsqrt(jnp.mean(xv * xv, axis=-1, keepdims=True) + EPS)
        xhat = xv * r
        dx = dres_ref[...]
        for g_ref, dh_ref, dg_ref in zip(g_refs, dh_refs, dg_refs):
            dh = dh_ref[...]
            part = jnp.sum(dh * xhat, axis=0, keepdims=True)

            @pl.when(i == 0)
            def _():
                dg_ref[...] = part

            @pl.when(i > 0)
            def _():
                dg_ref[...] += part

            dxhat = dh * g_ref[...]
            dx = dx + r * (dxhat - xhat * jnp.mean(dxhat * xhat, axis=-1, keepdims=True))
        dx_ref[...] = dx
        dxb_ref[...] = dx.astype(BF16)

    row = pl.BlockSpec((tr, d), lambda i: (i, 0))
    vec = pl.BlockSpec((1, d), lambda i: (0, 0))
    outs = pl.pallas_call(
        body, name=name, grid=(s // tr,),
        in_specs=[row, row] + [vec] * ng + [row] * ng,
        out_specs=[row, row] + [vec] * ng,
        out_shape=[jax.ShapeDtypeStruct((s, d), F32), jax.ShapeDtypeStruct((s, d), BF16)]
        + [jax.ShapeDtypeStruct((1, d), F32)] * ng,
        compiler_params=_params(("arbitrary",)),
    )(x, dres, *[g for g, _ in norms], *[dh for _, dh in norms])
    return outs[0], outs[1], list(outs[2:])


def loss_bwd(x2, target, gain, *, name):
    s, d = x2.shape
    tr = _pick(s, (256, 128, 8))
    nsteps = s // tr

    def body(x_ref, t_ref, g_ref, loss_ref, dg_ref, dx_ref, dxb_ref, sq_acc):
        i = pl.program_id(0)
        xv = x_ref[...]
        r = lax.rsqrt(jnp.mean(xv * xv, axis=-1, keepdims=True) + EPS)
        xhat = xv * r
        g = g_ref[...]
        err = xhat * g - t_ref[...]
        dy = err * (1.0 / d)
        sq = jnp.sum(err * err, axis=0, keepdims=True)
        dgp = jnp.sum(dy * xhat, axis=0, keepdims=True)

        @pl.when(i == 0)
        def _():
            sq_acc[...] = sq
            dg_ref[...] = dgp

        @pl.when(i > 0)
        def _():
            sq_acc[...] += sq
            dg_ref[...] += dgp

        dxhat = dy * g
        dx = r * (dxhat - xhat * jnp.mean(dxhat * xhat, axis=-1, keepdims=True))
        dx_ref[...] = dx
        dxb_ref[...] = dx.astype(BF16)

        @pl.when(i == nsteps - 1)
        def _():
            tot = jnp.sum(sq_acc[...], axis=-1, keepdims=True) * (0.5 / d)
            loss_ref[...] = jnp.broadcast_to(tot, (1, LANES))

    row = pl.BlockSpec((tr, d), lambda i: (i, 0))
    vec = pl.BlockSpec((1, d), lambda i: (0, 0))
    return pl.pallas_call(
        body, name=name, grid=(nsteps,),
        in_specs=[row, row, vec],
        out_specs=[pl.BlockSpec((1, LANES), lambda i: (0, 0)), vec, row, row],
        out_shape=[jax.ShapeDtypeStruct((1, LANES), F32), jax.ShapeDtypeStruct((1, d), F32),
                   jax.ShapeDtypeStruct((s, d), F32), jax.ShapeDtypeStruct((s, d), BF16)],
        scratch_shapes=[pltpu.VMEM((1, d), F32)],
        compiler_params=_params(("arbitrary",)),
    )(x2, target, gain)


def _lru_gates(xb, wr, wi, br, bi, sp):
    xbb = xb.astype(BF16)
    r = _sigmoid(jnp.dot(xbb, wr, preferred_element_type=F32) + br)
    ig = _sigmoid(jnp.dot(xbb, wi, preferred_element_type=F32) + bi)
    log_a = (-LRU_C) * r * sp
    a = jnp.exp(log_a)
    mult = jnp.sqrt(jnp.maximum(-jnp.tanh(log_a) * (a * a + 1.0), 0.0))
    return r, ig, a, mult


def _softplus_neg(lam):
    e = jnp.exp(-jnp.abs(lam))
    sp = jnp.maximum(-lam, 0.0) + jnp.log(1.0 + e)
    sg = jnp.where(lam >= 0, e, 1.0) / (1.0 + e)
    return sp, sg


def _conv(pad_ref, w, b, t):
    acc = b + w[CONV_W - 1:CONV_W, :] * pad_ref[pl.ds(SUBLANES, t), :]
    for dlt in range(1, CONV_W):
        acc = acc + w[CONV_W - 1 - dlt:CONV_W - dlt, :] * pad_ref[pl.ds(SUBLANES - dlt, t), :]
    return acc


def _lru_specs(t, bw, nb, time_of):
    blk = lambda c0: pl.BlockSpec((t, bw), lambda n, i, c0=c0: (time_of(i), c0 + n))
    vec = pl.BlockSpec((1, bw), lambda n, i: (0, n))
    wspec = pl.BlockSpec((None, bw, bw), lambda n, i: (n, 0, 0))
    cwspec = pl.BlockSpec((CONV_W, bw), lambda n, i: (0, n))
    return blk, vec, wspec, cwspec


def lru_fwd(proj, conv_w, conv_b, w_r, b_r, w_i, b_i, lam, *, name):
    s, r2 = proj.shape
    rr = r2 // 2
    nb, bw, _ = w_r.shape
    t = _pick(s, (1024, 512, 256, 128, 64, 32))
    ngroups = t // SUBLANES

    def body(xp_ref, gate_ref, cw_ref, cb_ref, wr_ref, br_ref, wi_ref, bi_ref, lam_ref,
             m_ref, h_ref, pad, hcarry, a_scr, u_scr):
        i = pl.program_id(1)

        @pl.when(i == 0)
        def _():
            pad[0:SUBLANES, :] = jnp.zeros((SUBLANES, bw), F32)
            hcarry[...] = jnp.zeros_like(hcarry)

        xpre = xp_ref[...]
        pad[pl.ds(SUBLANES, t), :] = xpre
        xb = _conv(pad, cw_ref[...], cb_ref[...], t)
        pad[0:SUBLANES, :] = xpre[t - SUBLANES:, :]
        sp, _ = _softplus_neg(lam_ref[...])
        _, ig, a, mult = _lru_gates(xb, wr_ref[...], wi_ref[...], br_ref[...], bi_ref[...], sp)
        a_scr[...] = a
        u_scr[...] = mult * (ig * xb)
        row = lax.broadcasted_iota(jnp.int32, (SUBLANES, bw), 0)

        def groups(gi, hprev):
            offs = [pl.multiple_of((gi * SCAN_UNROLL + u) * SUBLANES, SUBLANES) for u in range(SCAN_UNROLL)]
            scanned = []
            for off in offs:
                av = a_scr[pl.ds(off, SUBLANES), :]
                uv = u_scr[pl.ds(off, SUBLANES), :]
                for dlt in (1, 2, 4):
                    keep = row >= dlt
                    uv = jnp.where(keep, av * pltpu.roll(uv, dlt, 0) + uv, uv)
                    av = jnp.where(keep, av * pltpu.roll(av, dlt, 0), av)
                scanned.append((av, uv))
            for off, (av, uv) in zip(offs, scanned):
                hv = av * hprev + uv
                h_ref[pl.ds(off, SUBLANES), :] = hv
                hprev = hv[SUBLANES - 1:SUBLANES, :]
            return hprev

        hcarry[...] = lax.fori_loop(0, ngroups // SCAN_UNROLL, groups, hcarry[...])
        gate = gate_ref[...]
        m_ref[...] = (h_ref[...] * (gate * _sigmoid(gate))).astype(BF16)

    blk, vec, wspec, cwspec = _lru_specs(t, bw, nb, lambda i: i)
    return pl.pallas_call(
        body, name=name, grid=(nb, s // t),
        in_specs=[blk(0), blk(nb), cwspec, vec, wspec, vec, wspec, vec, vec],
        out_specs=[blk(0), blk(0)],
        out_shape=[jax.ShapeDtypeStruct((s, rr), BF16), jax.ShapeDtypeStruct((s, rr), F32)],
        scratch_shapes=[pltpu.VMEM((t + SUBLANES, bw), F32), pltpu.VMEM((1, bw), F32),
                        pltpu.VMEM((t, bw), F32), pltpu.VMEM((t, bw), F32)],
        compiler_params=_params(("parallel", "arbitrary")),
    )(proj, proj, conv_w, conv_b, w_r, b_r, w_i, b_i, lam)


def lru_bwd(proj, hst, dm, conv_w, conv_b, w_r, b_r, w_i, b_i, lam, *, name):
    s, r2 = proj.shape
    rr = r2 // 2
    nb, bw, _ = w_r.shape
    t = _pick(s, (1024, 512, 256, 128, 64, 32))
    nt = s // t
    ngroups = t // SUBLANES
    nt_dims = (((1,), (1,)), ((), ()))
    tn_dims = (((0,), (0,)), ((), ()))

    def body(xp_ref, xhalo_ref, gate_ref, h_ref, hhalo_ref, dm_ref, cw_ref, cb_ref, wr_ref, br_ref, wi_ref,
             bi_ref, lam_ref,
             dxp_ref, dgate_ref, dcw_ref, dcb_ref, dwr_ref, dbr_ref, dwi_ref, dbi_ref, dlam_ref,
             pad, hpad, dpad, ecarry, a_scr, b_scr, d_scr):
        step = pl.program_id(1)

        @pl.when(step == 0)
        def _():
            dpad[pl.ds(t, SUBLANES), :] = jnp.zeros((SUBLANES, bw), F32)
            ecarry[...] = jnp.zeros_like(ecarry)
            dcw_ref[...] = jnp.zeros_like(dcw_ref)
            dcb_ref[...] = jnp.zeros_like(dcb_ref)
            dwr_ref[...] = jnp.zeros_like(dwr_ref)
            dbr_ref[...] = jnp.zeros_like(dbr_ref)
            dwi_ref[...] = jnp.zeros_like(dwi_ref)
            dbi_ref[...] = jnp.zeros_like(dbi_ref)
            dlam_ref[...] = jnp.zeros_like(dlam_ref)

        past = jnp.where(step == nt - 1, 0.0, 1.0)
        pad[0:SUBLANES, :] = xhalo_ref[...] * past
        pad[pl.ds(SUBLANES, t), :] = xp_ref[...]
        hpad[0:SUBLANES, :] = hhalo_ref[...] * past
        hpad[pl.ds(SUBLANES, t), :] = h_ref[...]
        cw = cw_ref[...]
        xb = _conv(pad, cw, cb_ref[...], t)
        sp, sg = _softplus_neg(lam_ref[...])
        wr = wr_ref[...]
        wi = wi_ref[...]
        r, ig, a, mult = _lru_gates(xb, wr, wi, br_ref[...], bi_ref[...], sp)
        gate = gate_ref[...]
        sgate = _sigmoid(gate)
        dmv = dm_ref[...]
        dgate_ref[...] = (dmv * h_ref[...] * (sgate * (1.0 + gate * (1.0 - sgate)))).astype(BF16)
        dy = dmv * (gate * sgate)
        a_scr[...] = a
        b_scr[...] = a * dy
        row = lax.broadcasted_iota(jnp.int32, (SUBLANES, bw), 0)

        def groups(gi, enext):
            offs = [pl.multiple_of((ngroups - 1 - gi * SCAN_UNROLL - u) * SUBLANES, SUBLANES)
                    for u in range(SCAN_UNROLL)]
            scanned = []
            for off in offs:
                av = a_scr[pl.ds(off, SUBLANES), :]
                bv = b_scr[pl.ds(off, SUBLANES), :]
                for dlt in (1, 2, 4):
                    keep = row < SUBLANES - dlt
                    bv = jnp.where(keep, av * pltpu.roll(bv, SUBLANES - dlt, 0) + bv, bv)
                    av = jnp.where(keep, av * pltpu.roll(av, SUBLANES - dlt, 0), av)
                scanned.append((av, bv))
            for off, (av, bv) in zip(offs, scanned):
                ev = av * enext + bv
                d_scr[pl.ds(off, SUBLANES), :] = jnp.where(row == SUBLANES - 1, enext,
                                                           pltpu.roll(ev, SUBLANES - 1, 0))
                enext = ev[0:1, :]
            return enext

        ecarry[...] = lax.fori_loop(0, ngroups // SCAN_UNROLL, groups, ecarry[...])
        dtot = dy + d_scr[...]
        da = dtot * hpad[pl.ds(SUBLANES - 1, t), :]
        dmult = dtot * (ig * xb)
        dlog_a = da * a - dmult * (a * a) / mult
        dr_pre = dlog_a * ((-LRU_C) * sp) * (r * (1.0 - r))
        di_pre = (dtot * mult * xb) * (ig * (1.0 - ig))
        dlam_ref[...] += jnp.sum(dlog_a * r, axis=0, keepdims=True) * (LRU_C * sg)
        dbr_ref[...] += jnp.sum(dr_pre, axis=0, keepdims=True)
        dbi_ref[...] += jnp.sum(di_pre, axis=0, keepdims=True)
        drb = dr_pre.astype(BF16)
        dib = di_pre.astype(BF16)
        xbb = xb.astype(BF16)
        dxb = (dtot * mult * ig
               + lax.dot_general(drb, wr, nt_dims, preferred_element_type=F32)
               + lax.dot_general(dib, wi, nt_dims, preferred_element_type=F32))
        dwr_ref[...] += lax.dot_general(xbb, drb, tn_dims, preferred_element_type=F32)
        dwi_ref[...] += lax.dot_general(xbb, dib, tn_dims, preferred_element_type=F32)
        dcb_ref[...] += jnp.sum(dxb, axis=0, keepdims=True)
        dpad[pl.ds(0, t), :] = dxb
        dxpre = cw[CONV_W - 1:CONV_W, :] * dxb
        dcw_ref[CONV_W - 1:CONV_W, :] += jnp.sum(dxb * pad[pl.ds(SUBLANES, t), :], axis=0, keepdims=True)
        for dlt in range(1, CONV_W):
            dxpre = dxpre + cw[CONV_W - 1 - dlt:CONV_W - dlt, :] * dpad[pl.ds(dlt, t), :]
            dcw_ref[CONV_W - 1 - dlt:CONV_W - dlt, :] += jnp.sum(
                dxb * pad[pl.ds(SUBLANES - dlt, t), :], axis=0, keepdims=True)
        dpad[pl.ds(t, SUBLANES), :] = dxb[0:SUBLANES, :]
        dxp_ref[...] = dxpre.astype(BF16)

    rev = lambda i: nt - 1 - i
    blk, vec, wspec, cwspec = _lru_specs(t, bw, nb, rev)
    halo = pl.BlockSpec((SUBLANES, bw), lambda n, i: (jnp.maximum(rev(i) * ngroups - 1, 0), n))
    return pl.pallas_call(
        body, name=name, grid=(nb, nt),
        in_specs=[blk(0), halo, blk(nb), blk(0), halo, blk(0), cwspec, vec, wspec, vec, wspec, vec, vec],
        out_specs=[blk(0), blk(0), cwspec, vec, wspec, vec, wspec, vec, vec],
        out_shape=[jax.ShapeDtypeStruct((s, rr), BF16), jax.ShapeDtypeStruct((s, rr), BF16),
                   jax.ShapeDtypeStruct((CONV_W, rr), F32), jax.ShapeDtypeStruct((1, rr), F32),
                   jax.ShapeDtypeStruct((nb, bw, bw), F32), jax.ShapeDtypeStruct((1, rr), F32),
                   jax.ShapeDtypeStruct((nb, bw, bw), F32), jax.ShapeDtypeStruct((1, rr), F32),
                   jax.ShapeDtypeStruct((1, rr), F32)],
        scratch_shapes=[pltpu.VMEM((t + SUBLANES, bw), F32), pltpu.VMEM((t + SUBLANES, bw), F32),
                        pltpu.VMEM((t + SUBLANES, bw), F32), pltpu.VMEM((1, bw), F32),
                        pltpu.VMEM((t, bw), F32), pltpu.VMEM((t, bw), F32), pltpu.VMEM((t, bw), F32)],
        compiler_params=_params(("parallel", "arbitrary")),
    )(proj, proj, proj, hst, hst, dm, conv_w, conv_b, w_r, b_r, w_i, b_i, lam)


def _softplus(z):
    return jnp.maximum(z, 0.0) + jnp.log(1.0 + jnp.exp2(jnp.abs(z) * (-LOG2E)))


def _att_blocks(s):
    bk = ATT_BLOCK if s % ATT_BLOCK == 0 else s
    bq = ATT_QTILES * bk if s % (ATT_QTILES * bk) == 0 else bk
    return bk, bq


def _tile_base(i, r):
    return r * ((i * (i + 1)) // 2)


def attn_fwd(projb, kv, *, name):
    s, a2 = projb.shape
    a = a2 // 2
    nh = a // HEAD_DIM
    bk, bq = _att_blocks(s)
    r = bq // bk
    nq = s // bq
    ntiles = _tile_base(nq, r)
    scale = 1.0 / math.sqrt(HEAD_DIM)
    nt_dims = (((1,), (1,)), ((), ()))
    hp = ATT_HEADS if nh % ATT_HEADS == 0 else 1
    wd = hp * HEAD_DIM

    def body(q_ref, g_ref, k_ref, v_ref, m_ref, o_ref, saved_hbm, acc, stage, sems):
        hgrp, i = pl.program_id(0), pl.program_id(1)
        base = _tile_base(i, r)
        qb = (q_ref[...] * scale).astype(BF16)
        from_mat = (lax.broadcasted_iota(jnp.int32, (bk, bk), 0)
                    >= lax.broadcasted_iota(jnp.int32, (bk, bk), 1)).astype(BF16)
        rowi = lax.broadcasted_iota(jnp.int32, (bq, bk), 0)
        coli = lax.broadcasted_iota(jnp.int32, (bq, bk), 1)
        cols = [slice(hh * HEAD_DIM, (hh + 1) * HEAD_DIM) for hh in range(hp)]

        def save(slot, j):
            return pltpu.make_async_copy(stage.at[slot], saved_hbm.at[hgrp, base + j], sems.at[slot])

        def tile(j, n, carries, diag):
            r0 = 0 if diag is None else diag * bk
            live = slice(r0, bq)
            causal = None if diag is None else coli[live] < rowi[:bq - r0]
            slot = n % 2

            def free_slot():
                save(slot, 0).wait()

            if isinstance(n, int):
                if n >= 2:
                    free_slot()
            elif r >= 2:
                free_slot()
            else:
                pl.when(n >= 2)(free_slot)
            rows = pl.ds(pl.multiple_of(j * bk, bk), bk)
            zs = [lax.dot_general(qb[live, c], k_ref[rows, c], nt_dims, preferred_element_type=F32) for c in cols]
            sums, sigs = [], []
            for z in zs:
                sp = _softplus(z)
                sig = z - sp
                if causal is not None:
                    sp = jnp.where(causal, sp, 0.0)
                    sig = jnp.where(causal, sig, LOG_ZERO)
                sums.append(jnp.dot(sp.astype(BF16), from_mat, preferred_element_type=F32))
                sigs.append(sig.astype(BF16))
            out = []
            for hh in range(hp):
                w = jnp.exp(zs[hh] - sums[hh] - carries[hh][live])
                if causal is not None:
                    w = jnp.where(causal, w, 0.0)
                wb = w.astype(BF16)
                acc[live, cols[hh]] += jnp.dot(wb, v_ref[rows, cols[hh]], preferred_element_type=F32)
                stage[slot, 0, hh, live] = wb
                stage[slot, 1, hh, live] = sigs[hh]
                if r0:
                    stage[slot, :, hh, :r0] = jnp.zeros((2, r0, bk), BF16)
                grown = carries[hh][live] + sums[hh][:, 0:1]
                out.append(jnp.concatenate([carries[hh][:r0], grown], axis=0) if r0 else grown)
            save(slot, j).start()
            return tuple(out)

        acc[...] = jnp.zeros_like(acc)
        carries = tuple(jnp.zeros((bq, 1), F32) for _ in range(hp))
        for n, dg in enumerate(reversed(range(r))):
            carries = tile(r * i + dg, n, carries, dg)
        lax.fori_loop(0, r * i, lambda jj, c: tile(r * i - 1 - jj, r + jj, c, None), carries)
        ntile = r * (i + 1)
        for back in (1, 2):
            def drain(back=back):
                save((ntile - back) % 2, 0).wait()
            if r >= back:
                drain()
            else:
                pl.when(ntile >= back)(drain)
        o = acc[...]
        o_ref[...] = o
        gate = g_ref[...]
        m_ref[...] = (o * (gate * _sigmoid(gate))).astype(BF16)

    ng = nh // hp
    qspec = lambda c0: pl.BlockSpec((bq, wd), lambda h, i, c0=c0: (i, c0 + h))
    kspec = lambda c0: pl.BlockSpec((s, wd), lambda h, i, c0=c0: (0, c0 + h), pipeline_mode=pl.Buffered(1))
    hbm = pl.BlockSpec(memory_space=pl.ANY)
    saved = jax.ShapeDtypeStruct((ng, ntiles, 2, hp, bq, bk), BF16)
    return pl.pallas_call(
        body, name=name, grid=(ng, nq),
        in_specs=[qspec(0), qspec(ng), kspec(0), kspec(ng)],
        out_specs=[qspec(0), qspec(0), hbm],
        out_shape=[jax.ShapeDtypeStruct((s, a), BF16), jax.ShapeDtypeStruct((s, a), F32), saved],
        scratch_shapes=[pltpu.VMEM((bq, wd), F32), pltpu.VMEM((2, 2, hp, bq, bk), BF16),
                        pltpu.SemaphoreType.DMA((2,))],
        compiler_params=_params(("arbitrary", "arbitrary")),
    )(projb, projb, kv, kv)


def attn_bwd(projb, dm, o, kv, saved, *, name):
    s, a2 = projb.shape
    a = a2 // 2
    nh = a // HEAD_DIM
    bk, bq = _att_blocks(s)
    r = bq // bk
    nq = s // bq
    scale = 1.0 / math.sqrt(HEAD_DIM)
    nt_dims = (((1,), (1,)), ((), ()))
    tn_dims = (((0,), (0,)), ((), ()))
    hp = saved.shape[3]
    wd = hp * HEAD_DIM
    ahead = ATT_FETCH_AHEAD

    def body(q_ref, g_ref, dm_ref, o_ref, k_ref, v_ref, saved_hbm, dq_ref, dg_ref, dk_ref, dv_ref,
             dk_acc, dv_acc, dq_acc, stage, sems):
        hgrp, i = pl.program_id(0), pl.program_id(1)
        base = _tile_base(i, r)
        ntile = r * (i + 1)

        @pl.when(i == 0)
        def _():
            dk_acc[...] = jnp.zeros_like(dk_acc)
            dv_acc[...] = jnp.zeros_like(dv_acc)

        def fetch(j):
            slot = j % (ahead + 1)
            return pltpu.make_async_copy(saved_hbm.at[hgrp, base + j], stage.at[slot], sems.at[slot])

        for j0 in range(ahead):
            pl.when(j0 < ntile)(lambda j0=j0: fetch(j0).start())
        qb = (q_ref[...] * scale).astype(BF16)
        gate = g_ref[...]
        sgate = _sigmoid(gate)
        dmv = dm_ref[...]
        dob = (dmv * (gate * sgate)).astype(BF16)
        dg_ref[...] = (dmv * o_ref[...] * (sgate * (1.0 + gate * (1.0 - sgate)))).astype(BF16)
        upto_mat = (lax.broadcasted_iota(jnp.int32, (bk, bk), 0)
                    <= lax.broadcasted_iota(jnp.int32, (bk, bk), 1)).astype(BF16)
        cols = [slice(hh * HEAD_DIM, (hh + 1) * HEAD_DIM) for hh in range(hp)]
        dq_acc[...] = jnp.zeros_like(dq_acc)

        def tile(j, gcarries, r0=0, more=None):
            live = slice(r0, bq)
            slot = j % (ahead + 1)
            if more is None:
                pl.when(j + ahead < ntile)(lambda: fetch(j + ahead).start())
            elif more:
                fetch(j + ahead).start()
            fetch(j).wait()
            rows = pl.ds(pl.multiple_of(j * bk, bk), bk)
            dws = [lax.dot_general(dob[live, c], v_ref[rows, c], nt_dims, preferred_element_type=F32) for c in cols]
            gs, totals = [], []
            for hh in range(hp):
                wb = stage[slot, 0, hh, live]
                g = wb.astype(F32) * dws[hh]
                dv_acc[rows, cols[hh]] += lax.dot_general(wb, dob[live, cols[hh]], tn_dims,
                                                          preferred_element_type=F32)
                totals.append(jnp.dot(g.astype(BF16), upto_mat, preferred_element_type=F32))
                gs.append(g)
            out = []
            for hh in range(hp):
                dz = gs[hh] - (totals[hh] + gcarries[hh][live]) * jnp.exp(stage[slot, 1, hh, live]).astype(F32)
                dzb = dz.astype(BF16)
                dq_acc[live, cols[hh]] += jnp.dot(dzb, k_ref[rows, cols[hh]], preferred_element_type=F32)
                dk_acc[rows, cols[hh]] += lax.dot_general(dzb, qb[live, cols[hh]], tn_dims,
                                                          preferred_element_type=F32)
                grown = gcarries[hh][live] + totals[hh][:, bk - 1:bk]
                out.append(jnp.concatenate([gcarries[hh][:r0], grown], axis=0) if r0 else grown)
            return tuple(out)

        gcarries = lax.fori_loop(0, r * i, tile, tuple(jnp.zeros((bq, 1), F32) for _ in range(hp)))
        for dg in range(r):
            gcarries = tile(r * i + dg, gcarries, dg * bk, dg + ahead < r)
        dq_ref[...] = (dq_acc[...] * scale).astype(BF16)

        @pl.when(i == nq - 1)
        def _():
            dk_ref[...] = dk_acc[...].astype(BF16)
            dv_ref[...] = dv_acc[...].astype(BF16)

    ng = nh // hp
    once = pl.Buffered(1)
    qspec = lambda c0: pl.BlockSpec((bq, wd), lambda h, i, c0=c0: (i, c0 + h))
    kspec = lambda c0: pl.BlockSpec((s, wd), lambda h, i, c0=c0: (0, c0 + h), pipeline_mode=once)
    hbm = pl.BlockSpec(memory_space=pl.ANY)
    return pl.pallas_call(
        body, name=name, grid=(ng, nq),
        in_specs=[qspec(0), qspec(ng), qspec(0), qspec(0), kspec(0), kspec(ng), hbm],
        out_specs=[qspec(0), qspec(0), kspec(0), kspec(0)],
        out_shape=[jax.ShapeDtypeStruct((s, a), BF16)] * 4,
        scratch_shapes=[pltpu.VMEM((s, wd), F32), pltpu.VMEM((s, wd), F32), pltpu.VMEM((bq, wd), F32),
                        pltpu.VMEM((ahead + 1, 2, hp, bq, bk), BF16), pltpu.SemaphoreType.DMA((ahead + 1,))],
        compiler_params=_params(("arbitrary", "arbitrary"), vmem=V7X_VMEM_LIMIT_HIGH),
    )(projb, projb, dm, o, kv, kv, saved)


def _as2d(x):
    n = x.size
    cols = x.shape[-1]
    if cols % LANES != 0:
        cols = LANES
    return x.reshape(n // cols, cols)


def sum_parts(parts, *, name):
    p, rows, cols = parts.shape
    tr = _pick(rows, (512, 256, 128, 64, 32, 16))

    def body(p_ref, o_ref):
        acc = p_ref[0].astype(F32)
        for k in range(1, p):
            acc = acc + p_ref[k].astype(F32)
        o_ref[...] = acc

    return pl.pallas_call(
        body, name=name, grid=(rows // tr,),
        in_specs=[pl.BlockSpec((p, tr, cols), lambda i: (0, i, 0))],
        out_specs=pl.BlockSpec((tr, cols), lambda i: (i, 0)),
        out_shape=jax.ShapeDtypeStruct((rows, cols), F32),
        compiler_params=_params(("parallel",)),
    )(parts)


def adamw(w, g_parts, m, v, *, name):
    rows, cols = w.shape
    tr = _pick(rows, (128, 64, 32, 16, 8))
    np_ = len(g_parts)
    c1 = 1.0 / (1.0 - ADAM_B1 ** ADAM_STEP)
    c2 = 1.0 / (1.0 - ADAM_B2 ** ADAM_STEP)

    def body(*refs):
        w_ref, m_ref, v_ref = refs[0], refs[1], refs[2]
        g_refs = refs[3:3 + np_]
        go_ref, d_ref, mo_ref, vo_ref = refs[3 + np_:]
        g = g_refs[0][...]
        for gr in g_refs[1:]:
            g = g + gr[...]
        mn = ADAM_B1 * m_ref[...] + (1.0 - ADAM_B1) * g
        vn = ADAM_B2 * v_ref[...] + (1.0 - ADAM_B2) * (g * g)
        go_ref[...] = g
        mo_ref[...] = mn
        vo_ref[...] = vn
        d_ref[...] = (-ADAM_LR) * ((mn * c1) / (jnp.sqrt(vn * c2) + ADAM_EPS) + ADAM_WD * w_ref[...])

    spec = pl.BlockSpec((tr, cols), lambda i: (i, 0))
    return pl.pallas_call(
        body, name=name, grid=(rows // tr,),
        in_specs=[spec] * (3 + np_), out_specs=[spec] * 4,
        out_shape=[jax.ShapeDtypeStruct((rows, cols), F32)] * 4,
        compiler_params=_params(("parallel",)),
    )(w, m, v, *g_parts)


def exchange(kind, arrays, axes, *, name):
    na = len(arrays)
    hbm = pl.BlockSpec(memory_space=pl.ANY)

    def body(*refs):
        triples = _exchange(kind, refs[:na], refs[na:2 * na], axes, *refs[2 * na:])
        for step in range(3):
            for triple in triples:
                triple[step]()

    return pl.pallas_call(
        body, name=name, in_specs=[hbm] * na, out_specs=[hbm] * na,
        out_shape=_exchange_shapes(kind, arrays, axes), scratch_shapes=_exchange_sems(kind, na),
    )(*arrays)


def swap_cores(arrs, *, name):
    na = len(arrs)
    hbm = pl.BlockSpec(memory_space=pl.ANY)

    def body(*refs):
        a_refs = refs[:na]
        o_refs = refs[na:2 * na]
        send_sems, recv_sems = refs[2 * na:]
        x, y, c = _place()
        copies = []
        for ai in range(na):
            cp = pltpu.make_async_remote_copy(
                src_ref=a_refs[ai], dst_ref=o_refs[ai], send_sem=send_sems.at[ai], recv_sem=recv_sems.at[ai],
                device_id=(x, y, 1 - c), device_id_type=MESH)
            cp.start()
            copies.append(cp)
        for cp in copies:
            cp.wait()

    return pl.pallas_call(
        body, name=name,
        in_specs=[hbm] * na, out_specs=[hbm] * na,
        out_shape=[jax.ShapeDtypeStruct(a.shape, a.dtype) for a in arrs],
        scratch_shapes=[pltpu.SemaphoreType.DMA((na,)), pltpu.SemaphoreType.DMA((na,))],
    )(*arrs)


def allreduce_small(arrs, *, name):
    nar = len(arrs)
    width = max(a.shape[1] for a in arrs)
    starts, total = [], 0
    for a in arrs:
        starts.append(total)
        total += a.shape[0]
    total += (-total) % SUBLANES

    def body(*refs):
        in_refs, out_refs = refs[:nar], refs[nar:2 * nar]
        buf, slots, send_sems, recv_sems = refs[2 * nar:]
        x, y, c = _place()
        me = 4 * x + 2 * y + c
        buf[...] = jnp.zeros_like(buf)
        for ref, st in zip(in_refs, starts):
            buf[st:st + ref.shape[0], 0:ref.shape[1]] = ref[...]
        slots[0] = buf[...]
        copies = []
        for rel in range(1, 8):
            peer = (x ^ (rel >> 2), y ^ ((rel >> 1) & 1), c ^ (rel & 1))
            cp = pltpu.make_async_remote_copy(
                src_ref=buf, dst_ref=slots.at[rel], send_sem=send_sems.at[rel - 1],
                recv_sem=recv_sems.at[rel - 1], device_id=peer, device_id_type=MESH)
            cp.start()
            copies.append(cp)
        for cp in copies:
            cp.wait()
        acc = slots[me]
        for dev in range(1, 8):
            acc = acc + slots[dev ^ me]
        buf[...] = acc
        for ref, st in zip(out_refs, starts):
            ref[...] = buf[st:st + ref.shape[0], 0:ref.shape[1]]

    vm = pl.BlockSpec(memory_space=pltpu.VMEM)
    return pl.pallas_call(
        body, name=name, in_specs=[vm] * nar, out_specs=[vm] * nar,
        out_shape=[jax.ShapeDtypeStruct(a.shape, F32) for a in arrs],
        scratch_shapes=[pltpu.VMEM((total, width), F32), pltpu.VMEM((8, total, width), F32),
                        pltpu.SemaphoreType.DMA((7,)), pltpu.SemaphoreType.DMA((7,))],
    )(*arrs)


def adamw_small(ws, gs, ms, vs, *, name):
    n = len(ws)
    c1 = 1.0 / (1.0 - ADAM_B1 ** ADAM_STEP)
    c2 = 1.0 / (1.0 - ADAM_B2 ** ADAM_STEP)

    def body(*refs):
        w_refs, g_refs, m_refs, v_refs = (refs[k * n:(k + 1) * n] for k in range(4))
        d_refs, mo_refs, vo_refs = (refs[(4 + k) * n:(5 + k) * n] for k in range(3))
        for w_ref, g_ref, m_ref, v_ref, d_ref, mo_ref, vo_ref in zip(w_refs, g_refs, m_refs, v_refs,
                                                                     d_refs, mo_refs, vo_refs):
            g = g_ref[...]
            mn = ADAM_B1 * m_ref[...] + (1.0 - ADAM_B1) * g
            vn = ADAM_B2 * v_ref[...] + (1.0 - ADAM_B2) * (g * g)
            mo_ref[...] = mn
            vo_ref[...] = vn
            d_ref[...] = (-ADAM_LR) * ((mn * c1) / (jnp.sqrt(vn * c2) + ADAM_EPS) + ADAM_WD * w_ref[...])

    vm = pl.BlockSpec(memory_space=pltpu.VMEM)
    outs = pl.pallas_call(
        body, name=name, in_specs=[vm] * (4 * n), out_specs=[vm] * (3 * n),
        out_shape=[jax.ShapeDtypeStruct(w.shape, F32) for w in ws] * 3,
    )(*ws, *gs, *ms, *vs)
    return outs[:n], outs[n:2 * n], outs[2 * n:]


def kernel(x, a_norm, a_w_in, a_conv_w, a_conv_b, a_w_r, a_b_r, a_w_i, a_b_i, a_lambda, a_w_out, kv_norm, w_kv, b_norm, b_w_in, b_w_out, final_norm, loss_target, m_a_norm, m_a_w_in, m_a_conv_w, m_a_conv_b, m_a_w_r, m_a_b_r, m_a_w_i, m_a_b_i, m_a_lambda, m_a_w_out, m_kv_norm, m_w_kv, m_b_norm, m_b_w_in, m_b_w_out, m_final_norm, v_a_norm, v_a_w_in, v_a_conv_w, v_a_conv_b, v_a_w_r, v_a_b_r, v_a_w_i, v_a_b_i, v_a_lambda, v_a_w_out, v_kv_norm, v_w_kv, v_b_norm, v_b_w_in, v_b_w_out, v_final_norm):
    weights = dict(a_norm=a_norm, a_w_in=a_w_in, a_conv_w=a_conv_w, a_conv_b=a_conv_b, a_w_r=a_w_r, a_b_r=a_b_r,
                   a_w_i=a_w_i, a_b_i=a_b_i, a_lambda=a_lambda, a_w_out=a_w_out, kv_norm=kv_norm, w_kv=w_kv,
                   b_norm=b_norm, b_w_in=b_w_in, b_w_out=b_w_out, final_norm=final_norm)
    mom1 = dict(a_norm=m_a_norm, a_w_in=m_a_w_in, a_conv_w=m_a_conv_w, a_conv_b=m_a_conv_b, a_w_r=m_a_w_r,
                a_b_r=m_a_b_r, a_w_i=m_a_w_i, a_b_i=m_a_b_i, a_lambda=m_a_lambda, a_w_out=m_a_w_out,
                kv_norm=m_kv_norm, w_kv=m_w_kv, b_norm=m_b_norm, b_w_in=m_b_w_in, b_w_out=m_b_w_out,
                final_norm=m_final_norm)
    mom2 = dict(a_norm=v_a_norm, a_w_in=v_a_w_in, a_conv_w=v_a_conv_w, a_conv_b=v_a_conv_b, a_w_r=v_a_w_r,
                a_b_r=v_a_b_r, a_w_i=v_a_w_i, a_b_i=v_a_b_i, a_lambda=v_a_lambda, a_w_out=v_a_w_out,
                kv_norm=v_kv_norm, w_kv=v_w_kv, b_norm=v_b_norm, b_w_in=v_b_w_in, b_w_out=v_b_w_out,
                final_norm=v_final_norm)
    order = list(weights)
    x0 = x[0]
    target = loss_target[0]
    d = x0.shape[1]
    chip = 2 * lax.axis_index("x") + lax.axis_index("y")

    big = ["a_w_in", "a_w_r", "a_w_i", "a_w_out", "w_kv", "b_w_in", "b_w_out"]
    big_axis = dict(a_w_in=1, a_w_r=1, a_w_i=1, a_w_out=0, w_kv=1, b_w_in=1, b_w_out=0)
    local = dict(a_w_in=a_w_in[0], a_w_r=a_w_r[0], a_w_i=a_w_i[0], a_w_out=a_w_out[0], w_kv=w_kv,
                 b_w_in=b_w_in[0], b_w_out=b_w_out[0])
    shards = {n: local[n].astype(BF16) for n in big}
    first = ["a_w_in", "a_w_r", "a_w_i"]
    full = exchange("gather_halves", [shards[n] for n in first], [big_axis[n] for n in first], name="gather_first")
    full += exchange("gather", [a_conv_w[0], b_norm], [1, 1], name="gather_small")
    wf = dict(zip(first + ["a_conv_w", "b_norm"], full))
    wf.update(a_norm=a_norm, a_conv_b=a_conv_b, a_b_r=a_b_r, a_b_i=a_b_i, a_lambda=a_lambda,
              kv_norm=kv_norm.reshape(1, d), final_norm=final_norm.reshape(1, d))
    loss_part, grad_x, parts, gsmall = _local_grads(x0, target, wf, shards=shards, axes=big_axis)

    sums = [sum_parts(parts[n].reshape(4, *_as2d(parts[n][0]).shape), name="sum_" + n) for n in big]
    others = swap_cores(sums, name="swap_cores")

    small = ["a_norm", "a_conv_b", "a_b_r", "a_b_i", "a_lambda", "kv_norm", "final_norm", "a_conv_w", "b_norm"]
    *red, loss_sum = allreduce_small([gsmall[n] for n in small] + [loss_part], name="allreduce_small")
    gs = dict(zip(small, red))
    loss = loss_sum[0, 0]
    n_conv = a_conv_w.shape[2]
    gs["a_conv_w"] = lax.dynamic_slice_in_dim(gs["a_conv_w"], chip * n_conv, n_conv, axis=1)
    n_bn = b_norm.shape[1]
    gs["b_norm"] = lax.dynamic_slice_in_dim(gs["b_norm"], chip * n_bn, n_bn, axis=1)

    grads, deltas, new_m, new_v = {}, {}, {}, {}
    for n, s_mine, s_other in zip(big, sums, others):
        shp = weights[n].shape
        g, dlt, mn, vn = adamw(_as2d(weights[n]), [s_mine, s_other], _as2d(mom1[n]), _as2d(mom2[n]),
                               name="adamw_" + n)
        grads[n], deltas[n], new_m[n], new_v[n] = (t.reshape(shp) for t in (g, dlt, mn, vn))
    as_g = lambda src: [src[n].reshape(gs[n].shape) for n in small]
    outs = adamw_small(as_g(weights), [gs[n] for n in small], as_g(mom1), as_g(mom2), name="adamw_small")
    for dst, vals in zip((deltas, new_m, new_v), outs):
        for n, val in zip(small, vals):
            dst[n] = val.reshape(weights[n].shape)
    for n in small:
        grads[n] = gs[n].reshape(weights[n].shape)

    return (loss, grad_x[None], *[grads[n] for n in order], *[deltas[n] for n in order],
            *[new_m[n] for n in order], *[new_v[n] for n in order])


def _local_grads(x0, target, wf, shards=None, axes=None):
    a_norm, a_conv_b, a_b_r, a_b_i, a_lambda = (wf[n] for n in ("a_norm", "a_conv_b", "a_b_r", "a_b_i", "a_lambda"))
    kv_norm, final_norm = wf["kv_norm"], wf["final_norm"]
    wf = dict(wf)
    parts = {}

    def mm(*args, gather=(), scatter=None, **kw):
        if shards is None or not (gather or scatter):
            return matmul(*args, **kw)
        if gather:
            out, got = matmul(*args, exchange=("gather_halves", [shards[n] for n in gather], [axes[n] for n in gather]),
                              **kw)
            wf.update(zip(gather, got))
        else:
            out, got = matmul(*args, exchange=("scatter", list(scatter.values()), [axes[n] for n in scatter]), **kw)
            parts.update(zip(scatter, got))
        return out

    (h_a,) = rms_fwd(x0, [a_norm], name="norm_a")
    proj_a = mm(h_a, wf["a_w_in"], gather=("a_w_out",), name="a_in")
    m_a, hst = lru_fwd(proj_a, wf["a_conv_w"], a_conv_b, wf["a_w_r"], a_b_r, wf["a_w_i"], a_b_i, a_lambda,
                       name="lru_fwd")
    x1 = mm(m_a, wf["a_w_out"], residual=x0, gather=("w_kv",), name="a_out")
    kvn, hb = rms_fwd(x1, [kv_norm, wf["b_norm"]], name="norm_kv_b")
    kv = mm(kvn, wf["w_kv"], out_dtype=BF16, gather=("b_w_in",), name="kv_proj")
    proj_b = mm(hb, wf["b_w_in"], gather=("b_w_out",), name="b_in")
    m_b, o, saved = attn_fwd(proj_b, kv, name="attn_fwd")
    x2 = mm(m_b, wf["b_w_out"], residual=x1, name="b_out")
    loss_part, g_final, dx2, dx2b = loss_bwd(x2, target, final_norm, name="loss_bwd")

    dm_b = mm(dx2b, wf["b_w_out"], tb=True, name="b_out_dx")
    g_b_w_out = mm(m_b, dx2b, ta=True, out_dtype=BF16, name="b_out_dw")
    dq, dgate_b, dk, dv = attn_bwd(proj_b, dm_b, o, kv, saved, name="attn_bwd")
    dproj_b = (dq, dgate_b)
    dkv = (dk, dv)
    g_b_w_in = mm(hb, dproj_b, ta=True, out_dtype=BF16, scatter=dict(b_w_out=g_b_w_out), name="b_in_dw")
    g_w_kv = mm(kvn, dkv, ta=True, out_dtype=BF16, scatter=dict(b_w_in=g_b_w_in), name="kv_dw")
    dhb = mm(dproj_b, wf["b_w_in"], tb=True, scatter=dict(w_kv=g_w_kv), name="b_in_dx")
    dkvn = mm(dkv, wf["w_kv"], tb=True, name="kv_dx")
    dx1, dx1b, (g_kv_norm, g_b_norm) = rms_bwd(
        x1, dx2, [(kv_norm, dkvn), (wf["b_norm"], dhb)], name="norm_kv_b_bwd")

    g_a_w_out = mm(m_a, dx1b, ta=True, out_dtype=BF16, name="a_out_dw")
    dm_a = mm(dx1b, wf["a_w_out"], tb=True, scatter=dict(a_w_out=g_a_w_out), name="a_out_dx")
    dxpre, dgate_a, g_conv_w, g_conv_b, g_w_r, g_b_r, g_w_i, g_b_i, g_lambda = lru_bwd(
        proj_a, hst, dm_a, wf["a_conv_w"], a_conv_b, wf["a_w_r"], a_b_r, wf["a_w_i"], a_b_i, a_lambda,
        name="lru_bwd")
    dproj_a = (dxpre, dgate_a)
    g_w_r, g_w_i = g_w_r.astype(BF16), g_w_i.astype(BF16)
    g_a_w_in = mm(h_a, dproj_a, ta=True, out_dtype=BF16, scatter=dict(a_w_r=g_w_r, a_w_i=g_w_i), name="a_in_dw")
    dh_a = mm(dproj_a, wf["a_w_in"], tb=True, scatter=dict(a_w_in=g_a_w_in), name="a_in_dx")
    grad_x, _, (g_a_norm,) = rms_bwd(x0, dx1, [(a_norm, dh_a)], name="norm_a_bwd")

    gbig = parts if shards is not None else dict(
        a_w_in=g_a_w_in, a_w_r=g_w_r, a_w_i=g_w_i, a_w_out=g_a_w_out, w_kv=g_w_kv, b_w_in=g_b_w_in, b_w_out=g_b_w_out)
    gsmall = dict(a_norm=g_a_norm, a_conv_b=g_conv_b, a_b_r=g_b_r, a_b_i=g_b_i, a_lambda=g_lambda,
                  kv_norm=g_kv_norm, final_norm=g_final, a_conv_w=g_conv_w, b_norm=g_b_norm)
    return loss_part, grad_x, gbig, gsmall
```
